```python
import math
import jax, jax.numpy as jnp
from jax import lax
import numpy as np

D_MODEL = 1024
BATCH = 8
SEQ = 2048
DEPTH = 1

D_MIX = D_MODEL
D_SSM = D_MIX // 2
D_ATT = D_MIX - D_SSM
SSM_HEAD_DIM = 64
SSM_HEADS = D_SSM // SSM_HEAD_DIM
SSM_GROUPS = 2
SSM_STATE = 128
SSM_CONV = 4
SSM_CHUNK = 128
D_XBC = D_SSM + 2 * SSM_GROUPS * SSM_STATE
ATT_HEAD_DIM = 64
ATT_HEADS = D_ATT // ATT_HEAD_DIM
Q_BLOCK = 128
D_IN_SSM = D_SSM + D_XBC + SSM_HEADS
D_IN_ATT = 3 * D_ATT + ATT_HEADS
D_IN = D_IN_SSM + D_IN_ATT
N_EXPERT_GROUPS = 4
EXPERTS_PER_GROUP = 8
N_EXPERTS = N_EXPERT_GROUPS * EXPERTS_PER_GROUP
TOP_K = 2
D_EXPERT = D_MODEL // 2
MOE_BLOCK = 128
ALPHA = (2.0 * DEPTH) ** 0.25
BETA = (8.0 * DEPTH) ** -0.25
EPS = 1e-5

kernel_name = 'hymba_ssd_fox_hmoe_deepnorm'


def layer_norm(x, g, b):
    xf = x.astype(jnp.float32)
    mu = jnp.mean(xf, axis=-1, keepdims=True)
    var = jnp.mean(jnp.square(xf - mu), axis=-1, keepdims=True)
    return ((xf - mu) * lax.rsqrt(var + EPS) * g + b).astype(x.dtype)


def rms_norm(xf, g):
    return xf * lax.rsqrt(jnp.mean(xf * xf, axis=-1, keepdims=True) + EPS) * g


def causal_dwconv(x, w, b):
    K, C = w.shape
    y = lax.conv_general_dilated(x, w[:, None, :].astype(x.dtype), window_strides=(1,),
                                 padding=[(K - 1, 0)],
                                 dimension_numbers=('NWC', 'WIO', 'NWC'),
                                 feature_group_count=C)
    return y + b


def segsum_exp(a):
    L = a.shape[-1]
    cs = jnp.cumsum(a, axis=-1)
    diff = cs[..., :, None] - cs[..., None, :]
    mask = jnp.tril(jnp.ones((L, L), dtype=bool))
    return jnp.exp(jnp.where(mask, diff, -jnp.inf))


def ssd_mixer(zxbcdt, conv_w, conv_b, dt_bias, a_log, d_skip, norm_g):
    f32 = jnp.float32
    Bsz, T, _ = zxbcdt.shape
    nc = T // SSM_CHUNK
    R = SSM_HEADS // SSM_GROUPS
    z, xbc, dt = jnp.split(zxbcdt, [D_SSM, D_SSM + D_XBC], axis=-1)
    xbc = jax.nn.silu(causal_dwconv(xbc, conv_w, conv_b))
    xs, Bm, Cm = jnp.split(xbc, [D_SSM, D_SSM + SSM_GROUPS * SSM_STATE], axis=-1)
    dt = jax.nn.softplus(dt.astype(f32) + dt_bias)
    A = -jnp.exp(a_log.astype(f32)).reshape(SSM_GROUPS, R)
    x = xs.astype(f32).reshape(Bsz, nc, SSM_CHUNK, SSM_GROUPS, R, SSM_HEAD_DIM)
    Bc = Bm.astype(f32).reshape(Bsz, nc, SSM_CHUNK, SSM_GROUPS, SSM_STATE)
    Cc = Cm.astype(f32).reshape(Bsz, nc, SSM_CHUNK, SSM_GROUPS, SSM_STATE)
    dtc = dt.reshape(Bsz, nc, SSM_CHUNK, SSM_GROUPS, R)
    xdt = x * dtc[..., None]
    a = jnp.moveaxis(dtc * A, 2, -1)
    a_cs = jnp.cumsum(a, axis=-1)
    Lmat = segsum_exp(a)
    cb = jnp.einsum('bclgn,bcsgn->bcgls', Cc, Bc)
    y_diag = jnp.einsum('bcgls,bcgrls,bcsgrp->bclgrp', cb, Lmat, xdt)
    decay_states = jnp.exp(a_cs[..., -1:] - a_cs)
    states = jnp.einsum('bclgn,bcgrl,bclgrp->bcgrpn', Bc, decay_states, xdt)
    chunk_a = jnp.pad(jnp.moveaxis(a_cs[..., -1], 1, -1), [(0, 0), (0, 0), (0, 0), (1, 0)])
    decay_chunk = segsum_exp(chunk_a)
    states = jnp.pad(states, [(0, 0), (1, 0), (0, 0), (0, 0), (0, 0), (0, 0)])
    new_states = jnp.einsum('bgrzc,bcgrpn->bzgrpn', decay_chunk, states)
    prev_states = new_states[:, :-1]
    y_off = jnp.einsum('bclgn,bcgrpn,bcgrl->bclgrp', Cc, prev_states, jnp.exp(a_cs))
    y = y_diag + y_off + x * d_skip.reshape(SSM_GROUPS, R)[:, :, None]
    y = y.reshape(Bsz, T, D_SSM) * jax.nn.silu(z.astype(f32))
    y = y.reshape(Bsz, T, SSM_GROUPS, D_SSM // SSM_GROUPS)
    y = y * lax.rsqrt(jnp.mean(y * y, axis=-1, keepdims=True) + EPS)
    return y.reshape(Bsz, T, D_SSM) * norm_g


def forgetting_attention(qkvf, fg_bias, norm_g):
    f32 = jnp.float32
    Bsz, T, _ = qkvf.shape
    nb = T // Q_BLOCK
    q, k, v, f = jnp.split(qkvf, [D_ATT, 2 * D_ATT, 3 * D_ATT], axis=-1)
    q = q.reshape(Bsz, T, ATT_HEADS, ATT_HEAD_DIM)
    k = k.reshape(Bsz, T, ATT_HEADS, ATT_HEAD_DIM)
    v = v.reshape(Bsz, T, ATT_HEADS, ATT_HEAD_DIM)
    log_f = jax.nn.log_sigmoid(f.astype(f32) + fg_bias)
    cum = jnp.transpose(jnp.cumsum(log_f, axis=1), (0, 2, 1))
    scale = ATT_HEAD_DIM ** -0.5
    q_blocks = jnp.moveaxis(q.reshape(Bsz, nb, Q_BLOCK, ATT_HEADS, ATT_HEAD_DIM), 1, 0)
    cq_blocks = jnp.moveaxis(cum.reshape(Bsz, ATT_HEADS, nb, Q_BLOCK), 2, 0)
    q_pos = jnp.arange(T, dtype=jnp.int32).reshape(nb, Q_BLOCK)
    k_pos = jnp.arange(T, dtype=jnp.int32)

    def one_block(args):
        qb, cqb, pb = args
        s = jnp.einsum('bqhd,bkhd->bhqk', qb, k).astype(f32) * scale
        s = s + (cqb[..., :, None] - cum[..., None, :])
        s = jnp.where(pb[:, None] >= k_pos[None, :], s, -jnp.inf)
        p = jax.nn.softmax(s, axis=-1)
        return jnp.einsum('bhqk,bkhd->bqhd', p, v.astype(f32))

    out = lax.map(one_block, (q_blocks, cq_blocks, q_pos))
    out = jnp.moveaxis(out, 0, 1).reshape(Bsz, T, D_ATT)
    return rms_norm(out, norm_g)


def hybrid_mixer(u, w_in, conv_w, conv_b, dt_bias, a_log, d_skip, ssm_norm_g,
                 fg_bias, att_norm_g, w_out):
    proj = u @ w_in
    ssm_in, att_in = jnp.split(proj, [D_IN_SSM], axis=-1)
    y_ssm = ssd_mixer(ssm_in, conv_w, conv_b, dt_bias, a_log, d_skip, ssm_norm_g)
    y_att = forgetting_attention(att_in, fg_bias, att_norm_g)
    y = jnp.concatenate([y_ssm, y_att], axis=-1).astype(u.dtype)
    return y @ w_out


def hier_moe(u, router_g_w, router_g_b, router_e_w, router_e_b, w_gate, w_up, w_down):
    f32 = jnp.float32
    Bsz, T, D = u.shape
    N = Bsz * T
    xf = u.reshape(N, D)
    g_prob = jax.nn.softmax((xf @ router_g_w).astype(f32) + router_g_b, axis=-1)
    g_idx = jnp.argmax(g_prob, axis=-1)
    g_p = jnp.take_along_axis(g_prob, g_idx[:, None], axis=-1)[:, 0]
    e_logits = ((xf @ router_e_w).astype(f32) + router_e_b).reshape(N, N_EXPERT_GROUPS, EXPERTS_PER_GROUP)
    e_sel = jnp.take_along_axis(e_logits, g_idx[:, None, None], axis=1)[:, 0]
    top_p, top_local = lax.top_k(jax.nn.softmax(e_sel, axis=-1), TOP_K)
    top_p = top_p / jnp.sum(top_p, axis=-1, keepdims=True)
    weights = g_p[:, None] * top_p
    experts = g_idx[:, None] * EXPERTS_PER_GROUP + top_local
    A = N * TOP_K
    flat_e = experts.reshape(A).astype(jnp.int32)
    flat_w = weights.reshape(A)
    flat_tok = jnp.arange(A, dtype=jnp.int32) // TOP_K
    order = jnp.argsort(flat_e, stable=True)
    sorted_e = flat_e[order]
    counts = jnp.bincount(flat_e, length=N_EXPERTS)
    padded = (counts + MOE_BLOCK - 1) // MOE_BLOCK * MOE_BLOCK
    start = jnp.cumsum(counts) - counts
    pad_end = jnp.cumsum(padded)
    pad_start = pad_end - padded
    dest = pad_start[sorted_e] + (jnp.arange(A, dtype=jnp.int32) - start[sorted_e])
    P = A + N_EXPERTS * MOE_BLOCK
    n_blocks = P // MOE_BLOCK
    row_tok = jnp.full((P,), N, dtype=jnp.int32).at[dest].set(flat_tok[order])
    row_w = jnp.zeros((P,), f32).at[dest].set(flat_w[order])
    block_e = jnp.minimum(jnp.searchsorted(pad_end, jnp.arange(n_blocks) * MOE_BLOCK, side='right'),
                          N_EXPERTS - 1)
    x_pad = jnp.concatenate([xf, jnp.zeros((1, D), xf.dtype)], axis=0)
    x_rows = x_pad[row_tok].reshape(n_blocks, MOE_BLOCK, D)

    def expert_block(args):
        xb, e = args
        h = jax.nn.silu(xb @ w_gate[e]) * (xb @ w_up[e])
        return h @ w_down[e]

    y_rows = lax.map(expert_block, (x_rows, block_e)).reshape(P, D)
    out = jax.ops.segment_sum(y_rows.astype(f32) * row_w[:, None], row_tok, num_segments=N + 1)[:N]
    return out.reshape(Bsz, T, D).astype(u.dtype)


def setup_inputs(seed: int = 0) -> dict:
    key = jax.random.key(seed)
    ks = jax.random.split(key, 26)
    L, D = DEPTH, D_MODEL
    f32 = jnp.float32

    def nrm(k, shape, s):
        return s * jax.random.normal(k, shape, f32)

    dt0 = jnp.exp(jax.random.uniform(ks[7], (L, SSM_HEADS), f32, minval=math.log(1e-3), maxval=math.log(1e-1)))
    return {
        'x': nrm(ks[0], (BATCH, SEQ, D), 1.0),
        'c': nrm(ks[1], (BATCH, D), 1.0),
        'ada_w': nrm(ks[2], (L, D, 6 * D), 0.1 * D ** -0.5),
        'ada_b': nrm(ks[3], (L, 6 * D), 0.02),
        'w_in': nrm(ks[4], (L, D, D_IN), D ** -0.5),
        'conv_w': nrm(ks[5], (L, SSM_CONV, D_XBC), SSM_CONV ** -0.5),
        'conv_b': nrm(ks[6], (L, D_XBC), 0.02),
        'dt_bias': dt0 + jnp.log(-jnp.expm1(-dt0)),
        'a_log': jnp.log(jax.random.uniform(ks[8], (L, SSM_HEADS), f32, minval=1.0, maxval=16.0)),
        'd_skip': 1.0 + nrm(ks[9], (L, SSM_HEADS), 0.1),
        'ssm_norm_g': 1.0 + nrm(ks[10], (L, D_SSM), 0.02),
        'fg_bias': jax.random.uniform(ks[11], (L, ATT_HEADS), f32, minval=1.0, maxval=5.0),
        'att_norm_g': 1.0 + nrm(ks[12], (L, D_ATT), 0.02),
        'w_out': nrm(ks[13], (L, D_MIX, D), BETA * D_MIX ** -0.5),
        'ln1_g': 1.0 + nrm(ks[14], (L, D), 0.02),
        'ln1_b': nrm(ks[15], (L, D), 0.02),
        'router_g_w': nrm(ks[16], (L, D, N_EXPERT_GROUPS), D ** -0.5),
        'router_g_b': nrm(ks[17], (L, N_EXPERT_GROUPS), 0.01),
        'router_e_w': nrm(ks[18], (L, D, N_EXPERTS), D ** -0.5),
        'router_e_b': nrm(ks[19], (L, N_EXPERTS), 0.01),
        'w_gate': nrm(ks[20], (L, N_EXPERTS, D, D_EXPERT), D ** -0.5),
        'w_up': nrm(ks[21], (L, N_EXPERTS, D, D_EXPERT), D ** -0.5),
        'w_down': nrm(ks[22], (L, N_EXPERTS, D_EXPERT, D), BETA * D_EXPERT ** -0.5),
        'ln2_g': 1.0 + nrm(ks[23], (L, D), 0.02),
        'ln2_b': nrm(ks[24], (L, D), 0.02),
    }


def reference(x, c, ada_w, ada_b, w_in, conv_w, conv_b, dt_bias, a_log, d_skip, ssm_norm_g,
              fg_bias, att_norm_g, w_out, ln1_g, ln1_b, router_g_w, router_g_b, router_e_w,
              router_e_b, w_gate, w_up, w_down, ln2_g, ln2_b):
    for l in range(DEPTH):
        mod = jax.nn.silu(c) @ ada_w[l] + ada_b[l]
        sh1, sc1, gt1, sh2, sc2, gt2 = [m[:, None, :] for m in jnp.split(mod, 6, axis=-1)]
        u = x * (1.0 + sc1) + sh1
        h = hybrid_mixer(u, w_in[l], conv_w[l], conv_b[l], dt_bias[l], a_log[l], d_skip[l],
                         ssm_norm_g[l], fg_bias[l], att_norm_g[l], w_out[l])
        x = layer_norm(ALPHA * x + (1.0 + gt1) * h, ln1_g[l], ln1_b[l])
        u = x * (1.0 + sc2) + sh2
        h = hier_moe(u, router_g_w[l], router_g_b[l], router_e_w[l], router_e_b[l],
                     w_gate[l], w_up[l], w_down[l])
        x = layer_norm(ALPHA * x + (1.0 + gt2) * h, ln2_g[l], ln2_b[l])
    return x
```

```python
import functools

import jax
import jax.numpy as jnp
from jax import lax
from jax.experimental import pallas as pl
from jax.experimental.pallas import tpu as pltpu

f32 = jnp.float32
bf16 = jnp.bfloat16
i32 = jnp.int32

D = 1024
D_SSM = 512
D_ATT = 512
HEAD_DIM = 64
GROUP_W = 256
N_STATE = 128
CONV_K = 4
N_GROUPS_R = 4
EXPERTS_PER_GROUP = 8
N_EXPERTS = 32
D_EXPERT = 512
ALPHA = 2.0 ** 0.25
EPS = 1e-5
NEG = -1e30

SSD_CHUNK = 256
ATT_BLOCK = 256
INPROJ_ROWS = 512
OUTPROJ_ROWS = 512
MOE_ROWS = 256
COMBINE_ROWS = 256
VMEM_LIMIT = 48 * 1024 * 1024


def _dot(a, b):
    return jnp.dot(a, b, preferred_element_type=f32)


def _dot_nt(a, b):
    return lax.dot_general(a, b, (((1,), (1,)), ((), ())), preferred_element_type=f32)


def _dot_tn(a, b):
    return lax.dot_general(a, b, (((0,), (0,)), ((), ())), preferred_element_type=f32)


def _split3(v):
    hi = v.astype(bf16)
    r1 = v - hi.astype(f32)
    mid = r1.astype(bf16)
    lo = (r1 - mid.astype(f32)).astype(bf16)
    return hi, mid, lo


def _dot_exact_lhs(m, v):
    hi, mid, lo = _split3(v)
    return (_dot(m, hi) + _dot(m, mid)) + _dot(m, lo)


def _dot_exact_rhs(v, m):
    hi, mid, lo = _split3(v)
    return (_dot(hi, m) + _dot(mid, m)) + _dot(lo, m)


def _softplus(x):
    return jnp.maximum(x, 0.0) + jnp.log1p(jnp.exp(-jnp.abs(x)))


def _silu(x):
    return x * jax.nn.sigmoid(x)


def _ada_kernel(c_ref, w_ref, b_ref, o_ref):
    s = _silu(c_ref[...]).astype(bf16)
    o_ref[0] = _dot(s, w_ref[...].astype(bf16)) + b_ref[0]


def _ada(c, w, b):
    bsz = c.shape[0]
    return pl.pallas_call(
        _ada_kernel,
        grid=(6,),
        in_specs=[pl.BlockSpec((bsz, D), lambda j: (0, 0)),
                  pl.BlockSpec((D, D), lambda j: (0, j)),
                  pl.BlockSpec((1, 1, D), lambda j: (j, 0, 0))],
        out_specs=pl.BlockSpec((1, bsz, D), lambda j: (j, 0, 0)),
        out_shape=jax.ShapeDtypeStruct((6, bsz, D), f32),
        compiler_params=pltpu.CompilerParams(dimension_semantics=("arbitrary",), vmem_limit_bytes=VMEM_LIMIT),
        name="ada",
    )(c, w, b.reshape(6, 1, D))


def _inproj_kernel(x_ref, sc_ref, sh_ref, wm_ref, wsh_ref, wsl_ref, zx_ref, qkv_ref, dtf_ref):
    u = x_ref[0] * (1.0 + sc_ref[0, 0]) + sh_ref[0, 0]
    ub = u.astype(bf16)
    for j in range(3):
        zx_ref[0, :, j * 512:(j + 1) * 512] = _dot(ub, wm_ref[:, j * 512:(j + 1) * 512])
    for j in range(3):
        qkv_ref[0, :, j * 512:(j + 1) * 512] = _dot(ub, wm_ref[:, 1536 + j * 512:1536 + (j + 1) * 512]).astype(bf16)
    ul = (u - ub.astype(f32)).astype(bf16)
    dtf_ref[0] = (_dot(ub, wsh_ref[...]) + _dot(ul, wsh_ref[...])) + _dot(ub, wsl_ref[...])


def _inproj(x, mod4, w_main, ws_hi, ws_lo):
    bsz, t, _ = x.shape
    tm = min(INPROJ_ROWS, t)
    vec = lambda k: pl.BlockSpec((1, 1, 1, D), lambda b, i, k=k: (k, b, 0, 0))
    return pl.pallas_call(
        _inproj_kernel,
        grid=(bsz, t // tm),
        in_specs=[pl.BlockSpec((1, tm, D), lambda b, i: (b, i, 0)),
                  vec(1), vec(0),
                  pl.BlockSpec((D, 3072), lambda b, i: (0, 0)),
                  pl.BlockSpec((D, 128), lambda b, i: (0, 0)),
                  pl.BlockSpec((D, 128), lambda b, i: (0, 0))],
        out_specs=[pl.BlockSpec((1, tm, 1536), lambda b, i: (b, i, 0)),
                   pl.BlockSpec((1, tm, 1536), lambda b, i: (b, i, 0)),
                   pl.BlockSpec((1, tm, 128), lambda b, i: (b, i, 0))],
        out_shape=[jax.ShapeDtypeStruct((bsz, t, 1536), f32),
                   jax.ShapeDtypeStruct((bsz, t, 1536), bf16),
                   jax.ShapeDtypeStruct((bsz, t, 128), f32)],
        compiler_params=pltpu.CompilerParams(dimension_semantics=("parallel", "arbitrary"),
                                             vmem_limit_bytes=VMEM_LIMIT),
        name="inproj",
    )(x, mod4, mod4, w_main, ws_hi, ws_lo)


def _ssd_kernel(z_ref, xs_ref, bc_ref, dtf_ref, cw_ref, cb_ref, pc_ref, pe_ref,
                y_ref, cumc_ref, cumr_ref, xcat, state, carry, *, lc):
    j = pl.program_id(1)

    @pl.when(j == 0)
    def _init():
        xcat[0:8, :] = jnp.zeros((8, 2 * D_SSM), f32)
        state[...] = jnp.zeros_like(state)
        carry[...] = jnp.zeros_like(carry)

    xcat[8:8 + lc, 0:512] = xs_ref[0]
    xcat[8:8 + lc, 512:1024] = bc_ref[0]
    acc = cw_ref[0:1, :] * xcat[5:5 + lc, :] + cb_ref[...]
    for k in range(1, CONV_K):
        acc = acc + cw_ref[k:k + 1, :] * xcat[5 + k:5 + k + lc, :]
    xcat[0:8, :] = xcat[lc:lc + 8, :]
    xbc = _silu(acc)
    xs = xbc[:, 0:512]
    bm = xbc[:, 512:768]
    cm = xbc[:, 768:1024]

    dtf = dtf_ref[0]
    lane = lax.broadcasted_iota(i32, (lc, 128), 1)
    dt_c = _softplus(dtf + pc_ref[0:1, :])
    a_c = dt_c * (-jnp.exp(pc_ref[1:2, :]))
    logf = -_softplus(-(dtf + pc_ref[2:3, :]))
    v = jnp.where(lane < 8, a_c, logf)
    r_i = lax.broadcasted_iota(i32, (lc, lc), 0)
    c_i = lax.broadcasted_iota(i32, (lc, lc), 1)
    tri = r_i >= c_i
    tri_b = jnp.where(tri, 1.0, 0.0).astype(bf16)
    cum = _dot_exact_lhs(tri_b, v) + carry[...]
    carry[...] = jnp.where(lane[0:1, :] >= 8, cum[lc - 1:lc, :], 0.0)
    cumc_ref[0] = cum
    cum_t = cum.T
    cumr_ref[0] = cum_t[8:16, :]
    cs_t = cum_t[0:8, :]

    e_r = lax.broadcasted_iota(i32, (128, D_SSM), 0)
    e_c = lax.broadcasted_iota(i32, (128, D_SSM), 1)
    expand = jnp.where(jnp.right_shift(e_c, 6) == e_r, 1.0, 0.0).astype(bf16)
    dt_e = _dot_exact_rhs(dt_c, expand)
    cs_e = _dot_exact_rhs(cum, expand)

    xdt = xs * dt_e
    ecs = jnp.exp(cs_e)
    cs_last = cs_e[lc - 1:lc, :]
    dec_st = jnp.exp(cs_last - cs_e)
    lane_g = lax.broadcasted_iota(i32, (1, GROUP_W), 1)
    ys = []
    for g in range(2):
        gs = slice(g * GROUP_W, (g + 1) * GROUP_W)
        bg = bm[:, g * N_STATE:(g + 1) * N_STATE].astype(bf16)
        cg = cm[:, g * N_STATE:(g + 1) * N_STATE].astype(bf16)
        cb = _dot_nt(cg, bg)
        xdt_g = xdt[:, gs]
        xdt_gb = xdt_g.astype(bf16)
        ms, xb = [], []
        for hh in range(4):
            h = g * 4 + hh
            lm = jnp.exp(jnp.where(tri, cum[:, h:h + 1] - cs_t[h:h + 1, :], -jnp.inf))
            ms.append((cb * lm).astype(bf16))
            xb.append(jnp.where(jnp.right_shift(lane_g, 6) == hh, xdt_gb, jnp.zeros_like(xdt_gb)))
        y_diag = _dot(jnp.concatenate(ms, axis=1), jnp.concatenate(xb, axis=0))
        st = state[g]
        y_off = _dot(cg, st.astype(bf16)) * ecs[:, gs]
        upd = _dot_tn(bg, (xdt_g * dec_st[:, gs]).astype(bf16))
        state[g] = st * jnp.exp(cs_last[:, gs]) + upd
        ys.append(y_diag + y_off + xs[:, gs] * pe_ref[2:3, gs])

    outs = []
    for g in range(2):
        gs = slice(g * GROUP_W, (g + 1) * GROUP_W)
        yg = ys[g] * _silu(z_ref[0, :, gs])
        ms_ = jnp.mean(yg * yg, axis=-1, keepdims=True)
        outs.append(yg * lax.rsqrt(ms_ + EPS))
    y_ref[0] = (jnp.concatenate(outs, axis=1) * pe_ref[3:4, :]).astype(bf16)


def _ssd(zx, dtf, conv_w, conv_b, pc, pe):
    bsz, t, _ = zx.shape
    lc = min(SSD_CHUNK, t)
    col = lambda k: pl.BlockSpec((1, lc, 512), lambda b, j, k=k: (b, j, k))
    full = lambda shape: pl.BlockSpec(shape, lambda b, j: (0,) * len(shape))
    return pl.pallas_call(
        functools.partial(_ssd_kernel, lc=lc),
        grid=(bsz, t // lc),
        in_specs=[col(0), col(1), col(2),
                  pl.BlockSpec((1, lc, 128), lambda b, j: (b, j, 0)),
                  full((CONV_K, 2 * D_SSM)), full((1, 2 * D_SSM)), full((8, 128)), full((8, D_SSM))],
        out_specs=[pl.BlockSpec((1, lc, D_SSM), lambda b, j: (b, j, 0)),
                   pl.BlockSpec((1, lc, 128), lambda b, j: (b, j, 0)),
                   pl.BlockSpec((1, 8, lc), lambda b, j: (b, 0, j))],
        out_shape=[jax.ShapeDtypeStruct((bsz, t, D_SSM), bf16),
                   jax.ShapeDtypeStruct((bsz, t, 128), f32),
                   jax.ShapeDtypeStruct((bsz, 8, t), f32)],
        scratch_shapes=[pltpu.VMEM((lc + 8, 2 * D_SSM), f32),
                        pltpu.VMEM((2, N_STATE, GROUP_W), f32),
                        pltpu.VMEM((1, 128), f32)],
        compiler_params=pltpu.CompilerParams(dimension_semantics=("parallel", "arbitrary"),
                                             vmem_limit_bytes=VMEM_LIMIT),
        name="ssd",
    )(zx, zx, zx, dtf, conv_w, conv_b, pc, pe)


def _attn_kernel(q_ref, k_ref, v_ref, cc_ref, cr_ref, ng_ref, o_ref, vbd, *, tq, t):
    i = pl.program_id(1)
    nkb = t // tq
    lane_g = lax.broadcasted_iota(i32, (1, GROUP_W), 1)
    hmask = [jnp.right_shift(lane_g, 6) == hh for hh in range(4)]

    @pl.when(i == 0)
    def _build():
        for g in range(2):
            for jb in range(nkb):
                vb = v_ref[0, jb * tq:(jb + 1) * tq, g * GROUP_W:(g + 1) * GROUP_W]
                for hh in range(4):
                    vbd[g, jb, hh * tq:(hh + 1) * tq, :] = jnp.where(hmask[hh], vb, jnp.zeros_like(vb))

    cc = cc_ref[0]
    r_i = lax.broadcasted_iota(i32, (tq, tq), 0)
    c_i = lax.broadcasted_iota(i32, (tq, tq), 1)
    causal = r_i >= c_i

    def expand(vals):
        return jnp.where(hmask[0], vals[0], jnp.where(hmask[1], vals[1], jnp.where(hmask[2], vals[2], vals[3])))

    outs = []
    for g in range(2):
        gs = slice(g * GROUP_W, (g + 1) * GROUP_W)
        qg = q_ref[0, :, gs] * jnp.asarray(HEAD_DIM ** -0.5, bf16)
        qm = [jnp.where(hmask[hh], qg, jnp.zeros_like(qg)) for hh in range(4)]
        cq = [cc[:, 8 + 4 * g + hh:9 + 4 * g + hh] for hh in range(4)]

        def block(jb, carry, masked, g=g, gs=gs, qm=qm, cq=cq):
            ms, ls, acc = carry
            k0 = pl.multiple_of(jb * tq, tq)
            kb = k_ref[0, pl.ds(k0, tq), gs]
            ps, new_ms, new_ls, alphas = [], [], [], []
            for hh in range(4):
                s = _dot_nt(qm[hh], kb)
                ck = cr_ref[0, 4 * g + hh:4 * g + hh + 1, pl.ds(k0, tq)]
                s = s + (cq[hh] - ck)
                if masked:
                    s = jnp.where(causal, s, NEG)
                m_new = jnp.maximum(ms[hh], jnp.max(s, axis=-1, keepdims=True))
                alpha = jnp.exp(ms[hh] - m_new)
                p = jnp.exp(s - m_new)
                new_ls.append(alpha * ls[hh] + jnp.sum(p, axis=-1, keepdims=True))
                new_ms.append(m_new)
                alphas.append(alpha)
                ps.append(p.astype(bf16))
            pv = _dot(jnp.concatenate(ps, axis=1), vbd[g, jb])
            return tuple(new_ms), tuple(new_ls), acc * expand(alphas) + pv

        init = (tuple(jnp.full((tq, 1), NEG, f32) for _ in range(4)),
                tuple(jnp.zeros((tq, 1), f32) for _ in range(4)),
                jnp.zeros((tq, GROUP_W), f32))
        carry = lax.fori_loop(0, i, lambda jb, c: block(jb, c, False), init)
        _, ls, acc = block(i, carry, True)
        outs.append(acc * expand([1.0 / l for l in ls]))
    out = jnp.concatenate(outs, axis=1)
    ms_ = jnp.mean(out * out, axis=-1, keepdims=True)
    o_ref[0] = (out * lax.rsqrt(ms_ + EPS) * ng_ref[...]).astype(bf16)


def _attn(qkv, cumc, cumr, norm_g):
    bsz, t, _ = qkv.shape
    tq = min(ATT_BLOCK, t)
    return pl.pallas_call(
        functools.partial(_attn_kernel, tq=tq, t=t),
        grid=(bsz, t // tq),
        in_specs=[pl.BlockSpec((1, tq, D_ATT), lambda b, i: (b, i, 0)),
                  pl.BlockSpec((1, t, D_ATT), lambda b, i: (b, 0, 1)),
                  pl.BlockSpec((1, t, D_ATT), lambda b, i: (b, 0, 2)),
                  pl.BlockSpec((1, tq, 128), lambda b, i: (b, i, 0)),
                  pl.BlockSpec((1, 8, t), lambda b, i: (b, 0, 0)),
                  pl.BlockSpec((1, D_ATT), lambda b, i: (0, 0))],
        out_specs=pl.BlockSpec((1, tq, D_ATT), lambda b, i: (b, i, 0)),
        out_shape=jax.ShapeDtypeStruct((bsz, t, D_ATT), bf16),
        scratch_shapes=[pltpu.VMEM((2, t // tq, 4 * tq, GROUP_W), bf16)],
        compiler_params=pltpu.CompilerParams(dimension_semantics=("parallel", "arbitrary"),
                                             vmem_limit_bytes=VMEM_LIMIT),
        name="attn",
    )(qkv, qkv, qkv, cumc, cumr, norm_g)


def _layer_norm(y, g, b):
    mu = jnp.mean(y, axis=-1, keepdims=True)
    yc = y - mu
    var = jnp.mean(yc * yc, axis=-1, keepdims=True)
    return yc * lax.rsqrt(var + EPS) * g + b


def _outproj_kernel(ys_ref, ya_ref, x_ref, gt_ref, sc_ref, sh_ref, wo_ref, lng_ref, lnb_ref,
                    wrh_ref, wrl_ref, rb_ref, x1_ref, u2_ref, rt_ref, cnt_ref, carry, *, tm):
    i = pl.program_id(0)

    @pl.when(i == 0)
    def _init():
        carry[...] = jnp.zeros_like(carry)

    h = _dot(ys_ref[...], wo_ref[0:D_SSM, :]) + _dot(ya_ref[...], wo_ref[D_SSM:D, :])
    x1 = _layer_norm(ALPHA * x_ref[...] + (1.0 + gt_ref[0, 0]) * h, lng_ref[...], lnb_ref[...])
    x1_ref[...] = x1
    u2 = x1 * (1.0 + sc_ref[0, 0]) + sh_ref[0, 0]
    u2_ref[...] = u2

    uh = u2.astype(bf16)
    ul = (u2 - uh.astype(f32)).astype(bf16)
    logits = (_dot(uh, wrh_ref[...]) + _dot(ul, wrh_ref[...])) + _dot(uh, wrl_ref[...]) + rb_ref[...]
    lane = lax.broadcasted_iota(i32, (tm, 128), 1).astype(f32)
    big = jnp.float32(1e9)

    def first_max(vals):
        m = jnp.max(vals, axis=-1, keepdims=True)
        return m, jnp.min(jnp.where(vals == m, lane, big), axis=-1, keepdims=True)

    gl = jnp.where(lane < N_GROUPS_R, logits, NEG)
    gmax, gidx = first_max(gl)
    g_p = 1.0 / jnp.sum(jnp.exp(gl - gmax), axis=-1, keepdims=True)
    lo = N_GROUPS_R + EXPERTS_PER_GROUP * gidx
    el = jnp.where((lane >= lo) & (lane < lo + EXPERTS_PER_GROUP), logits, NEG)
    m1, i1 = first_max(el)
    el2 = jnp.where(lane == i1, NEG, el)
    m2, i2 = first_max(el2)
    r = jnp.exp(m2 - m1)
    w1 = g_p / (1.0 + r)
    w2 = g_p * r / (1.0 + r)

    oh1 = lane == i1
    oh2 = lane == i2
    oh = jnp.where(oh1 | oh2, 1.0, 0.0)
    r_i = lax.broadcasted_iota(i32, (tm, tm), 0)
    c_i = lax.broadcasted_iota(i32, (tm, tm), 1)
    lower = jnp.where(r_i > c_i, 1.0, 0.0).astype(bf16)
    prefix = _dot(lower, oh.astype(bf16)) + carry[...]
    rank1 = jnp.sum(jnp.where(oh1, prefix, 0.0), axis=-1, keepdims=True)
    rank2 = jnp.sum(jnp.where(oh2, prefix, 0.0), axis=-1, keepdims=True)
    carry[...] = carry[...] + jnp.sum(oh, axis=0, keepdims=True)
    cnt_ref[...] = jnp.broadcast_to(carry[...], (8, 128))

    rt = jnp.where(lane == 0, i1 - N_GROUPS_R, 0.0)
    rt = jnp.where(lane == 1, i2 - N_GROUPS_R, rt)
    rt = jnp.where(lane == 2, rank1, rt)
    rt = jnp.where(lane == 3, rank2, rt)
    rt = jnp.where(lane == 4, w1, rt)
    rt = jnp.where(lane == 5, w2, rt)
    rt_ref[...] = rt


def _outproj(y_ssm, y_att, x, mod4, w_out, ln_g, ln_b, wr_hi, wr_lo, rb, t):
    n = x.shape[0]
    tm = min(OUTPROJ_ROWS, t)
    nt = t // tm
    vec = lambda k: pl.BlockSpec((1, 1, 1, D), lambda i, k=k: (k, i // nt, 0, 0))
    full = lambda shape: pl.BlockSpec(shape, lambda i: (0,) * len(shape))
    rows = lambda w: pl.BlockSpec((tm, w), lambda i: (i, 0))
    return pl.pallas_call(
        functools.partial(_outproj_kernel, tm=tm),
        grid=(n // tm,),
        in_specs=[rows(D_SSM), rows(D_ATT), rows(D), vec(2), vec(4), vec(3),
                  full((D, D)), full((1, D)), full((1, D)), full((D, 128)), full((D, 128)), full((1, 128))],
        out_specs=[rows(D), rows(D), rows(128), full((8, 128))],
        out_shape=[jax.ShapeDtypeStruct((n, D), f32), jax.ShapeDtypeStruct((n, D), f32),
                   jax.ShapeDtypeStruct((n, 128), f32), jax.ShapeDtypeStruct((8, 128), f32)],
        scratch_shapes=[pltpu.VMEM((1, 128), f32)],
        compiler_params=pltpu.CompilerParams(dimension_semantics=("arbitrary",), vmem_limit_bytes=VMEM_LIMIT),
        name="outproj",
    )(y_ssm, y_att, x, mod4, mod4, mod4, w_out, ln_g, ln_b, wr_hi, wr_lo, rb)


def _row_copy(src_hbm, row, buf, slot, sem):
    return pltpu.make_async_copy(src_hbm.at[pl.ds(row, 1)], buf.at[pl.ds(slot, 1)], sem)


def _moe_kernel(be_ref, nv_ref, rt_ref, u_hbm, wg_ref, wu_ref, wd_ref, y_ref, xbuf, wgb, wub, wdb, sem, *, blk):
    i = pl.program_id(0)
    nv = nv_ref[i]

    @pl.when(i == 0)
    def _init():
        xbuf[...] = jnp.zeros_like(xbuf)

    @pl.when(nv == 0)
    def _empty():
        y_ref[...] = jnp.zeros_like(y_ref)

    @pl.when(nv > 0)
    def _work():
        def issue(r, c):
            _row_copy(u_hbm, rt_ref[i * blk + r], xbuf, r, sem).start()
            return c
        lax.fori_loop(0, nv, issue, 0)

        changed = jnp.logical_or(i == 0, be_ref[i] != be_ref[jnp.maximum(i - 1, 0)])

        @pl.when(changed)
        def _cast():
            wgb[...] = wg_ref[0].astype(bf16)
            wub[...] = wu_ref[0].astype(bf16)
            wdb[...] = wd_ref[0].astype(bf16)

        def wait(r, c):
            _row_copy(u_hbm, 0, xbuf, r, sem).wait()
            return c
        lax.fori_loop(0, nv, wait, 0)

        row = lax.broadcasted_iota(i32, (blk, 1), 0)
        xb = jnp.where(row < nv, xbuf[...], 0.0).astype(bf16)
        hid = (_silu(_dot(xb, wgb[...])) * _dot(xb, wub[...])).astype(bf16)
        y_ref[...] = _dot(hid, wdb[...])


def _moe(block_e, nvalid, row_tok, u2, w_gate, w_up, w_down, blk):
    nblk = block_e.shape[0]
    wspec = lambda a, b: pl.BlockSpec((1, a, b), lambda i, be, nv, rt: (be[i], 0, 0))
    grid_spec = pltpu.PrefetchScalarGridSpec(
        num_scalar_prefetch=3,
        grid=(nblk,),
        in_specs=[pl.BlockSpec(memory_space=pl.ANY), wspec(D, D_EXPERT), wspec(D, D_EXPERT), wspec(D_EXPERT, D)],
        out_specs=pl.BlockSpec((blk, D), lambda i, be, nv, rt: (i, 0)),
        scratch_shapes=[pltpu.VMEM((blk, D), f32),
                        pltpu.VMEM((D, D_EXPERT), bf16), pltpu.VMEM((D, D_EXPERT), bf16),
                        pltpu.VMEM((D_EXPERT, D), bf16),
                        pltpu.SemaphoreType.DMA(())],
    )
    return pl.pallas_call(
        functools.partial(_moe_kernel, blk=blk),
        grid_spec=grid_spec,
        out_shape=jax.ShapeDtypeStruct((nblk * blk, D), f32),
        compiler_params=pltpu.CompilerParams(dimension_semantics=("arbitrary",), vmem_limit_bytes=VMEM_LIMIT),
        name="moe",
    )(block_e, nvalid, row_tok, u2, w_gate, w_up, w_down)


def _combine_kernel(d1_ref, d2_ref, y_hbm, x1_ref, rt_ref, gt_ref, lng_ref, lnb_ref, o_ref, buf1, buf2, sem, *, tm):
    i = pl.program_id(0)

    def issue(r, c):
        _row_copy(y_hbm, d1_ref[i * tm + r], buf1, r, sem).start()
        _row_copy(y_hbm, d2_ref[i * tm + r], buf2, r, sem).start()
        return c
    lax.fori_loop(0, tm, issue, 0)

    def wait(r, c):
        _row_copy(y_hbm, 0, buf1, r, sem).wait()
        _row_copy(y_hbm, 0, buf2, r, sem).wait()
        return c
    lax.fori_loop(0, tm, wait, 0)

    rt = rt_ref[...]
    moe = rt[:, 4:5] * buf1[...] + rt[:, 5:6] * buf2[...]
    y = ALPHA * x1_ref[...] + (1.0 + gt_ref[0, 0]) * moe
    o_ref[...] = _layer_norm(y, lng_ref[...], lnb_ref[...])


def _combine(dest1, dest2, y_rows, x1, rt, mod4, ln_g, ln_b, t):
    n = x1.shape[0]
    tm = min(COMBINE_ROWS, t)
    nt = t // tm
    full = lambda shape: pl.BlockSpec(shape, lambda i, d1, d2: (0,) * len(shape))
    rows = lambda w: pl.BlockSpec((tm, w), lambda i, d1, d2: (i, 0))
    grid_spec = pltpu.PrefetchScalarGridSpec(
        num_scalar_prefetch=2,
        grid=(n // tm,),
        in_specs=[pl.BlockSpec(memory_space=pl.ANY), rows(D), rows(128),
                  pl.BlockSpec((1, 1, 1, D), lambda i, d1, d2: (5, i // nt, 0, 0)),
                  full((1, D)), full((1, D))],
        out_specs=rows(D),
        scratch_shapes=[pltpu.VMEM((tm, D), f32), pltpu.VMEM((tm, D), f32), pltpu.SemaphoreType.DMA(())],
    )
    return pl.pallas_call(
        functools.partial(_combine_kernel, tm=tm),
        grid_spec=grid_spec,
        out_shape=jax.ShapeDtypeStruct((n, D), f32),
        compiler_params=pltpu.CompilerParams(dimension_semantics=("arbitrary",), vmem_limit_bytes=VMEM_LIMIT),
        name="combine",
    )(dest1, dest2, y_rows, x1, rt, mod4, ln_g, ln_b)


def _hi_lo(w):
    hi = w.astype(bf16)
    return hi, (w - hi.astype(f32)).astype(bf16)


def _pad_lanes(v, offset, width):
    return jnp.zeros((width,), f32).at[offset:offset + v.shape[0]].set(v)


def _layer(x, c, ada_w, ada_b, w_in, conv_w, conv_b, dt_bias, a_log, d_skip, ssm_norm_g, fg_bias, att_norm_g,
           w_out, ln1_g, ln1_b, router_g_w, router_g_b, router_e_w, router_e_b, w_gate, w_up, w_down, ln2_g, ln2_b):
    bsz, t, _ = x.shape
    n = bsz * t

    mod4 = _ada(c, ada_w, ada_b).reshape(6, bsz, 1, D)

    w_main = jnp.concatenate([w_in[:, 0:1536], w_in[:, 1544:3080]], axis=1).astype(bf16)
    w_small = jnp.concatenate([w_in[:, 1536:1544], w_in[:, 3080:3088], jnp.zeros((D, 112), f32)], axis=1)
    ws_hi, ws_lo = _hi_lo(w_small)
    zx, qkv, dtf = _inproj(x, mod4, w_main, ws_hi, ws_lo)

    pc = jnp.stack([_pad_lanes(dt_bias, 0, 128), _pad_lanes(a_log, 0, 128), _pad_lanes(fg_bias, 8, 128)]
                   + [jnp.zeros((128,), f32)] * 5)
    rep = lambda v: jnp.repeat(v, HEAD_DIM)
    pe = jnp.stack([rep(dt_bias), rep(a_log), rep(d_skip), ssm_norm_g] + [jnp.zeros((D_SSM,), f32)] * 4)
    y_ssm, cumc, cumr = _ssd(zx, dtf, conv_w, conv_b.reshape(1, -1), pc, pe)

    y_att = _attn(qkv, cumc, cumr, att_norm_g.reshape(1, -1))

    wr = jnp.concatenate([router_g_w, router_e_w, jnp.zeros((D, 128 - N_GROUPS_R - N_EXPERTS), f32)], axis=1)
    wr_hi, wr_lo = _hi_lo(wr)
    rb = jnp.concatenate([router_g_b, router_e_b, jnp.zeros((128 - N_GROUPS_R - N_EXPERTS,), f32)]).reshape(1, 128)
    x1, u2, rt, cnt = _outproj(y_ssm.reshape(n, D_SSM), y_att.reshape(n, D_ATT), x.reshape(n, D), mod4,
                               w_out.astype(bf16), ln1_g.reshape(1, D), ln1_b.reshape(1, D), wr_hi, wr_lo, rb, t)

    blk = MOE_ROWS
    e1 = rt[:, 0].astype(i32)
    e2 = rt[:, 1].astype(i32)
    counts = cnt[0, N_GROUPS_R:N_GROUPS_R + N_EXPERTS].astype(i32)
    padded = (counts + blk - 1) // blk * blk
    pad_end = jnp.cumsum(padded)
    pad_start = pad_end - padded
    dest1 = pad_start[e1] + rt[:, 2].astype(i32)
    dest2 = pad_start[e2] + rt[:, 3].astype(i32)
    nblk = (2 * n) // blk + N_EXPERTS
    tok = jnp.arange(n, dtype=i32)
    row_tok = jnp.zeros((nblk * blk,), i32).at[dest1].set(tok).at[dest2].set(tok)
    blk_start = jnp.arange(nblk, dtype=i32) * blk
    block_e = jnp.minimum(jnp.searchsorted(pad_end, blk_start, side='right'), N_EXPERTS - 1).astype(i32)
    nvalid = jnp.clip(counts[block_e] - (blk_start - pad_start[block_e]), 0, blk).astype(i32)

    y_rows = _moe(block_e, nvalid, row_tok, u2, w_gate, w_up, w_down, blk)
    out = _combine(dest1, dest2, y_rows, x1, rt, mod4, ln2_g.reshape(1, D), ln2_b.reshape(1, D), t)
    return out.reshape(bsz, t, D)


def kernel(x, c, ada_w, ada_b, w_in, conv_w, conv_b, dt_bias, a_log, d_skip, ssm_norm_g, fg_bias, att_norm_g, w_out,
           ln1_g, ln1_b, router_g_w, router_g_b, router_e_w, router_e_b, w_gate, w_up, w_down, ln2_g, ln2_b):
    depth = ada_w.shape[0]
    for l in range(depth):
        x = _layer(x, c, ada_w[l], ada_b[l], w_in[l], conv_w[l], conv_b[l], dt_bias[l], a_log[l], d_skip[l],
                   ssm_norm_g[l], fg_bias[l], att_norm_g[l], w_out[l], ln1_g[l], ln1_b[l], router_g_w[l],
                   router_g_b[l], router_e_w[l], router_e_b[l], w_gate[l], w_up[l], w_down[l], ln2_g[l], ln2_b[l])
    return x
```

```python
import functools

import jax
import jax.numpy as jnp
import numpy as np
from jax import lax
from jax.experimental import pallas as pl
from jax.experimental.pallas import tpu as pltpu

f32 = jnp.float32
bf16 = jnp.bfloat16
i32 = jnp.int32

D = 1024
D_SSM = 512
D_ATT = 512
HEAD_DIM = 64
GROUP_W = 256
N_STATE = 128
CONV_K = 4
N_GROUPS_R = 4
EXPERTS_PER_GROUP = 8
N_EXPERTS = 32
D_EXPERT = 512
ALPHA = 2.0 ** 0.25
EPS = 1e-5
NEG = -1e30

SSD_CHUNK = 256
ATT_BLOCK = 256
INPROJ_ROWS = 512
OUTPROJ_ROWS = 512
MOE_ROWS = 256
COMBINE_ROWS = 256
VMEM_LIMIT = 48 * 1024 * 1024


def _dot(a, b):
    return jnp.dot(a, b, preferred_element_type=f32)


def _dot_nt(a, b):
    return lax.dot_general(a, b, (((1,), (1,)), ((), ())), preferred_element_type=f32)


def _dot_tn(a, b):
    return lax.dot_general(a, b, (((0,), (0,)), ((), ())), preferred_element_type=f32)


def _split3(v):
    hi = v.astype(bf16)
    r1 = v - hi.astype(f32)
    mid = r1.astype(bf16)
    lo = (r1 - mid.astype(f32)).astype(bf16)
    return hi, mid, lo


def _dot_exact_lhs(m, v):
    hi, mid, lo = _split3(v)
    return (_dot(m, hi) + _dot(m, mid)) + _dot(m, lo)


def _dot_exact_rhs(v, m):
    hi, mid, lo = _split3(v)
    return (_dot(hi, m) + _dot(mid, m)) + _dot(lo, m)


def _softplus(x):
    return jnp.maximum(x, 0.0) + jnp.log1p(jnp.exp(-jnp.abs(x)))


def _silu(x):
    return x * jax.nn.sigmoid(x)


def _ada_kernel(c_ref, w_ref, b_ref, o_ref):
    s = _silu(c_ref[...]).astype(bf16)
    o_ref[0] = _dot(s, w_ref[...].astype(bf16)) + b_ref[0]


def _ada(c, w, b):
    bsz = c.shape[0]
    return pl.pallas_call(
        _ada_kernel,
        grid=(6,),
        in_specs=[pl.BlockSpec((bsz, D), lambda j: (0, 0)),
                  pl.BlockSpec((D, D), lambda j: (0, j)),
                  pl.BlockSpec((1, 1, D), lambda j: (j, 0, 0))],
        out_specs=pl.BlockSpec((1, bsz, D), lambda j: (j, 0, 0)),
        out_shape=jax.ShapeDtypeStruct((6, bsz, D), f32),
        compiler_params=pltpu.CompilerParams(dimension_semantics=("arbitrary",), vmem_limit_bytes=VMEM_LIMIT),
        name="ada",
    )(c, w, b.reshape(6, 1, D))


def _inproj_kernel(x_ref, sc_ref, sh_ref, wm_ref, wsh_ref, wsl_ref, zx_ref, qkv_ref, dtf_ref):
    u = x_ref[0] * (1.0 + sc_ref[0, 0]) + sh_ref[0, 0]
    ub = u.astype(bf16)
    for j in range(3):
        zx_ref[0, :, j * 512:(j + 1) * 512] = _dot(ub, wm_ref[:, j * 512:(j + 1) * 512])
    for j in range(3):
        qkv_ref[0, :, j * 512:(j + 1) * 512] = _dot(ub, wm_ref[:, 1536 + j * 512:1536 + (j + 1) * 512]).astype(bf16)
    ul = (u - ub.astype(f32)).astype(bf16)
    dtf_ref[0] = (_dot(ub, wsh_ref[...]) + _dot(ul, wsh_ref[...])) + _dot(ub, wsl_ref[...])


def _inproj(x, mod4, w_main, ws_hi, ws_lo):
    bsz, t, _ = x.shape
    tm = min(INPROJ_ROWS, t)
    vec = lambda k: pl.BlockSpec((1, 1, 1, D), lambda b, i, k=k: (k, b, 0, 0))
    return pl.pallas_call(
        _inproj_kernel,
        grid=(bsz, t // tm),
        in_specs=[pl.BlockSpec((1, tm, D), lambda b, i: (b, i, 0)),
                  vec(1), vec(0),
                  pl.BlockSpec((D, 3072), lambda b, i: (0, 0)),
                  pl.BlockSpec((D, 128), lambda b, i: (0, 0)),
                  pl.BlockSpec((D, 128), lambda b, i: (0, 0))],
        out_specs=[pl.BlockSpec((1, tm, 1536), lambda b, i: (b, i, 0)),
                   pl.BlockSpec((1, tm, 1536), lambda b, i: (b, i, 0)),
                   pl.BlockSpec((1, tm, 128), lambda b, i: (b, i, 0))],
        out_shape=[jax.ShapeDtypeStruct((bsz, t, 1536), f32),
                   jax.ShapeDtypeStruct((bsz, t, 1536), bf16),
                   jax.ShapeDtypeStruct((bsz, t, 128), f32)],
        compiler_params=pltpu.CompilerParams(dimension_semantics=("parallel", "arbitrary"),
                                             vmem_limit_bytes=VMEM_LIMIT),
        name="inproj",
    )(x, mod4, mod4, w_main, ws_hi, ws_lo)


def _ssd_kernel(z_ref, xs_ref, bc_ref, dtf_ref, cw_ref, cb_ref, pc_ref, pe_ref,
                y_ref, cumc_ref, xcat, state, carry, *, lc):
    j = pl.program_id(1)

    @pl.when(j == 0)
    def _init():
        xcat[0:8, :] = jnp.zeros((8, 2 * D_SSM), f32)
        state[...] = jnp.zeros_like(state)
        carry[...] = jnp.zeros_like(carry)

    xcat[8:8 + lc, 0:512] = xs_ref[0]
    xcat[8:8 + lc, 512:1024] = bc_ref[0]
    acc = cw_ref[0:1, :] * xcat[5:5 + lc, :] + cb_ref[...]
    for k in range(1, CONV_K):
        acc = acc + cw_ref[k:k + 1, :] * xcat[5 + k:5 + k + lc, :]
    xcat[0:8, :] = xcat[lc:lc + 8, :]
    xbc = _silu(acc)
    xs = xbc[:, 0:512]
    bm = xbc[:, 512:768]
    cm = xbc[:, 768:1024]

    dtf = dtf_ref[0]
    lane = lax.broadcasted_iota(i32, (lc, 128), 1)
    dt_c = _softplus(dtf + pc_ref[0:1, :])
    a_c = dt_c * (-jnp.exp(pc_ref[1:2, :]))
    logf = -_softplus(-(dtf + pc_ref[2:3, :]))
    v = jnp.where(lane < 8, a_c, logf)
    r_i = lax.broadcasted_iota(i32, (lc, lc), 0)
    c_i = lax.broadcasted_iota(i32, (lc, lc), 1)
    tri = r_i >= c_i
    tri_b = jnp.where(tri, 1.0, 0.0).astype(bf16)
    cum = _dot_exact_lhs(tri_b, v) + carry[...]
    carry[...] = jnp.where(lane[0:1, :] >= 8, cum[lc - 1:lc, :], 0.0)
    cumc_ref[0] = cum
    cs_t = cum.T[0:8, :]

    e_r = lax.broadcasted_iota(i32, (128, D_SSM), 0)
    e_c = lax.broadcasted_iota(i32, (128, D_SSM), 1)
    expand = jnp.where(jnp.right_shift(e_c, 6) == e_r, 1.0, 0.0).astype(bf16)
    dt_e = _dot_exact_rhs(dt_c, expand)
    cs_e = _dot_exact_rhs(cum, expand)

    xdt = xs * dt_e
    ecs = jnp.exp(cs_e)
    cs_last = cs_e[lc - 1:lc, :]
    dec_st = jnp.exp(cs_last - cs_e)
    lane_g = lax.broadcasted_iota(i32, (1, GROUP_W), 1)
    ys = []
    for g in range(2):
        gs = slice(g * GROUP_W, (g + 1) * GROUP_W)
        bg = bm[:, g * N_STATE:(g + 1) * N_STATE].astype(bf16)
        cg = cm[:, g * N_STATE:(g + 1) * N_STATE].astype(bf16)
        cb = _dot_nt(cg, bg)
        xdt_g = xdt[:, gs]
        xdt_gb = xdt_g.astype(bf16)
        ms, xb = [], []
        for hh in range(4):
            h = g * 4 + hh
            lm = jnp.exp(jnp.where(tri, cum[:, h:h + 1] - cs_t[h:h + 1, :], -jnp.inf))
            ms.append((cb * lm).astype(bf16))
            xb.append(jnp.where(jnp.right_shift(lane_g, 6) == hh, xdt_gb, jnp.zeros_like(xdt_gb)))
        y_diag = _dot(jnp.concatenate(ms, axis=1), jnp.concatenate(xb, axis=0))
        st = state[g]
        y_off = _dot(cg, st.astype(bf16)) * ecs[:, gs]
        upd = _dot_tn(bg, (xdt_g * dec_st[:, gs]).astype(bf16))
        state[g] = st * jnp.exp(cs_last[:, gs]) + upd
        ys.append(y_diag + y_off + xs[:, gs] * pe_ref[2:3, gs])

    outs = []
    for g in range(2):
        gs = slice(g * GROUP_W, (g + 1) * GROUP_W)
        yg = ys[g] * _silu(z_ref[0, :, gs])
        ms_ = jnp.mean(yg * yg, axis=-1, keepdims=True)
        outs.append(yg * lax.rsqrt(ms_ + EPS))
    y_ref[0] = (jnp.concatenate(outs, axis=1) * pe_ref[3:4, :]).astype(bf16)


def _ssd(zx, dtf, conv_w, conv_b, pc, pe):
    bsz, t, _ = zx.shape
    lc = min(SSD_CHUNK, t)
    col = lambda k: pl.BlockSpec((1, lc, 512), lambda b, j, k=k: (b, j, k))
    full = lambda shape: pl.BlockSpec(shape, lambda b, j: (0,) * len(shape))
    return pl.pallas_call(
        functools.partial(_ssd_kernel, lc=lc),
        grid=(bsz, t // lc),
        in_specs=[col(0), col(1), col(2),
                  pl.BlockSpec((1, lc, 128), lambda b, j: (b, j, 0)),
                  full((CONV_K, 2 * D_SSM)), full((1, 2 * D_SSM)), full((8, 128)), full((8, D_SSM))],
        out_specs=[pl.BlockSpec((1, lc, D_SSM), lambda b, j: (b, j, 0)),
                   pl.BlockSpec((1, lc, 128), lambda b, j: (b, j, 0))],
        out_shape=[jax.ShapeDtypeStruct((bsz, t, D_SSM), bf16),
                   jax.ShapeDtypeStruct((bsz, t, 128), f32)],
        scratch_shapes=[pltpu.VMEM((lc + 8, 2 * D_SSM), f32),
                        pltpu.VMEM((2, N_STATE, GROUP_W), f32),
                        pltpu.VMEM((1, 128), f32)],
        compiler_params=pltpu.CompilerParams(dimension_semantics=("parallel", "arbitrary"),
                                             vmem_limit_bytes=VMEM_LIMIT),
        name="ssd",
    )(zx, zx, zx, dtf, conv_w, conv_b, pc, pe)


def _attn_kernel(q_ref, k_ref, v_ref, cc_ref, psel_ref, ng_ref, o_ref, kaug, vt, acc, *, tq, t):
    i = pl.program_id(1)
    nkb = t // tq
    n_heads = D_ATT // HEAD_DIM
    lane = lax.broadcasted_iota(i32, (1, 128), 1)
    lo_half = lane < HEAD_DIM

    @pl.when(i == 0)
    def _build():
        eye = jnp.where(lax.broadcasted_iota(i32, (D_ATT, D_ATT), 0) == lax.broadcasted_iota(i32, (D_ATT, D_ATT), 1),
                        1.0, 0.0).astype(bf16)
        for jb in range(nkb):
            rows = slice(jb * tq, (jb + 1) * tq)
            vt[jb] = _dot_nt(eye, v_ref[0, rows, :]).astype(bf16)
            pieces = jnp.concatenate(_split3(-cc_ref[0, rows, :]), axis=1)
            for p in range(n_heads // 2):
                a = _dot(pieces, psel_ref[p]).astype(bf16)
                kp = k_ref[0, rows, p * 128:(p + 1) * 128]
                kaug[2 * p, rows, :] = jnp.where(lo_half, kp, a)
                kaug[2 * p + 1, rows, :] = jnp.where(lo_half, a, kp)

    ones_hi = jnp.where((lane >= HEAD_DIM) & (lane < HEAD_DIM + 3), 1.0, 0.0).astype(bf16)
    ones_lo = jnp.where(lane < 3, 1.0, 0.0).astype(bf16)
    qa = []
    for p in range(n_heads // 2):
        qp = q_ref[0, :, p * 128:(p + 1) * 128] * jnp.asarray(HEAD_DIM ** -0.5, bf16)
        qa.append(jnp.where(lo_half, qp, ones_hi))
        qa.append(jnp.where(lo_half, ones_lo, qp))
    keep = lax.broadcasted_iota(i32, (tq, tq), 0) <= lax.broadcasted_iota(i32, (tq, tq), 1)
    acc[...] = jnp.zeros_like(acc)

    def block(jb, carry, masked):
        ms, ls = carry
        k0 = pl.multiple_of(jb * tq, tq)
        new_ms, new_ls = [], []
        scores = [_dot_nt(kaug[h, pl.ds(k0, tq), :], qa[h]) for h in range(n_heads)]
        for h in range(n_heads):
            s = scores[h]
            if masked:
                s = jnp.where(keep, s, NEG)
            m_new = jnp.maximum(ms[h], jnp.max(s, axis=0, keepdims=True))
            alpha = jnp.exp(ms[h] - m_new)
            p = jnp.exp(s - m_new)
            new_ls.append(alpha * ls[h] + jnp.sum(p, axis=0, keepdims=True))
            new_ms.append(m_new)
            pv = _dot(vt[jb, h * HEAD_DIM:(h + 1) * HEAD_DIM, :], p.astype(bf16))
            acc[h] = acc[h] * alpha + pv
        return tuple(new_ms), tuple(new_ls)

    init = (tuple(jnp.full((1, tq), NEG, f32) for _ in range(n_heads)),
            tuple(jnp.zeros((1, tq), f32) for _ in range(n_heads)))
    carry = lax.fori_loop(0, i, lambda jb, c: block(jb, c, False), init)
    _, ls = block(i, carry, True)
    out_t = jnp.concatenate([acc[h] * (1.0 / ls[h]) for h in range(n_heads)], axis=0)
    ms_ = jnp.mean(out_t * out_t, axis=0, keepdims=True)
    out_t = out_t * lax.rsqrt(ms_ + EPS)
    o_ref[0] = (out_t.T * ng_ref[...]).astype(bf16)


def _piece_select():
    sel = np.zeros((4, 384, 128), np.float32)
    for pair in range(4):
        for j in range(3):
            sel[pair, j * 128 + 8 + 2 * pair, HEAD_DIM + j] = 1.0
            sel[pair, j * 128 + 8 + 2 * pair + 1, j] = 1.0
    return jnp.asarray(sel, bf16)


def _attn(qkv, cumc, norm_g):
    bsz, t, _ = qkv.shape
    tq = min(ATT_BLOCK, t)
    return pl.pallas_call(
        functools.partial(_attn_kernel, tq=tq, t=t),
        grid=(bsz, t // tq),
        in_specs=[pl.BlockSpec((1, tq, D_ATT), lambda b, i: (b, i, 0)),
                  pl.BlockSpec((1, t, D_ATT), lambda b, i: (b, 0, 1)),
                  pl.BlockSpec((1, t, D_ATT), lambda b, i: (b, 0, 2)),
                  pl.BlockSpec((1, t, 128), lambda b, i: (b, 0, 0)),
                  pl.BlockSpec((4, 384, 128), lambda b, i: (0, 0, 0)),
                  pl.BlockSpec((1, D_ATT), lambda b, i: (0, 0))],
        out_specs=pl.BlockSpec((1, tq, D_ATT), lambda b, i: (b, i, 0)),
        out_shape=jax.ShapeDtypeStruct((bsz, t, D_ATT), bf16),
        scratch_shapes=[pltpu.VMEM((D_ATT // HEAD_DIM, t, 128), bf16),
                        pltpu.VMEM((t // tq, D_ATT, tq), bf16),
                        pltpu.VMEM((D_ATT // HEAD_DIM, HEAD_DIM, tq), f32)],
        compiler_params=pltpu.CompilerParams(dimension_semantics=("parallel", "arbitrary"),
                                             vmem_limit_bytes=VMEM_LIMIT),
        name="attn",
    )(qkv, qkv, qkv, cumc, _piece_select(), norm_g)


def _layer_norm(y, g, b):
    mu = jnp.mean(y, axis=-1, keepdims=True)
    yc = y - mu
    var = jnp.mean(yc * yc, axis=-1, keepdims=True)
    return yc * lax.rsqrt(var + EPS) * g + b


def _outproj_kernel(ys_ref, ya_ref, x_ref, gt_ref, sc_ref, sh_ref, wo_ref, lng_ref, lnb_ref,
                    wrh_ref, wrl_ref, rb_ref, x1_ref, u2_ref, rt_ref, cnt_ref, carry, *, tm):
    i = pl.program_id(0)

    @pl.when(i == 0)
    def _init():
        carry[...] = jnp.zeros_like(carry)

    h = _dot(ys_ref[...], wo_ref[0:D_SSM, :]) + _dot(ya_ref[...], wo_ref[D_SSM:D, :])
    x1 = _layer_norm(ALPHA * x_ref[...] + (1.0 + gt_ref[0, 0]) * h, lng_ref[...], lnb_ref[...])
    x1_ref[...] = x1
    u2 = x1 * (1.0 + sc_ref[0, 0]) + sh_ref[0, 0]
    u2_ref[...] = u2

    uh = u2.astype(bf16)
    ul = (u2 - uh.astype(f32)).astype(bf16)
    logits = (_dot(uh, wrh_ref[...]) + _dot(ul, wrh_ref[...])) + _dot(uh, wrl_ref[...]) + rb_ref[...]
    lane = lax.broadcasted_iota(i32, (tm, 128), 1).astype(f32)
    big = jnp.float32(1e9)

    def first_max(vals):
        m = jnp.max(vals, axis=-1, keepdims=True)
        return m, jnp.min(jnp.where(vals == m, lane, big), axis=-1, keepdims=True)

    gl = jnp.where(lane < N_GROUPS_R, logits, NEG)
    gmax, gidx = first_max(gl)
    g_p = 1.0 / jnp.sum(jnp.exp(gl - gmax), axis=-1, keepdims=True)
    lo = N_GROUPS_R + EXPERTS_PER_GROUP * gidx
    el = jnp.where((lane >= lo) & (lane < lo + EXPERTS_PER_GROUP), logits, NEG)
    m1, i1 = first_max(el)
    el2 = jnp.where(lane == i1, NEG, el)
    m2, i2 = first_max(el2)
    r = jnp.exp(m2 - m1)
    w1 = g_p / (1.0 + r)
    w2 = g_p * r / (1.0 + r)

    oh1 = lane == i1
    oh2 = lane == i2
    oh = jnp.where(oh1 | oh2, 1.0, 0.0)
    r_i = lax.broadcasted_iota(i32, (tm, tm), 0)
    c_i = lax.broadcasted_iota(i32, (tm, tm), 1)
    lower = jnp.where(r_i > c_i, 1.0, 0.0).astype(bf16)
    prefix = _dot(lower, oh.astype(bf16)) + carry[...]
    rank1 = jnp.sum(jnp.where(oh1, prefix, 0.0), axis=-1, keepdims=True)
    rank2 = jnp.sum(jnp.where(oh2, prefix, 0.0), axis=-1, keepdims=True)
    carry[...] = carry[...] + jnp.sum(oh, axis=0, keepdims=True)
    cnt_ref[...] = jnp.broadcast_to(carry[...], (8, 128))

    rt = jnp.where(lane == 0, i1 - N_GROUPS_R, 0.0)
    rt = jnp.where(lane == 1, i2 - N_GROUPS_R, rt)
    rt = jnp.where(lane == 2, rank1, rt)
    rt = jnp.where(lane == 3, rank2, rt)
    rt = jnp.where(lane == 4, w1, rt)
    rt = jnp.where(lane == 5, w2, rt)
    rt_ref[...] = rt


def _outproj(y_ssm, y_att, x, mod4, w_out, ln_g, ln_b, wr_hi, wr_lo, rb, t):
    n = x.shape[0]
    tm = min(OUTPROJ_ROWS, t)
    nt = t // tm
    vec = lambda k: pl.BlockSpec((1, 1, 1, D), lambda i, k=k: (k, i // nt, 0, 0))
    full = lambda shape: pl.BlockSpec(shape, lambda i: (0,) * len(shape))
    rows = lambda w: pl.BlockSpec((tm, w), lambda i: (i, 0))
    return pl.pallas_call(
        functools.partial(_outproj_kernel, tm=tm),
        grid=(n // tm,),
        in_specs=[rows(D_SSM), rows(D_ATT), rows(D), vec(2), vec(4), vec(3),
                  full((D, D)), full((1, D)), full((1, D)), full((D, 128)), full((D, 128)), full((1, 128))],
        out_specs=[rows(D), rows(D), rows(128), full((8, 128))],
        out_shape=[jax.ShapeDtypeStruct((n, D), f32), jax.ShapeDtypeStruct((n, D), f32),
                   jax.ShapeDtypeStruct((n, 128), f32), jax.ShapeDtypeStruct((8, 128), f32)],
        scratch_shapes=[pltpu.VMEM((1, 128), f32)],
        compiler_params=pltpu.CompilerParams(dimension_semantics=("arbitrary",), vmem_limit_bytes=VMEM_LIMIT),
        name="outproj",
    )(y_ssm, y_att, x, mod4, mod4, mod4, w_out, ln_g, ln_b, wr_hi, wr_lo, rb)


def _row_copy(src_hbm, row, buf, slot, sem):
    return pltpu.make_async_copy(src_hbm.at[pl.ds(row, 1)], buf.at[pl.ds(slot, 1)], sem)


def _dest_kernel(rt_ref, ps_ref, o_ref, *, tm):
    rt = rt_ref[...]
    lane = lax.broadcasted_iota(i32, (tm, 128), 1).astype(f32)
    ps = ps_ref[...]

    def start_of(e):
        return jnp.sum(jnp.where(lane == e + N_GROUPS_R, ps, 0.0), axis=-1, keepdims=True)

    d1 = start_of(rt[:, 0:1]) + rt[:, 2:3]
    d2 = start_of(rt[:, 1:2]) + rt[:, 3:4]
    dd = jnp.where(lane == 0, d1, jnp.where(lane == 1, d2, 0.0))
    o_ref[...] = dd.T[0:8, :].astype(i32)


def _dest(rt, ps_lanes, t):
    n = rt.shape[0]
    tm = min(OUTPROJ_ROWS, t)
    return pl.pallas_call(
        functools.partial(_dest_kernel, tm=tm),
        grid=(n // tm,),
        in_specs=[pl.BlockSpec((tm, 128), lambda i: (i, 0)), pl.BlockSpec((1, 128), lambda i: (0, 0))],
        out_specs=pl.BlockSpec((8, tm), lambda i: (0, i)),
        out_shape=jax.ShapeDtypeStruct((8, n), i32),
        compiler_params=pltpu.CompilerParams(dimension_semantics=("arbitrary",)),
        name="dest",
    )(rt, ps_lanes)


def _moe_kernel(be_ref, nu_ref, cnt_ref, ps_ref, d1_ref, d2_ref, u_hbm, wg_ref, wu_ref, wd_ref, y_ref,
                row_tok, xbuf, wgb, wub, wdb, sem, *, blk, n):
    i = pl.program_id(0)
    n_used = nu_ref[0]

    def gather(b, slot):
        for r in range(blk):
            _row_copy(u_hbm, row_tok[b * blk + r], xbuf.at[slot], r, sem.at[slot]).start()

    @pl.when(i == 0)
    def _prologue():
        def pad_expert(e, c):
            cnt = cnt_ref[e]

            def zero(j, c2):
                row_tok[ps_ref[e] + j] = 0
                return c2
            return lax.fori_loop(cnt, (cnt + blk - 1) // blk * blk, zero, c)
        lax.fori_loop(0, N_EXPERTS, pad_expert, 0)

        def fill(tok, c):
            row_tok[d1_ref[tok]] = tok
            row_tok[d2_ref[tok]] = tok
            return c
        lax.fori_loop(0, n, fill, 0, unroll=8)
        gather(0, 0)

    @pl.when(i >= n_used)
    def _unused():
        y_ref[...] = jnp.zeros_like(y_ref)

    @pl.when(jnp.logical_and(i < n_used, jnp.logical_or(i == 0, be_ref[i] != be_ref[jnp.maximum(i - 1, 0)])))
    def _cast():
        wgb[...] = wg_ref[0].astype(bf16)
        wub[...] = wu_ref[0].astype(bf16)
        wdb[...] = wd_ref[0].astype(bf16)

    def rows_ready(slot):
        return pltpu.make_async_copy(xbuf.at[slot], xbuf.at[slot], sem.at[slot])

    @pl.when(i < n_used)
    def _work():
        slot = i % 2
        rows_ready(slot).wait()
        gather(jnp.minimum(i + 1, n_used - 1), 1 - slot)
        xb = xbuf[slot].astype(bf16)
        hid = (_silu(_dot(xb, wgb[...])) * _dot(xb, wub[...])).astype(bf16)
        y_ref[...] = _dot(hid, wdb[...])

    @pl.when(i == n_used - 1)
    def _drain():
        rows_ready(1 - i % 2).wait()


def _moe(block_e, n_used, counts, pad_start, dest1, dest2, u2, w_gate, w_up, w_down, blk):
    nblk = block_e.shape[0]
    n = u2.shape[0]
    wspec = lambda a, b: pl.BlockSpec((1, a, b), lambda i, be, *_: (be[i], 0, 0))
    grid_spec = pltpu.PrefetchScalarGridSpec(
        num_scalar_prefetch=6,
        grid=(nblk,),
        in_specs=[pl.BlockSpec(memory_space=pl.ANY), wspec(D, D_EXPERT), wspec(D, D_EXPERT), wspec(D_EXPERT, D)],
        out_specs=pl.BlockSpec((blk, D), lambda i, *_: (i, 0)),
        scratch_shapes=[pltpu.SMEM((nblk * blk,), i32),
                        pltpu.VMEM((2, blk, D), f32),
                        pltpu.VMEM((D, D_EXPERT), bf16), pltpu.VMEM((D, D_EXPERT), bf16),
                        pltpu.VMEM((D_EXPERT, D), bf16),
                        pltpu.SemaphoreType.DMA((2,))],
    )
    return pl.pallas_call(
        functools.partial(_moe_kernel, blk=blk, n=n),
        grid_spec=grid_spec,
        out_shape=jax.ShapeDtypeStruct((nblk * blk, D), f32),
        compiler_params=pltpu.CompilerParams(dimension_semantics=("arbitrary",), vmem_limit_bytes=VMEM_LIMIT),
        name="moe",
    )(block_e, n_used, counts, pad_start, dest1, dest2, u2, w_gate, w_up, w_down)


def _combine_kernel(d1_ref, d2_ref, y_hbm, x1_ref, rt_ref, gt_ref, lng_ref, lnb_ref, o_ref, buf, sem, *, tm):
    i = pl.program_id(0)
    nsteps = pl.num_programs(0)

    @pl.when(i == 0)
    def _prologue():
        def issue(r, c):
            _row_copy(y_hbm, d1_ref[r], buf.at[0, 0], r, sem.at[0]).start()
            _row_copy(y_hbm, d2_ref[r], buf.at[0, 1], r, sem.at[0]).start()
            return c
        lax.fori_loop(0, tm, issue, 0)

    def rows_ready(slot):
        return pltpu.make_async_copy(buf.at[slot], buf.at[slot], sem.at[slot])

    slot = i % 2
    nxt = 1 - slot
    base = jnp.minimum(i + 1, nsteps - 1) * tm
    rows_ready(slot).wait()
    for r in range(tm):
        _row_copy(y_hbm, d1_ref[base + r], buf.at[nxt, 0], r, sem.at[nxt]).start()
        _row_copy(y_hbm, d2_ref[base + r], buf.at[nxt, 1], r, sem.at[nxt]).start()
    rt = rt_ref[...]
    moe = rt[:, 4:5] * buf[slot, 0] + rt[:, 5:6] * buf[slot, 1]
    y = ALPHA * x1_ref[...] + (1.0 + gt_ref[0, 0]) * moe
    o_ref[...] = _layer_norm(y, lng_ref[...], lnb_ref[...])

    @pl.when(i == nsteps - 1)
    def _drain():
        rows_ready(nxt).wait()


def _combine(dest1, dest2, y_rows, x1, rt, mod4, ln_g, ln_b, t):
    n = x1.shape[0]
    tm = min(COMBINE_ROWS, t)
    nt = t // tm
    full = lambda shape: pl.BlockSpec(shape, lambda i, d1, d2: (0,) * len(shape))
    rows = lambda w: pl.BlockSpec((tm, w), lambda i, d1, d2: (i, 0))
    grid_spec = pltpu.PrefetchScalarGridSpec(
        num_scalar_prefetch=2,
        grid=(n // tm,),
        in_specs=[pl.BlockSpec(memory_space=pl.ANY), rows(D), rows(128),
                  pl.BlockSpec((1, 1, 1, D), lambda i, d1, d2: (5, i // nt, 0, 0)),
                  full((1, D)), full((1, D))],
        out_specs=rows(D),
        scratch_shapes=[pltpu.VMEM((2, 2, tm, D), f32), pltpu.SemaphoreType.DMA((2,))],
    )
    return pl.pallas_call(
        functools.partial(_combine_kernel, tm=tm),
        grid_spec=grid_spec,
        out_shape=jax.ShapeDtypeStruct((n, D), f32),
        compiler_params=pltpu.CompilerParams(dimension_semantics=("arbitrary",), vmem_limit_bytes=VMEM_LIMIT),
        name="combine",
    )(dest1, dest2, y_rows, x1, rt, mod4, ln_g, ln_b)


def _hi_lo(w):
    hi = w.astype(bf16)
    return hi, (w - hi.astype(f32)).astype(bf16)


def _pad_lanes(v, offset, width):
    return jnp.zeros((width,), f32).at[offset:offset + v.shape[0]].set(v)


def _layer(x, c, ada_w, ada_b, w_in, conv_w, conv_b, dt_bias, a_log, d_skip, ssm_norm_g, fg_bias, att_norm_g,
           w_out, ln1_g, ln1_b, router_g_w, router_g_b, router_e_w, router_e_b, w_gate, w_up, w_down, ln2_g, ln2_b):
    bsz, t, _ = x.shape
    n = bsz * t

    mod4 = _ada(c, ada_w, ada_b).reshape(6, bsz, 1, D)

    w_main = jnp.concatenate([w_in[:, 0:1536], w_in[:, 1544:3080]], axis=1).astype(bf16)
    w_small = jnp.concatenate([w_in[:, 1536:1544], w_in[:, 3080:3088], jnp.zeros((D, 112), f32)], axis=1)
    ws_hi, ws_lo = _hi_lo(w_small)
    zx, qkv, dtf = _inproj(x, mod4, w_main, ws_hi, ws_lo)

    pc = jnp.stack([_pad_lanes(dt_bias, 0, 128), _pad_lanes(a_log, 0, 128), _pad_lanes(fg_bias, 8, 128)]
                   + [jnp.zeros((128,), f32)] * 5)
    rep = lambda v: jnp.repeat(v, HEAD_DIM)
    pe = jnp.stack([rep(dt_bias), rep(a_log), rep(d_skip), ssm_norm_g] + [jnp.zeros((D_SSM,), f32)] * 4)
    y_ssm, cumc = _ssd(zx, dtf, conv_w, conv_b.reshape(1, -1), pc, pe)

    y_att = _attn(qkv, cumc, att_norm_g.reshape(1, -1))

    wr = jnp.concatenate([router_g_w, router_e_w, jnp.zeros((D, 128 - N_GROUPS_R - N_EXPERTS), f32)], axis=1)
    wr_hi, wr_lo = _hi_lo(wr)
    rb = jnp.concatenate([router_g_b, router_e_b, jnp.zeros((128 - N_GROUPS_R - N_EXPERTS,), f32)]).reshape(1, 128)
    x1, u2, rt, cnt = _outproj(y_ssm.reshape(n, D_SSM), y_att.reshape(n, D_ATT), x.reshape(n, D), mod4,
                               w_out.astype(bf16), ln1_g.reshape(1, D), ln1_b.reshape(1, D), wr_hi, wr_lo, rb, t)

    blk = MOE_ROWS
    counts = cnt[0, N_GROUPS_R:N_GROUPS_R + N_EXPERTS].astype(i32)
    padded = (counts + blk - 1) // blk * blk
    pad_end = jnp.cumsum(padded)
    pad_start = pad_end - padded
    nblk = (2 * n) // blk + N_EXPERTS
    blk_start = jnp.arange(nblk, dtype=i32) * blk
    block_e = jnp.minimum(jnp.sum((pad_end[None, :] <= blk_start[:, None]).astype(i32), axis=1), N_EXPERTS - 1)
    n_used = (pad_end[N_EXPERTS - 1:] // blk).astype(i32)
    ps_lanes = jnp.zeros((1, 128), f32).at[0, N_GROUPS_R:N_GROUPS_R + N_EXPERTS].set(pad_start.astype(f32))
    dest = _dest(rt, ps_lanes, t)

    y_rows = _moe(block_e, n_used, counts, pad_start, dest[0], dest[1], u2, w_gate, w_up, w_down, blk)
    out = _combine(dest[0], dest[1], y_rows, x1, rt, mod4, ln2_g.reshape(1, D), ln2_b.reshape(1, D), t)
    return out.reshape(bsz, t, D)


def kernel(x, c, ada_w, ada_b, w_in, conv_w, conv_b, dt_bias, a_log, d_skip, ssm_norm_g, fg_bias, att_norm_g, w_out,
           ln1_g, ln1_b, router_g_w, router_g_b, router_e_w, router_e_b, w_gate, w_up, w_down, ln2_g, ln2_b):
    depth = ada_w.shape[0]
    for l in range(depth):
        x = _layer(x, c, ada_w[l], ada_b[l], w_in[l], conv_w[l], conv_b[l], dt_bias[l], a_log[l], d_skip[l],
                   ssm_norm_g[l], fg_bias[l], att_norm_g[l], w_out[l], ln1_g[l], ln1_b[l], router_g_w[l],
                   router_g_b[l], router_e_w[l], router_e_b[l], w_gate[l], w_up[l], w_down[l], ln2_g[l], ln2_b[l])
    return x
```

```python
import functools

import jax
import jax.numpy as jnp
import numpy as np
from jax import lax
from jax.experimental import pallas as pl
from jax.experimental.pallas import tpu as pltpu

f32 = jnp.float32
bf16 = jnp.bfloat16
i32 = jnp.int32

D = 1024
D_SSM = 512
D_ATT = 512
HEAD_DIM = 64
GROUP_W = 256
N_STATE = 128
CONV_K = 4
N_GROUPS_R = 4
EXPERTS_PER_GROUP = 8
N_EXPERTS = 32
D_EXPERT = 512
ALPHA = 2.0 ** 0.25
EPS = 1e-5
NEG = -1e30

SSD_CHUNK = 256
ATT_BLOCK = 256
INPROJ_ROWS = 512
OUTPROJ_ROWS = 512
MOE_ROWS = 256
COMBINE_ROWS = 256
DEST_ROWS = 2048
VMEM_LIMIT = 48 * 1024 * 1024


def _dot(a, b):
    return jnp.dot(a, b, preferred_element_type=f32)


def _dot_nt(a, b):
    return lax.dot_general(a, b, (((1,), (1,)), ((), ())), preferred_element_type=f32)


def _dot_tn(a, b):
    return lax.dot_general(a, b, (((0,), (0,)), ((), ())), preferred_element_type=f32)


def _split3(v):
    hi = v.astype(bf16)
    r1 = v - hi.astype(f32)
    mid = r1.astype(bf16)
    lo = (r1 - mid.astype(f32)).astype(bf16)
    return hi, mid, lo


def _dot_exact_lhs(m, v):
    hi, mid, lo = _split3(v)
    return (_dot(m, hi) + _dot(m, mid)) + _dot(m, lo)


def _dot_exact_rhs(v, m):
    hi, mid, lo = _split3(v)
    return (_dot(hi, m) + _dot(mid, m)) + _dot(lo, m)


ROW_TILE = 8


def _store_row_tiles(ref, val):
    rows = val.shape[0]
    for c in range(ROW_TILE):
        ref[pl.ds(c, rows, stride=ROW_TILE), :] = val[:, c * 128:(c + 1) * 128]


def _load_row_tiles(ref, rows):
    return jnp.concatenate([ref[pl.ds(c, rows, stride=ROW_TILE), :] for c in range(ROW_TILE)], axis=1)


def _softplus(x):
    return jnp.maximum(x, 0.0) + jnp.log1p(jnp.exp(-jnp.abs(x)))


def _silu(x):
    return x * jax.nn.sigmoid(x)


def _ada_kernel(c_ref, w_ref, b_ref, o_ref):
    s = _silu(c_ref[...]).astype(bf16)
    o_ref[0] = _dot(s, w_ref[...].astype(bf16)) + b_ref[0]


def _ada(c, w, b):
    bsz = c.shape[0]
    return pl.pallas_call(
        _ada_kernel,
        grid=(6,),
        in_specs=[pl.BlockSpec((bsz, D), lambda j: (0, 0)),
                  pl.BlockSpec((D, D), lambda j: (0, j)),
                  pl.BlockSpec((1, 1, D), lambda j: (j, 0, 0))],
        out_specs=pl.BlockSpec((1, bsz, D), lambda j: (j, 0, 0)),
        out_shape=jax.ShapeDtypeStruct((6, bsz, D), f32),
        compiler_params=pltpu.CompilerParams(dimension_semantics=("arbitrary",), vmem_limit_bytes=VMEM_LIMIT),
        name="ada",
    )(c, w, b.reshape(6, 1, D))


def _inproj_kernel(x_ref, sc_ref, sh_ref, wm_ref, wsh_ref, wsl_ref, zx_ref, qkv_ref, dtf_ref):
    u = x_ref[0] * (1.0 + sc_ref[0, 0]) + sh_ref[0, 0]
    ub = u.astype(bf16)
    for j in range(3):
        zx_ref[0, :, j * 512:(j + 1) * 512] = _dot(ub, wm_ref[:, j * 512:(j + 1) * 512])
    for j in range(3):
        qkv_ref[0, :, j * 512:(j + 1) * 512] = _dot(ub, wm_ref[:, 1536 + j * 512:1536 + (j + 1) * 512]).astype(bf16)
    ul = (u - ub.astype(f32)).astype(bf16)
    dtf_ref[0] = (_dot(ub, wsh_ref[...]) + _dot(ul, wsh_ref[...])) + _dot(ub, wsl_ref[...])


def _inproj(x, mod4, w_main, ws_hi, ws_lo):
    bsz, t, _ = x.shape
    tm = min(INPROJ_ROWS, t)
    vec = lambda k: pl.BlockSpec((1, 1, 1, D), lambda b, i, k=k: (k, b, 0, 0))
    return pl.pallas_call(
        _inproj_kernel,
        grid=(bsz, t // tm),
        in_specs=[pl.BlockSpec((1, tm, D), lambda b, i: (b, i, 0)),
                  vec(1), vec(0),
                  pl.BlockSpec((D, 3072), lambda b, i: (0, 0)),
                  pl.BlockSpec((D, 128), lambda b, i: (0, 0)),
                  pl.BlockSpec((D, 128), lambda b, i: (0, 0))],
        out_specs=[pl.BlockSpec((1, tm, 1536), lambda b, i: (b, i, 0)),
                   pl.BlockSpec((1, tm, 1536), lambda b, i: (b, i, 0)),
                   pl.BlockSpec((1, tm, 128), lambda b, i: (b, i, 0))],
        out_shape=[jax.ShapeDtypeStruct((bsz, t, 1536), f32),
                   jax.ShapeDtypeStruct((bsz, t, 1536), bf16),
                   jax.ShapeDtypeStruct((bsz, t, 128), f32)],
        compiler_params=pltpu.CompilerParams(dimension_semantics=("parallel", "arbitrary"),
                                             vmem_limit_bytes=VMEM_LIMIT),
        name="inproj",
    )(x, mod4, mod4, w_main, ws_hi, ws_lo)


def _ssd_kernel(z_ref, xs_ref, bc_ref, dtf_ref, cw_ref, cb_ref, pc_ref, pe_ref,
                y_ref, cumc_ref, xcat, state, carry, *, lc):
    j = pl.program_id(1)

    @pl.when(j == 0)
    def _init():
        xcat[0:8, :] = jnp.zeros((8, 2 * D_SSM), f32)
        state[...] = jnp.zeros_like(state)
        carry[...] = jnp.zeros_like(carry)

    xcat[8:8 + lc, 0:512] = xs_ref[0]
    xcat[8:8 + lc, 512:1024] = bc_ref[0]
    acc = cw_ref[0:1, :] * xcat[5:5 + lc, :] + cb_ref[...]
    for k in range(1, CONV_K):
        acc = acc + cw_ref[k:k + 1, :] * xcat[5 + k:5 + k + lc, :]
    xcat[0:8, :] = xcat[lc:lc + 8, :]
    xbc = _silu(acc)
    xs = xbc[:, 0:512]
    bm = xbc[:, 512:768]
    cm = xbc[:, 768:1024]

    dtf = dtf_ref[0]
    lane = lax.broadcasted_iota(i32, (lc, 128), 1)
    dt_c = _softplus(dtf + pc_ref[0:1, :])
    a_c = dt_c * (-jnp.exp(pc_ref[1:2, :]))
    logf = -_softplus(-(dtf + pc_ref[2:3, :]))
    v = jnp.where(lane < 8, a_c, logf)
    r_i = lax.broadcasted_iota(i32, (lc, lc), 0)
    c_i = lax.broadcasted_iota(i32, (lc, lc), 1)
    tri = r_i >= c_i
    tri_b = jnp.where(tri, 1.0, 0.0).astype(bf16)
    cum = _dot_exact_lhs(tri_b, v) + carry[...]
    carry[...] = jnp.where(lane[0:1, :] >= 8, cum[lc - 1:lc, :], 0.0)
    cumc_ref[0] = cum
    cs_t = cum.T[0:8, :]

    e_r = lax.broadcasted_iota(i32, (128, D_SSM), 0)
    e_c = lax.broadcasted_iota(i32, (128, D_SSM), 1)
    expand = jnp.where(jnp.right_shift(e_c, 6) == e_r, 1.0, 0.0).astype(bf16)
    dt_e = _dot_exact_rhs(dt_c, expand)
    cs_e = _dot_exact_rhs(cum, expand)

    xdt = xs * dt_e
    ecs = jnp.exp(cs_e)
    cs_last = cs_e[lc - 1:lc, :]
    dec_st = jnp.exp(cs_last - cs_e)
    lane_g = lax.broadcasted_iota(i32, (1, GROUP_W), 1)
    ys = []
    for g in range(2):
        gs = slice(g * GROUP_W, (g + 1) * GROUP_W)
        bg = bm[:, g * N_STATE:(g + 1) * N_STATE].astype(bf16)
        cg = cm[:, g * N_STATE:(g + 1) * N_STATE].astype(bf16)
        cb = _dot_nt(cg, bg)
        xdt_g = xdt[:, gs]
        xdt_gb = xdt_g.astype(bf16)
        ms, xb = [], []
        for hh in range(4):
            h = g * 4 + hh
            lm = jnp.exp(jnp.where(tri, cum[:, h:h + 1] - cs_t[h:h + 1, :], -jnp.inf))
            ms.append((cb * lm).astype(bf16))
            xb.append(jnp.where(jnp.right_shift(lane_g, 6) == hh, xdt_gb, jnp.zeros_like(xdt_gb)))
        y_diag = _dot(jnp.concatenate(ms, axis=1), jnp.concatenate(xb, axis=0))
        st = state[g]
        y_off = _dot(cg, st.astype(bf16)) * ecs[:, gs]
        upd = _dot_tn(bg, (xdt_g * dec_st[:, gs]).astype(bf16))
        state[g] = st * jnp.exp(cs_last[:, gs]) + upd
        ys.append(y_diag + y_off + xs[:, gs] * pe_ref[2:3, gs])

    outs = []
    for g in range(2):
        gs = slice(g * GROUP_W, (g + 1) * GROUP_W)
        yg = ys[g] * _silu(z_ref[0, :, gs])
        ms_ = jnp.mean(yg * yg, axis=-1, keepdims=True)
        outs.append(yg * lax.rsqrt(ms_ + EPS))
    y_ref[0] = (jnp.concatenate(outs, axis=1) * pe_ref[3:4, :]).astype(bf16)


def _ssd(zx, dtf, conv_w, conv_b, pc, pe):
    bsz, t, _ = zx.shape
    lc = min(SSD_CHUNK, t)
    col = lambda k: pl.BlockSpec((1, lc, 512), lambda b, j, k=k: (b, j, k))
    full = lambda shape: pl.BlockSpec(shape, lambda b, j: (0,) * len(shape))
    return pl.pallas_call(
        functools.partial(_ssd_kernel, lc=lc),
        grid=(bsz, t // lc),
        in_specs=[col(0), col(1), col(2),
                  pl.BlockSpec((1, lc, 128), lambda b, j: (b, j, 0)),
                  full((CONV_K, 2 * D_SSM)), full((1, 2 * D_SSM)), full((8, 128)), full((8, D_SSM))],
        out_specs=[pl.BlockSpec((1, lc, D_SSM), lambda b, j: (b, j, 0)),
                   pl.BlockSpec((1, lc, 128), lambda b, j: (b, j, 0))],
        out_shape=[jax.ShapeDtypeStruct((bsz, t, D_SSM), bf16),
                   jax.ShapeDtypeStruct((bsz, t, 128), f32)],
        scratch_shapes=[pltpu.VMEM((lc + 8, 2 * D_SSM), f32),
                        pltpu.VMEM((2, N_STATE, GROUP_W), f32),
                        pltpu.VMEM((1, 128), f32)],
        compiler_params=pltpu.CompilerParams(dimension_semantics=("parallel", "arbitrary"),
                                             vmem_limit_bytes=VMEM_LIMIT),
        name="ssd",
    )(zx, zx, zx, dtf, conv_w, conv_b, pc, pe)


def _attn_kernel(q_ref, k_ref, v_ref, cc_ref, psel_ref, ng_ref, o_ref, kaug, vt, acc, *, tq, t):
    i = pl.program_id(1)
    nkb = t // tq
    n_heads = D_ATT // HEAD_DIM
    lane = lax.broadcasted_iota(i32, (1, 128), 1)
    lo_half = lane < HEAD_DIM

    @pl.when(i == 0)
    def _build():
        eye = jnp.where(lax.broadcasted_iota(i32, (D_ATT, D_ATT), 0) == lax.broadcasted_iota(i32, (D_ATT, D_ATT), 1),
                        1.0, 0.0).astype(bf16)
        for jb in range(nkb):
            rows = slice(jb * tq, (jb + 1) * tq)
            vt[jb] = _dot_nt(eye, v_ref[0, rows, :]).astype(bf16)
            pieces = jnp.concatenate(_split3(-cc_ref[0, rows, :]), axis=1)
            for p in range(n_heads // 2):
                a = _dot(pieces, psel_ref[p]).astype(bf16)
                kp = k_ref[0, rows, p * 128:(p + 1) * 128]
                kaug[2 * p, rows, :] = jnp.where(lo_half, kp, a)
                kaug[2 * p + 1, rows, :] = jnp.where(lo_half, a, kp)

    ones_hi = jnp.where((lane >= HEAD_DIM) & (lane < HEAD_DIM + 3), 1.0, 0.0).astype(bf16)
    ones_lo = jnp.where(lane < 3, 1.0, 0.0).astype(bf16)
    qa = []
    for p in range(n_heads // 2):
        qp = q_ref[0, :, p * 128:(p + 1) * 128] * jnp.asarray(HEAD_DIM ** -0.5, bf16)
        qa.append(jnp.where(lo_half, qp, ones_hi))
        qa.append(jnp.where(lo_half, ones_lo, qp))
    keep = lax.broadcasted_iota(i32, (tq, tq), 0) <= lax.broadcasted_iota(i32, (tq, tq), 1)
    acc[...] = jnp.zeros_like(acc)

    def block(jb, carry, masked):
        ms, ls = carry
        k0 = pl.multiple_of(jb * tq, tq)
        new_ms, new_ls = [], []
        scores = [_dot_nt(kaug[h, pl.ds(k0, tq), :], qa[h]) for h in range(n_heads)]
        for h in range(n_heads):
            s = scores[h]
            if masked:
                s = jnp.where(keep, s, NEG)
            m_new = jnp.maximum(ms[h], jnp.max(s, axis=0, keepdims=True))
            alpha = jnp.exp(ms[h] - m_new)
            p = jnp.exp(s - m_new)
            new_ls.append(alpha * ls[h] + jnp.sum(p, axis=0, keepdims=True))
            new_ms.append(m_new)
            pv = _dot(vt[jb, h * HEAD_DIM:(h + 1) * HEAD_DIM, :], p.astype(bf16))
            acc[h] = acc[h] * alpha + pv
        return tuple(new_ms), tuple(new_ls)

    init = (tuple(jnp.full((1, tq), NEG, f32) for _ in range(n_heads)),
            tuple(jnp.zeros((1, tq), f32) for _ in range(n_heads)))
    carry = lax.fori_loop(0, i, lambda jb, c: block(jb, c, False), init)
    _, ls = block(i, carry, True)
    out_t = jnp.concatenate([acc[h] * (1.0 / ls[h]) for h in range(n_heads)], axis=0)
    ms_ = jnp.mean(out_t * out_t, axis=0, keepdims=True)
    out_t = out_t * lax.rsqrt(ms_ + EPS)
    o_ref[0] = (out_t.T * ng_ref[...]).astype(bf16)


def _piece_select():
    sel = np.zeros((4, 384, 128), np.float32)
    for pair in range(4):
        for j in range(3):
            sel[pair, j * 128 + 8 + 2 * pair, HEAD_DIM + j] = 1.0
            sel[pair, j * 128 + 8 + 2 * pair + 1, j] = 1.0
    return jnp.asarray(sel, bf16)


def _attn(qkv, cumc, norm_g):
    bsz, t, _ = qkv.shape
    tq = min(ATT_BLOCK, t)
    return pl.pallas_call(
        functools.partial(_attn_kernel, tq=tq, t=t),
        grid=(bsz, t // tq),
        in_specs=[pl.BlockSpec((1, tq, D_ATT), lambda b, i: (b, i, 0)),
                  pl.BlockSpec((1, t, D_ATT), lambda b, i: (b, 0, 1)),
                  pl.BlockSpec((1, t, D_ATT), lambda b, i: (b, 0, 2)),
                  pl.BlockSpec((1, t, 128), lambda b, i: (b, 0, 0)),
                  pl.BlockSpec((4, 384, 128), lambda b, i: (0, 0, 0)),
                  pl.BlockSpec((1, D_ATT), lambda b, i: (0, 0))],
        out_specs=pl.BlockSpec((1, tq, D_ATT), lambda b, i: (b, i, 0)),
        out_shape=jax.ShapeDtypeStruct((bsz, t, D_ATT), bf16),
        scratch_shapes=[pltpu.VMEM((D_ATT // HEAD_DIM, t, 128), bf16),
                        pltpu.VMEM((t // tq, D_ATT, tq), bf16),
                        pltpu.VMEM((D_ATT // HEAD_DIM, HEAD_DIM, tq), f32)],
        compiler_params=pltpu.CompilerParams(dimension_semantics=("parallel", "arbitrary"),
                                             vmem_limit_bytes=VMEM_LIMIT),
        name="attn",
    )(qkv, qkv, qkv, cumc, _piece_select(), norm_g)


def _layer_norm(y, g, b):
    mu = jnp.mean(y, axis=-1, keepdims=True)
    yc = y - mu
    var = jnp.mean(yc * yc, axis=-1, keepdims=True)
    return yc * lax.rsqrt(var + EPS) * g + b


def _outproj_kernel(ys_ref, ya_ref, x_ref, gt_ref, sc_ref, sh_ref, wo_ref, lng_ref, lnb_ref,
                    wrh_ref, wrl_ref, rb_ref, x1_ref, u2_ref, rt_ref, cnt_ref, carry, *, tm):
    i = pl.program_id(0)

    @pl.when(i == 0)
    def _init():
        carry[...] = jnp.zeros_like(carry)

    h = _dot(ys_ref[...], wo_ref[0:D_SSM, :]) + _dot(ya_ref[...], wo_ref[D_SSM:D, :])
    x1 = _layer_norm(ALPHA * x_ref[...] + (1.0 + gt_ref[0, 0]) * h, lng_ref[...], lnb_ref[...])
    x1_ref[...] = x1
    u2 = x1 * (1.0 + sc_ref[0, 0]) + sh_ref[0, 0]
    _store_row_tiles(u2_ref, u2)

    uh = u2.astype(bf16)
    ul = (u2 - uh.astype(f32)).astype(bf16)
    logits = (_dot(uh, wrh_ref[...]) + _dot(ul, wrh_ref[...])) + _dot(uh, wrl_ref[...]) + rb_ref[...]
    lane = lax.broadcasted_iota(i32, (tm, 128), 1).astype(f32)
    big = jnp.float32(1e9)

    def first_max(vals):
        m = jnp.max(vals, axis=-1, keepdims=True)
        return m, jnp.min(jnp.where(vals == m, lane, big), axis=-1, keepdims=True)

    gl = jnp.where(lane < N_GROUPS_R, logits, NEG)
    gmax, gidx = first_max(gl)
    g_p = 1.0 / jnp.sum(jnp.exp(gl - gmax), axis=-1, keepdims=True)
    lo = N_GROUPS_R + EXPERTS_PER_GROUP * gidx
    el = jnp.where((lane >= lo) & (lane < lo + EXPERTS_PER_GROUP), logits, NEG)
    m1, i1 = first_max(el)
    el2 = jnp.where(lane == i1, NEG, el)
    m2, i2 = first_max(el2)
    r = jnp.exp(m2 - m1)
    w1 = g_p / (1.0 + r)
    w2 = g_p * r / (1.0 + r)

    oh1 = lane == i1
    oh2 = lane == i2
    oh = jnp.where(oh1 | oh2, 1.0, 0.0)
    r_i = lax.broadcasted_iota(i32, (tm, tm), 0)
    c_i = lax.broadcasted_iota(i32, (tm, tm), 1)
    lower = jnp.where(r_i > c_i, 1.0, 0.0).astype(bf16)
    prefix = _dot(lower, oh.astype(bf16)) + carry[...]
    rank1 = jnp.sum(jnp.where(oh1, prefix, 0.0), axis=-1, keepdims=True)
    rank2 = jnp.sum(jnp.where(oh2, prefix, 0.0), axis=-1, keepdims=True)
    carry[...] = carry[...] + jnp.sum(oh, axis=0, keepdims=True)
    cnt_ref[...] = jnp.broadcast_to(carry[...], (8, 128))

    rt = jnp.where(lane == 0, i1 - N_GROUPS_R, 0.0)
    rt = jnp.where(lane == 1, i2 - N_GROUPS_R, rt)
    rt = jnp.where(lane == 2, rank1, rt)
    rt = jnp.where(lane == 3, rank2, rt)
    rt = jnp.where(lane == 4, w1, rt)
    rt = jnp.where(lane == 5, w2, rt)
    rt_ref[...] = rt


def _outproj(y_ssm, y_att, x, mod4, w_out, ln_g, ln_b, wr_hi, wr_lo, rb, t):
    n = x.shape[0]
    tm = min(OUTPROJ_ROWS, t)
    nt = t // tm
    vec = lambda k: pl.BlockSpec((1, 1, 1, D), lambda i, k=k: (k, i // nt, 0, 0))
    full = lambda shape: pl.BlockSpec(shape, lambda i: (0,) * len(shape))
    rows = lambda w: pl.BlockSpec((tm, w), lambda i: (i, 0))
    return pl.pallas_call(
        functools.partial(_outproj_kernel, tm=tm),
        grid=(n // tm,),
        in_specs=[rows(D_SSM), rows(D_ATT), rows(D), vec(2), vec(4), vec(3),
                  full((D, D)), full((1, D)), full((1, D)), full((D, 128)), full((D, 128)), full((1, 128))],
        out_specs=[rows(D), pl.BlockSpec((tm * ROW_TILE, 128), lambda i: (i, 0)), rows(128), full((8, 128))],
        out_shape=[jax.ShapeDtypeStruct((n, D), f32), jax.ShapeDtypeStruct((n * ROW_TILE, 128), f32),
                   jax.ShapeDtypeStruct((n, 128), f32), jax.ShapeDtypeStruct((8, 128), f32)],
        scratch_shapes=[pltpu.VMEM((1, 128), f32)],
        compiler_params=pltpu.CompilerParams(dimension_semantics=("arbitrary",), vmem_limit_bytes=VMEM_LIMIT),
        name="outproj",
    )(y_ssm, y_att, x, mod4, mod4, mod4, w_out, ln_g, ln_b, wr_hi, wr_lo, rb)


def _tile_copy(src_hbm, row, buf, slot, sem):
    return pltpu.make_async_copy(src_hbm.at[row], buf.at[pl.ds(slot * ROW_TILE, ROW_TILE)], sem)


def _dest_kernel(rt_ref, ps_ref, o_ref, *, tm):
    rt = rt_ref[...]
    lane = lax.broadcasted_iota(i32, (tm, 128), 1).astype(f32)
    ps = ps_ref[...]

    def start_of(e):
        return jnp.sum(jnp.where(lane == e + N_GROUPS_R, ps, 0.0), axis=-1, keepdims=True)

    d1 = start_of(rt[:, 0:1]) + rt[:, 2:3]
    d2 = start_of(rt[:, 1:2]) + rt[:, 3:4]
    dd = jnp.where(lane == 0, d1, jnp.where(lane == 1, d2, 0.0))
    o_ref[...] = dd.T[0:8, :].astype(i32)


def _dest(rt, ps_lanes):
    n = rt.shape[0]
    tm = min(DEST_ROWS, n)
    return pl.pallas_call(
        functools.partial(_dest_kernel, tm=tm),
        grid=(n // tm,),
        in_specs=[pl.BlockSpec((tm, 128), lambda i: (i, 0)), pl.BlockSpec((1, 128), lambda i: (0, 0))],
        out_specs=pl.BlockSpec((8, tm), lambda i: (0, i)),
        out_shape=jax.ShapeDtypeStruct((8, n), i32),
        compiler_params=pltpu.CompilerParams(dimension_semantics=("arbitrary",)),
        name="dest",
    )(rt, ps_lanes)


def _moe_kernel(be_ref, nu_ref, cnt_ref, ps_ref, d1_ref, d2_ref, u_hbm, wg_ref, wu_ref, wd_ref, y_ref,
                row_tok, xbuf, wgb, wub, wdb, sem, *, blk, n):
    i = pl.program_id(0)
    n_used = nu_ref[0]
    last = n_used - 1

    def gather(b, slot):
        for r in range(blk):
            _tile_copy(u_hbm, row_tok[b * blk + r], xbuf.at[slot], r, sem.at[slot]).start(priority=r % 2)

    def gather_rolled(b, slot):
        def issue(r, c):
            _tile_copy(u_hbm, row_tok[b * blk + r], xbuf.at[slot], r, sem.at[slot]).start()
            return c
        lax.fori_loop(0, blk, issue, 0)

    @pl.when(i == 0)
    def _prologue():
        def pad_expert(e, c):
            cnt = cnt_ref[e]

            def zero(j, c2):
                row_tok[ps_ref[e] + j] = 0
                return c2
            return lax.fori_loop(cnt, (cnt + blk - 1) // blk * blk, zero, c)
        lax.fori_loop(0, N_EXPERTS, pad_expert, 0)

        def fill(tok, c):
            row_tok[d1_ref[tok]] = tok
            row_tok[d2_ref[tok]] = tok
            return c
        lax.fori_loop(0, n, fill, 0, unroll=8)
        gather_rolled(0, 0)
        gather_rolled(jnp.minimum(1, last), 1)

    @pl.when(i >= n_used)
    def _unused():
        y_ref[...] = jnp.zeros_like(y_ref)

    @pl.when(jnp.logical_and(i < n_used, jnp.logical_or(i == 0, be_ref[i] != be_ref[jnp.maximum(i - 1, 0)])))
    def _cast():
        wgb[...] = wg_ref[0].astype(bf16)
        wub[...] = wu_ref[0].astype(bf16)
        wdb[...] = wd_ref[0].astype(bf16)

    def rows_ready(slot):
        return pltpu.make_async_copy(xbuf.at[slot], xbuf.at[slot], sem.at[slot])

    @pl.when(i < n_used)
    def _work():
        slot = i % 3
        rows_ready(slot).wait()
        gather(jnp.minimum(i + 2, last), (i + 2) % 3)
        xb = _load_row_tiles(xbuf.at[slot], blk).astype(bf16)
        hid = (_silu(_dot(xb, wgb[...])) * _dot(xb, wub[...])).astype(bf16)
        _store_row_tiles(y_ref, _dot(hid, wdb[...]))

    @pl.when(i == last)
    def _drain():
        rows_ready((i + 1) % 3).wait()
        rows_ready((i + 2) % 3).wait()


def _moe(block_e, n_used, counts, pad_start, dest1, dest2, u2, w_gate, w_up, w_down, blk):
    nblk = block_e.shape[0]
    n = u2.shape[0]
    wspec = lambda a, b: pl.BlockSpec((1, a, b), lambda i, be, *_: (be[i], 0, 0))
    grid_spec = pltpu.PrefetchScalarGridSpec(
        num_scalar_prefetch=6,
        grid=(nblk,),
        in_specs=[pl.BlockSpec(memory_space=pl.ANY), wspec(D, D_EXPERT), wspec(D, D_EXPERT), wspec(D_EXPERT, D)],
        out_specs=pl.BlockSpec((blk * ROW_TILE, 128), lambda i, *_: (i, 0)),
        scratch_shapes=[pltpu.SMEM((nblk * blk,), i32),
                        pltpu.VMEM((3, blk * ROW_TILE, 128), f32),
                        pltpu.VMEM((D, D_EXPERT), bf16), pltpu.VMEM((D, D_EXPERT), bf16),
                        pltpu.VMEM((D_EXPERT, D), bf16),
                        pltpu.SemaphoreType.DMA((3,))],
    )
    return pl.pallas_call(
        functools.partial(_moe_kernel, blk=blk, n=n),
        grid_spec=grid_spec,
        out_shape=jax.ShapeDtypeStruct((nblk * blk * ROW_TILE, 128), f32),
        compiler_params=pltpu.CompilerParams(dimension_semantics=("arbitrary",), vmem_limit_bytes=VMEM_LIMIT),
        name="moe",
    )(block_e, n_used, counts, pad_start, dest1, dest2, u2, w_gate, w_up, w_down)


def _combine_kernel(d1_ref, d2_ref, y_hbm, x1_ref, rt_ref, gt_ref, lng_ref, lnb_ref, o_ref, buf, sem, *, tm):
    i = pl.program_id(0)
    last = pl.num_programs(0) - 1

    @pl.when(i == 0)
    def _prologue():
        for s in range(2):
            def issue(r, c, s=s):
                base = jnp.minimum(s, last) * tm
                _tile_copy(y_hbm, d1_ref[base + r], buf.at[s, 0], r, sem.at[s]).start()
                _tile_copy(y_hbm, d2_ref[base + r], buf.at[s, 1], r, sem.at[s]).start()
                return c
            lax.fori_loop(0, tm, issue, 0)

    def rows_ready(slot):
        return pltpu.make_async_copy(buf.at[slot], buf.at[slot], sem.at[slot])

    slot = i % 3
    nxt = (i + 2) % 3
    base = jnp.minimum(i + 2, last) * tm
    rows_ready(slot).wait()
    for r in range(tm):
        _tile_copy(y_hbm, d1_ref[base + r], buf.at[nxt, 0], r, sem.at[nxt]).start(priority=0)
        _tile_copy(y_hbm, d2_ref[base + r], buf.at[nxt, 1], r, sem.at[nxt]).start(priority=1)
    rt = rt_ref[...]
    moe = rt[:, 4:5] * _load_row_tiles(buf.at[slot, 0], tm) + rt[:, 5:6] * _load_row_tiles(buf.at[slot, 1], tm)
    y = ALPHA * x1_ref[...] + (1.0 + gt_ref[0, 0]) * moe
    o_ref[...] = _layer_norm(y, lng_ref[...], lnb_ref[...])

    @pl.when(i == last)
    def _drain():
        rows_ready((i + 1) % 3).wait()
        rows_ready((i + 2) % 3).wait()


def _combine(dest1, dest2, y_rows, x1, rt, mod4, ln_g, ln_b, t):
    n = x1.shape[0]
    tm = min(COMBINE_ROWS, t)
    nt = t // tm
    full = lambda shape: pl.BlockSpec(shape, lambda i, d1, d2: (0,) * len(shape))
    rows = lambda w: pl.BlockSpec((tm, w), lambda i, d1, d2: (i, 0))
    grid_spec = pltpu.PrefetchScalarGridSpec(
        num_scalar_prefetch=2,
        grid=(n // tm,),
        in_specs=[pl.BlockSpec(memory_space=pl.ANY), rows(D), rows(128),
                  pl.BlockSpec((1, 1, 1, D), lambda i, d1, d2: (5, i // nt, 0, 0)),
                  full((1, D)), full((1, D))],
        out_specs=rows(D),
        scratch_shapes=[pltpu.VMEM((3, 2, tm * ROW_TILE, 128), f32), pltpu.SemaphoreType.DMA((3,))],
    )
    return pl.pallas_call(
        functools.partial(_combine_kernel, tm=tm),
        grid_spec=grid_spec,
        out_shape=jax.ShapeDtypeStruct((n, D), f32),
        compiler_params=pltpu.CompilerParams(dimension_semantics=("arbitrary",), vmem_limit_bytes=VMEM_LIMIT),
        name="combine",
    )(dest1, dest2, y_rows, x1, rt, mod4, ln_g, ln_b)


def _hi_lo(w):
    hi = w.astype(bf16)
    return hi, (w - hi.astype(f32)).astype(bf16)


def _pad_lanes(v, offset, width):
    return jnp.zeros((width,), f32).at[offset:offset + v.shape[0]].set(v)


def _layer(x, c, ada_w, ada_b, w_in, conv_w, conv_b, dt_bias, a_log, d_skip, ssm_norm_g, fg_bias, att_norm_g,
           w_out, ln1_g, ln1_b, router_g_w, router_g_b, router_e_w, router_e_b, w_gate, w_up, w_down, ln2_g, ln2_b):
    bsz, t, _ = x.shape
    n = bsz * t

    mod4 = _ada(c, ada_w, ada_b).reshape(6, bsz, 1, D)

    w_main = jnp.concatenate([w_in[:, 0:1536], w_in[:, 1544:3080]], axis=1).astype(bf16)
    w_small = jnp.concatenate([w_in[:, 1536:1544], w_in[:, 3080:3088], jnp.zeros((D, 112), f32)], axis=1)
    ws_hi, ws_lo = _hi_lo(w_small)
    zx, qkv, dtf = _inproj(x, mod4, w_main, ws_hi, ws_lo)

    pc = jnp.stack([_pad_lanes(dt_bias, 0, 128), _pad_lanes(a_log, 0, 128), _pad_lanes(fg_bias, 8, 128)]
                   + [jnp.zeros((128,), f32)] * 5)
    rep = lambda v: jnp.repeat(v, HEAD_DIM)
    pe = jnp.stack([rep(dt_bias), rep(a_log), rep(d_skip), ssm_norm_g] + [jnp.zeros((D_SSM,), f32)] * 4)
    y_ssm, cumc = _ssd(zx, dtf, conv_w, conv_b.reshape(1, -1), pc, pe)

    y_att = _attn(qkv, cumc, att_norm_g.reshape(1, -1))

    wr = jnp.concatenate([router_g_w, router_e_w, jnp.zeros((D, 128 - N_GROUPS_R - N_EXPERTS), f32)], axis=1)
    wr_hi, wr_lo = _hi_lo(wr)
    rb = jnp.concatenate([router_g_b, router_e_b, jnp.zeros((128 - N_GROUPS_R - N_EXPERTS,), f32)]).reshape(1, 128)
    x1, u2, rt, cnt = _outproj(y_ssm.reshape(n, D_SSM), y_att.reshape(n, D_ATT), x.reshape(n, D), mod4,
                               w_out.astype(bf16), ln1_g.reshape(1, D), ln1_b.reshape(1, D), wr_hi, wr_lo, rb, t)

    blk = MOE_ROWS
    counts = cnt[0, N_GROUPS_R:N_GROUPS_R + N_EXPERTS].astype(i32)
    padded = (counts + blk - 1) // blk * blk
    pad_end = jnp.cumsum(padded)
    pad_start = pad_end - padded
    nblk = (2 * n) // blk + N_EXPERTS
    blk_start = jnp.arange(nblk, dtype=i32) * blk
    block_e = jnp.minimum(jnp.sum((pad_end[None, :] <= blk_start[:, None]).astype(i32), axis=1), N_EXPERTS - 1)
    n_used = (pad_end[N_EXPERTS - 1:] // blk).astype(i32)
    ps_lanes = jnp.zeros((1, 128), f32).at[0, N_GROUPS_R:N_GROUPS_R + N_EXPERTS].set(pad_start.astype(f32))
    dest = _dest(rt, ps_lanes)

    y_rows = _moe(block_e, n_used, counts, pad_start, dest[0], dest[1], u2.reshape(n, ROW_TILE, 128),
                  w_gate, w_up, w_down, blk)
    out = _combine(dest[0], dest[1], y_rows.reshape(nblk * blk, ROW_TILE, 128), x1, rt, mod4,
                   ln2_g.reshape(1, D), ln2_b.reshape(1, D), t)
    return out.reshape(bsz, t, D)


def kernel(x, c, ada_w, ada_b, w_in, conv_w, conv_b, dt_bias, a_log, d_skip, ssm_norm_g, fg_bias, att_norm_g, w_out,
           ln1_g, ln1_b, router_g_w, router_g_b, router_e_w, router_e_b, w_gate, w_up, w_down, ln2_g, ln2_b):
    depth = ada_w.shape[0]
    for l in range(depth):
        x = _layer(x, c, ada_w[l], ada_b[l], w_in[l], conv_w[l], conv_b[l], dt_bias[l], a_log[l], d_skip[l],
                   ssm_norm_g[l], fg_bias[l], att_norm_g[l], w_out[l], ln1_g[l], ln1_b[l], router_g_w[l],
                   router_g_b[l], router_e_w[l], router_e_b[l], w_gate[l], w_up[l], w_down[l], ln2_g[l], ln2_b[l])
    return x
```

```python
import functools

import jax
import jax.numpy as jnp
import numpy as np
from jax import lax
from jax.experimental import pallas as pl
from jax.experimental.pallas import tpu as pltpu

f32 = jnp.float32
bf16 = jnp.bfloat16
i32 = jnp.int32

D = 1024
D_SSM = 512
D_ATT = 512
HEAD_DIM = 64
GROUP_W = 256
N_STATE = 128
CONV_K = 4
N_GROUPS_R = 4
EXPERTS_PER_GROUP = 8
N_EXPERTS = 32
D_EXPERT = 512
ALPHA = 2.0 ** 0.25
EPS = 1e-5
NEG = -1e30

SSD_CHUNK = 256
ATT_BLOCK = 256
INPROJ_ROWS = 512
OUTPROJ_ROWS = 512
MOE_ROWS = 256
COMBINE_ROWS = 256
DEST_ROWS = 2048
VMEM_LIMIT = 48 * 1024 * 1024


def _dot(a, b):
    return jnp.dot(a, b, preferred_element_type=f32)


def _dot_nt(a, b):
    return lax.dot_general(a, b, (((1,), (1,)), ((), ())), preferred_element_type=f32)


def _dot_tn(a, b):
    return lax.dot_general(a, b, (((0,), (0,)), ((), ())), preferred_element_type=f32)


def _split3(v):
    hi = v.astype(bf16)
    r1 = v - hi.astype(f32)
    mid = r1.astype(bf16)
    lo = (r1 - mid.astype(f32)).astype(bf16)
    return hi, mid, lo


def _dot_exact_lhs(m, v):
    hi, mid, lo = _split3(v)
    return (_dot(m, hi) + _dot(m, mid)) + _dot(m, lo)


def _dot_exact_rhs(v, m):
    hi, mid, lo = _split3(v)
    return (_dot(hi, m) + _dot(mid, m)) + _dot(lo, m)


ROW_TILE = 8


def _store_row_tiles(ref, val):
    rows = val.shape[0]
    for c in range(ROW_TILE):
        ref[pl.ds(c, rows, stride=ROW_TILE), :] = val[:, c * 128:(c + 1) * 128]


def _load_row_tiles(ref, rows):
    return jnp.concatenate([ref[pl.ds(c, rows, stride=ROW_TILE), :] for c in range(ROW_TILE)], axis=1)


def _softplus(x):
    return jnp.maximum(x, 0.0) + jnp.log1p(jnp.exp(-jnp.abs(x)))


def _silu(x):
    return x * jax.nn.sigmoid(x)


def _ada_kernel(c_ref, w_ref, b_ref, o_ref):
    s = _silu(c_ref[...]).astype(bf16)
    o_ref[0] = _dot(s, w_ref[...].astype(bf16)) + b_ref[0]


def _ada(c, w, b):
    bsz = c.shape[0]
    return pl.pallas_call(
        _ada_kernel,
        grid=(6,),
        in_specs=[pl.BlockSpec((bsz, D), lambda j: (0, 0)),
                  pl.BlockSpec((D, D), lambda j: (0, j)),
                  pl.BlockSpec((1, 1, D), lambda j: (j, 0, 0))],
        out_specs=pl.BlockSpec((1, bsz, D), lambda j: (j, 0, 0)),
        out_shape=jax.ShapeDtypeStruct((6, bsz, D), f32),
        compiler_params=pltpu.CompilerParams(dimension_semantics=("arbitrary",), vmem_limit_bytes=VMEM_LIMIT),
        name="ada",
    )(c, w, b.reshape(6, 1, D))


def _inproj_kernel(x_ref, sc_ref, sh_ref, wm_ref, wsh_ref, wsl_ref, zx_ref, qkv_ref, dtf_ref):
    u = x_ref[0] * (1.0 + sc_ref[0, 0]) + sh_ref[0, 0]
    ub = u.astype(bf16)
    for j in range(3):
        zx_ref[0, :, j * 512:(j + 1) * 512] = _dot(ub, wm_ref[:, j * 512:(j + 1) * 512])
    for j in range(3):
        qkv_ref[0, :, j * 512:(j + 1) * 512] = _dot(ub, wm_ref[:, 1536 + j * 512:1536 + (j + 1) * 512]).astype(bf16)
    ul = (u - ub.astype(f32)).astype(bf16)
    dtf_ref[0] = (_dot(ub, wsh_ref[...]) + _dot(ul, wsh_ref[...])) + _dot(ub, wsl_ref[...])


def _inproj(x, mod4, w_main, ws_hi, ws_lo):
    bsz, t, _ = x.shape
    tm = min(INPROJ_ROWS, t)
    vec = lambda k: pl.BlockSpec((1, 1, 1, D), lambda b, i, k=k: (k, b, 0, 0))
    return pl.pallas_call(
        _inproj_kernel,
        grid=(bsz, t // tm),
        in_specs=[pl.BlockSpec((1, tm, D), lambda b, i: (b, i, 0)),
                  vec(1), vec(0),
                  pl.BlockSpec((D, 3072), lambda b, i: (0, 0)),
                  pl.BlockSpec((D, 128), lambda b, i: (0, 0)),
                  pl.BlockSpec((D, 128), lambda b, i: (0, 0))],
        out_specs=[pl.BlockSpec((1, tm, 1536), lambda b, i: (b, i, 0)),
                   pl.BlockSpec((1, tm, 1536), lambda b, i: (b, i, 0)),
                   pl.BlockSpec((1, tm, 128), lambda b, i: (b, i, 0))],
        out_shape=[jax.ShapeDtypeStruct((bsz, t, 1536), f32),
                   jax.ShapeDtypeStruct((bsz, t, 1536), bf16),
                   jax.ShapeDtypeStruct((bsz, t, 128), f32)],
        compiler_params=pltpu.CompilerParams(dimension_semantics=("parallel", "arbitrary"),
                                             vmem_limit_bytes=VMEM_LIMIT),
        name="inproj",
    )(x, mod4, mod4, w_main, ws_hi, ws_lo)


def _ssd_kernel(z_ref, xs_ref, bc_ref, dtf_ref, cw_ref, cb_ref, pc_ref, pe_ref,
                y_ref, cumc_ref, xcat, state, carry, *, lc):
    j = pl.program_id(1)

    @pl.when(j == 0)
    def _init():
        xcat[0:8, :] = jnp.zeros((8, 2 * D_SSM), f32)
        state[...] = jnp.zeros_like(state)
        carry[...] = jnp.zeros_like(carry)

    xcat[8:8 + lc, 0:512] = xs_ref[0]
    xcat[8:8 + lc, 512:1024] = bc_ref[0]
    acc = cw_ref[0:1, :] * xcat[5:5 + lc, :] + cb_ref[...]
    for k in range(1, CONV_K):
        acc = acc + cw_ref[k:k + 1, :] * xcat[5 + k:5 + k + lc, :]
    xcat[0:8, :] = xcat[lc:lc + 8, :]
    xbc = _silu(acc)
    xs = xbc[:, 0:512]
    bm = xbc[:, 512:768]
    cm = xbc[:, 768:1024]

    dtf = dtf_ref[0]
    lane = lax.broadcasted_iota(i32, (lc, 128), 1)
    dt_c = _softplus(dtf + pc_ref[0:1, :])
    a_c = dt_c * (-jnp.exp(pc_ref[1:2, :]))
    logf = -_softplus(-(dtf + pc_ref[2:3, :]))
    v = jnp.where(lane < 8, a_c, logf)
    r_i = lax.broadcasted_iota(i32, (lc, lc), 0)
    c_i = lax.broadcasted_iota(i32, (lc, lc), 1)
    tri = r_i >= c_i
    tri_b = jnp.where(tri, 1.0, 0.0).astype(bf16)
    cum = _dot_exact_lhs(tri_b, v) + carry[...]
    carry[...] = jnp.where(lane[0:1, :] >= 8, cum[lc - 1:lc, :], 0.0)
    cumc_ref[0] = cum
    cs_t = cum.T[0:8, :]

    e_r = lax.broadcasted_iota(i32, (128, D_SSM), 0)
    e_c = lax.broadcasted_iota(i32, (128, D_SSM), 1)
    expand = jnp.where(jnp.right_shift(e_c, 6) == e_r, 1.0, 0.0).astype(bf16)
    dt_e = _dot_exact_rhs(dt_c, expand)
    cs_e = _dot_exact_rhs(cum, expand)

    xdt = xs * dt_e
    ecs = jnp.exp(cs_e)
    cs_last = cs_e[lc - 1:lc, :]
    dec_st = jnp.exp(cs_last - cs_e)
    lane_g = lax.broadcasted_iota(i32, (1, GROUP_W), 1)
    ys = []
    for g in range(2):
        gs = slice(g * GROUP_W, (g + 1) * GROUP_W)
        bg = bm[:, g * N_STATE:(g + 1) * N_STATE].astype(bf16)
        cg = cm[:, g * N_STATE:(g + 1) * N_STATE].astype(bf16)
        cb = _dot_nt(cg, bg)
        xdt_g = xdt[:, gs]
        xdt_gb = xdt_g.astype(bf16)
        ms, xb = [], []
        for hh in range(4):
            h = g * 4 + hh
            lm = jnp.exp(jnp.where(tri, cum[:, h:h + 1] - cs_t[h:h + 1, :], -jnp.inf))
            ms.append((cb * lm).astype(bf16))
            xb.append(jnp.where(jnp.right_shift(lane_g, 6) == hh, xdt_gb, jnp.zeros_like(xdt_gb)))
        y_diag = _dot(jnp.concatenate(ms, axis=1), jnp.concatenate(xb, axis=0))
        st = state[g]
        y_off = _dot(cg, st.astype(bf16)) * ecs[:, gs]
        upd = _dot_tn(bg, (xdt_g * dec_st[:, gs]).astype(bf16))
        state[g] = st * jnp.exp(cs_last[:, gs]) + upd
        ys.append(y_diag + y_off + xs[:, gs] * pe_ref[2:3, gs])

    outs = []
    for g in range(2):
        gs = slice(g * GROUP_W, (g + 1) * GROUP_W)
        yg = ys[g] * _silu(z_ref[0, :, gs])
        ms_ = jnp.mean(yg * yg, axis=-1, keepdims=True)
        outs.append(yg * lax.rsqrt(ms_ + EPS))
    y_ref[0] = (jnp.concatenate(outs, axis=1) * pe_ref[3:4, :]).astype(bf16)


def _ssd(zx, dtf, conv_w, conv_b, pc, pe):
    bsz, t, _ = zx.shape
    lc = min(SSD_CHUNK, t)
    col = lambda k: pl.BlockSpec((1, lc, 512), lambda b, j, k=k: (b, j, k))
    full = lambda shape: pl.BlockSpec(shape, lambda b, j: (0,) * len(shape))
    return pl.pallas_call(
        functools.partial(_ssd_kernel, lc=lc),
        grid=(bsz, t // lc),
        in_specs=[col(0), col(1), col(2),
                  pl.BlockSpec((1, lc, 128), lambda b, j: (b, j, 0)),
                  full((CONV_K, 2 * D_SSM)), full((1, 2 * D_SSM)), full((8, 128)), full((8, D_SSM))],
        out_specs=[pl.BlockSpec((1, lc, D_SSM), lambda b, j: (b, j, 0)),
                   pl.BlockSpec((1, lc, 128), lambda b, j: (b, j, 0))],
        out_shape=[jax.ShapeDtypeStruct((bsz, t, D_SSM), bf16),
                   jax.ShapeDtypeStruct((bsz, t, 128), f32)],
        scratch_shapes=[pltpu.VMEM((lc + 8, 2 * D_SSM), f32),
                        pltpu.VMEM((2, N_STATE, GROUP_W), f32),
                        pltpu.VMEM((1, 128), f32)],
        compiler_params=pltpu.CompilerParams(dimension_semantics=("parallel", "arbitrary"),
                                             vmem_limit_bytes=VMEM_LIMIT),
        name="ssd",
    )(zx, zx, zx, dtf, conv_w, conv_b, pc, pe)


def _attn_kernel(q_ref, k_ref, v_ref, cc_ref, psel_ref, ng_ref, o_ref, kaug, vt, acc, *, tq, t):
    i = pl.program_id(1)
    nkb = t // tq
    n_heads = D_ATT // HEAD_DIM
    lane = lax.broadcasted_iota(i32, (1, 128), 1)
    lo_half = lane < HEAD_DIM

    @pl.when(i == 0)
    def _build():
        eye = jnp.where(lax.broadcasted_iota(i32, (D_ATT, D_ATT), 0) == lax.broadcasted_iota(i32, (D_ATT, D_ATT), 1),
                        1.0, 0.0).astype(bf16)
        for jb in range(nkb):
            rows = slice(jb * tq, (jb + 1) * tq)
            vt[jb] = _dot_nt(eye, v_ref[0, rows, :]).astype(bf16)
            pieces = jnp.concatenate(_split3(-cc_ref[0, rows, :]), axis=1)
            for p in range(n_heads // 2):
                a = _dot(pieces, psel_ref[p]).astype(bf16)
                kp = k_ref[0, rows, p * 128:(p + 1) * 128]
                kaug[2 * p, rows, :] = jnp.where(lo_half, kp, a)
                kaug[2 * p + 1, rows, :] = jnp.where(lo_half, a, kp)

    ones_hi = jnp.where((lane >= HEAD_DIM) & (lane < HEAD_DIM + 3), 1.0, 0.0).astype(bf16)
    ones_lo = jnp.where(lane < 3, 1.0, 0.0).astype(bf16)
    qa = []
    for p in range(n_heads // 2):
        qp = q_ref[0, :, p * 128:(p + 1) * 128] * jnp.asarray(HEAD_DIM ** -0.5, bf16)
        qa.append(jnp.where(lo_half, qp, ones_hi))
        qa.append(jnp.where(lo_half, ones_lo, qp))
    keep = lax.broadcasted_iota(i32, (tq, tq), 0) <= lax.broadcasted_iota(i32, (tq, tq), 1)
    acc[...] = jnp.zeros_like(acc)

    def block(jb, carry, masked):
        ms, ls = carry
        k0 = pl.multiple_of(jb * tq, tq)
        new_ms, new_ls = [], []
        scores = [_dot_nt(kaug[h, pl.ds(k0, tq), :], qa[h]) for h in range(n_heads)]
        for h in range(n_heads):
            s = scores[h]
            if masked:
                s = jnp.where(keep, s, NEG)
            m_new = jnp.maximum(ms[h], jnp.max(s, axis=0, keepdims=True))
            alpha = jnp.exp(ms[h] - m_new)
            p = jnp.exp(s - m_new)
            new_ls.append(alpha * ls[h] + jnp.sum(p, axis=0, keepdims=True))
            new_ms.append(m_new)
            pv = _dot(vt[jb, h * HEAD_DIM:(h + 1) * HEAD_DIM, :], p.astype(bf16))
            acc[h] = acc[h] * alpha + pv
        return tuple(new_ms), tuple(new_ls)

    init = (tuple(jnp.full((1, tq), NEG, f32) for _ in range(n_heads)),
            tuple(jnp.zeros((1, tq), f32) for _ in range(n_heads)))
    carry = lax.fori_loop(0, i, lambda jb, c: block(jb, c, False), init)
    _, ls = block(i, carry, True)
    out_t = jnp.concatenate([acc[h] * (1.0 / ls[h]) for h in range(n_heads)], axis=0)
    ms_ = jnp.mean(out_t * out_t, axis=0, keepdims=True)
    out_t = out_t * lax.rsqrt(ms_ + EPS)
    o_ref[0] = (out_t.T * ng_ref[...]).astype(bf16)


def _piece_select():
    sel = np.zeros((4, 384, 128), np.float32)
    for pair in range(4):
        for j in range(3):
            sel[pair, j * 128 + 8 + 2 * pair, HEAD_DIM + j] = 1.0
            sel[pair, j * 128 + 8 + 2 * pair + 1, j] = 1.0
    return jnp.asarray(sel, bf16)


def _attn(qkv, cumc, norm_g):
    bsz, t, _ = qkv.shape
    tq = min(ATT_BLOCK, t)
    return pl.pallas_call(
        functools.partial(_attn_kernel, tq=tq, t=t),
        grid=(bsz, t // tq),
        in_specs=[pl.BlockSpec((1, tq, D_ATT), lambda b, i: (b, i, 0)),
                  pl.BlockSpec((1, t, D_ATT), lambda b, i: (b, 0, 1)),
                  pl.BlockSpec((1, t, D_ATT), lambda b, i: (b, 0, 2)),
                  pl.BlockSpec((1, t, 128), lambda b, i: (b, 0, 0)),
                  pl.BlockSpec((4, 384, 128), lambda b, i: (0, 0, 0)),
                  pl.BlockSpec((1, D_ATT), lambda b, i: (0, 0))],
        out_specs=pl.BlockSpec((1, tq, D_ATT), lambda b, i: (b, i, 0)),
        out_shape=jax.ShapeDtypeStruct((bsz, t, D_ATT), bf16),
        scratch_shapes=[pltpu.VMEM((D_ATT // HEAD_DIM, t, 128), bf16),
                        pltpu.VMEM((t // tq, D_ATT, tq), bf16),
                        pltpu.VMEM((D_ATT // HEAD_DIM, HEAD_DIM, tq), f32)],
        compiler_params=pltpu.CompilerParams(dimension_semantics=("parallel", "arbitrary"),
                                             vmem_limit_bytes=VMEM_LIMIT),
        name="attn",
    )(qkv, qkv, qkv, cumc, _piece_select(), norm_g)


def _layer_norm(y, g, b):
    mu = jnp.mean(y, axis=-1, keepdims=True)
    yc = y - mu
    var = jnp.mean(yc * yc, axis=-1, keepdims=True)
    return yc * lax.rsqrt(var + EPS) * g + b


def _outproj_kernel(ys_ref, ya_ref, x_ref, gt_ref, sc_ref, sh_ref, wo_ref, lng_ref, lnb_ref, wrh_ref, wrl_ref, rb_ref,
                    x1_ref, rt_ref, dest_ref, cnt_ref, xrows_hbm,
                    carry, u2t, didx_v, didx_s, cnt_v, cnt_s, zeros, idx_sem, sc_sem, z_sem, *, tm, cap, blk):
    i = pl.program_id(0)
    last = pl.num_programs(0) - 1
    cur = i % 2
    prev = 1 - cur

    def idx_ready(slot):
        return pltpu.make_async_copy(didx_v.at[slot], didx_s.at[slot], idx_sem.at[slot])

    def dispatched(slot):
        return pltpu.make_async_copy(u2t.at[slot], u2t.at[slot], sc_sem.at[slot])

    def dispatch_copy(slot, r, k):
        return pltpu.make_async_copy(u2t.at[slot, pl.ds(r * ROW_TILE, ROW_TILE)], xrows_hbm.at[didx_s[slot, k, r]],
                                     sc_sem.at[slot])

    @pl.when(i == 0)
    def _init():
        carry[...] = jnp.zeros_like(carry)

    @pl.when(i >= 2)
    def _reuse():
        dispatched(cur).wait()
        dispatched(cur).wait()

    def step(dispatch_prev):
        if dispatch_prev:
            idx_ready(prev).wait()
            for r_ in range(tm):
                dispatch_copy(prev, r_, 0).start(priority=0)
                dispatch_copy(prev, r_, 1).start(priority=1)
        h = _dot(ys_ref[...], wo_ref[0:D_SSM, :]) + _dot(ya_ref[...], wo_ref[D_SSM:D, :])
        x1 = _layer_norm(ALPHA * x_ref[...] + (1.0 + gt_ref[0, 0]) * h, lng_ref[...], lnb_ref[...])
        x1_ref[...] = x1
        u2 = x1 * (1.0 + sc_ref[0, 0]) + sh_ref[0, 0]
        _store_row_tiles(u2t.at[cur], u2)

        uh = u2.astype(bf16)
        ul = (u2 - uh.astype(f32)).astype(bf16)
        logits = (_dot(uh, wrh_ref[...]) + _dot(ul, wrh_ref[...])) + _dot(uh, wrl_ref[...]) + rb_ref[...]
        lane = lax.broadcasted_iota(i32, (tm, 128), 1).astype(f32)
        big = jnp.float32(1e9)

        def first_max(vals):
            m = jnp.max(vals, axis=-1, keepdims=True)
            return m, jnp.min(jnp.where(vals == m, lane, big), axis=-1, keepdims=True)

        gl = jnp.where(lane < N_GROUPS_R, logits, NEG)
        gmax, gidx = first_max(gl)
        g_p = 1.0 / jnp.sum(jnp.exp(gl - gmax), axis=-1, keepdims=True)
        lo = N_GROUPS_R + EXPERTS_PER_GROUP * gidx
        el = jnp.where((lane >= lo) & (lane < lo + EXPERTS_PER_GROUP), logits, NEG)
        m1, i1 = first_max(el)
        el2 = jnp.where(lane == i1, NEG, el)
        m2, i2 = first_max(el2)
        r = jnp.exp(m2 - m1)
        w1 = g_p / (1.0 + r)
        w2 = g_p * r / (1.0 + r)

        oh1 = lane == i1
        oh2 = lane == i2
        oh = jnp.where(oh1 | oh2, 1.0, 0.0)
        r_i = lax.broadcasted_iota(i32, (tm, tm), 0)
        c_i = lax.broadcasted_iota(i32, (tm, tm), 1)
        lower = jnp.where(r_i > c_i, 1.0, 0.0).astype(bf16)
        prefix = _dot(lower, oh.astype(bf16)) + carry[...]
        rank1 = jnp.sum(jnp.where(oh1, prefix, 0.0), axis=-1, keepdims=True)
        rank2 = jnp.sum(jnp.where(oh2, prefix, 0.0), axis=-1, keepdims=True)
        carry[...] = carry[...] + jnp.sum(oh, axis=0, keepdims=True)
        cnt_ref[...] = jnp.broadcast_to(carry[...], (8, 128))

        rt = jnp.where(lane == 4, w1, jnp.where(lane == 5, w2, 0.0))
        rt_ref[...] = rt

        d1 = (i1 - N_GROUPS_R) * cap + rank1
        d2 = (i2 - N_GROUPS_R) * cap + rank2
        dd = jnp.where(lane == 0, d1, jnp.where(lane == 1, d2, 0.0)).T[0:8, :].astype(i32)
        dest_ref[...] = dd
        didx_v[cur] = dd
        idx_ready(cur).start()

    pl.when(i == 0)(lambda: step(False))
    pl.when(i > 0)(lambda: step(True))

    @pl.when(i == last)
    def _epilogue():
        idx_ready(cur).wait()

        def issue(r_, c):
            dispatch_copy(cur, r_, 0).start()
            dispatch_copy(cur, r_, 1).start()
            return c
        lax.fori_loop(0, tm, issue, 0)

        cnt_v[...] = jnp.broadcast_to(carry[...], (8, 128)).astype(i32)
        to_smem = pltpu.make_async_copy(cnt_v, cnt_s, z_sem)
        to_smem.start()
        to_smem.wait()
        zeros[...] = jnp.zeros_like(zeros)

        def pad_copy(e):
            return pltpu.make_async_copy(zeros, xrows_hbm.at[pl.ds(e * cap + cnt_s[0, N_GROUPS_R + e], blk)], z_sem)

        def pad_start(e, c):
            pad_copy(e).start()
            return c

        def pad_wait(e, c):
            pad_copy(e).wait()
            return c
        lax.fori_loop(0, N_EXPERTS, pad_start, 0)
        lax.fori_loop(0, N_EXPERTS, pad_wait, 0)

        @pl.when(i > 0)
        def _prev_done():
            dispatched(prev).wait()
            dispatched(prev).wait()
        dispatched(cur).wait()
        dispatched(cur).wait()


def _outproj(y_ssm, y_att, x, mod4, w_out, ln_g, ln_b, wr_hi, wr_lo, rb, t, cap, blk):
    n = x.shape[0]
    tm = min(OUTPROJ_ROWS, t)
    nt = t // tm
    vec = lambda k: pl.BlockSpec((1, 1, 1, D), lambda i, k=k: (k, i // nt, 0, 0))
    full = lambda shape: pl.BlockSpec(shape, lambda i: (0,) * len(shape))
    rows = lambda w: pl.BlockSpec((tm, w), lambda i: (i, 0))
    return pl.pallas_call(
        functools.partial(_outproj_kernel, tm=tm, cap=cap, blk=blk),
        grid=(n // tm,),
        in_specs=[rows(D_SSM), rows(D_ATT), rows(D), vec(2), vec(4), vec(3),
                  full((D, D)), full((1, D)), full((1, D)), full((D, 128)), full((D, 128)), full((1, 128))],
        out_specs=[rows(D), rows(128), pl.BlockSpec((8, tm), lambda i: (0, i)), full((8, 128)),
                   pl.BlockSpec(memory_space=pl.ANY)],
        out_shape=[jax.ShapeDtypeStruct((n, D), f32), jax.ShapeDtypeStruct((n, 128), f32),
                   jax.ShapeDtypeStruct((8, n), i32), jax.ShapeDtypeStruct((8, 128), f32),
                   jax.ShapeDtypeStruct((N_EXPERTS * cap, ROW_TILE, 128), f32)],
        scratch_shapes=[pltpu.VMEM((1, 128), f32),
                        pltpu.VMEM((2, tm * ROW_TILE, 128), f32),
                        pltpu.VMEM((2, 8, tm), i32), pltpu.SMEM((2, 8, tm), i32),
                        pltpu.VMEM((8, 128), i32), pltpu.SMEM((8, 128), i32),
                        pltpu.VMEM((blk, ROW_TILE, 128), f32),
                        pltpu.SemaphoreType.DMA((2,)), pltpu.SemaphoreType.DMA((2,)), pltpu.SemaphoreType.DMA(())],
        compiler_params=pltpu.CompilerParams(dimension_semantics=("arbitrary",), vmem_limit_bytes=VMEM_LIMIT),
        name="outproj",
    )(y_ssm, y_att, x, mod4, mod4, mod4, w_out, ln_g, ln_b, wr_hi, wr_lo, rb)


def _tile_copy(src_hbm, row, buf, slot, sem):
    return pltpu.make_async_copy(src_hbm.at[row], buf.at[pl.ds(slot * ROW_TILE, ROW_TILE)], sem)


def _moe_kernel(be_ref, br_ref, nu_ref, x_ref, wg_ref, wu_ref, wd_ref, y_ref, wgb, wub, wdb, *, blk):
    i = pl.program_id(0)
    used = i < nu_ref[0]

    @pl.when(jnp.logical_and(used, jnp.logical_or(i == 0, be_ref[i] != be_ref[jnp.maximum(i - 1, 0)])))
    def _cast():
        wgb[...] = wg_ref[0].astype(bf16)
        wub[...] = wu_ref[0].astype(bf16)
        wdb[...] = wd_ref[0].astype(bf16)

    @pl.when(used)
    def _work():
        xb = _load_row_tiles(x_ref, blk).astype(bf16)
        hid = (_silu(_dot(xb, wgb[...])) * _dot(xb, wub[...])).astype(bf16)
        _store_row_tiles(y_ref, _dot(hid, wdb[...]))


def _moe(block_e, block_row, n_used, x_rows, w_gate, w_up, w_down, blk):
    nblk = block_e.shape[0]
    wspec = lambda a, b: pl.BlockSpec((1, a, b), lambda i, be, br, nu: (be[i], 0, 0))
    rspec = pl.BlockSpec((blk * ROW_TILE, 128), lambda i, be, br, nu: (br[i], 0))
    grid_spec = pltpu.PrefetchScalarGridSpec(
        num_scalar_prefetch=3,
        grid=(nblk,),
        in_specs=[rspec, wspec(D, D_EXPERT), wspec(D, D_EXPERT), wspec(D_EXPERT, D)],
        out_specs=rspec,
        scratch_shapes=[pltpu.VMEM((D, D_EXPERT), bf16), pltpu.VMEM((D, D_EXPERT), bf16),
                        pltpu.VMEM((D_EXPERT, D), bf16)],
    )
    return pl.pallas_call(
        functools.partial(_moe_kernel, blk=blk),
        grid_spec=grid_spec,
        out_shape=jax.ShapeDtypeStruct(x_rows.shape, f32),
        compiler_params=pltpu.CompilerParams(dimension_semantics=("arbitrary",), vmem_limit_bytes=VMEM_LIMIT),
        name="moe",
    )(block_e, block_row, n_used, x_rows, w_gate, w_up, w_down)


def _combine_kernel(d1_ref, d2_ref, y_hbm, x1_ref, rt_ref, gt_ref, lng_ref, lnb_ref, o_ref, buf, sem, *, tm):
    i = pl.program_id(0)
    last = pl.num_programs(0) - 1

    @pl.when(i == 0)
    def _prologue():
        for s in range(2):
            def issue(r, c, s=s):
                base = jnp.minimum(s, last) * tm
                _tile_copy(y_hbm, d1_ref[base + r], buf.at[s, 0], r, sem.at[s]).start()
                _tile_copy(y_hbm, d2_ref[base + r], buf.at[s, 1], r, sem.at[s]).start()
                return c
            lax.fori_loop(0, tm, issue, 0)

    def rows_ready(slot):
        return pltpu.make_async_copy(buf.at[slot], buf.at[slot], sem.at[slot])

    slot = i % 3
    nxt = (i + 2) % 3
    base = jnp.minimum(i + 2, last) * tm
    rows_ready(slot).wait()
    for r in range(tm):
        _tile_copy(y_hbm, d1_ref[base + r], buf.at[nxt, 0], r, sem.at[nxt]).start(priority=0)
        _tile_copy(y_hbm, d2_ref[base + r], buf.at[nxt, 1], r, sem.at[nxt]).start(priority=1)
    rt = rt_ref[...]
    moe = rt[:, 4:5] * _load_row_tiles(buf.at[slot, 0], tm) + rt[:, 5:6] * _load_row_tiles(buf.at[slot, 1], tm)
    y = ALPHA * x1_ref[...] + (1.0 + gt_ref[0, 0]) * moe
    o_ref[...] = _layer_norm(y, lng_ref[...], lnb_ref[...])

    @pl.when(i == last)
    def _drain():
        rows_ready((i + 1) % 3).wait()
        rows_ready((i + 2) % 3).wait()


def _combine(dest1, dest2, y_rows, x1, rt, mod4, ln_g, ln_b, t):
    n = x1.shape[0]
    tm = min(COMBINE_ROWS, t)
    nt = t // tm
    full = lambda shape: pl.BlockSpec(shape, lambda i, d1, d2: (0,) * len(shape))
    rows = lambda w: pl.BlockSpec((tm, w), lambda i, d1, d2: (i, 0))
    grid_spec = pltpu.PrefetchScalarGridSpec(
        num_scalar_prefetch=2,
        grid=(n // tm,),
        in_specs=[pl.BlockSpec(memory_space=pl.ANY), rows(D), rows(128),
                  pl.BlockSpec((1, 1, 1, D), lambda i, d1, d2: (5, i // nt, 0, 0)),
                  full((1, D)), full((1, D))],
        out_specs=rows(D),
        scratch_shapes=[pltpu.VMEM((3, 2, tm * ROW_TILE, 128), f32), pltpu.SemaphoreType.DMA((3,))],
    )
    return pl.pallas_call(
        functools.partial(_combine_kernel, tm=tm),
        grid_spec=grid_spec,
        out_shape=jax.ShapeDtypeStruct((n, D), f32),
        compiler_params=pltpu.CompilerParams(dimension_semantics=("arbitrary",), vmem_limit_bytes=VMEM_LIMIT),
        name="combine",
    )(dest1, dest2, y_rows, x1, rt, mod4, ln_g, ln_b)


def _hi_lo(w):
    hi = w.astype(bf16)
    return hi, (w - hi.astype(f32)).astype(bf16)


def _pad_lanes(v, offset, width):
    return jnp.zeros((width,), f32).at[offset:offset + v.shape[0]].set(v)


def _layer(x, c, ada_w, ada_b, w_in, conv_w, conv_b, dt_bias, a_log, d_skip, ssm_norm_g, fg_bias, att_norm_g,
           w_out, ln1_g, ln1_b, router_g_w, router_g_b, router_e_w, router_e_b, w_gate, w_up, w_down, ln2_g, ln2_b):
    bsz, t, _ = x.shape
    n = bsz * t

    mod4 = _ada(c, ada_w, ada_b).reshape(6, bsz, 1, D)

    w_main = jnp.concatenate([w_in[:, 0:1536], w_in[:, 1544:3080]], axis=1).astype(bf16)
    w_small = jnp.concatenate([w_in[:, 1536:1544], w_in[:, 3080:3088], jnp.zeros((D, 112), f32)], axis=1)
    ws_hi, ws_lo = _hi_lo(w_small)
    zx, qkv, dtf = _inproj(x, mod4, w_main, ws_hi, ws_lo)

    pc = jnp.stack([_pad_lanes(dt_bias, 0, 128), _pad_lanes(a_log, 0, 128), _pad_lanes(fg_bias, 8, 128)]
                   + [jnp.zeros((128,), f32)] * 5)
    rep = lambda v: jnp.repeat(v, HEAD_DIM)
    pe = jnp.stack([rep(dt_bias), rep(a_log), rep(d_skip), ssm_norm_g] + [jnp.zeros((D_SSM,), f32)] * 4)
    y_ssm, cumc = _ssd(zx, dtf, conv_w, conv_b.reshape(1, -1), pc, pe)

    y_att = _attn(qkv, cumc, att_norm_g.reshape(1, -1))

    wr = jnp.concatenate([router_g_w, router_e_w, jnp.zeros((D, 128 - N_GROUPS_R - N_EXPERTS), f32)], axis=1)
    wr_hi, wr_lo = _hi_lo(wr)
    rb = jnp.concatenate([router_g_b, router_e_b, jnp.zeros((128 - N_GROUPS_R - N_EXPERTS,), f32)]).reshape(1, 128)
    blk = MOE_ROWS
    cap = n + blk
    x1, rt, dest, cnt, x_rows = _outproj(y_ssm.reshape(n, D_SSM), y_att.reshape(n, D_ATT), x.reshape(n, D), mod4,
                                         w_out.astype(bf16), ln1_g.reshape(1, D), ln1_b.reshape(1, D),
                                         wr_hi, wr_lo, rb, t, cap, blk)

    counts = cnt[0, N_GROUPS_R:N_GROUPS_R + N_EXPERTS].astype(i32)
    nb = (counts + blk - 1) // blk
    nb_end = jnp.cumsum(nb)
    n_used = nb_end[N_EXPERTS - 1:]
    nblk = (2 * n) // blk + N_EXPERTS
    step = jnp.minimum(jnp.arange(nblk, dtype=i32), n_used[0] - 1)
    owner = (nb_end[None, :] <= step[:, None]).astype(i32)
    block_e = jnp.sum(owner, axis=1)
    first = jnp.sum(owner * nb[None, :], axis=1)
    block_row = block_e * (cap // blk) + (step - first)

    y_rows = _moe(block_e, block_row, n_used, x_rows.reshape(N_EXPERTS * cap * ROW_TILE, 128),
                  w_gate, w_up, w_down, blk)
    out = _combine(dest[0], dest[1], y_rows.reshape(N_EXPERTS * cap, ROW_TILE, 128), x1, rt, mod4,
                   ln2_g.reshape(1, D), ln2_b.reshape(1, D), t)
    return out.reshape(bsz, t, D)


def kernel(x, c, ada_w, ada_b, w_in, conv_w, conv_b, dt_bias, a_log, d_skip, ssm_norm_g, fg_bias, att_norm_g, w_out,
           ln1_g, ln1_b, router_g_w, router_g_b, router_e_w, router_e_b, w_gate, w_up, w_down, ln2_g, ln2_b):
    depth = ada_w.shape[0]
    for l in range(depth):
        x = _layer(x, c, ada_w[l], ada_b[l], w_in[l], conv_w[l], conv_b[l], dt_bias[l], a_log[l], d_skip[l],
                   ssm_norm_g[l], fg_bias[l], att_norm_g[l], w_out[l], ln1_g[l], ln1_b[l], router_g_w[l],
                   router_g_b[l], router_e_w[l], router_e_b[l], w_gate[l], w_up[l], w_down[l], ln2_g[l], ln2_b[l])
    return x
```

```python
import functools

import jax
import jax.numpy as jnp
import numpy as np
from jax import lax
from jax.experimental import pallas as pl
from jax.experimental.pallas import tpu as pltpu

f32 = jnp.float32
bf16 = jnp.bfloat16
i32 = jnp.int32

D = 1024
D_SSM = 512
D_ATT = 512
HEAD_DIM = 64
GROUP_W = 256
N_STATE = 128
CONV_K = 4
N_GROUPS_R = 4
EXPERTS_PER_GROUP = 8
N_EXPERTS = 32
D_EXPERT = 512
ALPHA = 2.0 ** 0.25
EPS = 1e-5
NEG = -1e30
LOG2E = 1.4426950408889634
QK_SCALE = HEAD_DIM ** -0.5 * LOG2E
V_ROWS = 80

SSD_CHUNK = 256
ATT_BLOCK = 256
INPROJ_ROWS = 512
OUTPROJ_ROWS = 512
MOE_ROWS = 256
COMBINE_ROWS = 256
DEST_ROWS = 2048
VMEM_LIMIT = 48 * 1024 * 1024


def _dot(a, b):
    return jnp.dot(a, b, preferred_element_type=f32)


def _dot_nt(a, b):
    return lax.dot_general(a, b, (((1,), (1,)), ((), ())), preferred_element_type=f32)


def _dot_tn(a, b):
    return lax.dot_general(a, b, (((0,), (0,)), ((), ())), preferred_element_type=f32)


def _split3(v):
    hi = v.astype(bf16)
    r1 = v - hi.astype(f32)
    mid = r1.astype(bf16)
    lo = (r1 - mid.astype(f32)).astype(bf16)
    return hi, mid, lo


def _dot_exact_lhs(m, v):
    hi, mid, lo = _split3(v)
    return (_dot(m, hi) + _dot(m, mid)) + _dot(m, lo)


def _dot_exact_rhs(v, m):
    hi, mid, lo = _split3(v)
    return (_dot(hi, m) + _dot(mid, m)) + _dot(lo, m)


ROW_TILE = 8


def _store_row_tiles(ref, val):
    rows = val.shape[0]
    for c in range(ROW_TILE):
        ref[pl.ds(c, rows, stride=ROW_TILE), :] = val[:, c * 128:(c + 1) * 128]


def _load_row_tiles(ref, rows):
    return jnp.concatenate([ref[pl.ds(c, rows, stride=ROW_TILE), :] for c in range(ROW_TILE)], axis=1)


def _softplus(x):
    return jnp.maximum(x, 0.0) + jnp.log1p(jnp.exp(-jnp.abs(x)))


def _silu(x):
    return x * jax.nn.sigmoid(x)


def _ada_kernel(c_ref, w_ref, b_ref, o_ref):
    s = _silu(c_ref[...]).astype(bf16)
    o_ref[0] = _dot(s, w_ref[...].astype(bf16)) + b_ref[0]


def _ada(c, w, b):
    bsz = c.shape[0]
    return pl.pallas_call(
        _ada_kernel,
        grid=(6,),
        in_specs=[pl.BlockSpec((bsz, D), lambda j: (0, 0)),
                  pl.BlockSpec((D, D), lambda j: (0, j)),
                  pl.BlockSpec((1, 1, D), lambda j: (j, 0, 0))],
        out_specs=pl.BlockSpec((1, bsz, D), lambda j: (j, 0, 0)),
        out_shape=jax.ShapeDtypeStruct((6, bsz, D), f32),
        compiler_params=pltpu.CompilerParams(dimension_semantics=("arbitrary",), vmem_limit_bytes=VMEM_LIMIT),
        name="ada",
    )(c, w, b.reshape(6, 1, D))


def _inproj_kernel(x_ref, sc_ref, sh_ref, wm_ref, wsh_ref, wsl_ref, zx_ref, qkv_ref, dtf_ref):
    u = x_ref[0] * (1.0 + sc_ref[0, 0]) + sh_ref[0, 0]
    ub = u.astype(bf16)
    for j in range(3):
        zx_ref[0, :, j * 512:(j + 1) * 512] = _dot(ub, wm_ref[:, j * 512:(j + 1) * 512])
    for j, scale in enumerate((QK_SCALE, 1.0, 1.0)):
        qkv_ref[0, :, j * 512:(j + 1) * 512] = (
            _dot(ub, wm_ref[:, 1536 + j * 512:1536 + (j + 1) * 512]) * scale).astype(bf16)
    ul = (u - ub.astype(f32)).astype(bf16)
    dtf_ref[0] = (_dot(ub, wsh_ref[...]) + _dot(ul, wsh_ref[...])) + _dot(ub, wsl_ref[...])


def _inproj(x, mod4, w_main, ws_hi, ws_lo):
    bsz, t, _ = x.shape
    tm = min(INPROJ_ROWS, t)
    vec = lambda k: pl.BlockSpec((1, 1, 1, D), lambda b, i, k=k: (k, b, 0, 0))
    return pl.pallas_call(
        _inproj_kernel,
        grid=(bsz, t // tm),
        in_specs=[pl.BlockSpec((1, tm, D), lambda b, i: (b, i, 0)),
                  vec(1), vec(0),
                  pl.BlockSpec((D, 3072), lambda b, i: (0, 0)),
                  pl.BlockSpec((D, 128), lambda b, i: (0, 0)),
                  pl.BlockSpec((D, 128), lambda b, i: (0, 0))],
        out_specs=[pl.BlockSpec((1, tm, 1536), lambda b, i: (b, i, 0)),
                   pl.BlockSpec((1, tm, 1536), lambda b, i: (b, i, 0)),
                   pl.BlockSpec((1, tm, 128), lambda b, i: (b, i, 0))],
        out_shape=[jax.ShapeDtypeStruct((bsz, t, 1536), f32),
                   jax.ShapeDtypeStruct((bsz, t, 1536), bf16),
                   jax.ShapeDtypeStruct((bsz, t, 128), f32)],
        compiler_params=pltpu.CompilerParams(dimension_semantics=("parallel", "arbitrary"),
                                             vmem_limit_bytes=VMEM_LIMIT),
        name="inproj",
    )(x, mod4, mod4, w_main, ws_hi, ws_lo)


def _ssd_kernel(z_ref, xs_ref, bc_ref, dtf_ref, cw_ref, cb_ref, pc_ref, pe_ref,
                y_ref, cumc_ref, xcat, state, carry, *, lc):
    j = pl.program_id(1)

    @pl.when(j == 0)
    def _init():
        xcat[0:8, :] = jnp.zeros((8, 2 * D_SSM), f32)
        state[...] = jnp.zeros_like(state)
        carry[...] = jnp.zeros_like(carry)

    xcat[8:8 + lc, 0:512] = xs_ref[0]
    xcat[8:8 + lc, 512:1024] = bc_ref[0]
    acc = cw_ref[0:1, :] * xcat[5:5 + lc, :] + cb_ref[...]
    for k in range(1, CONV_K):
        acc = acc + cw_ref[k:k + 1, :] * xcat[5 + k:5 + k + lc, :]
    xcat[0:8, :] = xcat[lc:lc + 8, :]
    xbc = _silu(acc)
    xs = xbc[:, 0:512]
    bm = xbc[:, 512:768]
    cm = xbc[:, 768:1024]

    dtf = dtf_ref[0]
    lane = lax.broadcasted_iota(i32, (lc, 128), 1)
    dt_c = _softplus(dtf + pc_ref[0:1, :])
    a_c = dt_c * (-jnp.exp(pc_ref[1:2, :]))
    logf = -_softplus(-(dtf + pc_ref[2:3, :]))
    v = jnp.where(lane < 8, a_c, logf)
    r_i = lax.broadcasted_iota(i32, (lc, lc), 0)
    c_i = lax.broadcasted_iota(i32, (lc, lc), 1)
    tri = r_i >= c_i
    tri_b = jnp.where(tri, 1.0, 0.0).astype(bf16)
    cum = _dot_exact_lhs(tri_b, v) + carry[...]
    carry[...] = jnp.where(lane[0:1, :] >= 8, cum[lc - 1:lc, :], 0.0)
    cumc_ref[0] = cum
    cs_t = cum.T[0:8, :]

    e_r = lax.broadcasted_iota(i32, (128, D_SSM), 0)
    e_c = lax.broadcasted_iota(i32, (128, D_SSM), 1)
    expand = jnp.where(jnp.right_shift(e_c, 6) == e_r, 1.0, 0.0).astype(bf16)
    dt_e = _dot_exact_rhs(dt_c, expand)
    cs_e = _dot_exact_rhs(cum, expand)

    xdt = xs * dt_e
    ecs = jnp.exp(cs_e)
    cs_last = cs_e[lc - 1:lc, :]
    dec_st = jnp.exp(cs_last - cs_e)
    lane_g = lax.broadcasted_iota(i32, (1, GROUP_W), 1)
    ys = []
    for g in range(2):
        gs = slice(g * GROUP_W, (g + 1) * GROUP_W)
        bg = bm[:, g * N_STATE:(g + 1) * N_STATE].astype(bf16)
        cg = cm[:, g * N_STATE:(g + 1) * N_STATE].astype(bf16)
        cb = _dot_nt(cg, bg)
        xdt_g = xdt[:, gs]
        xdt_gb = xdt_g.astype(bf16)
        ms, xb = [], []
        for hh in range(4):
            h = g * 4 + hh
            lm = jnp.exp(jnp.where(tri, cum[:, h:h + 1] - cs_t[h:h + 1, :], -jnp.inf))
            ms.append((cb * lm).astype(bf16))
            xb.append(jnp.where(jnp.right_shift(lane_g, 6) == hh, xdt_gb, jnp.zeros_like(xdt_gb)))
        y_diag = _dot(jnp.concatenate(ms, axis=1), jnp.concatenate(xb, axis=0))
        st = state[g]
        y_off = _dot(cg, st.astype(bf16)) * ecs[:, gs]
        upd = _dot_tn(bg, (xdt_g * dec_st[:, gs]).astype(bf16))
        state[g] = st * jnp.exp(cs_last[:, gs]) + upd
        ys.append(y_diag + y_off + xs[:, gs] * pe_ref[2:3, gs])

    outs = []
    for g in range(2):
        gs = slice(g * GROUP_W, (g + 1) * GROUP_W)
        yg = ys[g] * _silu(z_ref[0, :, gs])
        ms_ = jnp.mean(yg * yg, axis=-1, keepdims=True)
        outs.append(yg * lax.rsqrt(ms_ + EPS))
    y_ref[0] = (jnp.concatenate(outs, axis=1) * pe_ref[3:4, :]).astype(bf16)


def _ssd(zx, dtf, conv_w, conv_b, pc, pe):
    bsz, t, _ = zx.shape
    lc = min(SSD_CHUNK, t)
    col = lambda k: pl.BlockSpec((1, lc, 512), lambda b, j, k=k: (b, j, k))
    full = lambda shape: pl.BlockSpec(shape, lambda b, j: (0,) * len(shape))
    return pl.pallas_call(
        functools.partial(_ssd_kernel, lc=lc),
        grid=(bsz, t // lc),
        in_specs=[col(0), col(1), col(2),
                  pl.BlockSpec((1, lc, 128), lambda b, j: (b, j, 0)),
                  full((CONV_K, 2 * D_SSM)), full((1, 2 * D_SSM)), full((8, 128)), full((8, D_SSM))],
        out_specs=[pl.BlockSpec((1, lc, D_SSM), lambda b, j: (b, j, 0)),
                   pl.BlockSpec((1, lc, 128), lambda b, j: (b, j, 0))],
        out_shape=[jax.ShapeDtypeStruct((bsz, t, D_SSM), bf16),
                   jax.ShapeDtypeStruct((bsz, t, 128), f32)],
        scratch_shapes=[pltpu.VMEM((lc + 8, 2 * D_SSM), f32),
                        pltpu.VMEM((2, N_STATE, GROUP_W), f32),
                        pltpu.VMEM((1, 128), f32)],
        compiler_params=pltpu.CompilerParams(dimension_semantics=("parallel", "arbitrary"),
                                             vmem_limit_bytes=VMEM_LIMIT),
        name="ssd",
    )(zx, zx, zx, dtf, conv_w, conv_b, pc, pe)


def _attn_kernel(q_ref, k_ref, v_ref, cc_ref, psel_ref, ng_ref, o_ref, kaug, vt, acc, *, tq, t):
    i = pl.program_id(1)
    nkb = t // tq
    n_heads = D_ATT // HEAD_DIM
    lane = lax.broadcasted_iota(i32, (1, 128), 1)
    lo_half = lane < HEAD_DIM

    @pl.when(i == 0)
    def _build():
        eye = jnp.where(lax.broadcasted_iota(i32, (D_ATT, D_ATT), 0) == lax.broadcasted_iota(i32, (D_ATT, D_ATT), 1),
                        1.0, 0.0).astype(bf16)
        ones_rows = jnp.where(lax.broadcasted_iota(i32, (V_ROWS - HEAD_DIM, tq), 0) == 0, 1.0, 0.0).astype(bf16)
        for jb in range(nkb):
            rows = slice(jb * tq, (jb + 1) * tq)
            v_t = _dot_nt(eye, v_ref[0, rows, :]).astype(bf16)
            for h in range(n_heads):
                vt[jb, h * V_ROWS:h * V_ROWS + HEAD_DIM, :] = v_t[h * HEAD_DIM:(h + 1) * HEAD_DIM, :]
                vt[jb, h * V_ROWS + HEAD_DIM:(h + 1) * V_ROWS, :] = ones_rows
            pieces = jnp.concatenate(_split3(cc_ref[0, rows, :] * (-LOG2E)), axis=1)
            for p in range(n_heads // 2):
                a = _dot(pieces, psel_ref[p]).astype(bf16)
                kp = k_ref[0, rows, p * 128:(p + 1) * 128]
                kaug[2 * p, rows, :] = jnp.where(lo_half, kp, a)
                kaug[2 * p + 1, rows, :] = jnp.where(lo_half, a, kp)

    ones_hi = jnp.where((lane >= HEAD_DIM) & (lane < HEAD_DIM + 3), 1.0, 0.0).astype(bf16)
    ones_lo = jnp.where(lane < 3, 1.0, 0.0).astype(bf16)
    qa = []
    for p in range(n_heads // 2):
        qp = q_ref[0, :, p * 128:(p + 1) * 128]
        qa.append(jnp.where(lo_half, qp, ones_hi))
        qa.append(jnp.where(lo_half, ones_lo, qp))
    keep = lax.broadcasted_iota(i32, (tq, tq), 0) <= lax.broadcasted_iota(i32, (tq, tq), 1)
    acc[...] = jnp.zeros_like(acc)

    def block(jb, ms, masked):
        k0 = pl.multiple_of(jb * tq, tq)
        new_ms = []
        scores = [_dot_nt(kaug[h, pl.ds(k0, tq), :], qa[h]) for h in range(n_heads)]
        for h in range(n_heads):
            s = scores[h]
            if masked:
                s = jnp.where(keep, s, NEG)
            m_new = jnp.maximum(ms[h], jnp.max(s, axis=0, keepdims=True))
            alpha = jnp.exp2(ms[h] - m_new)
            p = jnp.exp2(s - m_new).astype(bf16)
            new_ms.append(m_new)
            acc[h] = acc[h] * alpha + _dot(vt[jb, h * V_ROWS:(h + 1) * V_ROWS, :], p)
        return tuple(new_ms)

    init = tuple(jnp.full((1, tq), NEG, f32) for _ in range(n_heads))
    ms = lax.fori_loop(0, i, lambda jb, c: block(jb, c, False), init)
    block(i, ms, True)
    out_t = jnp.concatenate([acc[h, 0:HEAD_DIM, :] * (1.0 / acc[h, HEAD_DIM:HEAD_DIM + 1, :]) for h in range(n_heads)],
                            axis=0)
    ms_ = jnp.mean(out_t * out_t, axis=0, keepdims=True)
    out_t = out_t * lax.rsqrt(ms_ + EPS)
    o_ref[0] = (out_t.T * ng_ref[...]).astype(bf16)


def _piece_select():
    sel = np.zeros((4, 384, 128), np.float32)
    for pair in range(4):
        for j in range(3):
            sel[pair, j * 128 + 8 + 2 * pair, HEAD_DIM + j] = 1.0
            sel[pair, j * 128 + 8 + 2 * pair + 1, j] = 1.0
    return jnp.asarray(sel, bf16)


def _attn(qkv, cumc, norm_g):
    bsz, t, _ = qkv.shape
    tq = min(ATT_BLOCK, t)
    return pl.pallas_call(
        functools.partial(_attn_kernel, tq=tq, t=t),
        grid=(bsz, t // tq),
        in_specs=[pl.BlockSpec((1, tq, D_ATT), lambda b, i: (b, i, 0)),
                  pl.BlockSpec((1, t, D_ATT), lambda b, i: (b, 0, 1)),
                  pl.BlockSpec((1, t, D_ATT), lambda b, i: (b, 0, 2)),
                  pl.BlockSpec((1, t, 128), lambda b, i: (b, 0, 0)),
                  pl.BlockSpec((4, 384, 128), lambda b, i: (0, 0, 0)),
                  pl.BlockSpec((1, D_ATT), lambda b, i: (0, 0))],
        out_specs=pl.BlockSpec((1, tq, D_ATT), lambda b, i: (b, i, 0)),
        out_shape=jax.ShapeDtypeStruct((bsz, t, D_ATT), bf16),
        scratch_shapes=[pltpu.VMEM((D_ATT // HEAD_DIM, t, 128), bf16),
                        pltpu.VMEM((t // tq, (D_ATT // HEAD_DIM) * V_ROWS, tq), bf16),
                        pltpu.VMEM((D_ATT // HEAD_DIM, V_ROWS, tq), f32)],
        compiler_params=pltpu.CompilerParams(dimension_semantics=("parallel", "arbitrary"),
                                             vmem_limit_bytes=VMEM_LIMIT),
        name="attn",
    )(qkv, qkv, qkv, cumc, _piece_select(), norm_g)


def _layer_norm(y, g, b):
    mu = jnp.mean(y, axis=-1, keepdims=True)
    yc = y - mu
    var = jnp.mean(yc * yc, axis=-1, keepdims=True)
    return yc * lax.rsqrt(var + EPS) * g + b


def _outproj_kernel(ys_ref, ya_ref, x_ref, gt_ref, sc_ref, sh_ref, wo_ref, lng_ref, lnb_ref, wrh_ref, wrl_ref, rb_ref,
                    x1_ref, rt_ref, dest_ref, cnt_ref, xrows_hbm,
                    carry, u2t, didx_v, didx_s, cnt_v, cnt_s, zeros, idx_sem, sc_sem, z_sem, *, tm, cap, blk):
    i = pl.program_id(0)
    last = pl.num_programs(0) - 1
    cur = i % 2
    prev = 1 - cur

    def idx_ready(slot):
        return pltpu.make_async_copy(didx_v.at[slot], didx_s.at[slot], idx_sem.at[slot])

    def dispatched(slot):
        return pltpu.make_async_copy(u2t.at[slot], u2t.at[slot], sc_sem.at[slot])

    def dispatch_copy(slot, r, k):
        return pltpu.make_async_copy(u2t.at[slot, pl.ds(r * ROW_TILE, ROW_TILE)], xrows_hbm.at[didx_s[slot, k, r]],
                                     sc_sem.at[slot])

    @pl.when(i == 0)
    def _init():
        carry[...] = jnp.zeros_like(carry)

    @pl.when(i >= 2)
    def _reuse():
        dispatched(cur).wait()
        dispatched(cur).wait()

    def step(dispatch_prev):
        if dispatch_prev:
            idx_ready(prev).wait()
            for r_ in range(tm):
                dispatch_copy(prev, r_, 0).start(priority=0)
                dispatch_copy(prev, r_, 1).start(priority=1)
        h = _dot(ys_ref[...], wo_ref[0:D_SSM, :]) + _dot(ya_ref[...], wo_ref[D_SSM:D, :])
        x1 = _layer_norm(ALPHA * x_ref[...] + (1.0 + gt_ref[0, 0]) * h, lng_ref[...], lnb_ref[...])
        x1_ref[...] = x1
        u2 = x1 * (1.0 + sc_ref[0, 0]) + sh_ref[0, 0]
        _store_row_tiles(u2t.at[cur], u2)

        uh = u2.astype(bf16)
        ul = (u2 - uh.astype(f32)).astype(bf16)
        logits = (_dot(uh, wrh_ref[...]) + _dot(ul, wrh_ref[...])) + _dot(uh, wrl_ref[...]) + rb_ref[...]
        lane = lax.broadcasted_iota(i32, (tm, 128), 1).astype(f32)
        big = jnp.float32(1e9)

        def first_max(vals):
            m = jnp.max(vals, axis=-1, keepdims=True)
            return m, jnp.min(jnp.where(vals == m, lane, big), axis=-1, keepdims=True)

        gl = jnp.where(lane < N_GROUPS_R, logits, NEG)
        gmax, gidx = first_max(gl)
        g_p = 1.0 / jnp.sum(jnp.exp(gl - gmax), axis=-1, keepdims=True)
        lo = N_GROUPS_R + EXPERTS_PER_GROUP * gidx
        el = jnp.where((lane >= lo) & (lane < lo + EXPERTS_PER_GROUP), logits, NEG)
        m1, i1 = first_max(el)
        el2 = jnp.where(lane == i1, NEG, el)
        m2, i2 = first_max(el2)
        r = jnp.exp(m2 - m1)
        w1 = g_p / (1.0 + r)
        w2 = g_p * r / (1.0 + r)

        oh1 = lane == i1
        oh2 = lane == i2
        oh = jnp.where(oh1 | oh2, 1.0, 0.0)
        r_i = lax.broadcasted_iota(i32, (tm, tm), 0)
        c_i = lax.broadcasted_iota(i32, (tm, tm), 1)
        lower = jnp.where(r_i > c_i, 1.0, 0.0).astype(bf16)
        prefix = _dot(lower, oh.astype(bf16)) + carry[...]
        rank1 = jnp.sum(jnp.where(oh1, prefix, 0.0), axis=-1, keepdims=True)
        rank2 = jnp.sum(jnp.where(oh2, prefix, 0.0), axis=-1, keepdims=True)
        carry[...] = carry[...] + jnp.sum(oh, axis=0, keepdims=True)
        cnt_ref[...] = jnp.broadcast_to(carry[...], (8, 128))

        rt = jnp.where(lane == 4, w1, jnp.where(lane == 5, w2, 0.0))
        rt_ref[...] = rt

        d1 = (i1 - N_GROUPS_R) * cap + rank1
        d2 = (i2 - N_GROUPS_R) * cap + rank2
        dd = jnp.where(lane == 0, d1, jnp.where(lane == 1, d2, 0.0)).T[0:8, :].astype(i32)
        dest_ref[...] = dd
        didx_v[cur] = dd
        idx_ready(cur).start()

    pl.when(i == 0)(lambda: step(False))
    pl.when(i > 0)(lambda: step(True))

    @pl.when(i == last)
    def _epilogue():
        idx_ready(cur).wait()

        def issue(r_, c):
            dispatch_copy(cur, r_, 0).start()
            dispatch_copy(cur, r_, 1).start()
            return c
        lax.fori_loop(0, tm, issue, 0)

        cnt_v[...] = jnp.broadcast_to(carry[...], (8, 128)).astype(i32)
        to_smem = pltpu.make_async_copy(cnt_v, cnt_s, z_sem)
        to_smem.start()
        to_smem.wait()
        zeros[...] = jnp.zeros_like(zeros)

        def pad_copy(e):
            return pltpu.make_async_copy(zeros, xrows_hbm.at[pl.ds(e * cap + cnt_s[0, N_GROUPS_R + e], blk)], z_sem)

        def pad_start(e, c):
            pad_copy(e).start()
            return c

        def pad_wait(e, c):
            pad_copy(e).wait()
            return c
        lax.fori_loop(0, N_EXPERTS, pad_start, 0)
        lax.fori_loop(0, N_EXPERTS, pad_wait, 0)

        @pl.when(i > 0)
        def _prev_done():
            dispatched(prev).wait()
            dispatched(prev).wait()
        dispatched(cur).wait()
        dispatched(cur).wait()


def _outproj(y_ssm, y_att, x, mod4, w_out, ln_g, ln_b, wr_hi, wr_lo, rb, t, cap, blk):
    n = x.shape[0]
    tm = min(OUTPROJ_ROWS, t)
    nt = t // tm
    vec = lambda k: pl.BlockSpec((1, 1, 1, D), lambda i, k=k: (k, i // nt, 0, 0))
    full = lambda shape: pl.BlockSpec(shape, lambda i: (0,) * len(shape))
    rows = lambda w: pl.BlockSpec((tm, w), lambda i: (i, 0))
    return pl.pallas_call(
        functools.partial(_outproj_kernel, tm=tm, cap=cap, blk=blk),
        grid=(n // tm,),
        in_specs=[rows(D_SSM), rows(D_ATT), rows(D), vec(2), vec(4), vec(3),
                  full((D, D)), full((1, D)), full((1, D)), full((D, 128)), full((D, 128)), full((1, 128))],
        out_specs=[rows(D), rows(128), pl.BlockSpec((8, tm), lambda i: (0, i)), full((8, 128)),
                   pl.BlockSpec(memory_space=pl.ANY)],
        out_shape=[jax.ShapeDtypeStruct((n, D), f32), jax.ShapeDtypeStruct((n, 128), f32),
                   jax.ShapeDtypeStruct((8, n), i32), jax.ShapeDtypeStruct((8, 128), f32),
                   jax.ShapeDtypeStruct((N_EXPERTS * cap, ROW_TILE, 128), f32)],
        scratch_shapes=[pltpu.VMEM((1, 128), f32),
                        pltpu.VMEM((2, tm * ROW_TILE, 128), f32),
                        pltpu.VMEM((2, 8, tm), i32), pltpu.SMEM((2, 8, tm), i32),
                        pltpu.VMEM((8, 128), i32), pltpu.SMEM((8, 128), i32),
                        pltpu.VMEM((blk, ROW_TILE, 128), f32),
                        pltpu.SemaphoreType.DMA((2,)), pltpu.SemaphoreType.DMA((2,)), pltpu.SemaphoreType.DMA(())],
        compiler_params=pltpu.CompilerParams(dimension_semantics=("arbitrary",), vmem_limit_bytes=VMEM_LIMIT),
        name="outproj",
    )(y_ssm, y_att, x, mod4, mod4, mod4, w_out, ln_g, ln_b, wr_hi, wr_lo, rb)


def _tile_copy(src_hbm, row, buf, slot, sem):
    return pltpu.make_async_copy(src_hbm.at[row], buf.at[pl.ds(slot * ROW_TILE, ROW_TILE)], sem)


def _moe_kernel(be_ref, br_ref, nu_ref, nxt_ref, par_ref, x_ref, wg_hbm, wu_hbm, wd_hbm, y_ref,
                wfg, wfu, wfd, wgb, wub, wdb, wsem, *, blk):
    i = pl.program_id(0)
    used = i < nu_ref[0]

    def fetch(e, slot):
        return [pltpu.make_async_copy(src.at[e], dst.at[slot], wsem.at[slot])
                for src, dst in ((wg_hbm, wfg), (wu_hbm, wfu), (wd_hbm, wfd))]

    @pl.when(i == 0)
    def _first():
        for c in fetch(be_ref[0], par_ref[0]):
            c.start()

    @pl.when(jnp.logical_and(used, jnp.logical_or(i == 0, be_ref[i] != be_ref[jnp.maximum(i - 1, 0)])))
    def _switch():
        slot = par_ref[i]
        for c in fetch(be_ref[i], slot):
            c.wait()

        @pl.when(nxt_ref[i] >= 0)
        def _next():
            for c in fetch(nxt_ref[i], 1 - slot):
                c.start()
        wgb[...] = wfg[slot].astype(bf16)
        wub[...] = wfu[slot].astype(bf16)
        wdb[...] = wfd[slot].astype(bf16)

    @pl.when(used)
    def _work():
        xb = _load_row_tiles(x_ref, blk).astype(bf16)
        hid = (_silu(_dot(xb, wgb[...])) * _dot(xb, wub[...])).astype(bf16)
        _store_row_tiles(y_ref, _dot(hid, wdb[...]))


def _moe(block_e, block_row, n_used, next_e, parity, x_rows, w_gate, w_up, w_down, blk):
    nblk = block_e.shape[0]
    rspec = pl.BlockSpec((blk * ROW_TILE, 128), lambda i, be, br, *_: (br[i], 0))
    anyspec = pl.BlockSpec(memory_space=pl.ANY)
    grid_spec = pltpu.PrefetchScalarGridSpec(
        num_scalar_prefetch=5,
        grid=(nblk,),
        in_specs=[rspec, anyspec, anyspec, anyspec],
        out_specs=rspec,
        scratch_shapes=[pltpu.VMEM((2, D, D_EXPERT), f32), pltpu.VMEM((2, D, D_EXPERT), f32),
                        pltpu.VMEM((2, D_EXPERT, D), f32),
                        pltpu.VMEM((D, D_EXPERT), bf16), pltpu.VMEM((D, D_EXPERT), bf16),
                        pltpu.VMEM((D_EXPERT, D), bf16),
                        pltpu.SemaphoreType.DMA((2,))],
    )
    return pl.pallas_call(
        functools.partial(_moe_kernel, blk=blk),
        grid_spec=grid_spec,
        out_shape=jax.ShapeDtypeStruct(x_rows.shape, f32),
        compiler_params=pltpu.CompilerParams(dimension_semantics=("arbitrary",), vmem_limit_bytes=VMEM_LIMIT),
        name="moe",
    )(block_e, block_row, n_used, next_e, parity, x_rows, w_gate, w_up, w_down)


def _combine_kernel(d1_ref, d2_ref, y_hbm, x1_ref, rt_ref, gt_ref, lng_ref, lnb_ref, o_ref, buf, sem, *, tm):
    i = pl.program_id(0)
    last = pl.num_programs(0) - 1

    @pl.when(i == 0)
    def _prologue():
        for s in range(2):
            def issue(r, c, s=s):
                base = jnp.minimum(s, last) * tm
                _tile_copy(y_hbm, d1_ref[base + r], buf.at[s, 0], r, sem.at[s]).start()
                _tile_copy(y_hbm, d2_ref[base + r], buf.at[s, 1], r, sem.at[s]).start()
                return c
            lax.fori_loop(0, tm, issue, 0)

    def rows_ready(slot):
        return pltpu.make_async_copy(buf.at[slot], buf.at[slot], sem.at[slot])

    slot = i % 3
    nxt = (i + 2) % 3
    base = jnp.minimum(i + 2, last) * tm
    rows_ready(slot).wait()
    for r in range(tm):
        _tile_copy(y_hbm, d1_ref[base + r], buf.at[nxt, 0], r, sem.at[nxt]).start(priority=0)
        _tile_copy(y_hbm, d2_ref[base + r], buf.at[nxt, 1], r, sem.at[nxt]).start(priority=1)
    rt = rt_ref[...]
    moe = rt[:, 4:5] * _load_row_tiles(buf.at[slot, 0], tm) + rt[:, 5:6] * _load_row_tiles(buf.at[slot, 1], tm)
    y = ALPHA * x1_ref[...] + (1.0 + gt_ref[0, 0]) * moe
    o_ref[...] = _layer_norm(y, lng_ref[...], lnb_ref[...])

    @pl.when(i == last)
    def _drain():
        rows_ready((i + 1) % 3).wait()
        rows_ready((i + 2) % 3).wait()


def _combine(dest1, dest2, y_rows, x1, rt, mod4, ln_g, ln_b, t):
    n = x1.shape[0]
    tm = min(COMBINE_ROWS, t)
    nt = t // tm
    full = lambda shape: pl.BlockSpec(shape, lambda i, d1, d2: (0,) * len(shape))
    rows = lambda w: pl.BlockSpec((tm, w), lambda i, d1, d2: (i, 0))
    grid_spec = pltpu.PrefetchScalarGridSpec(
        num_scalar_prefetch=2,
        grid=(n // tm,),
        in_specs=[pl.BlockSpec(memory_space=pl.ANY), rows(D), rows(128),
                  pl.BlockSpec((1, 1, 1, D), lambda i, d1, d2: (5, i // nt, 0, 0)),
                  full((1, D)), full((1, D))],
        out_specs=rows(D),
        scratch_shapes=[pltpu.VMEM((3, 2, tm * ROW_TILE, 128), f32), pltpu.SemaphoreType.DMA((3,))],
    )
    return pl.pallas_call(
        functools.partial(_combine_kernel, tm=tm),
        grid_spec=grid_spec,
        out_shape=jax.ShapeDtypeStruct((n, D), f32),
        compiler_params=pltpu.CompilerParams(dimension_semantics=("arbitrary",), vmem_limit_bytes=VMEM_LIMIT),
        name="combine",
    )(dest1, dest2, y_rows, x1, rt, mod4, ln_g, ln_b)


def _hi_lo(w):
    hi = w.astype(bf16)
    return hi, (w - hi.astype(f32)).astype(bf16)


def _pad_lanes(v, offset, width):
    return jnp.zeros((width,), f32).at[offset:offset + v.shape[0]].set(v)


def _layer(x, c, ada_w, ada_b, w_in, conv_w, conv_b, dt_bias, a_log, d_skip, ssm_norm_g, fg_bias, att_norm_g,
           w_out, ln1_g, ln1_b, router_g_w, router_g_b, router_e_w, router_e_b, w_gate, w_up, w_down, ln2_g, ln2_b):
    bsz, t, _ = x.shape
    n = bsz * t

    mod4 = _ada(c, ada_w, ada_b).reshape(6, bsz, 1, D)

    w_main = jnp.concatenate([w_in[:, 0:1536], w_in[:, 1544:3080]], axis=1).astype(bf16)
    w_small = jnp.concatenate([w_in[:, 1536:1544], w_in[:, 3080:3088], jnp.zeros((D, 112), f32)], axis=1)
    ws_hi, ws_lo = _hi_lo(w_small)
    zx, qkv, dtf = _inproj(x, mod4, w_main, ws_hi, ws_lo)

    pc = jnp.stack([_pad_lanes(dt_bias, 0, 128), _pad_lanes(a_log, 0, 128), _pad_lanes(fg_bias, 8, 128)]
                   + [jnp.zeros((128,), f32)] * 5)
    rep = lambda v: jnp.repeat(v, HEAD_DIM)
    pe = jnp.stack([rep(dt_bias), rep(a_log), rep(d_skip), ssm_norm_g] + [jnp.zeros((D_SSM,), f32)] * 4)
    y_ssm, cumc = _ssd(zx, dtf, conv_w, conv_b.reshape(1, -1), pc, pe)

    y_att = _attn(qkv, cumc, att_norm_g.reshape(1, -1))

    wr = jnp.concatenate([router_g_w, router_e_w, jnp.zeros((D, 128 - N_GROUPS_R - N_EXPERTS), f32)], axis=1)
    wr_hi, wr_lo = _hi_lo(wr)
    rb = jnp.concatenate([router_g_b, router_e_b, jnp.zeros((128 - N_GROUPS_R - N_EXPERTS,), f32)]).reshape(1, 128)
    blk = MOE_ROWS
    cap = n + blk
    x1, rt, dest, cnt, x_rows = _outproj(y_ssm.reshape(n, D_SSM), y_att.reshape(n, D_ATT), x.reshape(n, D), mod4,
                                         w_out.astype(bf16), ln1_g.reshape(1, D), ln1_b.reshape(1, D),
                                         wr_hi, wr_lo, rb, t, cap, blk)

    counts = cnt[0, N_GROUPS_R:N_GROUPS_R + N_EXPERTS].astype(i32)
    nb = (counts + blk - 1) // blk
    nb_end = jnp.cumsum(nb)
    n_used = nb_end[N_EXPERTS - 1:]
    nblk = (2 * n) // blk + N_EXPERTS
    step = jnp.minimum(jnp.arange(nblk, dtype=i32), n_used[0] - 1)
    owner = (nb_end[None, :] <= step[:, None]).astype(i32)
    block_e = jnp.sum(owner, axis=1)
    first = jnp.sum(owner * nb[None, :], axis=1)
    block_row = block_e * (cap // blk) + (step - first)
    run_end = first + jnp.sum((jnp.arange(N_EXPERTS)[None, :] == block_e[:, None]) * nb[None, :], axis=1)
    next_e = jnp.where(run_end < n_used[0], jnp.sum((nb_end[None, :] <= run_end[:, None]).astype(i32), axis=1), -1)
    parity = jnp.sum(((jnp.arange(N_EXPERTS)[None, :] < block_e[:, None]) & (nb[None, :] > 0)).astype(i32), axis=1) % 2

    y_rows = _moe(block_e, block_row, n_used, next_e, parity, x_rows.reshape(N_EXPERTS * cap * ROW_TILE, 128),
                  w_gate, w_up, w_down, blk)
    out = _combine(dest[0], dest[1], y_rows.reshape(N_EXPERTS * cap, ROW_TILE, 128), x1, rt, mod4,
                   ln2_g.reshape(1, D), ln2_b.reshape(1, D), t)
    return out.reshape(bsz, t, D)


def kernel(x, c, ada_w, ada_b, w_in, conv_w, conv_b, dt_bias, a_log, d_skip, ssm_norm_g, fg_bias, att_norm_g, w_out,
           ln1_g, ln1_b, router_g_w, router_g_b, router_e_w, router_e_b, w_gate, w_up, w_down, ln2_g, ln2_b):
    depth = ada_w.shape[0]
    for l in range(depth):
        x = _layer(x, c, ada_w[l], ada_b[l], w_in[l], conv_w[l], conv_b[l], dt_bias[l], a_log[l], d_skip[l],
                   ssm_norm_g[l], fg_bias[l], att_norm_g[l], w_out[l], ln1_g[l], ln1_b[l], router_g_w[l],
                   router_g_b[l], router_e_w[l], router_e_b[l], w_gate[l], w_up[l], w_down[l], ln2_g[l], ln2_b[l])
    return x
```

```python
import functools

import jax
import jax.numpy as jnp
import numpy as np
from jax import lax
from jax.experimental import pallas as pl
from jax.experimental.pallas import tpu as pltpu

f32 = jnp.float32
bf16 = jnp.bfloat16
i32 = jnp.int32

D = 1024
D_SSM = 512
D_ATT = 512
HEAD_DIM = 64
GROUP_W = 256
N_STATE = 128
CONV_K = 4
N_GROUPS_R = 4
EXPERTS_PER_GROUP = 8
N_EXPERTS = 32
D_EXPERT = 512
ALPHA = 2.0 ** 0.25
EPS = 1e-5
NEG = -1e30
LOG2E = 1.4426950408889634
QK_SCALE = HEAD_DIM ** -0.5 * LOG2E
V_ROWS = 80

SSD_CHUNK = 256
ATT_BLOCK = 256
INPROJ_ROWS = 512
OUTPROJ_ROWS = 512
OUTPROJ_PARTS = 2
MOE_ROWS = 256
COMBINE_ROWS = 256
DEST_ROWS = 2048
VMEM_LIMIT = 48 * 1024 * 1024


def _dot(a, b):
    return jnp.dot(a, b, preferred_element_type=f32)


def _dot_nt(a, b):
    return lax.dot_general(a, b, (((1,), (1,)), ((), ())), preferred_element_type=f32)


def _dot_tn(a, b):
    return lax.dot_general(a, b, (((0,), (0,)), ((), ())), preferred_element_type=f32)


def _split3(v):
    hi = v.astype(bf16)
    r1 = v - hi.astype(f32)
    mid = r1.astype(bf16)
    lo = (r1 - mid.astype(f32)).astype(bf16)
    return hi, mid, lo


def _dot_exact_lhs(m, v):
    hi, mid, lo = _split3(v)
    return (_dot(m, hi) + _dot(m, mid)) + _dot(m, lo)


def _dot_exact_rhs(v, m):
    hi, mid, lo = _split3(v)
    return (_dot(hi, m) + _dot(mid, m)) + _dot(lo, m)


ROW_TILE = 8


def _store_row_tiles(ref, val):
    rows = val.shape[0]
    for c in range(ROW_TILE):
        ref[pl.ds(c, rows, stride=ROW_TILE), :] = val[:, c * 128:(c + 1) * 128]


def _load_row_tiles(ref, rows):
    return jnp.concatenate([ref[pl.ds(c, rows, stride=ROW_TILE), :] for c in range(ROW_TILE)], axis=1)


def _softplus(x):
    return jnp.maximum(x, 0.0) + jnp.log1p(jnp.exp(-jnp.abs(x)))


def _silu(x):
    return x * jax.nn.sigmoid(x)


def _ada_kernel(c_ref, w_ref, b_ref, o_ref):
    s = _silu(c_ref[...]).astype(bf16)
    o_ref[0] = _dot(s, w_ref[...].astype(bf16)) + b_ref[0]


def _ada(c, w, b):
    bsz = c.shape[0]
    return pl.pallas_call(
        _ada_kernel,
        grid=(6,),
        in_specs=[pl.BlockSpec((bsz, D), lambda j: (0, 0)),
                  pl.BlockSpec((D, D), lambda j: (0, j)),
                  pl.BlockSpec((1, 1, D), lambda j: (j, 0, 0))],
        out_specs=pl.BlockSpec((1, bsz, D), lambda j: (j, 0, 0)),
        out_shape=jax.ShapeDtypeStruct((6, bsz, D), f32),
        compiler_params=pltpu.CompilerParams(dimension_semantics=("arbitrary",), vmem_limit_bytes=VMEM_LIMIT),
        name="ada",
    )(c, w, b.reshape(6, 1, D))


def _inproj_kernel(x_ref, sc_ref, sh_ref, wm_ref, ws_ref, cw_ref, cb_ref, zx_ref, bc_ref, qkv_ref, dtf_ref, xcat, *, tm):
    i = pl.program_id(1)

    @pl.when(i == 0)
    def _init():
        xcat[0:8, :] = jnp.zeros((8, 2 * D_SSM), f32)

    u = x_ref[0] * (1.0 + sc_ref[0, 0]) + sh_ref[0, 0]
    ub = u.astype(bf16)
    xcat[8:8 + tm, 0:512] = _dot(ub, wm_ref[:, 512:1024])
    xcat[8:8 + tm, 512:1024] = _dot(ub, wm_ref[:, 1024:1536])
    zx_ref[0, :, 0:512] = _dot(ub, wm_ref[:, 0:512])
    acc = cw_ref[0:1, :] * xcat[5:5 + tm, :] + cb_ref[...]
    for k in range(1, CONV_K):
        acc = acc + cw_ref[k:k + 1, :] * xcat[5 + k:5 + k + tm, :]
    xcat[0:8, :] = xcat[tm:tm + 8, :]
    xbc = _silu(acc)
    zx_ref[0, :, 512:1024] = xbc[:, 0:512]
    bc_ref[0] = xbc[:, 512:1024].astype(bf16)
    for j, scale in enumerate((QK_SCALE, 1.0, 1.0)):
        qkv_ref[0, :, j * 512:(j + 1) * 512] = (
            _dot(ub, wm_ref[:, 1536 + j * 512:1536 + (j + 1) * 512]) * scale).astype(bf16)
    ul = (u - ub.astype(f32)).astype(bf16)
    d_hl = _dot(ub, ws_ref[...])
    dtf_ref[0] = (d_hl[:, 0:128] + _dot(ul, ws_ref[:, 0:128])) + d_hl[:, 128:256]


def _inproj(x, mod4, w_main, ws, conv_w, conv_b):
    bsz, t, _ = x.shape
    tm = min(INPROJ_ROWS, t)
    vec = lambda k: pl.BlockSpec((1, 1, 1, D), lambda b, i, k=k: (k, b, 0, 0))
    full = lambda shape: pl.BlockSpec(shape, lambda b, i: (0,) * len(shape))
    rows = lambda w: pl.BlockSpec((1, tm, w), lambda b, i: (b, i, 0))
    return pl.pallas_call(
        functools.partial(_inproj_kernel, tm=tm),
        grid=(bsz, t // tm),
        in_specs=[rows(D), vec(1), vec(0), full((D, 3072)), full((D, 256)),
                  full((CONV_K, 2 * D_SSM)), full((1, 2 * D_SSM))],
        out_specs=[rows(1024), rows(512), rows(1536), rows(128)],
        out_shape=[jax.ShapeDtypeStruct((bsz, t, 1024), f32),
                   jax.ShapeDtypeStruct((bsz, t, 512), bf16),
                   jax.ShapeDtypeStruct((bsz, t, 1536), bf16),
                   jax.ShapeDtypeStruct((bsz, t, 128), f32)],
        scratch_shapes=[pltpu.VMEM((tm + 8, 2 * D_SSM), f32)],
        compiler_params=pltpu.CompilerParams(dimension_semantics=("parallel", "arbitrary"),
                                             vmem_limit_bytes=VMEM_LIMIT),
        name="inproj",
    )(x, mod4, mod4, w_main, ws, conv_w, conv_b)


def _ssd_kernel(z_ref, xs_ref, bc_ref, dtf_ref, pc_ref, pe_ref, y_ref, cumc_ref, state, carry, *, lc):
    j = pl.program_id(1)

    @pl.when(j == 0)
    def _init():
        state[...] = jnp.zeros_like(state)
        carry[...] = jnp.zeros_like(carry)

    xs = xs_ref[0]
    bm = bc_ref[0, :, 0:256]
    cm = bc_ref[0, :, 256:512]

    dtf = dtf_ref[0]
    lane = lax.broadcasted_iota(i32, (lc, 128), 1)
    dt_c = _softplus(dtf + pc_ref[0:1, :])
    a_c = dt_c * (-jnp.exp(pc_ref[1:2, :]))
    logf = -_softplus(-(dtf + pc_ref[2:3, :]))
    v = jnp.where(lane < 8, a_c, logf)
    r_i = lax.broadcasted_iota(i32, (lc, lc), 0)
    c_i = lax.broadcasted_iota(i32, (lc, lc), 1)
    tri = r_i >= c_i
    tri_b = jnp.where(tri, 1.0, 0.0).astype(bf16)
    cum = _dot_exact_lhs(tri_b, v) + carry[...]
    carry[...] = jnp.where(lane[0:1, :] >= 8, cum[lc - 1:lc, :], 0.0)
    cumc_ref[0] = cum
    cs_t = cum.T[0:8, :]

    e_r = lax.broadcasted_iota(i32, (128, D_SSM), 0)
    e_c = lax.broadcasted_iota(i32, (128, D_SSM), 1)
    expand = jnp.where(jnp.right_shift(e_c, 6) == e_r, 1.0, 0.0).astype(bf16)
    dt_e = _dot_exact_rhs(dt_c, expand)
    cs_e = _dot_exact_rhs(cum, expand)

    xdt = xs * dt_e
    ecs = jnp.exp(cs_e)
    cs_last = cs_e[lc - 1:lc, :]
    dec_st = jnp.exp(cs_last - cs_e)
    lane_g = lax.broadcasted_iota(i32, (1, GROUP_W), 1)
    ys = []
    for g in range(2):
        gs = slice(g * GROUP_W, (g + 1) * GROUP_W)
        bg = bm[:, g * N_STATE:(g + 1) * N_STATE]
        cg = cm[:, g * N_STATE:(g + 1) * N_STATE]
        cb = _dot_nt(cg, bg)
        xdt_g = xdt[:, gs]
        xdt_gb = xdt_g.astype(bf16)
        ms, xb = [], []
        for hh in range(4):
            h = g * 4 + hh
            lm = jnp.exp(jnp.where(tri, cum[:, h:h + 1] - cs_t[h:h + 1, :], -jnp.inf))
            ms.append((cb * lm).astype(bf16))
            xb.append(jnp.where(jnp.right_shift(lane_g, 6) == hh, xdt_gb, jnp.zeros_like(xdt_gb)))
        y_diag = _dot(jnp.concatenate(ms, axis=1), jnp.concatenate(xb, axis=0))
        st = state[g]
        y_off = _dot(cg, st.astype(bf16)) * ecs[:, gs]
        upd = _dot_tn(bg, (xdt_g * dec_st[:, gs]).astype(bf16))
        state[g] = st * jnp.exp(cs_last[:, gs]) + upd
        ys.append(y_diag + y_off + xs[:, gs] * pe_ref[2:3, gs])

    outs = []
    for g in range(2):
        gs = slice(g * GROUP_W, (g + 1) * GROUP_W)
        yg = ys[g] * _silu(z_ref[0, :, gs])
        ms_ = jnp.mean(yg * yg, axis=-1, keepdims=True)
        outs.append(yg * lax.rsqrt(ms_ + EPS))
    y_ref[0] = (jnp.concatenate(outs, axis=1) * pe_ref[3:4, :]).astype(bf16)


def _ssd(zx, bc, dtf, pc, pe):
    bsz, t, _ = zx.shape
    lc = min(SSD_CHUNK, t)
    col = lambda k: pl.BlockSpec((1, lc, 512), lambda b, j, k=k: (b, j, k))
    full = lambda shape: pl.BlockSpec(shape, lambda b, j: (0,) * len(shape))
    return pl.pallas_call(
        functools.partial(_ssd_kernel, lc=lc),
        grid=(bsz, t // lc),
        in_specs=[col(0), col(1), col(0),
                  pl.BlockSpec((1, lc, 128), lambda b, j: (b, j, 0)),
                  full((8, 128)), full((8, D_SSM))],
        out_specs=[pl.BlockSpec((1, lc, D_SSM), lambda b, j: (b, j, 0)),
                   pl.BlockSpec((1, lc, 128), lambda b, j: (b, j, 0))],
        out_shape=[jax.ShapeDtypeStruct((bsz, t, D_SSM), bf16),
                   jax.ShapeDtypeStruct((bsz, t, 128), f32)],
        scratch_shapes=[pltpu.VMEM((2, N_STATE, GROUP_W), f32),
                        pltpu.VMEM((1, 128), f32)],
        compiler_params=pltpu.CompilerParams(dimension_semantics=("parallel", "arbitrary"),
                                             vmem_limit_bytes=VMEM_LIMIT),
        name="ssd",
    )(zx, zx, bc, dtf, pc, pe)


def _attn_kernel(q_ref, k_ref, v_ref, cc_ref, psel_ref, ng_ref, o_ref, kaug, vt, acc, *, tq, t):
    i = pl.program_id(1)
    nkb = t // tq
    n_heads = D_ATT // HEAD_DIM
    lane = lax.broadcasted_iota(i32, (1, 128), 1)
    lo_half = lane < HEAD_DIM

    @pl.when(i == 0)
    def _build():
        eye = jnp.where(lax.broadcasted_iota(i32, (D_ATT, D_ATT), 0) == lax.broadcasted_iota(i32, (D_ATT, D_ATT), 1),
                        1.0, 0.0).astype(bf16)
        ones_rows = jnp.where(lax.broadcasted_iota(i32, (V_ROWS - HEAD_DIM, tq), 0) == 0, 1.0, 0.0).astype(bf16)
        for jb in range(nkb):
            rows = slice(jb * tq, (jb + 1) * tq)
            v_t = _dot_nt(eye, v_ref[0, rows, :]).astype(bf16)
            for h in range(n_heads):
                vt[jb, h * V_ROWS:h * V_ROWS + HEAD_DIM, :] = v_t[h * HEAD_DIM:(h + 1) * HEAD_DIM, :]
                vt[jb, h * V_ROWS + HEAD_DIM:(h + 1) * V_ROWS, :] = ones_rows
            pieces = jnp.concatenate(_split3(cc_ref[0, rows, :] * (-LOG2E)), axis=1)
            for p in range(n_heads // 2):
                a = _dot(pieces, psel_ref[p]).astype(bf16)
                kp = k_ref[0, rows, p * 128:(p + 1) * 128]
                kaug[2 * p, rows, :] = jnp.where(lo_half, kp, a)
                kaug[2 * p + 1, rows, :] = jnp.where(lo_half, a, kp)

    ones_hi = jnp.where((lane >= HEAD_DIM) & (lane < HEAD_DIM + 3), 1.0, 0.0).astype(bf16)
    ones_lo = jnp.where(lane < 3, 1.0, 0.0).astype(bf16)
    qa = []
    for p in range(n_heads // 2):
        qp = q_ref[0, :, p * 128:(p + 1) * 128]
        qa.append(jnp.where(lo_half, qp, ones_hi))
        qa.append(jnp.where(lo_half, ones_lo, qp))
    keep = lax.broadcasted_iota(i32, (tq, tq), 0) <= lax.broadcasted_iota(i32, (tq, tq), 1)
    acc[...] = jnp.zeros_like(acc)

    def block(jb, ms, masked):
        k0 = pl.multiple_of(jb * tq, tq)
        new_ms = []
        scores = [_dot_nt(kaug[h, pl.ds(k0, tq), :], qa[h]) for h in range(n_heads)]
        for h in range(n_heads):
            s = scores[h]
            if masked:
                s = jnp.where(keep, s, NEG)
            m_new = jnp.maximum(ms[h], jnp.max(s, axis=0, keepdims=True))
            alpha = jnp.exp2(ms[h] - m_new)
            p = jnp.exp2(s - m_new).astype(bf16)
            new_ms.append(m_new)
            acc[h] = acc[h] * alpha + _dot(vt[jb, h * V_ROWS:(h + 1) * V_ROWS, :], p)
        return tuple(new_ms)

    init = tuple(jnp.full((1, tq), NEG, f32) for _ in range(n_heads))
    ms = lax.fori_loop(0, i, lambda jb, c: block(jb, c, False), init)
    block(i, ms, True)
    out_t = jnp.concatenate([acc[h, 0:HEAD_DIM, :] * (1.0 / acc[h, HEAD_DIM:HEAD_DIM + 1, :]) for h in range(n_heads)],
                            axis=0)
    ms_ = jnp.mean(out_t * out_t, axis=0, keepdims=True)
    out_t = out_t * lax.rsqrt(ms_ + EPS)
    o_ref[0] = (out_t.T * ng_ref[...]).astype(bf16)


def _piece_select():
    sel = np.zeros((4, 384, 128), np.float32)
    for pair in range(4):
        for j in range(3):
            sel[pair, j * 128 + 8 + 2 * pair, HEAD_DIM + j] = 1.0
            sel[pair, j * 128 + 8 + 2 * pair + 1, j] = 1.0
    return jnp.asarray(sel, bf16)


def _attn(qkv, cumc, norm_g):
    bsz, t, _ = qkv.shape
    tq = min(ATT_BLOCK, t)
    return pl.pallas_call(
        functools.partial(_attn_kernel, tq=tq, t=t),
        grid=(bsz, t // tq),
        in_specs=[pl.BlockSpec((1, tq, D_ATT), lambda b, i: (b, i, 0)),
                  pl.BlockSpec((1, t, D_ATT), lambda b, i: (b, 0, 1)),
                  pl.BlockSpec((1, t, D_ATT), lambda b, i: (b, 0, 2)),
                  pl.BlockSpec((1, t, 128), lambda b, i: (b, 0, 0)),
                  pl.BlockSpec((4, 384, 128), lambda b, i: (0, 0, 0)),
                  pl.BlockSpec((1, D_ATT), lambda b, i: (0, 0))],
        out_specs=pl.BlockSpec((1, tq, D_ATT), lambda b, i: (b, i, 0)),
        out_shape=jax.ShapeDtypeStruct((bsz, t, D_ATT), bf16),
        scratch_shapes=[pltpu.VMEM((D_ATT // HEAD_DIM, t, 128), bf16),
                        pltpu.VMEM((t // tq, (D_ATT // HEAD_DIM) * V_ROWS, tq), bf16),
                        pltpu.VMEM((D_ATT // HEAD_DIM, V_ROWS, tq), f32)],
        compiler_params=pltpu.CompilerParams(dimension_semantics=("parallel", "arbitrary"),
                                             vmem_limit_bytes=VMEM_LIMIT),
        name="attn",
    )(qkv, qkv, qkv, cumc, _piece_select(), norm_g)


def _layer_norm(y, g, b):
    mu = jnp.mean(y, axis=-1, keepdims=True)
    yc = y - mu
    var = jnp.mean(yc * yc, axis=-1, keepdims=True)
    return yc * lax.rsqrt(var + EPS) * g + b


def _outproj_kernel(ys_ref, ya_ref, x_ref, gt_ref, sc_ref, sh_ref, wo_ref, lng_ref, lnb_ref, wrh_ref, wrl_ref, rb_ref,
                    x1_ref, rt_ref, dest_ref, cnt_ref, xrows_hbm,
                    carry, u2t, didx_v, didx_s, cnt_v, cnt_s, zeros, idx_sem, sc_sem, z_sem, *, tm, cap, blk):
    i = pl.program_id(0)
    last = pl.num_programs(0) - 1
    cur = i % 2
    prev = 1 - cur

    def idx_ready(slot):
        return pltpu.make_async_copy(didx_v.at[slot], didx_s.at[slot], idx_sem.at[slot])

    def dispatched(slot):
        return pltpu.make_async_copy(u2t.at[slot], u2t.at[slot], sc_sem.at[slot])

    def dispatch_copy(slot, r, k):
        return pltpu.make_async_copy(u2t.at[slot, pl.ds(r * ROW_TILE, ROW_TILE)], xrows_hbm.at[didx_s[slot, k, r]],
                                     sc_sem.at[slot])

    @pl.when(i == 0)
    def _init():
        carry[...] = jnp.zeros_like(carry)

    @pl.when(i >= 2)
    def _reuse():
        dispatched(cur).wait()
        dispatched(cur).wait()

    def step(dispatch_prev):
        n_parts = OUTPROJ_PARTS
        n_slices = 3 * n_parts + 2

        def dispatch_slice(c):
            if dispatch_prev:
                for r_ in range(c * tm // n_slices, (c + 1) * tm // n_slices):
                    dispatch_copy(prev, r_, 0).start(priority=0)
                    dispatch_copy(prev, r_, 1).start(priority=1)

        if dispatch_prev:
            idx_ready(prev).wait()
        part = tm // n_parts
        hs = []
        for a in range(n_parts):
            rs = slice(a * part, (a + 1) * part)
            dispatch_slice(a)
            hs.append(_dot(ys_ref[rs, :], wo_ref[0:D_SSM, :]) + _dot(ya_ref[rs, :], wo_ref[D_SSM:D, :]))
        logit_parts = []
        for a in range(n_parts):
            rs = slice(a * part, (a + 1) * part)
            dispatch_slice(n_parts + 2 * a)
            x1 = _layer_norm(ALPHA * x_ref[rs, :] + (1.0 + gt_ref[0, 0]) * hs[a], lng_ref[...], lnb_ref[...])
            x1_ref[rs, :] = x1
            u2 = x1 * (1.0 + sc_ref[0, 0]) + sh_ref[0, 0]
            _store_row_tiles(u2t.at[cur, pl.ds(a * part * ROW_TILE, part * ROW_TILE)], u2)
            dispatch_slice(n_parts + 2 * a + 1)
            uh = u2.astype(bf16)
            ul = (u2 - uh.astype(f32)).astype(bf16)
            logit_parts.append((_dot(uh, wrh_ref[...]) + _dot(ul, wrh_ref[...])) + _dot(uh, wrl_ref[...]))
        logits = jnp.concatenate(logit_parts, axis=0) + rb_ref[...]
        lane = lax.broadcasted_iota(i32, (tm, 128), 1).astype(f32)
        big = jnp.float32(1e9)

        def first_max(vals):
            m = jnp.max(vals, axis=-1, keepdims=True)
            return m, jnp.min(jnp.where(vals == m, lane, big), axis=-1, keepdims=True)

        gl = jnp.where(lane < N_GROUPS_R, logits, NEG)
        gmax, gidx = first_max(gl)
        g_p = 1.0 / jnp.sum(jnp.exp(gl - gmax), axis=-1, keepdims=True)
        lo = N_GROUPS_R + EXPERTS_PER_GROUP * gidx
        el = jnp.where((lane >= lo) & (lane < lo + EXPERTS_PER_GROUP), logits, NEG)
        m1, i1 = first_max(el)
        el2 = jnp.where(lane == i1, NEG, el)
        m2, i2 = first_max(el2)
        r = jnp.exp(m2 - m1)
        w1 = g_p / (1.0 + r)
        w2 = g_p * r / (1.0 + r)
        dispatch_slice(3 * n_parts)

        oh1 = lane == i1
        oh2 = lane == i2
        oh = jnp.where(oh1 | oh2, 1.0, 0.0)
        r_i = lax.broadcasted_iota(i32, (tm, tm), 0)
        c_i = lax.broadcasted_iota(i32, (tm, tm), 1)
        lower = jnp.where(r_i > c_i, 1.0, 0.0).astype(bf16)
        prefix = _dot(lower, oh.astype(bf16)) + carry[...]
        rank1 = jnp.sum(jnp.where(oh1, prefix, 0.0), axis=-1, keepdims=True)
        rank2 = jnp.sum(jnp.where(oh2, prefix, 0.0), axis=-1, keepdims=True)
        carry[...] = carry[...] + jnp.sum(oh, axis=0, keepdims=True)
        cnt_ref[...] = jnp.broadcast_to(carry[...], (8, 128))
        dispatch_slice(3 * n_parts + 1)

        rt = jnp.where(lane == 4, w1, jnp.where(lane == 5, w2, 0.0))
        rt_ref[...] = rt

        d1 = (i1 - N_GROUPS_R) * cap + rank1
        d2 = (i2 - N_GROUPS_R) * cap + rank2
        dd = jnp.where(lane == 0, d1, jnp.where(lane == 1, d2, 0.0)).T[0:8, :].astype(i32)
        dest_ref[...] = dd
        didx_v[cur] = dd
        idx_ready(cur).start()

    pl.when(i == 0)(lambda: step(False))
    pl.when(i > 0)(lambda: step(True))

    @pl.when(i == last)
    def _epilogue():
        idx_ready(cur).wait()

        def issue(r_, c):
            dispatch_copy(cur, r_, 0).start()
            dispatch_copy(cur, r_, 1).start()
            return c
        lax.fori_loop(0, tm, issue, 0)

        cnt_v[...] = jnp.broadcast_to(carry[...], (8, 128)).astype(i32)
        to_smem = pltpu.make_async_copy(cnt_v, cnt_s, z_sem)
        to_smem.start()
        to_smem.wait()
        zeros[...] = jnp.zeros_like(zeros)

        def pad_copy(e):
            return pltpu.make_async_copy(zeros, xrows_hbm.at[pl.ds(e * cap + cnt_s[0, N_GROUPS_R + e], blk)], z_sem)

        def pad_start(e, c):
            pad_copy(e).start()
            return c

        def pad_wait(e, c):
            pad_copy(e).wait()
            return c
        lax.fori_loop(0, N_EXPERTS, pad_start, 0)
        lax.fori_loop(0, N_EXPERTS, pad_wait, 0)

        @pl.when(i > 0)
        def _prev_done():
            dispatched(prev).wait()
            dispatched(prev).wait()
        dispatched(cur).wait()
        dispatched(cur).wait()


def _outproj(y_ssm, y_att, x, mod4, w_out, ln_g, ln_b, wr_hi, wr_lo, rb, t, cap, blk):
    n = x.shape[0]
    tm = min(OUTPROJ_ROWS, t)
    nt = t // tm
    vec = lambda k: pl.BlockSpec((1, 1, 1, D), lambda i, k=k: (k, i // nt, 0, 0))
    full = lambda shape: pl.BlockSpec(shape, lambda i: (0,) * len(shape))
    rows = lambda w: pl.BlockSpec((tm, w), lambda i: (i, 0))
    return pl.pallas_call(
        functools.partial(_outproj_kernel, tm=tm, cap=cap, blk=blk),
        grid=(n // tm,),
        in_specs=[rows(D_SSM), rows(D_ATT), rows(D), vec(2), vec(4), vec(3),
                  full((D, D)), full((1, D)), full((1, D)), full((D, 128)), full((D, 128)), full((1, 128))],
        out_specs=[rows(D), rows(128), pl.BlockSpec((8, tm), lambda i: (0, i)), full((8, 128)),
                   pl.BlockSpec(memory_space=pl.ANY)],
        out_shape=[jax.ShapeDtypeStruct((n, D), f32), jax.ShapeDtypeStruct((n, 128), f32),
                   jax.ShapeDtypeStruct((8, n), i32), jax.ShapeDtypeStruct((8, 128), f32),
                   jax.ShapeDtypeStruct((N_EXPERTS * cap, ROW_TILE, 128), f32)],
        scratch_shapes=[pltpu.VMEM((1, 128), f32),
                        pltpu.VMEM((2, tm * ROW_TILE, 128), f32),
                        pltpu.VMEM((2, 8, tm), i32), pltpu.SMEM((2, 8, tm), i32),
                        pltpu.VMEM((8, 128), i32), pltpu.SMEM((8, 128), i32),
                        pltpu.VMEM((blk, ROW_TILE, 128), f32),
                        pltpu.SemaphoreType.DMA((2,)), pltpu.SemaphoreType.DMA((2,)), pltpu.SemaphoreType.DMA(())],
        compiler_params=pltpu.CompilerParams(dimension_semantics=("arbitrary",), vmem_limit_bytes=VMEM_LIMIT),
        name="outproj",
    )(y_ssm, y_att, x, mod4, mod4, mod4, w_out, ln_g, ln_b, wr_hi, wr_lo, rb)


def _tile_copy(src_hbm, row, buf, slot, sem):
    return pltpu.make_async_copy(src_hbm.at[row], buf.at[pl.ds(slot * ROW_TILE, ROW_TILE)], sem)


def _moe_kernel(be_ref, br_ref, nu_ref, nxt_ref, par_ref, x_ref, wg_hbm, wu_hbm, wd_hbm, y_ref,
                wfg, wfu, wfd, wgb, wub, wdb, wsem, *, blk):
    i = pl.program_id(0)
    used = i < nu_ref[0]

    def fetch(e, slot):
        return [pltpu.make_async_copy(src.at[e], dst.at[slot], wsem.at[slot])
                for src, dst in ((wg_hbm, wfg), (wu_hbm, wfu), (wd_hbm, wfd))]

    @pl.when(i == 0)
    def _first():
        for c in fetch(be_ref[0], par_ref[0]):
            c.start()

    @pl.when(jnp.logical_and(used, jnp.logical_or(i == 0, be_ref[i] != be_ref[jnp.maximum(i - 1, 0)])))
    def _switch():
        slot = par_ref[i]
        for c in fetch(be_ref[i], slot):
            c.wait()

        @pl.when(nxt_ref[i] >= 0)
        def _next():
            for c in fetch(nxt_ref[i], 1 - slot):
                c.start()
        wgb[...] = wfg[slot].astype(bf16)
        wub[...] = wfu[slot].astype(bf16)
        wdb[...] = wfd[slot].astype(bf16)

    @pl.when(used)
    def _work():
        xb = _load_row_tiles(x_ref, blk).astype(bf16)
        hid = (_silu(_dot(xb, wgb[...])) * _dot(xb, wub[...])).astype(bf16)
        _store_row_tiles(y_ref, _dot(hid, wdb[...]))


def _moe(block_e, block_row, n_used, next_e, parity, x_rows, w_gate, w_up, w_down, blk):
    nblk = block_e.shape[0]
    rspec = pl.BlockSpec((blk * ROW_TILE, 128), lambda i, be, br, *_: (br[i], 0))
    anyspec = pl.BlockSpec(memory_space=pl.ANY)
    grid_spec = pltpu.PrefetchScalarGridSpec(
        num_scalar_prefetch=5,
        grid=(nblk,),
        in_specs=[rspec, anyspec, anyspec, anyspec],
        out_specs=rspec,
        scratch_shapes=[pltpu.VMEM((2, D, D_EXPERT), f32), pltpu.VMEM((2, D, D_EXPERT), f32),
                        pltpu.VMEM((2, D_EXPERT, D), f32),
                        pltpu.VMEM((D, D_EXPERT), bf16), pltpu.VMEM((D, D_EXPERT), bf16),
                        pltpu.VMEM((D_EXPERT, D), bf16),
                        pltpu.SemaphoreType.DMA((2,))],
    )
    return pl.pallas_call(
        functools.partial(_moe_kernel, blk=blk),
        grid_spec=grid_spec,
        out_shape=jax.ShapeDtypeStruct(x_rows.shape, f32),
        compiler_params=pltpu.CompilerParams(dimension_semantics=("arbitrary",), vmem_limit_bytes=VMEM_LIMIT),
        name="moe",
    )(block_e, block_row, n_used, next_e, parity, x_rows, w_gate, w_up, w_down)


def _combine_kernel(d1_ref, d2_ref, y_hbm, x1_ref, rt_ref, gt_ref, lng_ref, lnb_ref, o_ref, buf, sem, *, tm):
    i = pl.program_id(0)
    last = pl.num_programs(0) - 1

    @pl.when(i == 0)
    def _prologue():
        for s in range(2):
            def issue(r, c, s=s):
                base = jnp.minimum(s, last) * tm
                _tile_copy(y_hbm, d1_ref[base + r], buf.at[s, 0], r, sem.at[s]).start()
                _tile_copy(y_hbm, d2_ref[base + r], buf.at[s, 1], r, sem.at[s]).start()
                return c
            lax.fori_loop(0, tm, issue, 0)

    def rows_ready(slot):
        return pltpu.make_async_copy(buf.at[slot], buf.at[slot], sem.at[slot])

    slot = i % 3
    nxt = (i + 2) % 3
    base = jnp.minimum(i + 2, last) * tm
    rows_ready(slot).wait()
    for r in range(tm):
        _tile_copy(y_hbm, d1_ref[base + r], buf.at[nxt, 0], r, sem.at[nxt]).start(priority=0)
        _tile_copy(y_hbm, d2_ref[base + r], buf.at[nxt, 1], r, sem.at[nxt]).start(priority=1)
    rt = rt_ref[...]
    moe = rt[:, 4:5] * _load_row_tiles(buf.at[slot, 0], tm) + rt[:, 5:6] * _load_row_tiles(buf.at[slot, 1], tm)
    y = ALPHA * x1_ref[...] + (1.0 + gt_ref[0, 0]) * moe
    o_ref[...] = _layer_norm(y, lng_ref[...], lnb_ref[...])

    @pl.when(i == last)
    def _drain():
        rows_ready((i + 1) % 3).wait()
        rows_ready((i + 2) % 3).wait()


def _combine(dest1, dest2, y_rows, x1, rt, mod4, ln_g, ln_b, t):
    n = x1.shape[0]
    tm = min(COMBINE_ROWS, t)
    nt = t // tm
    full = lambda shape: pl.BlockSpec(shape, lambda i, d1, d2: (0,) * len(shape))
    rows = lambda w: pl.BlockSpec((tm, w), lambda i, d1, d2: (i, 0))
    grid_spec = pltpu.PrefetchScalarGridSpec(
        num_scalar_prefetch=2,
        grid=(n // tm,),
        in_specs=[pl.BlockSpec(memory_space=pl.ANY), rows(D), rows(128),
                  pl.BlockSpec((1, 1, 1, D), lambda i, d1, d2: (5, i // nt, 0, 0)),
                  full((1, D)), full((1, D))],
        out_specs=rows(D),
        scratch_shapes=[pltpu.VMEM((3, 2, tm * ROW_TILE, 128), f32), pltpu.SemaphoreType.DMA((3,))],
    )
    return pl.pallas_call(
        functools.partial(_combine_kernel, tm=tm),
        grid_spec=grid_spec,
        out_shape=jax.ShapeDtypeStruct((n, D), f32),
        compiler_params=pltpu.CompilerParams(dimension_semantics=("arbitrary",), vmem_limit_bytes=VMEM_LIMIT),
        name="combine",
    )(dest1, dest2, y_rows, x1, rt, mod4, ln_g, ln_b)


def _hi_lo(w):
    hi = w.astype(bf16)
    return hi, (w - hi.astype(f32)).astype(bf16)


def _pad_lanes(v, offset, width):
    return jnp.zeros((width,), f32).at[offset:offset + v.shape[0]].set(v)


def _layer(x, c, ada_w, ada_b, w_in, conv_w, conv_b, dt_bias, a_log, d_skip, ssm_norm_g, fg_bias, att_norm_g,
           w_out, ln1_g, ln1_b, router_g_w, router_g_b, router_e_w, router_e_b, w_gate, w_up, w_down, ln2_g, ln2_b):
    bsz, t, _ = x.shape
    n = bsz * t

    mod4 = _ada(c, ada_w, ada_b).reshape(6, bsz, 1, D)

    w_main = jnp.concatenate([w_in[:, 0:1536], w_in[:, 1544:3080]], axis=1).astype(bf16)
    w_small = jnp.concatenate([w_in[:, 1536:1544], w_in[:, 3080:3088], jnp.zeros((D, 112), f32)], axis=1)
    zx, bc, qkv, dtf = _inproj(x, mod4, w_main, jnp.concatenate(_hi_lo(w_small), axis=1), conv_w,
                               conv_b.reshape(1, -1))

    pc = jnp.stack([_pad_lanes(dt_bias, 0, 128), _pad_lanes(a_log, 0, 128), _pad_lanes(fg_bias, 8, 128)]
                   + [jnp.zeros((128,), f32)] * 5)
    rep = lambda v: jnp.repeat(v, HEAD_DIM)
    pe = jnp.stack([rep(dt_bias), rep(a_log), rep(d_skip), ssm_norm_g] + [jnp.zeros((D_SSM,), f32)] * 4)
    y_ssm, cumc = _ssd(zx, bc, dtf, pc, pe)

    y_att = _attn(qkv, cumc, att_norm_g.reshape(1, -1))

    wr = jnp.concatenate([router_g_w, router_e_w, jnp.zeros((D, 128 - N_GROUPS_R - N_EXPERTS), f32)], axis=1)
    wr_hi, wr_lo = _hi_lo(wr)
    rb = jnp.concatenate([router_g_b, router_e_b, jnp.zeros((128 - N_GROUPS_R - N_EXPERTS,), f32)]).reshape(1, 128)
    blk = MOE_ROWS
    cap = n + blk
    x1, rt, dest, cnt, x_rows = _outproj(y_ssm.reshape(n, D_SSM), y_att.reshape(n, D_ATT), x.reshape(n, D), mod4,
                                         w_out.astype(bf16), ln1_g.reshape(1, D), ln1_b.reshape(1, D),
                                         wr_hi, wr_lo, rb, t, cap, blk)

    counts = cnt[0, N_GROUPS_R:N_GROUPS_R + N_EXPERTS].astype(i32)
    nb = (counts + blk - 1) // blk
    nb_end = jnp.cumsum(nb)
    n_used = nb_end[N_EXPERTS - 1:]
    nblk = (2 * n) // blk + N_EXPERTS
    step = jnp.minimum(jnp.arange(nblk, dtype=i32), n_used[0] - 1)
    owner = (nb_end[None, :] <= step[:, None]).astype(i32)
    block_e = jnp.sum(owner, axis=1)
    first = jnp.sum(owner * nb[None, :], axis=1)
    block_row = block_e * (cap // blk) + (step - first)
    run_end = first + jnp.sum((jnp.arange(N_EXPERTS)[None, :] == block_e[:, None]) * nb[None, :], axis=1)
    next_e = jnp.where(run_end < n_used[0], jnp.sum((nb_end[None, :] <= run_end[:, None]).astype(i32), axis=1), -1)
    parity = jnp.sum(((jnp.arange(N_EXPERTS)[None, :] < block_e[:, None]) & (nb[None, :] > 0)).astype(i32), axis=1) % 2

    y_rows = _moe(block_e, block_row, n_used, next_e, parity, x_rows.reshape(N_EXPERTS * cap * ROW_TILE, 128),
                  w_gate, w_up, w_down, blk)
    out = _combine(dest[0], dest[1], y_rows.reshape(N_EXPERTS * cap, ROW_TILE, 128), x1, rt, mod4,
                   ln2_g.reshape(1, D), ln2_b.reshape(1, D), t)
    return out.reshape(bsz, t, D)


def kernel(x, c, ada_w, ada_b, w_in, conv_w, conv_b, dt_bias, a_log, d_skip, ssm_norm_g, fg_bias, att_norm_g, w_out,
           ln1_g, ln1_b, router_g_w, router_g_b, router_e_w, router_e_b, w_gate, w_up, w_down, ln2_g, ln2_b):
    depth = ada_w.shape[0]
    for l in range(depth):
        x = _layer(x, c, ada_w[l], ada_b[l], w_in[l], conv_w[l], conv_b[l], dt_bias[l], a_log[l], d_skip[l],
                   ssm_norm_g[l], fg_bias[l], att_norm_g[l], w_out[l], ln1_g[l], ln1_b[l], router_g_w[l],
                   router_g_b[l], router_e_w[l], router_e_b[l], w_gate[l], w_up[l], w_down[l], ln2_g[l], ln2_b[l])
    return x
```

```python
import functools

import jax
import jax.numpy as jnp
import numpy as np
from jax import lax
from jax.experimental import pallas as pl
from jax.experimental.pallas import tpu as pltpu

f32 = jnp.float32
bf16 = jnp.bfloat16
i32 = jnp.int32

D = 1024
D_SSM = 512
D_ATT = 512
HEAD_DIM = 64
GROUP_W = 256
N_STATE = 128
CONV_K = 4
N_GROUPS_R = 4
EXPERTS_PER_GROUP = 8
N_EXPERTS = 32
D_EXPERT = 512
ALPHA = 2.0 ** 0.25
EPS = 1e-5
NEG = -1e30
LOG2E = 1.4426950408889634
QK_SCALE = HEAD_DIM ** -0.5 * LOG2E
V_ROWS = 80

SSD_CHUNK = 256
ATT_BLOCK = 256
INPROJ_ROWS = 512
OUTPROJ_ROWS = 512
OUTPROJ_PARTS = 2
MOE_ROWS = 256
COMBINE_ROWS = 256
DEST_ROWS = 2048
VMEM_LIMIT = 48 * 1024 * 1024


def _dot(a, b):
    return jnp.dot(a, b, preferred_element_type=f32)


def _dot_nt(a, b):
    return lax.dot_general(a, b, (((1,), (1,)), ((), ())), preferred_element_type=f32)


def _dot_tn(a, b):
    return lax.dot_general(a, b, (((0,), (0,)), ((), ())), preferred_element_type=f32)


def _split3(v):
    hi = v.astype(bf16)
    r1 = v - hi.astype(f32)
    mid = r1.astype(bf16)
    lo = (r1 - mid.astype(f32)).astype(bf16)
    return hi, mid, lo


def _dot_exact_lhs(m, v):
    hi, mid, lo = _split3(v)
    return (_dot(m, hi) + _dot(m, mid)) + _dot(m, lo)


def _dot_exact_rhs(v, m):
    hi, mid, lo = _split3(v)
    return (_dot(hi, m) + _dot(mid, m)) + _dot(lo, m)


ROW_TILE = 8


def _store_row_tiles(ref, val):
    rows = val.shape[0]
    for c in range(ROW_TILE):
        ref[pl.ds(c, rows, stride=ROW_TILE), :] = val[:, c * 128:(c + 1) * 128]


def _load_row_tiles(ref, rows):
    return jnp.concatenate([ref[pl.ds(c, rows, stride=ROW_TILE), :] for c in range(ROW_TILE)], axis=1)


def _softplus(x):
    return jnp.maximum(x, 0.0) + jnp.log1p(jnp.exp(-jnp.abs(x)))


def _silu(x):
    return x * jax.nn.sigmoid(x)


def _ada_kernel(c_ref, w_ref, b_ref, o_ref):
    s = _silu(c_ref[...]).astype(bf16)
    o_ref[0] = _dot(s, w_ref[...].astype(bf16)) + b_ref[0]


def _ada(c, w, b):
    bsz = c.shape[0]
    return pl.pallas_call(
        _ada_kernel,
        grid=(6,),
        in_specs=[pl.BlockSpec((bsz, D), lambda j: (0, 0)),
                  pl.BlockSpec((D, D), lambda j: (0, j)),
                  pl.BlockSpec((1, 1, D), lambda j: (j, 0, 0))],
        out_specs=pl.BlockSpec((1, bsz, D), lambda j: (j, 0, 0)),
        out_shape=jax.ShapeDtypeStruct((6, bsz, D), f32),
        compiler_params=pltpu.CompilerParams(dimension_semantics=("arbitrary",), vmem_limit_bytes=VMEM_LIMIT),
        name="ada",
    )(c, w, b.reshape(6, 1, D))


def _inproj_kernel(x_ref, sc_ref, sh_ref, wm_ref, ws_ref, cw_ref, cb_ref, zx_ref, bc_ref, qkv_ref, dtf_ref, xcat, *, tm):
    i = pl.program_id(1)

    @pl.when(i == 0)
    def _init():
        xcat[0:8, :] = jnp.zeros((8, 2 * D_SSM), f32)

    u = x_ref[0] * (1.0 + sc_ref[0, 0]) + sh_ref[0, 0]
    ub = u.astype(bf16)
    xcat[8:8 + tm, 0:512] = _dot(ub, wm_ref[:, 512:1024])
    xcat[8:8 + tm, 512:1024] = _dot(ub, wm_ref[:, 1024:1536])
    zx_ref[0, :, 0:512] = _dot(ub, wm_ref[:, 0:512])
    acc = cw_ref[0:1, :] * xcat[5:5 + tm, :] + cb_ref[...]
    for k in range(1, CONV_K):
        acc = acc + cw_ref[k:k + 1, :] * xcat[5 + k:5 + k + tm, :]
    xcat[0:8, :] = xcat[tm:tm + 8, :]
    xbc = _silu(acc)
    zx_ref[0, :, 512:1024] = xbc[:, 0:512]
    bc_ref[0] = xbc[:, 512:1024].astype(bf16)
    for j, scale in enumerate((QK_SCALE, 1.0, 1.0)):
        qkv_ref[0, :, j * 512:(j + 1) * 512] = (
            _dot(ub, wm_ref[:, 1536 + j * 512:1536 + (j + 1) * 512]) * scale).astype(bf16)
    ul = (u - ub.astype(f32)).astype(bf16)
    d_hl = _dot(ub, ws_ref[...])
    dtf_ref[0] = (d_hl[:, 0:128] + _dot(ul, ws_ref[:, 0:128])) + d_hl[:, 128:256]


def _inproj(x, mod4, w_main, ws, conv_w, conv_b):
    bsz, t, _ = x.shape
    tm = min(INPROJ_ROWS, t)
    vec = lambda k: pl.BlockSpec((1, 1, 1, D), lambda b, i, k=k: (k, b, 0, 0))
    full = lambda shape: pl.BlockSpec(shape, lambda b, i: (0,) * len(shape))
    rows = lambda w: pl.BlockSpec((1, tm, w), lambda b, i: (b, i, 0))
    return pl.pallas_call(
        functools.partial(_inproj_kernel, tm=tm),
        grid=(bsz, t // tm),
        in_specs=[rows(D), vec(1), vec(0), full((D, 3072)), full((D, 256)),
                  full((CONV_K, 2 * D_SSM)), full((1, 2 * D_SSM))],
        out_specs=[rows(1024), rows(512), rows(1536), rows(128)],
        out_shape=[jax.ShapeDtypeStruct((bsz, t, 1024), f32),
                   jax.ShapeDtypeStruct((bsz, t, 512), bf16),
                   jax.ShapeDtypeStruct((bsz, t, 1536), bf16),
                   jax.ShapeDtypeStruct((bsz, t, 128), f32)],
        scratch_shapes=[pltpu.VMEM((tm + 8, 2 * D_SSM), f32)],
        compiler_params=pltpu.CompilerParams(dimension_semantics=("parallel", "arbitrary"),
                                             vmem_limit_bytes=VMEM_LIMIT),
        name="inproj",
    )(x, mod4, mod4, w_main, ws, conv_w, conv_b)


def _ssd_kernel(z_ref, xs_ref, bc_ref, dtf_ref, pc_ref, pe_ref, y_ref, cumc_ref, state, carry, *, lc):
    j = pl.program_id(1)

    @pl.when(j == 0)
    def _init():
        state[...] = jnp.zeros_like(state)
        carry[...] = jnp.zeros_like(carry)

    xs = xs_ref[0]
    bm = bc_ref[0, :, 0:256]
    cm = bc_ref[0, :, 256:512]

    dtf = dtf_ref[0]
    lane = lax.broadcasted_iota(i32, (lc, 128), 1)
    dt_c = _softplus(dtf + pc_ref[0:1, :])
    a_c = dt_c * (-jnp.exp(pc_ref[1:2, :]))
    logf = -_softplus(-(dtf + pc_ref[2:3, :]))
    v = jnp.where(lane < 8, a_c, logf)
    r_i = lax.broadcasted_iota(i32, (lc, lc), 0)
    c_i = lax.broadcasted_iota(i32, (lc, lc), 1)
    tri = r_i >= c_i
    tri_b = jnp.where(tri, 1.0, 0.0).astype(bf16)
    cum = _dot_exact_lhs(tri_b, v) + carry[...]
    carry[...] = jnp.where(lane[0:1, :] >= 8, cum[lc - 1:lc, :], 0.0)
    cumc_ref[0] = cum
    cs_t = cum.T[0:8, :]

    e_r = lax.broadcasted_iota(i32, (128, D_SSM), 0)
    e_c = lax.broadcasted_iota(i32, (128, D_SSM), 1)
    expand = jnp.where(jnp.right_shift(e_c, 6) == e_r, 1.0, 0.0).astype(bf16)
    dt_e = _dot_exact_rhs(dt_c, expand)
    cs_e = _dot_exact_rhs(cum, expand)

    xdt = xs * dt_e
    ecs = jnp.exp(cs_e)
    cs_last = cs_e[lc - 1:lc, :]
    dec_st = jnp.exp(cs_last - cs_e)
    lane_g = lax.broadcasted_iota(i32, (1, GROUP_W), 1)
    ys = []
    for g in range(2):
        gs = slice(g * GROUP_W, (g + 1) * GROUP_W)
        bg = bm[:, g * N_STATE:(g + 1) * N_STATE]
        cg = cm[:, g * N_STATE:(g + 1) * N_STATE]
        cb = _dot_nt(cg, bg)
        xdt_g = xdt[:, gs]
        xdt_gb = xdt_g.astype(bf16)
        ms, xb = [], []
        for hh in range(4):
            h = g * 4 + hh
            lm = jnp.exp(jnp.where(tri, cum[:, h:h + 1] - cs_t[h:h + 1, :], -jnp.inf))
            ms.append((cb * lm).astype(bf16))
            xb.append(jnp.where(jnp.right_shift(lane_g, 6) == hh, xdt_gb, jnp.zeros_like(xdt_gb)))
        y_diag = _dot(jnp.concatenate(ms, axis=1), jnp.concatenate(xb, axis=0))
        st = state[g]
        y_off = _dot(cg, st.astype(bf16)) * ecs[:, gs]
        upd = _dot_tn(bg, (xdt_g * dec_st[:, gs]).astype(bf16))
        state[g] = st * jnp.exp(cs_last[:, gs]) + upd
        ys.append(y_diag + y_off + xs[:, gs] * pe_ref[2:3, gs])

    outs = []
    for g in range(2):
        gs = slice(g * GROUP_W, (g + 1) * GROUP_W)
        yg = ys[g] * _silu(z_ref[0, :, gs])
        ms_ = jnp.mean(yg * yg, axis=-1, keepdims=True)
        outs.append(yg * lax.rsqrt(ms_ + EPS))
    y_ref[0] = (jnp.concatenate(outs, axis=1) * pe_ref[3:4, :]).astype(bf16)


def _ssd(zx, bc, dtf, pc, pe):
    bsz, t, _ = zx.shape
    lc = min(SSD_CHUNK, t)
    col = lambda k: pl.BlockSpec((1, lc, 512), lambda b, j, k=k: (b, j, k))
    full = lambda shape: pl.BlockSpec(shape, lambda b, j: (0,) * len(shape))
    return pl.pallas_call(
        functools.partial(_ssd_kernel, lc=lc),
        grid=(bsz, t // lc),
        in_specs=[col(0), col(1), col(0),
                  pl.BlockSpec((1, lc, 128), lambda b, j: (b, j, 0)),
                  full((8, 128)), full((8, D_SSM))],
        out_specs=[pl.BlockSpec((1, lc, D_SSM), lambda b, j: (b, j, 0)),
                   pl.BlockSpec((1, lc, 128), lambda b, j: (b, j, 0))],
        out_shape=[jax.ShapeDtypeStruct((bsz, t, D_SSM), bf16),
                   jax.ShapeDtypeStruct((bsz, t, 128), f32)],
        scratch_shapes=[pltpu.VMEM((2, N_STATE, GROUP_W), f32),
                        pltpu.VMEM((1, 128), f32)],
        compiler_params=pltpu.CompilerParams(dimension_semantics=("parallel", "arbitrary"),
                                             vmem_limit_bytes=VMEM_LIMIT),
        name="ssd",
    )(zx, zx, bc, dtf, pc, pe)


def _attn_kernel(q_ref, k_ref, v_ref, cc_ref, psel_ref, ng_ref, o_ref, kaug, vt, acc, sc0, sc1, *, tq, t):
    i = pl.program_id(1)
    nkb = t // tq
    n_heads = D_ATT // HEAD_DIM
    lane = lax.broadcasted_iota(i32, (1, 128), 1)
    lo_half = lane < HEAD_DIM

    @pl.when(i == 0)
    def _build():
        eye = jnp.where(lax.broadcasted_iota(i32, (D_ATT, D_ATT), 0) == lax.broadcasted_iota(i32, (D_ATT, D_ATT), 1),
                        1.0, 0.0).astype(bf16)
        ones_rows = jnp.where(lax.broadcasted_iota(i32, (V_ROWS - HEAD_DIM, tq), 0) == 0, 1.0, 0.0).astype(bf16)
        for jb in range(nkb):
            rows = slice(jb * tq, (jb + 1) * tq)
            v_t = _dot_nt(eye, v_ref[0, rows, :]).astype(bf16)
            for h in range(n_heads):
                vt[jb, h * V_ROWS:h * V_ROWS + HEAD_DIM, :] = v_t[h * HEAD_DIM:(h + 1) * HEAD_DIM, :]
                vt[jb, h * V_ROWS + HEAD_DIM:(h + 1) * V_ROWS, :] = ones_rows
            pieces = jnp.concatenate(_split3(cc_ref[0, rows, :] * (-LOG2E)), axis=1)
            for p in range(n_heads // 2):
                a = _dot(pieces, psel_ref[p]).astype(bf16)
                kp = k_ref[0, rows, p * 128:(p + 1) * 128]
                kaug[2 * p, rows, :] = jnp.where(lo_half, kp, a)
                kaug[2 * p + 1, rows, :] = jnp.where(lo_half, a, kp)

    ones_hi = jnp.where((lane >= HEAD_DIM) & (lane < HEAD_DIM + 3), 1.0, 0.0).astype(bf16)
    ones_lo = jnp.where(lane < 3, 1.0, 0.0).astype(bf16)
    qa = []
    for p in range(n_heads // 2):
        qp = q_ref[0, :, p * 128:(p + 1) * 128]
        qa.append(jnp.where(lo_half, qp, ones_hi))
        qa.append(jnp.where(lo_half, ones_lo, qp))
    keep = lax.broadcasted_iota(i32, (tq, tq), 0) <= lax.broadcasted_iota(i32, (tq, tq), 1)
    acc[...] = jnp.zeros_like(acc)

    def score(jb, buf):
        k0 = pl.multiple_of(jb * tq, tq)
        for h in range(n_heads):
            buf[h] = _dot_nt(kaug[h, pl.ds(k0, tq), :], qa[h])

    def absorb(jb, ms, masked, buf):
        new_ms = []
        for h in range(n_heads):
            s = buf[h]
            if masked:
                s = jnp.where(keep, s, NEG)
            m_new = jnp.maximum(ms[h], jnp.max(s, axis=0, keepdims=True))
            alpha = jnp.exp2(ms[h] - m_new)
            p = jnp.exp2(s - m_new).astype(bf16)
            new_ms.append(m_new)
            acc[h] = acc[h] * alpha + _dot(vt[jb, h * V_ROWS:(h + 1) * V_ROWS, :], p)
        return tuple(new_ms)

    def pair(pp, ms):
        j0 = 2 * pp
        score(j0 + 1, sc1)
        ms = absorb(j0, ms, False, sc0)
        score(j0 + 2, sc0)
        return absorb(j0 + 1, ms, False, sc1)

    score(0, sc0)
    ms = lax.fori_loop(0, i // 2, pair, tuple(jnp.full((1, tq), NEG, f32) for _ in range(n_heads)))

    @pl.when(i % 2 == 0)
    def _even():
        absorb(i, ms, True, sc0)

    @pl.when(i % 2 == 1)
    def _odd():
        score(i, sc1)
        absorb(i, absorb(i - 1, ms, False, sc0), True, sc1)

    out_t = jnp.concatenate([acc[h, 0:HEAD_DIM, :] * (1.0 / acc[h, HEAD_DIM:HEAD_DIM + 1, :]) for h in range(n_heads)],
                            axis=0)
    ms_ = jnp.mean(out_t * out_t, axis=0, keepdims=True)
    out_t = out_t * lax.rsqrt(ms_ + EPS)
    o_ref[0] = (out_t.T * ng_ref[...]).astype(bf16)


def _piece_select():
    sel = np.zeros((4, 384, 128), np.float32)
    for pair in range(4):
        for j in range(3):
            sel[pair, j * 128 + 8 + 2 * pair, HEAD_DIM + j] = 1.0
            sel[pair, j * 128 + 8 + 2 * pair + 1, j] = 1.0
    return jnp.asarray(sel, bf16)


def _attn(qkv, cumc, norm_g):
    bsz, t, _ = qkv.shape
    tq = min(ATT_BLOCK, t)
    return pl.pallas_call(
        functools.partial(_attn_kernel, tq=tq, t=t),
        grid=(bsz, t // tq),
        in_specs=[pl.BlockSpec((1, tq, D_ATT), lambda b, i: (b, i, 0)),
                  pl.BlockSpec((1, t, D_ATT), lambda b, i: (b, 0, 1)),
                  pl.BlockSpec((1, t, D_ATT), lambda b, i: (b, 0, 2)),
                  pl.BlockSpec((1, t, 128), lambda b, i: (b, 0, 0)),
                  pl.BlockSpec((4, 384, 128), lambda b, i: (0, 0, 0)),
                  pl.BlockSpec((1, D_ATT), lambda b, i: (0, 0))],
        out_specs=pl.BlockSpec((1, tq, D_ATT), lambda b, i: (b, i, 0)),
        out_shape=jax.ShapeDtypeStruct((bsz, t, D_ATT), bf16),
        scratch_shapes=[pltpu.VMEM((D_ATT // HEAD_DIM, t, 128), bf16),
                        pltpu.VMEM((t // tq, (D_ATT // HEAD_DIM) * V_ROWS, tq), bf16),
                        pltpu.VMEM((D_ATT // HEAD_DIM, V_ROWS, tq), f32),
                        pltpu.VMEM((D_ATT // HEAD_DIM, tq, tq), f32),
                        pltpu.VMEM((D_ATT // HEAD_DIM, tq, tq), f32)],
        compiler_params=pltpu.CompilerParams(dimension_semantics=("parallel", "arbitrary"),
                                             vmem_limit_bytes=VMEM_LIMIT),
        name="attn",
    )(qkv, qkv, qkv, cumc, _piece_select(), norm_g)


def _layer_norm(y, g, b):
    mu = jnp.mean(y, axis=-1, keepdims=True)
    yc = y - mu
    var = jnp.mean(yc * yc, axis=-1, keepdims=True)
    return yc * lax.rsqrt(var + EPS) * g + b


def _outproj_kernel(ys_ref, ya_ref, x_ref, gt_ref, sc_ref, sh_ref, wo_ref, lng_ref, lnb_ref, wrh_ref, wrl_ref, rb_ref,
                    x1_ref, rt_ref, dest_ref, cnt_ref, xrows_hbm,
                    carry, u2t, didx_v, didx_s, cnt_v, cnt_s, zeros, idx_sem, sc_sem, z_sem, *, tm, cap, blk):
    i = pl.program_id(0)
    last = pl.num_programs(0) - 1
    cur = i % 2
    prev = 1 - cur

    def idx_ready(slot):
        return pltpu.make_async_copy(didx_v.at[slot], didx_s.at[slot], idx_sem.at[slot])

    def dispatched(slot):
        return pltpu.make_async_copy(u2t.at[slot], u2t.at[slot], sc_sem.at[slot])

    def dispatch_copy(slot, r, k):
        return pltpu.make_async_copy(u2t.at[slot, pl.ds(r * ROW_TILE, ROW_TILE)], xrows_hbm.at[didx_s[slot, k, r]],
                                     sc_sem.at[slot])

    @pl.when(i == 0)
    def _init():
        carry[...] = jnp.zeros_like(carry)

    @pl.when(i >= 2)
    def _reuse():
        dispatched(cur).wait()
        dispatched(cur).wait()

    def step(dispatch_prev):
        n_parts = OUTPROJ_PARTS
        n_slices = 3 * n_parts + 2

        def dispatch_slice(c):
            if dispatch_prev:
                for r_ in range(c * tm // n_slices, (c + 1) * tm // n_slices):
                    dispatch_copy(prev, r_, 0).start(priority=0)
                    dispatch_copy(prev, r_, 1).start(priority=1)

        if dispatch_prev:
            idx_ready(prev).wait()
        part = tm // n_parts
        hs = []
        for a in range(n_parts):
            rs = slice(a * part, (a + 1) * part)
            dispatch_slice(a)
            hs.append(_dot(ys_ref[rs, :], wo_ref[0:D_SSM, :]) + _dot(ya_ref[rs, :], wo_ref[D_SSM:D, :]))
        logit_parts = []
        for a in range(n_parts):
            rs = slice(a * part, (a + 1) * part)
            dispatch_slice(n_parts + 2 * a)
            x1 = _layer_norm(ALPHA * x_ref[rs, :] + (1.0 + gt_ref[0, 0]) * hs[a], lng_ref[...], lnb_ref[...])
            x1_ref[rs, :] = x1
            u2 = x1 * (1.0 + sc_ref[0, 0]) + sh_ref[0, 0]
            _store_row_tiles(u2t.at[cur, pl.ds(a * part * ROW_TILE, part * ROW_TILE)], u2)
            dispatch_slice(n_parts + 2 * a + 1)
            uh = u2.astype(bf16)
            ul = (u2 - uh.astype(f32)).astype(bf16)
            logit_parts.append((_dot(uh, wrh_ref[...]) + _dot(ul, wrh_ref[...])) + _dot(uh, wrl_ref[...]))
        logits = jnp.concatenate(logit_parts, axis=0) + rb_ref[...]
        lane = lax.broadcasted_iota(i32, (tm, 128), 1).astype(f32)
        big = jnp.float32(1e9)

        def first_max(vals):
            m = jnp.max(vals, axis=-1, keepdims=True)
            return m, jnp.min(jnp.where(vals == m, lane, big), axis=-1, keepdims=True)

        gl = jnp.where(lane < N_GROUPS_R, logits, NEG)
        gmax, gidx = first_max(gl)
        g_p = 1.0 / jnp.sum(jnp.exp(gl - gmax), axis=-1, keepdims=True)
        lo = N_GROUPS_R + EXPERTS_PER_GROUP * gidx
        el = jnp.where((lane >= lo) & (lane < lo + EXPERTS_PER_GROUP), logits, NEG)
        m1, i1 = first_max(el)
        el2 = jnp.where(lane == i1, NEG, el)
        m2, i2 = first_max(el2)
        r = jnp.exp(m2 - m1)
        w1 = g_p / (1.0 + r)
        w2 = g_p * r / (1.0 + r)
        dispatch_slice(3 * n_parts)

        oh1 = lane == i1
        oh2 = lane == i2
        oh = jnp.where(oh1 | oh2, 1.0, 0.0)
        r_i = lax.broadcasted_iota(i32, (tm, tm), 0)
        c_i = lax.broadcasted_iota(i32, (tm, tm), 1)
        lower = jnp.where(r_i > c_i, 1.0, 0.0).astype(bf16)
        prefix = _dot(lower, oh.astype(bf16)) + carry[...]
        rank1 = jnp.sum(jnp.where(oh1, prefix, 0.0), axis=-1, keepdims=True)
        rank2 = jnp.sum(jnp.where(oh2, prefix, 0.0), axis=-1, keepdims=True)
        carry[...] = carry[...] + jnp.sum(oh, axis=0, keepdims=True)
        cnt_ref[...] = jnp.broadcast_to(carry[...], (8, 128))
        dispatch_slice(3 * n_parts + 1)

        rt = jnp.where(lane == 4, w1, jnp.where(lane == 5, w2, 0.0))
        rt_ref[...] = rt

        d1 = (i1 - N_GROUPS_R) * cap + rank1
        d2 = (i2 - N_GROUPS_R) * cap + rank2
        dd = jnp.where(lane == 0, d1, jnp.where(lane == 1, d2, 0.0)).T[0:8, :].astype(i32)
        dest_ref[...] = dd
        didx_v[cur] = dd
        idx_ready(cur).start()

    pl.when(i == 0)(lambda: step(False))
    pl.when(i > 0)(lambda: step(True))

    @pl.when(i == last)
    def _epilogue():
        idx_ready(cur).wait()

        def issue(r_, c):
            dispatch_copy(cur, r_, 0).start()
            dispatch_copy(cur, r_, 1).start()
            return c
        lax.fori_loop(0, tm, issue, 0)

        cnt_v[...] = jnp.broadcast_to(carry[...], (8, 128)).astype(i32)
        to_smem = pltpu.make_async_copy(cnt_v, cnt_s, z_sem)
        to_smem.start()
        to_smem.wait()
        zeros[...] = jnp.zeros_like(zeros)

        def pad_copy(e):
            return pltpu.make_async_copy(zeros, xrows_hbm.at[pl.ds(e * cap + cnt_s[0, N_GROUPS_R + e], blk)], z_sem)

        def pad_start(e, c):
            pad_copy(e).start()
            return c

        def pad_wait(e, c):
            pad_copy(e).wait()
            return c
        lax.fori_loop(0, N_EXPERTS, pad_start, 0)
        lax.fori_loop(0, N_EXPERTS, pad_wait, 0)

        @pl.when(i > 0)
        def _prev_done():
            dispatched(prev).wait()
            dispatched(prev).wait()
        dispatched(cur).wait()
        dispatched(cur).wait()


def _outproj(y_ssm, y_att, x, mod4, w_out, ln_g, ln_b, wr_hi, wr_lo, rb, t, cap, blk):
    n = x.shape[0]
    tm = min(OUTPROJ_ROWS, t)
    nt = t // tm
    vec = lambda k: pl.BlockSpec((1, 1, 1, D), lambda i, k=k: (k, i // nt, 0, 0))
    full = lambda shape: pl.BlockSpec(shape, lambda i: (0,) * len(shape))
    rows = lambda w: pl.BlockSpec((tm, w), lambda i: (i, 0))
    return pl.pallas_call(
        functools.partial(_outproj_kernel, tm=tm, cap=cap, blk=blk),
        grid=(n // tm,),
        in_specs=[rows(D_SSM), rows(D_ATT), rows(D), vec(2), vec(4), vec(3),
                  full((D, D)), full((1, D)), full((1, D)), full((D, 128)), full((D, 128)), full((1, 128))],
        out_specs=[rows(D), rows(128), pl.BlockSpec((8, tm), lambda i: (0, i)), full((8, 128)),
                   pl.BlockSpec(memory_space=pl.ANY)],
        out_shape=[jax.ShapeDtypeStruct((n, D), f32), jax.ShapeDtypeStruct((n, 128), f32),
                   jax.ShapeDtypeStruct((8, n), i32), jax.ShapeDtypeStruct((8, 128), f32),
                   jax.ShapeDtypeStruct((N_EXPERTS * cap, ROW_TILE, 128), f32)],
        scratch_shapes=[pltpu.VMEM((1, 128), f32),
                        pltpu.VMEM((2, tm * ROW_TILE, 128), f32),
                        pltpu.VMEM((2, 8, tm), i32), pltpu.SMEM((2, 8, tm), i32),
                        pltpu.VMEM((8, 128), i32), pltpu.SMEM((8, 128), i32),
                        pltpu.VMEM((blk, ROW_TILE, 128), f32),
                        pltpu.SemaphoreType.DMA((2,)), pltpu.SemaphoreType.DMA((2,)), pltpu.SemaphoreType.DMA(())],
        compiler_params=pltpu.CompilerParams(dimension_semantics=("arbitrary",), vmem_limit_bytes=VMEM_LIMIT),
        name="outproj",
    )(y_ssm, y_att, x, mod4, mod4, mod4, w_out, ln_g, ln_b, wr_hi, wr_lo, rb)


def _tile_copy(src_hbm, row, buf, slot, sem):
    return pltpu.make_async_copy(src_hbm.at[row], buf.at[pl.ds(slot * ROW_TILE, ROW_TILE)], sem)


def _moe_kernel(be_ref, br_ref, nu_ref, nxt_ref, par_ref, x_ref, wg_hbm, wu_hbm, wd_hbm, y_ref,
                wfg, wfu, wfd, wgb, wub, wdb, wsem, *, blk):
    i = pl.program_id(0)
    used = i < nu_ref[0]

    def fetch(e, slot):
        return [pltpu.make_async_copy(src.at[e], dst.at[slot], wsem.at[slot])
                for src, dst in ((wg_hbm, wfg), (wu_hbm, wfu), (wd_hbm, wfd))]

    @pl.when(i == 0)
    def _first():
        for c in fetch(be_ref[0], par_ref[0]):
            c.start()

    @pl.when(jnp.logical_and(used, jnp.logical_or(i == 0, be_ref[i] != be_ref[jnp.maximum(i - 1, 0)])))
    def _switch():
        slot = par_ref[i]
        for c in fetch(be_ref[i], slot):
            c.wait()

        @pl.when(nxt_ref[i] >= 0)
        def _next():
            for c in fetch(nxt_ref[i], 1 - slot):
                c.start()
        wgb[...] = wfg[slot].astype(bf16)
        wub[...] = wfu[slot].astype(bf16)
        wdb[...] = wfd[slot].astype(bf16)

    @pl.when(used)
    def _work():
        xb = _load_row_tiles(x_ref, blk).astype(bf16)
        hid = (_silu(_dot(xb, wgb[...])) * _dot(xb, wub[...])).astype(bf16)
        _store_row_tiles(y_ref, _dot(hid, wdb[...]))


def _moe(block_e, block_row, n_used, next_e, parity, x_rows, w_gate, w_up, w_down, blk):
    nblk = block_e.shape[0]
    rspec = pl.BlockSpec((blk * ROW_TILE, 128), lambda i, be, br, *_: (br[i], 0))
    anyspec = pl.BlockSpec(memory_space=pl.ANY)
    grid_spec = pltpu.PrefetchScalarGridSpec(
        num_scalar_prefetch=5,
        grid=(nblk,),
        in_specs=[rspec, anyspec, anyspec, anyspec],
        out_specs=rspec,
        scratch_shapes=[pltpu.VMEM((2, D, D_EXPERT), f32), pltpu.VMEM((2, D, D_EXPERT), f32),
                        pltpu.VMEM((2, D_EXPERT, D), f32),
                        pltpu.VMEM((D, D_EXPERT), bf16), pltpu.VMEM((D, D_EXPERT), bf16),
                        pltpu.VMEM((D_EXPERT, D), bf16),
                        pltpu.SemaphoreType.DMA((2,))],
    )
    return pl.pallas_call(
        functools.partial(_moe_kernel, blk=blk),
        grid_spec=grid_spec,
        out_shape=jax.ShapeDtypeStruct(x_rows.shape, f32),
        compiler_params=pltpu.CompilerParams(dimension_semantics=("arbitrary",), vmem_limit_bytes=VMEM_LIMIT),
        name="moe",
    )(block_e, block_row, n_used, next_e, parity, x_rows, w_gate, w_up, w_down)


def _combine_kernel(d1_ref, d2_ref, y_hbm, x1_ref, rt_ref, gt_ref, lng_ref, lnb_ref, o_ref, buf, sem, *, tm):
    i = pl.program_id(0)
    last = pl.num_programs(0) - 1

    @pl.when(i == 0)
    def _prologue():
        for s in range(2):
            def issue(r, c, s=s):
                base = jnp.minimum(s, last) * tm
                _tile_copy(y_hbm, d1_ref[base + r], buf.at[s, 0], r, sem.at[s]).start()
                _tile_copy(y_hbm, d2_ref[base + r], buf.at[s, 1], r, sem.at[s]).start()
                return c
            lax.fori_loop(0, tm, issue, 0)

    def rows_ready(slot):
        return pltpu.make_async_copy(buf.at[slot], buf.at[slot], sem.at[slot])

    slot = i % 3
    nxt = (i + 2) % 3
    base = jnp.minimum(i + 2, last) * tm
    rows_ready(slot).wait()
    for r in range(tm):
        _tile_copy(y_hbm, d1_ref[base + r], buf.at[nxt, 0], r, sem.at[nxt]).start(priority=0)
        _tile_copy(y_hbm, d2_ref[base + r], buf.at[nxt, 1], r, sem.at[nxt]).start(priority=1)
    rt = rt_ref[...]
    moe = rt[:, 4:5] * _load_row_tiles(buf.at[slot, 0], tm) + rt[:, 5:6] * _load_row_tiles(buf.at[slot, 1], tm)
    y = ALPHA * x1_ref[...] + (1.0 + gt_ref[0, 0]) * moe
    o_ref[...] = _layer_norm(y, lng_ref[...], lnb_ref[...])

    @pl.when(i == last)
    def _drain():
        rows_ready((i + 1) % 3).wait()
        rows_ready((i + 2) % 3).wait()


def _combine(dest1, dest2, y_rows, x1, rt, mod4, ln_g, ln_b, t):
    n = x1.shape[0]
    tm = min(COMBINE_ROWS, t)
    nt = t // tm
    full = lambda shape: pl.BlockSpec(shape, lambda i, d1, d2: (0,) * len(shape))
    rows = lambda w: pl.BlockSpec((tm, w), lambda i, d1, d2: (i, 0))
    grid_spec = pltpu.PrefetchScalarGridSpec(
        num_scalar_prefetch=2,
        grid=(n // tm,),
        in_specs=[pl.BlockSpec(memory_space=pl.ANY), rows(D), rows(128),
                  pl.BlockSpec((1, 1, 1, D), lambda i, d1, d2: (5, i // nt, 0, 0)),
                  full((1, D)), full((1, D))],
        out_specs=rows(D),
        scratch_shapes=[pltpu.VMEM((3, 2, tm * ROW_TILE, 128), f32), pltpu.SemaphoreType.DMA((3,))],
    )
    return pl.pallas_call(
        functools.partial(_combine_kernel, tm=tm),
        grid_spec=grid_spec,
        out_shape=jax.ShapeDtypeStruct((n, D), f32),
        compiler_params=pltpu.CompilerParams(dimension_semantics=("arbitrary",), vmem_limit_bytes=VMEM_LIMIT),
        name="combine",
    )(dest1, dest2, y_rows, x1, rt, mod4, ln_g, ln_b)


def _hi_lo(w):
    hi = w.astype(bf16)
    return hi, (w - hi.astype(f32)).astype(bf16)


def _pad_lanes(v, offset, width):
    return jnp.zeros((width,), f32).at[offset:offset + v.shape[0]].set(v)


def _layer(x, c, ada_w, ada_b, w_in, conv_w, conv_b, dt_bias, a_log, d_skip, ssm_norm_g, fg_bias, att_norm_g,
           w_out, ln1_g, ln1_b, router_g_w, router_g_b, router_e_w, router_e_b, w_gate, w_up, w_down, ln2_g, ln2_b):
    bsz, t, _ = x.shape
    n = bsz * t

    mod4 = _ada(c, ada_w, ada_b).reshape(6, bsz, 1, D)

    w_main = jnp.concatenate([w_in[:, 0:1536], w_in[:, 1544:3080]], axis=1).astype(bf16)
    w_small = jnp.concatenate([w_in[:, 1536:1544], w_in[:, 3080:3088], jnp.zeros((D, 112), f32)], axis=1)
    zx, bc, qkv, dtf = _inproj(x, mod4, w_main, jnp.concatenate(_hi_lo(w_small), axis=1), conv_w,
                               conv_b.reshape(1, -1))

    pc = jnp.stack([_pad_lanes(dt_bias, 0, 128), _pad_lanes(a_log, 0, 128), _pad_lanes(fg_bias, 8, 128)]
                   + [jnp.zeros((128,), f32)] * 5)
    rep = lambda v: jnp.repeat(v, HEAD_DIM)
    pe = jnp.stack([rep(dt_bias), rep(a_log), rep(d_skip), ssm_norm_g] + [jnp.zeros((D_SSM,), f32)] * 4)
    y_ssm, cumc = _ssd(zx, bc, dtf, pc, pe)

    y_att = _attn(qkv, cumc, att_norm_g.reshape(1, -1))

    wr = jnp.concatenate([router_g_w, router_e_w, jnp.zeros((D, 128 - N_GROUPS_R - N_EXPERTS), f32)], axis=1)
    wr_hi, wr_lo = _hi_lo(wr)
    rb = jnp.concatenate([router_g_b, router_e_b, jnp.zeros((128 - N_GROUPS_R - N_EXPERTS,), f32)]).reshape(1, 128)
    blk = MOE_ROWS
    cap = n + blk
    x1, rt, dest, cnt, x_rows = _outproj(y_ssm.reshape(n, D_SSM), y_att.reshape(n, D_ATT), x.reshape(n, D), mod4,
                                         w_out.astype(bf16), ln1_g.reshape(1, D), ln1_b.reshape(1, D),
                                         wr_hi, wr_lo, rb, t, cap, blk)

    counts = cnt[0, N_GROUPS_R:N_GROUPS_R + N_EXPERTS].astype(i32)
    nb = (counts + blk - 1) // blk
    nb_end = jnp.cumsum(nb)
    n_used = nb_end[N_EXPERTS - 1:]
    nblk = (2 * n) // blk + N_EXPERTS
    step = jnp.minimum(jnp.arange(nblk, dtype=i32), n_used[0] - 1)
    owner = (nb_end[None, :] <= step[:, None]).astype(i32)
    block_e = jnp.sum(owner, axis=1)
    first = jnp.sum(owner * nb[None, :], axis=1)
    block_row = block_e * (cap // blk) + (step - first)
    run_end = first + jnp.sum((jnp.arange(N_EXPERTS)[None, :] == block_e[:, None]) * nb[None, :], axis=1)
    next_e = jnp.where(run_end < n_used[0], jnp.sum((nb_end[None, :] <= run_end[:, None]).astype(i32), axis=1), -1)
    parity = jnp.sum(((jnp.arange(N_EXPERTS)[None, :] < block_e[:, None]) & (nb[None, :] > 0)).astype(i32), axis=1) % 2

    y_rows = _moe(block_e, block_row, n_used, next_e, parity, x_rows.reshape(N_EXPERTS * cap * ROW_TILE, 128),
                  w_gate, w_up, w_down, blk)
    out = _combine(dest[0], dest[1], y_rows.reshape(N_EXPERTS * cap, ROW_TILE, 128), x1, rt, mod4,
                   ln2_g.reshape(1, D), ln2_b.reshape(1, D), t)
    return out.reshape(bsz, t, D)


def kernel(x, c, ada_w, ada_b, w_in, conv_w, conv_b, dt_bias, a_log, d_skip, ssm_norm_g, fg_bias, att_norm_g, w_out,
           ln1_g, ln1_b, router_g_w, router_g_b, router_e_w, router_e_b, w_gate, w_up, w_down, ln2_g, ln2_b):
    depth = ada_w.shape[0]
    for l in range(depth):
        x = _layer(x, c, ada_w[l], ada_b[l], w_in[l], conv_w[l], conv_b[l], dt_bias[l], a_log[l], d_skip[l],
                   ssm_norm_g[l], fg_bias[l], att_norm_g[l], w_out[l], ln1_g[l], ln1_b[l], router_g_w[l],
                   router_g_b[l], router_e_w[l], router_e_b[l], w_gate[l], w_up[l], w_down[l], ln2_g[l], ln2_b[l])
    return x
```

```python
import functools

import jax
import jax.numpy as jnp
import numpy as np
from jax import lax
from jax.experimental import pallas as pl
from jax.experimental.pallas import tpu as pltpu

f32 = jnp.float32
bf16 = jnp.bfloat16
i32 = jnp.int32

D = 1024
D_SSM = 512
D_ATT = 512
HEAD_DIM = 64
GROUP_W = 256
N_STATE = 128
CONV_K = 4
N_GROUPS_R = 4
EXPERTS_PER_GROUP = 8
N_EXPERTS = 32
D_EXPERT = 512
ALPHA = 2.0 ** 0.25
EPS = 1e-5
NEG = -1e30
LOG2E = 1.4426950408889634
QK_SCALE = HEAD_DIM ** -0.5 * LOG2E
V_ROWS = 80

SSD_CHUNK = 256
ATT_BLOCK = 256
INPROJ_ROWS = 512
OUTPROJ_ROWS = 512
OUTPROJ_PARTS = 2
MOE_ROWS = 256
COMBINE_ROWS = 256
DEST_ROWS = 2048
VMEM_LIMIT = 48 * 1024 * 1024


def _dot(a, b):
    return jnp.dot(a, b, preferred_element_type=f32)


def _dot_nt(a, b):
    return lax.dot_general(a, b, (((1,), (1,)), ((), ())), preferred_element_type=f32)


def _dot_tn(a, b):
    return lax.dot_general(a, b, (((0,), (0,)), ((), ())), preferred_element_type=f32)


def _split3(v):
    hi = v.astype(bf16)
    r1 = v - hi.astype(f32)
    mid = r1.astype(bf16)
    lo = (r1 - mid.astype(f32)).astype(bf16)
    return hi, mid, lo


def _dot_exact_lhs(m, v):
    hi, mid, lo = _split3(v)
    return (_dot(m, hi) + _dot(m, mid)) + _dot(m, lo)


def _dot_exact_rhs(v, m):
    hi, mid, lo = _split3(v)
    return (_dot(hi, m) + _dot(mid, m)) + _dot(lo, m)


ROW_TILE = 8


def _store_row_tiles(ref, val):
    rows = val.shape[0]
    for c in range(ROW_TILE):
        ref[pl.ds(c, rows, stride=ROW_TILE), :] = val[:, c * 128:(c + 1) * 128]


def _load_row_tiles(ref, rows):
    return jnp.concatenate([ref[pl.ds(c, rows, stride=ROW_TILE), :] for c in range(ROW_TILE)], axis=1)


def _softplus(x):
    return jnp.maximum(x, 0.0) + jnp.log1p(jnp.exp(-jnp.abs(x)))


def _silu(x):
    return x * jax.nn.sigmoid(x)


def _ada_kernel(c_ref, w_ref, b_ref, o_ref):
    s = _silu(c_ref[...]).astype(bf16)
    o_ref[0] = _dot(s, w_ref[...].astype(bf16)) + b_ref[0]


def _ada(c, w, b):
    bsz = c.shape[0]
    return pl.pallas_call(
        _ada_kernel,
        grid=(6,),
        in_specs=[pl.BlockSpec((bsz, D), lambda j: (0, 0)),
                  pl.BlockSpec((D, D), lambda j: (0, j)),
                  pl.BlockSpec((1, 1, D), lambda j: (j, 0, 0))],
        out_specs=pl.BlockSpec((1, bsz, D), lambda j: (j, 0, 0)),
        out_shape=jax.ShapeDtypeStruct((6, bsz, D), f32),
        compiler_params=pltpu.CompilerParams(dimension_semantics=("arbitrary",), vmem_limit_bytes=VMEM_LIMIT),
        name="ada",
    )(c, w, b.reshape(6, 1, D))


def _inproj_kernel(x_ref, sc_ref, sh_ref, wm_ref, ws_ref, cw_ref, cb_ref, zx_ref, bc_ref, qkv_ref, dtf_ref, xcat, *, tm):
    i = pl.program_id(1)

    @pl.when(i == 0)
    def _init():
        xcat[0:8, :] = jnp.zeros((8, 2 * D_SSM), f32)

    u = x_ref[0] * (1.0 + sc_ref[0, 0]) + sh_ref[0, 0]
    ub = u.astype(bf16)
    xcat[8:8 + tm, 0:512] = _dot(ub, wm_ref[:, 512:1024])
    xcat[8:8 + tm, 512:1024] = _dot(ub, wm_ref[:, 1024:1536])
    zx_ref[0, :, 0:512] = _dot(ub, wm_ref[:, 0:512])
    acc = cw_ref[0:1, :] * xcat[5:5 + tm, :] + cb_ref[...]
    for k in range(1, CONV_K):
        acc = acc + cw_ref[k:k + 1, :] * xcat[5 + k:5 + k + tm, :]
    xcat[0:8, :] = xcat[tm:tm + 8, :]
    xbc = _silu(acc)
    zx_ref[0, :, 512:1024] = xbc[:, 0:512]
    bc_ref[0] = xbc[:, 512:1024].astype(bf16)
    for j, scale in enumerate((QK_SCALE, 1.0, 1.0)):
        qkv_ref[0, :, j * 512:(j + 1) * 512] = (
            _dot(ub, wm_ref[:, 1536 + j * 512:1536 + (j + 1) * 512]) * scale).astype(bf16)
    ul = (u - ub.astype(f32)).astype(bf16)
    d_hl = _dot(ub, ws_ref[...])
    dtf_ref[0] = (d_hl[:, 0:128] + _dot(ul, ws_ref[:, 0:128])) + d_hl[:, 128:256]


def _inproj(x, mod4, w_main, ws, conv_w, conv_b):
    bsz, t, _ = x.shape
    tm = min(INPROJ_ROWS, t)
    vec = lambda k: pl.BlockSpec((1, 1, 1, D), lambda b, i, k=k: (k, b, 0, 0))
    full = lambda shape: pl.BlockSpec(shape, lambda b, i: (0,) * len(shape))
    rows = lambda w: pl.BlockSpec((1, tm, w), lambda b, i: (b, i, 0))
    return pl.pallas_call(
        functools.partial(_inproj_kernel, tm=tm),
        grid=(bsz, t // tm),
        in_specs=[rows(D), vec(1), vec(0), full((D, 3072)), full((D, 256)),
                  full((CONV_K, 2 * D_SSM)), full((1, 2 * D_SSM))],
        out_specs=[rows(1024), rows(512), rows(1536), rows(128)],
        out_shape=[jax.ShapeDtypeStruct((bsz, t, 1024), f32),
                   jax.ShapeDtypeStruct((bsz, t, 512), bf16),
                   jax.ShapeDtypeStruct((bsz, t, 1536), bf16),
                   jax.ShapeDtypeStruct((bsz, t, 128), f32)],
        scratch_shapes=[pltpu.VMEM((tm + 8, 2 * D_SSM), f32)],
        compiler_params=pltpu.CompilerParams(dimension_semantics=("parallel", "arbitrary"),
                                             vmem_limit_bytes=VMEM_LIMIT),
        name="inproj",
    )(x, mod4, mod4, w_main, ws, conv_w, conv_b)


def _ssd_kernel(z_ref, xs_ref, bc_ref, dtf_ref, pc_ref, pe_ref, y_ref, cumc_ref, state, carry, *, lc):
    j = pl.program_id(1)

    @pl.when(j == 0)
    def _init():
        state[...] = jnp.zeros_like(state)
        carry[...] = jnp.zeros_like(carry)

    xs = xs_ref[0]
    bm = bc_ref[0, :, 0:256]
    cm = bc_ref[0, :, 256:512]

    dtf = dtf_ref[0]
    lane = lax.broadcasted_iota(i32, (lc, 128), 1)
    dt_c = _softplus(dtf + pc_ref[0:1, :])
    a_c = dt_c * (-jnp.exp(pc_ref[1:2, :]))
    logf = -_softplus(-(dtf + pc_ref[2:3, :]))
    v = jnp.where(lane < 8, a_c, logf)
    r_i = lax.broadcasted_iota(i32, (lc, lc), 0)
    c_i = lax.broadcasted_iota(i32, (lc, lc), 1)
    tri = r_i >= c_i
    tri_b = jnp.where(tri, 1.0, 0.0).astype(bf16)
    cum = _dot_exact_lhs(tri_b, v) + carry[...]
    carry[...] = jnp.where(lane[0:1, :] >= 8, cum[lc - 1:lc, :], 0.0)
    cumc_ref[0] = cum
    cs_t = cum.T[0:8, :]

    e_r = lax.broadcasted_iota(i32, (128, D_SSM), 0)
    e_c = lax.broadcasted_iota(i32, (128, D_SSM), 1)
    expand = jnp.where(jnp.right_shift(e_c, 6) == e_r, 1.0, 0.0).astype(bf16)
    dt_e = _dot_exact_rhs(dt_c, expand)
    cs_e = _dot_exact_rhs(cum, expand)

    xdt = xs * dt_e
    ecs = jnp.exp(cs_e)
    cs_last = cs_e[lc - 1:lc, :]
    dec_st = jnp.exp(cs_last - cs_e)
    lane_g = lax.broadcasted_iota(i32, (1, GROUP_W), 1)
    ys = []
    for g in range(2):
        gs = slice(g * GROUP_W, (g + 1) * GROUP_W)
        bg = bm[:, g * N_STATE:(g + 1) * N_STATE]
        cg = cm[:, g * N_STATE:(g + 1) * N_STATE]
        cb = _dot_nt(cg, bg)
        xdt_g = xdt[:, gs]
        xdt_gb = xdt_g.astype(bf16)
        ms, xb = [], []
        for hh in range(4):
            h = g * 4 + hh
            lm = jnp.exp(jnp.where(tri, cum[:, h:h + 1] - cs_t[h:h + 1, :], -jnp.inf))
            ms.append((cb * lm).astype(bf16))
            xb.append(jnp.where(jnp.right_shift(lane_g, 6) == hh, xdt_gb, jnp.zeros_like(xdt_gb)))
        y_diag = _dot(jnp.concatenate(ms, axis=1), jnp.concatenate(xb, axis=0))
        st = state[g]
        y_off = _dot(cg, st.astype(bf16)) * ecs[:, gs]
        upd = _dot_tn(bg, (xdt_g * dec_st[:, gs]).astype(bf16))
        state[g] = st * jnp.exp(cs_last[:, gs]) + upd
        ys.append(y_diag + y_off + xs[:, gs] * pe_ref[2:3, gs])

    outs = []
    for g in range(2):
        gs = slice(g * GROUP_W, (g + 1) * GROUP_W)
        yg = ys[g] * _silu(z_ref[0, :, gs])
        ms_ = jnp.mean(yg * yg, axis=-1, keepdims=True)
        outs.append(yg * lax.rsqrt(ms_ + EPS))
    y_ref[0] = (jnp.concatenate(outs, axis=1) * pe_ref[3:4, :]).astype(bf16)


def _ssd(zx, bc, dtf, pc, pe):
    bsz, t, _ = zx.shape
    lc = min(SSD_CHUNK, t)
    col = lambda k: pl.BlockSpec((1, lc, 512), lambda b, j, k=k: (b, j, k))
    full = lambda shape: pl.BlockSpec(shape, lambda b, j: (0,) * len(shape))
    return pl.pallas_call(
        functools.partial(_ssd_kernel, lc=lc),
        grid=(bsz, t // lc),
        in_specs=[col(0), col(1), col(0),
                  pl.BlockSpec((1, lc, 128), lambda b, j: (b, j, 0)),
                  full((8, 128)), full((8, D_SSM))],
        out_specs=[pl.BlockSpec((1, lc, D_SSM), lambda b, j: (b, j, 0)),
                   pl.BlockSpec((1, lc, 128), lambda b, j: (b, j, 0))],
        out_shape=[jax.ShapeDtypeStruct((bsz, t, D_SSM), bf16),
                   jax.ShapeDtypeStruct((bsz, t, 128), f32)],
        scratch_shapes=[pltpu.VMEM((2, N_STATE, GROUP_W), f32),
                        pltpu.VMEM((1, 128), f32)],
        compiler_params=pltpu.CompilerParams(dimension_semantics=("parallel", "arbitrary"),
                                             vmem_limit_bytes=VMEM_LIMIT),
        name="ssd",
    )(zx, zx, bc, dtf, pc, pe)


def _attn_kernel(q_ref, k_ref, v_ref, cc_ref, psel_ref, ng_ref, o_ref, kaug, vt, acc, sc0, sc1, *, tq, t):
    i = pl.program_id(1)
    nkb = t // tq
    n_heads = D_ATT // HEAD_DIM
    lane = lax.broadcasted_iota(i32, (1, 128), 1)
    lo_half = lane < HEAD_DIM

    @pl.when(i == 0)
    def _build():
        eye = jnp.where(lax.broadcasted_iota(i32, (D_ATT, D_ATT), 0) == lax.broadcasted_iota(i32, (D_ATT, D_ATT), 1),
                        1.0, 0.0).astype(bf16)
        ones_rows = jnp.where(lax.broadcasted_iota(i32, (V_ROWS - HEAD_DIM, tq), 0) == 0, 1.0, 0.0).astype(bf16)
        for jb in range(nkb):
            rows = slice(jb * tq, (jb + 1) * tq)
            v_t = _dot_nt(eye, v_ref[0, rows, :]).astype(bf16)
            for h in range(n_heads):
                vt[jb, h * V_ROWS:h * V_ROWS + HEAD_DIM, :] = v_t[h * HEAD_DIM:(h + 1) * HEAD_DIM, :]
                vt[jb, h * V_ROWS + HEAD_DIM:(h + 1) * V_ROWS, :] = ones_rows
            pieces = jnp.concatenate(_split3(cc_ref[0, rows, :] * (-LOG2E)), axis=1)
            for p in range(n_heads // 2):
                a = _dot(pieces, psel_ref[p]).astype(bf16)
                kp = k_ref[0, rows, p * 128:(p + 1) * 128]
                kaug[2 * p, rows, :] = jnp.where(lo_half, kp, a)
                kaug[2 * p + 1, rows, :] = jnp.where(lo_half, a, kp)

    ones_hi = jnp.where((lane >= HEAD_DIM) & (lane < HEAD_DIM + 3), 1.0, 0.0).astype(bf16)
    ones_lo = jnp.where(lane < 3, 1.0, 0.0).astype(bf16)
    qa = []
    for p in range(n_heads // 2):
        qp = q_ref[0, :, p * 128:(p + 1) * 128]
        qa.append(jnp.where(lo_half, qp, ones_hi))
        qa.append(jnp.where(lo_half, ones_lo, qp))
    keep = lax.broadcasted_iota(i32, (tq, tq), 0) <= lax.broadcasted_iota(i32, (tq, tq), 1)
    acc[...] = jnp.zeros_like(acc)

    def score(jb, buf):
        k0 = pl.multiple_of(jb * tq, tq)
        for h in range(n_heads):
            buf[h] = _dot_nt(kaug[h, pl.ds(k0, tq), :], qa[h])

    def absorb(jb, ms, masked, buf):
        new_ms = []
        for h in range(n_heads):
            s = buf[h]
            if masked:
                s = jnp.where(keep, s, NEG)
            m_new = jnp.maximum(ms[h], jnp.max(s, axis=0, keepdims=True))
            alpha = jnp.exp2(ms[h] - m_new)
            p = jnp.exp2(s - m_new).astype(bf16)
            new_ms.append(m_new)
            acc[h] = acc[h] * alpha + _dot(vt[jb, h * V_ROWS:(h + 1) * V_ROWS, :], p)
        return tuple(new_ms)

    def pair(pp, ms):
        j0 = 2 * pp
        score(j0 + 1, sc1)
        ms = absorb(j0, ms, False, sc0)
        score(j0 + 2, sc0)
        return absorb(j0 + 1, ms, False, sc1)

    score(0, sc0)
    ms = lax.fori_loop(0, i // 2, pair, tuple(jnp.full((1, tq), NEG, f32) for _ in range(n_heads)))

    @pl.when(i % 2 == 0)
    def _even():
        absorb(i, ms, True, sc0)

    @pl.when(i % 2 == 1)
    def _odd():
        score(i, sc1)
        absorb(i, absorb(i - 1, ms, False, sc0), True, sc1)

    out_t = jnp.concatenate([acc[h, 0:HEAD_DIM, :] * (1.0 / acc[h, HEAD_DIM:HEAD_DIM + 1, :]) for h in range(n_heads)],
                            axis=0)
    ms_ = jnp.mean(out_t * out_t, axis=0, keepdims=True)
    out_t = out_t * lax.rsqrt(ms_ + EPS)
    o_ref[0] = (out_t.T * ng_ref[...]).astype(bf16)


def _piece_select():
    sel = np.zeros((4, 384, 128), np.float32)
    for pair in range(4):
        for j in range(3):
            sel[pair, j * 128 + 8 + 2 * pair, HEAD_DIM + j] = 1.0
            sel[pair, j * 128 + 8 + 2 * pair + 1, j] = 1.0
    return jnp.asarray(sel, bf16)


def _attn(qkv, cumc, norm_g):
    bsz, t, _ = qkv.shape
    tq = min(ATT_BLOCK, t)
    return pl.pallas_call(
        functools.partial(_attn_kernel, tq=tq, t=t),
        grid=(bsz, t // tq),
        in_specs=[pl.BlockSpec((1, tq, D_ATT), lambda b, i: (b, i, 0)),
                  pl.BlockSpec((1, t, D_ATT), lambda b, i: (b, 0, 1)),
                  pl.BlockSpec((1, t, D_ATT), lambda b, i: (b, 0, 2)),
                  pl.BlockSpec((1, t, 128), lambda b, i: (b, 0, 0)),
                  pl.BlockSpec((4, 384, 128), lambda b, i: (0, 0, 0)),
                  pl.BlockSpec((1, D_ATT), lambda b, i: (0, 0))],
        out_specs=pl.BlockSpec((1, tq, D_ATT), lambda b, i: (b, i, 0)),
        out_shape=jax.ShapeDtypeStruct((bsz, t, D_ATT), bf16),
        scratch_shapes=[pltpu.VMEM((D_ATT // HEAD_DIM, t, 128), bf16),
                        pltpu.VMEM((t // tq, (D_ATT // HEAD_DIM) * V_ROWS, tq), bf16),
                        pltpu.VMEM((D_ATT // HEAD_DIM, V_ROWS, tq), f32),
                        pltpu.VMEM((D_ATT // HEAD_DIM, tq, tq), f32),
                        pltpu.VMEM((D_ATT // HEAD_DIM, tq, tq), f32)],
        compiler_params=pltpu.CompilerParams(dimension_semantics=("parallel", "arbitrary"),
                                             vmem_limit_bytes=VMEM_LIMIT),
        name="attn",
    )(qkv, qkv, qkv, cumc, _piece_select(), norm_g)


def _layer_norm(y, g, b):
    mu = jnp.mean(y, axis=-1, keepdims=True)
    yc = y - mu
    var = jnp.mean(yc * yc, axis=-1, keepdims=True)
    return yc * lax.rsqrt(var + EPS) * g + b


def _outproj_kernel(ys_ref, ya_ref, x_ref, gt_ref, sc_ref, sh_ref, wo_ref, lng_ref, lnb_ref, wrh_ref, wrl_ref, rb_ref,
                    x1_ref, rt_ref, dest_ref, cnt_ref, tbl_ref, xrows_hbm,
                    carry, galloc, tbl, u2t, didx_v, didx_s, cnt_v, cnt_s, zeros, idx_sem, sc_sem, z_sem,
                    *, tm, blk, n_blk):
    i = pl.program_id(0)
    last = pl.num_programs(0) - 1
    cur = i % 2
    prev = 1 - cur

    def idx_ready(slot):
        return pltpu.make_async_copy(didx_v.at[slot], didx_s.at[slot], idx_sem.at[slot])

    def dispatched(slot):
        return pltpu.make_async_copy(u2t.at[slot], u2t.at[slot], sc_sem.at[slot])

    def dispatch_copy(slot, r, k):
        return pltpu.make_async_copy(u2t.at[slot, pl.ds(r * ROW_TILE, ROW_TILE)], xrows_hbm.at[didx_s[slot, k, r]],
                                     sc_sem.at[slot])

    @pl.when(i == 0)
    def _init():
        carry[...] = jnp.zeros_like(carry)
        galloc[...] = jnp.zeros_like(galloc)
        tbl[...] = jnp.zeros_like(tbl)

    @pl.when(i >= 2)
    def _reuse():
        dispatched(cur).wait()
        dispatched(cur).wait()

    def step(dispatch_prev):
        n_parts = OUTPROJ_PARTS
        n_slices = 3 * n_parts + 2

        def dispatch_slice(c):
            if dispatch_prev:
                for r_ in range(c * tm // n_slices, (c + 1) * tm // n_slices):
                    dispatch_copy(prev, r_, 0).start(priority=0)
                    dispatch_copy(prev, r_, 1).start(priority=1)

        if dispatch_prev:
            idx_ready(prev).wait()
        part = tm // n_parts
        hs = []
        for a in range(n_parts):
            rs = slice(a * part, (a + 1) * part)
            dispatch_slice(a)
            hs.append(_dot(ys_ref[rs, :], wo_ref[0:D_SSM, :]) + _dot(ya_ref[rs, :], wo_ref[D_SSM:D, :]))
        logit_parts = []
        for a in range(n_parts):
            rs = slice(a * part, (a + 1) * part)
            dispatch_slice(n_parts + 2 * a)
            x1 = _layer_norm(ALPHA * x_ref[rs, :] + (1.0 + gt_ref[0, 0]) * hs[a], lng_ref[...], lnb_ref[...])
            x1_ref[rs, :] = x1
            u2 = x1 * (1.0 + sc_ref[0, 0]) + sh_ref[0, 0]
            _store_row_tiles(u2t.at[cur, pl.ds(a * part * ROW_TILE, part * ROW_TILE)], u2)
            dispatch_slice(n_parts + 2 * a + 1)
            uh = u2.astype(bf16)
            ul = (u2 - uh.astype(f32)).astype(bf16)
            logit_parts.append((_dot(uh, wrh_ref[...]) + _dot(ul, wrh_ref[...])) + _dot(uh, wrl_ref[...]))
        logits = jnp.concatenate(logit_parts, axis=0) + rb_ref[...]
        lane = lax.broadcasted_iota(i32, (tm, 128), 1).astype(f32)
        big = jnp.float32(1e9)

        def first_max(vals):
            m = jnp.max(vals, axis=-1, keepdims=True)
            return m, jnp.min(jnp.where(vals == m, lane, big), axis=-1, keepdims=True)

        gl = jnp.where(lane < N_GROUPS_R, logits, NEG)
        gmax, gidx = first_max(gl)
        g_p = 1.0 / jnp.sum(jnp.exp(gl - gmax), axis=-1, keepdims=True)
        lo = N_GROUPS_R + EXPERTS_PER_GROUP * gidx
        el = jnp.where((lane >= lo) & (lane < lo + EXPERTS_PER_GROUP), logits, NEG)
        m1, i1 = first_max(el)
        el2 = jnp.where(lane == i1, NEG, el)
        m2, i2 = first_max(el2)
        r = jnp.exp(m2 - m1)
        w1 = g_p / (1.0 + r)
        w2 = g_p * r / (1.0 + r)
        dispatch_slice(3 * n_parts)

        oh1 = lane == i1
        oh2 = lane == i2
        oh = jnp.where(oh1 | oh2, 1.0, 0.0)
        r_i = lax.broadcasted_iota(i32, (tm, tm), 0)
        c_i = lax.broadcasted_iota(i32, (tm, tm), 1)
        lower = jnp.where(r_i > c_i, 1.0, 0.0).astype(bf16)
        c_old = carry[...]
        prefix = _dot(lower, oh.astype(bf16)) + c_old
        rank1 = jnp.sum(jnp.where(oh1, prefix, 0.0), axis=-1, keepdims=True)
        rank2 = jnp.sum(jnp.where(oh2, prefix, 0.0), axis=-1, keepdims=True)
        c_new = c_old + jnp.sum(oh, axis=0, keepdims=True)
        carry[...] = c_new
        dispatch_slice(3 * n_parts + 1)

        nb_old = jnp.floor((c_old + (blk - 1)) * (1.0 / blk))
        nb_new = jnp.floor((c_new + (blk - 1)) * (1.0 / blk))
        fresh = nb_new - nb_old
        sq_r = lax.broadcasted_iota(i32, (128, 128), 0)
        sq_c = lax.broadcasted_iota(i32, (128, 128), 1)
        before = jnp.where(sq_r < sq_c, 1.0, 0.0).astype(bf16)
        base = galloc[...] + _dot(jnp.broadcast_to(fresh, (8, 128)).astype(bf16), before)[0:1, :]
        galloc[...] = galloc[...] + jnp.sum(fresh, axis=-1, keepdims=True)
        ordinal = sq_r.astype(f32)
        tbl[...] = jnp.where((ordinal >= nb_old) & (ordinal < nb_new), base + (ordinal - nb_old), tbl[...])
        tbl_ref[...] = tbl[...]
        cnt_ref[...] = jnp.concatenate([c_new, galloc[...], jnp.zeros((6, 128), f32)], axis=0)

        rt = jnp.where(lane == 4, w1, jnp.where(lane == 5, w2, 0.0))
        rt_ref[...] = rt

        ids_bf = tbl[...].astype(bf16)

        def row_of(onehot, rank):
            nth = jnp.floor(rank * (1.0 / blk))
            ids = _dot(jnp.where(lane == nth, 1.0, 0.0).astype(bf16), ids_bf)
            return jnp.sum(jnp.where(onehot, ids, 0.0), axis=-1, keepdims=True) * blk + (rank - nth * blk)

        d1 = row_of(oh1, rank1)
        d2 = row_of(oh2, rank2)
        dd = jnp.where(lane == 0, d1, jnp.where(lane == 1, d2, 0.0)).T[0:8, :].astype(i32)
        dest_ref[...] = dd
        didx_v[cur] = dd
        idx_ready(cur).start()

    pl.when(i == 0)(lambda: step(False))
    pl.when(i > 0)(lambda: step(True))

    @pl.when(i == last)
    def _epilogue():
        idx_ready(cur).wait()

        def issue(r_, c):
            dispatch_copy(cur, r_, 0).start()
            dispatch_copy(cur, r_, 1).start()
            return c
        lax.fori_loop(0, tm, issue, 0)

        cnt = carry[...]
        n_blocks = jnp.floor((cnt + (blk - 1)) * (1.0 / blk))
        ordinal = lax.broadcasted_iota(i32, (128, 128), 0).astype(f32)
        last_id = jnp.sum(jnp.where(ordinal == n_blocks - 1.0, tbl[...], 0.0), axis=0, keepdims=True)
        used = cnt - (n_blocks - 1.0) * blk
        cnt_v[...] = jnp.concatenate([last_id * blk + used, blk - used, galloc[...], jnp.zeros((5, 128), f32)],
                                     axis=0).astype(i32)
        to_smem = pltpu.make_async_copy(cnt_v, cnt_s, z_sem)
        to_smem.start()
        to_smem.wait()
        zeros[...] = jnp.zeros_like(zeros)
        sizes = [1 << b for b in reversed(range(blk.bit_length() - 1))]

        def for_each_fill(fn):
            def tail(e, c):
                pad = cnt_s[1, N_GROUPS_R + e]
                off = cnt_s[0, N_GROUPS_R + e]
                for sz in sizes:
                    @pl.when((pad & sz) != 0)
                    def _(off=off, sz=sz):
                        fn(pltpu.make_async_copy(zeros.at[pl.ds(0, sz)], xrows_hbm.at[pl.ds(off, sz)], z_sem))
                    off = off + (pad & sz)
                return c
            lax.fori_loop(0, N_EXPERTS, tail, 0)

            def whole(b, c):
                fn(pltpu.make_async_copy(zeros, xrows_hbm.at[pl.ds(b * blk, blk)], z_sem))
                return c
            lax.fori_loop(cnt_s[2, 0], n_blk, whole, 0)

        for_each_fill(lambda copy: copy.start())
        for_each_fill(lambda copy: copy.wait())

        @pl.when(i > 0)
        def _prev_done():
            dispatched(prev).wait()
            dispatched(prev).wait()
        dispatched(cur).wait()
        dispatched(cur).wait()


def _outproj(y_ssm, y_att, x, mod4, w_out, ln_g, ln_b, wr_hi, wr_lo, rb, t, blk, n_blk):
    n = x.shape[0]
    tm = min(OUTPROJ_ROWS, t)
    nt = t // tm
    assert n // blk + 1 <= 128 and n_blk <= 256, "block-id table: 128 blocks per expert, ids exact in bf16"
    vec = lambda k: pl.BlockSpec((1, 1, 1, D), lambda i, k=k: (k, i // nt, 0, 0))
    full = lambda shape: pl.BlockSpec(shape, lambda i: (0,) * len(shape))
    rows = lambda w: pl.BlockSpec((tm, w), lambda i: (i, 0))
    return pl.pallas_call(
        functools.partial(_outproj_kernel, tm=tm, blk=blk, n_blk=n_blk),
        grid=(n // tm,),
        in_specs=[rows(D_SSM), rows(D_ATT), rows(D), vec(2), vec(4), vec(3),
                  full((D, D)), full((1, D)), full((1, D)), full((D, 128)), full((D, 128)), full((1, 128))],
        out_specs=[rows(D), rows(128), pl.BlockSpec((8, tm), lambda i: (0, i)), full((8, 128)), full((128, 128)),
                   pl.BlockSpec(memory_space=pl.ANY)],
        out_shape=[jax.ShapeDtypeStruct((n, D), f32), jax.ShapeDtypeStruct((n, 128), f32),
                   jax.ShapeDtypeStruct((8, n), i32), jax.ShapeDtypeStruct((8, 128), f32),
                   jax.ShapeDtypeStruct((128, 128), f32),
                   jax.ShapeDtypeStruct((n_blk * blk, ROW_TILE, 128), f32)],
        scratch_shapes=[pltpu.VMEM((1, 128), f32), pltpu.VMEM((1, 128), f32), pltpu.VMEM((128, 128), f32),
                        pltpu.VMEM((2, tm * ROW_TILE, 128), f32),
                        pltpu.VMEM((2, 8, tm), i32), pltpu.SMEM((2, 8, tm), i32),
                        pltpu.VMEM((8, 128), i32), pltpu.SMEM((8, 128), i32),
                        pltpu.VMEM((blk, ROW_TILE, 128), f32),
                        pltpu.SemaphoreType.DMA((2,)), pltpu.SemaphoreType.DMA((2,)), pltpu.SemaphoreType.DMA(())],
        compiler_params=pltpu.CompilerParams(dimension_semantics=("arbitrary",), vmem_limit_bytes=VMEM_LIMIT),
        name="outproj",
    )(y_ssm, y_att, x, mod4, mod4, mod4, w_out, ln_g, ln_b, wr_hi, wr_lo, rb)


def _tile_copy(src_hbm, row, buf, slot, sem):
    return pltpu.make_async_copy(src_hbm.at[row], buf.at[pl.ds(slot * ROW_TILE, ROW_TILE)], sem)


def _moe_kernel(be_ref, br_ref, nu_ref, nxt_ref, par_ref, x_ref, wg_hbm, wu_hbm, wd_hbm, y_ref,
                wfg, wfu, wfd, wgb, wub, wdb, wsem, *, blk):
    i = pl.program_id(0)
    used = i < nu_ref[0]

    def fetch(e, slot):
        return [pltpu.make_async_copy(src.at[e], dst.at[slot], wsem.at[slot])
                for src, dst in ((wg_hbm, wfg), (wu_hbm, wfu), (wd_hbm, wfd))]

    @pl.when(i == 0)
    def _first():
        for c in fetch(be_ref[0], par_ref[0]):
            c.start()

    @pl.when(jnp.logical_and(used, jnp.logical_or(i == 0, be_ref[i] != be_ref[jnp.maximum(i - 1, 0)])))
    def _switch():
        slot = par_ref[i]
        for c in fetch(be_ref[i], slot):
            c.wait()

        @pl.when(nxt_ref[i] >= 0)
        def _next():
            for c in fetch(nxt_ref[i], 1 - slot):
                c.start()
        wgb[...] = wfg[slot].astype(bf16)
        wub[...] = wfu[slot].astype(bf16)
        wdb[...] = wfd[slot].astype(bf16)

    @pl.when(jnp.logical_not(used))
    def _spare():
        y_ref[...] = jnp.zeros_like(y_ref)

    @pl.when(used)
    def _work():
        xb = _load_row_tiles(x_ref, blk).astype(bf16)
        hid = (_silu(_dot(xb, wgb[...])) * _dot(xb, wub[...])).astype(bf16)
        _store_row_tiles(y_ref, _dot(hid, wdb[...]))


def _moe(block_e, block_row, n_used, next_e, parity, x_rows, w_gate, w_up, w_down, blk):
    nblk = block_e.shape[0]
    rspec = pl.BlockSpec((blk * ROW_TILE, 128), lambda i, be, br, *_: (br[i], 0))
    anyspec = pl.BlockSpec(memory_space=pl.ANY)
    grid_spec = pltpu.PrefetchScalarGridSpec(
        num_scalar_prefetch=5,
        grid=(nblk,),
        in_specs=[rspec, anyspec, anyspec, anyspec],
        out_specs=rspec,
        scratch_shapes=[pltpu.VMEM((2, D, D_EXPERT), f32), pltpu.VMEM((2, D, D_EXPERT), f32),
                        pltpu.VMEM((2, D_EXPERT, D), f32),
                        pltpu.VMEM((D, D_EXPERT), bf16), pltpu.VMEM((D, D_EXPERT), bf16),
                        pltpu.VMEM((D_EXPERT, D), bf16),
                        pltpu.SemaphoreType.DMA((2,))],
    )
    return pl.pallas_call(
        functools.partial(_moe_kernel, blk=blk),
        grid_spec=grid_spec,
        out_shape=jax.ShapeDtypeStruct(x_rows.shape, f32),
        compiler_params=pltpu.CompilerParams(dimension_semantics=("arbitrary",), vmem_limit_bytes=VMEM_LIMIT),
        name="moe",
    )(block_e, block_row, n_used, next_e, parity, x_rows, w_gate, w_up, w_down)


def _combine_kernel(d1_ref, d2_ref, y_hbm, x1_ref, rt_ref, gt_ref, lng_ref, lnb_ref, o_ref, buf, sem, *, tm):
    i = pl.program_id(0)
    last = pl.num_programs(0) - 1

    @pl.when(i == 0)
    def _prologue():
        for s in range(2):
            def issue(r, c, s=s):
                base = jnp.minimum(s, last) * tm
                _tile_copy(y_hbm, d1_ref[base + r], buf.at[s, 0], r, sem.at[s]).start()
                _tile_copy(y_hbm, d2_ref[base + r], buf.at[s, 1], r, sem.at[s]).start()
                return c
            lax.fori_loop(0, tm, issue, 0)

    def rows_ready(slot):
        return pltpu.make_async_copy(buf.at[slot], buf.at[slot], sem.at[slot])

    slot = i % 3
    nxt = (i + 2) % 3
    base = jnp.minimum(i + 2, last) * tm
    rows_ready(slot).wait()
    for r in range(tm):
        _tile_copy(y_hbm, d1_ref[base + r], buf.at[nxt, 0], r, sem.at[nxt]).start(priority=0)
        _tile_copy(y_hbm, d2_ref[base + r], buf.at[nxt, 1], r, sem.at[nxt]).start(priority=1)
    rt = rt_ref[...]
    moe = rt[:, 4:5] * _load_row_tiles(buf.at[slot, 0], tm) + rt[:, 5:6] * _load_row_tiles(buf.at[slot, 1], tm)
    y = ALPHA * x1_ref[...] + (1.0 + gt_ref[0, 0]) * moe
    o_ref[...] = _layer_norm(y, lng_ref[...], lnb_ref[...])

    @pl.when(i == last)
    def _drain():
        rows_ready((i + 1) % 3).wait()
        rows_ready((i + 2) % 3).wait()


def _combine(dest1, dest2, y_rows, x1, rt, mod4, ln_g, ln_b, t):
    n = x1.shape[0]
    tm = min(COMBINE_ROWS, t)
    nt = t // tm
    full = lambda shape: pl.BlockSpec(shape, lambda i, d1, d2: (0,) * len(shape))
    rows = lambda w: pl.BlockSpec((tm, w), lambda i, d1, d2: (i, 0))
    grid_spec = pltpu.PrefetchScalarGridSpec(
        num_scalar_prefetch=2,
        grid=(n // tm,),
        in_specs=[pl.BlockSpec(memory_space=pl.ANY), rows(D), rows(128),
                  pl.BlockSpec((1, 1, 1, D), lambda i, d1, d2: (5, i // nt, 0, 0)),
                  full((1, D)), full((1, D))],
        out_specs=rows(D),
        scratch_shapes=[pltpu.VMEM((3, 2, tm * ROW_TILE, 128), f32), pltpu.SemaphoreType.DMA((3,))],
    )
    return pl.pallas_call(
        functools.partial(_combine_kernel, tm=tm),
        grid_spec=grid_spec,
        out_shape=jax.ShapeDtypeStruct((n, D), f32),
        compiler_params=pltpu.CompilerParams(dimension_semantics=("arbitrary",), vmem_limit_bytes=VMEM_LIMIT),
        name="combine",
    )(dest1, dest2, y_rows, x1, rt, mod4, ln_g, ln_b)


def _hi_lo(w):
    hi = w.astype(bf16)
    return hi, (w - hi.astype(f32)).astype(bf16)


def _pad_lanes(v, offset, width):
    return jnp.zeros((width,), f32).at[offset:offset + v.shape[0]].set(v)


def _layer(x, c, ada_w, ada_b, w_in, conv_w, conv_b, dt_bias, a_log, d_skip, ssm_norm_g, fg_bias, att_norm_g,
           w_out, ln1_g, ln1_b, router_g_w, router_g_b, router_e_w, router_e_b, w_gate, w_up, w_down, ln2_g, ln2_b):
    bsz, t, _ = x.shape
    n = bsz * t

    mod4 = _ada(c, ada_w, ada_b).reshape(6, bsz, 1, D)

    w_main = jnp.concatenate([w_in[:, 0:1536], w_in[:, 1544:3080]], axis=1).astype(bf16)
    w_small = jnp.concatenate([w_in[:, 1536:1544], w_in[:, 3080:3088], jnp.zeros((D, 112), f32)], axis=1)
    zx, bc, qkv, dtf = _inproj(x, mod4, w_main, jnp.concatenate(_hi_lo(w_small), axis=1), conv_w,
                               conv_b.reshape(1, -1))

    pc = jnp.stack([_pad_lanes(dt_bias, 0, 128), _pad_lanes(a_log, 0, 128), _pad_lanes(fg_bias, 8, 128)]
                   + [jnp.zeros((128,), f32)] * 5)
    rep = lambda v: jnp.repeat(v, HEAD_DIM)
    pe = jnp.stack([rep(dt_bias), rep(a_log), rep(d_skip), ssm_norm_g] + [jnp.zeros((D_SSM,), f32)] * 4)
    y_ssm, cumc = _ssd(zx, bc, dtf, pc, pe)

    y_att = _attn(qkv, cumc, att_norm_g.reshape(1, -1))

    wr = jnp.concatenate([router_g_w, router_e_w, jnp.zeros((D, 128 - N_GROUPS_R - N_EXPERTS), f32)], axis=1)
    wr_hi, wr_lo = _hi_lo(wr)
    rb = jnp.concatenate([router_g_b, router_e_b, jnp.zeros((128 - N_GROUPS_R - N_EXPERTS,), f32)]).reshape(1, 128)
    blk = MOE_ROWS
    nblk = (2 * n) // blk + N_EXPERTS
    x1, rt, dest, cnt, tbl, x_rows = _outproj(y_ssm.reshape(n, D_SSM), y_att.reshape(n, D_ATT), x.reshape(n, D), mod4,
                                              w_out.astype(bf16), ln1_g.reshape(1, D), ln1_b.reshape(1, D),
                                              wr_hi, wr_lo, rb, t, blk, nblk)

    counts = cnt[0, N_GROUPS_R:N_GROUPS_R + N_EXPERTS].astype(i32)
    nb = (counts + blk - 1) // blk
    nb_end = jnp.cumsum(nb)
    n_used = nb_end[N_EXPERTS - 1:]
    every = jnp.arange(nblk, dtype=i32)
    step = jnp.minimum(every, n_used[0] - 1)
    owner = (nb_end[None, :] <= step[:, None]).astype(i32)
    block_e = jnp.sum(owner, axis=1)
    first = jnp.sum(owner * nb[None, :], axis=1)
    lanes = jnp.arange(128, dtype=i32)[None, :]
    nth_row = jnp.dot((lanes == (step - first)[:, None]).astype(f32), tbl, precision=lax.Precision.HIGHEST)
    block_id = jnp.sum(jnp.where(lanes == (block_e + N_GROUPS_R)[:, None], nth_row, 0.0), axis=1).astype(i32)
    block_row = jnp.where(every < n_used[0], block_id, every)
    run_end = first + jnp.sum((jnp.arange(N_EXPERTS)[None, :] == block_e[:, None]) * nb[None, :], axis=1)
    next_e = jnp.where(run_end < n_used[0], jnp.sum((nb_end[None, :] <= run_end[:, None]).astype(i32), axis=1), -1)
    parity = jnp.sum(((jnp.arange(N_EXPERTS)[None, :] < block_e[:, None]) & (nb[None, :] > 0)).astype(i32), axis=1) % 2

    y_rows = _moe(block_e, block_row, n_used, next_e, parity, x_rows.reshape(nblk * blk * ROW_TILE, 128),
                  w_gate, w_up, w_down, blk)
    out = _combine(dest[0], dest[1], y_rows.reshape(nblk * blk, ROW_TILE, 128), x1, rt, mod4,
                   ln2_g.reshape(1, D), ln2_b.reshape(1, D), t)
    return out.reshape(bsz, t, D)


def kernel(x, c, ada_w, ada_b, w_in, conv_w, conv_b, dt_bias, a_log, d_skip, ssm_norm_g, fg_bias, att_norm_g, w_out,
           ln1_g, ln1_b, router_g_w, router_g_b, router_e_w, router_e_b, w_gate, w_up, w_down, ln2_g, ln2_b):
    depth = ada_w.shape[0]
    for l in range(depth):
        x = _layer(x, c, ada_w[l], ada_b[l], w_in[l], conv_w[l], conv_b[l], dt_bias[l], a_log[l], d_skip[l],
                   ssm_norm_g[l], fg_bias[l], att_norm_g[l], w_out[l], ln1_g[l], ln1_b[l], router_g_w[l],
                   router_g_b[l], router_e_w[l], router_e_b[l], w_gate[l], w_up[l], w_down[l], ln2_g[l], ln2_b[l])
    return x
```

```python
import functools

import jax
import jax.numpy as jnp
import numpy as np
from jax import lax
from jax.experimental import pallas as pl
from jax.experimental.pallas import tpu as pltpu

f32 = jnp.float32
bf16 = jnp.bfloat16
i32 = jnp.int32

D = 1024
D_SSM = 512
D_ATT = 512
HEAD_DIM = 64
GROUP_W = 256
N_STATE = 128
CONV_K = 4
N_GROUPS_R = 4
EXPERTS_PER_GROUP = 8
N_EXPERTS = 32
D_EXPERT = 512
ALPHA = 2.0 ** 0.25
EPS = 1e-5
NEG = -1e30
LOG2E = 1.4426950408889634
QK_SCALE = HEAD_DIM ** -0.5 * LOG2E
V_ROWS = 80

SSD_CHUNK = 256
ATT_BLOCK = 256
INPROJ_ROWS = 512
OUTPROJ_ROWS = 512
OUTPROJ_PARTS = 2
MOE_ROWS = 256
COMBINE_ROWS = 256
DEST_ROWS = 2048
VMEM_LIMIT = 48 * 1024 * 1024


def _dot(a, b):
    return jnp.dot(a, b, preferred_element_type=f32)


def _dot_nt(a, b):
    return lax.dot_general(a, b, (((1,), (1,)), ((), ())), preferred_element_type=f32)


def _dot_tn(a, b):
    return lax.dot_general(a, b, (((0,), (0,)), ((), ())), preferred_element_type=f32)


def _split3(v):
    hi = v.astype(bf16)
    r1 = v - hi.astype(f32)
    mid = r1.astype(bf16)
    lo = (r1 - mid.astype(f32)).astype(bf16)
    return hi, mid, lo


def _dot_exact_lhs(m, v):
    hi, mid, lo = _split3(v)
    return (_dot(m, hi) + _dot(m, mid)) + _dot(m, lo)


def _dot_exact_rhs(v, m):
    hi, mid, lo = _split3(v)
    return (_dot(hi, m) + _dot(mid, m)) + _dot(lo, m)


ROW_TILE = 8


def _store_row_tiles(ref, val):
    rows = val.shape[0]
    for c in range(ROW_TILE):
        ref[pl.ds(c, rows, stride=ROW_TILE), :] = val[:, c * 128:(c + 1) * 128]


def _load_row_tiles(ref, rows):
    return jnp.concatenate([ref[pl.ds(c, rows, stride=ROW_TILE), :] for c in range(ROW_TILE)], axis=1)


def _softplus(x):
    return jnp.maximum(x, 0.0) + jnp.log1p(jnp.exp(-jnp.abs(x)))


def _silu(x):
    return x * jax.nn.sigmoid(x)


def _ada_kernel(c_ref, w_ref, b_ref, o_ref):
    s = _silu(c_ref[...]).astype(bf16)
    o_ref[0] = _dot(s, w_ref[...].astype(bf16)) + b_ref[0]


def _ada(c, w, b):
    bsz = c.shape[0]
    return pl.pallas_call(
        _ada_kernel,
        grid=(6,),
        in_specs=[pl.BlockSpec((bsz, D), lambda j: (0, 0)),
                  pl.BlockSpec((D, D), lambda j: (0, j)),
                  pl.BlockSpec((1, 1, D), lambda j: (j, 0, 0))],
        out_specs=pl.BlockSpec((1, bsz, D), lambda j: (j, 0, 0)),
        out_shape=jax.ShapeDtypeStruct((6, bsz, D), f32),
        compiler_params=pltpu.CompilerParams(dimension_semantics=("arbitrary",), vmem_limit_bytes=VMEM_LIMIT),
        name="ada",
    )(c, w, b.reshape(6, 1, D))


W_IN_COLS = 3088
PACK_ROWS = 256


def _pack_w_in_kernel(w_ref, wm_ref, ws_ref):
    w = w_ref[...]
    wm_ref[:, 0:1536] = w[:, 0:1536].astype(bf16)
    wm_ref[:, 1536:3072] = w[:, 1544:3080].astype(bf16)
    lane = lax.broadcasted_iota(i32, (1, 128), 1)
    small = jnp.where(lane < 8, w[:, 1536:1664],
                      jnp.where(lane < 16, pltpu.roll(w[:, W_IN_COLS - 128:W_IN_COLS], 16, axis=1), 0.0))
    hi = small.astype(bf16)
    ws_ref[:, 0:128] = hi
    ws_ref[:, 128:256] = (small - hi.astype(f32)).astype(bf16)


def _pack_w_in(w_in):
    return pl.pallas_call(
        _pack_w_in_kernel,
        grid=(D // PACK_ROWS,),
        in_specs=[pl.BlockSpec((PACK_ROWS, W_IN_COLS), lambda i: (i, 0))],
        out_specs=[pl.BlockSpec((PACK_ROWS, 3072), lambda i: (i, 0)), pl.BlockSpec((PACK_ROWS, 256), lambda i: (i, 0))],
        out_shape=[jax.ShapeDtypeStruct((D, 3072), bf16), jax.ShapeDtypeStruct((D, 256), bf16)],
        compiler_params=pltpu.CompilerParams(dimension_semantics=("arbitrary",), vmem_limit_bytes=VMEM_LIMIT),
        name="pack_w_in",
    )(w_in)


def _inproj_kernel(x_ref, sc_ref, sh_ref, wm_ref, ws_ref, cw_ref, cb_ref, zx_ref, bc_ref, qkv_ref, dtf_ref, xcat, *, tm):
    i = pl.program_id(1)

    @pl.when(i == 0)
    def _init():
        xcat[0:8, :] = jnp.zeros((8, 2 * D_SSM), f32)

    u = x_ref[0] * (1.0 + sc_ref[0, 0]) + sh_ref[0, 0]
    ub = u.astype(bf16)
    xcat[8:8 + tm, 0:512] = _dot(ub, wm_ref[:, 512:1024])
    xcat[8:8 + tm, 512:1024] = _dot(ub, wm_ref[:, 1024:1536])
    zx_ref[0, :, 0:512] = _dot(ub, wm_ref[:, 0:512])
    acc = cw_ref[0:1, :] * xcat[5:5 + tm, :] + cb_ref[...]
    for k in range(1, CONV_K):
        acc = acc + cw_ref[k:k + 1, :] * xcat[5 + k:5 + k + tm, :]
    xcat[0:8, :] = xcat[tm:tm + 8, :]
    xbc = _silu(acc)
    zx_ref[0, :, 512:1024] = xbc[:, 0:512]
    bc_ref[0] = xbc[:, 512:1024].astype(bf16)
    for j, scale in enumerate((QK_SCALE, 1.0, 1.0)):
        qkv_ref[0, :, j * 512:(j + 1) * 512] = (
            _dot(ub, wm_ref[:, 1536 + j * 512:1536 + (j + 1) * 512]) * scale).astype(bf16)
    ul = (u - ub.astype(f32)).astype(bf16)
    d_hl = _dot(ub, ws_ref[...])
    dtf_ref[0] = (d_hl[:, 0:128] + _dot(ul, ws_ref[:, 0:128])) + d_hl[:, 128:256]


def _inproj(x, mod4, w_main, ws, conv_w, conv_b):
    bsz, t, _ = x.shape
    tm = min(INPROJ_ROWS, t)
    vec = lambda k: pl.BlockSpec((1, 1, 1, D), lambda b, i, k=k: (k, b, 0, 0))
    full = lambda shape: pl.BlockSpec(shape, lambda b, i: (0,) * len(shape))
    rows = lambda w: pl.BlockSpec((1, tm, w), lambda b, i: (b, i, 0))
    return pl.pallas_call(
        functools.partial(_inproj_kernel, tm=tm),
        grid=(bsz, t // tm),
        in_specs=[rows(D), vec(1), vec(0), full((D, 3072)), full((D, 256)),
                  full((CONV_K, 2 * D_SSM)), full((1, 2 * D_SSM))],
        out_specs=[rows(1024), rows(512), rows(1536), rows(128)],
        out_shape=[jax.ShapeDtypeStruct((bsz, t, 1024), f32),
                   jax.ShapeDtypeStruct((bsz, t, 512), bf16),
                   jax.ShapeDtypeStruct((bsz, t, 1536), bf16),
                   jax.ShapeDtypeStruct((bsz, t, 128), f32)],
        scratch_shapes=[pltpu.VMEM((tm + 8, 2 * D_SSM), f32)],
        compiler_params=pltpu.CompilerParams(dimension_semantics=("parallel", "arbitrary"),
                                             vmem_limit_bytes=VMEM_LIMIT),
        name="inproj",
    )(x, mod4, mod4, w_main, ws, conv_w, conv_b)


def _ssd_kernel(z_ref, xs_ref, bc_ref, dtf_ref, pc_ref, pe_ref, y_ref, cumc_ref, state, carry, *, lc):
    j = pl.program_id(1)

    @pl.when(j == 0)
    def _init():
        state[...] = jnp.zeros_like(state)
        carry[...] = jnp.zeros_like(carry)

    xs = xs_ref[0]
    bm = bc_ref[0, :, 0:256]
    cm = bc_ref[0, :, 256:512]

    dtf = dtf_ref[0]
    lane = lax.broadcasted_iota(i32, (lc, 128), 1)
    dt_c = _softplus(dtf + pc_ref[0:1, :])
    a_c = dt_c * (-jnp.exp(pc_ref[1:2, :]))
    logf = -_softplus(-(dtf + pc_ref[2:3, :]))
    v = jnp.where(lane < 8, a_c, logf)
    r_i = lax.broadcasted_iota(i32, (lc, lc), 0)
    c_i = lax.broadcasted_iota(i32, (lc, lc), 1)
    tri = r_i >= c_i
    tri_b = jnp.where(tri, 1.0, 0.0).astype(bf16)
    cum = _dot_exact_lhs(tri_b, v) + carry[...]
    carry[...] = jnp.where(lane[0:1, :] >= 8, cum[lc - 1:lc, :], 0.0)
    cumc_ref[0] = cum
    cs_t = cum.T[0:8, :]

    e_r = lax.broadcasted_iota(i32, (128, D_SSM), 0)
    e_c = lax.broadcasted_iota(i32, (128, D_SSM), 1)
    expand = jnp.where(jnp.right_shift(e_c, 6) == e_r, 1.0, 0.0).astype(bf16)
    dt_e = _dot_exact_rhs(dt_c, expand)
    cs_e = _dot_exact_rhs(cum, expand)

    xdt = xs * dt_e
    ecs = jnp.exp(cs_e)
    cs_last = cs_e[lc - 1:lc, :]
    dec_st = jnp.exp(cs_last - cs_e)
    lane_g = lax.broadcasted_iota(i32, (1, GROUP_W), 1)
    ys = []
    for g in range(2):
        gs = slice(g * GROUP_W, (g + 1) * GROUP_W)
        bg = bm[:, g * N_STATE:(g + 1) * N_STATE]
        cg = cm[:, g * N_STATE:(g + 1) * N_STATE]
        cb = _dot_nt(cg, bg)
        xdt_g = xdt[:, gs]
        xdt_gb = xdt_g.astype(bf16)
        ms, xb = [], []
        for hh in range(4):
            h = g * 4 + hh
            lm = jnp.exp(jnp.where(tri, cum[:, h:h + 1] - cs_t[h:h + 1, :], -jnp.inf))
            ms.append((cb * lm).astype(bf16))
            xb.append(jnp.where(jnp.right_shift(lane_g, 6) == hh, xdt_gb, jnp.zeros_like(xdt_gb)))
        y_diag = _dot(jnp.concatenate(ms, axis=1), jnp.concatenate(xb, axis=0))
        st = state[g]
        y_off = _dot(cg, st.astype(bf16)) * ecs[:, gs]
        upd = _dot_tn(bg, (xdt_g * dec_st[:, gs]).astype(bf16))
        state[g] = st * jnp.exp(cs_last[:, gs]) + upd
        ys.append(y_diag + y_off + xs[:, gs] * pe_ref[2:3, gs])

    outs = []
    for g in range(2):
        gs = slice(g * GROUP_W, (g + 1) * GROUP_W)
        yg = ys[g] * _silu(z_ref[0, :, gs])
        ms_ = jnp.mean(yg * yg, axis=-1, keepdims=True)
        outs.append(yg * lax.rsqrt(ms_ + EPS))
    y_ref[0] = (jnp.concatenate(outs, axis=1) * pe_ref[3:4, :]).astype(bf16)


def _ssd(zx, bc, dtf, pc, pe):
    bsz, t, _ = zx.shape
    lc = min(SSD_CHUNK, t)
    col = lambda k: pl.BlockSpec((1, lc, 512), lambda b, j, k=k: (b, j, k))
    full = lambda shape: pl.BlockSpec(shape, lambda b, j: (0,) * len(shape))
    return pl.pallas_call(
        functools.partial(_ssd_kernel, lc=lc),
        grid=(bsz, t // lc),
        in_specs=[col(0), col(1), col(0),
                  pl.BlockSpec((1, lc, 128), lambda b, j: (b, j, 0)),
                  full((8, 128)), full((8, D_SSM))],
        out_specs=[pl.BlockSpec((1, lc, D_SSM), lambda b, j: (b, j, 0)),
                   pl.BlockSpec((1, lc, 128), lambda b, j: (b, j, 0))],
        out_shape=[jax.ShapeDtypeStruct((bsz, t, D_SSM), bf16),
                   jax.ShapeDtypeStruct((bsz, t, 128), f32)],
        scratch_shapes=[pltpu.VMEM((2, N_STATE, GROUP_W), f32),
                        pltpu.VMEM((1, 128), f32)],
        compiler_params=pltpu.CompilerParams(dimension_semantics=("parallel", "arbitrary"),
                                             vmem_limit_bytes=VMEM_LIMIT),
        name="ssd",
    )(zx, zx, bc, dtf, pc, pe)


def _attn_kernel(q_ref, k_ref, v_ref, cc_ref, psel_ref, ng_ref, o_ref, kaug, vt, acc, sc0, sc1, *, tq, t):
    i = pl.program_id(1)
    nkb = t // tq
    n_heads = D_ATT // HEAD_DIM
    lane = lax.broadcasted_iota(i32, (1, 128), 1)
    lo_half = lane < HEAD_DIM

    @pl.when(i == 0)
    def _build():
        eye = jnp.where(lax.broadcasted_iota(i32, (D_ATT, D_ATT), 0) == lax.broadcasted_iota(i32, (D_ATT, D_ATT), 1),
                        1.0, 0.0).astype(bf16)
        ones_rows = jnp.where(lax.broadcasted_iota(i32, (V_ROWS - HEAD_DIM, tq), 0) == 0, 1.0, 0.0).astype(bf16)
        for jb in range(nkb):
            rows = slice(jb * tq, (jb + 1) * tq)
            v_t = _dot_nt(eye, v_ref[0, rows, :]).astype(bf16)
            for h in range(n_heads):
                vt[jb, h * V_ROWS:h * V_ROWS + HEAD_DIM, :] = v_t[h * HEAD_DIM:(h + 1) * HEAD_DIM, :]
                vt[jb, h * V_ROWS + HEAD_DIM:(h + 1) * V_ROWS, :] = ones_rows
            pieces = jnp.concatenate(_split3(cc_ref[0, rows, :] * (-LOG2E)), axis=1)
            for p in range(n_heads // 2):
                a = _dot(pieces, psel_ref[p]).astype(bf16)
                kp = k_ref[0, rows, p * 128:(p + 1) * 128]
                kaug[2 * p, rows, :] = jnp.where(lo_half, kp, a)
                kaug[2 * p + 1, rows, :] = jnp.where(lo_half, a, kp)

    ones_hi = jnp.where((lane >= HEAD_DIM) & (lane < HEAD_DIM + 3), 1.0, 0.0).astype(bf16)
    ones_lo = jnp.where(lane < 3, 1.0, 0.0).astype(bf16)
    qa = []
    for p in range(n_heads // 2):
        qp = q_ref[0, :, p * 128:(p + 1) * 128]
        qa.append(jnp.where(lo_half, qp, ones_hi))
        qa.append(jnp.where(lo_half, ones_lo, qp))
    keep = lax.broadcasted_iota(i32, (tq, tq), 0) <= lax.broadcasted_iota(i32, (tq, tq), 1)
    acc[...] = jnp.zeros_like(acc)

    def score(jb, buf):
        k0 = pl.multiple_of(jb * tq, tq)
        for h in range(n_heads):
            buf[h] = _dot_nt(kaug[h, pl.ds(k0, tq), :], qa[h])

    def absorb(jb, ms, masked, buf):
        new_ms = []
        for h in range(n_heads):
            s = buf[h]
            if masked:
                s = jnp.where(keep, s, NEG)
            m_new = jnp.maximum(ms[h], jnp.max(s, axis=0, keepdims=True))
            alpha = jnp.exp2(ms[h] - m_new)
            p = jnp.exp2(s - m_new).astype(bf16)
            new_ms.append(m_new)
            acc[h] = acc[h] * alpha + _dot(vt[jb, h * V_ROWS:(h + 1) * V_ROWS, :], p)
        return tuple(new_ms)

    def pair(pp, ms):
        j0 = 2 * pp
        score(j0 + 1, sc1)
        ms = absorb(j0, ms, False, sc0)
        score(j0 + 2, sc0)
        return absorb(j0 + 1, ms, False, sc1)

    score(0, sc0)
    ms = lax.fori_loop(0, i // 2, pair, tuple(jnp.full((1, tq), NEG, f32) for _ in range(n_heads)))

    @pl.when(i % 2 == 0)
    def _even():
        absorb(i, ms, True, sc0)

    @pl.when(i % 2 == 1)
    def _odd():
        score(i, sc1)
        absorb(i, absorb(i - 1, ms, False, sc0), True, sc1)

    out_t = jnp.concatenate([acc[h, 0:HEAD_DIM, :] * (1.0 / acc[h, HEAD_DIM:HEAD_DIM + 1, :]) for h in range(n_heads)],
                            axis=0)
    ms_ = jnp.mean(out_t * out_t, axis=0, keepdims=True)
    out_t = out_t * lax.rsqrt(ms_ + EPS)
    o_ref[0] = (out_t.T * ng_ref[...]).astype(bf16)


def _piece_select():
    sel = np.zeros((4, 384, 128), np.float32)
    for pair in range(4):
        for j in range(3):
            sel[pair, j * 128 + 8 + 2 * pair, HEAD_DIM + j] = 1.0
            sel[pair, j * 128 + 8 + 2 * pair + 1, j] = 1.0
    return jnp.asarray(sel, bf16)


def _attn(qkv, cumc, norm_g):
    bsz, t, _ = qkv.shape
    tq = min(ATT_BLOCK, t)
    return pl.pallas_call(
        functools.partial(_attn_kernel, tq=tq, t=t),
        grid=(bsz, t // tq),
        in_specs=[pl.BlockSpec((1, tq, D_ATT), lambda b, i: (b, i, 0)),
                  pl.BlockSpec((1, t, D_ATT), lambda b, i: (b, 0, 1)),
                  pl.BlockSpec((1, t, D_ATT), lambda b, i: (b, 0, 2)),
                  pl.BlockSpec((1, t, 128), lambda b, i: (b, 0, 0)),
                  pl.BlockSpec((4, 384, 128), lambda b, i: (0, 0, 0)),
                  pl.BlockSpec((1, D_ATT), lambda b, i: (0, 0))],
        out_specs=pl.BlockSpec((1, tq, D_ATT), lambda b, i: (b, i, 0)),
        out_shape=jax.ShapeDtypeStruct((bsz, t, D_ATT), bf16),
        scratch_shapes=[pltpu.VMEM((D_ATT // HEAD_DIM, t, 128), bf16),
                        pltpu.VMEM((t // tq, (D_ATT // HEAD_DIM) * V_ROWS, tq), bf16),
                        pltpu.VMEM((D_ATT // HEAD_DIM, V_ROWS, tq), f32),
                        pltpu.VMEM((D_ATT // HEAD_DIM, tq, tq), f32),
                        pltpu.VMEM((D_ATT // HEAD_DIM, tq, tq), f32)],
        compiler_params=pltpu.CompilerParams(dimension_semantics=("parallel", "arbitrary"),
                                             vmem_limit_bytes=VMEM_LIMIT),
        name="attn",
    )(qkv, qkv, qkv, cumc, _piece_select(), norm_g)


def _layer_norm(y, g, b):
    mu = jnp.mean(y, axis=-1, keepdims=True)
    yc = y - mu
    var = jnp.mean(yc * yc, axis=-1, keepdims=True)
    return yc * lax.rsqrt(var + EPS) * g + b


def _outproj_kernel(ys_ref, ya_ref, x_ref, gt_ref, sc_ref, sh_ref, wo_ref, lng_ref, lnb_ref, wrh_ref, wrl_ref, rb_ref,
                    x1_ref, rt_ref, dest_ref, cnt_ref, tbl_ref, xrows_hbm,
                    carry, galloc, tbl, u2t, didx_v, didx_s, cnt_v, cnt_s, zeros, idx_sem, sc_sem, z_sem,
                    *, tm, blk, n_blk):
    i = pl.program_id(0)
    last = pl.num_programs(0) - 1
    cur = i % 2
    prev = 1 - cur

    def idx_ready(slot):
        return pltpu.make_async_copy(didx_v.at[slot], didx_s.at[slot], idx_sem.at[slot])

    def dispatched(slot):
        return pltpu.make_async_copy(u2t.at[slot], u2t.at[slot], sc_sem.at[slot])

    def dispatch_copy(slot, r, k):
        return pltpu.make_async_copy(u2t.at[slot, pl.ds(r * ROW_TILE, ROW_TILE)], xrows_hbm.at[didx_s[slot, k, r]],
                                     sc_sem.at[slot])

    @pl.when(i == 0)
    def _init():
        carry[...] = jnp.zeros_like(carry)
        galloc[...] = jnp.zeros_like(galloc)
        tbl[...] = jnp.zeros_like(tbl)

    @pl.when(i >= 2)
    def _reuse():
        dispatched(cur).wait()
        dispatched(cur).wait()

    def step(dispatch_prev):
        n_parts = OUTPROJ_PARTS
        n_slices = 3 * n_parts + 2

        def dispatch_slice(c):
            if dispatch_prev:
                for r_ in range(c * tm // n_slices, (c + 1) * tm // n_slices):
                    dispatch_copy(prev, r_, 0).start(priority=0)
                    dispatch_copy(prev, r_, 1).start(priority=1)

        if dispatch_prev:
            idx_ready(prev).wait()
        part = tm // n_parts
        hs = []
        for a in range(n_parts):
            rs = slice(a * part, (a + 1) * part)
            dispatch_slice(a)
            hs.append(_dot(ys_ref[rs, :], wo_ref[0:D_SSM, :]) + _dot(ya_ref[rs, :], wo_ref[D_SSM:D, :]))
        logit_parts = []
        for a in range(n_parts):
            rs = slice(a * part, (a + 1) * part)
            dispatch_slice(n_parts + 2 * a)
            x1 = _layer_norm(ALPHA * x_ref[rs, :] + (1.0 + gt_ref[0, 0]) * hs[a], lng_ref[...], lnb_ref[...])
            x1_ref[rs, :] = x1
            u2 = x1 * (1.0 + sc_ref[0, 0]) + sh_ref[0, 0]
            _store_row_tiles(u2t.at[cur, pl.ds(a * part * ROW_TILE, part * ROW_TILE)], u2)
            dispatch_slice(n_parts + 2 * a + 1)
            uh = u2.astype(bf16)
            ul = (u2 - uh.astype(f32)).astype(bf16)
            logit_parts.append((_dot(uh, wrh_ref[...]) + _dot(ul, wrh_ref[...])) + _dot(uh, wrl_ref[...]))
        logits = jnp.concatenate(logit_parts, axis=0) + rb_ref[...]
        lane = lax.broadcasted_iota(i32, (tm, 128), 1).astype(f32)
        big = jnp.float32(1e9)

        def first_max(vals):
            m = jnp.max(vals, axis=-1, keepdims=True)
            return m, jnp.min(jnp.where(vals == m, lane, big), axis=-1, keepdims=True)

        gl = jnp.where(lane < N_GROUPS_R, logits, NEG)
        gmax, gidx = first_max(gl)
        g_p = 1.0 / jnp.sum(jnp.exp(gl - gmax), axis=-1, keepdims=True)
        lo = N_GROUPS_R + EXPERTS_PER_GROUP * gidx
        el = jnp.where((lane >= lo) & (lane < lo + EXPERTS_PER_GROUP), logits, NEG)
        m1, i1 = first_max(el)
        el2 = jnp.where(lane == i1, NEG, el)
        m2, i2 = first_max(el2)
        r = jnp.exp(m2 - m1)
        w1 = g_p / (1.0 + r)
        w2 = g_p * r / (1.0 + r)
        dispatch_slice(3 * n_parts)

        oh1 = lane == i1
        oh2 = lane == i2
        oh = jnp.where(oh1 | oh2, 1.0, 0.0)
        r_i = lax.broadcasted_iota(i32, (tm, tm), 0)
        c_i = lax.broadcasted_iota(i32, (tm, tm), 1)
        lower = jnp.where(r_i > c_i, 1.0, 0.0).astype(bf16)
        c_old = carry[...]
        prefix = _dot(lower, oh.astype(bf16)) + c_old
        rank1 = jnp.sum(jnp.where(oh1, prefix, 0.0), axis=-1, keepdims=True)
        rank2 = jnp.sum(jnp.where(oh2, prefix, 0.0), axis=-1, keepdims=True)
        c_new = c_old + jnp.sum(oh, axis=0, keepdims=True)
        carry[...] = c_new
        dispatch_slice(3 * n_parts + 1)

        nb_old = jnp.floor((c_old + (blk - 1)) * (1.0 / blk))
        nb_new = jnp.floor((c_new + (blk - 1)) * (1.0 / blk))
        fresh = nb_new - nb_old
        sq_r = lax.broadcasted_iota(i32, (128, 128), 0)
        sq_c = lax.broadcasted_iota(i32, (128, 128), 1)
        before = jnp.where(sq_r < sq_c, 1.0, 0.0).astype(bf16)
        base = galloc[...] + _dot(jnp.broadcast_to(fresh, (8, 128)).astype(bf16), before)[0:1, :]
        galloc[...] = galloc[...] + jnp.sum(fresh, axis=-1, keepdims=True)
        ordinal = sq_r.astype(f32)
        tbl[...] = jnp.where((ordinal >= nb_old) & (ordinal < nb_new), base + (ordinal - nb_old), tbl[...])
        tbl_ref[...] = tbl[...]
        cnt_ref[...] = jnp.concatenate([c_new, galloc[...], jnp.zeros((6, 128), f32)], axis=0)

        rt = jnp.where(lane == 4, w1, jnp.where(lane == 5, w2, 0.0))
        rt_ref[...] = rt

        ids_bf = tbl[...].astype(bf16)

        def row_of(onehot, rank):
            nth = jnp.floor(rank * (1.0 / blk))
            ids = _dot(jnp.where(lane == nth, 1.0, 0.0).astype(bf16), ids_bf)
            return jnp.sum(jnp.where(onehot, ids, 0.0), axis=-1, keepdims=True) * blk + (rank - nth * blk)

        d1 = row_of(oh1, rank1)
        d2 = row_of(oh2, rank2)
        dd = jnp.where(lane == 0, d1, jnp.where(lane == 1, d2, 0.0)).T[0:8, :].astype(i32)
        dest_ref[...] = dd
        didx_v[cur] = dd
        idx_ready(cur).start()

    pl.when(i == 0)(lambda: step(False))
    pl.when(i > 0)(lambda: step(True))

    @pl.when(i == last)
    def _epilogue():
        idx_ready(cur).wait()

        def issue(r_, c):
            dispatch_copy(cur, r_, 0).start()
            dispatch_copy(cur, r_, 1).start()
            return c
        lax.fori_loop(0, tm, issue, 0)

        cnt = carry[...]
        n_blocks = jnp.floor((cnt + (blk - 1)) * (1.0 / blk))
        ordinal = lax.broadcasted_iota(i32, (128, 128), 0).astype(f32)
        last_id = jnp.sum(jnp.where(ordinal == n_blocks - 1.0, tbl[...], 0.0), axis=0, keepdims=True)
        used = cnt - (n_blocks - 1.0) * blk
        cnt_v[...] = jnp.concatenate([last_id * blk + used, blk - used, galloc[...], jnp.zeros((5, 128), f32)],
                                     axis=0).astype(i32)
        to_smem = pltpu.make_async_copy(cnt_v, cnt_s, z_sem)
        to_smem.start()
        to_smem.wait()
        zeros[...] = jnp.zeros_like(zeros)
        sizes = [1 << b for b in reversed(range(blk.bit_length() - 1))]

        def for_each_fill(fn):
            def tail(e, c):
                pad = cnt_s[1, N_GROUPS_R + e]
                off = cnt_s[0, N_GROUPS_R + e]
                for sz in sizes:
                    @pl.when((pad & sz) != 0)
                    def _(off=off, sz=sz):
                        fn(pltpu.make_async_copy(zeros.at[pl.ds(0, sz)], xrows_hbm.at[pl.ds(off, sz)], z_sem))
                    off = off + (pad & sz)
                return c
            lax.fori_loop(0, N_EXPERTS, tail, 0)

            def whole(b, c):
                fn(pltpu.make_async_copy(zeros, xrows_hbm.at[pl.ds(b * blk, blk)], z_sem))
                return c
            lax.fori_loop(cnt_s[2, 0], n_blk, whole, 0)

        for_each_fill(lambda copy: copy.start())
        for_each_fill(lambda copy: copy.wait())

        @pl.when(i > 0)
        def _prev_done():
            dispatched(prev).wait()
            dispatched(prev).wait()
        dispatched(cur).wait()
        dispatched(cur).wait()


def _outproj(y_ssm, y_att, x, mod4, w_out, ln_g, ln_b, wr_hi, wr_lo, rb, t, blk, n_blk):
    n = x.shape[0]
    tm = min(OUTPROJ_ROWS, t)
    nt = t // tm
    assert n // blk + 1 <= 128 and n_blk <= 256, "block-id table: 128 blocks per expert, ids exact in bf16"
    vec = lambda k: pl.BlockSpec((1, 1, 1, D), lambda i, k=k: (k, i // nt, 0, 0))
    full = lambda shape: pl.BlockSpec(shape, lambda i: (0,) * len(shape))
    rows = lambda w: pl.BlockSpec((tm, w), lambda i: (i, 0))
    return pl.pallas_call(
        functools.partial(_outproj_kernel, tm=tm, blk=blk, n_blk=n_blk),
        grid=(n // tm,),
        in_specs=[rows(D_SSM), rows(D_ATT), rows(D), vec(2), vec(4), vec(3),
                  full((D, D)), full((1, D)), full((1, D)), full((D, 128)), full((D, 128)), full((1, 128))],
        out_specs=[rows(D), rows(128), pl.BlockSpec((8, tm), lambda i: (0, i)), full((8, 128)), full((128, 128)),
                   pl.BlockSpec(memory_space=pl.ANY)],
        out_shape=[jax.ShapeDtypeStruct((n, D), f32), jax.ShapeDtypeStruct((n, 128), f32),
                   jax.ShapeDtypeStruct((8, n), i32), jax.ShapeDtypeStruct((8, 128), f32),
                   jax.ShapeDtypeStruct((128, 128), f32),
                   jax.ShapeDtypeStruct((n_blk * blk, ROW_TILE, 128), f32)],
        scratch_shapes=[pltpu.VMEM((1, 128), f32), pltpu.VMEM((1, 128), f32), pltpu.VMEM((128, 128), f32),
                        pltpu.VMEM((2, tm * ROW_TILE, 128), f32),
                        pltpu.VMEM((2, 8, tm), i32), pltpu.SMEM((2, 8, tm), i32),
                        pltpu.VMEM((8, 128), i32), pltpu.SMEM((8, 128), i32),
                        pltpu.VMEM((blk, ROW_TILE, 128), f32),
                        pltpu.SemaphoreType.DMA((2,)), pltpu.SemaphoreType.DMA((2,)), pltpu.SemaphoreType.DMA(())],
        compiler_params=pltpu.CompilerParams(dimension_semantics=("arbitrary",), vmem_limit_bytes=VMEM_LIMIT),
        name="outproj",
    )(y_ssm, y_att, x, mod4, mod4, mod4, w_out, ln_g, ln_b, wr_hi, wr_lo, rb)


def _tile_copy(src_hbm, row, buf, slot, sem):
    return pltpu.make_async_copy(src_hbm.at[row], buf.at[pl.ds(slot * ROW_TILE, ROW_TILE)], sem)


def _moe_kernel(be_ref, br_ref, nu_ref, nxt_ref, par_ref, x_ref, wg_hbm, wu_hbm, wd_hbm, y_ref,
                wfg, wfu, wfd, wgb, wub, wdb, wsem, *, blk):
    i = pl.program_id(0)
    used = i < nu_ref[0]

    def fetch(e, slot):
        return [pltpu.make_async_copy(src.at[e], dst.at[slot], wsem.at[slot])
                for src, dst in ((wg_hbm, wfg), (wu_hbm, wfu), (wd_hbm, wfd))]

    @pl.when(i == 0)
    def _first():
        for c in fetch(be_ref[0], par_ref[0]):
            c.start()

    @pl.when(jnp.logical_and(used, jnp.logical_or(i == 0, be_ref[i] != be_ref[jnp.maximum(i - 1, 0)])))
    def _switch():
        slot = par_ref[i]
        for c in fetch(be_ref[i], slot):
            c.wait()

        @pl.when(nxt_ref[i] >= 0)
        def _next():
            for c in fetch(nxt_ref[i], 1 - slot):
                c.start()
        wgb[...] = wfg[slot].astype(bf16)
        wub[...] = wfu[slot].astype(bf16)
        wdb[...] = wfd[slot].astype(bf16)

    @pl.when(jnp.logical_not(used))
    def _spare():
        y_ref[...] = jnp.zeros_like(y_ref)

    @pl.when(used)
    def _work():
        xb = _load_row_tiles(x_ref, blk).astype(bf16)
        hid = (_silu(_dot(xb, wgb[...])) * _dot(xb, wub[...])).astype(bf16)
        _store_row_tiles(y_ref, _dot(hid, wdb[...]))


def _moe(block_e, block_row, n_used, next_e, parity, x_rows, w_gate, w_up, w_down, blk):
    nblk = block_e.shape[0]
    rspec = pl.BlockSpec((blk * ROW_TILE, 128), lambda i, be, br, *_: (br[i], 0))
    xspec = pl.BlockSpec((blk * ROW_TILE, 128), lambda i, be, br, nu, *_: (br[jnp.minimum(i, nu[0] - 1)], 0))
    anyspec = pl.BlockSpec(memory_space=pl.ANY)
    grid_spec = pltpu.PrefetchScalarGridSpec(
        num_scalar_prefetch=5,
        grid=(nblk,),
        in_specs=[xspec, anyspec, anyspec, anyspec],
        out_specs=rspec,
        scratch_shapes=[pltpu.VMEM((2, D, D_EXPERT), f32), pltpu.VMEM((2, D, D_EXPERT), f32),
                        pltpu.VMEM((2, D_EXPERT, D), f32),
                        pltpu.VMEM((D, D_EXPERT), bf16), pltpu.VMEM((D, D_EXPERT), bf16),
                        pltpu.VMEM((D_EXPERT, D), bf16),
                        pltpu.SemaphoreType.DMA((2,))],
    )
    return pl.pallas_call(
        functools.partial(_moe_kernel, blk=blk),
        grid_spec=grid_spec,
        out_shape=jax.ShapeDtypeStruct(x_rows.shape, f32),
        compiler_params=pltpu.CompilerParams(dimension_semantics=("arbitrary",), vmem_limit_bytes=VMEM_LIMIT),
        name="moe",
    )(block_e, block_row, n_used, next_e, parity, x_rows, w_gate, w_up, w_down)


def _combine_kernel(d1_ref, d2_ref, y_hbm, x1_ref, rt_ref, gt_ref, lng_ref, lnb_ref, o_ref, buf, sem, *, tm):
    i = pl.program_id(0)
    last = pl.num_programs(0) - 1

    @pl.when(i == 0)
    def _prologue():
        for s in range(2):
            def issue(r, c, s=s):
                base = jnp.minimum(s, last) * tm
                _tile_copy(y_hbm, d1_ref[base + r], buf.at[s, 0], r, sem.at[s]).start()
                _tile_copy(y_hbm, d2_ref[base + r], buf.at[s, 1], r, sem.at[s]).start()
                return c
            lax.fori_loop(0, tm, issue, 0)

    def rows_ready(slot):
        return pltpu.make_async_copy(buf.at[slot], buf.at[slot], sem.at[slot])

    slot = i % 3
    nxt = (i + 2) % 3
    base = jnp.minimum(i + 2, last) * tm
    rows_ready(slot).wait()
    for r in range(tm):
        _tile_copy(y_hbm, d1_ref[base + r], buf.at[nxt, 0], r, sem.at[nxt]).start(priority=0)
        _tile_copy(y_hbm, d2_ref[base + r], buf.at[nxt, 1], r, sem.at[nxt]).start(priority=1)
    rt = rt_ref[...]
    moe = rt[:, 4:5] * _load_row_tiles(buf.at[slot, 0], tm) + rt[:, 5:6] * _load_row_tiles(buf.at[slot, 1], tm)
    y = ALPHA * x1_ref[...] + (1.0 + gt_ref[0, 0]) * moe
    o_ref[...] = _layer_norm(y, lng_ref[...], lnb_ref[...])

    @pl.when(i == last)
    def _drain():
        rows_ready((i + 1) % 3).wait()
        rows_ready((i + 2) % 3).wait()


def _combine(dest1, dest2, y_rows, x1, rt, mod4, ln_g, ln_b, t):
    n = x1.shape[0]
    tm = min(COMBINE_ROWS, t)
    nt = t // tm
    full = lambda shape: pl.BlockSpec(shape, lambda i, d1, d2: (0,) * len(shape))
    rows = lambda w: pl.BlockSpec((tm, w), lambda i, d1, d2: (i, 0))
    grid_spec = pltpu.PrefetchScalarGridSpec(
        num_scalar_prefetch=2,
        grid=(n // tm,),
        in_specs=[pl.BlockSpec(memory_space=pl.ANY), rows(D), rows(128),
                  pl.BlockSpec((1, 1, 1, D), lambda i, d1, d2: (5, i // nt, 0, 0)),
                  full((1, D)), full((1, D))],
        out_specs=rows(D),
        scratch_shapes=[pltpu.VMEM((3, 2, tm * ROW_TILE, 128), f32), pltpu.SemaphoreType.DMA((3,))],
    )
    return pl.pallas_call(
        functools.partial(_combine_kernel, tm=tm),
        grid_spec=grid_spec,
        out_shape=jax.ShapeDtypeStruct((n, D), f32),
        compiler_params=pltpu.CompilerParams(dimension_semantics=("arbitrary",), vmem_limit_bytes=VMEM_LIMIT),
        name="combine",
    )(dest1, dest2, y_rows, x1, rt, mod4, ln_g, ln_b)


def _hi_lo(w):
    hi = w.astype(bf16)
    return hi, (w - hi.astype(f32)).astype(bf16)


def _pad_lanes(v, offset, width):
    return jnp.zeros((width,), f32).at[offset:offset + v.shape[0]].set(v)


def _layer(x, c, ada_w, ada_b, w_in, conv_w, conv_b, dt_bias, a_log, d_skip, ssm_norm_g, fg_bias, att_norm_g,
           w_out, ln1_g, ln1_b, router_g_w, router_g_b, router_e_w, router_e_b, w_gate, w_up, w_down, ln2_g, ln2_b):
    bsz, t, _ = x.shape
    n = bsz * t

    mod4 = _ada(c, ada_w, ada_b).reshape(6, bsz, 1, D)

    w_main, w_small = _pack_w_in(w_in)
    zx, bc, qkv, dtf = _inproj(x, mod4, w_main, w_small, conv_w, conv_b.reshape(1, -1))

    pc = jnp.stack([_pad_lanes(dt_bias, 0, 128), _pad_lanes(a_log, 0, 128), _pad_lanes(fg_bias, 8, 128)]
                   + [jnp.zeros((128,), f32)] * 5)
    rep = lambda v: jnp.repeat(v, HEAD_DIM)
    pe = jnp.stack([rep(dt_bias), rep(a_log), rep(d_skip), ssm_norm_g] + [jnp.zeros((D_SSM,), f32)] * 4)
    y_ssm, cumc = _ssd(zx, bc, dtf, pc, pe)

    y_att = _attn(qkv, cumc, att_norm_g.reshape(1, -1))

    wr = jnp.concatenate([router_g_w, router_e_w, jnp.zeros((D, 128 - N_GROUPS_R - N_EXPERTS), f32)], axis=1)
    wr_hi, wr_lo = _hi_lo(wr)
    rb = jnp.concatenate([router_g_b, router_e_b, jnp.zeros((128 - N_GROUPS_R - N_EXPERTS,), f32)]).reshape(1, 128)
    blk = MOE_ROWS
    nblk = (2 * n) // blk + N_EXPERTS
    x1, rt, dest, cnt, tbl, x_rows = _outproj(y_ssm.reshape(n, D_SSM), y_att.reshape(n, D_ATT), x.reshape(n, D), mod4,
                                              w_out.astype(bf16), ln1_g.reshape(1, D), ln1_b.reshape(1, D),
                                              wr_hi, wr_lo, rb, t, blk, nblk)

    counts = cnt[0, N_GROUPS_R:N_GROUPS_R + N_EXPERTS].astype(i32)
    nb = (counts + blk - 1) // blk
    nb_end = jnp.cumsum(nb)
    n_used = nb_end[N_EXPERTS - 1:]
    every = jnp.arange(nblk, dtype=i32)
    step = jnp.minimum(every, n_used[0] - 1)
    owner = (nb_end[None, :] <= step[:, None]).astype(i32)
    block_e = jnp.sum(owner, axis=1)
    first = jnp.sum(owner * nb[None, :], axis=1)
    lanes = jnp.arange(128, dtype=i32)[None, :]
    nth_row = jnp.dot((lanes == (step - first)[:, None]).astype(f32), tbl, precision=lax.Precision.HIGHEST)
    block_id = jnp.sum(jnp.where(lanes == (block_e + N_GROUPS_R)[:, None], nth_row, 0.0), axis=1).astype(i32)
    block_row = jnp.where(every < n_used[0], block_id, every)
    run_end = first + jnp.sum((jnp.arange(N_EXPERTS)[None, :] == block_e[:, None]) * nb[None, :], axis=1)
    next_e = jnp.where(run_end < n_used[0], jnp.sum((nb_end[None, :] <= run_end[:, None]).astype(i32), axis=1), -1)
    parity = jnp.sum(((jnp.arange(N_EXPERTS)[None, :] < block_e[:, None]) & (nb[None, :] > 0)).astype(i32), axis=1) % 2

    y_rows = _moe(block_e, block_row, n_used, next_e, parity, x_rows.reshape(nblk * blk * ROW_TILE, 128),
                  w_gate, w_up, w_down, blk)
    out = _combine(dest[0], dest[1], y_rows.reshape(nblk * blk, ROW_TILE, 128), x1, rt, mod4,
                   ln2_g.reshape(1, D), ln2_b.reshape(1, D), t)
    return out.reshape(bsz, t, D)


def kernel(x, c, ada_w, ada_b, w_in, conv_w, conv_b, dt_bias, a_log, d_skip, ssm_norm_g, fg_bias, att_norm_g, w_out,
           ln1_g, ln1_b, router_g_w, router_g_b, router_e_w, router_e_b, w_gate, w_up, w_down, ln2_g, ln2_b):
    depth = ada_w.shape[0]
    for l in range(depth):
        x = _layer(x, c, ada_w[l], ada_b[l], w_in[l], conv_w[l], conv_b[l], dt_bias[l], a_log[l], d_skip[l],
                   ssm_norm_g[l], fg_bias[l], att_norm_g[l], w_out[l], ln1_g[l], ln1_b[l], router_g_w[l],
                   router_g_b[l], router_e_w[l], router_e_b[l], w_gate[l], w_up[l], w_down[l], ln2_g[l], ln2_b[l])
    return x
```

```python
import functools

import jax
import jax.numpy as jnp
import numpy as np
from jax import lax
from jax.experimental import pallas as pl
from jax.experimental.pallas import tpu as pltpu

f32 = jnp.float32
bf16 = jnp.bfloat16
i32 = jnp.int32

D = 1024
D_SSM = 512
D_ATT = 512
HEAD_DIM = 64
GROUP_W = 256
N_STATE = 128
CONV_K = 4
N_GROUPS_R = 4
EXPERTS_PER_GROUP = 8
N_EXPERTS = 32
D_EXPERT = 512
ALPHA = 2.0 ** 0.25
EPS = 1e-5
NEG = -1e30
LOG2E = 1.4426950408889634
QK_SCALE = HEAD_DIM ** -0.5 * LOG2E
V_ROWS = 80

SSD_CHUNK = 256
ATT_BLOCK = 256
INPROJ_ROWS = 512
OUTPROJ_ROWS = 512
OUTPROJ_PARTS = 2
MOE_ROWS = 256
COMBINE_ROWS = 256
DEST_ROWS = 2048
VMEM_LIMIT = 48 * 1024 * 1024


def _dot(a, b):
    return jnp.dot(a, b, preferred_element_type=f32)


def _dot_nt(a, b):
    return lax.dot_general(a, b, (((1,), (1,)), ((), ())), preferred_element_type=f32)


def _dot_tn(a, b):
    return lax.dot_general(a, b, (((0,), (0,)), ((), ())), preferred_element_type=f32)


def _split3(v):
    hi = v.astype(bf16)
    r1 = v - hi.astype(f32)
    mid = r1.astype(bf16)
    lo = (r1 - mid.astype(f32)).astype(bf16)
    return hi, mid, lo


def _dot_exact_lhs(m, v):
    hi, mid, lo = _split3(v)
    return (_dot(m, hi) + _dot(m, mid)) + _dot(m, lo)


def _dot_exact_rhs(v, m):
    hi, mid, lo = _split3(v)
    return (_dot(hi, m) + _dot(mid, m)) + _dot(lo, m)


ROW_TILE = 8


def _store_row_tiles(ref, val):
    rows = val.shape[0]
    for c in range(ROW_TILE):
        ref[pl.ds(c, rows, stride=ROW_TILE), :] = val[:, c * 128:(c + 1) * 128]


def _load_row_tiles(ref, rows):
    return jnp.concatenate([ref[pl.ds(c, rows, stride=ROW_TILE), :] for c in range(ROW_TILE)], axis=1)


def _softplus(x):
    return jnp.maximum(x, 0.0) + jnp.log1p(jnp.exp(-jnp.abs(x)))


def _silu(x):
    return x * jax.nn.sigmoid(x)


def _ada_kernel(c_ref, w_ref, b_ref, o_ref):
    s = _silu(c_ref[...]).astype(bf16)
    o_ref[0] = _dot(s, w_ref[...].astype(bf16)) + b_ref[0]


def _ada(c, w, b):
    bsz = c.shape[0]
    return pl.pallas_call(
        _ada_kernel,
        grid=(6,),
        in_specs=[pl.BlockSpec((bsz, D), lambda j: (0, 0)),
                  pl.BlockSpec((D, D), lambda j: (0, j)),
                  pl.BlockSpec((1, 1, D), lambda j: (j, 0, 0))],
        out_specs=pl.BlockSpec((1, bsz, D), lambda j: (j, 0, 0)),
        out_shape=jax.ShapeDtypeStruct((6, bsz, D), f32),
        compiler_params=pltpu.CompilerParams(dimension_semantics=("arbitrary",), vmem_limit_bytes=VMEM_LIMIT),
        name="ada",
    )(c, w, b.reshape(6, 1, D))


W_IN_COLS = 3088
PACK_ROWS = 256


def _pack_w_in_kernel(wt_ref, wm_ref, ws_ref):
    for j in range(3072 // PACK_ROWS):
        src = j * PACK_ROWS if j * PACK_ROWS < 1536 else j * PACK_ROWS + 8
        wm_ref[:, j * PACK_ROWS:(j + 1) * PACK_ROWS] = wt_ref[src:src + PACK_ROWS, :].T.astype(bf16)
    small = jnp.concatenate([wt_ref[1536:1544, :], wt_ref[W_IN_COLS - 8:W_IN_COLS, :], jnp.zeros((112, D), f32)],
                            axis=0).T
    hi = small.astype(bf16)
    ws_ref[:, 0:128] = hi
    ws_ref[:, 128:256] = (small - hi.astype(f32)).astype(bf16)


def _pack_w_in(w_in):
    full = lambda shape: pl.BlockSpec(shape, lambda i: (0,) * len(shape))
    return pl.pallas_call(
        _pack_w_in_kernel,
        grid=(1,),
        in_specs=[full((W_IN_COLS, D))],
        out_specs=[full((D, 3072)), full((D, 256))],
        out_shape=[jax.ShapeDtypeStruct((D, 3072), bf16), jax.ShapeDtypeStruct((D, 256), bf16)],
        compiler_params=pltpu.CompilerParams(dimension_semantics=("arbitrary",), vmem_limit_bytes=VMEM_LIMIT),
        name="pack_w_in",
    )(w_in.T)


def _inproj_kernel(x_ref, sc_ref, sh_ref, wm_ref, ws_ref, cw_ref, cb_ref, zx_ref, bc_ref, qkv_ref, dtf_ref, xcat, *, tm):
    i = pl.program_id(1)

    @pl.when(i == 0)
    def _init():
        xcat[0:8, :] = jnp.zeros((8, 2 * D_SSM), f32)

    u = x_ref[0] * (1.0 + sc_ref[0, 0]) + sh_ref[0, 0]
    ub = u.astype(bf16)
    xcat[8:8 + tm, 0:512] = _dot(ub, wm_ref[:, 512:1024])
    xcat[8:8 + tm, 512:1024] = _dot(ub, wm_ref[:, 1024:1536])
    zx_ref[0, :, 0:512] = _dot(ub, wm_ref[:, 0:512])
    acc = cw_ref[0:1, :] * xcat[5:5 + tm, :] + cb_ref[...]
    for k in range(1, CONV_K):
        acc = acc + cw_ref[k:k + 1, :] * xcat[5 + k:5 + k + tm, :]
    xcat[0:8, :] = xcat[tm:tm + 8, :]
    xbc = _silu(acc)
    zx_ref[0, :, 512:1024] = xbc[:, 0:512]
    bc_ref[0] = xbc[:, 512:1024].astype(bf16)
    for j, scale in enumerate((QK_SCALE, 1.0, 1.0)):
        qkv_ref[0, :, j * 512:(j + 1) * 512] = (
            _dot(ub, wm_ref[:, 1536 + j * 512:1536 + (j + 1) * 512]) * scale).astype(bf16)
    ul = (u - ub.astype(f32)).astype(bf16)
    d_hl = _dot(ub, ws_ref[...])
    dtf_ref[0] = (d_hl[:, 0:128] + _dot(ul, ws_ref[:, 0:128])) + d_hl[:, 128:256]


def _inproj(x, mod4, w_main, ws, conv_w, conv_b):
    bsz, t, _ = x.shape
    tm = min(INPROJ_ROWS, t)
    vec = lambda k: pl.BlockSpec((1, 1, 1, D), lambda b, i, k=k: (k, b, 0, 0))
    full = lambda shape: pl.BlockSpec(shape, lambda b, i: (0,) * len(shape))
    rows = lambda w: pl.BlockSpec((1, tm, w), lambda b, i: (b, i, 0))
    return pl.pallas_call(
        functools.partial(_inproj_kernel, tm=tm),
        grid=(bsz, t // tm),
        in_specs=[rows(D), vec(1), vec(0), full((D, 3072)), full((D, 256)),
                  full((CONV_K, 2 * D_SSM)), full((1, 2 * D_SSM))],
        out_specs=[rows(1024), rows(512), rows(1536), rows(128)],
        out_shape=[jax.ShapeDtypeStruct((bsz, t, 1024), f32),
                   jax.ShapeDtypeStruct((bsz, t, 512), bf16),
                   jax.ShapeDtypeStruct((bsz, t, 1536), bf16),
                   jax.ShapeDtypeStruct((bsz, t, 128), f32)],
        scratch_shapes=[pltpu.VMEM((tm + 8, 2 * D_SSM), f32)],
        compiler_params=pltpu.CompilerParams(dimension_semantics=("parallel", "arbitrary"),
                                             vmem_limit_bytes=VMEM_LIMIT),
        name="inproj",
    )(x, mod4, mod4, w_main, ws, conv_w, conv_b)


def _ssd_kernel(z_ref, xs_ref, bc_ref, dtf_ref, pc_ref, pe_ref, y_ref, cumc_ref, state, carry, *, lc):
    j = pl.program_id(1)

    @pl.when(j == 0)
    def _init():
        state[...] = jnp.zeros_like(state)
        carry[...] = jnp.zeros_like(carry)

    xs = xs_ref[0]
    bm = bc_ref[0, :, 0:256]
    cm = bc_ref[0, :, 256:512]

    dtf = dtf_ref[0]
    lane = lax.broadcasted_iota(i32, (lc, 128), 1)
    dt_c = _softplus(dtf + pc_ref[0:1, :])
    a_c = dt_c * (-jnp.exp(pc_ref[1:2, :]))
    logf = -_softplus(-(dtf + pc_ref[2:3, :]))
    v = jnp.where(lane < 8, a_c, logf)
    r_i = lax.broadcasted_iota(i32, (lc, lc), 0)
    c_i = lax.broadcasted_iota(i32, (lc, lc), 1)
    tri = r_i >= c_i
    tri_b = jnp.where(tri, 1.0, 0.0).astype(bf16)
    cum = _dot_exact_lhs(tri_b, v) + carry[...]
    carry[...] = jnp.where(lane[0:1, :] >= 8, cum[lc - 1:lc, :], 0.0)
    cumc_ref[0] = cum
    cs_t = cum.T[0:8, :]

    e_r = lax.broadcasted_iota(i32, (128, D_SSM), 0)
    e_c = lax.broadcasted_iota(i32, (128, D_SSM), 1)
    expand = jnp.where(jnp.right_shift(e_c, 6) == e_r, 1.0, 0.0).astype(bf16)
    dt_e = _dot_exact_rhs(dt_c, expand)
    cs_e = _dot_exact_rhs(cum, expand)

    xdt = xs * dt_e
    ecs = jnp.exp(cs_e)
    cs_last = cs_e[lc - 1:lc, :]
    dec_st = jnp.exp(cs_last - cs_e)
    lane_g = lax.broadcasted_iota(i32, (1, GROUP_W), 1)
    ys = []
    for g in range(2):
        gs = slice(g * GROUP_W, (g + 1) * GROUP_W)
        bg = bm[:, g * N_STATE:(g + 1) * N_STATE]
        cg = cm[:, g * N_STATE:(g + 1) * N_STATE]
        cb = _dot_nt(cg, bg)
        xdt_g = xdt[:, gs]
        xdt_gb = xdt_g.astype(bf16)
        ms, xb = [], []
        for hh in range(4):
            h = g * 4 + hh
            lm = jnp.exp(jnp.where(tri, cum[:, h:h + 1] - cs_t[h:h + 1, :], -jnp.inf))
            ms.append((cb * lm).astype(bf16))
            xb.append(jnp.where(jnp.right_shift(lane_g, 6) == hh, xdt_gb, jnp.zeros_like(xdt_gb)))
        y_diag = _dot(jnp.concatenate(ms, axis=1), jnp.concatenate(xb, axis=0))
        st = state[g]
        y_off = _dot(cg, st.astype(bf16)) * ecs[:, gs]
        upd = _dot_tn(bg, (xdt_g * dec_st[:, gs]).astype(bf16))
        state[g] = st * jnp.exp(cs_last[:, gs]) + upd
        ys.append(y_diag + y_off + xs[:, gs] * pe_ref[2:3, gs])

    outs = []
    for g in range(2):
        gs = slice(g * GROUP_W, (g + 1) * GROUP_W)
        yg = ys[g] * _silu(z_ref[0, :, gs])
        ms_ = jnp.mean(yg * yg, axis=-1, keepdims=True)
        outs.append(yg * lax.rsqrt(ms_ + EPS))
    y_ref[0] = (jnp.concatenate(outs, axis=1) * pe_ref[3:4, :]).astype(bf16)


def _ssd(zx, bc, dtf, pc, pe):
    bsz, t, _ = zx.shape
    lc = min(SSD_CHUNK, t)
    col = lambda k: pl.BlockSpec((1, lc, 512), lambda b, j, k=k: (b, j, k))
    full = lambda shape: pl.BlockSpec(shape, lambda b, j: (0,) * len(shape))
    return pl.pallas_call(
        functools.partial(_ssd_kernel, lc=lc),
        grid=(bsz, t // lc),
        in_specs=[col(0), col(1), col(0),
                  pl.BlockSpec((1, lc, 128), lambda b, j: (b, j, 0)),
                  full((8, 128)), full((8, D_SSM))],
        out_specs=[pl.BlockSpec((1, lc, D_SSM), lambda b, j: (b, j, 0)),
                   pl.BlockSpec((1, lc, 128), lambda b, j: (b, j, 0))],
        out_shape=[jax.ShapeDtypeStruct((bsz, t, D_SSM), bf16),
                   jax.ShapeDtypeStruct((bsz, t, 128), f32)],
        scratch_shapes=[pltpu.VMEM((2, N_STATE, GROUP_W), f32),
                        pltpu.VMEM((1, 128), f32)],
        compiler_params=pltpu.CompilerParams(dimension_semantics=("parallel", "arbitrary"),
                                             vmem_limit_bytes=VMEM_LIMIT),
        name="ssd",
    )(zx, zx, bc, dtf, pc, pe)


def _attn_kernel(q_ref, k_ref, v_ref, cc_ref, psel_ref, ng_ref, o_ref, kaug, vt, acc, sc0, sc1, *, tq, t):
    i = pl.program_id(1)
    nkb = t // tq
    n_heads = D_ATT // HEAD_DIM
    lane = lax.broadcasted_iota(i32, (1, 128), 1)
    lo_half = lane < HEAD_DIM

    @pl.when(i == 0)
    def _build():
        eye = jnp.where(lax.broadcasted_iota(i32, (D_ATT, D_ATT), 0) == lax.broadcasted_iota(i32, (D_ATT, D_ATT), 1),
                        1.0, 0.0).astype(bf16)
        ones_rows = jnp.where(lax.broadcasted_iota(i32, (V_ROWS - HEAD_DIM, tq), 0) == 0, 1.0, 0.0).astype(bf16)
        for jb in range(nkb):
            rows = slice(jb * tq, (jb + 1) * tq)
            v_t = _dot_nt(eye, v_ref[0, rows, :]).astype(bf16)
            for h in range(n_heads):
                vt[jb, h * V_ROWS:h * V_ROWS + HEAD_DIM, :] = v_t[h * HEAD_DIM:(h + 1) * HEAD_DIM, :]
                vt[jb, h * V_ROWS + HEAD_DIM:(h + 1) * V_ROWS, :] = ones_rows
            pieces = jnp.concatenate(_split3(cc_ref[0, rows, :] * (-LOG2E)), axis=1)
            for p in range(n_heads // 2):
                a = _dot(pieces, psel_ref[p]).astype(bf16)
                kp = k_ref[0, rows, p * 128:(p + 1) * 128]
                kaug[2 * p, rows, :] = jnp.where(lo_half, kp, a)
                kaug[2 * p + 1, rows, :] = jnp.where(lo_half, a, kp)

    ones_hi = jnp.where((lane >= HEAD_DIM) & (lane < HEAD_DIM + 3), 1.0, 0.0).astype(bf16)
    ones_lo = jnp.where(lane < 3, 1.0, 0.0).astype(bf16)
    qa = []
    for p in range(n_heads // 2):
        qp = q_ref[0, :, p * 128:(p + 1) * 128]
        qa.append(jnp.where(lo_half, qp, ones_hi))
        qa.append(jnp.where(lo_half, ones_lo, qp))
    keep = lax.broadcasted_iota(i32, (tq, tq), 0) <= lax.broadcasted_iota(i32, (tq, tq), 1)
    acc[...] = jnp.zeros_like(acc)

    def score(jb, buf):
        k0 = pl.multiple_of(jb * tq, tq)
        for h in range(n_heads):
            buf[h] = _dot_nt(kaug[h, pl.ds(k0, tq), :], qa[h])

    def absorb(jb, ms, masked, buf):
        new_ms = []
        for h in range(n_heads):
            s = buf[h]
            if masked:
                s = jnp.where(keep, s, NEG)
            m_new = jnp.maximum(ms[h], jnp.max(s, axis=0, keepdims=True))
            alpha = jnp.exp2(ms[h] - m_new)
            p = jnp.exp2(s - m_new).astype(bf16)
            new_ms.append(m_new)
            acc[h] = acc[h] * alpha + _dot(vt[jb, h * V_ROWS:(h + 1) * V_ROWS, :], p)
        return tuple(new_ms)

    def pair(pp, ms):
        j0 = 2 * pp
        score(j0 + 1, sc1)
        ms = absorb(j0, ms, False, sc0)
        score(j0 + 2, sc0)
        return absorb(j0 + 1, ms, False, sc1)

    score(0, sc0)
    ms = lax.fori_loop(0, i // 2, pair, tuple(jnp.full((1, tq), NEG, f32) for _ in range(n_heads)))

    @pl.when(i % 2 == 0)
    def _even():
        absorb(i, ms, True, sc0)

    @pl.when(i % 2 == 1)
    def _odd():
        score(i, sc1)
        absorb(i, absorb(i - 1, ms, False, sc0), True, sc1)

    out_t = jnp.concatenate([acc[h, 0:HEAD_DIM, :] * (1.0 / acc[h, HEAD_DIM:HEAD_DIM + 1, :]) for h in range(n_heads)],
                            axis=0)
    ms_ = jnp.mean(out_t * out_t, axis=0, keepdims=True)
    out_t = out_t * lax.rsqrt(ms_ + EPS)
    o_ref[0] = (out_t.T * ng_ref[...]).astype(bf16)


def _piece_select():
    sel = np.zeros((4, 384, 128), np.float32)
    for pair in range(4):
        for j in range(3):
            sel[pair, j * 128 + 8 + 2 * pair, HEAD_DIM + j] = 1.0
            sel[pair, j * 128 + 8 + 2 * pair + 1, j] = 1.0
    return jnp.asarray(sel, bf16)


def _attn(qkv, cumc, norm_g):
    bsz, t, _ = qkv.shape
    tq = min(ATT_BLOCK, t)
    return pl.pallas_call(
        functools.partial(_attn_kernel, tq=tq, t=t),
        grid=(bsz, t // tq),
        in_specs=[pl.BlockSpec((1, tq, D_ATT), lambda b, i: (b, i, 0)),
                  pl.BlockSpec((1, t, D_ATT), lambda b, i: (b, 0, 1)),
                  pl.BlockSpec((1, t, D_ATT), lambda b, i: (b, 0, 2)),
                  pl.BlockSpec((1, t, 128), lambda b, i: (b, 0, 0)),
                  pl.BlockSpec((4, 384, 128), lambda b, i: (0, 0, 0)),
                  pl.BlockSpec((1, D_ATT), lambda b, i: (0, 0))],
        out_specs=pl.BlockSpec((1, tq, D_ATT), lambda b, i: (b, i, 0)),
        out_shape=jax.ShapeDtypeStruct((bsz, t, D_ATT), bf16),
        scratch_shapes=[pltpu.VMEM((D_ATT // HEAD_DIM, t, 128), bf16),
                        pltpu.VMEM((t // tq, (D_ATT // HEAD_DIM) * V_ROWS, tq), bf16),
                        pltpu.VMEM((D_ATT // HEAD_DIM, V_ROWS, tq), f32),
                        pltpu.VMEM((D_ATT // HEAD_DIM, tq, tq), f32),
                        pltpu.VMEM((D_ATT // HEAD_DIM, tq, tq), f32)],
        compiler_params=pltpu.CompilerParams(dimension_semantics=("parallel", "arbitrary"),
                                             vmem_limit_bytes=VMEM_LIMIT),
        name="attn",
    )(qkv, qkv, qkv, cumc, _piece_select(), norm_g)


def _layer_norm(y, g, b):
    mu = jnp.mean(y, axis=-1, keepdims=True)
    yc = y - mu
    var = jnp.mean(yc * yc, axis=-1, keepdims=True)
    return yc * lax.rsqrt(var + EPS) * g + b


def _outproj_kernel(ys_ref, ya_ref, x_ref, gt_ref, sc_ref, sh_ref, wo_ref, lng_ref, lnb_ref, wrh_ref, wrl_ref, rb_ref,
                    x1_ref, rt_ref, dest_ref, cnt_ref, tbl_ref, xrows_hbm,
                    carry, galloc, tbl, u2t, didx_v, didx_s, cnt_v, cnt_s, zeros, idx_sem, sc_sem, z_sem,
                    *, tm, blk, n_blk):
    i = pl.program_id(0)
    last = pl.num_programs(0) - 1
    cur = i % 2
    prev = 1 - cur

    def idx_ready(slot):
        return pltpu.make_async_copy(didx_v.at[slot], didx_s.at[slot], idx_sem.at[slot])

    def dispatched(slot):
        return pltpu.make_async_copy(u2t.at[slot], u2t.at[slot], sc_sem.at[slot])

    def dispatch_copy(slot, r, k):
        return pltpu.make_async_copy(u2t.at[slot, pl.ds(r * ROW_TILE, ROW_TILE)], xrows_hbm.at[didx_s[slot, k, r]],
                                     sc_sem.at[slot])

    @pl.when(i == 0)
    def _init():
        carry[...] = jnp.zeros_like(carry)
        galloc[...] = jnp.zeros_like(galloc)
        tbl[...] = jnp.zeros_like(tbl)

    @pl.when(i >= 2)
    def _reuse():
        dispatched(cur).wait()
        dispatched(cur).wait()

    def step(dispatch_prev):
        n_parts = OUTPROJ_PARTS
        n_slices = 3 * n_parts + 2

        def dispatch_slice(c):
            if dispatch_prev:
                for r_ in range(c * tm // n_slices, (c + 1) * tm // n_slices):
                    dispatch_copy(prev, r_, 0).start(priority=0)
                    dispatch_copy(prev, r_, 1).start(priority=1)

        if dispatch_prev:
            idx_ready(prev).wait()
        part = tm // n_parts
        hs = []
        for a in range(n_parts):
            rs = slice(a * part, (a + 1) * part)
            dispatch_slice(a)
            hs.append(_dot(ys_ref[rs, :], wo_ref[0:D_SSM, :]) + _dot(ya_ref[rs, :], wo_ref[D_SSM:D, :]))
        logit_parts = []
        for a in range(n_parts):
            rs = slice(a * part, (a + 1) * part)
            dispatch_slice(n_parts + 2 * a)
            x1 = _layer_norm(ALPHA * x_ref[rs, :] + (1.0 + gt_ref[0, 0]) * hs[a], lng_ref[...], lnb_ref[...])
            x1_ref[rs, :] = x1
            u2 = x1 * (1.0 + sc_ref[0, 0]) + sh_ref[0, 0]
            _store_row_tiles(u2t.at[cur, pl.ds(a * part * ROW_TILE, part * ROW_TILE)], u2)
            dispatch_slice(n_parts + 2 * a + 1)
            uh = u2.astype(bf16)
            ul = (u2 - uh.astype(f32)).astype(bf16)
            logit_parts.append((_dot(uh, wrh_ref[...]) + _dot(ul, wrh_ref[...])) + _dot(uh, wrl_ref[...]))
        logits = jnp.concatenate(logit_parts, axis=0) + rb_ref[...]
        lane = lax.broadcasted_iota(i32, (tm, 128), 1).astype(f32)
        big = jnp.float32(1e9)

        def first_max(vals):
            m = jnp.max(vals, axis=-1, keepdims=True)
            return m, jnp.min(jnp.where(vals == m, lane, big), axis=-1, keepdims=True)

        gl = jnp.where(lane < N_GROUPS_R, logits, NEG)
        gmax, gidx = first_max(gl)
        g_p = 1.0 / jnp.sum(jnp.exp(gl - gmax), axis=-1, keepdims=True)
        lo = N_GROUPS_R + EXPERTS_PER_GROUP * gidx
        el = jnp.where((lane >= lo) & (lane < lo + EXPERTS_PER_GROUP), logits, NEG)
        m1, i1 = first_max(el)
        el2 = jnp.where(lane == i1, NEG, el)
        m2, i2 = first_max(el2)
        r = jnp.exp(m2 - m1)
        w1 = g_p / (1.0 + r)
        w2 = g_p * r / (1.0 + r)
        dispatch_slice(3 * n_parts)

        oh1 = lane == i1
        oh2 = lane == i2
        oh = jnp.where(oh1 | oh2, 1.0, 0.0)
        r_i = lax.broadcasted_iota(i32, (tm, tm), 0)
        c_i = lax.broadcasted_iota(i32, (tm, tm), 1)
        lower = jnp.where(r_i > c_i, 1.0, 0.0).astype(bf16)
        c_old = carry[...]
        prefix = _dot(lower, oh.astype(bf16)) + c_old
        rank1 = jnp.sum(jnp.where(oh1, prefix, 0.0), axis=-1, keepdims=True)
        rank2 = jnp.sum(jnp.where(oh2, prefix, 0.0), axis=-1, keepdims=True)
        c_new = c_old + jnp.sum(oh, axis=0, keepdims=True)
        carry[...] = c_new
        dispatch_slice(3 * n_parts + 1)

        nb_old = jnp.floor((c_old + (blk - 1)) * (1.0 / blk))
        nb_new = jnp.floor((c_new + (blk - 1)) * (1.0 / blk))
        fresh = nb_new - nb_old
        sq_r = lax.broadcasted_iota(i32, (128, 128), 0)
        sq_c = lax.broadcasted_iota(i32, (128, 128), 1)
        before = jnp.where(sq_r < sq_c, 1.0, 0.0).astype(bf16)
        base = galloc[...] + _dot(jnp.broadcast_to(fresh, (8, 128)).astype(bf16), before)[0:1, :]
        galloc[...] = galloc[...] + jnp.sum(fresh, axis=-1, keepdims=True)
        ordinal = sq_r.astype(f32)
        tbl[...] = jnp.where((ordinal >= nb_old) & (ordinal < nb_new), base + (ordinal - nb_old), tbl[...])
        tbl_ref[...] = tbl[...]
        cnt_ref[...] = jnp.concatenate([c_new, galloc[...], jnp.zeros((6, 128), f32)], axis=0)

        rt = jnp.where(lane == 4, w1, jnp.where(lane == 5, w2, 0.0))
        rt_ref[...] = rt

        ids_bf = tbl[...].astype(bf16)

        def row_of(onehot, rank):
            nth = jnp.floor(rank * (1.0 / blk))
            ids = _dot(jnp.where(lane == nth, 1.0, 0.0).astype(bf16), ids_bf)
            return jnp.sum(jnp.where(onehot, ids, 0.0), axis=-1, keepdims=True) * blk + (rank - nth * blk)

        d1 = row_of(oh1, rank1)
        d2 = row_of(oh2, rank2)
        dd = jnp.where(lane == 0, d1, jnp.where(lane == 1, d2, 0.0)).T[0:8, :].astype(i32)
        dest_ref[...] = dd
        didx_v[cur] = dd
        idx_ready(cur).start()

    pl.when(i == 0)(lambda: step(False))
    pl.when(i > 0)(lambda: step(True))

    @pl.when(i == last)
    def _epilogue():
        idx_ready(cur).wait()

        def issue(r_, c):
            dispatch_copy(cur, r_, 0).start()
            dispatch_copy(cur, r_, 1).start()
            return c
        lax.fori_loop(0, tm, issue, 0)

        cnt = carry[...]
        n_blocks = jnp.floor((cnt + (blk - 1)) * (1.0 / blk))
        ordinal = lax.broadcasted_iota(i32, (128, 128), 0).astype(f32)
        last_id = jnp.sum(jnp.where(ordinal == n_blocks - 1.0, tbl[...], 0.0), axis=0, keepdims=True)
        used = cnt - (n_blocks - 1.0) * blk
        cnt_v[...] = jnp.concatenate([last_id * blk + used, blk - used, galloc[...], jnp.zeros((5, 128), f32)],
                                     axis=0).astype(i32)
        to_smem = pltpu.make_async_copy(cnt_v, cnt_s, z_sem)
        to_smem.start()
        to_smem.wait()
        zeros[...] = jnp.zeros_like(zeros)
        sizes = [1 << b for b in reversed(range(blk.bit_length() - 1))]

        def for_each_fill(fn):
            def tail(e, c):
                pad = cnt_s[1, N_GROUPS_R + e]
                off = cnt_s[0, N_GROUPS_R + e]
                for sz in sizes:
                    @pl.when((pad & sz) != 0)
                    def _(off=off, sz=sz):
                        fn(pltpu.make_async_copy(zeros.at[pl.ds(0, sz)], xrows_hbm.at[pl.ds(off, sz)], z_sem))
                    off = off + (pad & sz)
                return c
            lax.fori_loop(0, N_EXPERTS, tail, 0)

            def whole(b, c):
                fn(pltpu.make_async_copy(zeros, xrows_hbm.at[pl.ds(b * blk, blk)], z_sem))
                return c
            lax.fori_loop(cnt_s[2, 0], n_blk, whole, 0)

        for_each_fill(lambda copy: copy.start())
        for_each_fill(lambda copy: copy.wait())

        @pl.when(i > 0)
        def _prev_done():
            dispatched(prev).wait()
            dispatched(prev).wait()
        dispatched(cur).wait()
        dispatched(cur).wait()


def _outproj(y_ssm, y_att, x, mod4, w_out, ln_g, ln_b, wr_hi, wr_lo, rb, t, blk, n_blk):
    n = x.shape[0]
    tm = min(OUTPROJ_ROWS, t)
    nt = t // tm
    assert n // blk + 1 <= 128 and n_blk <= 256, "block-id table: 128 blocks per expert, ids exact in bf16"
    vec = lambda k: pl.BlockSpec((1, 1, 1, D), lambda i, k=k: (k, i // nt, 0, 0))
    full = lambda shape: pl.BlockSpec(shape, lambda i: (0,) * len(shape))
    rows = lambda w: pl.BlockSpec((tm, w), lambda i: (i, 0))
    return pl.pallas_call(
        functools.partial(_outproj_kernel, tm=tm, blk=blk, n_blk=n_blk),
        grid=(n // tm,),
        in_specs=[rows(D_SSM), rows(D_ATT), rows(D), vec(2), vec(4), vec(3),
                  full((D, D)), full((1, D)), full((1, D)), full((D, 128)), full((D, 128)), full((1, 128))],
        out_specs=[rows(D), rows(128), pl.BlockSpec((8, tm), lambda i: (0, i)), full((8, 128)), full((128, 128)),
                   pl.BlockSpec(memory_space=pl.ANY)],
        out_shape=[jax.ShapeDtypeStruct((n, D), f32), jax.ShapeDtypeStruct((n, 128), f32),
                   jax.ShapeDtypeStruct((8, n), i32), jax.ShapeDtypeStruct((8, 128), f32),
                   jax.ShapeDtypeStruct((128, 128), f32),
                   jax.ShapeDtypeStruct((n_blk * blk, ROW_TILE, 128), f32)],
        scratch_shapes=[pltpu.VMEM((1, 128), f32), pltpu.VMEM((1, 128), f32), pltpu.VMEM((128, 128), f32),
                        pltpu.VMEM((2, tm * ROW_TILE, 128), f32),
                        pltpu.VMEM((2, 8, tm), i32), pltpu.SMEM((2, 8, tm), i32),
                        pltpu.VMEM((8, 128), i32), pltpu.SMEM((8, 128), i32),
                        pltpu.VMEM((blk, ROW_TILE, 128), f32),
                        pltpu.SemaphoreType.DMA((2,)), pltpu.SemaphoreType.DMA((2,)), pltpu.SemaphoreType.DMA(())],
        compiler_params=pltpu.CompilerParams(dimension_semantics=("arbitrary",), vmem_limit_bytes=VMEM_LIMIT),
        name="outproj",
    )(y_ssm, y_att, x, mod4, mod4, mod4, w_out, ln_g, ln_b, wr_hi, wr_lo, rb)


def _tile_copy(src_hbm, row, buf, slot, sem):
    return pltpu.make_async_copy(src_hbm.at[row], buf.at[pl.ds(slot * ROW_TILE, ROW_TILE)], sem)


def _moe_kernel(be_ref, br_ref, nu_ref, nxt_ref, par_ref, x_ref, wg_hbm, wu_hbm, wd_hbm, y_ref,
                wfg, wfu, wfd, wgb, wub, wdb, wsem, *, blk):
    i = pl.program_id(0)
    used = i < nu_ref[0]

    def fetch(e, slot):
        return [pltpu.make_async_copy(src.at[e], dst.at[slot], wsem.at[slot])
                for src, dst in ((wg_hbm, wfg), (wu_hbm, wfu), (wd_hbm, wfd))]

    @pl.when(i == 0)
    def _first():
        for c in fetch(be_ref[0], par_ref[0]):
            c.start()

    @pl.when(jnp.logical_and(used, jnp.logical_or(i == 0, be_ref[i] != be_ref[jnp.maximum(i - 1, 0)])))
    def _switch():
        slot = par_ref[i]
        for c in fetch(be_ref[i], slot):
            c.wait()

        @pl.when(nxt_ref[i] >= 0)
        def _next():
            for c in fetch(nxt_ref[i], 1 - slot):
                c.start()
        wgb[...] = wfg[slot].astype(bf16)
        wub[...] = wfu[slot].astype(bf16)
        wdb[...] = wfd[slot].astype(bf16)

    @pl.when(jnp.logical_not(used))
    def _spare():
        y_ref[...] = jnp.zeros_like(y_ref)

    @pl.when(used)
    def _work():
        xb = _load_row_tiles(x_ref, blk).astype(bf16)
        hid = (_silu(_dot(xb, wgb[...])) * _dot(xb, wub[...])).astype(bf16)
        _store_row_tiles(y_ref, _dot(hid, wdb[...]))


def _moe(block_e, block_row, n_used, next_e, parity, x_rows, w_gate, w_up, w_down, blk):
    nblk = block_e.shape[0]
    rspec = pl.BlockSpec((blk * ROW_TILE, 128), lambda i, be, br, *_: (br[i], 0))
    xspec = pl.BlockSpec((blk * ROW_TILE, 128), lambda i, be, br, nu, *_: (br[jnp.minimum(i, nu[0] - 1)], 0))
    anyspec = pl.BlockSpec(memory_space=pl.ANY)
    grid_spec = pltpu.PrefetchScalarGridSpec(
        num_scalar_prefetch=5,
        grid=(nblk,),
        in_specs=[xspec, anyspec, anyspec, anyspec],
        out_specs=rspec,
        scratch_shapes=[pltpu.VMEM((2, D, D_EXPERT), f32), pltpu.VMEM((2, D, D_EXPERT), f32),
                        pltpu.VMEM((2, D_EXPERT, D), f32),
                        pltpu.VMEM((D, D_EXPERT), bf16), pltpu.VMEM((D, D_EXPERT), bf16),
                        pltpu.VMEM((D_EXPERT, D), bf16),
                        pltpu.SemaphoreType.DMA((2,))],
    )
    return pl.pallas_call(
        functools.partial(_moe_kernel, blk=blk),
        grid_spec=grid_spec,
        out_shape=jax.ShapeDtypeStruct(x_rows.shape, f32),
        compiler_params=pltpu.CompilerParams(dimension_semantics=("arbitrary",), vmem_limit_bytes=VMEM_LIMIT),
        name="moe",
    )(block_e, block_row, n_used, next_e, parity, x_rows, w_gate, w_up, w_down)


def _combine_kernel(d1_ref, d2_ref, y_hbm, x1_ref, rt_ref, gt_ref, lng_ref, lnb_ref, o_ref, buf, sem, *, tm):
    i = pl.program_id(0)
    last = pl.num_programs(0) - 1

    @pl.when(i == 0)
    def _prologue():
        for s in range(2):
            def issue(r, c, s=s):
                base = jnp.minimum(s, last) * tm
                _tile_copy(y_hbm, d1_ref[base + r], buf.at[s, 0], r, sem.at[s]).start()
                _tile_copy(y_hbm, d2_ref[base + r], buf.at[s, 1], r, sem.at[s]).start()
                return c
            lax.fori_loop(0, tm, issue, 0)

    def rows_ready(slot):
        return pltpu.make_async_copy(buf.at[slot], buf.at[slot], sem.at[slot])

    slot = i % 3
    nxt = (i + 2) % 3
    base = jnp.minimum(i + 2, last) * tm
    rows_ready(slot).wait()
    for r in range(tm):
        _tile_copy(y_hbm, d1_ref[base + r], buf.at[nxt, 0], r, sem.at[nxt]).start(priority=0)
        _tile_copy(y_hbm, d2_ref[base + r], buf.at[nxt, 1], r, sem.at[nxt]).start(priority=1)
    rt = rt_ref[...]
    moe = rt[:, 4:5] * _load_row_tiles(buf.at[slot, 0], tm) + rt[:, 5:6] * _load_row_tiles(buf.at[slot, 1], tm)
    y = ALPHA * x1_ref[...] + (1.0 + gt_ref[0, 0]) * moe
    o_ref[...] = _layer_norm(y, lng_ref[...], lnb_ref[...])

    @pl.when(i == last)
    def _drain():
        rows_ready((i + 1) % 3).wait()
        rows_ready((i + 2) % 3).wait()


def _combine(dest1, dest2, y_rows, x1, rt, mod4, ln_g, ln_b, t):
    n = x1.shape[0]
    tm = min(COMBINE_ROWS, t)
    nt = t // tm
    full = lambda shape: pl.BlockSpec(shape, lambda i, d1, d2: (0,) * len(shape))
    rows = lambda w: pl.BlockSpec((tm, w), lambda i, d1, d2: (i, 0))
    grid_spec = pltpu.PrefetchScalarGridSpec(
        num_scalar_prefetch=2,
        grid=(n // tm,),
        in_specs=[pl.BlockSpec(memory_space=pl.ANY), rows(D), rows(128),
                  pl.BlockSpec((1, 1, 1, D), lambda i, d1, d2: (5, i // nt, 0, 0)),
                  full((1, D)), full((1, D))],
        out_specs=rows(D),
        scratch_shapes=[pltpu.VMEM((3, 2, tm * ROW_TILE, 128), f32), pltpu.SemaphoreType.DMA((3,))],
    )
    return pl.pallas_call(
        functools.partial(_combine_kernel, tm=tm),
        grid_spec=grid_spec,
        out_shape=jax.ShapeDtypeStruct((n, D), f32),
        compiler_params=pltpu.CompilerParams(dimension_semantics=("arbitrary",), vmem_limit_bytes=VMEM_LIMIT),
        name="combine",
    )(dest1, dest2, y_rows, x1, rt, mod4, ln_g, ln_b)


def _hi_lo(w):
    hi = w.astype(bf16)
    return hi, (w - hi.astype(f32)).astype(bf16)


def _pad_lanes(v, offset, width):
    return jnp.zeros((width,), f32).at[offset:offset + v.shape[0]].set(v)


def _layer(x, c, ada_w, ada_b, w_in, conv_w, conv_b, dt_bias, a_log, d_skip, ssm_norm_g, fg_bias, att_norm_g,
           w_out, ln1_g, ln1_b, router_g_w, router_g_b, router_e_w, router_e_b, w_gate, w_up, w_down, ln2_g, ln2_b):
    bsz, t, _ = x.shape
    n = bsz * t

    mod4 = _ada(c, ada_w, ada_b).reshape(6, bsz, 1, D)

    w_main, w_small = _pack_w_in(w_in)
    zx, bc, qkv, dtf = _inproj(x, mod4, w_main, w_small, conv_w, conv_b.reshape(1, -1))

    pc = jnp.stack([_pad_lanes(dt_bias, 0, 128), _pad_lanes(a_log, 0, 128), _pad_lanes(fg_bias, 8, 128)]
                   + [jnp.zeros((128,), f32)] * 5)
    rep = lambda v: jnp.repeat(v, HEAD_DIM)
    pe = jnp.stack([rep(dt_bias), rep(a_log), rep(d_skip), ssm_norm_g] + [jnp.zeros((D_SSM,), f32)] * 4)
    y_ssm, cumc = _ssd(zx, bc, dtf, pc, pe)

    y_att = _attn(qkv, cumc, att_norm_g.reshape(1, -1))

    wr = jnp.concatenate([router_g_w, router_e_w, jnp.zeros((D, 128 - N_GROUPS_R - N_EXPERTS), f32)], axis=1)
    wr_hi, wr_lo = _hi_lo(wr)
    rb = jnp.concatenate([router_g_b, router_e_b, jnp.zeros((128 - N_GROUPS_R - N_EXPERTS,), f32)]).reshape(1, 128)
    blk = MOE_ROWS
    nblk = (2 * n) // blk + N_EXPERTS
    x1, rt, dest, cnt, tbl, x_rows = _outproj(y_ssm.reshape(n, D_SSM), y_att.reshape(n, D_ATT), x.reshape(n, D), mod4,
                                              w_out.astype(bf16), ln1_g.reshape(1, D), ln1_b.reshape(1, D),
                                              wr_hi, wr_lo, rb, t, blk, nblk)

    counts = cnt[0, N_GROUPS_R:N_GROUPS_R + N_EXPERTS].astype(i32)
    nb = (counts + blk - 1) // blk
    nb_end = jnp.cumsum(nb)
    n_used = nb_end[N_EXPERTS - 1:]
    every = jnp.arange(nblk, dtype=i32)
    step = jnp.minimum(every, n_used[0] - 1)
    owner = (nb_end[None, :] <= step[:, None]).astype(i32)
    block_e = jnp.sum(owner, axis=1)
    first = jnp.sum(owner * nb[None, :], axis=1)
    lanes = jnp.arange(128, dtype=i32)[None, :]
    nth_row = jnp.dot((lanes == (step - first)[:, None]).astype(f32), tbl, precision=lax.Precision.HIGHEST)
    block_id = jnp.sum(jnp.where(lanes == (block_e + N_GROUPS_R)[:, None], nth_row, 0.0), axis=1).astype(i32)
    block_row = jnp.where(every < n_used[0], block_id, every)
    run_end = first + jnp.sum((jnp.arange(N_EXPERTS)[None, :] == block_e[:, None]) * nb[None, :], axis=1)
    next_e = jnp.where(run_end < n_used[0], jnp.sum((nb_end[None, :] <= run_end[:, None]).astype(i32), axis=1), -1)
    parity = jnp.sum(((jnp.arange(N_EXPERTS)[None, :] < block_e[:, None]) & (nb[None, :] > 0)).astype(i32), axis=1) % 2

    y_rows = _moe(block_e, block_row, n_used, next_e, parity, x_rows.reshape(nblk * blk * ROW_TILE, 128),
                  w_gate, w_up, w_down, blk)
    out = _combine(dest[0], dest[1], y_rows.reshape(nblk * blk, ROW_TILE, 128), x1, rt, mod4,
                   ln2_g.reshape(1, D), ln2_b.reshape(1, D), t)
    return out.reshape(bsz, t, D)


def kernel(x, c, ada_w, ada_b, w_in, conv_w, conv_b, dt_bias, a_log, d_skip, ssm_norm_g, fg_bias, att_norm_g, w_out,
           ln1_g, ln1_b, router_g_w, router_g_b, router_e_w, router_e_b, w_gate, w_up, w_down, ln2_g, ln2_b):
    depth = ada_w.shape[0]
    for l in range(depth):
        x = _layer(x, c, ada_w[l], ada_b[l], w_in[l], conv_w[l], conv_b[l], dt_bias[l], a_log[l], d_skip[l],
                   ssm_norm_g[l], fg_bias[l], att_norm_g[l], w_out[l], ln1_g[l], ln1_b[l], router_g_w[l],
                   router_g_b[l], router_e_w[l], router_e_b[l], w_gate[l], w_up[l], w_down[l], ln2_g[l], ln2_b[l])
    return x
```

```python
import functools

import jax
import jax.numpy as jnp
import numpy as np
from jax import lax
from jax.experimental import pallas as pl
from jax.experimental.pallas import tpu as pltpu

f32 = jnp.float32
bf16 = jnp.bfloat16
i32 = jnp.int32

D = 1024
D_SSM = 512
D_ATT = 512
HEAD_DIM = 64
GROUP_W = 256
N_STATE = 128
CONV_K = 4
N_GROUPS_R = 4
EXPERTS_PER_GROUP = 8
N_EXPERTS = 32
D_EXPERT = 512
ALPHA = 2.0 ** 0.25
EPS = 1e-5
NEG = -1e30
LOG2E = 1.4426950408889634
QK_SCALE = HEAD_DIM ** -0.5 * LOG2E
V_ROWS = 80

SSD_CHUNK = 256
ATT_BLOCK = 256
INPROJ_ROWS = 512
OUTPROJ_ROWS = 512
OUTPROJ_PARTS = 2
MOE_ROWS = 256
COMBINE_ROWS = 256
DEST_ROWS = 2048
VMEM_LIMIT = 48 * 1024 * 1024


def _dot(a, b):
    return jnp.dot(a, b, preferred_element_type=f32)


def _dot_nt(a, b):
    return lax.dot_general(a, b, (((1,), (1,)), ((), ())), preferred_element_type=f32)


def _dot_tn(a, b):
    return lax.dot_general(a, b, (((0,), (0,)), ((), ())), preferred_element_type=f32)


def _split3(v):
    hi = v.astype(bf16)
    r1 = v - hi.astype(f32)
    mid = r1.astype(bf16)
    lo = (r1 - mid.astype(f32)).astype(bf16)
    return hi, mid, lo


def _dot_exact_lhs(m, v):
    hi, mid, lo = _split3(v)
    return (_dot(m, hi) + _dot(m, mid)) + _dot(m, lo)


def _dot_exact_rhs(v, m):
    hi, mid, lo = _split3(v)
    return (_dot(hi, m) + _dot(mid, m)) + _dot(lo, m)


ROW_TILE = 8


def _store_row_tiles(ref, val):
    rows = val.shape[0]
    for c in range(ROW_TILE):
        ref[pl.ds(c, rows, stride=ROW_TILE), :] = val[:, c * 128:(c + 1) * 128]


def _load_row_tiles(ref, rows):
    return jnp.concatenate([ref[pl.ds(c, rows, stride=ROW_TILE), :] for c in range(ROW_TILE)], axis=1)


def _softplus(x):
    return jnp.maximum(x, 0.0) + jnp.log1p(jnp.exp(-jnp.abs(x)))


def _silu(x):
    return x * jax.nn.sigmoid(x)


def _ada_kernel(c_ref, w_ref, b_ref, o_ref):
    s = _silu(c_ref[...]).astype(bf16)
    o_ref[0] = _dot(s, w_ref[...].astype(bf16)) + b_ref[0]


def _ada(c, w, b):
    bsz = c.shape[0]
    return pl.pallas_call(
        _ada_kernel,
        grid=(6,),
        in_specs=[pl.BlockSpec((bsz, D), lambda j: (0, 0)),
                  pl.BlockSpec((D, D), lambda j: (0, j)),
                  pl.BlockSpec((1, 1, D), lambda j: (j, 0, 0))],
        out_specs=pl.BlockSpec((1, bsz, D), lambda j: (j, 0, 0)),
        out_shape=jax.ShapeDtypeStruct((6, bsz, D), f32),
        compiler_params=pltpu.CompilerParams(dimension_semantics=("arbitrary",), vmem_limit_bytes=VMEM_LIMIT),
        name="ada",
    )(c, w, b.reshape(6, 1, D))


W_IN_COLS = 3088
PACK_ROWS = 256


def _pack_w_in_kernel(wt_ref, wm_ref, ws_ref):
    for j in range(3072 // PACK_ROWS):
        src = j * PACK_ROWS if j * PACK_ROWS < 1536 else j * PACK_ROWS + 8
        wm_ref[:, j * PACK_ROWS:(j + 1) * PACK_ROWS] = wt_ref[src:src + PACK_ROWS, :].T.astype(bf16)
    small = jnp.concatenate([wt_ref[1536:1544, :], wt_ref[W_IN_COLS - 8:W_IN_COLS, :], jnp.zeros((112, D), f32)],
                            axis=0).T
    hi = small.astype(bf16)
    ws_ref[:, 0:128] = hi
    ws_ref[:, 128:256] = (small - hi.astype(f32)).astype(bf16)


def _pack_w_in(w_in):
    full = lambda shape: pl.BlockSpec(shape, lambda i: (0,) * len(shape))
    return pl.pallas_call(
        _pack_w_in_kernel,
        grid=(1,),
        in_specs=[full((W_IN_COLS, D))],
        out_specs=[full((D, 3072)), full((D, 256))],
        out_shape=[jax.ShapeDtypeStruct((D, 3072), bf16), jax.ShapeDtypeStruct((D, 256), bf16)],
        compiler_params=pltpu.CompilerParams(dimension_semantics=("arbitrary",), vmem_limit_bytes=VMEM_LIMIT),
        name="pack_w_in",
    )(w_in.T)


def _inproj_kernel(x_ref, sc_ref, sh_ref, wm_ref, ws_ref, cw_ref, cb_ref, zx_ref, bc_ref, qkv_ref, dtf_ref, xcat, *, tm):
    i = pl.program_id(1)

    @pl.when(i == 0)
    def _init():
        xcat[0:8, :] = jnp.zeros((8, 2 * D_SSM), f32)

    u = x_ref[0] * (1.0 + sc_ref[0, 0]) + sh_ref[0, 0]
    ub = u.astype(bf16)
    xcat[8:8 + tm, 0:512] = _dot(ub, wm_ref[:, 512:1024])
    xcat[8:8 + tm, 512:1024] = _dot(ub, wm_ref[:, 1024:1536])
    zx_ref[0, :, 0:512] = _dot(ub, wm_ref[:, 0:512])
    acc = cw_ref[0:1, :] * xcat[5:5 + tm, :] + cb_ref[...]
    for k in range(1, CONV_K):
        acc = acc + cw_ref[k:k + 1, :] * xcat[5 + k:5 + k + tm, :]
    xcat[0:8, :] = xcat[tm:tm + 8, :]
    xbc = _silu(acc)
    zx_ref[0, :, 512:1024] = xbc[:, 0:512]
    bc_ref[0] = xbc[:, 512:1024].astype(bf16)
    for j, scale in enumerate((QK_SCALE, 1.0, 1.0)):
        qkv_ref[0, :, j * 512:(j + 1) * 512] = (
            _dot(ub, wm_ref[:, 1536 + j * 512:1536 + (j + 1) * 512]) * scale).astype(bf16)
    ul = (u - ub.astype(f32)).astype(bf16)
    d_hl = _dot(ub, ws_ref[...])
    dtf_ref[0] = (d_hl[:, 0:128] + _dot(ul, ws_ref[:, 0:128])) + d_hl[:, 128:256]


def _inproj(x, mod4, w_main, ws, conv_w, conv_b):
    bsz, t, _ = x.shape
    tm = min(INPROJ_ROWS, t)
    vec = lambda k: pl.BlockSpec((1, 1, 1, D), lambda b, i, k=k: (k, b, 0, 0))
    full = lambda shape: pl.BlockSpec(shape, lambda b, i: (0,) * len(shape))
    rows = lambda w: pl.BlockSpec((1, tm, w), lambda b, i: (b, i, 0))
    return pl.pallas_call(
        functools.partial(_inproj_kernel, tm=tm),
        grid=(bsz, t // tm),
        in_specs=[rows(D), vec(1), vec(0), full((D, 3072)), full((D, 256)),
                  full((CONV_K, 2 * D_SSM)), full((1, 2 * D_SSM))],
        out_specs=[rows(1024), rows(512), rows(1536), rows(128)],
        out_shape=[jax.ShapeDtypeStruct((bsz, t, 1024), f32),
                   jax.ShapeDtypeStruct((bsz, t, 512), bf16),
                   jax.ShapeDtypeStruct((bsz, t, 1536), bf16),
                   jax.ShapeDtypeStruct((bsz, t, 128), f32)],
        scratch_shapes=[pltpu.VMEM((tm + 8, 2 * D_SSM), f32)],
        compiler_params=pltpu.CompilerParams(dimension_semantics=("parallel", "arbitrary"),
                                             vmem_limit_bytes=VMEM_LIMIT),
        name="inproj",
    )(x, mod4, mod4, w_main, ws, conv_w, conv_b)


def _ssd_kernel(z_ref, xs_ref, bc_ref, dtf_ref, pc_ref, pe_ref, y_ref, cumc_ref, state, carry, *, lc):
    j = pl.program_id(1)

    @pl.when(j == 0)
    def _init():
        state[...] = jnp.zeros_like(state)
        carry[...] = jnp.zeros_like(carry)

    xs = xs_ref[0]
    bm = bc_ref[0, :, 0:256]
    cm = bc_ref[0, :, 256:512]

    dtf = dtf_ref[0]
    lane = lax.broadcasted_iota(i32, (lc, 128), 1)
    dt_c = _softplus(dtf + pc_ref[0:1, :])
    a_c = dt_c * (-jnp.exp(pc_ref[1:2, :]))
    logf = -_softplus(-(dtf + pc_ref[2:3, :]))
    v = jnp.where(lane < 8, a_c, logf)
    r_i = lax.broadcasted_iota(i32, (lc, lc), 0)
    c_i = lax.broadcasted_iota(i32, (lc, lc), 1)
    tri = r_i >= c_i
    tri_b = jnp.where(tri, 1.0, 0.0).astype(bf16)
    cum = _dot_exact_lhs(tri_b, v) + carry[...]
    carry[...] = jnp.where(lane[0:1, :] >= 8, cum[lc - 1:lc, :], 0.0)
    cumc_ref[0] = cum
    cs_t = cum.T[0:8, :]

    e_r = lax.broadcasted_iota(i32, (128, D_SSM), 0)
    e_c = lax.broadcasted_iota(i32, (128, D_SSM), 1)
    expand = jnp.where(jnp.right_shift(e_c, 6) == e_r, 1.0, 0.0).astype(bf16)
    dt_e = _dot_exact_rhs(dt_c, expand)
    cs_e = _dot_exact_rhs(cum, expand)

    xdt = xs * dt_e
    ecs = jnp.exp(cs_e)
    cs_last = cs_e[lc - 1:lc, :]
    dec_st = jnp.exp(cs_last - cs_e)
    lane_g = lax.broadcasted_iota(i32, (1, GROUP_W), 1)
    ys = []
    for g in range(2):
        gs = slice(g * GROUP_W, (g + 1) * GROUP_W)
        bg = bm[:, g * N_STATE:(g + 1) * N_STATE]
        cg = cm[:, g * N_STATE:(g + 1) * N_STATE]
        cb = _dot_nt(cg, bg)
        xdt_g = xdt[:, gs]
        xdt_gb = xdt_g.astype(bf16)
        ms, xb = [], []
        for hh in range(4):
            h = g * 4 + hh
            lm = jnp.exp(jnp.where(tri, cum[:, h:h + 1] - cs_t[h:h + 1, :], -jnp.inf))
            ms.append((cb * lm).astype(bf16))
            xb.append(jnp.where(jnp.right_shift(lane_g, 6) == hh, xdt_gb, jnp.zeros_like(xdt_gb)))
        y_diag = _dot(jnp.concatenate(ms, axis=1), jnp.concatenate(xb, axis=0))
        st = state[g]
        y_off = _dot(cg, st.astype(bf16)) * ecs[:, gs]
        upd = _dot_tn(bg, (xdt_g * dec_st[:, gs]).astype(bf16))
        state[g] = st * jnp.exp(cs_last[:, gs]) + upd
        ys.append(y_diag + y_off + xs[:, gs] * pe_ref[2:3, gs])

    outs = []
    for g in range(2):
        gs = slice(g * GROUP_W, (g + 1) * GROUP_W)
        yg = ys[g] * _silu(z_ref[0, :, gs])
        ms_ = jnp.mean(yg * yg, axis=-1, keepdims=True)
        outs.append(yg * lax.rsqrt(ms_ + EPS))
    y_ref[0] = (jnp.concatenate(outs, axis=1) * pe_ref[3:4, :]).astype(bf16)


def _ssd(zx, bc, dtf, pc, pe):
    bsz, t, _ = zx.shape
    lc = min(SSD_CHUNK, t)
    col = lambda k: pl.BlockSpec((1, lc, 512), lambda b, j, k=k: (b, j, k))
    full = lambda shape: pl.BlockSpec(shape, lambda b, j: (0,) * len(shape))
    return pl.pallas_call(
        functools.partial(_ssd_kernel, lc=lc),
        grid=(bsz, t // lc),
        in_specs=[col(0), col(1), col(0),
                  pl.BlockSpec((1, lc, 128), lambda b, j: (b, j, 0)),
                  full((8, 128)), full((8, D_SSM))],
        out_specs=[pl.BlockSpec((1, lc, D_SSM), lambda b, j: (b, j, 0)),
                   pl.BlockSpec((1, lc, 128), lambda b, j: (b, j, 0))],
        out_shape=[jax.ShapeDtypeStruct((bsz, t, D_SSM), bf16),
                   jax.ShapeDtypeStruct((bsz, t, 128), f32)],
        scratch_shapes=[pltpu.VMEM((2, N_STATE, GROUP_W), f32),
                        pltpu.VMEM((1, 128), f32)],
        compiler_params=pltpu.CompilerParams(dimension_semantics=("parallel", "arbitrary"),
                                             vmem_limit_bytes=VMEM_LIMIT),
        name="ssd",
    )(zx, zx, bc, dtf, pc, pe)


def _attn_kernel(q_ref, k_ref, v_ref, cc_ref, psel_ref, ng_ref, o_ref, kaug, vt, acc, sc0, sc1, *, tq, t):
    i = pl.program_id(1)
    nkb = t // tq
    n_heads = D_ATT // HEAD_DIM
    lane = lax.broadcasted_iota(i32, (1, 128), 1)
    lo_half = lane < HEAD_DIM

    @pl.when(i == 0)
    def _build():
        eye = jnp.where(lax.broadcasted_iota(i32, (D_ATT, D_ATT), 0) == lax.broadcasted_iota(i32, (D_ATT, D_ATT), 1),
                        1.0, 0.0).astype(bf16)
        ones_rows = jnp.where(lax.broadcasted_iota(i32, (V_ROWS - HEAD_DIM, tq), 0) == 0, 1.0, 0.0).astype(bf16)
        for jb in range(nkb):
            rows = slice(jb * tq, (jb + 1) * tq)
            v_t = _dot_nt(eye, v_ref[0, rows, :]).astype(bf16)
            for h in range(n_heads):
                vt[jb, h * V_ROWS:h * V_ROWS + HEAD_DIM, :] = v_t[h * HEAD_DIM:(h + 1) * HEAD_DIM, :]
                vt[jb, h * V_ROWS + HEAD_DIM:(h + 1) * V_ROWS, :] = ones_rows
            pieces = jnp.concatenate(_split3(cc_ref[0, rows, :] * (-LOG2E)), axis=1)
            for p in range(n_heads // 2):
                a = _dot(pieces, psel_ref[p]).astype(bf16)
                kp = k_ref[0, rows, p * 128:(p + 1) * 128]
                kaug[2 * p, rows, :] = jnp.where(lo_half, kp, a)
                kaug[2 * p + 1, rows, :] = jnp.where(lo_half, a, kp)

    ones_hi = jnp.where((lane >= HEAD_DIM) & (lane < HEAD_DIM + 3), 1.0, 0.0).astype(bf16)
    ones_lo = jnp.where(lane < 3, 1.0, 0.0).astype(bf16)
    qa = []
    for p in range(n_heads // 2):
        qp = q_ref[0, :, p * 128:(p + 1) * 128]
        qa.append(jnp.where(lo_half, qp, ones_hi))
        qa.append(jnp.where(lo_half, ones_lo, qp))
    keep = lax.broadcasted_iota(i32, (tq, tq), 0) <= lax.broadcasted_iota(i32, (tq, tq), 1)
    acc[...] = jnp.zeros_like(acc)

    def score(jb, buf):
        k0 = pl.multiple_of(jb * tq, tq)
        for h in range(n_heads):
            buf[h] = _dot_nt(kaug[h, pl.ds(k0, tq), :], qa[h])

    def absorb(jb, ms, masked, buf):
        new_ms = []
        for h in range(n_heads):
            s = buf[h]
            if masked:
                s = jnp.where(keep, s, NEG)
            m_new = jnp.maximum(ms[h], jnp.max(s, axis=0, keepdims=True))
            alpha = jnp.exp2(ms[h] - m_new)
            p = jnp.exp2(s - m_new).astype(bf16)
            new_ms.append(m_new)
            acc[h] = acc[h] * alpha + _dot(vt[jb, h * V_ROWS:(h + 1) * V_ROWS, :], p)
        return tuple(new_ms)

    def pair(pp, ms):
        j0 = 2 * pp
        score(j0 + 1, sc1)
        ms = absorb(j0, ms, False, sc0)
        score(j0 + 2, sc0)
        return absorb(j0 + 1, ms, False, sc1)

    score(0, sc0)
    ms = lax.fori_loop(0, i // 2, pair, tuple(jnp.full((1, tq), NEG, f32) for _ in range(n_heads)))

    @pl.when(i % 2 == 0)
    def _even():
        absorb(i, ms, True, sc0)

    @pl.when(i % 2 == 1)
    def _odd():
        score(i, sc1)
        absorb(i, absorb(i - 1, ms, False, sc0), True, sc1)

    out_t = jnp.concatenate([acc[h, 0:HEAD_DIM, :] * (1.0 / acc[h, HEAD_DIM:HEAD_DIM + 1, :]) for h in range(n_heads)],
                            axis=0)
    ms_ = jnp.mean(out_t * out_t, axis=0, keepdims=True)
    out_t = out_t * lax.rsqrt(ms_ + EPS)
    o_ref[0] = (out_t.T * ng_ref[...]).astype(bf16)


def _piece_select():
    sel = np.zeros((4, 384, 128), np.float32)
    for pair in range(4):
        for j in range(3):
            sel[pair, j * 128 + 8 + 2 * pair, HEAD_DIM + j] = 1.0
            sel[pair, j * 128 + 8 + 2 * pair + 1, j] = 1.0
    return jnp.asarray(sel, bf16)


def _attn(qkv, cumc, norm_g):
    bsz, t, _ = qkv.shape
    tq = min(ATT_BLOCK, t)
    return pl.pallas_call(
        functools.partial(_attn_kernel, tq=tq, t=t),
        grid=(bsz, t // tq),
        in_specs=[pl.BlockSpec((1, tq, D_ATT), lambda b, i: (b, i, 0)),
                  pl.BlockSpec((1, t, D_ATT), lambda b, i: (b, 0, 1)),
                  pl.BlockSpec((1, t, D_ATT), lambda b, i: (b, 0, 2)),
                  pl.BlockSpec((1, t, 128), lambda b, i: (b, 0, 0)),
                  pl.BlockSpec((4, 384, 128), lambda b, i: (0, 0, 0)),
                  pl.BlockSpec((1, D_ATT), lambda b, i: (0, 0))],
        out_specs=pl.BlockSpec((1, tq, D_ATT), lambda b, i: (b, i, 0)),
        out_shape=jax.ShapeDtypeStruct((bsz, t, D_ATT), bf16),
        scratch_shapes=[pltpu.VMEM((D_ATT // HEAD_DIM, t, 128), bf16),
                        pltpu.VMEM((t // tq, (D_ATT // HEAD_DIM) * V_ROWS, tq), bf16),
                        pltpu.VMEM((D_ATT // HEAD_DIM, V_ROWS, tq), f32),
                        pltpu.VMEM((D_ATT // HEAD_DIM, tq, tq), f32),
                        pltpu.VMEM((D_ATT // HEAD_DIM, tq, tq), f32)],
        compiler_params=pltpu.CompilerParams(dimension_semantics=("parallel", "arbitrary"),
                                             vmem_limit_bytes=VMEM_LIMIT),
        name="attn",
    )(qkv, qkv, qkv, cumc, _piece_select(), norm_g)


def _layer_norm(y, g, b):
    mu = jnp.mean(y, axis=-1, keepdims=True)
    yc = y - mu
    var = jnp.mean(yc * yc, axis=-1, keepdims=True)
    return yc * lax.rsqrt(var + EPS) * g + b


def _outproj_kernel(ys_ref, ya_ref, x_ref, gt_ref, sc_ref, sh_ref, wo_ref, lng_ref, lnb_ref, wrh_ref, wrl_ref, rb_ref,
                    x1_ref, rt_ref, dest_ref, cnt_ref, tbl_ref, xrows_hbm,
                    carry, galloc, tbl, u2t, didx_v, didx_s, cnt_v, cnt_s, zeros, idx_sem, sc_sem, z_sem,
                    *, tm, blk, n_blk):
    i = pl.program_id(0)
    last = pl.num_programs(0) - 1
    cur = i % 2
    prev = 1 - cur

    def idx_ready(slot):
        return pltpu.make_async_copy(didx_v.at[slot], didx_s.at[slot], idx_sem.at[slot])

    def dispatched(slot):
        return pltpu.make_async_copy(u2t.at[slot], u2t.at[slot], sc_sem.at[slot])

    def dispatch_copy(slot, r, k):
        return pltpu.make_async_copy(u2t.at[slot, pl.ds(r * ROW_TILE, ROW_TILE)], xrows_hbm.at[didx_s[slot, k, r]],
                                     sc_sem.at[slot])

    @pl.when(i == 0)
    def _init():
        carry[...] = jnp.zeros_like(carry)
        galloc[...] = jnp.zeros_like(galloc)
        tbl[...] = jnp.zeros_like(tbl)

    @pl.when(i >= 2)
    def _reuse():
        dispatched(cur).wait()
        dispatched(cur).wait()

    def step(dispatch_prev):
        n_parts = OUTPROJ_PARTS
        n_slices = 3 * n_parts + 2

        def dispatch_slice(c):
            if dispatch_prev:
                for r_ in range(c * tm // n_slices, (c + 1) * tm // n_slices):
                    dispatch_copy(prev, r_, 0).start(priority=0)
                    dispatch_copy(prev, r_, 1).start(priority=1)

        if dispatch_prev:
            idx_ready(prev).wait()
        part = tm // n_parts
        hs = []
        for a in range(n_parts):
            rs = slice(a * part, (a + 1) * part)
            dispatch_slice(a)
            hs.append(_dot(ys_ref[rs, :], wo_ref[0:D_SSM, :]) + _dot(ya_ref[rs, :], wo_ref[D_SSM:D, :]))
        logit_parts = []
        for a in range(n_parts):
            rs = slice(a * part, (a + 1) * part)
            dispatch_slice(n_parts + 2 * a)
            x1 = _layer_norm(ALPHA * x_ref[rs, :] + (1.0 + gt_ref[0, 0]) * hs[a], lng_ref[...], lnb_ref[...])
            x1_ref[rs, :] = x1
            u2 = x1 * (1.0 + sc_ref[0, 0]) + sh_ref[0, 0]
            _store_row_tiles(u2t.at[cur, pl.ds(a * part * ROW_TILE, part * ROW_TILE)], u2)
            dispatch_slice(n_parts + 2 * a + 1)
            uh = u2.astype(bf16)
            ul = (u2 - uh.astype(f32)).astype(bf16)
            logit_parts.append((_dot(uh, wrh_ref[...]) + _dot(ul, wrh_ref[...])) + _dot(uh, wrl_ref[...]))
        logits = jnp.concatenate(logit_parts, axis=0) + rb_ref[...]
        lane = lax.broadcasted_iota(i32, (tm, 128), 1).astype(f32)
        big = jnp.float32(1e9)

        def first_max(vals):
            m = jnp.max(vals, axis=-1, keepdims=True)
            return m, jnp.min(jnp.where(vals == m, lane, big), axis=-1, keepdims=True)

        gl = jnp.where(lane < N_GROUPS_R, logits, NEG)
        gmax, gidx = first_max(gl)
        g_p = 1.0 / jnp.sum(jnp.exp(gl - gmax), axis=-1, keepdims=True)
        lo = N_GROUPS_R + EXPERTS_PER_GROUP * gidx
        el = jnp.where((lane >= lo) & (lane < lo + EXPERTS_PER_GROUP), logits, NEG)
        m1, i1 = first_max(el)
        el2 = jnp.where(lane == i1, NEG, el)
        m2, i2 = first_max(el2)
        r = jnp.exp(m2 - m1)
        w1 = g_p / (1.0 + r)
        w2 = g_p * r / (1.0 + r)
        dispatch_slice(3 * n_parts)

        oh1 = lane == i1
        oh2 = lane == i2
        oh = jnp.where(oh1 | oh2, 1.0, 0.0)
        r_i = lax.broadcasted_iota(i32, (tm, tm), 0)
        c_i = lax.broadcasted_iota(i32, (tm, tm), 1)
        lower = jnp.where(r_i > c_i, 1.0, 0.0).astype(bf16)
        c_old = carry[...]
        prefix = _dot(lower, oh.astype(bf16)) + c_old
        rank1 = jnp.sum(jnp.where(oh1, prefix, 0.0), axis=-1, keepdims=True)
        rank2 = jnp.sum(jnp.where(oh2, prefix, 0.0), axis=-1, keepdims=True)
        c_new = c_old + jnp.sum(oh, axis=0, keepdims=True)
        carry[...] = c_new
        dispatch_slice(3 * n_parts + 1)

        nb_old = jnp.floor((c_old + (blk - 1)) * (1.0 / blk))
        nb_new = jnp.floor((c_new + (blk - 1)) * (1.0 / blk))
        fresh = nb_new - nb_old
        sq_r = lax.broadcasted_iota(i32, (128, 128), 0)
        sq_c = lax.broadcasted_iota(i32, (128, 128), 1)
        before = jnp.where(sq_r < sq_c, 1.0, 0.0).astype(bf16)
        base = galloc[...] + _dot(jnp.broadcast_to(fresh, (8, 128)).astype(bf16), before)[0:1, :]
        galloc[...] = galloc[...] + jnp.sum(fresh, axis=-1, keepdims=True)
        ordinal = sq_r.astype(f32)
        tbl[...] = jnp.where((ordinal >= nb_old) & (ordinal < nb_new), base + (ordinal - nb_old), tbl[...])
        tbl_ref[...] = tbl[...]
        cnt_ref[...] = jnp.concatenate([c_new, galloc[...], jnp.zeros((6, 128), f32)], axis=0)

        rt = jnp.where(lane == 4, w1, jnp.where(lane == 5, w2, 0.0))
        rt_ref[...] = rt

        ids_bf = tbl[...].astype(bf16)

        def row_of(onehot, rank):
            nth = jnp.floor(rank * (1.0 / blk))
            ids = _dot(jnp.where(lane == nth, 1.0, 0.0).astype(bf16), ids_bf)
            return jnp.sum(jnp.where(onehot, ids, 0.0), axis=-1, keepdims=True) * blk + (rank - nth * blk)

        d1 = row_of(oh1, rank1)
        d2 = row_of(oh2, rank2)
        dd = jnp.where(lane == 0, d1, jnp.where(lane == 1, d2, 0.0)).T[0:8, :].astype(i32)
        dest_ref[...] = dd
        didx_v[cur] = dd
        idx_ready(cur).start()

    pl.when(i == 0)(lambda: step(False))
    pl.when(i > 0)(lambda: step(True))

    @pl.when(i == last)
    def _epilogue():
        idx_ready(cur).wait()

        def issue(r_, c):
            dispatch_copy(cur, r_, 0).start()
            dispatch_copy(cur, r_, 1).start()
            return c
        lax.fori_loop(0, tm, issue, 0)

        cnt = carry[...]
        n_blocks = jnp.floor((cnt + (blk - 1)) * (1.0 / blk))
        ordinal = lax.broadcasted_iota(i32, (128, 128), 0).astype(f32)
        last_id = jnp.sum(jnp.where(ordinal == n_blocks - 1.0, tbl[...], 0.0), axis=0, keepdims=True)
        used = cnt - (n_blocks - 1.0) * blk
        cnt_v[...] = jnp.concatenate([last_id * blk + used, blk - used, galloc[...], jnp.zeros((5, 128), f32)],
                                     axis=0).astype(i32)
        to_smem = pltpu.make_async_copy(cnt_v, cnt_s, z_sem)
        to_smem.start()
        to_smem.wait()
        zeros[...] = jnp.zeros_like(zeros)
        sizes = [1 << b for b in reversed(range(blk.bit_length() - 1))]

        def for_each_fill(fn):
            def tail(e, c):
                pad = cnt_s[1, N_GROUPS_R + e]
                off = cnt_s[0, N_GROUPS_R + e]
                for sz in sizes:
                    @pl.when((pad & sz) != 0)
                    def _(off=off, sz=sz):
                        fn(pltpu.make_async_copy(zeros.at[pl.ds(0, sz)], xrows_hbm.at[pl.ds(off, sz)], z_sem))
                    off = off + (pad & sz)
                return c
            lax.fori_loop(0, N_EXPERTS, tail, 0)

            def whole(b, c):
                fn(pltpu.make_async_copy(zeros, xrows_hbm.at[pl.ds(b * blk, blk)], z_sem))
                return c
            lax.fori_loop(cnt_s[2, 0], n_blk, whole, 0)

        for_each_fill(lambda copy: copy.start())
        for_each_fill(lambda copy: copy.wait())

        @pl.when(i > 0)
        def _prev_done():
            dispatched(prev).wait()
            dispatched(prev).wait()
        dispatched(cur).wait()
        dispatched(cur).wait()


def _outproj(y_ssm, y_att, x, mod4, w_out, ln_g, ln_b, wr_hi, wr_lo, rb, t, blk, n_blk):
    n = x.shape[0]
    tm = min(OUTPROJ_ROWS, t)
    nt = t // tm
    assert n // blk + 1 <= 128 and n_blk <= 256, "block-id table: 128 blocks per expert, ids exact in bf16"
    vec = lambda k: pl.BlockSpec((1, 1, 1, D), lambda i, k=k: (k, i // nt, 0, 0))
    full = lambda shape: pl.BlockSpec(shape, lambda i: (0,) * len(shape))
    rows = lambda w: pl.BlockSpec((tm, w), lambda i: (i, 0))
    return pl.pallas_call(
        functools.partial(_outproj_kernel, tm=tm, blk=blk, n_blk=n_blk),
        grid=(n // tm,),
        in_specs=[rows(D_SSM), rows(D_ATT), rows(D), vec(2), vec(4), vec(3),
                  full((D, D)), full((1, D)), full((1, D)), full((D, 128)), full((D, 128)), full((1, 128))],
        out_specs=[rows(D), rows(128), pl.BlockSpec((8, tm), lambda i: (0, i)), full((8, 128)), full((128, 128)),
                   pl.BlockSpec(memory_space=pl.ANY)],
        out_shape=[jax.ShapeDtypeStruct((n, D), f32), jax.ShapeDtypeStruct((n, 128), f32),
                   jax.ShapeDtypeStruct((8, n), i32), jax.ShapeDtypeStruct((8, 128), f32),
                   jax.ShapeDtypeStruct((128, 128), f32),
                   jax.ShapeDtypeStruct((n_blk * blk, ROW_TILE, 128), f32)],
        scratch_shapes=[pltpu.VMEM((1, 128), f32), pltpu.VMEM((1, 128), f32), pltpu.VMEM((128, 128), f32),
                        pltpu.VMEM((2, tm * ROW_TILE, 128), f32),
                        pltpu.VMEM((2, 8, tm), i32), pltpu.SMEM((2, 8, tm), i32),
                        pltpu.VMEM((8, 128), i32), pltpu.SMEM((8, 128), i32),
                        pltpu.VMEM((blk, ROW_TILE, 128), f32),
                        pltpu.SemaphoreType.DMA((2,)), pltpu.SemaphoreType.DMA((2,)), pltpu.SemaphoreType.DMA(())],
        compiler_params=pltpu.CompilerParams(dimension_semantics=("arbitrary",), vmem_limit_bytes=VMEM_LIMIT),
        name="outproj",
    )(y_ssm, y_att, x, mod4, mod4, mod4, w_out, ln_g, ln_b, wr_hi, wr_lo, rb)


def _tile_copy(src_hbm, row, buf, slot, sem):
    return pltpu.make_async_copy(src_hbm.at[row], buf.at[pl.ds(slot * ROW_TILE, ROW_TILE)], sem)


def _moe_kernel(be_ref, br_ref, nu_ref, nxt_ref, par_ref, x_ref, wg_hbm, wu_hbm, wd_hbm, y_ref,
                wfg, wfu, wfd, wgb, wub, wdb, wsem, *, blk):
    i = pl.program_id(0)
    used = i < nu_ref[0]

    def fetch(e, slot):
        return [pltpu.make_async_copy(src.at[e], dst.at[slot], wsem.at[slot])
                for src, dst in ((wg_hbm, wfg), (wu_hbm, wfu), (wd_hbm, wfd))]

    @pl.when(i == 0)
    def _first():
        for c in fetch(be_ref[0], par_ref[0]):
            c.start()

    @pl.when(jnp.logical_and(used, jnp.logical_or(i == 0, be_ref[i] != be_ref[jnp.maximum(i - 1, 0)])))
    def _switch():
        slot = par_ref[i]
        for c in fetch(be_ref[i], slot):
            c.wait()

        @pl.when(nxt_ref[i] >= 0)
        def _next():
            for c in fetch(nxt_ref[i], 1 - slot):
                c.start(priority=1)
        wgb[...] = wfg[slot].astype(bf16)
        wub[...] = wfu[slot].astype(bf16)
        wdb[...] = wfd[slot].astype(bf16)

    @pl.when(jnp.logical_not(used))
    def _spare():
        y_ref[...] = jnp.zeros_like(y_ref)

    @pl.when(used)
    def _work():
        xb = _load_row_tiles(x_ref, blk).astype(bf16)
        hid = (_silu(_dot(xb, wgb[...])) * _dot(xb, wub[...])).astype(bf16)
        _store_row_tiles(y_ref, _dot(hid, wdb[...]))


def _moe(block_e, block_row, n_used, next_e, parity, x_rows, w_gate, w_up, w_down, blk):
    nblk = block_e.shape[0]
    rspec = pl.BlockSpec((blk * ROW_TILE, 128), lambda i, be, br, *_: (br[i], 0))
    xspec = pl.BlockSpec((blk * ROW_TILE, 128), lambda i, be, br, nu, *_: (br[jnp.minimum(i, nu[0] - 1)], 0))
    anyspec = pl.BlockSpec(memory_space=pl.ANY)
    grid_spec = pltpu.PrefetchScalarGridSpec(
        num_scalar_prefetch=5,
        grid=(nblk,),
        in_specs=[xspec, anyspec, anyspec, anyspec],
        out_specs=rspec,
        scratch_shapes=[pltpu.VMEM((2, D, D_EXPERT), f32), pltpu.VMEM((2, D, D_EXPERT), f32),
                        pltpu.VMEM((2, D_EXPERT, D), f32),
                        pltpu.VMEM((D, D_EXPERT), bf16), pltpu.VMEM((D, D_EXPERT), bf16),
                        pltpu.VMEM((D_EXPERT, D), bf16),
                        pltpu.SemaphoreType.DMA((2,))],
    )
    return pl.pallas_call(
        functools.partial(_moe_kernel, blk=blk),
        grid_spec=grid_spec,
        out_shape=jax.ShapeDtypeStruct(x_rows.shape, f32),
        compiler_params=pltpu.CompilerParams(dimension_semantics=("arbitrary",), vmem_limit_bytes=VMEM_LIMIT),
        name="moe",
    )(block_e, block_row, n_used, next_e, parity, x_rows, w_gate, w_up, w_down)


def _combine_kernel(d1_ref, d2_ref, y_hbm, x1_ref, rt_ref, gt_ref, lng_ref, lnb_ref, o_ref, buf, sem, *, tm):
    i = pl.program_id(0)
    last = pl.num_programs(0) - 1

    @pl.when(i == 0)
    def _prologue():
        for s in range(2):
            def issue(r, c, s=s):
                base = jnp.minimum(s, last) * tm
                _tile_copy(y_hbm, d1_ref[base + r], buf.at[s, 0], r, sem.at[s]).start()
                _tile_copy(y_hbm, d2_ref[base + r], buf.at[s, 1], r, sem.at[s]).start()
                return c
            lax.fori_loop(0, tm, issue, 0)

    def rows_ready(slot):
        return pltpu.make_async_copy(buf.at[slot], buf.at[slot], sem.at[slot])

    slot = i % 3
    nxt = (i + 2) % 3
    base = jnp.minimum(i + 2, last) * tm
    rows_ready(slot).wait()
    for r in range(tm):
        _tile_copy(y_hbm, d1_ref[base + r], buf.at[nxt, 0], r, sem.at[nxt]).start(priority=0)
        _tile_copy(y_hbm, d2_ref[base + r], buf.at[nxt, 1], r, sem.at[nxt]).start(priority=1)
    rt = rt_ref[...]
    moe = rt[:, 4:5] * _load_row_tiles(buf.at[slot, 0], tm) + rt[:, 5:6] * _load_row_tiles(buf.at[slot, 1], tm)
    y = ALPHA * x1_ref[...] + (1.0 + gt_ref[0, 0]) * moe
    o_ref[...] = _layer_norm(y, lng_ref[...], lnb_ref[...])

    @pl.when(i == last)
    def _drain():
        rows_ready((i + 1) % 3).wait()
        rows_ready((i + 2) % 3).wait()


def _combine(dest1, dest2, y_rows, x1, rt, mod4, ln_g, ln_b, t):
    n = x1.shape[0]
    tm = min(COMBINE_ROWS, t)
    nt = t // tm
    full = lambda shape: pl.BlockSpec(shape, lambda i, d1, d2: (0,) * len(shape))
    rows = lambda w: pl.BlockSpec((tm, w), lambda i, d1, d2: (i, 0))
    grid_spec = pltpu.PrefetchScalarGridSpec(
        num_scalar_prefetch=2,
        grid=(n // tm,),
        in_specs=[pl.BlockSpec(memory_space=pl.ANY), rows(D), rows(128),
                  pl.BlockSpec((1, 1, 1, D), lambda i, d1, d2: (5, i // nt, 0, 0)),
                  full((1, D)), full((1, D))],
        out_specs=rows(D),
        scratch_shapes=[pltpu.VMEM((3, 2, tm * ROW_TILE, 128), f32), pltpu.SemaphoreType.DMA((3,))],
    )
    return pl.pallas_call(
        functools.partial(_combine_kernel, tm=tm),
        grid_spec=grid_spec,
        out_shape=jax.ShapeDtypeStruct((n, D), f32),
        compiler_params=pltpu.CompilerParams(dimension_semantics=("arbitrary",), vmem_limit_bytes=VMEM_LIMIT),
        name="combine",
    )(dest1, dest2, y_rows, x1, rt, mod4, ln_g, ln_b)


def _hi_lo(w):
    hi = w.astype(bf16)
    return hi, (w - hi.astype(f32)).astype(bf16)


def _pad_lanes(v, offset, width):
    return jnp.zeros((width,), f32).at[offset:offset + v.shape[0]].set(v)


def _layer(x, c, ada_w, ada_b, w_in, conv_w, conv_b, dt_bias, a_log, d_skip, ssm_norm_g, fg_bias, att_norm_g,
           w_out, ln1_g, ln1_b, router_g_w, router_g_b, router_e_w, router_e_b, w_gate, w_up, w_down, ln2_g, ln2_b):
    bsz, t, _ = x.shape
    n = bsz * t

    mod4 = _ada(c, ada_w, ada_b).reshape(6, bsz, 1, D)

    w_main, w_small = _pack_w_in(w_in)
    zx, bc, qkv, dtf = _inproj(x, mod4, w_main, w_small, conv_w, conv_b.reshape(1, -1))

    pc = jnp.stack([_pad_lanes(dt_bias, 0, 128), _pad_lanes(a_log, 0, 128), _pad_lanes(fg_bias, 8, 128)]
                   + [jnp.zeros((128,), f32)] * 5)
    rep = lambda v: jnp.repeat(v, HEAD_DIM)
    pe = jnp.stack([rep(dt_bias), rep(a_log), rep(d_skip), ssm_norm_g] + [jnp.zeros((D_SSM,), f32)] * 4)
    y_ssm, cumc = _ssd(zx, bc, dtf, pc, pe)

    y_att = _attn(qkv, cumc, att_norm_g.reshape(1, -1))

    wr = jnp.concatenate([router_g_w, router_e_w, jnp.zeros((D, 128 - N_GROUPS_R - N_EXPERTS), f32)], axis=1)
    wr_hi, wr_lo = _hi_lo(wr)
    rb = jnp.concatenate([router_g_b, router_e_b, jnp.zeros((128 - N_GROUPS_R - N_EXPERTS,), f32)]).reshape(1, 128)
    blk = MOE_ROWS
    nblk = (2 * n) // blk + N_EXPERTS
    x1, rt, dest, cnt, tbl, x_rows = _outproj(y_ssm.reshape(n, D_SSM), y_att.reshape(n, D_ATT), x.reshape(n, D), mod4,
                                              w_out.astype(bf16), ln1_g.reshape(1, D), ln1_b.reshape(1, D),
                                              wr_hi, wr_lo, rb, t, blk, nblk)

    counts = cnt[0, N_GROUPS_R:N_GROUPS_R + N_EXPERTS].astype(i32)
    nb = (counts + blk - 1) // blk
    nb_end = jnp.cumsum(nb)
    n_used = nb_end[N_EXPERTS - 1:]
    every = jnp.arange(nblk, dtype=i32)
    step = jnp.minimum(every, n_used[0] - 1)
    owner = (nb_end[None, :] <= step[:, None]).astype(i32)
    block_e = jnp.sum(owner, axis=1)
    first = jnp.sum(owner * nb[None, :], axis=1)
    lanes = jnp.arange(128, dtype=i32)[None, :]
    nth_row = jnp.dot((lanes == (step - first)[:, None]).astype(f32), tbl, precision=lax.Precision.HIGHEST)
    block_id = jnp.sum(jnp.where(lanes == (block_e + N_GROUPS_R)[:, None], nth_row, 0.0), axis=1).astype(i32)
    block_row = jnp.where(every < n_used[0], block_id, every)
    run_end = first + jnp.sum((jnp.arange(N_EXPERTS)[None, :] == block_e[:, None]) * nb[None, :], axis=1)
    next_e = jnp.where(run_end < n_used[0], jnp.sum((nb_end[None, :] <= run_end[:, None]).astype(i32), axis=1), -1)
    parity = jnp.sum(((jnp.arange(N_EXPERTS)[None, :] < block_e[:, None]) & (nb[None, :] > 0)).astype(i32), axis=1) % 2

    y_rows = _moe(block_e, block_row, n_used, next_e, parity, x_rows.reshape(nblk * blk * ROW_TILE, 128),
                  w_gate, w_up, w_down, blk)
    out = _combine(dest[0], dest[1], y_rows.reshape(nblk * blk, ROW_TILE, 128), x1, rt, mod4,
                   ln2_g.reshape(1, D), ln2_b.reshape(1, D), t)
    return out.reshape(bsz, t, D)


def kernel(x, c, ada_w, ada_b, w_in, conv_w, conv_b, dt_bias, a_log, d_skip, ssm_norm_g, fg_bias, att_norm_g, w_out,
           ln1_g, ln1_b, router_g_w, router_g_b, router_e_w, router_e_b, w_gate, w_up, w_down, ln2_g, ln2_b):
    depth = ada_w.shape[0]
    for l in range(depth):
        x = _layer(x, c, ada_w[l], ada_b[l], w_in[l], conv_w[l], conv_b[l], dt_bias[l], a_log[l], d_skip[l],
                   ssm_norm_g[l], fg_bias[l], att_norm_g[l], w_out[l], ln1_g[l], ln1_b[l], router_g_w[l],
                   router_g_b[l], router_e_w[l], router_e_b[l], w_gate[l], w_up[l], w_down[l], ln2_g[l], ln2_b[l])
    return x
```

```python
import functools

import jax
import jax.numpy as jnp
import numpy as np
from jax import lax
from jax.experimental import pallas as pl
from jax.experimental.pallas import tpu as pltpu

f32 = jnp.float32
bf16 = jnp.bfloat16
i32 = jnp.int32

D = 1024
D_SSM = 512
D_ATT = 512
HEAD_DIM = 64
GROUP_W = 256
N_STATE = 128
CONV_K = 4
N_GROUPS_R = 4
EXPERTS_PER_GROUP = 8
N_EXPERTS = 32
D_EXPERT = 512
ALPHA = 2.0 ** 0.25
EPS = 1e-5
NEG = -1e30
LOG2E = 1.4426950408889634
QK_SCALE = HEAD_DIM ** -0.5 * LOG2E
V_ROWS = 80

SSD_CHUNK = 256
ATT_BLOCK = 256
INPROJ_ROWS = 512
OUTPROJ_ROWS = 512
OUTPROJ_PARTS = 2
MOE_ROWS = 256
COMBINE_ROWS = 256
DEST_ROWS = 2048
VMEM_LIMIT = 48 * 1024 * 1024


def _dot(a, b):
    return jnp.dot(a, b, preferred_element_type=f32)


def _dot_nt(a, b):
    return lax.dot_general(a, b, (((1,), (1,)), ((), ())), preferred_element_type=f32)


def _dot_tn(a, b):
    return lax.dot_general(a, b, (((0,), (0,)), ((), ())), preferred_element_type=f32)


def _split3(v):
    hi = v.astype(bf16)
    r1 = v - hi.astype(f32)
    mid = r1.astype(bf16)
    lo = (r1 - mid.astype(f32)).astype(bf16)
    return hi, mid, lo


def _dot_exact_lhs(m, v):
    hi, mid, lo = _split3(v)
    return (_dot(m, hi) + _dot(m, mid)) + _dot(m, lo)


def _dot_exact_rhs(v, m):
    hi, mid, lo = _split3(v)
    return (_dot(hi, m) + _dot(mid, m)) + _dot(lo, m)


ROW_TILE = 8


def _store_row_tiles(ref, val):
    rows = val.shape[0]
    for c in range(ROW_TILE):
        ref[pl.ds(c, rows, stride=ROW_TILE), :] = val[:, c * 128:(c + 1) * 128]


def _load_row_tiles(ref, rows):
    return jnp.concatenate([ref[pl.ds(c, rows, stride=ROW_TILE), :] for c in range(ROW_TILE)], axis=1)


def _softplus(x):
    return jnp.maximum(x, 0.0) + jnp.log1p(jnp.exp(-jnp.abs(x)))


def _silu(x):
    return x * jax.nn.sigmoid(x)


def _ada_kernel(c_ref, w_ref, b_ref, o_ref):
    s = _silu(c_ref[...]).astype(bf16)
    o_ref[0] = _dot(s, w_ref[...].astype(bf16)) + b_ref[0]


def _ada(c, w, b):
    bsz = c.shape[0]
    return pl.pallas_call(
        _ada_kernel,
        grid=(6,),
        in_specs=[pl.BlockSpec((bsz, D), lambda j: (0, 0)),
                  pl.BlockSpec((D, D), lambda j: (0, j)),
                  pl.BlockSpec((1, 1, D), lambda j: (j, 0, 0))],
        out_specs=pl.BlockSpec((1, bsz, D), lambda j: (j, 0, 0)),
        out_shape=jax.ShapeDtypeStruct((6, bsz, D), f32),
        compiler_params=pltpu.CompilerParams(dimension_semantics=("arbitrary",), vmem_limit_bytes=VMEM_LIMIT),
        name="ada",
    )(c, w, b.reshape(6, 1, D))


W_IN_COLS = 3088
PACK_ROWS = 256


def _pack_w_in_kernel(wt_ref, wm_ref, ws_ref):
    for j in range(3072 // PACK_ROWS):
        src = j * PACK_ROWS if j * PACK_ROWS < 1536 else j * PACK_ROWS + 8
        wm_ref[:, j * PACK_ROWS:(j + 1) * PACK_ROWS] = wt_ref[src:src + PACK_ROWS, :].T.astype(bf16)
    small = jnp.concatenate([wt_ref[1536:1544, :], wt_ref[W_IN_COLS - 8:W_IN_COLS, :], jnp.zeros((112, D), f32)],
                            axis=0).T
    hi = small.astype(bf16)
    ws_ref[:, 0:128] = hi
    ws_ref[:, 128:256] = (small - hi.astype(f32)).astype(bf16)


def _pack_w_in(w_in):
    full = lambda shape: pl.BlockSpec(shape, lambda i: (0,) * len(shape))
    return pl.pallas_call(
        _pack_w_in_kernel,
        grid=(1,),
        in_specs=[full((W_IN_COLS, D))],
        out_specs=[full((D, 3072)), full((D, 256))],
        out_shape=[jax.ShapeDtypeStruct((D, 3072), bf16), jax.ShapeDtypeStruct((D, 256), bf16)],
        compiler_params=pltpu.CompilerParams(dimension_semantics=("arbitrary",), vmem_limit_bytes=VMEM_LIMIT),
        name="pack_w_in",
    )(w_in.T)


def _inproj_kernel(x_ref, sc_ref, sh_ref, wm_ref, ws_ref, cw_ref, cb_ref, zx_ref, bc_ref, qkv_ref, dtf_ref, xcat, *, tm):
    i = pl.program_id(1)

    @pl.when(i == 0)
    def _init():
        xcat[0:8, :] = jnp.zeros((8, 2 * D_SSM), f32)

    u = x_ref[0] * (1.0 + sc_ref[0, 0]) + sh_ref[0, 0]
    ub = u.astype(bf16)
    xcat[8:8 + tm, 0:512] = _dot(ub, wm_ref[:, 512:1024])
    xcat[8:8 + tm, 512:1024] = _dot(ub, wm_ref[:, 1024:1536])
    zx_ref[0, :, 0:512] = _dot(ub, wm_ref[:, 0:512])
    acc = cw_ref[0:1, :] * xcat[5:5 + tm, :] + cb_ref[...]
    for k in range(1, CONV_K):
        acc = acc + cw_ref[k:k + 1, :] * xcat[5 + k:5 + k + tm, :]
    xcat[0:8, :] = xcat[tm:tm + 8, :]
    xbc = _silu(acc)
    zx_ref[0, :, 512:1024] = xbc[:, 0:512]
    bc_ref[0] = xbc[:, 512:1024].astype(bf16)
    for j, scale in enumerate((QK_SCALE, 1.0, 1.0)):
        qkv_ref[0, :, j * 512:(j + 1) * 512] = (
            _dot(ub, wm_ref[:, 1536 + j * 512:1536 + (j + 1) * 512]) * scale).astype(bf16)
    ul = (u - ub.astype(f32)).astype(bf16)
    d_hl = _dot(ub, ws_ref[...])
    dtf_ref[0] = (d_hl[:, 0:128] + _dot(ul, ws_ref[:, 0:128])) + d_hl[:, 128:256]


def _inproj(x, mod4, w_main, ws, conv_w, conv_b):
    bsz, t, _ = x.shape
    tm = min(INPROJ_ROWS, t)
    vec = lambda k: pl.BlockSpec((1, 1, 1, D), lambda b, i, k=k: (k, b, 0, 0))
    full = lambda shape: pl.BlockSpec(shape, lambda b, i: (0,) * len(shape))
    rows = lambda w: pl.BlockSpec((1, tm, w), lambda b, i: (b, i, 0))
    return pl.pallas_call(
        functools.partial(_inproj_kernel, tm=tm),
        grid=(bsz, t // tm),
        in_specs=[rows(D), vec(1), vec(0), full((D, 3072)), full((D, 256)),
                  full((CONV_K, 2 * D_SSM)), full((1, 2 * D_SSM))],
        out_specs=[rows(1024), rows(512), rows(1536), rows(128)],
        out_shape=[jax.ShapeDtypeStruct((bsz, t, 1024), f32),
                   jax.ShapeDtypeStruct((bsz, t, 512), bf16),
                   jax.ShapeDtypeStruct((bsz, t, 1536), bf16),
                   jax.ShapeDtypeStruct((bsz, t, 128), f32)],
        scratch_shapes=[pltpu.VMEM((tm + 8, 2 * D_SSM), f32)],
        compiler_params=pltpu.CompilerParams(dimension_semantics=("parallel", "arbitrary"),
                                             vmem_limit_bytes=VMEM_LIMIT),
        name="inproj",
    )(x, mod4, mod4, w_main, ws, conv_w, conv_b)


def _ssd_kernel(z_ref, xs_ref, bc_ref, dtf_ref, pc_ref, pe_ref, y_ref, cumc_ref, state, carry, *, lc):
    j = pl.program_id(1)

    @pl.when(j == 0)
    def _init():
        state[...] = jnp.zeros_like(state)
        carry[...] = jnp.zeros_like(carry)

    xs = xs_ref[0]
    bm = bc_ref[0, :, 0:256]
    cm = bc_ref[0, :, 256:512]

    dtf = dtf_ref[0]
    lane = lax.broadcasted_iota(i32, (lc, 128), 1)
    dt_c = _softplus(dtf + pc_ref[0:1, :])
    a_c = dt_c * (-jnp.exp(pc_ref[1:2, :]))
    logf = -_softplus(-(dtf + pc_ref[2:3, :]))
    v = jnp.where(lane < 8, a_c, logf)
    r_i = lax.broadcasted_iota(i32, (lc, lc), 0)
    c_i = lax.broadcasted_iota(i32, (lc, lc), 1)
    tri = r_i >= c_i
    tri_b = jnp.where(tri, 1.0, 0.0).astype(bf16)
    cum = _dot_exact_lhs(tri_b, v) + carry[...]
    carry[...] = jnp.where(lane[0:1, :] >= 8, cum[lc - 1:lc, :], 0.0)
    cumc_ref[0] = cum
    cs_t = cum.T[0:8, :]

    e_r = lax.broadcasted_iota(i32, (128, D_SSM), 0)
    e_c = lax.broadcasted_iota(i32, (128, D_SSM), 1)
    expand = jnp.where(jnp.right_shift(e_c, 6) == e_r, 1.0, 0.0).astype(bf16)
    dt_e = _dot_exact_rhs(dt_c, expand)
    cs_e = _dot_exact_rhs(cum, expand)

    xdt = xs * dt_e
    ecs = jnp.exp(cs_e)
    cs_last = cs_e[lc - 1:lc, :]
    dec_st = jnp.exp(cs_last - cs_e)
    lane_g = lax.broadcasted_iota(i32, (1, GROUP_W), 1)
    ys = []
    for g in range(2):
        gs = slice(g * GROUP_W, (g + 1) * GROUP_W)
        bg = bm[:, g * N_STATE:(g + 1) * N_STATE]
        cg = cm[:, g * N_STATE:(g + 1) * N_STATE]
        cb = _dot_nt(cg, bg)
        xdt_g = xdt[:, gs]
        xdt_gb = xdt_g.astype(bf16)
        ms, xb = [], []
        for hh in range(4):
            h = g * 4 + hh
            lm = jnp.exp(jnp.where(tri, cum[:, h:h + 1] - cs_t[h:h + 1, :], -jnp.inf))
            ms.append((cb * lm).astype(bf16))
            xb.append(jnp.where(jnp.right_shift(lane_g, 6) == hh, xdt_gb, jnp.zeros_like(xdt_gb)))
        y_diag = _dot(jnp.concatenate(ms, axis=1), jnp.concatenate(xb, axis=0))
        st = state[g]
        y_off = _dot(cg, st.astype(bf16)) * ecs[:, gs]
        upd = _dot_tn(bg, (xdt_g * dec_st[:, gs]).astype(bf16))
        state[g] = st * jnp.exp(cs_last[:, gs]) + upd
        ys.append(y_diag + y_off + xs[:, gs] * pe_ref[2:3, gs])

    outs = []
    for g in range(2):
        gs = slice(g * GROUP_W, (g + 1) * GROUP_W)
        yg = ys[g] * _silu(z_ref[0, :, gs])
        ms_ = jnp.mean(yg * yg, axis=-1, keepdims=True)
        outs.append(yg * lax.rsqrt(ms_ + EPS))
    y_ref[0] = (jnp.concatenate(outs, axis=1) * pe_ref[3:4, :]).astype(bf16)


def _ssd(zx, bc, dtf, pc, pe):
    bsz, t, _ = zx.shape
    lc = min(SSD_CHUNK, t)
    col = lambda k: pl.BlockSpec((1, lc, 512), lambda b, j, k=k: (b, j, k))
    full = lambda shape: pl.BlockSpec(shape, lambda b, j: (0,) * len(shape))
    return pl.pallas_call(
        functools.partial(_ssd_kernel, lc=lc),
        grid=(bsz, t // lc),
        in_specs=[col(0), col(1), col(0),
                  pl.BlockSpec((1, lc, 128), lambda b, j: (b, j, 0)),
                  full((8, 128)), full((8, D_SSM))],
        out_specs=[pl.BlockSpec((1, lc, D_SSM), lambda b, j: (b, j, 0)),
                   pl.BlockSpec((1, lc, 128), lambda b, j: (b, j, 0))],
        out_shape=[jax.ShapeDtypeStruct((bsz, t, D_SSM), bf16),
                   jax.ShapeDtypeStruct((bsz, t, 128), f32)],
        scratch_shapes=[pltpu.VMEM((2, N_STATE, GROUP_W), f32),
                        pltpu.VMEM((1, 128), f32)],
        compiler_params=pltpu.CompilerParams(dimension_semantics=("parallel", "arbitrary"),
                                             vmem_limit_bytes=VMEM_LIMIT),
        name="ssd",
    )(zx, zx, bc, dtf, pc, pe)


def _attn_kernel(q_ref, k_ref, v_ref, cc_ref, psel_ref, ng_ref, o_ref, kaug, vt, acc, sc0, sc1, *, tq, t):
    i = pl.program_id(1)
    nkb = t // tq
    n_heads = D_ATT // HEAD_DIM
    lane = lax.broadcasted_iota(i32, (1, 128), 1)
    lo_half = lane < HEAD_DIM

    @pl.when(i == 0)
    def _build():
        eye = jnp.where(lax.broadcasted_iota(i32, (D_ATT, D_ATT), 0) == lax.broadcasted_iota(i32, (D_ATT, D_ATT), 1),
                        1.0, 0.0).astype(bf16)
        ones_rows = jnp.where(lax.broadcasted_iota(i32, (V_ROWS - HEAD_DIM, tq), 0) == 0, 1.0, 0.0).astype(bf16)
        for jb in range(nkb):
            rows = slice(jb * tq, (jb + 1) * tq)
            v_t = _dot_nt(eye, v_ref[0, rows, :]).astype(bf16)
            for h in range(n_heads):
                vt[jb, h * V_ROWS:h * V_ROWS + HEAD_DIM, :] = v_t[h * HEAD_DIM:(h + 1) * HEAD_DIM, :]
                vt[jb, h * V_ROWS + HEAD_DIM:(h + 1) * V_ROWS, :] = ones_rows
            pieces = jnp.concatenate(_split3(cc_ref[0, rows, :] * (-LOG2E)), axis=1)
            for p in range(n_heads // 2):
                a = _dot(pieces, psel_ref[p]).astype(bf16)
                kp = k_ref[0, rows, p * 128:(p + 1) * 128]
                kaug[2 * p, rows, :] = jnp.where(lo_half, kp, a)
                kaug[2 * p + 1, rows, :] = jnp.where(lo_half, a, kp)

    ones_hi = jnp.where((lane >= HEAD_DIM) & (lane < HEAD_DIM + 3), 1.0, 0.0).astype(bf16)
    ones_lo = jnp.where(lane < 3, 1.0, 0.0).astype(bf16)
    qa = []
    for p in range(n_heads // 2):
        qp = q_ref[0, :, p * 128:(p + 1) * 128]
        qa.append(jnp.where(lo_half, qp, ones_hi))
        qa.append(jnp.where(lo_half, ones_lo, qp))
    keep = lax.broadcasted_iota(i32, (tq, tq), 0) <= lax.broadcasted_iota(i32, (tq, tq), 1)
    acc[...] = jnp.zeros_like(acc)

    def score(jb, buf):
        k0 = pl.multiple_of(jb * tq, tq)
        for h in range(n_heads):
            buf[h] = _dot_nt(kaug[h, pl.ds(k0, tq), :], qa[h])

    def absorb(jb, ms, masked, buf):
        new_ms = []
        for h in range(n_heads):
            s = buf[h]
            if masked:
                s = jnp.where(keep, s, NEG)
            m_new = jnp.maximum(ms[h], jnp.max(s, axis=0, keepdims=True))
            alpha = jnp.exp2(ms[h] - m_new)
            p = jnp.exp2(s - m_new).astype(bf16)
            new_ms.append(m_new)
            acc[h] = acc[h] * alpha + _dot(vt[jb, h * V_ROWS:(h + 1) * V_ROWS, :], p)
        return tuple(new_ms)

    def pair(pp, ms):
        j0 = 2 * pp
        score(j0 + 1, sc1)
        ms = absorb(j0, ms, False, sc0)
        score(j0 + 2, sc0)
        return absorb(j0 + 1, ms, False, sc1)

    score(0, sc0)
    ms = lax.fori_loop(0, i // 2, pair, tuple(jnp.full((1, tq), NEG, f32) for _ in range(n_heads)))

    @pl.when(i % 2 == 0)
    def _even():
        absorb(i, ms, True, sc0)

    @pl.when(i % 2 == 1)
    def _odd():
        score(i, sc1)
        absorb(i, absorb(i - 1, ms, False, sc0), True, sc1)

    out_t = jnp.concatenate([acc[h, 0:HEAD_DIM, :] * (1.0 / acc[h, HEAD_DIM:HEAD_DIM + 1, :]) for h in range(n_heads)],
                            axis=0)
    ms_ = jnp.mean(out_t * out_t, axis=0, keepdims=True)
    out_t = out_t * lax.rsqrt(ms_ + EPS)
    o_ref[0] = (out_t.T * ng_ref[...]).astype(bf16)


def _piece_select():
    sel = np.zeros((4, 384, 128), np.float32)
    for pair in range(4):
        for j in range(3):
            sel[pair, j * 128 + 8 + 2 * pair, HEAD_DIM + j] = 1.0
            sel[pair, j * 128 + 8 + 2 * pair + 1, j] = 1.0
    return jnp.asarray(sel, bf16)


def _attn(qkv, cumc, norm_g):
    bsz, t, _ = qkv.shape
    tq = min(ATT_BLOCK, t)
    return pl.pallas_call(
        functools.partial(_attn_kernel, tq=tq, t=t),
        grid=(bsz, t // tq),
        in_specs=[pl.BlockSpec((1, tq, D_ATT), lambda b, i: (b, i, 0)),
                  pl.BlockSpec((1, t, D_ATT), lambda b, i: (b, 0, 1)),
                  pl.BlockSpec((1, t, D_ATT), lambda b, i: (b, 0, 2)),
                  pl.BlockSpec((1, t, 128), lambda b, i: (b, 0, 0)),
                  pl.BlockSpec((4, 384, 128), lambda b, i: (0, 0, 0)),
                  pl.BlockSpec((1, D_ATT), lambda b, i: (0, 0))],
        out_specs=pl.BlockSpec((1, tq, D_ATT), lambda b, i: (b, i, 0)),
        out_shape=jax.ShapeDtypeStruct((bsz, t, D_ATT), bf16),
        scratch_shapes=[pltpu.VMEM((D_ATT // HEAD_DIM, t, 128), bf16),
                        pltpu.VMEM((t // tq, (D_ATT // HEAD_DIM) * V_ROWS, tq), bf16),
                        pltpu.VMEM((D_ATT // HEAD_DIM, V_ROWS, tq), f32),
                        pltpu.VMEM((D_ATT // HEAD_DIM, tq, tq), f32),
                        pltpu.VMEM((D_ATT // HEAD_DIM, tq, tq), f32)],
        compiler_params=pltpu.CompilerParams(dimension_semantics=("parallel", "arbitrary"),
                                             vmem_limit_bytes=VMEM_LIMIT),
        name="attn",
    )(qkv, qkv, qkv, cumc, _piece_select(), norm_g)


def _layer_norm(y, g, b):
    mu = jnp.mean(y, axis=-1, keepdims=True)
    yc = y - mu
    var = jnp.mean(yc * yc, axis=-1, keepdims=True)
    return yc * lax.rsqrt(var + EPS) * g + b


def _outproj_kernel(ys_ref, ya_ref, x_ref, gt_ref, sc_ref, sh_ref, wo_ref, lng_ref, lnb_ref, wrh_ref, wrl_ref, rb_ref,
                    x1_ref, rt_ref, dest_ref, cnt_ref, tbl_ref, xrows_hbm,
                    carry, galloc, tbl, u2t, didx_v, didx_s, cnt_v, cnt_s, zeros, idx_sem, sc_sem, z_sem,
                    *, tm, blk, n_blk):
    i = pl.program_id(0)
    last = pl.num_programs(0) - 1
    cur = i % 2
    prev = 1 - cur

    def idx_ready(slot):
        return pltpu.make_async_copy(didx_v.at[slot], didx_s.at[slot], idx_sem.at[slot])

    def dispatched(slot):
        return pltpu.make_async_copy(u2t.at[slot], u2t.at[slot], sc_sem.at[slot])

    def dispatch_copy(slot, r, k):
        return pltpu.make_async_copy(u2t.at[slot, pl.ds(r * ROW_TILE, ROW_TILE)], xrows_hbm.at[didx_s[slot, k, r]],
                                     sc_sem.at[slot])

    @pl.when(i == 0)
    def _init():
        carry[...] = jnp.zeros_like(carry)
        galloc[...] = jnp.zeros_like(galloc)
        tbl[...] = jnp.zeros_like(tbl)

    @pl.when(i >= 2)
    def _reuse():
        dispatched(cur).wait()
        dispatched(cur).wait()

    def step(dispatch_prev):
        n_parts = OUTPROJ_PARTS
        n_slices = 3 * n_parts + 2

        def dispatch_slice(c):
            if dispatch_prev:
                for r_ in range(c * tm // n_slices, (c + 1) * tm // n_slices):
                    dispatch_copy(prev, r_, 0).start(priority=1)
                    dispatch_copy(prev, r_, 1).start(priority=1)

        if dispatch_prev:
            idx_ready(prev).wait()
        part = tm // n_parts
        hs = []
        for a in range(n_parts):
            rs = slice(a * part, (a + 1) * part)
            dispatch_slice(a)
            hs.append(_dot(ys_ref[rs, :], wo_ref[0:D_SSM, :]) + _dot(ya_ref[rs, :], wo_ref[D_SSM:D, :]))
        logit_parts = []
        for a in range(n_parts):
            rs = slice(a * part, (a + 1) * part)
            dispatch_slice(n_parts + 2 * a)
            x1 = _layer_norm(ALPHA * x_ref[rs, :] + (1.0 + gt_ref[0, 0]) * hs[a], lng_ref[...], lnb_ref[...])
            x1_ref[rs, :] = x1
            u2 = x1 * (1.0 + sc_ref[0, 0]) + sh_ref[0, 0]
            _store_row_tiles(u2t.at[cur, pl.ds(a * part * ROW_TILE, part * ROW_TILE)], u2)
            dispatch_slice(n_parts + 2 * a + 1)
            uh = u2.astype(bf16)
            ul = (u2 - uh.astype(f32)).astype(bf16)
            logit_parts.append((_dot(uh, wrh_ref[...]) + _dot(ul, wrh_ref[...])) + _dot(uh, wrl_ref[...]))
        logits = jnp.concatenate(logit_parts, axis=0) + rb_ref[...]
        lane = lax.broadcasted_iota(i32, (tm, 128), 1).astype(f32)
        big = jnp.float32(1e9)

        def first_max(vals):
            m = jnp.max(vals, axis=-1, keepdims=True)
            return m, jnp.min(jnp.where(vals == m, lane, big), axis=-1, keepdims=True)

        gl = jnp.where(lane < N_GROUPS_R, logits, NEG)
        gmax, gidx = first_max(gl)
        g_p = 1.0 / jnp.sum(jnp.exp(gl - gmax), axis=-1, keepdims=True)
        lo = N_GROUPS_R + EXPERTS_PER_GROUP * gidx
        el = jnp.where((lane >= lo) & (lane < lo + EXPERTS_PER_GROUP), logits, NEG)
        m1, i1 = first_max(el)
        el2 = jnp.where(lane == i1, NEG, el)
        m2, i2 = first_max(el2)
        r = jnp.exp(m2 - m1)
        w1 = g_p / (1.0 + r)
        w2 = g_p * r / (1.0 + r)
        dispatch_slice(3 * n_parts)

        oh1 = lane == i1
        oh2 = lane == i2
        oh = jnp.where(oh1 | oh2, 1.0, 0.0)
        r_i = lax.broadcasted_iota(i32, (tm, tm), 0)
        c_i = lax.broadcasted_iota(i32, (tm, tm), 1)
        lower = jnp.where(r_i > c_i, 1.0, 0.0).astype(bf16)
        c_old = carry[...]
        prefix = _dot(lower, oh.astype(bf16)) + c_old
        rank1 = jnp.sum(jnp.where(oh1, prefix, 0.0), axis=-1, keepdims=True)
        rank2 = jnp.sum(jnp.where(oh2, prefix, 0.0), axis=-1, keepdims=True)
        c_new = c_old + jnp.sum(oh, axis=0, keepdims=True)
        carry[...] = c_new
        dispatch_slice(3 * n_parts + 1)

        nb_old = jnp.floor((c_old + (blk - 1)) * (1.0 / blk))
        nb_new = jnp.floor((c_new + (blk - 1)) * (1.0 / blk))
        fresh = nb_new - nb_old
        sq_r = lax.broadcasted_iota(i32, (128, 128), 0)
        sq_c = lax.broadcasted_iota(i32, (128, 128), 1)
        before = jnp.where(sq_r < sq_c, 1.0, 0.0).astype(bf16)
        base = galloc[...] + _dot(jnp.broadcast_to(fresh, (8, 128)).astype(bf16), before)[0:1, :]
        galloc[...] = galloc[...] + jnp.sum(fresh, axis=-1, keepdims=True)
        ordinal = sq_r.astype(f32)
        tbl[...] = jnp.where((ordinal >= nb_old) & (ordinal < nb_new), base + (ordinal - nb_old), tbl[...])
        tbl_ref[...] = tbl[...]
        cnt_ref[...] = jnp.concatenate([c_new, galloc[...], jnp.zeros((6, 128), f32)], axis=0)

        rt = jnp.where(lane == 4, w1, jnp.where(lane == 5, w2, 0.0))
        rt_ref[...] = rt

        ids_bf = tbl[...].astype(bf16)

        def row_of(onehot, rank):
            nth = jnp.floor(rank * (1.0 / blk))
            ids = _dot(jnp.where(lane == nth, 1.0, 0.0).astype(bf16), ids_bf)
            return jnp.sum(jnp.where(onehot, ids, 0.0), axis=-1, keepdims=True) * blk + (rank - nth * blk)

        d1 = row_of(oh1, rank1)
        d2 = row_of(oh2, rank2)
        dd = jnp.where(lane == 0, d1, jnp.where(lane == 1, d2, 0.0)).T[0:8, :].astype(i32)
        dest_ref[...] = dd
        didx_v[cur] = dd
        idx_ready(cur).start()

    pl.when(i == 0)(lambda: step(False))
    pl.when(i > 0)(lambda: step(True))

    @pl.when(i == last)
    def _epilogue():
        idx_ready(cur).wait()

        def issue(r_, c):
            dispatch_copy(cur, r_, 0).start()
            dispatch_copy(cur, r_, 1).start()
            return c
        lax.fori_loop(0, tm, issue, 0)

        cnt = carry[...]
        n_blocks = jnp.floor((cnt + (blk - 1)) * (1.0 / blk))
        ordinal = lax.broadcasted_iota(i32, (128, 128), 0).astype(f32)
        last_id = jnp.sum(jnp.where(ordinal == n_blocks - 1.0, tbl[...], 0.0), axis=0, keepdims=True)
        used = cnt - (n_blocks - 1.0) * blk
        cnt_v[...] = jnp.concatenate([last_id * blk + used, blk - used, galloc[...], jnp.zeros((5, 128), f32)],
                                     axis=0).astype(i32)
        to_smem = pltpu.make_async_copy(cnt_v, cnt_s, z_sem)
        to_smem.start()
        to_smem.wait()
        zeros[...] = jnp.zeros_like(zeros)
        sizes = [1 << b for b in reversed(range(blk.bit_length() - 1))]

        def for_each_fill(fn):
            def tail(e, c):
                pad = cnt_s[1, N_GROUPS_R + e]
                off = cnt_s[0, N_GROUPS_R + e]
                for sz in sizes:
                    @pl.when((pad & sz) != 0)
                    def _(off=off, sz=sz):
                        fn(pltpu.make_async_copy(zeros.at[pl.ds(0, sz)], xrows_hbm.at[pl.ds(off, sz)], z_sem))
                    off = off + (pad & sz)
                return c
            lax.fori_loop(0, N_EXPERTS, tail, 0)

            def whole(b, c):
                fn(pltpu.make_async_copy(zeros, xrows_hbm.at[pl.ds(b * blk, blk)], z_sem))
                return c
            lax.fori_loop(cnt_s[2, 0], n_blk, whole, 0)

        for_each_fill(lambda copy: copy.start())
        for_each_fill(lambda copy: copy.wait())

        @pl.when(i > 0)
        def _prev_done():
            dispatched(prev).wait()
            dispatched(prev).wait()
        dispatched(cur).wait()
        dispatched(cur).wait()


def _outproj(y_ssm, y_att, x, mod4, w_out, ln_g, ln_b, wr_hi, wr_lo, rb, t, blk, n_blk):
    n = x.shape[0]
    tm = min(OUTPROJ_ROWS, t)
    nt = t // tm
    assert n // blk + 1 <= 128 and n_blk <= 256, "block-id table: 128 blocks per expert, ids exact in bf16"
    vec = lambda k: pl.BlockSpec((1, 1, 1, D), lambda i, k=k: (k, i // nt, 0, 0))
    full = lambda shape: pl.BlockSpec(shape, lambda i: (0,) * len(shape))
    rows = lambda w: pl.BlockSpec((tm, w), lambda i: (i, 0))
    return pl.pallas_call(
        functools.partial(_outproj_kernel, tm=tm, blk=blk, n_blk=n_blk),
        grid=(n // tm,),
        in_specs=[rows(D_SSM), rows(D_ATT), rows(D), vec(2), vec(4), vec(3),
                  full((D, D)), full((1, D)), full((1, D)), full((D, 128)), full((D, 128)), full((1, 128))],
        out_specs=[rows(D), rows(128), pl.BlockSpec((8, tm), lambda i: (0, i)), full((8, 128)), full((128, 128)),
                   pl.BlockSpec(memory_space=pl.ANY)],
        out_shape=[jax.ShapeDtypeStruct((n, D), f32), jax.ShapeDtypeStruct((n, 128), f32),
                   jax.ShapeDtypeStruct((8, n), i32), jax.ShapeDtypeStruct((8, 128), f32),
                   jax.ShapeDtypeStruct((128, 128), f32),
                   jax.ShapeDtypeStruct((n_blk * blk, ROW_TILE, 128), f32)],
        scratch_shapes=[pltpu.VMEM((1, 128), f32), pltpu.VMEM((1, 128), f32), pltpu.VMEM((128, 128), f32),
                        pltpu.VMEM((2, tm * ROW_TILE, 128), f32),
                        pltpu.VMEM((2, 8, tm), i32), pltpu.SMEM((2, 8, tm), i32),
                        pltpu.VMEM((8, 128), i32), pltpu.SMEM((8, 128), i32),
                        pltpu.VMEM((blk, ROW_TILE, 128), f32),
                        pltpu.SemaphoreType.DMA((2,)), pltpu.SemaphoreType.DMA((2,)), pltpu.SemaphoreType.DMA(())],
        compiler_params=pltpu.CompilerParams(dimension_semantics=("arbitrary",), vmem_limit_bytes=VMEM_LIMIT),
        name="outproj",
    )(y_ssm, y_att, x, mod4, mod4, mod4, w_out, ln_g, ln_b, wr_hi, wr_lo, rb)


def _tile_copy(src_hbm, row, buf, slot, sem):
    return pltpu.make_async_copy(src_hbm.at[row], buf.at[pl.ds(slot * ROW_TILE, ROW_TILE)], sem)


def _moe_kernel(be_ref, br_ref, nu_ref, nxt_ref, par_ref, x_ref, wg_hbm, wu_hbm, wd_hbm, y_ref,
                wfg, wfu, wfd, wgb, wub, wdb, wsem, *, blk):
    i = pl.program_id(0)
    used = i < nu_ref[0]

    def fetch(e, slot):
        return [pltpu.make_async_copy(src.at[e], dst.at[slot], wsem.at[slot])
                for src, dst in ((wg_hbm, wfg), (wu_hbm, wfu), (wd_hbm, wfd))]

    @pl.when(i == 0)
    def _first():
        for c in fetch(be_ref[0], par_ref[0]):
            c.start()

    @pl.when(jnp.logical_and(used, jnp.logical_or(i == 0, be_ref[i] != be_ref[jnp.maximum(i - 1, 0)])))
    def _switch():
        slot = par_ref[i]
        for c in fetch(be_ref[i], slot):
            c.wait()

        @pl.when(nxt_ref[i] >= 0)
        def _next():
            for c in fetch(nxt_ref[i], 1 - slot):
                c.start(priority=1)
        wgb[...] = wfg[slot].astype(bf16)
        wub[...] = wfu[slot].astype(bf16)
        wdb[...] = wfd[slot].astype(bf16)

    @pl.when(jnp.logical_not(used))
    def _spare():
        y_ref[...] = jnp.zeros_like(y_ref)

    @pl.when(used)
    def _work():
        xb = _load_row_tiles(x_ref, blk).astype(bf16)
        hid = (_silu(_dot(xb, wgb[...])) * _dot(xb, wub[...])).astype(bf16)
        _store_row_tiles(y_ref, _dot(hid, wdb[...]))


def _moe(block_e, block_row, n_used, next_e, parity, x_rows, w_gate, w_up, w_down, blk):
    nblk = block_e.shape[0]
    rspec = pl.BlockSpec((blk * ROW_TILE, 128), lambda i, be, br, *_: (br[i], 0))
    xspec = pl.BlockSpec((blk * ROW_TILE, 128), lambda i, be, br, nu, *_: (br[jnp.minimum(i, nu[0] - 1)], 0))
    anyspec = pl.BlockSpec(memory_space=pl.ANY)
    grid_spec = pltpu.PrefetchScalarGridSpec(
        num_scalar_prefetch=5,
        grid=(nblk,),
        in_specs=[xspec, anyspec, anyspec, anyspec],
        out_specs=rspec,
        scratch_shapes=[pltpu.VMEM((2, D, D_EXPERT), f32), pltpu.VMEM((2, D, D_EXPERT), f32),
                        pltpu.VMEM((2, D_EXPERT, D), f32),
                        pltpu.VMEM((D, D_EXPERT), bf16), pltpu.VMEM((D, D_EXPERT), bf16),
                        pltpu.VMEM((D_EXPERT, D), bf16),
                        pltpu.SemaphoreType.DMA((2,))],
    )
    return pl.pallas_call(
        functools.partial(_moe_kernel, blk=blk),
        grid_spec=grid_spec,
        out_shape=jax.ShapeDtypeStruct(x_rows.shape, f32),
        compiler_params=pltpu.CompilerParams(dimension_semantics=("arbitrary",), vmem_limit_bytes=VMEM_LIMIT),
        name="moe",
    )(block_e, block_row, n_used, next_e, parity, x_rows, w_gate, w_up, w_down)


def _combine_kernel(d1_ref, d2_ref, y_hbm, x1_ref, rt_ref, gt_ref, lng_ref, lnb_ref, o_ref, buf, sem, *, tm):
    i = pl.program_id(0)
    last = pl.num_programs(0) - 1

    @pl.when(i == 0)
    def _prologue():
        for s in range(2):
            def issue(r, c, s=s):
                base = jnp.minimum(s, last) * tm
                _tile_copy(y_hbm, d1_ref[base + r], buf.at[s, 0], r, sem.at[s]).start()
                _tile_copy(y_hbm, d2_ref[base + r], buf.at[s, 1], r, sem.at[s]).start()
                return c
            lax.fori_loop(0, tm, issue, 0)

    def rows_ready(slot):
        return pltpu.make_async_copy(buf.at[slot], buf.at[slot], sem.at[slot])

    slot = i % 3
    nxt = (i + 2) % 3
    base = jnp.minimum(i + 2, last) * tm
    rows_ready(slot).wait()
    for r in range(tm):
        _tile_copy(y_hbm, d1_ref[base + r], buf.at[nxt, 0], r, sem.at[nxt]).start(priority=0)
        _tile_copy(y_hbm, d2_ref[base + r], buf.at[nxt, 1], r, sem.at[nxt]).start(priority=1)
    rt = rt_ref[...]
    moe = rt[:, 4:5] * _load_row_tiles(buf.at[slot, 0], tm) + rt[:, 5:6] * _load_row_tiles(buf.at[slot, 1], tm)
    y = ALPHA * x1_ref[...] + (1.0 + gt_ref[0, 0]) * moe
    o_ref[...] = _layer_norm(y, lng_ref[...], lnb_ref[...])

    @pl.when(i == last)
    def _drain():
        rows_ready((i + 1) % 3).wait()
        rows_ready((i + 2) % 3).wait()


def _combine(dest1, dest2, y_rows, x1, rt, mod4, ln_g, ln_b, t):
    n = x1.shape[0]
    tm = min(COMBINE_ROWS, t)
    nt = t // tm
    full = lambda shape: pl.BlockSpec(shape, lambda i, d1, d2: (0,) * len(shape))
    rows = lambda w: pl.BlockSpec((tm, w), lambda i, d1, d2: (i, 0))
    grid_spec = pltpu.PrefetchScalarGridSpec(
        num_scalar_prefetch=2,
        grid=(n // tm,),
        in_specs=[pl.BlockSpec(memory_space=pl.ANY), rows(D), rows(128),
                  pl.BlockSpec((1, 1, 1, D), lambda i, d1, d2: (5, i // nt, 0, 0)),
                  full((1, D)), full((1, D))],
        out_specs=rows(D),
        scratch_shapes=[pltpu.VMEM((3, 2, tm * ROW_TILE, 128), f32), pltpu.SemaphoreType.DMA((3,))],
    )
    return pl.pallas_call(
        functools.partial(_combine_kernel, tm=tm),
        grid_spec=grid_spec,
        out_shape=jax.ShapeDtypeStruct((n, D), f32),
        compiler_params=pltpu.CompilerParams(dimension_semantics=("arbitrary",), vmem_limit_bytes=VMEM_LIMIT),
        name="combine",
    )(dest1, dest2, y_rows, x1, rt, mod4, ln_g, ln_b)


def _hi_lo(w):
    hi = w.astype(bf16)
    return hi, (w - hi.astype(f32)).astype(bf16)


def _pad_lanes(v, offset, width):
    return jnp.zeros((width,), f32).at[offset:offset + v.shape[0]].set(v)


def _layer(x, c, ada_w, ada_b, w_in, conv_w, conv_b, dt_bias, a_log, d_skip, ssm_norm_g, fg_bias, att_norm_g,
           w_out, ln1_g, ln1_b, router_g_w, router_g_b, router_e_w, router_e_b, w_gate, w_up, w_down, ln2_g, ln2_b):
    bsz, t, _ = x.shape
    n = bsz * t

    mod4 = _ada(c, ada_w, ada_b).reshape(6, bsz, 1, D)

    w_main, w_small = _pack_w_in(w_in)
    zx, bc, qkv, dtf = _inproj(x, mod4, w_main, w_small, conv_w, conv_b.reshape(1, -1))

    pc = jnp.stack([_pad_lanes(dt_bias, 0, 128), _pad_lanes(a_log, 0, 128), _pad_lanes(fg_bias, 8, 128)]
                   + [jnp.zeros((128,), f32)] * 5)
    rep = lambda v: jnp.repeat(v, HEAD_DIM)
    pe = jnp.stack([rep(dt_bias), rep(a_log), rep(d_skip), ssm_norm_g] + [jnp.zeros((D_SSM,), f32)] * 4)
    y_ssm, cumc = _ssd(zx, bc, dtf, pc, pe)

    y_att = _attn(qkv, cumc, att_norm_g.reshape(1, -1))

    wr = jnp.concatenate([router_g_w, router_e_w, jnp.zeros((D, 128 - N_GROUPS_R - N_EXPERTS), f32)], axis=1)
    wr_hi, wr_lo = _hi_lo(wr)
    rb = jnp.concatenate([router_g_b, router_e_b, jnp.zeros((128 - N_GROUPS_R - N_EXPERTS,), f32)]).reshape(1, 128)
    blk = MOE_ROWS
    nblk = (2 * n) // blk + N_EXPERTS
    x1, rt, dest, cnt, tbl, x_rows = _outproj(y_ssm.reshape(n, D_SSM), y_att.reshape(n, D_ATT), x.reshape(n, D), mod4,
                                              w_out.astype(bf16), ln1_g.reshape(1, D), ln1_b.reshape(1, D),
                                              wr_hi, wr_lo, rb, t, blk, nblk)

    counts = cnt[0, N_GROUPS_R:N_GROUPS_R + N_EXPERTS].astype(i32)
    nb = (counts + blk - 1) // blk
    nb_end = jnp.cumsum(nb)
    n_used = nb_end[N_EXPERTS - 1:]
    every = jnp.arange(nblk, dtype=i32)
    step = jnp.minimum(every, n_used[0] - 1)
    owner = (nb_end[None, :] <= step[:, None]).astype(i32)
    block_e = jnp.sum(owner, axis=1)
    first = jnp.sum(owner * nb[None, :], axis=1)
    lanes = jnp.arange(128, dtype=i32)[None, :]
    nth_row = jnp.dot((lanes == (step - first)[:, None]).astype(f32), tbl, precision=lax.Precision.HIGHEST)
    block_id = jnp.sum(jnp.where(lanes == (block_e + N_GROUPS_R)[:, None], nth_row, 0.0), axis=1).astype(i32)
    block_row = jnp.where(every < n_used[0], block_id, every)
    run_end = first + jnp.sum((jnp.arange(N_EXPERTS)[None, :] == block_e[:, None]) * nb[None, :], axis=1)
    next_e = jnp.where(run_end < n_used[0], jnp.sum((nb_end[None, :] <= run_end[:, None]).astype(i32), axis=1), -1)
    parity = jnp.sum(((jnp.arange(N_EXPERTS)[None, :] < block_e[:, None]) & (nb[None, :] > 0)).astype(i32), axis=1) % 2

    y_rows = _moe(block_e, block_row, n_used, next_e, parity, x_rows.reshape(nblk * blk * ROW_TILE, 128),
                  w_gate, w_up, w_down, blk)
    out = _combine(dest[0], dest[1], y_rows.reshape(nblk * blk, ROW_TILE, 128), x1, rt, mod4,
                   ln2_g.reshape(1, D), ln2_b.reshape(1, D), t)
    return out.reshape(bsz, t, D)


def kernel(x, c, ada_w, ada_b, w_in, conv_w, conv_b, dt_bias, a_log, d_skip, ssm_norm_g, fg_bias, att_norm_g, w_out,
           ln1_g, ln1_b, router_g_w, router_g_b, router_e_w, router_e_b, w_gate, w_up, w_down, ln2_g, ln2_b):
    depth = ada_w.shape[0]
    for l in range(depth):
        x = _layer(x, c, ada_w[l], ada_b[l], w_in[l], conv_w[l], conv_b[l], dt_bias[l], a_log[l], d_skip[l],
                   ssm_norm_g[l], fg_bias[l], att_norm_g[l], w_out[l], ln1_g[l], ln1_b[l], router_g_w[l],
                   router_g_b[l], router_e_w[l], router_e_b[l], w_gate[l], w_up[l], w_down[l], ln2_g[l], ln2_b[l])
    return x
```

```python
import functools

import jax
import jax.numpy as jnp
import numpy as np
from jax import lax
from jax.experimental import pallas as pl
from jax.experimental.pallas import tpu as pltpu

f32 = jnp.float32
bf16 = jnp.bfloat16
i32 = jnp.int32

D = 1024
D_SSM = 512
D_ATT = 512
HEAD_DIM = 64
GROUP_W = 256
N_STATE = 128
CONV_K = 4
N_GROUPS_R = 4
EXPERTS_PER_GROUP = 8
N_EXPERTS = 32
D_EXPERT = 512
ALPHA = 2.0 ** 0.25
EPS = 1e-5
NEG = -1e30
LOG2E = 1.4426950408889634
QK_SCALE = HEAD_DIM ** -0.5 * LOG2E
V_ROWS = 80

SSD_CHUNK = 256
ATT_BLOCK = 256
INPROJ_ROWS = 512
OUTPROJ_ROWS = 512
OUTPROJ_PARTS = 2
MOE_ROWS = 512
COMBINE_ROWS = 256
DEST_ROWS = 2048
VMEM_LIMIT = 48 * 1024 * 1024


def _dot(a, b):
    return jnp.dot(a, b, preferred_element_type=f32)


def _dot_nt(a, b):
    return lax.dot_general(a, b, (((1,), (1,)), ((), ())), preferred_element_type=f32)


def _dot_tn(a, b):
    return lax.dot_general(a, b, (((0,), (0,)), ((), ())), preferred_element_type=f32)


def _split3(v):
    hi = v.astype(bf16)
    r1 = v - hi.astype(f32)
    mid = r1.astype(bf16)
    lo = (r1 - mid.astype(f32)).astype(bf16)
    return hi, mid, lo


def _dot_exact_lhs(m, v):
    hi, mid, lo = _split3(v)
    return (_dot(m, hi) + _dot(m, mid)) + _dot(m, lo)


def _dot_exact_rhs(v, m):
    hi, mid, lo = _split3(v)
    return (_dot(hi, m) + _dot(mid, m)) + _dot(lo, m)


ROW_TILE = 8


def _store_row_tiles(ref, val):
    rows = val.shape[0]
    for c in range(ROW_TILE):
        ref[pl.ds(c, rows, stride=ROW_TILE), :] = val[:, c * 128:(c + 1) * 128]


def _load_row_tiles(ref, rows):
    return jnp.concatenate([ref[pl.ds(c, rows, stride=ROW_TILE), :] for c in range(ROW_TILE)], axis=1)


def _softplus(x):
    return jnp.maximum(x, 0.0) + jnp.log1p(jnp.exp(-jnp.abs(x)))


def _silu(x):
    return x * jax.nn.sigmoid(x)


def _ada_kernel(c_ref, w_ref, b_ref, o_ref):
    s = _silu(c_ref[...]).astype(bf16)
    o_ref[0] = _dot(s, w_ref[...].astype(bf16)) + b_ref[0]


def _ada(c, w, b):
    bsz = c.shape[0]
    return pl.pallas_call(
        _ada_kernel,
        grid=(6,),
        in_specs=[pl.BlockSpec((bsz, D), lambda j: (0, 0)),
                  pl.BlockSpec((D, D), lambda j: (0, j)),
                  pl.BlockSpec((1, 1, D), lambda j: (j, 0, 0))],
        out_specs=pl.BlockSpec((1, bsz, D), lambda j: (j, 0, 0)),
        out_shape=jax.ShapeDtypeStruct((6, bsz, D), f32),
        compiler_params=pltpu.CompilerParams(dimension_semantics=("arbitrary",), vmem_limit_bytes=VMEM_LIMIT),
        name="ada",
    )(c, w, b.reshape(6, 1, D))


W_IN_COLS = 3088
PACK_ROWS = 256


def _pack_w_in_kernel(wt_ref, wm_ref, ws_ref):
    for j in range(3072 // PACK_ROWS):
        src = j * PACK_ROWS if j * PACK_ROWS < 1536 else j * PACK_ROWS + 8
        wm_ref[:, j * PACK_ROWS:(j + 1) * PACK_ROWS] = wt_ref[src:src + PACK_ROWS, :].T.astype(bf16)
    small = jnp.concatenate([wt_ref[1536:1544, :], wt_ref[W_IN_COLS - 8:W_IN_COLS, :], jnp.zeros((112, D), f32)],
                            axis=0).T
    hi = small.astype(bf16)
    ws_ref[:, 0:128] = hi
    ws_ref[:, 128:256] = (small - hi.astype(f32)).astype(bf16)


def _pack_w_in(w_in):
    full = lambda shape: pl.BlockSpec(shape, lambda i: (0,) * len(shape))
    return pl.pallas_call(
        _pack_w_in_kernel,
        grid=(1,),
        in_specs=[full((W_IN_COLS, D))],
        out_specs=[full((D, 3072)), full((D, 256))],
        out_shape=[jax.ShapeDtypeStruct((D, 3072), bf16), jax.ShapeDtypeStruct((D, 256), bf16)],
        compiler_params=pltpu.CompilerParams(dimension_semantics=("arbitrary",), vmem_limit_bytes=VMEM_LIMIT),
        name="pack_w_in",
    )(w_in.T)


def _inproj_kernel(x_ref, sc_ref, sh_ref, wm_ref, ws_ref, cw_ref, cb_ref, zx_ref, bc_ref, qkv_ref, dtf_ref, xcat, *, tm):
    i = pl.program_id(1)

    @pl.when(i == 0)
    def _init():
        xcat[0:8, :] = jnp.zeros((8, 2 * D_SSM), f32)

    u = x_ref[0] * (1.0 + sc_ref[0, 0]) + sh_ref[0, 0]
    ub = u.astype(bf16)
    xcat[8:8 + tm, 0:512] = _dot(ub, wm_ref[:, 512:1024])
    xcat[8:8 + tm, 512:1024] = _dot(ub, wm_ref[:, 1024:1536])
    zx_ref[0, :, 0:512] = _dot(ub, wm_ref[:, 0:512])
    acc = cw_ref[0:1, :] * xcat[5:5 + tm, :] + cb_ref[...]
    for k in range(1, CONV_K):
        acc = acc + cw_ref[k:k + 1, :] * xcat[5 + k:5 + k + tm, :]
    xcat[0:8, :] = xcat[tm:tm + 8, :]
    xbc = _silu(acc)
    zx_ref[0, :, 512:1024] = xbc[:, 0:512]
    bc_ref[0] = xbc[:, 512:1024].astype(bf16)
    for j, scale in enumerate((QK_SCALE, 1.0, 1.0)):
        qkv_ref[0, :, j * 512:(j + 1) * 512] = (
            _dot(ub, wm_ref[:, 1536 + j * 512:1536 + (j + 1) * 512]) * scale).astype(bf16)
    ul = (u - ub.astype(f32)).astype(bf16)
    d_hl = _dot(ub, ws_ref[...])
    dtf_ref[0] = (d_hl[:, 0:128] + _dot(ul, ws_ref[:, 0:128])) + d_hl[:, 128:256]


def _inproj(x, mod4, w_main, ws, conv_w, conv_b):
    bsz, t, _ = x.shape
    tm = min(INPROJ_ROWS, t)
    vec = lambda k: pl.BlockSpec((1, 1, 1, D), lambda b, i, k=k: (k, b, 0, 0))
    full = lambda shape: pl.BlockSpec(shape, lambda b, i: (0,) * len(shape))
    rows = lambda w: pl.BlockSpec((1, tm, w), lambda b, i: (b, i, 0))
    return pl.pallas_call(
        functools.partial(_inproj_kernel, tm=tm),
        grid=(bsz, t // tm),
        in_specs=[rows(D), vec(1), vec(0), full((D, 3072)), full((D, 256)),
                  full((CONV_K, 2 * D_SSM)), full((1, 2 * D_SSM))],
        out_specs=[rows(1024), rows(512), rows(1536), rows(128)],
        out_shape=[jax.ShapeDtypeStruct((bsz, t, 1024), f32),
                   jax.ShapeDtypeStruct((bsz, t, 512), bf16),
                   jax.ShapeDtypeStruct((bsz, t, 1536), bf16),
                   jax.ShapeDtypeStruct((bsz, t, 128), f32)],
        scratch_shapes=[pltpu.VMEM((tm + 8, 2 * D_SSM), f32)],
        compiler_params=pltpu.CompilerParams(dimension_semantics=("parallel", "arbitrary"),
                                             vmem_limit_bytes=VMEM_LIMIT),
        name="inproj",
    )(x, mod4, mod4, w_main, ws, conv_w, conv_b)


def _ssd_kernel(z_ref, xs_ref, bc_ref, dtf_ref, pc_ref, pe_ref, y_ref, cumc_ref, state, carry, *, lc):
    j = pl.program_id(1)

    @pl.when(j == 0)
    def _init():
        state[...] = jnp.zeros_like(state)
        carry[...] = jnp.zeros_like(carry)

    xs = xs_ref[0]
    bm = bc_ref[0, :, 0:256]
    cm = bc_ref[0, :, 256:512]

    dtf = dtf_ref[0]
    lane = lax.broadcasted_iota(i32, (lc, 128), 1)
    dt_c = _softplus(dtf + pc_ref[0:1, :])
    a_c = dt_c * (-jnp.exp(pc_ref[1:2, :]))
    logf = -_softplus(-(dtf + pc_ref[2:3, :]))
    v = jnp.where(lane < 8, a_c, logf)
    r_i = lax.broadcasted_iota(i32, (lc, lc), 0)
    c_i = lax.broadcasted_iota(i32, (lc, lc), 1)
    tri = r_i >= c_i
    tri_b = jnp.where(tri, 1.0, 0.0).astype(bf16)
    cum = _dot_exact_lhs(tri_b, v) + carry[...]
    carry[...] = jnp.where(lane[0:1, :] >= 8, cum[lc - 1:lc, :], 0.0)
    cumc_ref[0] = cum
    cs_t = cum.T[0:8, :]

    e_r = lax.broadcasted_iota(i32, (128, D_SSM), 0)
    e_c = lax.broadcasted_iota(i32, (128, D_SSM), 1)
    expand = jnp.where(jnp.right_shift(e_c, 6) == e_r, 1.0, 0.0).astype(bf16)
    dt_e = _dot_exact_rhs(dt_c, expand)
    cs_e = _dot_exact_rhs(cum, expand)

    xdt = xs * dt_e
    ecs = jnp.exp(cs_e)
    cs_last = cs_e[lc - 1:lc, :]
    dec_st = jnp.exp(cs_last - cs_e)
    lane_g = lax.broadcasted_iota(i32, (1, GROUP_W), 1)
    ys = []
    for g in range(2):
        gs = slice(g * GROUP_W, (g + 1) * GROUP_W)
        bg = bm[:, g * N_STATE:(g + 1) * N_STATE]
        cg = cm[:, g * N_STATE:(g + 1) * N_STATE]
        cb = _dot_nt(cg, bg)
        xdt_g = xdt[:, gs]
        xdt_gb = xdt_g.astype(bf16)
        ms, xb = [], []
        for hh in range(4):
            h = g * 4 + hh
            lm = jnp.exp(jnp.where(tri, cum[:, h:h + 1] - cs_t[h:h + 1, :], -jnp.inf))
            ms.append((cb * lm).astype(bf16))
            xb.append(jnp.where(jnp.right_shift(lane_g, 6) == hh, xdt_gb, jnp.zeros_like(xdt_gb)))
        y_diag = _dot(jnp.concatenate(ms, axis=1), jnp.concatenate(xb, axis=0))
        st = state[g]
        y_off = _dot(cg, st.astype(bf16)) * ecs[:, gs]
        upd = _dot_tn(bg, (xdt_g * dec_st[:, gs]).astype(bf16))
        state[g] = st * jnp.exp(cs_last[:, gs]) + upd
        ys.append(y_diag + y_off + xs[:, gs] * pe_ref[2:3, gs])

    outs = []
    for g in range(2):
        gs = slice(g * GROUP_W, (g + 1) * GROUP_W)
        yg = ys[g] * _silu(z_ref[0, :, gs])
        ms_ = jnp.mean(yg * yg, axis=-1, keepdims=True)
        outs.append(yg * lax.rsqrt(ms_ + EPS))
    y_ref[0] = (jnp.concatenate(outs, axis=1) * pe_ref[3:4, :]).astype(bf16)


def _ssd(zx, bc, dtf, pc, pe):
    bsz, t, _ = zx.shape
    lc = min(SSD_CHUNK, t)
    col = lambda k: pl.BlockSpec((1, lc, 512), lambda b, j, k=k: (b, j, k))
    full = lambda shape: pl.BlockSpec(shape, lambda b, j: (0,) * len(shape))
    return pl.pallas_call(
        functools.partial(_ssd_kernel, lc=lc),
        grid=(bsz, t // lc),
        in_specs=[col(0), col(1), col(0),
                  pl.BlockSpec((1, lc, 128), lambda b, j: (b, j, 0)),
                  full((8, 128)), full((8, D_SSM))],
        out_specs=[pl.BlockSpec((1, lc, D_SSM), lambda b, j: (b, j, 0)),
                   pl.BlockSpec((1, lc, 128), lambda b, j: (b, j, 0))],
        out_shape=[jax.ShapeDtypeStruct((bsz, t, D_SSM), bf16),
                   jax.ShapeDtypeStruct((bsz, t, 128), f32)],
        scratch_shapes=[pltpu.VMEM((2, N_STATE, GROUP_W), f32),
                        pltpu.VMEM((1, 128), f32)],
        compiler_params=pltpu.CompilerParams(dimension_semantics=("parallel", "arbitrary"),
                                             vmem_limit_bytes=VMEM_LIMIT),
        name="ssd",
    )(zx, zx, bc, dtf, pc, pe)


def _attn_kernel(q_ref, k_ref, v_ref, cc_ref, psel_ref, ng_ref, o_ref, kaug, vt, acc, sc0, sc1, *, tq, t):
    i = pl.program_id(1)
    nkb = t // tq
    n_heads = D_ATT // HEAD_DIM
    lane = lax.broadcasted_iota(i32, (1, 128), 1)
    lo_half = lane < HEAD_DIM

    @pl.when(i == 0)
    def _build():
        eye = jnp.where(lax.broadcasted_iota(i32, (D_ATT, D_ATT), 0) == lax.broadcasted_iota(i32, (D_ATT, D_ATT), 1),
                        1.0, 0.0).astype(bf16)
        ones_rows = jnp.where(lax.broadcasted_iota(i32, (V_ROWS - HEAD_DIM, tq), 0) == 0, 1.0, 0.0).astype(bf16)
        for jb in range(nkb):
            rows = slice(jb * tq, (jb + 1) * tq)
            v_t = _dot_nt(eye, v_ref[0, rows, :]).astype(bf16)
            for h in range(n_heads):
                vt[jb, h * V_ROWS:h * V_ROWS + HEAD_DIM, :] = v_t[h * HEAD_DIM:(h + 1) * HEAD_DIM, :]
                vt[jb, h * V_ROWS + HEAD_DIM:(h + 1) * V_ROWS, :] = ones_rows
            pieces = jnp.concatenate(_split3(cc_ref[0, rows, :] * (-LOG2E)), axis=1)
            for p in range(n_heads // 2):
                a = _dot(pieces, psel_ref[p]).astype(bf16)
                kp = k_ref[0, rows, p * 128:(p + 1) * 128]
                kaug[2 * p, rows, :] = jnp.where(lo_half, kp, a)
                kaug[2 * p + 1, rows, :] = jnp.where(lo_half, a, kp)

    ones_hi = jnp.where((lane >= HEAD_DIM) & (lane < HEAD_DIM + 3), 1.0, 0.0).astype(bf16)
    ones_lo = jnp.where(lane < 3, 1.0, 0.0).astype(bf16)
    qa = []
    for p in range(n_heads // 2):
        qp = q_ref[0, :, p * 128:(p + 1) * 128]
        qa.append(jnp.where(lo_half, qp, ones_hi))
        qa.append(jnp.where(lo_half, ones_lo, qp))
    keep = lax.broadcasted_iota(i32, (tq, tq), 0) <= lax.broadcasted_iota(i32, (tq, tq), 1)
    acc[...] = jnp.zeros_like(acc)

    def score(jb, buf):
        k0 = pl.multiple_of(jb * tq, tq)
        for h in range(n_heads):
            buf[h] = _dot_nt(kaug[h, pl.ds(k0, tq), :], qa[h])

    def absorb(jb, ms, masked, buf):
        new_ms = []
        for h in range(n_heads):
            s = buf[h]
            if masked:
                s = jnp.where(keep, s, NEG)
            m_new = jnp.maximum(ms[h], jnp.max(s, axis=0, keepdims=True))
            alpha = jnp.exp2(ms[h] - m_new)
            p = jnp.exp2(s - m_new).astype(bf16)
            new_ms.append(m_new)
            acc[h] = acc[h] * alpha + _dot(vt[jb, h * V_ROWS:(h + 1) * V_ROWS, :], p)
        return tuple(new_ms)

    def pair(pp, ms):
        j0 = 2 * pp
        score(j0 + 1, sc1)
        ms = absorb(j0, ms, False, sc0)
        score(j0 + 2, sc0)
        return absorb(j0 + 1, ms, False, sc1)

    score(0, sc0)
    ms = lax.fori_loop(0, i // 2, pair, tuple(jnp.full((1, tq), NEG, f32) for _ in range(n_heads)))

    @pl.when(i % 2 == 0)
    def _even():
        absorb(i, ms, True, sc0)

    @pl.when(i % 2 == 1)
    def _odd():
        score(i, sc1)
        absorb(i, absorb(i - 1, ms, False, sc0), True, sc1)

    out_t = jnp.concatenate([acc[h, 0:HEAD_DIM, :] * (1.0 / acc[h, HEAD_DIM:HEAD_DIM + 1, :]) for h in range(n_heads)],
                            axis=0)
    ms_ = jnp.mean(out_t * out_t, axis=0, keepdims=True)
    out_t = out_t * lax.rsqrt(ms_ + EPS)
    o_ref[0] = (out_t.T * ng_ref[...]).astype(bf16)


def _piece_select():
    sel = np.zeros((4, 384, 128), np.float32)
    for pair in range(4):
        for j in range(3):
            sel[pair, j * 128 + 8 + 2 * pair, HEAD_DIM + j] = 1.0
            sel[pair, j * 128 + 8 + 2 * pair + 1, j] = 1.0
    return jnp.asarray(sel, bf16)


def _attn(qkv, cumc, norm_g):
    bsz, t, _ = qkv.shape
    tq = min(ATT_BLOCK, t)
    return pl.pallas_call(
        functools.partial(_attn_kernel, tq=tq, t=t),
        grid=(bsz, t // tq),
        in_specs=[pl.BlockSpec((1, tq, D_ATT), lambda b, i: (b, i, 0)),
                  pl.BlockSpec((1, t, D_ATT), lambda b, i: (b, 0, 1)),
                  pl.BlockSpec((1, t, D_ATT), lambda b, i: (b, 0, 2)),
                  pl.BlockSpec((1, t, 128), lambda b, i: (b, 0, 0)),
                  pl.BlockSpec((4, 384, 128), lambda b, i: (0, 0, 0)),
                  pl.BlockSpec((1, D_ATT), lambda b, i: (0, 0))],
        out_specs=pl.BlockSpec((1, tq, D_ATT), lambda b, i: (b, i, 0)),
        out_shape=jax.ShapeDtypeStruct((bsz, t, D_ATT), bf16),
        scratch_shapes=[pltpu.VMEM((D_ATT // HEAD_DIM, t, 128), bf16),
                        pltpu.VMEM((t // tq, (D_ATT // HEAD_DIM) * V_ROWS, tq), bf16),
                        pltpu.VMEM((D_ATT // HEAD_DIM, V_ROWS, tq), f32),
                        pltpu.VMEM((D_ATT // HEAD_DIM, tq, tq), f32),
                        pltpu.VMEM((D_ATT // HEAD_DIM, tq, tq), f32)],
        compiler_params=pltpu.CompilerParams(dimension_semantics=("parallel", "arbitrary"),
                                             vmem_limit_bytes=VMEM_LIMIT),
        name="attn",
    )(qkv, qkv, qkv, cumc, _piece_select(), norm_g)


def _layer_norm(y, g, b):
    mu = jnp.mean(y, axis=-1, keepdims=True)
    yc = y - mu
    var = jnp.mean(yc * yc, axis=-1, keepdims=True)
    return yc * lax.rsqrt(var + EPS) * g + b


def _outproj_kernel(ys_ref, ya_ref, x_ref, gt_ref, sc_ref, sh_ref, wo_ref, lng_ref, lnb_ref, wrh_ref, wrl_ref, rb_ref,
                    x1_ref, rt_ref, dest_ref, cnt_ref, tbl_ref, xrows_hbm,
                    carry, galloc, tbl, u2t, didx_v, didx_s, cnt_v, cnt_s, zeros, idx_sem, sc_sem, z_sem,
                    *, tm, blk, n_blk):
    i = pl.program_id(0)
    last = pl.num_programs(0) - 1
    cur = i % 2
    prev = 1 - cur

    def idx_ready(slot):
        return pltpu.make_async_copy(didx_v.at[slot], didx_s.at[slot], idx_sem.at[slot])

    def dispatched(slot):
        return pltpu.make_async_copy(u2t.at[slot], u2t.at[slot], sc_sem.at[slot])

    def dispatch_copy(slot, r, k):
        return pltpu.make_async_copy(u2t.at[slot, pl.ds(r * ROW_TILE, ROW_TILE)], xrows_hbm.at[didx_s[slot, k, r]],
                                     sc_sem.at[slot])

    @pl.when(i == 0)
    def _init():
        carry[...] = jnp.zeros_like(carry)
        galloc[...] = jnp.zeros_like(galloc)
        tbl[...] = jnp.zeros_like(tbl)

    @pl.when(i >= 2)
    def _reuse():
        dispatched(cur).wait()
        dispatched(cur).wait()

    def step(dispatch_prev):
        n_parts = OUTPROJ_PARTS
        n_slices = 3 * n_parts + 2

        def dispatch_slice(c):
            if dispatch_prev:
                for r_ in range(c * tm // n_slices, (c + 1) * tm // n_slices):
                    dispatch_copy(prev, r_, 0).start(priority=0)
                    dispatch_copy(prev, r_, 1).start(priority=1)

        if dispatch_prev:
            idx_ready(prev).wait()
        part = tm // n_parts
        hs = []
        for a in range(n_parts):
            rs = slice(a * part, (a + 1) * part)
            dispatch_slice(a)
            hs.append(_dot(ys_ref[rs, :], wo_ref[0:D_SSM, :]) + _dot(ya_ref[rs, :], wo_ref[D_SSM:D, :]))
        logit_parts = []
        for a in range(n_parts):
            rs = slice(a * part, (a + 1) * part)
            dispatch_slice(n_parts + 2 * a)
            x1 = _layer_norm(ALPHA * x_ref[rs, :] + (1.0 + gt_ref[0, 0]) * hs[a], lng_ref[...], lnb_ref[...])
            x1_ref[rs, :] = x1
            u2 = x1 * (1.0 + sc_ref[0, 0]) + sh_ref[0, 0]
            _store_row_tiles(u2t.at[cur, pl.ds(a * part * ROW_TILE, part * ROW_TILE)], u2)
            dispatch_slice(n_parts + 2 * a + 1)
            uh = u2.astype(bf16)
            ul = (u2 - uh.astype(f32)).astype(bf16)
            logit_parts.append((_dot(uh, wrh_ref[...]) + _dot(ul, wrh_ref[...])) + _dot(uh, wrl_ref[...]))
        logits = jnp.concatenate(logit_parts, axis=0) + rb_ref[...]
        lane = lax.broadcasted_iota(i32, (tm, 128), 1).astype(f32)
        big = jnp.float32(1e9)

        def first_max(vals):
            m = jnp.max(vals, axis=-1, keepdims=True)
            return m, jnp.min(jnp.where(vals == m, lane, big), axis=-1, keepdims=True)

        gl = jnp.where(lane < N_GROUPS_R, logits, NEG)
        gmax, gidx = first_max(gl)
        g_p = 1.0 / jnp.sum(jnp.exp(gl - gmax), axis=-1, keepdims=True)
        lo = N_GROUPS_R + EXPERTS_PER_GROUP * gidx
        el = jnp.where((lane >= lo) & (lane < lo + EXPERTS_PER_GROUP), logits, NEG)
        m1, i1 = first_max(el)
        el2 = jnp.where(lane == i1, NEG, el)
        m2, i2 = first_max(el2)
        r = jnp.exp(m2 - m1)
        w1 = g_p / (1.0 + r)
        w2 = g_p * r / (1.0 + r)
        dispatch_slice(3 * n_parts)

        oh1 = lane == i1
        oh2 = lane == i2
        oh = jnp.where(oh1 | oh2, 1.0, 0.0)
        r_i = lax.broadcasted_iota(i32, (tm, tm), 0)
        c_i = lax.broadcasted_iota(i32, (tm, tm), 1)
        lower = jnp.where(r_i > c_i, 1.0, 0.0).astype(bf16)
        c_old = carry[...]
        prefix = _dot(lower, oh.astype(bf16)) + c_old
        rank1 = jnp.sum(jnp.where(oh1, prefix, 0.0), axis=-1, keepdims=True)
        rank2 = jnp.sum(jnp.where(oh2, prefix, 0.0), axis=-1, keepdims=True)
        c_new = c_old + jnp.sum(oh, axis=0, keepdims=True)
        carry[...] = c_new
        dispatch_slice(3 * n_parts + 1)

        nb_old = jnp.floor((c_old + (blk - 1)) * (1.0 / blk))
        nb_new = jnp.floor((c_new + (blk - 1)) * (1.0 / blk))
        fresh = nb_new - nb_old
        sq_r = lax.broadcasted_iota(i32, (128, 128), 0)
        sq_c = lax.broadcasted_iota(i32, (128, 128), 1)
        before = jnp.where(sq_r < sq_c, 1.0, 0.0).astype(bf16)
        base = galloc[...] + _dot(jnp.broadcast_to(fresh, (8, 128)).astype(bf16), before)[0:1, :]
        galloc[...] = galloc[...] + jnp.sum(fresh, axis=-1, keepdims=True)
        ordinal = sq_r.astype(f32)
        tbl[...] = jnp.where((ordinal >= nb_old) & (ordinal < nb_new), base + (ordinal - nb_old), tbl[...])
        tbl_ref[...] = tbl[...]
        cnt_ref[...] = jnp.concatenate([c_new, galloc[...], jnp.zeros((6, 128), f32)], axis=0)

        rt = jnp.where(lane == 4, w1, jnp.where(lane == 5, w2, 0.0))
        rt_ref[...] = rt

        ids_bf = tbl[...].astype(bf16)

        def row_of(onehot, rank):
            nth = jnp.floor(rank * (1.0 / blk))
            ids = _dot(jnp.where(lane == nth, 1.0, 0.0).astype(bf16), ids_bf)
            return jnp.sum(jnp.where(onehot, ids, 0.0), axis=-1, keepdims=True) * blk + (rank - nth * blk)

        d1 = row_of(oh1, rank1)
        d2 = row_of(oh2, rank2)
        dd = jnp.where(lane == 0, d1, jnp.where(lane == 1, d2, 0.0)).T[0:8, :].astype(i32)
        dest_ref[...] = dd
        didx_v[cur] = dd
        idx_ready(cur).start()

    pl.when(i == 0)(lambda: step(False))
    pl.when(i > 0)(lambda: step(True))

    @pl.when(i == last)
    def _epilogue():
        idx_ready(cur).wait()

        def issue(r_, c):
            dispatch_copy(cur, r_, 0).start()
            dispatch_copy(cur, r_, 1).start()
            return c
        lax.fori_loop(0, tm, issue, 0)

        cnt = carry[...]
        n_blocks = jnp.floor((cnt + (blk - 1)) * (1.0 / blk))
        ordinal = lax.broadcasted_iota(i32, (128, 128), 0).astype(f32)
        last_id = jnp.sum(jnp.where(ordinal == n_blocks - 1.0, tbl[...], 0.0), axis=0, keepdims=True)
        used = cnt - (n_blocks - 1.0) * blk
        cnt_v[...] = jnp.concatenate([last_id * blk + used, blk - used, galloc[...], jnp.zeros((5, 128), f32)],
                                     axis=0).astype(i32)
        to_smem = pltpu.make_async_copy(cnt_v, cnt_s, z_sem)
        to_smem.start()
        to_smem.wait()
        zeros[...] = jnp.zeros_like(zeros)
        sizes = [1 << b for b in reversed(range(blk.bit_length() - 1))]

        def for_each_fill(fn):
            def tail(e, c):
                pad = cnt_s[1, N_GROUPS_R + e]
                off = cnt_s[0, N_GROUPS_R + e]
                for sz in sizes:
                    @pl.when((pad & sz) != 0)
                    def _(off=off, sz=sz):
                        fn(pltpu.make_async_copy(zeros.at[pl.ds(0, sz)], xrows_hbm.at[pl.ds(off, sz)], z_sem))
                    off = off + (pad & sz)
                return c
            lax.fori_loop(0, N_EXPERTS, tail, 0)

            def whole(b, c):
                fn(pltpu.make_async_copy(zeros, xrows_hbm.at[pl.ds(b * blk, blk)], z_sem))
                return c
            lax.fori_loop(cnt_s[2, 0], n_blk, whole, 0)

        for_each_fill(lambda copy: copy.start())
        for_each_fill(lambda copy: copy.wait())

        @pl.when(i > 0)
        def _prev_done():
            dispatched(prev).wait()
            dispatched(prev).wait()
        dispatched(cur).wait()
        dispatched(cur).wait()


def _outproj(y_ssm, y_att, x, mod4, w_out, ln_g, ln_b, wr_hi, wr_lo, rb, t, blk, n_blk):
    n = x.shape[0]
    tm = min(OUTPROJ_ROWS, t)
    nt = t // tm
    assert n // blk + 1 <= 128 and n_blk <= 256, "block-id table: 128 blocks per expert, ids exact in bf16"
    vec = lambda k: pl.BlockSpec((1, 1, 1, D), lambda i, k=k: (k, i // nt, 0, 0))
    full = lambda shape: pl.BlockSpec(shape, lambda i: (0,) * len(shape))
    rows = lambda w: pl.BlockSpec((tm, w), lambda i: (i, 0))
    return pl.pallas_call(
        functools.partial(_outproj_kernel, tm=tm, blk=blk, n_blk=n_blk),
        grid=(n // tm,),
        in_specs=[rows(D_SSM), rows(D_ATT), rows(D), vec(2), vec(4), vec(3),
                  full((D, D)), full((1, D)), full((1, D)), full((D, 128)), full((D, 128)), full((1, 128))],
        out_specs=[rows(D), rows(128), pl.BlockSpec((8, tm), lambda i: (0, i)), full((8, 128)), full((128, 128)),
                   pl.BlockSpec(memory_space=pl.ANY)],
        out_shape=[jax.ShapeDtypeStruct((n, D), f32), jax.ShapeDtypeStruct((n, 128), f32),
                   jax.ShapeDtypeStruct((8, n), i32), jax.ShapeDtypeStruct((8, 128), f32),
                   jax.ShapeDtypeStruct((128, 128), f32),
                   jax.ShapeDtypeStruct((n_blk * blk, ROW_TILE, 128), f32)],
        scratch_shapes=[pltpu.VMEM((1, 128), f32), pltpu.VMEM((1, 128), f32), pltpu.VMEM((128, 128), f32),
                        pltpu.VMEM((2, tm * ROW_TILE, 128), f32),
                        pltpu.VMEM((2, 8, tm), i32), pltpu.SMEM((2, 8, tm), i32),
                        pltpu.VMEM((8, 128), i32), pltpu.SMEM((8, 128), i32),
                        pltpu.VMEM((blk, ROW_TILE, 128), f32),
                        pltpu.SemaphoreType.DMA((2,)), pltpu.SemaphoreType.DMA((2,)), pltpu.SemaphoreType.DMA(())],
        compiler_params=pltpu.CompilerParams(dimension_semantics=("arbitrary",), vmem_limit_bytes=VMEM_LIMIT),
        name="outproj",
    )(y_ssm, y_att, x, mod4, mod4, mod4, w_out, ln_g, ln_b, wr_hi, wr_lo, rb)


def _tile_copy(src_hbm, row, buf, slot, sem):
    return pltpu.make_async_copy(src_hbm.at[row], buf.at[pl.ds(slot * ROW_TILE, ROW_TILE)], sem)


def _moe_kernel(be_ref, br_ref, nu_ref, nxt_ref, par_ref, x_ref, wg_hbm, wu_hbm, wd_hbm, y_ref,
                wfg, wfu, wfd, wgb, wub, wdb, wsem, *, blk):
    i = pl.program_id(0)
    used = i < nu_ref[0]

    def fetch(e, slot):
        return [pltpu.make_async_copy(src.at[e], dst.at[slot], wsem.at[slot])
                for src, dst in ((wg_hbm, wfg), (wu_hbm, wfu), (wd_hbm, wfd))]

    @pl.when(i == 0)
    def _first():
        for c in fetch(be_ref[0], par_ref[0]):
            c.start()

    @pl.when(jnp.logical_and(used, jnp.logical_or(i == 0, be_ref[i] != be_ref[jnp.maximum(i - 1, 0)])))
    def _switch():
        slot = par_ref[i]
        for c in fetch(be_ref[i], slot):
            c.wait()

        @pl.when(nxt_ref[i] >= 0)
        def _next():
            for c in fetch(nxt_ref[i], 1 - slot):
                c.start(priority=1)
        wgb[...] = wfg[slot].astype(bf16)
        wub[...] = wfu[slot].astype(bf16)
        wdb[...] = wfd[slot].astype(bf16)

    @pl.when(jnp.logical_not(used))
    def _spare():
        y_ref[...] = jnp.zeros_like(y_ref)

    @pl.when(used)
    def _work():
        xb = _load_row_tiles(x_ref, blk).astype(bf16)
        hid = (_silu(_dot(xb, wgb[...])) * _dot(xb, wub[...])).astype(bf16)
        _store_row_tiles(y_ref, _dot(hid, wdb[...]))


def _moe(block_e, block_row, n_used, next_e, parity, x_rows, w_gate, w_up, w_down, blk):
    nblk = block_e.shape[0]
    rspec = pl.BlockSpec((blk * ROW_TILE, 128), lambda i, be, br, *_: (br[i], 0))
    xspec = pl.BlockSpec((blk * ROW_TILE, 128), lambda i, be, br, nu, *_: (br[jnp.minimum(i, nu[0] - 1)], 0))
    anyspec = pl.BlockSpec(memory_space=pl.ANY)
    grid_spec = pltpu.PrefetchScalarGridSpec(
        num_scalar_prefetch=5,
        grid=(nblk,),
        in_specs=[xspec, anyspec, anyspec, anyspec],
        out_specs=rspec,
        scratch_shapes=[pltpu.VMEM((2, D, D_EXPERT), f32), pltpu.VMEM((2, D, D_EXPERT), f32),
                        pltpu.VMEM((2, D_EXPERT, D), f32),
                        pltpu.VMEM((D, D_EXPERT), bf16), pltpu.VMEM((D, D_EXPERT), bf16),
                        pltpu.VMEM((D_EXPERT, D), bf16),
                        pltpu.SemaphoreType.DMA((2,))],
    )
    return pl.pallas_call(
        functools.partial(_moe_kernel, blk=blk),
        grid_spec=grid_spec,
        out_shape=jax.ShapeDtypeStruct(x_rows.shape, f32),
        compiler_params=pltpu.CompilerParams(dimension_semantics=("arbitrary",), vmem_limit_bytes=VMEM_LIMIT),
        name="moe",
    )(block_e, block_row, n_used, next_e, parity, x_rows, w_gate, w_up, w_down)


def _combine_kernel(d1_ref, d2_ref, y_hbm, x1_ref, rt_ref, gt_ref, lng_ref, lnb_ref, o_ref, buf, sem, *, tm):
    i = pl.program_id(0)
    last = pl.num_programs(0) - 1

    @pl.when(i == 0)
    def _prologue():
        for s in range(2):
            def issue(r, c, s=s):
                base = jnp.minimum(s, last) * tm
                _tile_copy(y_hbm, d1_ref[base + r], buf.at[s, 0], r, sem.at[s]).start()
                _tile_copy(y_hbm, d2_ref[base + r], buf.at[s, 1], r, sem.at[s]).start()
                return c
            lax.fori_loop(0, tm, issue, 0)

    def rows_ready(slot):
        return pltpu.make_async_copy(buf.at[slot], buf.at[slot], sem.at[slot])

    slot = i % 3
    nxt = (i + 2) % 3
    base = jnp.minimum(i + 2, last) * tm
    rows_ready(slot).wait()
    for r in range(tm):
        _tile_copy(y_hbm, d1_ref[base + r], buf.at[nxt, 0], r, sem.at[nxt]).start(priority=0)
        _tile_copy(y_hbm, d2_ref[base + r], buf.at[nxt, 1], r, sem.at[nxt]).start(priority=1)
    rt = rt_ref[...]
    moe = rt[:, 4:5] * _load_row_tiles(buf.at[slot, 0], tm) + rt[:, 5:6] * _load_row_tiles(buf.at[slot, 1], tm)
    y = ALPHA * x1_ref[...] + (1.0 + gt_ref[0, 0]) * moe
    o_ref[...] = _layer_norm(y, lng_ref[...], lnb_ref[...])

    @pl.when(i == last)
    def _drain():
        rows_ready((i + 1) % 3).wait()
        rows_ready((i + 2) % 3).wait()


def _combine(dest1, dest2, y_rows, x1, rt, mod4, ln_g, ln_b, t):
    n = x1.shape[0]
    tm = min(COMBINE_ROWS, t)
    nt = t // tm
    full = lambda shape: pl.BlockSpec(shape, lambda i, d1, d2: (0,) * len(shape))
    rows = lambda w: pl.BlockSpec((tm, w), lambda i, d1, d2: (i, 0))
    grid_spec = pltpu.PrefetchScalarGridSpec(
        num_scalar_prefetch=2,
        grid=(n // tm,),
        in_specs=[pl.BlockSpec(memory_space=pl.ANY), rows(D), rows(128),
                  pl.BlockSpec((1, 1, 1, D), lambda i, d1, d2: (5, i // nt, 0, 0)),
                  full((1, D)), full((1, D))],
        out_specs=rows(D),
        scratch_shapes=[pltpu.VMEM((3, 2, tm * ROW_TILE, 128), f32), pltpu.SemaphoreType.DMA((3,))],
    )
    return pl.pallas_call(
        functools.partial(_combine_kernel, tm=tm),
        grid_spec=grid_spec,
        out_shape=jax.ShapeDtypeStruct((n, D), f32),
        compiler_params=pltpu.CompilerParams(dimension_semantics=("arbitrary",), vmem_limit_bytes=VMEM_LIMIT),
        name="combine",
    )(dest1, dest2, y_rows, x1, rt, mod4, ln_g, ln_b)


def _hi_lo(w):
    hi = w.astype(bf16)
    return hi, (w - hi.astype(f32)).astype(bf16)


def _pad_lanes(v, offset, width):
    return jnp.zeros((width,), f32).at[offset:offset + v.shape[0]].set(v)


def _layer(x, c, ada_w, ada_b, w_in, conv_w, conv_b, dt_bias, a_log, d_skip, ssm_norm_g, fg_bias, att_norm_g,
           w_out, ln1_g, ln1_b, router_g_w, router_g_b, router_e_w, router_e_b, w_gate, w_up, w_down, ln2_g, ln2_b):
    bsz, t, _ = x.shape
    n = bsz * t

    mod4 = _ada(c, ada_w, ada_b).reshape(6, bsz, 1, D)

    w_main, w_small = _pack_w_in(w_in)
    zx, bc, qkv, dtf = _inproj(x, mod4, w_main, w_small, conv_w, conv_b.reshape(1, -1))

    pc = jnp.stack([_pad_lanes(dt_bias, 0, 128), _pad_lanes(a_log, 0, 128), _pad_lanes(fg_bias, 8, 128)]
                   + [jnp.zeros((128,), f32)] * 5)
    rep = lambda v: jnp.repeat(v, HEAD_DIM)
    pe = jnp.stack([rep(dt_bias), rep(a_log), rep(d_skip), ssm_norm_g] + [jnp.zeros((D_SSM,), f32)] * 4)
    y_ssm, cumc = _ssd(zx, bc, dtf, pc, pe)

    y_att = _attn(qkv, cumc, att_norm_g.reshape(1, -1))

    wr = jnp.concatenate([router_g_w, router_e_w, jnp.zeros((D, 128 - N_GROUPS_R - N_EXPERTS), f32)], axis=1)
    wr_hi, wr_lo = _hi_lo(wr)
    rb = jnp.concatenate([router_g_b, router_e_b, jnp.zeros((128 - N_GROUPS_R - N_EXPERTS,), f32)]).reshape(1, 128)
    blk = MOE_ROWS
    nblk = (2 * n) // blk + N_EXPERTS
    x1, rt, dest, cnt, tbl, x_rows = _outproj(y_ssm.reshape(n, D_SSM), y_att.reshape(n, D_ATT), x.reshape(n, D), mod4,
                                              w_out.astype(bf16), ln1_g.reshape(1, D), ln1_b.reshape(1, D),
                                              wr_hi, wr_lo, rb, t, blk, nblk)

    counts = cnt[0, N_GROUPS_R:N_GROUPS_R + N_EXPERTS].astype(i32)
    nb = (counts + blk - 1) // blk
    nb_end = jnp.cumsum(nb)
    n_used = nb_end[N_EXPERTS - 1:]
    every = jnp.arange(nblk, dtype=i32)
    step = jnp.minimum(every, n_used[0] - 1)
    owner = (nb_end[None, :] <= step[:, None]).astype(i32)
    block_e = jnp.sum(owner, axis=1)
    first = jnp.sum(owner * nb[None, :], axis=1)
    lanes = jnp.arange(128, dtype=i32)[None, :]
    nth_row = jnp.dot((lanes == (step - first)[:, None]).astype(f32), tbl, precision=lax.Precision.HIGHEST)
    block_id = jnp.sum(jnp.where(lanes == (block_e + N_GROUPS_R)[:, None], nth_row, 0.0), axis=1).astype(i32)
    block_row = jnp.where(every < n_used[0], block_id, every)
    run_end = first + jnp.sum((jnp.arange(N_EXPERTS)[None, :] == block_e[:, None]) * nb[None, :], axis=1)
    next_e = jnp.where(run_end < n_used[0], jnp.sum((nb_end[None, :] <= run_end[:, None]).astype(i32), axis=1), -1)
    parity = jnp.sum(((jnp.arange(N_EXPERTS)[None, :] < block_e[:, None]) & (nb[None, :] > 0)).astype(i32), axis=1) % 2

    y_rows = _moe(block_e, block_row, n_used, next_e, parity, x_rows.reshape(nblk * blk * ROW_TILE, 128),
                  w_gate, w_up, w_down, blk)
    out = _combine(dest[0], dest[1], y_rows.reshape(nblk * blk, ROW_TILE, 128), x1, rt, mod4,
                   ln2_g.reshape(1, D), ln2_b.reshape(1, D), t)
    return out.reshape(bsz, t, D)


def kernel(x, c, ada_w, ada_b, w_in, conv_w, conv_b, dt_bias, a_log, d_skip, ssm_norm_g, fg_bias, att_norm_g, w_out,
           ln1_g, ln1_b, router_g_w, router_g_b, router_e_w, router_e_b, w_gate, w_up, w_down, ln2_g, ln2_b):
    depth = ada_w.shape[0]
    for l in range(depth):
        x = _layer(x, c, ada_w[l], ada_b[l], w_in[l], conv_w[l], conv_b[l], dt_bias[l], a_log[l], d_skip[l],
                   ssm_norm_g[l], fg_bias[l], att_norm_g[l], w_out[l], ln1_g[l], ln1_b[l], router_g_w[l],
                   router_g_b[l], router_e_w[l], router_e_b[l], w_gate[l], w_up[l], w_down[l], ln2_g[l], ln2_b[l])
    return x
```

```python
import functools

import jax
import jax.numpy as jnp
import numpy as np
from jax import lax
from jax.experimental import pallas as pl
from jax.experimental.pallas import tpu as pltpu

f32 = jnp.float32
bf16 = jnp.bfloat16
i32 = jnp.int32

D = 1024
D_SSM = 512
D_ATT = 512
HEAD_DIM = 64
GROUP_W = 256
N_STATE = 128
CONV_K = 4
N_GROUPS_R = 4
EXPERTS_PER_GROUP = 8
N_EXPERTS = 32
D_EXPERT = 512
ALPHA = 2.0 ** 0.25
EPS = 1e-5
NEG = -1e30
LOG2E = 1.4426950408889634
QK_SCALE = HEAD_DIM ** -0.5 * LOG2E
V_ROWS = 80

SSD_CHUNK = 256
ATT_BLOCK = 256
INPROJ_ROWS = 512
OUTPROJ_ROWS = 512
OUTPROJ_PARTS = 2
MOE_ROWS = 512
COMBINE_ROWS = 256
DEST_ROWS = 2048
VMEM_LIMIT = 48 * 1024 * 1024


def _dot(a, b):
    return jnp.dot(a, b, preferred_element_type=f32)


def _dot_nt(a, b):
    return lax.dot_general(a, b, (((1,), (1,)), ((), ())), preferred_element_type=f32)


def _dot_tn(a, b):
    return lax.dot_general(a, b, (((0,), (0,)), ((), ())), preferred_element_type=f32)


def _split3(v):
    hi = v.astype(bf16)
    r1 = v - hi.astype(f32)
    mid = r1.astype(bf16)
    lo = (r1 - mid.astype(f32)).astype(bf16)
    return hi, mid, lo


def _dot_exact_lhs(m, v):
    hi, mid, lo = _split3(v)
    return (_dot(m, hi) + _dot(m, mid)) + _dot(m, lo)


def _dot_exact_rhs(v, m):
    hi, mid, lo = _split3(v)
    return (_dot(hi, m) + _dot(mid, m)) + _dot(lo, m)


ROW_TILE = 8


def _store_row_tiles(ref, val):
    rows = val.shape[0]
    for c in range(ROW_TILE):
        ref[pl.ds(c, rows, stride=ROW_TILE), :] = val[:, c * 128:(c + 1) * 128]


def _load_row_tiles(ref, rows):
    return jnp.concatenate([ref[pl.ds(c, rows, stride=ROW_TILE), :] for c in range(ROW_TILE)], axis=1)


PACK_TILE = 4
u32 = jnp.uint32


def _pack_rows(vb):
    lo = pltpu.bitcast(vb[:, 0:D // 2].astype(f32), u32) >> 16
    hi = pltpu.bitcast(vb[:, D // 2:D].astype(f32), u32) & jnp.uint32(0xFFFF0000)
    return lo | hi


def _store_packed_rows(ref, packed):
    rows = packed.shape[0]
    for c in range(PACK_TILE):
        ref[pl.ds(c, rows, stride=PACK_TILE), :] = packed[:, c * 128:(c + 1) * 128]


def _load_packed_rows(ref, rows):
    words = [ref[pl.ds(c, rows, stride=PACK_TILE), :] for c in range(PACK_TILE)]
    lo = jnp.concatenate([pltpu.bitcast(w << 16, f32) for w in words], axis=1)
    hi = jnp.concatenate([pltpu.bitcast(w & jnp.uint32(0xFFFF0000), f32) for w in words], axis=1)
    return lo.astype(bf16), hi.astype(bf16)


def _softplus(x):
    return jnp.maximum(x, 0.0) + jnp.log1p(jnp.exp(-jnp.abs(x)))


def _silu(x):
    return x * jax.nn.sigmoid(x)


def _ada_kernel(c_ref, w_ref, b_ref, o_ref):
    s = _silu(c_ref[...]).astype(bf16)
    o_ref[0] = _dot(s, w_ref[...].astype(bf16)) + b_ref[0]


def _ada(c, w, b):
    bsz = c.shape[0]
    return pl.pallas_call(
        _ada_kernel,
        grid=(6,),
        in_specs=[pl.BlockSpec((bsz, D), lambda j: (0, 0)),
                  pl.BlockSpec((D, D), lambda j: (0, j)),
                  pl.BlockSpec((1, 1, D), lambda j: (j, 0, 0))],
        out_specs=pl.BlockSpec((1, bsz, D), lambda j: (j, 0, 0)),
        out_shape=jax.ShapeDtypeStruct((6, bsz, D), f32),
        compiler_params=pltpu.CompilerParams(dimension_semantics=("arbitrary",), vmem_limit_bytes=VMEM_LIMIT),
        name="ada",
    )(c, w, b.reshape(6, 1, D))


W_IN_COLS = 3088
PACK_ROWS = 256


def _pack_w_in_kernel(wt_ref, wm_ref, ws_ref):
    for j in range(3072 // PACK_ROWS):
        src = j * PACK_ROWS if j * PACK_ROWS < 1536 else j * PACK_ROWS + 8
        wm_ref[:, j * PACK_ROWS:(j + 1) * PACK_ROWS] = wt_ref[src:src + PACK_ROWS, :].T.astype(bf16)
    small = jnp.concatenate([wt_ref[1536:1544, :], wt_ref[W_IN_COLS - 8:W_IN_COLS, :], jnp.zeros((112, D), f32)],
                            axis=0).T
    hi = small.astype(bf16)
    ws_ref[:, 0:128] = hi
    ws_ref[:, 128:256] = (small - hi.astype(f32)).astype(bf16)


def _pack_w_in(w_in):
    full = lambda shape: pl.BlockSpec(shape, lambda i: (0,) * len(shape))
    return pl.pallas_call(
        _pack_w_in_kernel,
        grid=(1,),
        in_specs=[full((W_IN_COLS, D))],
        out_specs=[full((D, 3072)), full((D, 256))],
        out_shape=[jax.ShapeDtypeStruct((D, 3072), bf16), jax.ShapeDtypeStruct((D, 256), bf16)],
        compiler_params=pltpu.CompilerParams(dimension_semantics=("arbitrary",), vmem_limit_bytes=VMEM_LIMIT),
        name="pack_w_in",
    )(w_in.T)


def _inproj_kernel(x_ref, sc_ref, sh_ref, wm_ref, ws_ref, cw_ref, cb_ref, zx_ref, bc_ref, qkv_ref, dtf_ref, xcat, *, tm):
    i = pl.program_id(1)

    @pl.when(i == 0)
    def _init():
        xcat[0:8, :] = jnp.zeros((8, 2 * D_SSM), f32)

    u = x_ref[0] * (1.0 + sc_ref[0, 0]) + sh_ref[0, 0]
    ub = u.astype(bf16)
    xcat[8:8 + tm, 0:512] = _dot(ub, wm_ref[:, 512:1024])
    xcat[8:8 + tm, 512:1024] = _dot(ub, wm_ref[:, 1024:1536])
    zx_ref[0, :, 0:512] = _dot(ub, wm_ref[:, 0:512])
    acc = cw_ref[0:1, :] * xcat[5:5 + tm, :] + cb_ref[...]
    for k in range(1, CONV_K):
        acc = acc + cw_ref[k:k + 1, :] * xcat[5 + k:5 + k + tm, :]
    xcat[0:8, :] = xcat[tm:tm + 8, :]
    xbc = _silu(acc)
    zx_ref[0, :, 512:1024] = xbc[:, 0:512]
    bc_ref[0] = xbc[:, 512:1024].astype(bf16)
    for j, scale in enumerate((QK_SCALE, 1.0, 1.0)):
        qkv_ref[0, :, j * 512:(j + 1) * 512] = (
            _dot(ub, wm_ref[:, 1536 + j * 512:1536 + (j + 1) * 512]) * scale).astype(bf16)
    ul = (u - ub.astype(f32)).astype(bf16)
    d_hl = _dot(ub, ws_ref[...])
    dtf_ref[0] = (d_hl[:, 0:128] + _dot(ul, ws_ref[:, 0:128])) + d_hl[:, 128:256]


def _inproj(x, mod4, w_main, ws, conv_w, conv_b):
    bsz, t, _ = x.shape
    tm = min(INPROJ_ROWS, t)
    vec = lambda k: pl.BlockSpec((1, 1, 1, D), lambda b, i, k=k: (k, b, 0, 0))
    full = lambda shape: pl.BlockSpec(shape, lambda b, i: (0,) * len(shape))
    rows = lambda w: pl.BlockSpec((1, tm, w), lambda b, i: (b, i, 0))
    return pl.pallas_call(
        functools.partial(_inproj_kernel, tm=tm),
        grid=(bsz, t // tm),
        in_specs=[rows(D), vec(1), vec(0), full((D, 3072)), full((D, 256)),
                  full((CONV_K, 2 * D_SSM)), full((1, 2 * D_SSM))],
        out_specs=[rows(1024), rows(512), rows(1536), rows(128)],
        out_shape=[jax.ShapeDtypeStruct((bsz, t, 1024), f32),
                   jax.ShapeDtypeStruct((bsz, t, 512), bf16),
                   jax.ShapeDtypeStruct((bsz, t, 1536), bf16),
                   jax.ShapeDtypeStruct((bsz, t, 128), f32)],
        scratch_shapes=[pltpu.VMEM((tm + 8, 2 * D_SSM), f32)],
        compiler_params=pltpu.CompilerParams(dimension_semantics=("parallel", "arbitrary"),
                                             vmem_limit_bytes=VMEM_LIMIT),
        name="inproj",
    )(x, mod4, mod4, w_main, ws, conv_w, conv_b)


def _ssd_kernel(z_ref, xs_ref, bc_ref, dtf_ref, pc_ref, pe_ref, y_ref, cumc_ref, state, carry, *, lc):
    j = pl.program_id(1)

    @pl.when(j == 0)
    def _init():
        state[...] = jnp.zeros_like(state)
        carry[...] = jnp.zeros_like(carry)

    xs = xs_ref[0]
    bm = bc_ref[0, :, 0:256]
    cm = bc_ref[0, :, 256:512]

    dtf = dtf_ref[0]
    lane = lax.broadcasted_iota(i32, (lc, 128), 1)
    dt_c = _softplus(dtf + pc_ref[0:1, :])
    a_c = dt_c * (-jnp.exp(pc_ref[1:2, :]))
    logf = -_softplus(-(dtf + pc_ref[2:3, :]))
    v = jnp.where(lane < 8, a_c, logf)
    r_i = lax.broadcasted_iota(i32, (lc, lc), 0)
    c_i = lax.broadcasted_iota(i32, (lc, lc), 1)
    tri = r_i >= c_i
    tri_b = jnp.where(tri, 1.0, 0.0).astype(bf16)
    cum = _dot_exact_lhs(tri_b, v) + carry[...]
    carry[...] = jnp.where(lane[0:1, :] >= 8, cum[lc - 1:lc, :], 0.0)
    cumc_ref[0] = cum
    cs_t = cum.T[0:8, :]

    e_r = lax.broadcasted_iota(i32, (128, D_SSM), 0)
    e_c = lax.broadcasted_iota(i32, (128, D_SSM), 1)
    expand = jnp.where(jnp.right_shift(e_c, 6) == e_r, 1.0, 0.0).astype(bf16)
    dt_e = _dot_exact_rhs(dt_c, expand)
    cs_e = _dot_exact_rhs(cum, expand)

    xdt = xs * dt_e
    ecs = jnp.exp(cs_e)
    cs_last = cs_e[lc - 1:lc, :]
    dec_st = jnp.exp(cs_last - cs_e)
    lane_g = lax.broadcasted_iota(i32, (1, GROUP_W), 1)
    ys = []
    for g in range(2):
        gs = slice(g * GROUP_W, (g + 1) * GROUP_W)
        bg = bm[:, g * N_STATE:(g + 1) * N_STATE]
        cg = cm[:, g * N_STATE:(g + 1) * N_STATE]
        cb = _dot_nt(cg, bg)
        xdt_g = xdt[:, gs]
        xdt_gb = xdt_g.astype(bf16)
        ms, xb = [], []
        for hh in range(4):
            h = g * 4 + hh
            lm = jnp.exp(jnp.where(tri, cum[:, h:h + 1] - cs_t[h:h + 1, :], -jnp.inf))
            ms.append((cb * lm).astype(bf16))
            xb.append(jnp.where(jnp.right_shift(lane_g, 6) == hh, xdt_gb, jnp.zeros_like(xdt_gb)))
        y_diag = _dot(jnp.concatenate(ms, axis=1), jnp.concatenate(xb, axis=0))
        st = state[g]
        y_off = _dot(cg, st.astype(bf16)) * ecs[:, gs]
        upd = _dot_tn(bg, (xdt_g * dec_st[:, gs]).astype(bf16))
        state[g] = st * jnp.exp(cs_last[:, gs]) + upd
        ys.append(y_diag + y_off + xs[:, gs] * pe_ref[2:3, gs])

    outs = []
    for g in range(2):
        gs = slice(g * GROUP_W, (g + 1) * GROUP_W)
        yg = ys[g] * _silu(z_ref[0, :, gs])
        ms_ = jnp.mean(yg * yg, axis=-1, keepdims=True)
        outs.append(yg * lax.rsqrt(ms_ + EPS))
    y_ref[0] = (jnp.concatenate(outs, axis=1) * pe_ref[3:4, :]).astype(bf16)


def _ssd(zx, bc, dtf, pc, pe):
    bsz, t, _ = zx.shape
    lc = min(SSD_CHUNK, t)
    col = lambda k: pl.BlockSpec((1, lc, 512), lambda b, j, k=k: (b, j, k))
    full = lambda shape: pl.BlockSpec(shape, lambda b, j: (0,) * len(shape))
    return pl.pallas_call(
        functools.partial(_ssd_kernel, lc=lc),
        grid=(bsz, t // lc),
        in_specs=[col(0), col(1), col(0),
                  pl.BlockSpec((1, lc, 128), lambda b, j: (b, j, 0)),
                  full((8, 128)), full((8, D_SSM))],
        out_specs=[pl.BlockSpec((1, lc, D_SSM), lambda b, j: (b, j, 0)),
                   pl.BlockSpec((1, lc, 128), lambda b, j: (b, j, 0))],
        out_shape=[jax.ShapeDtypeStruct((bsz, t, D_SSM), bf16),
                   jax.ShapeDtypeStruct((bsz, t, 128), f32)],
        scratch_shapes=[pltpu.VMEM((2, N_STATE, GROUP_W), f32),
                        pltpu.VMEM((1, 128), f32)],
        compiler_params=pltpu.CompilerParams(dimension_semantics=("parallel", "arbitrary"),
                                             vmem_limit_bytes=VMEM_LIMIT),
        name="ssd",
    )(zx, zx, bc, dtf, pc, pe)


def _attn_kernel(q_ref, k_ref, v_ref, cc_ref, psel_ref, ng_ref, o_ref, kaug, vt, acc, sc0, sc1, *, tq, t):
    i = pl.program_id(1)
    nkb = t // tq
    n_heads = D_ATT // HEAD_DIM
    lane = lax.broadcasted_iota(i32, (1, 128), 1)
    lo_half = lane < HEAD_DIM

    @pl.when(i == 0)
    def _build():
        eye = jnp.where(lax.broadcasted_iota(i32, (D_ATT, D_ATT), 0) == lax.broadcasted_iota(i32, (D_ATT, D_ATT), 1),
                        1.0, 0.0).astype(bf16)
        ones_rows = jnp.where(lax.broadcasted_iota(i32, (V_ROWS - HEAD_DIM, tq), 0) == 0, 1.0, 0.0).astype(bf16)
        for jb in range(nkb):
            rows = slice(jb * tq, (jb + 1) * tq)
            v_t = _dot_nt(eye, v_ref[0, rows, :]).astype(bf16)
            for h in range(n_heads):
                vt[jb, h * V_ROWS:h * V_ROWS + HEAD_DIM, :] = v_t[h * HEAD_DIM:(h + 1) * HEAD_DIM, :]
                vt[jb, h * V_ROWS + HEAD_DIM:(h + 1) * V_ROWS, :] = ones_rows
            pieces = jnp.concatenate(_split3(cc_ref[0, rows, :] * (-LOG2E)), axis=1)
            for p in range(n_heads // 2):
                a = _dot(pieces, psel_ref[p]).astype(bf16)
                kp = k_ref[0, rows, p * 128:(p + 1) * 128]
                kaug[2 * p, rows, :] = jnp.where(lo_half, kp, a)
                kaug[2 * p + 1, rows, :] = jnp.where(lo_half, a, kp)

    ones_hi = jnp.where((lane >= HEAD_DIM) & (lane < HEAD_DIM + 3), 1.0, 0.0).astype(bf16)
    ones_lo = jnp.where(lane < 3, 1.0, 0.0).astype(bf16)
    qa = []
    for p in range(n_heads // 2):
        qp = q_ref[0, :, p * 128:(p + 1) * 128]
        qa.append(jnp.where(lo_half, qp, ones_hi))
        qa.append(jnp.where(lo_half, ones_lo, qp))
    keep = lax.broadcasted_iota(i32, (tq, tq), 0) <= lax.broadcasted_iota(i32, (tq, tq), 1)
    acc[...] = jnp.zeros_like(acc)

    def score(jb, buf):
        k0 = pl.multiple_of(jb * tq, tq)
        for h in range(n_heads):
            buf[h] = _dot_nt(kaug[h, pl.ds(k0, tq), :], qa[h])

    def absorb(jb, ms, masked, buf):
        new_ms = []
        for h in range(n_heads):
            s = buf[h]
            if masked:
                s = jnp.where(keep, s, NEG)
            m_new = jnp.maximum(ms[h], jnp.max(s, axis=0, keepdims=True))
            alpha = jnp.exp2(ms[h] - m_new)
            p = jnp.exp2(s - m_new).astype(bf16)
            new_ms.append(m_new)
            acc[h] = acc[h] * alpha + _dot(vt[jb, h * V_ROWS:(h + 1) * V_ROWS, :], p)
        return tuple(new_ms)

    def pair(pp, ms):
        j0 = 2 * pp
        score(j0 + 1, sc1)
        ms = absorb(j0, ms, False, sc0)
        score(j0 + 2, sc0)
        return absorb(j0 + 1, ms, False, sc1)

    score(0, sc0)
    ms = lax.fori_loop(0, i // 2, pair, tuple(jnp.full((1, tq), NEG, f32) for _ in range(n_heads)))

    @pl.when(i % 2 == 0)
    def _even():
        absorb(i, ms, True, sc0)

    @pl.when(i % 2 == 1)
    def _odd():
        score(i, sc1)
        absorb(i, absorb(i - 1, ms, False, sc0), True, sc1)

    out_t = jnp.concatenate([acc[h, 0:HEAD_DIM, :] * (1.0 / acc[h, HEAD_DIM:HEAD_DIM + 1, :]) for h in range(n_heads)],
                            axis=0)
    ms_ = jnp.mean(out_t * out_t, axis=0, keepdims=True)
    out_t = out_t * lax.rsqrt(ms_ + EPS)
    o_ref[0] = (out_t.T * ng_ref[...]).astype(bf16)


def _piece_select():
    sel = np.zeros((4, 384, 128), np.float32)
    for pair in range(4):
        for j in range(3):
            sel[pair, j * 128 + 8 + 2 * pair, HEAD_DIM + j] = 1.0
            sel[pair, j * 128 + 8 + 2 * pair + 1, j] = 1.0
    return jnp.asarray(sel, bf16)


def _attn(qkv, cumc, norm_g):
    bsz, t, _ = qkv.shape
    tq = min(ATT_BLOCK, t)
    return pl.pallas_call(
        functools.partial(_attn_kernel, tq=tq, t=t),
        grid=(bsz, t // tq),
        in_specs=[pl.BlockSpec((1, tq, D_ATT), lambda b, i: (b, i, 0)),
                  pl.BlockSpec((1, t, D_ATT), lambda b, i: (b, 0, 1)),
                  pl.BlockSpec((1, t, D_ATT), lambda b, i: (b, 0, 2)),
                  pl.BlockSpec((1, t, 128), lambda b, i: (b, 0, 0)),
                  pl.BlockSpec((4, 384, 128), lambda b, i: (0, 0, 0)),
                  pl.BlockSpec((1, D_ATT), lambda b, i: (0, 0))],
        out_specs=pl.BlockSpec((1, tq, D_ATT), lambda b, i: (b, i, 0)),
        out_shape=jax.ShapeDtypeStruct((bsz, t, D_ATT), bf16),
        scratch_shapes=[pltpu.VMEM((D_ATT // HEAD_DIM, t, 128), bf16),
                        pltpu.VMEM((t // tq, (D_ATT // HEAD_DIM) * V_ROWS, tq), bf16),
                        pltpu.VMEM((D_ATT // HEAD_DIM, V_ROWS, tq), f32),
                        pltpu.VMEM((D_ATT // HEAD_DIM, tq, tq), f32),
                        pltpu.VMEM((D_ATT // HEAD_DIM, tq, tq), f32)],
        compiler_params=pltpu.CompilerParams(dimension_semantics=("parallel", "arbitrary"),
                                             vmem_limit_bytes=VMEM_LIMIT),
        name="attn",
    )(qkv, qkv, qkv, cumc, _piece_select(), norm_g)


def _layer_norm(y, g, b):
    mu = jnp.mean(y, axis=-1, keepdims=True)
    yc = y - mu
    var = jnp.mean(yc * yc, axis=-1, keepdims=True)
    return yc * lax.rsqrt(var + EPS) * g + b


def _outproj_kernel(ys_ref, ya_ref, x_ref, gt_ref, sc_ref, sh_ref, wo_ref, lng_ref, lnb_ref, wrh_ref, wrl_ref, rb_ref,
                    x1_ref, rt_ref, dest_ref, cnt_ref, tbl_ref, xrows_hbm,
                    carry, galloc, tbl, u2t, didx_v, didx_s, cnt_v, cnt_s, zeros, idx_sem, sc_sem, z_sem,
                    *, tm, blk, n_blk):
    i = pl.program_id(0)
    last = pl.num_programs(0) - 1
    cur = i % 2
    prev = 1 - cur

    def idx_ready(slot):
        return pltpu.make_async_copy(didx_v.at[slot], didx_s.at[slot], idx_sem.at[slot])

    def dispatched(slot):
        return pltpu.make_async_copy(u2t.at[slot], u2t.at[slot], sc_sem.at[slot])

    def dispatch_copy(slot, r, k):
        return pltpu.make_async_copy(u2t.at[slot, pl.ds(r * PACK_TILE, PACK_TILE)],
                                     xrows_hbm.at[pl.ds(didx_s[slot, k, r] * PACK_TILE, PACK_TILE)], sc_sem.at[slot])

    @pl.when(i == 0)
    def _init():
        carry[...] = jnp.zeros_like(carry)
        galloc[...] = jnp.zeros_like(galloc)
        tbl[...] = jnp.zeros_like(tbl)

    @pl.when(i >= 2)
    def _reuse():
        dispatched(cur).wait()
        dispatched(cur).wait()

    def step(dispatch_prev):
        n_parts = OUTPROJ_PARTS
        n_slices = 3 * n_parts + 2

        def dispatch_slice(c):
            if dispatch_prev:
                for r_ in range(c * tm // n_slices, (c + 1) * tm // n_slices):
                    dispatch_copy(prev, r_, 0).start(priority=0)
                    dispatch_copy(prev, r_, 1).start(priority=1)

        if dispatch_prev:
            idx_ready(prev).wait()
        part = tm // n_parts
        hs = []
        for a in range(n_parts):
            rs = slice(a * part, (a + 1) * part)
            dispatch_slice(a)
            hs.append(_dot(ys_ref[rs, :], wo_ref[0:D_SSM, :]) + _dot(ya_ref[rs, :], wo_ref[D_SSM:D, :]))
        logit_parts = []
        for a in range(n_parts):
            rs = slice(a * part, (a + 1) * part)
            dispatch_slice(n_parts + 2 * a)
            x1 = _layer_norm(ALPHA * x_ref[rs, :] + (1.0 + gt_ref[0, 0]) * hs[a], lng_ref[...], lnb_ref[...])
            x1_ref[rs, :] = x1
            u2 = x1 * (1.0 + sc_ref[0, 0]) + sh_ref[0, 0]
            uh = u2.astype(bf16)
            _store_packed_rows(u2t.at[cur, pl.ds(a * part * PACK_TILE, part * PACK_TILE)], _pack_rows(uh))
            dispatch_slice(n_parts + 2 * a + 1)
            ul = (u2 - uh.astype(f32)).astype(bf16)
            logit_parts.append((_dot(uh, wrh_ref[...]) + _dot(ul, wrh_ref[...])) + _dot(uh, wrl_ref[...]))
        logits = jnp.concatenate(logit_parts, axis=0) + rb_ref[...]
        lane = lax.broadcasted_iota(i32, (tm, 128), 1).astype(f32)
        big = jnp.float32(1e9)

        def first_max(vals):
            m = jnp.max(vals, axis=-1, keepdims=True)
            return m, jnp.min(jnp.where(vals == m, lane, big), axis=-1, keepdims=True)

        gl = jnp.where(lane < N_GROUPS_R, logits, NEG)
        gmax, gidx = first_max(gl)
        g_p = 1.0 / jnp.sum(jnp.exp(gl - gmax), axis=-1, keepdims=True)
        lo = N_GROUPS_R + EXPERTS_PER_GROUP * gidx
        el = jnp.where((lane >= lo) & (lane < lo + EXPERTS_PER_GROUP), logits, NEG)
        m1, i1 = first_max(el)
        el2 = jnp.where(lane == i1, NEG, el)
        m2, i2 = first_max(el2)
        r = jnp.exp(m2 - m1)
        w1 = g_p / (1.0 + r)
        w2 = g_p * r / (1.0 + r)
        dispatch_slice(3 * n_parts)

        oh1 = lane == i1
        oh2 = lane == i2
        oh = jnp.where(oh1 | oh2, 1.0, 0.0)
        r_i = lax.broadcasted_iota(i32, (tm, tm), 0)
        c_i = lax.broadcasted_iota(i32, (tm, tm), 1)
        lower = jnp.where(r_i > c_i, 1.0, 0.0).astype(bf16)
        c_old = carry[...]
        prefix = _dot(lower, oh.astype(bf16)) + c_old
        rank1 = jnp.sum(jnp.where(oh1, prefix, 0.0), axis=-1, keepdims=True)
        rank2 = jnp.sum(jnp.where(oh2, prefix, 0.0), axis=-1, keepdims=True)
        c_new = c_old + jnp.sum(oh, axis=0, keepdims=True)
        carry[...] = c_new
        dispatch_slice(3 * n_parts + 1)

        nb_old = jnp.floor((c_old + (blk - 1)) * (1.0 / blk))
        nb_new = jnp.floor((c_new + (blk - 1)) * (1.0 / blk))
        fresh = nb_new - nb_old
        sq_r = lax.broadcasted_iota(i32, (128, 128), 0)
        sq_c = lax.broadcasted_iota(i32, (128, 128), 1)
        before = jnp.where(sq_r < sq_c, 1.0, 0.0).astype(bf16)
        base = galloc[...] + _dot(jnp.broadcast_to(fresh, (8, 128)).astype(bf16), before)[0:1, :]
        galloc[...] = galloc[...] + jnp.sum(fresh, axis=-1, keepdims=True)
        ordinal = sq_r.astype(f32)
        tbl[...] = jnp.where((ordinal >= nb_old) & (ordinal < nb_new), base + (ordinal - nb_old), tbl[...])
        tbl_ref[...] = tbl[...]
        cnt_ref[...] = jnp.concatenate([c_new, galloc[...], jnp.zeros((6, 128), f32)], axis=0)

        rt = jnp.where(lane == 4, w1, jnp.where(lane == 5, w2, 0.0))
        rt_ref[...] = rt

        ids_bf = tbl[...].astype(bf16)

        def row_of(onehot, rank):
            nth = jnp.floor(rank * (1.0 / blk))
            ids = _dot(jnp.where(lane == nth, 1.0, 0.0).astype(bf16), ids_bf)
            return jnp.sum(jnp.where(onehot, ids, 0.0), axis=-1, keepdims=True) * blk + (rank - nth * blk)

        d1 = row_of(oh1, rank1)
        d2 = row_of(oh2, rank2)
        dd = jnp.where(lane == 0, d1, jnp.where(lane == 1, d2, 0.0)).T[0:8, :].astype(i32)
        dest_ref[...] = dd
        didx_v[cur] = dd
        idx_ready(cur).start()

    pl.when(i == 0)(lambda: step(False))
    pl.when(i > 0)(lambda: step(True))

    @pl.when(i == last)
    def _epilogue():
        idx_ready(cur).wait()

        def issue(r_, c):
            dispatch_copy(cur, r_, 0).start()
            dispatch_copy(cur, r_, 1).start()
            return c
        lax.fori_loop(0, tm, issue, 0)

        cnt = carry[...]
        n_blocks = jnp.floor((cnt + (blk - 1)) * (1.0 / blk))
        ordinal = lax.broadcasted_iota(i32, (128, 128), 0).astype(f32)
        last_id = jnp.sum(jnp.where(ordinal == n_blocks - 1.0, tbl[...], 0.0), axis=0, keepdims=True)
        used = cnt - (n_blocks - 1.0) * blk
        cnt_v[...] = jnp.concatenate([last_id * blk + used, blk - used, galloc[...], jnp.zeros((5, 128), f32)],
                                     axis=0).astype(i32)
        to_smem = pltpu.make_async_copy(cnt_v, cnt_s, z_sem)
        to_smem.start()
        to_smem.wait()
        zeros[...] = jnp.zeros_like(zeros)
        sizes = [1 << b for b in reversed(range(blk.bit_length() - 1))]

        def for_each_fill(fn):
            def tail(e, c):
                pad = cnt_s[1, N_GROUPS_R + e]
                off = cnt_s[0, N_GROUPS_R + e]
                for sz in sizes:
                    @pl.when((pad & sz) != 0)
                    def _(off=off, sz=sz):
                        fn(pltpu.make_async_copy(zeros.at[pl.ds(0, sz * PACK_TILE)],
                                                 xrows_hbm.at[pl.ds(off * PACK_TILE, sz * PACK_TILE)], z_sem))
                    off = off + (pad & sz)
                return c
            lax.fori_loop(0, N_EXPERTS, tail, 0)

            def whole(b, c):
                fn(pltpu.make_async_copy(zeros, xrows_hbm.at[pl.ds(b * blk * PACK_TILE, blk * PACK_TILE)], z_sem))
                return c
            lax.fori_loop(cnt_s[2, 0], n_blk, whole, 0)

        for_each_fill(lambda copy: copy.start())
        for_each_fill(lambda copy: copy.wait())

        @pl.when(i > 0)
        def _prev_done():
            dispatched(prev).wait()
            dispatched(prev).wait()
        dispatched(cur).wait()
        dispatched(cur).wait()


def _outproj(y_ssm, y_att, x, mod4, w_out, ln_g, ln_b, wr_hi, wr_lo, rb, t, blk, n_blk):
    n = x.shape[0]
    tm = min(OUTPROJ_ROWS, t)
    nt = t // tm
    assert n // blk + 1 <= 128 and n_blk <= 256, "block-id table: 128 blocks per expert, ids exact in bf16"
    vec = lambda k: pl.BlockSpec((1, 1, 1, D), lambda i, k=k: (k, i // nt, 0, 0))
    full = lambda shape: pl.BlockSpec(shape, lambda i: (0,) * len(shape))
    rows = lambda w: pl.BlockSpec((tm, w), lambda i: (i, 0))
    return pl.pallas_call(
        functools.partial(_outproj_kernel, tm=tm, blk=blk, n_blk=n_blk),
        grid=(n // tm,),
        in_specs=[rows(D_SSM), rows(D_ATT), rows(D), vec(2), vec(4), vec(3),
                  full((D, D)), full((1, D)), full((1, D)), full((D, 128)), full((D, 128)), full((1, 128))],
        out_specs=[rows(D), rows(128), pl.BlockSpec((8, tm), lambda i: (0, i)), full((8, 128)), full((128, 128)),
                   pl.BlockSpec(memory_space=pl.ANY)],
        out_shape=[jax.ShapeDtypeStruct((n, D), f32), jax.ShapeDtypeStruct((n, 128), f32),
                   jax.ShapeDtypeStruct((8, n), i32), jax.ShapeDtypeStruct((8, 128), f32),
                   jax.ShapeDtypeStruct((128, 128), f32),
                   jax.ShapeDtypeStruct((n_blk * blk * PACK_TILE, 128), u32)],
        scratch_shapes=[pltpu.VMEM((1, 128), f32), pltpu.VMEM((1, 128), f32), pltpu.VMEM((128, 128), f32),
                        pltpu.VMEM((2, tm * PACK_TILE, 128), u32),
                        pltpu.VMEM((2, 8, tm), i32), pltpu.SMEM((2, 8, tm), i32),
                        pltpu.VMEM((8, 128), i32), pltpu.SMEM((8, 128), i32),
                        pltpu.VMEM((blk * PACK_TILE, 128), u32),
                        pltpu.SemaphoreType.DMA((2,)), pltpu.SemaphoreType.DMA((2,)), pltpu.SemaphoreType.DMA(())],
        compiler_params=pltpu.CompilerParams(dimension_semantics=("arbitrary",), vmem_limit_bytes=VMEM_LIMIT),
        name="outproj",
    )(y_ssm, y_att, x, mod4, mod4, mod4, w_out, ln_g, ln_b, wr_hi, wr_lo, rb)


def _tile_copy(src_hbm, row, buf, slot, sem):
    return pltpu.make_async_copy(src_hbm.at[row], buf.at[pl.ds(slot * ROW_TILE, ROW_TILE)], sem)


def _moe_kernel(be_ref, br_ref, nu_ref, nxt_ref, par_ref, x_ref, wg_hbm, wu_hbm, wd_hbm, y_ref,
                wfg, wfu, wfd, wgb, wub, wdb, wsem, *, blk):
    i = pl.program_id(0)
    used = i < nu_ref[0]

    def fetch(e, slot):
        return [pltpu.make_async_copy(src.at[e], dst.at[slot], wsem.at[slot])
                for src, dst in ((wg_hbm, wfg), (wu_hbm, wfu), (wd_hbm, wfd))]

    @pl.when(i == 0)
    def _first():
        for c in fetch(be_ref[0], par_ref[0]):
            c.start()

    @pl.when(jnp.logical_and(used, jnp.logical_or(i == 0, be_ref[i] != be_ref[jnp.maximum(i - 1, 0)])))
    def _switch():
        slot = par_ref[i]
        for c in fetch(be_ref[i], slot):
            c.wait()

        @pl.when(nxt_ref[i] >= 0)
        def _next():
            for c in fetch(nxt_ref[i], 1 - slot):
                c.start(priority=1)
        wgb[...] = wfg[slot].astype(bf16)
        wub[...] = wfu[slot].astype(bf16)
        wdb[...] = wfd[slot].astype(bf16)

    @pl.when(jnp.logical_not(used))
    def _spare():
        y_ref[...] = jnp.zeros_like(y_ref)

    @pl.when(used)
    def _work():
        xa, xb = _load_packed_rows(x_ref, blk)
        half = D // 2
        gate = _dot(xa, wgb[0:half, :]) + _dot(xb, wgb[half:D, :])
        up = _dot(xa, wub[0:half, :]) + _dot(xb, wub[half:D, :])
        hid = (_silu(gate) * up).astype(bf16)
        _store_row_tiles(y_ref, _dot(hid, wdb[...]))


def _moe(block_e, block_row, n_used, next_e, parity, x_rows, w_gate, w_up, w_down, blk):
    nblk = block_e.shape[0]
    rspec = pl.BlockSpec((blk * ROW_TILE, 128), lambda i, be, br, *_: (br[i], 0))
    xspec = pl.BlockSpec((blk * PACK_TILE, 128), lambda i, be, br, nu, *_: (br[jnp.minimum(i, nu[0] - 1)], 0))
    anyspec = pl.BlockSpec(memory_space=pl.ANY)
    grid_spec = pltpu.PrefetchScalarGridSpec(
        num_scalar_prefetch=5,
        grid=(nblk,),
        in_specs=[xspec, anyspec, anyspec, anyspec],
        out_specs=rspec,
        scratch_shapes=[pltpu.VMEM((2, D, D_EXPERT), f32), pltpu.VMEM((2, D, D_EXPERT), f32),
                        pltpu.VMEM((2, D_EXPERT, D), f32),
                        pltpu.VMEM((D, D_EXPERT), bf16), pltpu.VMEM((D, D_EXPERT), bf16),
                        pltpu.VMEM((D_EXPERT, D), bf16),
                        pltpu.SemaphoreType.DMA((2,))],
    )
    return pl.pallas_call(
        functools.partial(_moe_kernel, blk=blk),
        grid_spec=grid_spec,
        out_shape=jax.ShapeDtypeStruct((nblk * blk * ROW_TILE, 128), f32),
        compiler_params=pltpu.CompilerParams(dimension_semantics=("arbitrary",), vmem_limit_bytes=VMEM_LIMIT),
        name="moe",
    )(block_e, block_row, n_used, next_e, parity, x_rows, w_gate, w_up, w_down)


def _combine_kernel(d1_ref, d2_ref, y_hbm, x1_ref, rt_ref, gt_ref, lng_ref, lnb_ref, o_ref, buf, sem, *, tm):
    i = pl.program_id(0)
    last = pl.num_programs(0) - 1

    @pl.when(i == 0)
    def _prologue():
        for s in range(2):
            def issue(r, c, s=s):
                base = jnp.minimum(s, last) * tm
                _tile_copy(y_hbm, d1_ref[base + r], buf.at[s, 0], r, sem.at[s]).start()
                _tile_copy(y_hbm, d2_ref[base + r], buf.at[s, 1], r, sem.at[s]).start()
                return c
            lax.fori_loop(0, tm, issue, 0)

    def rows_ready(slot):
        return pltpu.make_async_copy(buf.at[slot], buf.at[slot], sem.at[slot])

    slot = i % 3
    nxt = (i + 2) % 3
    base = jnp.minimum(i + 2, last) * tm
    rows_ready(slot).wait()
    for r in range(tm):
        _tile_copy(y_hbm, d1_ref[base + r], buf.at[nxt, 0], r, sem.at[nxt]).start(priority=0)
        _tile_copy(y_hbm, d2_ref[base + r], buf.at[nxt, 1], r, sem.at[nxt]).start(priority=1)
    rt = rt_ref[...]
    moe = rt[:, 4:5] * _load_row_tiles(buf.at[slot, 0], tm) + rt[:, 5:6] * _load_row_tiles(buf.at[slot, 1], tm)
    y = ALPHA * x1_ref[...] + (1.0 + gt_ref[0, 0]) * moe
    o_ref[...] = _layer_norm(y, lng_ref[...], lnb_ref[...])

    @pl.when(i == last)
    def _drain():
        rows_ready((i + 1) % 3).wait()
        rows_ready((i + 2) % 3).wait()


def _combine(dest1, dest2, y_rows, x1, rt, mod4, ln_g, ln_b, t):
    n = x1.shape[0]
    tm = min(COMBINE_ROWS, t)
    nt = t // tm
    full = lambda shape: pl.BlockSpec(shape, lambda i, d1, d2: (0,) * len(shape))
    rows = lambda w: pl.BlockSpec((tm, w), lambda i, d1, d2: (i, 0))
    grid_spec = pltpu.PrefetchScalarGridSpec(
        num_scalar_prefetch=2,
        grid=(n // tm,),
        in_specs=[pl.BlockSpec(memory_space=pl.ANY), rows(D), rows(128),
                  pl.BlockSpec((1, 1, 1, D), lambda i, d1, d2: (5, i // nt, 0, 0)),
                  full((1, D)), full((1, D))],
        out_specs=rows(D),
        scratch_shapes=[pltpu.VMEM((3, 2, tm * ROW_TILE, 128), f32), pltpu.SemaphoreType.DMA((3,))],
    )
    return pl.pallas_call(
        functools.partial(_combine_kernel, tm=tm),
        grid_spec=grid_spec,
        out_shape=jax.ShapeDtypeStruct((n, D), f32),
        compiler_params=pltpu.CompilerParams(dimension_semantics=("arbitrary",), vmem_limit_bytes=VMEM_LIMIT),
        name="combine",
    )(dest1, dest2, y_rows, x1, rt, mod4, ln_g, ln_b)


def _hi_lo(w):
    hi = w.astype(bf16)
    return hi, (w - hi.astype(f32)).astype(bf16)


def _pad_lanes(v, offset, width):
    return jnp.zeros((width,), f32).at[offset:offset + v.shape[0]].set(v)


def _layer(x, c, ada_w, ada_b, w_in, conv_w, conv_b, dt_bias, a_log, d_skip, ssm_norm_g, fg_bias, att_norm_g,
           w_out, ln1_g, ln1_b, router_g_w, router_g_b, router_e_w, router_e_b, w_gate, w_up, w_down, ln2_g, ln2_b):
    bsz, t, _ = x.shape
    n = bsz * t

    mod4 = _ada(c, ada_w, ada_b).reshape(6, bsz, 1, D)

    w_main, w_small = _pack_w_in(w_in)
    zx, bc, qkv, dtf = _inproj(x, mod4, w_main, w_small, conv_w, conv_b.reshape(1, -1))

    pc = jnp.stack([_pad_lanes(dt_bias, 0, 128), _pad_lanes(a_log, 0, 128), _pad_lanes(fg_bias, 8, 128)]
                   + [jnp.zeros((128,), f32)] * 5)
    rep = lambda v: jnp.repeat(v, HEAD_DIM)
    pe = jnp.stack([rep(dt_bias), rep(a_log), rep(d_skip), ssm_norm_g] + [jnp.zeros((D_SSM,), f32)] * 4)
    y_ssm, cumc = _ssd(zx, bc, dtf, pc, pe)

    y_att = _attn(qkv, cumc, att_norm_g.reshape(1, -1))

    wr = jnp.concatenate([router_g_w, router_e_w, jnp.zeros((D, 128 - N_GROUPS_R - N_EXPERTS), f32)], axis=1)
    wr_hi, wr_lo = _hi_lo(wr)
    rb = jnp.concatenate([router_g_b, router_e_b, jnp.zeros((128 - N_GROUPS_R - N_EXPERTS,), f32)]).reshape(1, 128)
    blk = MOE_ROWS
    nblk = (2 * n) // blk + N_EXPERTS
    x1, rt, dest, cnt, tbl, x_rows = _outproj(y_ssm.reshape(n, D_SSM), y_att.reshape(n, D_ATT), x.reshape(n, D), mod4,
                                              w_out.astype(bf16), ln1_g.reshape(1, D), ln1_b.reshape(1, D),
                                              wr_hi, wr_lo, rb, t, blk, nblk)

    counts = cnt[0, N_GROUPS_R:N_GROUPS_R + N_EXPERTS].astype(i32)
    nb = (counts + blk - 1) // blk
    nb_end = jnp.cumsum(nb)
    n_used = nb_end[N_EXPERTS - 1:]
    every = jnp.arange(nblk, dtype=i32)
    step = jnp.minimum(every, n_used[0] - 1)
    owner = (nb_end[None, :] <= step[:, None]).astype(i32)
    block_e = jnp.sum(owner, axis=1)
    first = jnp.sum(owner * nb[None, :], axis=1)
    lanes = jnp.arange(128, dtype=i32)[None, :]
    nth_row = jnp.dot((lanes == (step - first)[:, None]).astype(f32), tbl, precision=lax.Precision.HIGHEST)
    block_id = jnp.sum(jnp.where(lanes == (block_e + N_GROUPS_R)[:, None], nth_row, 0.0), axis=1).astype(i32)
    block_row = jnp.where(every < n_used[0], block_id, every)
    run_end = first + jnp.sum((jnp.arange(N_EXPERTS)[None, :] == block_e[:, None]) * nb[None, :], axis=1)
    next_e = jnp.where(run_end < n_used[0], jnp.sum((nb_end[None, :] <= run_end[:, None]).astype(i32), axis=1), -1)
    parity = jnp.sum(((jnp.arange(N_EXPERTS)[None, :] < block_e[:, None]) & (nb[None, :] > 0)).astype(i32), axis=1) % 2

    y_rows = _moe(block_e, block_row, n_used, next_e, parity, x_rows,
                  w_gate, w_up, w_down, blk)
    out = _combine(dest[0], dest[1], y_rows.reshape(nblk * blk, ROW_TILE, 128), x1, rt, mod4,
                   ln2_g.reshape(1, D), ln2_b.reshape(1, D), t)
    return out.reshape(bsz, t, D)


def kernel(x, c, ada_w, ada_b, w_in, conv_w, conv_b, dt_bias, a_log, d_skip, ssm_norm_g, fg_bias, att_norm_g, w_out,
           ln1_g, ln1_b, router_g_w, router_g_b, router_e_w, router_e_b, w_gate, w_up, w_down, ln2_g, ln2_b):
    depth = ada_w.shape[0]
    for l in range(depth):
        x = _layer(x, c, ada_w[l], ada_b[l], w_in[l], conv_w[l], conv_b[l], dt_bias[l], a_log[l], d_skip[l],
                   ssm_norm_g[l], fg_bias[l], att_norm_g[l], w_out[l], ln1_g[l], ln1_b[l], router_g_w[l],
                   router_g_b[l], router_e_w[l], router_e_b[l], w_gate[l], w_up[l], w_down[l], ln2_g[l], ln2_b[l])
    return x
```

```python
import functools

import jax
import jax.numpy as jnp
import numpy as np
from jax import lax
from jax.experimental import pallas as pl
from jax.experimental.pallas import tpu as pltpu

f32 = jnp.float32
bf16 = jnp.bfloat16
i32 = jnp.int32

D = 1024
D_SSM = 512
D_ATT = 512
HEAD_DIM = 64
GROUP_W = 256
N_STATE = 128
CONV_K = 4
N_GROUPS_R = 4
EXPERTS_PER_GROUP = 8
N_EXPERTS = 32
D_EXPERT = 512
ALPHA = 2.0 ** 0.25
EPS = 1e-5
NEG = -1e30
LOG2E = 1.4426950408889634
QK_SCALE = HEAD_DIM ** -0.5 * LOG2E
V_ROWS = 80

SSD_CHUNK = 256
ATT_BLOCK = 256
INPROJ_ROWS = 512
OUTPROJ_ROWS = 512
OUTPROJ_PARTS = 2
MOE_ROWS = 512
COMBINE_ROWS = 256
DEST_ROWS = 2048
VMEM_LIMIT = 48 * 1024 * 1024


def _dot(a, b):
    return jnp.dot(a, b, preferred_element_type=f32)


def _dot_nt(a, b):
    return lax.dot_general(a, b, (((1,), (1,)), ((), ())), preferred_element_type=f32)


def _dot_tn(a, b):
    return lax.dot_general(a, b, (((0,), (0,)), ((), ())), preferred_element_type=f32)


def _split3(v):
    hi = v.astype(bf16)
    r1 = v - hi.astype(f32)
    mid = r1.astype(bf16)
    lo = (r1 - mid.astype(f32)).astype(bf16)
    return hi, mid, lo


def _dot_exact_lhs(m, v):
    hi, mid, lo = _split3(v)
    return (_dot(m, hi) + _dot(m, mid)) + _dot(m, lo)


def _dot_exact_rhs(v, m):
    hi, mid, lo = _split3(v)
    return (_dot(hi, m) + _dot(mid, m)) + _dot(lo, m)


ROW_TILE = 8


def _store_row_tiles(ref, val):
    rows = val.shape[0]
    for c in range(ROW_TILE):
        ref[pl.ds(c, rows, stride=ROW_TILE), :] = val[:, c * 128:(c + 1) * 128]


def _load_row_tiles(ref, rows):
    return jnp.concatenate([ref[pl.ds(c, rows, stride=ROW_TILE), :] for c in range(ROW_TILE)], axis=1)


PACK_TILE = 4
u32 = jnp.uint32


def _pack_rows(vb):
    lo = pltpu.bitcast(vb[:, 0:D // 2].astype(f32), u32) >> 16
    hi = pltpu.bitcast(vb[:, D // 2:D].astype(f32), u32) & jnp.uint32(0xFFFF0000)
    return lo | hi


def _store_packed_rows(ref, packed):
    rows = packed.shape[0]
    for c in range(PACK_TILE):
        ref[pl.ds(c, rows, stride=PACK_TILE), :] = packed[:, c * 128:(c + 1) * 128]


def _load_packed_rows(ref, rows):
    words = [ref[pl.ds(c, rows, stride=PACK_TILE), :] for c in range(PACK_TILE)]
    lo = jnp.concatenate([pltpu.bitcast(w << 16, f32) for w in words], axis=1)
    hi = jnp.concatenate([pltpu.bitcast(w & jnp.uint32(0xFFFF0000), f32) for w in words], axis=1)
    return lo.astype(bf16), hi.astype(bf16)


def _softplus(x):
    return jnp.maximum(x, 0.0) + jnp.log1p(jnp.exp(-jnp.abs(x)))


def _silu(x):
    return x * jax.nn.sigmoid(x)


def _ada_kernel(c_ref, w_ref, b_ref, o_ref):
    s = _silu(c_ref[...]).astype(bf16)
    o_ref[0] = _dot(s, w_ref[...].astype(bf16)) + b_ref[0]


def _ada(c, w, b):
    bsz = c.shape[0]
    return pl.pallas_call(
        _ada_kernel,
        grid=(6,),
        in_specs=[pl.BlockSpec((bsz, D), lambda j: (0, 0)),
                  pl.BlockSpec((D, D), lambda j: (0, j)),
                  pl.BlockSpec((1, 1, D), lambda j: (j, 0, 0))],
        out_specs=pl.BlockSpec((1, bsz, D), lambda j: (j, 0, 0)),
        out_shape=jax.ShapeDtypeStruct((6, bsz, D), f32),
        compiler_params=pltpu.CompilerParams(dimension_semantics=("arbitrary",), vmem_limit_bytes=VMEM_LIMIT),
        name="ada",
    )(c, w, b.reshape(6, 1, D))


W_IN_COLS = 3088
PACK_ROWS = 256


def _pack_w_in_kernel(wt_ref, wm_ref, ws_ref):
    for j in range(3072 // PACK_ROWS):
        src = j * PACK_ROWS if j * PACK_ROWS < 1536 else j * PACK_ROWS + 8
        wm_ref[:, j * PACK_ROWS:(j + 1) * PACK_ROWS] = wt_ref[src:src + PACK_ROWS, :].T.astype(bf16)
    small = jnp.concatenate([wt_ref[1536:1544, :], wt_ref[W_IN_COLS - 8:W_IN_COLS, :], jnp.zeros((112, D), f32)],
                            axis=0).T
    hi = small.astype(bf16)
    ws_ref[:, 0:128] = hi
    ws_ref[:, 128:256] = (small - hi.astype(f32)).astype(bf16)


def _pack_w_in(w_in):
    full = lambda shape: pl.BlockSpec(shape, lambda i: (0,) * len(shape))
    return pl.pallas_call(
        _pack_w_in_kernel,
        grid=(1,),
        in_specs=[full((W_IN_COLS, D))],
        out_specs=[full((D, 3072)), full((D, 256))],
        out_shape=[jax.ShapeDtypeStruct((D, 3072), bf16), jax.ShapeDtypeStruct((D, 256), bf16)],
        compiler_params=pltpu.CompilerParams(dimension_semantics=("arbitrary",), vmem_limit_bytes=VMEM_LIMIT),
        name="pack_w_in",
    )(w_in.T)


def _inproj_kernel(x_ref, sc_ref, sh_ref, wm_ref, ws_ref, cw_ref, cb_ref, zx_ref, bc_ref, qkv_ref, dtf_ref, xcat, *, tm):
    i = pl.program_id(1)

    @pl.when(i == 0)
    def _init():
        xcat[0:8, :] = jnp.zeros((8, 2 * D_SSM), f32)

    u = x_ref[0] * (1.0 + sc_ref[0, 0]) + sh_ref[0, 0]
    ub = u.astype(bf16)
    xcat[8:8 + tm, 0:512] = _dot(ub, wm_ref[:, 512:1024])
    xcat[8:8 + tm, 512:1024] = _dot(ub, wm_ref[:, 1024:1536])
    zx_ref[0, :, 0:512] = _dot(ub, wm_ref[:, 0:512])
    acc = cw_ref[0:1, :] * xcat[5:5 + tm, :] + cb_ref[...]
    for k in range(1, CONV_K):
        acc = acc + cw_ref[k:k + 1, :] * xcat[5 + k:5 + k + tm, :]
    xcat[0:8, :] = xcat[tm:tm + 8, :]
    xbc = _silu(acc)
    zx_ref[0, :, 512:1024] = xbc[:, 0:512]
    bc_ref[0] = xbc[:, 512:1024].astype(bf16)
    for j, scale in enumerate((QK_SCALE, 1.0, 1.0)):
        qkv_ref[0, :, j * 512:(j + 1) * 512] = (
            _dot(ub, wm_ref[:, 1536 + j * 512:1536 + (j + 1) * 512]) * scale).astype(bf16)
    ul = (u - ub.astype(f32)).astype(bf16)
    d_hl = _dot(ub, ws_ref[...])
    dtf_ref[0] = (d_hl[:, 0:128] + _dot(ul, ws_ref[:, 0:128])) + d_hl[:, 128:256]


def _inproj(x, mod4, w_main, ws, conv_w, conv_b):
    bsz, t, _ = x.shape
    tm = min(INPROJ_ROWS, t)
    vec = lambda k: pl.BlockSpec((1, 1, 1, D), lambda b, i, k=k: (k, b, 0, 0))
    full = lambda shape: pl.BlockSpec(shape, lambda b, i: (0,) * len(shape))
    rows = lambda w: pl.BlockSpec((1, tm, w), lambda b, i: (b, i, 0))
    return pl.pallas_call(
        functools.partial(_inproj_kernel, tm=tm),
        grid=(bsz, t // tm),
        in_specs=[rows(D), vec(1), vec(0), full((D, 3072)), full((D, 256)),
                  full((CONV_K, 2 * D_SSM)), full((1, 2 * D_SSM))],
        out_specs=[rows(1024), rows(512), rows(1536), rows(128)],
        out_shape=[jax.ShapeDtypeStruct((bsz, t, 1024), f32),
                   jax.ShapeDtypeStruct((bsz, t, 512), bf16),
                   jax.ShapeDtypeStruct((bsz, t, 1536), bf16),
                   jax.ShapeDtypeStruct((bsz, t, 128), f32)],
        scratch_shapes=[pltpu.VMEM((tm + 8, 2 * D_SSM), f32)],
        compiler_params=pltpu.CompilerParams(dimension_semantics=("parallel", "arbitrary"),
                                             vmem_limit_bytes=VMEM_LIMIT),
        name="inproj",
    )(x, mod4, mod4, w_main, ws, conv_w, conv_b)


def _ssd_kernel(z_ref, xs_ref, bc_ref, dtf_ref, pc_ref, pe_ref, y_ref, cumc_ref, state, carry, *, lc):
    j = pl.program_id(1)

    @pl.when(j == 0)
    def _init():
        state[...] = jnp.zeros_like(state)
        carry[...] = jnp.zeros_like(carry)

    xs = xs_ref[0]
    bm = bc_ref[0, :, 0:256]
    cm = bc_ref[0, :, 256:512]

    dtf = dtf_ref[0]
    lane = lax.broadcasted_iota(i32, (lc, 128), 1)
    dt_c = _softplus(dtf + pc_ref[0:1, :])
    a_c = dt_c * (-jnp.exp(pc_ref[1:2, :]))
    logf = -_softplus(-(dtf + pc_ref[2:3, :]))
    v = jnp.where(lane < 8, a_c, logf)
    r_i = lax.broadcasted_iota(i32, (lc, lc), 0)
    c_i = lax.broadcasted_iota(i32, (lc, lc), 1)
    tri = r_i >= c_i
    tri_b = jnp.where(tri, 1.0, 0.0).astype(bf16)
    cum = _dot_exact_lhs(tri_b, v) + carry[...]
    carry[...] = jnp.where(lane[0:1, :] >= 8, cum[lc - 1:lc, :], 0.0)
    cumc_ref[0] = cum
    cs_t = cum.T[0:8, :]

    e_r = lax.broadcasted_iota(i32, (128, D_SSM), 0)
    e_c = lax.broadcasted_iota(i32, (128, D_SSM), 1)
    expand = jnp.where(jnp.right_shift(e_c, 6) == e_r, 1.0, 0.0).astype(bf16)
    dt_e = _dot_exact_rhs(dt_c, expand)
    cs_e = _dot_exact_rhs(cum, expand)

    xdt = xs * dt_e
    ecs = jnp.exp(cs_e)
    cs_last = cs_e[lc - 1:lc, :]
    dec_st = jnp.exp(cs_last - cs_e)
    lane_g = lax.broadcasted_iota(i32, (1, GROUP_W), 1)
    ys = []
    for g in range(2):
        gs = slice(g * GROUP_W, (g + 1) * GROUP_W)
        bg = bm[:, g * N_STATE:(g + 1) * N_STATE]
        cg = cm[:, g * N_STATE:(g + 1) * N_STATE]
        cb = _dot_nt(cg, bg)
        xdt_g = xdt[:, gs]
        xdt_gb = xdt_g.astype(bf16)
        ms, xb = [], []
        for hh in range(4):
            h = g * 4 + hh
            lm = jnp.exp(jnp.where(tri, cum[:, h:h + 1] - cs_t[h:h + 1, :], -jnp.inf))
            ms.append((cb * lm).astype(bf16))
            xb.append(jnp.where(jnp.right_shift(lane_g, 6) == hh, xdt_gb, jnp.zeros_like(xdt_gb)))
        y_diag = _dot(jnp.concatenate(ms, axis=1), jnp.concatenate(xb, axis=0))
        st = state[g]
        y_off = _dot(cg, st.astype(bf16)) * ecs[:, gs]
        upd = _dot_tn(bg, (xdt_g * dec_st[:, gs]).astype(bf16))
        state[g] = st * jnp.exp(cs_last[:, gs]) + upd
        ys.append(y_diag + y_off + xs[:, gs] * pe_ref[2:3, gs])

    outs = []
    for g in range(2):
        gs = slice(g * GROUP_W, (g + 1) * GROUP_W)
        yg = ys[g] * _silu(z_ref[0, :, gs])
        ms_ = jnp.mean(yg * yg, axis=-1, keepdims=True)
        outs.append(yg * lax.rsqrt(ms_ + EPS))
    y_ref[0] = (jnp.concatenate(outs, axis=1) * pe_ref[3:4, :]).astype(bf16)


def _ssd(zx, bc, dtf, pc, pe):
    bsz, t, _ = zx.shape
    lc = min(SSD_CHUNK, t)
    col = lambda k: pl.BlockSpec((1, lc, 512), lambda b, j, k=k: (b, j, k))
    full = lambda shape: pl.BlockSpec(shape, lambda b, j: (0,) * len(shape))
    return pl.pallas_call(
        functools.partial(_ssd_kernel, lc=lc),
        grid=(bsz, t // lc),
        in_specs=[col(0), col(1), col(0),
                  pl.BlockSpec((1, lc, 128), lambda b, j: (b, j, 0)),
                  full((8, 128)), full((8, D_SSM))],
        out_specs=[pl.BlockSpec((1, lc, D_SSM), lambda b, j: (b, j, 0)),
                   pl.BlockSpec((1, lc, 128), lambda b, j: (b, j, 0))],
        out_shape=[jax.ShapeDtypeStruct((bsz, t, D_SSM), bf16),
                   jax.ShapeDtypeStruct((bsz, t, 128), f32)],
        scratch_shapes=[pltpu.VMEM((2, N_STATE, GROUP_W), f32),
                        pltpu.VMEM((1, 128), f32)],
        compiler_params=pltpu.CompilerParams(dimension_semantics=("parallel", "arbitrary"),
                                             vmem_limit_bytes=VMEM_LIMIT),
        name="ssd",
    )(zx, zx, bc, dtf, pc, pe)


def _attn_kernel(q_ref, k_ref, v_ref, cc_ref, psel_ref, ng_ref, o_ref, kaug, vt, acc, sc0, sc1, *, tq, t):
    i = pl.program_id(1)
    nkb = t // tq
    n_heads = D_ATT // HEAD_DIM
    lane = lax.broadcasted_iota(i32, (1, 128), 1)
    lo_half = lane < HEAD_DIM

    @pl.when(i == 0)
    def _build():
        eye = jnp.where(lax.broadcasted_iota(i32, (D_ATT, D_ATT), 0) == lax.broadcasted_iota(i32, (D_ATT, D_ATT), 1),
                        1.0, 0.0).astype(bf16)
        ones_rows = jnp.where(lax.broadcasted_iota(i32, (V_ROWS - HEAD_DIM, tq), 0) == 0, 1.0, 0.0).astype(bf16)
        for jb in range(nkb):
            rows = slice(jb * tq, (jb + 1) * tq)
            v_t = _dot_nt(eye, v_ref[0, rows, :]).astype(bf16)
            for h in range(n_heads):
                vt[jb, h * V_ROWS:h * V_ROWS + HEAD_DIM, :] = v_t[h * HEAD_DIM:(h + 1) * HEAD_DIM, :]
                vt[jb, h * V_ROWS + HEAD_DIM:(h + 1) * V_ROWS, :] = ones_rows
            pieces = jnp.concatenate(_split3(cc_ref[0, rows, :] * (-LOG2E)), axis=1)
            for p in range(n_heads // 2):
                a = _dot(pieces, psel_ref[p]).astype(bf16)
                kp = k_ref[0, rows, p * 128:(p + 1) * 128]
                kaug[2 * p, rows, :] = jnp.where(lo_half, kp, a)
                kaug[2 * p + 1, rows, :] = jnp.where(lo_half, a, kp)

    ones_hi = jnp.where((lane >= HEAD_DIM) & (lane < HEAD_DIM + 3), 1.0, 0.0).astype(bf16)
    ones_lo = jnp.where(lane < 3, 1.0, 0.0).astype(bf16)
    qa = []
    for p in range(n_heads // 2):
        qp = q_ref[0, :, p * 128:(p + 1) * 128]
        qa.append(jnp.where(lo_half, qp, ones_hi))
        qa.append(jnp.where(lo_half, ones_lo, qp))
    keep = lax.broadcasted_iota(i32, (tq, tq), 0) <= lax.broadcasted_iota(i32, (tq, tq), 1)
    acc[...] = jnp.zeros_like(acc)

    def score(jb, buf):
        k0 = pl.multiple_of(jb * tq, tq)
        for h in range(n_heads):
            buf[h] = _dot_nt(kaug[h, pl.ds(k0, tq), :], qa[h])

    def absorb(jb, ms, masked, buf):
        new_ms = []
        for h in range(n_heads):
            s = buf[h]
            if masked:
                s = jnp.where(keep, s, NEG)
            m_new = jnp.maximum(ms[h], jnp.max(s, axis=0, keepdims=True))
            alpha = jnp.exp2(ms[h] - m_new)
            p = jnp.exp2(s - m_new).astype(bf16)
            new_ms.append(m_new)
            acc[h] = acc[h] * alpha + _dot(vt[jb, h * V_ROWS:(h + 1) * V_ROWS, :], p)
        return tuple(new_ms)

    def pair(pp, ms):
        j0 = 2 * pp
        score(j0 + 1, sc1)
        ms = absorb(j0, ms, False, sc0)
        score(j0 + 2, sc0)
        return absorb(j0 + 1, ms, False, sc1)

    score(0, sc0)
    ms = lax.fori_loop(0, i // 2, pair, tuple(jnp.full((1, tq), NEG, f32) for _ in range(n_heads)))

    @pl.when(i % 2 == 0)
    def _even():
        absorb(i, ms, True, sc0)

    @pl.when(i % 2 == 1)
    def _odd():
        score(i, sc1)
        absorb(i, absorb(i - 1, ms, False, sc0), True, sc1)

    out_t = jnp.concatenate([acc[h, 0:HEAD_DIM, :] * (1.0 / acc[h, HEAD_DIM:HEAD_DIM + 1, :]) for h in range(n_heads)],
                            axis=0)
    ms_ = jnp.mean(out_t * out_t, axis=0, keepdims=True)
    out_t = out_t * lax.rsqrt(ms_ + EPS)
    o_ref[0] = (out_t.T * ng_ref[...]).astype(bf16)


def _piece_select():
    sel = np.zeros((4, 384, 128), np.float32)
    for pair in range(4):
        for j in range(3):
            sel[pair, j * 128 + 8 + 2 * pair, HEAD_DIM + j] = 1.0
            sel[pair, j * 128 + 8 + 2 * pair + 1, j] = 1.0
    return jnp.asarray(sel, bf16)


def _attn(qkv, cumc, norm_g):
    bsz, t, _ = qkv.shape
    tq = min(ATT_BLOCK, t)
    return pl.pallas_call(
        functools.partial(_attn_kernel, tq=tq, t=t),
        grid=(bsz, t // tq),
        in_specs=[pl.BlockSpec((1, tq, D_ATT), lambda b, i: (b, i, 0)),
                  pl.BlockSpec((1, t, D_ATT), lambda b, i: (b, 0, 1)),
                  pl.BlockSpec((1, t, D_ATT), lambda b, i: (b, 0, 2)),
                  pl.BlockSpec((1, t, 128), lambda b, i: (b, 0, 0)),
                  pl.BlockSpec((4, 384, 128), lambda b, i: (0, 0, 0)),
                  pl.BlockSpec((1, D_ATT), lambda b, i: (0, 0))],
        out_specs=pl.BlockSpec((1, tq, D_ATT), lambda b, i: (b, i, 0)),
        out_shape=jax.ShapeDtypeStruct((bsz, t, D_ATT), bf16),
        scratch_shapes=[pltpu.VMEM((D_ATT // HEAD_DIM, t, 128), bf16),
                        pltpu.VMEM((t // tq, (D_ATT // HEAD_DIM) * V_ROWS, tq), bf16),
                        pltpu.VMEM((D_ATT // HEAD_DIM, V_ROWS, tq), f32),
                        pltpu.VMEM((D_ATT // HEAD_DIM, tq, tq), f32),
                        pltpu.VMEM((D_ATT // HEAD_DIM, tq, tq), f32)],
        compiler_params=pltpu.CompilerParams(dimension_semantics=("parallel", "arbitrary"),
                                             vmem_limit_bytes=VMEM_LIMIT),
        name="attn",
    )(qkv, qkv, qkv, cumc, _piece_select(), norm_g)


def _layer_norm(y, g, b):
    mu = jnp.mean(y, axis=-1, keepdims=True)
    yc = y - mu
    var = jnp.mean(yc * yc, axis=-1, keepdims=True)
    return yc * lax.rsqrt(var + EPS) * g + b


def _outproj_kernel(ys_ref, ya_ref, x_ref, gt_ref, sc_ref, sh_ref, wo_ref, lng_ref, lnb_ref, wrh_ref, wrl_ref, rb_ref,
                    x1_ref, rt_ref, dest_ref, cnt_ref, tbl_ref, xrows_hbm,
                    carry, galloc, tbl, u2t, didx_v, didx_s, cnt_v, cnt_s, zeros, idx_sem, sc_sem, z_sem,
                    *, tm, blk, n_blk):
    i = pl.program_id(0)
    last = pl.num_programs(0) - 1
    cur = i % 2
    prev = 1 - cur

    def idx_ready(slot):
        return pltpu.make_async_copy(didx_v.at[slot], didx_s.at[slot], idx_sem.at[slot])

    def dispatched(slot):
        return pltpu.make_async_copy(u2t.at[slot], u2t.at[slot], sc_sem.at[slot])

    def dispatch_copy(slot, r, k):
        return pltpu.make_async_copy(u2t.at[slot, pl.ds(r * PACK_TILE, PACK_TILE)],
                                     xrows_hbm.at[pl.ds(didx_s[slot, k, r] * PACK_TILE, PACK_TILE)], sc_sem.at[slot])

    @pl.when(i == 0)
    def _init():
        carry[...] = jnp.zeros_like(carry)
        galloc[...] = jnp.zeros_like(galloc)
        tbl[...] = jnp.zeros_like(tbl)

    @pl.when(i >= 2)
    def _reuse():
        dispatched(cur).wait()
        dispatched(cur).wait()

    def step(dispatch_prev):
        n_parts = OUTPROJ_PARTS
        n_slices = 3 * n_parts + 2

        def dispatch_slice(c):
            if dispatch_prev:
                for r_ in range(c * tm // n_slices, (c + 1) * tm // n_slices):
                    dispatch_copy(prev, r_, 0).start(priority=0)
                    dispatch_copy(prev, r_, 1).start(priority=1)

        if dispatch_prev:
            idx_ready(prev).wait()
        part = tm // n_parts
        hs = []
        for a in range(n_parts):
            rs = slice(a * part, (a + 1) * part)
            dispatch_slice(a)
            hs.append(_dot(ys_ref[rs, :], wo_ref[0:D_SSM, :]) + _dot(ya_ref[rs, :], wo_ref[D_SSM:D, :]))
        logit_parts = []
        for a in range(n_parts):
            rs = slice(a * part, (a + 1) * part)
            dispatch_slice(n_parts + 2 * a)
            x1 = _layer_norm(ALPHA * x_ref[rs, :] + (1.0 + gt_ref[0, 0]) * hs[a], lng_ref[...], lnb_ref[...])
            x1_ref[rs, :] = x1
            u2 = x1 * (1.0 + sc_ref[0, 0]) + sh_ref[0, 0]
            uh = u2.astype(bf16)
            _store_packed_rows(u2t.at[cur, pl.ds(a * part * PACK_TILE, part * PACK_TILE)], _pack_rows(uh))
            dispatch_slice(n_parts + 2 * a + 1)
            ul = (u2 - uh.astype(f32)).astype(bf16)
            logit_parts.append((_dot(uh, wrh_ref[...]) + _dot(ul, wrh_ref[...])) + _dot(uh, wrl_ref[...]))
        logits = jnp.concatenate(logit_parts, axis=0) + rb_ref[...]
        lane = lax.broadcasted_iota(i32, (tm, 128), 1).astype(f32)
        big = jnp.float32(1e9)

        def first_max(vals):
            m = jnp.max(vals, axis=-1, keepdims=True)
            return m, jnp.min(jnp.where(vals == m, lane, big), axis=-1, keepdims=True)

        gl = jnp.where(lane < N_GROUPS_R, logits, NEG)
        gmax, gidx = first_max(gl)
        g_p = 1.0 / jnp.sum(jnp.exp(gl - gmax), axis=-1, keepdims=True)
        lo = N_GROUPS_R + EXPERTS_PER_GROUP * gidx
        el = jnp.where((lane >= lo) & (lane < lo + EXPERTS_PER_GROUP), logits, NEG)
        m1, i1 = first_max(el)
        el2 = jnp.where(lane == i1, NEG, el)
        m2, i2 = first_max(el2)
        r = jnp.exp(m2 - m1)
        w1 = g_p / (1.0 + r)
        w2 = g_p * r / (1.0 + r)
        dispatch_slice(3 * n_parts)

        oh1 = lane == i1
        oh2 = lane == i2
        oh = jnp.where(oh1 | oh2, 1.0, 0.0)
        r_i = lax.broadcasted_iota(i32, (tm, tm), 0)
        c_i = lax.broadcasted_iota(i32, (tm, tm), 1)
        lower = jnp.where(r_i > c_i, 1.0, 0.0).astype(bf16)
        c_old = carry[...]
        prefix = _dot(lower, oh.astype(bf16)) + c_old
        rank1 = jnp.sum(jnp.where(oh1, prefix, 0.0), axis=-1, keepdims=True)
        rank2 = jnp.sum(jnp.where(oh2, prefix, 0.0), axis=-1, keepdims=True)
        c_new = c_old + jnp.sum(oh, axis=0, keepdims=True)
        carry[...] = c_new
        dispatch_slice(3 * n_parts + 1)

        nb_old = jnp.floor((c_old + (blk - 1)) * (1.0 / blk))
        nb_new = jnp.floor((c_new + (blk - 1)) * (1.0 / blk))
        fresh = nb_new - nb_old
        sq_r = lax.broadcasted_iota(i32, (128, 128), 0)
        sq_c = lax.broadcasted_iota(i32, (128, 128), 1)
        before = jnp.where(sq_r < sq_c, 1.0, 0.0).astype(bf16)
        base = galloc[...] + _dot(jnp.broadcast_to(fresh, (8, 128)).astype(bf16), before)[0:1, :]
        galloc[...] = galloc[...] + jnp.sum(fresh, axis=-1, keepdims=True)
        ordinal = sq_r.astype(f32)
        tbl[...] = jnp.where((ordinal >= nb_old) & (ordinal < nb_new), base + (ordinal - nb_old), tbl[...])
        tbl_ref[...] = tbl[...]
        cnt_ref[...] = jnp.concatenate([c_new, galloc[...], jnp.zeros((6, 128), f32)], axis=0)

        rt = jnp.where(lane == 4, w1, jnp.where(lane == 5, w2, 0.0))
        rt_ref[...] = rt

        ids_bf = tbl[...].astype(bf16)

        def row_of(onehot, rank):
            nth = jnp.floor(rank * (1.0 / blk))
            ids = _dot(jnp.where(lane == nth, 1.0, 0.0).astype(bf16), ids_bf)
            return jnp.sum(jnp.where(onehot, ids, 0.0), axis=-1, keepdims=True) * blk + (rank - nth * blk)

        d1 = row_of(oh1, rank1)
        d2 = row_of(oh2, rank2)
        dd = jnp.where(lane == 0, d1, jnp.where(lane == 1, d2, 0.0)).T[0:8, :].astype(i32)
        dest_ref[...] = dd
        didx_v[cur] = dd
        idx_ready(cur).start()

    pl.when(i == 0)(lambda: step(False))
    pl.when(i > 0)(lambda: step(True))

    @pl.when(i == last)
    def _epilogue():
        idx_ready(cur).wait()

        def issue(r_, c):
            dispatch_copy(cur, r_, 0).start()
            dispatch_copy(cur, r_, 1).start()
            return c
        lax.fori_loop(0, tm, issue, 0)

        cnt = carry[...]
        n_blocks = jnp.floor((cnt + (blk - 1)) * (1.0 / blk))
        ordinal = lax.broadcasted_iota(i32, (128, 128), 0).astype(f32)
        last_id = jnp.sum(jnp.where(ordinal == n_blocks - 1.0, tbl[...], 0.0), axis=0, keepdims=True)
        used = cnt - (n_blocks - 1.0) * blk
        cnt_v[...] = jnp.concatenate([last_id * blk + used, blk - used, galloc[...], jnp.zeros((5, 128), f32)],
                                     axis=0).astype(i32)
        to_smem = pltpu.make_async_copy(cnt_v, cnt_s, z_sem)
        to_smem.start()
        to_smem.wait()
        zeros[...] = jnp.zeros_like(zeros)
        sizes = [1 << b for b in reversed(range(blk.bit_length() - 1))]

        def for_each_fill(fn):
            def tail(e, c):
                pad = cnt_s[1, N_GROUPS_R + e]
                off = cnt_s[0, N_GROUPS_R + e]
                for sz in sizes:
                    @pl.when((pad & sz) != 0)
                    def _(off=off, sz=sz):
                        fn(pltpu.make_async_copy(zeros.at[pl.ds(0, sz * PACK_TILE)],
                                                 xrows_hbm.at[pl.ds(off * PACK_TILE, sz * PACK_TILE)], z_sem))
                    off = off + (pad & sz)
                return c
            lax.fori_loop(0, N_EXPERTS, tail, 0)

            def whole(b, c):
                fn(pltpu.make_async_copy(zeros, xrows_hbm.at[pl.ds(b * blk * PACK_TILE, blk * PACK_TILE)], z_sem))
                return c
            lax.fori_loop(cnt_s[2, 0], n_blk, whole, 0)

        for_each_fill(lambda copy: copy.start())
        for_each_fill(lambda copy: copy.wait())

        @pl.when(i > 0)
        def _prev_done():
            dispatched(prev).wait()
            dispatched(prev).wait()
        dispatched(cur).wait()
        dispatched(cur).wait()


def _outproj(y_ssm, y_att, x, mod4, w_out, ln_g, ln_b, wr_hi, wr_lo, rb, t, blk, n_blk):
    n = x.shape[0]
    tm = min(OUTPROJ_ROWS, t)
    nt = t // tm
    assert n // blk + 1 <= 128 and n_blk <= 256, "block-id table: 128 blocks per expert, ids exact in bf16"
    vec = lambda k: pl.BlockSpec((1, 1, 1, D), lambda i, k=k: (k, i // nt, 0, 0))
    full = lambda shape: pl.BlockSpec(shape, lambda i: (0,) * len(shape))
    rows = lambda w: pl.BlockSpec((tm, w), lambda i: (i, 0))
    return pl.pallas_call(
        functools.partial(_outproj_kernel, tm=tm, blk=blk, n_blk=n_blk),
        grid=(n // tm,),
        in_specs=[rows(D_SSM), rows(D_ATT), rows(D), vec(2), vec(4), vec(3),
                  full((D, D)), full((1, D)), full((1, D)), full((D, 128)), full((D, 128)), full((1, 128))],
        out_specs=[rows(D), rows(128), pl.BlockSpec((8, tm), lambda i: (0, i)), full((8, 128)), full((128, 128)),
                   pl.BlockSpec(memory_space=pl.ANY)],
        out_shape=[jax.ShapeDtypeStruct((n, D), f32), jax.ShapeDtypeStruct((n, 128), f32),
                   jax.ShapeDtypeStruct((8, n), i32), jax.ShapeDtypeStruct((8, 128), f32),
                   jax.ShapeDtypeStruct((128, 128), f32),
                   jax.ShapeDtypeStruct((n_blk * blk * PACK_TILE, 128), u32)],
        scratch_shapes=[pltpu.VMEM((1, 128), f32), pltpu.VMEM((1, 128), f32), pltpu.VMEM((128, 128), f32),
                        pltpu.VMEM((2, tm * PACK_TILE, 128), u32),
                        pltpu.VMEM((2, 8, tm), i32), pltpu.SMEM((2, 8, tm), i32),
                        pltpu.VMEM((8, 128), i32), pltpu.SMEM((8, 128), i32),
                        pltpu.VMEM((blk * PACK_TILE, 128), u32),
                        pltpu.SemaphoreType.DMA((2,)), pltpu.SemaphoreType.DMA((2,)), pltpu.SemaphoreType.DMA(())],
        compiler_params=pltpu.CompilerParams(dimension_semantics=("arbitrary",), vmem_limit_bytes=VMEM_LIMIT),
        name="outproj",
    )(y_ssm, y_att, x, mod4, mod4, mod4, w_out, ln_g, ln_b, wr_hi, wr_lo, rb)


def _tile_copy(src_hbm, row, buf, slot, sem):
    return pltpu.make_async_copy(src_hbm.at[row], buf.at[pl.ds(slot * ROW_TILE, ROW_TILE)], sem)


def _moe_kernel(be_ref, br_ref, nu_ref, nxt_ref, par_ref, nv_ref, x_ref, wg_hbm, wu_hbm, wd_hbm, y_ref,
                wfg, wfu, wfd, wgb, wub, wdb, wsem, *, blk):
    i = pl.program_id(0)
    used = i < nu_ref[0]

    def fetch(e, slot):
        return [pltpu.make_async_copy(src.at[e], dst.at[slot], wsem.at[slot])
                for src, dst in ((wg_hbm, wfg), (wu_hbm, wfu), (wd_hbm, wfd))]

    @pl.when(i == 0)
    def _first():
        for c in fetch(be_ref[0], par_ref[0]):
            c.start()

    @pl.when(jnp.logical_and(used, jnp.logical_or(i == 0, be_ref[i] != be_ref[jnp.maximum(i - 1, 0)])))
    def _switch():
        slot = par_ref[i]
        for c in fetch(be_ref[i], slot):
            c.wait()

        @pl.when(nxt_ref[i] >= 0)
        def _next():
            for c in fetch(nxt_ref[i], 1 - slot):
                c.start(priority=1)
        wgb[...] = wfg[slot].astype(bf16)
        wub[...] = wfu[slot].astype(bf16)
        wdb[...] = wfd[slot].astype(bf16)

    @pl.when(jnp.logical_not(used))
    def _spare():
        y_ref[...] = jnp.zeros_like(y_ref)

    def mlp(rows):
        xa, xb = _load_packed_rows(x_ref.at[pl.ds(0, rows * PACK_TILE)], rows)
        half = D // 2
        gate = _dot(xa, wgb[0:half, :]) + _dot(xb, wgb[half:D, :])
        up = _dot(xa, wub[0:half, :]) + _dot(xb, wub[half:D, :])
        hid = (_silu(gate) * up).astype(bf16)
        _store_row_tiles(y_ref.at[pl.ds(0, rows * ROW_TILE)], _dot(hid, wdb[...]))

    sparse = nv_ref[i] <= blk // 2

    @pl.when(jnp.logical_and(used, jnp.logical_not(sparse)))
    def _full():
        mlp(blk)

    @pl.when(jnp.logical_and(used, sparse))
    def _half():
        mlp(blk // 2)
        y_ref[pl.ds(blk // 2 * ROW_TILE, blk // 2 * ROW_TILE), :] = jnp.zeros((blk // 2 * ROW_TILE, 128), f32)


def _moe(block_e, block_row, n_used, next_e, parity, n_valid, x_rows, w_gate, w_up, w_down, blk):
    nblk = block_e.shape[0]
    rspec = pl.BlockSpec((blk * ROW_TILE, 128), lambda i, be, br, *_: (br[i], 0))
    xspec = pl.BlockSpec((blk * PACK_TILE, 128), lambda i, be, br, nu, *_: (br[jnp.minimum(i, nu[0] - 1)], 0))
    anyspec = pl.BlockSpec(memory_space=pl.ANY)
    grid_spec = pltpu.PrefetchScalarGridSpec(
        num_scalar_prefetch=6,
        grid=(nblk,),
        in_specs=[xspec, anyspec, anyspec, anyspec],
        out_specs=rspec,
        scratch_shapes=[pltpu.VMEM((2, D, D_EXPERT), f32), pltpu.VMEM((2, D, D_EXPERT), f32),
                        pltpu.VMEM((2, D_EXPERT, D), f32),
                        pltpu.VMEM((D, D_EXPERT), bf16), pltpu.VMEM((D, D_EXPERT), bf16),
                        pltpu.VMEM((D_EXPERT, D), bf16),
                        pltpu.SemaphoreType.DMA((2,))],
    )
    return pl.pallas_call(
        functools.partial(_moe_kernel, blk=blk),
        grid_spec=grid_spec,
        out_shape=jax.ShapeDtypeStruct((nblk * blk * ROW_TILE, 128), f32),
        compiler_params=pltpu.CompilerParams(dimension_semantics=("arbitrary",), vmem_limit_bytes=VMEM_LIMIT),
        name="moe",
    )(block_e, block_row, n_used, next_e, parity, n_valid, x_rows, w_gate, w_up, w_down)


def _combine_kernel(d1_ref, d2_ref, y_hbm, x1_ref, rt_ref, gt_ref, lng_ref, lnb_ref, o_ref, buf, sem, *, tm):
    i = pl.program_id(0)
    last = pl.num_programs(0) - 1

    @pl.when(i == 0)
    def _prologue():
        for s in range(2):
            def issue(r, c, s=s):
                base = jnp.minimum(s, last) * tm
                _tile_copy(y_hbm, d1_ref[base + r], buf.at[s, 0], r, sem.at[s]).start()
                _tile_copy(y_hbm, d2_ref[base + r], buf.at[s, 1], r, sem.at[s]).start()
                return c
            lax.fori_loop(0, tm, issue, 0)

    def rows_ready(slot):
        return pltpu.make_async_copy(buf.at[slot], buf.at[slot], sem.at[slot])

    slot = i % 3
    nxt = (i + 2) % 3
    base = jnp.minimum(i + 2, last) * tm
    rows_ready(slot).wait()
    for r in range(tm):
        _tile_copy(y_hbm, d1_ref[base + r], buf.at[nxt, 0], r, sem.at[nxt]).start(priority=0)
        _tile_copy(y_hbm, d2_ref[base + r], buf.at[nxt, 1], r, sem.at[nxt]).start(priority=1)
    rt = rt_ref[...]
    moe = rt[:, 4:5] * _load_row_tiles(buf.at[slot, 0], tm) + rt[:, 5:6] * _load_row_tiles(buf.at[slot, 1], tm)
    y = ALPHA * x1_ref[...] + (1.0 + gt_ref[0, 0]) * moe
    o_ref[...] = _layer_norm(y, lng_ref[...], lnb_ref[...])

    @pl.when(i == last)
    def _drain():
        rows_ready((i + 1) % 3).wait()
        rows_ready((i + 2) % 3).wait()


def _combine(dest1, dest2, y_rows, x1, rt, mod4, ln_g, ln_b, t):
    n = x1.shape[0]
    tm = min(COMBINE_ROWS, t)
    nt = t // tm
    full = lambda shape: pl.BlockSpec(shape, lambda i, d1, d2: (0,) * len(shape))
    rows = lambda w: pl.BlockSpec((tm, w), lambda i, d1, d2: (i, 0))
    grid_spec = pltpu.PrefetchScalarGridSpec(
        num_scalar_prefetch=2,
        grid=(n // tm,),
        in_specs=[pl.BlockSpec(memory_space=pl.ANY), rows(D), rows(128),
                  pl.BlockSpec((1, 1, 1, D), lambda i, d1, d2: (5, i // nt, 0, 0)),
                  full((1, D)), full((1, D))],
        out_specs=rows(D),
        scratch_shapes=[pltpu.VMEM((3, 2, tm * ROW_TILE, 128), f32), pltpu.SemaphoreType.DMA((3,))],
    )
    return pl.pallas_call(
        functools.partial(_combine_kernel, tm=tm),
        grid_spec=grid_spec,
        out_shape=jax.ShapeDtypeStruct((n, D), f32),
        compiler_params=pltpu.CompilerParams(dimension_semantics=("arbitrary",), vmem_limit_bytes=VMEM_LIMIT),
        name="combine",
    )(dest1, dest2, y_rows, x1, rt, mod4, ln_g, ln_b)


def _hi_lo(w):
    hi = w.astype(bf16)
    return hi, (w - hi.astype(f32)).astype(bf16)


def _pad_lanes(v, offset, width):
    return jnp.zeros((width,), f32).at[offset:offset + v.shape[0]].set(v)


def _layer(x, c, ada_w, ada_b, w_in, conv_w, conv_b, dt_bias, a_log, d_skip, ssm_norm_g, fg_bias, att_norm_g,
           w_out, ln1_g, ln1_b, router_g_w, router_g_b, router_e_w, router_e_b, w_gate, w_up, w_down, ln2_g, ln2_b):
    bsz, t, _ = x.shape
    n = bsz * t

    mod4 = _ada(c, ada_w, ada_b).reshape(6, bsz, 1, D)

    w_main, w_small = _pack_w_in(w_in)
    zx, bc, qkv, dtf = _inproj(x, mod4, w_main, w_small, conv_w, conv_b.reshape(1, -1))

    pc = jnp.stack([_pad_lanes(dt_bias, 0, 128), _pad_lanes(a_log, 0, 128), _pad_lanes(fg_bias, 8, 128)]
                   + [jnp.zeros((128,), f32)] * 5)
    rep = lambda v: jnp.repeat(v, HEAD_DIM)
    pe = jnp.stack([rep(dt_bias), rep(a_log), rep(d_skip), ssm_norm_g] + [jnp.zeros((D_SSM,), f32)] * 4)
    y_ssm, cumc = _ssd(zx, bc, dtf, pc, pe)

    y_att = _attn(qkv, cumc, att_norm_g.reshape(1, -1))

    wr = jnp.concatenate([router_g_w, router_e_w, jnp.zeros((D, 128 - N_GROUPS_R - N_EXPERTS), f32)], axis=1)
    wr_hi, wr_lo = _hi_lo(wr)
    rb = jnp.concatenate([router_g_b, router_e_b, jnp.zeros((128 - N_GROUPS_R - N_EXPERTS,), f32)]).reshape(1, 128)
    blk = MOE_ROWS
    nblk = (2 * n) // blk + N_EXPERTS
    x1, rt, dest, cnt, tbl, x_rows = _outproj(y_ssm.reshape(n, D_SSM), y_att.reshape(n, D_ATT), x.reshape(n, D), mod4,
                                              w_out.astype(bf16), ln1_g.reshape(1, D), ln1_b.reshape(1, D),
                                              wr_hi, wr_lo, rb, t, blk, nblk)

    counts = cnt[0, N_GROUPS_R:N_GROUPS_R + N_EXPERTS].astype(i32)
    nb = (counts + blk - 1) // blk
    nb_end = jnp.cumsum(nb)
    n_used = nb_end[N_EXPERTS - 1:]
    every = jnp.arange(nblk, dtype=i32)
    step = jnp.minimum(every, n_used[0] - 1)
    owner = (nb_end[None, :] <= step[:, None]).astype(i32)
    block_e = jnp.sum(owner, axis=1)
    first = jnp.sum(owner * nb[None, :], axis=1)
    lanes = jnp.arange(128, dtype=i32)[None, :]
    nth_row = jnp.dot((lanes == (step - first)[:, None]).astype(f32), tbl, precision=lax.Precision.HIGHEST)
    block_id = jnp.sum(jnp.where(lanes == (block_e + N_GROUPS_R)[:, None], nth_row, 0.0), axis=1).astype(i32)
    block_row = jnp.where(every < n_used[0], block_id, every)
    run_end = first + jnp.sum((jnp.arange(N_EXPERTS)[None, :] == block_e[:, None]) * nb[None, :], axis=1)
    next_e = jnp.where(run_end < n_used[0], jnp.sum((nb_end[None, :] <= run_end[:, None]).astype(i32), axis=1), -1)
    parity = jnp.sum(((jnp.arange(N_EXPERTS)[None, :] < block_e[:, None]) & (nb[None, :] > 0)).astype(i32), axis=1) % 2

    own = (jnp.arange(N_EXPERTS)[None, :] == block_e[:, None]).astype(i32)
    n_valid = jnp.clip(jnp.sum(own * counts[None, :], axis=1) - (step - first) * blk, 0, blk)
    y_rows = _moe(block_e, block_row, n_used, next_e, parity, n_valid, x_rows,
                  w_gate, w_up, w_down, blk)
    out = _combine(dest[0], dest[1], y_rows.reshape(nblk * blk, ROW_TILE, 128), x1, rt, mod4,
                   ln2_g.reshape(1, D), ln2_b.reshape(1, D), t)
    return out.reshape(bsz, t, D)


def kernel(x, c, ada_w, ada_b, w_in, conv_w, conv_b, dt_bias, a_log, d_skip, ssm_norm_g, fg_bias, att_norm_g, w_out,
           ln1_g, ln1_b, router_g_w, router_g_b, router_e_w, router_e_b, w_gate, w_up, w_down, ln2_g, ln2_b):
    depth = ada_w.shape[0]
    for l in range(depth):
        x = _layer(x, c, ada_w[l], ada_b[l], w_in[l], conv_w[l], conv_b[l], dt_bias[l], a_log[l], d_skip[l],
                   ssm_norm_g[l], fg_bias[l], att_norm_g[l], w_out[l], ln1_g[l], ln1_b[l], router_g_w[l],
                   router_g_b[l], router_e_w[l], router_e_b[l], w_gate[l], w_up[l], w_down[l], ln2_g[l], ln2_b[l])
    return x
```

```python
import functools

import jax
import jax.numpy as jnp
import numpy as np
from jax import lax
from jax.experimental import pallas as pl
from jax.experimental.pallas import tpu as pltpu

f32 = jnp.float32
bf16 = jnp.bfloat16
i32 = jnp.int32

D = 1024
D_SSM = 512
D_ATT = 512
HEAD_DIM = 64
GROUP_W = 256
N_STATE = 128
CONV_K = 4
N_GROUPS_R = 4
EXPERTS_PER_GROUP = 8
N_EXPERTS = 32
D_EXPERT = 512
ALPHA = 2.0 ** 0.25
EPS = 1e-5
NEG = -1e30
LOG2E = 1.4426950408889634
QK_SCALE = HEAD_DIM ** -0.5 * LOG2E
V_ROWS = 80

SSD_CHUNK = 256
ATT_BLOCK = 256
INPROJ_ROWS = 512
OUTPROJ_ROWS = 512
OUTPROJ_PARTS = 2
MOE_ROWS = 512
COMBINE_ROWS = 512
VMEM_LIMIT = 48 * 1024 * 1024


def _dot(a, b):
    return jnp.dot(a, b, preferred_element_type=f32)


def _dot_nt(a, b):
    return lax.dot_general(a, b, (((1,), (1,)), ((), ())), preferred_element_type=f32)


def _dot_tn(a, b):
    return lax.dot_general(a, b, (((0,), (0,)), ((), ())), preferred_element_type=f32)


def _split3(v):
    hi = v.astype(bf16)
    r1 = v - hi.astype(f32)
    mid = r1.astype(bf16)
    lo = (r1 - mid.astype(f32)).astype(bf16)
    return hi, mid, lo


def _dot_exact_lhs(m, v):
    hi, mid, lo = _split3(v)
    return (_dot(m, hi) + _dot(m, mid)) + _dot(m, lo)


def _dot_exact_rhs(v, m):
    hi, mid, lo = _split3(v)
    return (_dot(hi, m) + _dot(mid, m)) + _dot(lo, m)


ROW_TILE = 8


def _store_row_tiles(ref, val):
    rows = val.shape[0]
    for c in range(ROW_TILE):
        ref[pl.ds(c, rows, stride=ROW_TILE), :] = val[:, c * 128:(c + 1) * 128]


def _load_row_tiles(ref, rows):
    return jnp.concatenate([ref[pl.ds(c, rows, stride=ROW_TILE), :] for c in range(ROW_TILE)], axis=1)


PACK_TILE = 4
u32 = jnp.uint32


def _pack_rows(vb):
    lo = pltpu.bitcast(vb[:, 0:D // 2].astype(f32), u32) >> 16
    hi = pltpu.bitcast(vb[:, D // 2:D].astype(f32), u32) & jnp.uint32(0xFFFF0000)
    return lo | hi


def _store_packed_rows(ref, packed):
    rows = packed.shape[0]
    for c in range(PACK_TILE):
        ref[pl.ds(c, rows, stride=PACK_TILE), :] = packed[:, c * 128:(c + 1) * 128]


def _load_packed_rows(ref, rows):
    words = [ref[pl.ds(c, rows, stride=PACK_TILE), :] for c in range(PACK_TILE)]
    lo = jnp.concatenate([pltpu.bitcast(w << 16, f32) for w in words], axis=1)
    hi = jnp.concatenate([pltpu.bitcast(w & jnp.uint32(0xFFFF0000), f32) for w in words], axis=1)
    return lo.astype(bf16), hi.astype(bf16)


def _softplus(x):
    return jnp.maximum(x, 0.0) + jnp.log1p(jnp.exp(-jnp.abs(x)))


def _silu(x):
    return x * jax.nn.sigmoid(x)


def _ada_kernel(c_ref, w_ref, b_ref, o_ref):
    s = _silu(c_ref[...]).astype(bf16)
    o_ref[0] = _dot(s, w_ref[...].astype(bf16)) + b_ref[0]


def _ada(c, w, b):
    bsz = c.shape[0]
    return pl.pallas_call(
        _ada_kernel,
        grid=(6,),
        in_specs=[pl.BlockSpec((bsz, D), lambda j: (0, 0)),
                  pl.BlockSpec((D, D), lambda j: (0, j)),
                  pl.BlockSpec((1, 1, D), lambda j: (j, 0, 0))],
        out_specs=pl.BlockSpec((1, bsz, D), lambda j: (j, 0, 0)),
        out_shape=jax.ShapeDtypeStruct((6, bsz, D), f32),
        compiler_params=pltpu.CompilerParams(dimension_semantics=("arbitrary",), vmem_limit_bytes=VMEM_LIMIT),
        name="ada",
    )(c, w, b.reshape(6, 1, D))


W_IN_COLS = 3088
PACK_ROWS = 256


def _pack_w_in_kernel(wt_ref, wm_ref, ws_ref):
    for j in range(3072 // PACK_ROWS):
        src = j * PACK_ROWS if j * PACK_ROWS < 1536 else j * PACK_ROWS + 8
        wm_ref[:, j * PACK_ROWS:(j + 1) * PACK_ROWS] = wt_ref[src:src + PACK_ROWS, :].T.astype(bf16)
    small = jnp.concatenate([wt_ref[1536:1544, :], wt_ref[W_IN_COLS - 8:W_IN_COLS, :], jnp.zeros((112, D), f32)],
                            axis=0).T
    hi = small.astype(bf16)
    ws_ref[:, 0:128] = hi
    ws_ref[:, 128:256] = (small - hi.astype(f32)).astype(bf16)


def _pack_w_in(w_in):
    full = lambda shape: pl.BlockSpec(shape, lambda i: (0,) * len(shape))
    return pl.pallas_call(
        _pack_w_in_kernel,
        grid=(1,),
        in_specs=[full((W_IN_COLS, D))],
        out_specs=[full((D, 3072)), full((D, 256))],
        out_shape=[jax.ShapeDtypeStruct((D, 3072), bf16), jax.ShapeDtypeStruct((D, 256), bf16)],
        compiler_params=pltpu.CompilerParams(dimension_semantics=("arbitrary",), vmem_limit_bytes=VMEM_LIMIT),
        name="pack_w_in",
    )(w_in.T)


def _inproj_kernel(x_ref, sc_ref, sh_ref, wm_ref, ws_ref, cw_ref, cb_ref, zx_ref, bc_ref, qkv_ref, dtf_ref, xcat, *, tm):
    i = pl.program_id(1)

    @pl.when(i == 0)
    def _init():
        xcat[0:8, :] = jnp.zeros((8, 2 * D_SSM), f32)

    u = x_ref[0] * (1.0 + sc_ref[0, 0]) + sh_ref[0, 0]
    ub = u.astype(bf16)
    xcat[8:8 + tm, 0:512] = _dot(ub, wm_ref[:, 512:1024])
    xcat[8:8 + tm, 512:1024] = _dot(ub, wm_ref[:, 1024:1536])
    zx_ref[0, :, 0:512] = _dot(ub, wm_ref[:, 0:512])
    acc = cw_ref[0:1, :] * xcat[5:5 + tm, :] + cb_ref[...]
    for k in range(1, CONV_K):
        acc = acc + cw_ref[k:k + 1, :] * xcat[5 + k:5 + k + tm, :]
    xcat[0:8, :] = xcat[tm:tm + 8, :]
    xbc = _silu(acc)
    zx_ref[0, :, 512:1024] = xbc[:, 0:512]
    bc_ref[0] = xbc[:, 512:1024].astype(bf16)
    for j, scale in enumerate((QK_SCALE, 1.0, 1.0)):
        qkv_ref[0, :, j * 512:(j + 1) * 512] = (
            _dot(ub, wm_ref[:, 1536 + j * 512:1536 + (j + 1) * 512]) * scale).astype(bf16)
    ul = (u - ub.astype(f32)).astype(bf16)
    d_hl = _dot(ub, ws_ref[...])
    dtf_ref[0] = (d_hl[:, 0:128] + _dot(ul, ws_ref[:, 0:128])) + d_hl[:, 128:256]


def _inproj(x, mod4, w_main, ws, conv_w, conv_b):
    bsz, t, _ = x.shape
    tm = min(INPROJ_ROWS, t)
    vec = lambda k: pl.BlockSpec((1, 1, 1, D), lambda b, i, k=k: (k, b, 0, 0))
    full = lambda shape: pl.BlockSpec(shape, lambda b, i: (0,) * len(shape))
    rows = lambda w: pl.BlockSpec((1, tm, w), lambda b, i: (b, i, 0))
    return pl.pallas_call(
        functools.partial(_inproj_kernel, tm=tm),
        grid=(bsz, t // tm),
        in_specs=[rows(D), vec(1), vec(0), full((D, 3072)), full((D, 256)),
                  full((CONV_K, 2 * D_SSM)), full((1, 2 * D_SSM))],
        out_specs=[rows(1024), rows(512), rows(1536), rows(128)],
        out_shape=[jax.ShapeDtypeStruct((bsz, t, 1024), f32),
                   jax.ShapeDtypeStruct((bsz, t, 512), bf16),
                   jax.ShapeDtypeStruct((bsz, t, 1536), bf16),
                   jax.ShapeDtypeStruct((bsz, t, 128), f32)],
        scratch_shapes=[pltpu.VMEM((tm + 8, 2 * D_SSM), f32)],
        compiler_params=pltpu.CompilerParams(dimension_semantics=("parallel", "arbitrary"),
                                             vmem_limit_bytes=VMEM_LIMIT),
        name="inproj",
    )(x, mod4, mod4, w_main, ws, conv_w, conv_b)


def _ssd_kernel(z_ref, xs_ref, bc_ref, dtf_ref, pc_ref, pe_ref, y_ref, cumc_ref, state, carry, *, lc):
    j = pl.program_id(1)

    @pl.when(j == 0)
    def _init():
        state[...] = jnp.zeros_like(state)
        carry[...] = jnp.zeros_like(carry)

    xs = xs_ref[0]
    bm = bc_ref[0, :, 0:256]
    cm = bc_ref[0, :, 256:512]

    dtf = dtf_ref[0]
    lane = lax.broadcasted_iota(i32, (lc, 128), 1)
    dt_c = _softplus(dtf + pc_ref[0:1, :])
    a_c = dt_c * (-jnp.exp(pc_ref[1:2, :]))
    logf = -_softplus(-(dtf + pc_ref[2:3, :]))
    v = jnp.where(lane < 8, a_c, logf)
    r_i = lax.broadcasted_iota(i32, (lc, lc), 0)
    c_i = lax.broadcasted_iota(i32, (lc, lc), 1)
    tri = r_i >= c_i
    tri_b = jnp.where(tri, 1.0, 0.0).astype(bf16)
    cum = _dot_exact_lhs(tri_b, v) + carry[...]
    carry[...] = jnp.where(lane[0:1, :] >= 8, cum[lc - 1:lc, :], 0.0)
    cumc_ref[0] = cum
    cs_t = cum.T[0:8, :]

    e_r = lax.broadcasted_iota(i32, (128, D_SSM), 0)
    e_c = lax.broadcasted_iota(i32, (128, D_SSM), 1)
    expand = jnp.where(jnp.right_shift(e_c, 6) == e_r, 1.0, 0.0).astype(bf16)
    dt_e = _dot_exact_rhs(dt_c, expand)
    cs_e = _dot_exact_rhs(cum, expand)

    xdt = xs * dt_e
    ecs = jnp.exp(cs_e)
    cs_last = cs_e[lc - 1:lc, :]
    dec_st = jnp.exp(cs_last - cs_e)
    lane_g = lax.broadcasted_iota(i32, (1, GROUP_W), 1)
    ys = []
    for g in range(2):
        gs = slice(g * GROUP_W, (g + 1) * GROUP_W)
        bg = bm[:, g * N_STATE:(g + 1) * N_STATE]
        cg = cm[:, g * N_STATE:(g + 1) * N_STATE]
        cb = _dot_nt(cg, bg)
        xdt_g = xdt[:, gs]
        xdt_gb = xdt_g.astype(bf16)
        ms, xb = [], []
        for hh in range(4):
            h = g * 4 + hh
            lm = jnp.exp(jnp.where(tri, cum[:, h:h + 1] - cs_t[h:h + 1, :], -jnp.inf))
            ms.append((cb * lm).astype(bf16))
            xb.append(jnp.where(jnp.right_shift(lane_g, 6) == hh, xdt_gb, jnp.zeros_like(xdt_gb)))
        y_diag = _dot(jnp.concatenate(ms, axis=1), jnp.concatenate(xb, axis=0))
        st = state[g]
        y_off = _dot(cg, st.astype(bf16)) * ecs[:, gs]
        upd = _dot_tn(bg, (xdt_g * dec_st[:, gs]).astype(bf16))
        state[g] = st * jnp.exp(cs_last[:, gs]) + upd
        ys.append(y_diag + y_off + xs[:, gs] * pe_ref[2:3, gs])

    outs = []
    for g in range(2):
        gs = slice(g * GROUP_W, (g + 1) * GROUP_W)
        yg = ys[g] * _silu(z_ref[0, :, gs])
        ms_ = jnp.mean(yg * yg, axis=-1, keepdims=True)
        outs.append(yg * lax.rsqrt(ms_ + EPS))
    y_ref[0] = (jnp.concatenate(outs, axis=1) * pe_ref[3:4, :]).astype(bf16)


def _ssd(zx, bc, dtf, pc, pe):
    bsz, t, _ = zx.shape
    lc = min(SSD_CHUNK, t)
    col = lambda k: pl.BlockSpec((1, lc, 512), lambda b, j, k=k: (b, j, k))
    full = lambda shape: pl.BlockSpec(shape, lambda b, j: (0,) * len(shape))
    return pl.pallas_call(
        functools.partial(_ssd_kernel, lc=lc),
        grid=(bsz, t // lc),
        in_specs=[col(0), col(1), col(0),
                  pl.BlockSpec((1, lc, 128), lambda b, j: (b, j, 0)),
                  full((8, 128)), full((8, D_SSM))],
        out_specs=[pl.BlockSpec((1, lc, D_SSM), lambda b, j: (b, j, 0)),
                   pl.BlockSpec((1, lc, 128), lambda b, j: (b, j, 0))],
        out_shape=[jax.ShapeDtypeStruct((bsz, t, D_SSM), bf16),
                   jax.ShapeDtypeStruct((bsz, t, 128), f32)],
        scratch_shapes=[pltpu.VMEM((2, N_STATE, GROUP_W), f32),
                        pltpu.VMEM((1, 128), f32)],
        compiler_params=pltpu.CompilerParams(dimension_semantics=("parallel", "arbitrary"),
                                             vmem_limit_bytes=VMEM_LIMIT),
        name="ssd",
    )(zx, zx, bc, dtf, pc, pe)


def _attn_kernel(q_ref, k_ref, v_ref, cc_ref, psel_ref, ng_ref, o_ref, kaug, vt, acc, sc0, sc1, *, tq, t):
    i = pl.program_id(1)
    nkb = t // tq
    n_heads = D_ATT // HEAD_DIM
    lane = lax.broadcasted_iota(i32, (1, 128), 1)
    lo_half = lane < HEAD_DIM

    @pl.when(i == 0)
    def _build():
        eye = jnp.where(lax.broadcasted_iota(i32, (D_ATT, D_ATT), 0) == lax.broadcasted_iota(i32, (D_ATT, D_ATT), 1),
                        1.0, 0.0).astype(bf16)
        ones_rows = jnp.where(lax.broadcasted_iota(i32, (V_ROWS - HEAD_DIM, tq), 0) == 0, 1.0, 0.0).astype(bf16)
        for jb in range(nkb):
            rows = slice(jb * tq, (jb + 1) * tq)
            v_t = _dot_nt(eye, v_ref[0, rows, :]).astype(bf16)
            for h in range(n_heads):
                vt[jb, h * V_ROWS:h * V_ROWS + HEAD_DIM, :] = v_t[h * HEAD_DIM:(h + 1) * HEAD_DIM, :]
                vt[jb, h * V_ROWS + HEAD_DIM:(h + 1) * V_ROWS, :] = ones_rows
            pieces = jnp.concatenate(_split3(cc_ref[0, rows, :] * (-LOG2E)), axis=1)
            for p in range(n_heads // 2):
                a = _dot(pieces, psel_ref[p]).astype(bf16)
                kp = k_ref[0, rows, p * 128:(p + 1) * 128]
                kaug[2 * p, rows, :] = jnp.where(lo_half, kp, a)
                kaug[2 * p + 1, rows, :] = jnp.where(lo_half, a, kp)

    ones_hi = jnp.where((lane >= HEAD_DIM) & (lane < HEAD_DIM + 3), 1.0, 0.0).astype(bf16)
    ones_lo = jnp.where(lane < 3, 1.0, 0.0).astype(bf16)
    qa = []
    for p in range(n_heads // 2):
        qp = q_ref[0, :, p * 128:(p + 1) * 128]
        qa.append(jnp.where(lo_half, qp, ones_hi))
        qa.append(jnp.where(lo_half, ones_lo, qp))
    keep = lax.broadcasted_iota(i32, (tq, tq), 0) <= lax.broadcasted_iota(i32, (tq, tq), 1)
    acc[...] = jnp.zeros_like(acc)

    def score(jb, buf):
        k0 = pl.multiple_of(jb * tq, tq)
        for h in range(n_heads):
            buf[h] = _dot_nt(kaug[h, pl.ds(k0, tq), :], qa[h])

    def absorb(jb, ms, masked, buf):
        new_ms = []
        for h in range(n_heads):
            s = buf[h]
            if masked:
                s = jnp.where(keep, s, NEG)
            m_new = jnp.maximum(ms[h], jnp.max(s, axis=0, keepdims=True))
            alpha = jnp.exp2(ms[h] - m_new)
            p = jnp.exp2(s - m_new).astype(bf16)
            new_ms.append(m_new)
            acc[h] = acc[h] * alpha + _dot(vt[jb, h * V_ROWS:(h + 1) * V_ROWS, :], p)
        return tuple(new_ms)

    def pair(pp, ms):
        j0 = 2 * pp
        score(j0 + 1, sc1)
        ms = absorb(j0, ms, False, sc0)
        score(j0 + 2, sc0)
        return absorb(j0 + 1, ms, False, sc1)

    score(0, sc0)
    ms = lax.fori_loop(0, i // 2, pair, tuple(jnp.full((1, tq), NEG, f32) for _ in range(n_heads)))

    @pl.when(i % 2 == 0)
    def _even():
        absorb(i, ms, True, sc0)

    @pl.when(i % 2 == 1)
    def _odd():
        score(i, sc1)
        absorb(i, absorb(i - 1, ms, False, sc0), True, sc1)

    out_t = jnp.concatenate([acc[h, 0:HEAD_DIM, :] * (1.0 / acc[h, HEAD_DIM:HEAD_DIM + 1, :]) for h in range(n_heads)],
                            axis=0)
    ms_ = jnp.mean(out_t * out_t, axis=0, keepdims=True)
    out_t = out_t * lax.rsqrt(ms_ + EPS)
    o_ref[0] = (out_t.T * ng_ref[...]).astype(bf16)


def _piece_select():
    sel = np.zeros((4, 384, 128), np.float32)
    for pair in range(4):
        for j in range(3):
            sel[pair, j * 128 + 8 + 2 * pair, HEAD_DIM + j] = 1.0
            sel[pair, j * 128 + 8 + 2 * pair + 1, j] = 1.0
    return jnp.asarray(sel, bf16)


def _attn(qkv, cumc, norm_g):
    bsz, t, _ = qkv.shape
    tq = min(ATT_BLOCK, t)
    return pl.pallas_call(
        functools.partial(_attn_kernel, tq=tq, t=t),
        grid=(bsz, t // tq),
        in_specs=[pl.BlockSpec((1, tq, D_ATT), lambda b, i: (b, i, 0)),
                  pl.BlockSpec((1, t, D_ATT), lambda b, i: (b, 0, 1)),
                  pl.BlockSpec((1, t, D_ATT), lambda b, i: (b, 0, 2)),
                  pl.BlockSpec((1, t, 128), lambda b, i: (b, 0, 0)),
                  pl.BlockSpec((4, 384, 128), lambda b, i: (0, 0, 0)),
                  pl.BlockSpec((1, D_ATT), lambda b, i: (0, 0))],
        out_specs=pl.BlockSpec((1, tq, D_ATT), lambda b, i: (b, i, 0)),
        out_shape=jax.ShapeDtypeStruct((bsz, t, D_ATT), bf16),
        scratch_shapes=[pltpu.VMEM((D_ATT // HEAD_DIM, t, 128), bf16),
                        pltpu.VMEM((t // tq, (D_ATT // HEAD_DIM) * V_ROWS, tq), bf16),
                        pltpu.VMEM((D_ATT // HEAD_DIM, V_ROWS, tq), f32),
                        pltpu.VMEM((D_ATT // HEAD_DIM, tq, tq), f32),
                        pltpu.VMEM((D_ATT // HEAD_DIM, tq, tq), f32)],
        compiler_params=pltpu.CompilerParams(dimension_semantics=("parallel", "arbitrary"),
                                             vmem_limit_bytes=VMEM_LIMIT),
        name="attn",
    )(qkv, qkv, qkv, cumc, _piece_select(), norm_g)


def _layer_norm(y, g, b):
    mu = jnp.mean(y, axis=-1, keepdims=True)
    yc = y - mu
    var = jnp.mean(yc * yc, axis=-1, keepdims=True)
    return yc * lax.rsqrt(var + EPS) * g + b


def _outproj_kernel(ys_ref, ya_ref, x_ref, gt_ref, sc_ref, sh_ref, wo_ref, lng_ref, lnb_ref, wrh_ref, wrl_ref, rb_ref,
                    x1_ref, rt_ref, dest_ref, cnt_ref, tbl_ref, xrows_hbm,
                    carry, galloc, tbl, u2t, didx_v, didx_s, cnt_v, cnt_s, zeros, idx_sem, sc_sem, z_sem,
                    *, tm, blk, n_blk):
    i = pl.program_id(0)
    last = pl.num_programs(0) - 1
    cur = i % 2
    prev = 1 - cur

    def idx_ready(slot):
        return pltpu.make_async_copy(didx_v.at[slot], didx_s.at[slot], idx_sem.at[slot])

    def dispatched(slot):
        return pltpu.make_async_copy(u2t.at[slot], u2t.at[slot], sc_sem.at[slot])

    def dispatch_copy(slot, r, k):
        return pltpu.make_async_copy(u2t.at[slot, pl.ds(r * PACK_TILE, PACK_TILE)],
                                     xrows_hbm.at[pl.ds(didx_s[slot, k, r] * PACK_TILE, PACK_TILE)], sc_sem.at[slot])

    @pl.when(i == 0)
    def _init():
        carry[...] = jnp.zeros_like(carry)
        galloc[...] = jnp.zeros_like(galloc)
        tbl[...] = jnp.zeros_like(tbl)

    @pl.when(i >= 2)
    def _reuse():
        dispatched(cur).wait()
        dispatched(cur).wait()

    def step(dispatch_prev):
        n_parts = OUTPROJ_PARTS
        n_slices = 3 * n_parts + 2

        def dispatch_slice(c):
            if dispatch_prev:
                for r_ in range(c * tm // n_slices, (c + 1) * tm // n_slices):
                    dispatch_copy(prev, r_, 0).start(priority=0)
                    dispatch_copy(prev, r_, 1).start(priority=1)

        if dispatch_prev:
            idx_ready(prev).wait()
        part = tm // n_parts
        hs = []
        for a in range(n_parts):
            rs = slice(a * part, (a + 1) * part)
            dispatch_slice(a)
            hs.append(_dot(ys_ref[rs, :], wo_ref[0:D_SSM, :]) + _dot(ya_ref[rs, :], wo_ref[D_SSM:D, :]))
        logit_parts = []
        for a in range(n_parts):
            rs = slice(a * part, (a + 1) * part)
            dispatch_slice(n_parts + 2 * a)
            x1 = _layer_norm(ALPHA * x_ref[rs, :] + (1.0 + gt_ref[0, 0]) * hs[a], lng_ref[...], lnb_ref[...])
            x1_ref[rs, :] = x1
            u2 = x1 * (1.0 + sc_ref[0, 0]) + sh_ref[0, 0]
            uh = u2.astype(bf16)
            _store_packed_rows(u2t.at[cur, pl.ds(a * part * PACK_TILE, part * PACK_TILE)], _pack_rows(uh))
            dispatch_slice(n_parts + 2 * a + 1)
            ul = (u2 - uh.astype(f32)).astype(bf16)
            logit_parts.append((_dot(uh, wrh_ref[...]) + _dot(ul, wrh_ref[...])) + _dot(uh, wrl_ref[...]))
        logits = jnp.concatenate(logit_parts, axis=0) + rb_ref[...]
        lane = lax.broadcasted_iota(i32, (tm, 128), 1).astype(f32)
        big = jnp.float32(1e9)

        def first_max(vals):
            m = jnp.max(vals, axis=-1, keepdims=True)
            return m, jnp.min(jnp.where(vals == m, lane, big), axis=-1, keepdims=True)

        gl = jnp.where(lane < N_GROUPS_R, logits, NEG)
        gmax, gidx = first_max(gl)
        g_p = 1.0 / jnp.sum(jnp.exp(gl - gmax), axis=-1, keepdims=True)
        lo = N_GROUPS_R + EXPERTS_PER_GROUP * gidx
        el = jnp.where((lane >= lo) & (lane < lo + EXPERTS_PER_GROUP), logits, NEG)
        m1, i1 = first_max(el)
        el2 = jnp.where(lane == i1, NEG, el)
        m2, i2 = first_max(el2)
        r = jnp.exp(m2 - m1)
        w1 = g_p / (1.0 + r)
        w2 = g_p * r / (1.0 + r)
        dispatch_slice(3 * n_parts)

        oh1 = lane == i1
        oh2 = lane == i2
        oh = jnp.where(oh1 | oh2, 1.0, 0.0)
        r_i = lax.broadcasted_iota(i32, (tm, tm), 0)
        c_i = lax.broadcasted_iota(i32, (tm, tm), 1)
        lower = jnp.where(r_i > c_i, 1.0, 0.0).astype(bf16)
        c_old = carry[...]
        prefix = _dot(lower, oh.astype(bf16)) + c_old
        rank1 = jnp.sum(jnp.where(oh1, prefix, 0.0), axis=-1, keepdims=True)
        rank2 = jnp.sum(jnp.where(oh2, prefix, 0.0), axis=-1, keepdims=True)
        c_new = c_old + jnp.sum(oh, axis=0, keepdims=True)
        carry[...] = c_new
        dispatch_slice(3 * n_parts + 1)

        nb_old = jnp.floor((c_old + (blk - 1)) * (1.0 / blk))
        nb_new = jnp.floor((c_new + (blk - 1)) * (1.0 / blk))
        fresh = nb_new - nb_old
        sq_r = lax.broadcasted_iota(i32, (128, 128), 0)
        sq_c = lax.broadcasted_iota(i32, (128, 128), 1)
        before = jnp.where(sq_r < sq_c, 1.0, 0.0).astype(bf16)
        base = galloc[...] + _dot(jnp.broadcast_to(fresh, (8, 128)).astype(bf16), before)[0:1, :]
        galloc[...] = galloc[...] + jnp.sum(fresh, axis=-1, keepdims=True)
        ordinal = sq_r.astype(f32)
        tbl[...] = jnp.where((ordinal >= nb_old) & (ordinal < nb_new), base + (ordinal - nb_old), tbl[...])
        tbl_ref[...] = tbl[...]
        cnt_ref[...] = jnp.concatenate([c_new, galloc[...], jnp.zeros((6, 128), f32)], axis=0)

        rt = jnp.where(lane == 4, w1, jnp.where(lane == 5, w2, 0.0))
        rt_ref[...] = rt

        ids_bf = tbl[...].astype(bf16)

        def row_of(onehot, rank):
            nth = jnp.floor(rank * (1.0 / blk))
            ids = _dot(jnp.where(lane == nth, 1.0, 0.0).astype(bf16), ids_bf)
            return jnp.sum(jnp.where(onehot, ids, 0.0), axis=-1, keepdims=True) * blk + (rank - nth * blk)

        d1 = row_of(oh1, rank1)
        d2 = row_of(oh2, rank2)
        dd = jnp.where(lane == 0, d1, jnp.where(lane == 1, d2, 0.0)).T[0:8, :].astype(i32)
        dest_ref[...] = dd
        didx_v[cur] = dd
        idx_ready(cur).start()

    pl.when(i == 0)(lambda: step(False))
    pl.when(i > 0)(lambda: step(True))

    @pl.when(i == last)
    def _epilogue():
        idx_ready(cur).wait()

        def issue(r_, c):
            dispatch_copy(cur, r_, 0).start()
            dispatch_copy(cur, r_, 1).start()
            return c
        lax.fori_loop(0, tm, issue, 0)

        cnt = carry[...]
        n_blocks = jnp.floor((cnt + (blk - 1)) * (1.0 / blk))
        ordinal = lax.broadcasted_iota(i32, (128, 128), 0).astype(f32)
        last_id = jnp.sum(jnp.where(ordinal == n_blocks - 1.0, tbl[...], 0.0), axis=0, keepdims=True)
        used = cnt - (n_blocks - 1.0) * blk
        cnt_v[...] = jnp.concatenate([last_id * blk + used, blk - used, galloc[...], jnp.zeros((5, 128), f32)],
                                     axis=0).astype(i32)
        to_smem = pltpu.make_async_copy(cnt_v, cnt_s, z_sem)
        to_smem.start()
        to_smem.wait()
        zeros[...] = jnp.zeros_like(zeros)
        sizes = [1 << b for b in reversed(range(blk.bit_length() - 1))]

        def for_each_fill(fn):
            def tail(e, c):
                pad = cnt_s[1, N_GROUPS_R + e]
                off = cnt_s[0, N_GROUPS_R + e]
                for sz in sizes:
                    @pl.when((pad & sz) != 0)
                    def _(off=off, sz=sz):
                        fn(pltpu.make_async_copy(zeros.at[pl.ds(0, sz * PACK_TILE)],
                                                 xrows_hbm.at[pl.ds(off * PACK_TILE, sz * PACK_TILE)], z_sem))
                    off = off + (pad & sz)
                return c
            lax.fori_loop(0, N_EXPERTS, tail, 0)

            def whole(b, c):
                fn(pltpu.make_async_copy(zeros, xrows_hbm.at[pl.ds(b * blk * PACK_TILE, blk * PACK_TILE)], z_sem))
                return c
            lax.fori_loop(cnt_s[2, 0], n_blk, whole, 0)

        for_each_fill(lambda copy: copy.start())
        for_each_fill(lambda copy: copy.wait())

        @pl.when(i > 0)
        def _prev_done():
            dispatched(prev).wait()
            dispatched(prev).wait()
        dispatched(cur).wait()
        dispatched(cur).wait()


def _outproj(y_ssm, y_att, x, mod4, w_out, ln_g, ln_b, wr_hi, wr_lo, rb, t, blk, n_blk):
    n = x.shape[0]
    tm = min(OUTPROJ_ROWS, t)
    nt = t // tm
    assert n // blk + 1 <= 128 and n_blk <= 256, "block-id table: 128 blocks per expert, ids exact in bf16"
    vec = lambda k: pl.BlockSpec((1, 1, 1, D), lambda i, k=k: (k, i // nt, 0, 0))
    full = lambda shape: pl.BlockSpec(shape, lambda i: (0,) * len(shape))
    rows = lambda w: pl.BlockSpec((tm, w), lambda i: (i, 0))
    return pl.pallas_call(
        functools.partial(_outproj_kernel, tm=tm, blk=blk, n_blk=n_blk),
        grid=(n // tm,),
        in_specs=[rows(D_SSM), rows(D_ATT), rows(D), vec(2), vec(4), vec(3),
                  full((D, D)), full((1, D)), full((1, D)), full((D, 128)), full((D, 128)), full((1, 128))],
        out_specs=[rows(D), rows(128), pl.BlockSpec((8, tm), lambda i: (0, i)), full((8, 128)), full((128, 128)),
                   pl.BlockSpec(memory_space=pl.ANY)],
        out_shape=[jax.ShapeDtypeStruct((n, D), f32), jax.ShapeDtypeStruct((n, 128), f32),
                   jax.ShapeDtypeStruct((8, n), i32), jax.ShapeDtypeStruct((8, 128), f32),
                   jax.ShapeDtypeStruct((128, 128), f32),
                   jax.ShapeDtypeStruct((n_blk * blk * PACK_TILE, 128), u32)],
        scratch_shapes=[pltpu.VMEM((1, 128), f32), pltpu.VMEM((1, 128), f32), pltpu.VMEM((128, 128), f32),
                        pltpu.VMEM((2, tm * PACK_TILE, 128), u32),
                        pltpu.VMEM((2, 8, tm), i32), pltpu.SMEM((2, 8, tm), i32),
                        pltpu.VMEM((8, 128), i32), pltpu.SMEM((8, 128), i32),
                        pltpu.VMEM((blk * PACK_TILE, 128), u32),
                        pltpu.SemaphoreType.DMA((2,)), pltpu.SemaphoreType.DMA((2,)), pltpu.SemaphoreType.DMA(())],
        compiler_params=pltpu.CompilerParams(dimension_semantics=("arbitrary",), vmem_limit_bytes=VMEM_LIMIT),
        name="outproj",
    )(y_ssm, y_att, x, mod4, mod4, mod4, w_out, ln_g, ln_b, wr_hi, wr_lo, rb)


def _tile_copy(src_hbm, row, buf, slot, sem):
    return pltpu.make_async_copy(src_hbm.at[row], buf.at[pl.ds(slot * ROW_TILE, ROW_TILE)], sem)


def _moe_kernel(be_ref, br_ref, nu_ref, nxt_ref, par_ref, nv_ref, x_ref, wg_hbm, wu_hbm, wd_hbm, y_ref,
                wfg, wfu, wfd, wgb, wub, wdb, wsem, *, blk):
    i = pl.program_id(0)
    used = i < nu_ref[0]

    def fetch(e, slot):
        return [pltpu.make_async_copy(src.at[e], dst.at[slot], wsem.at[slot])
                for src, dst in ((wg_hbm, wfg), (wu_hbm, wfu), (wd_hbm, wfd))]

    @pl.when(i == 0)
    def _first():
        for c in fetch(be_ref[0], par_ref[0]):
            c.start()

    @pl.when(jnp.logical_and(used, jnp.logical_or(i == 0, be_ref[i] != be_ref[jnp.maximum(i - 1, 0)])))
    def _switch():
        slot = par_ref[i]
        for c in fetch(be_ref[i], slot):
            c.wait()

        @pl.when(nxt_ref[i] >= 0)
        def _next():
            for c in fetch(nxt_ref[i], 1 - slot):
                c.start(priority=1)
        wgb[...] = wfg[slot].astype(bf16)
        wub[...] = wfu[slot].astype(bf16)
        wdb[...] = wfd[slot].astype(bf16)

    @pl.when(jnp.logical_not(used))
    def _spare():
        y_ref[...] = jnp.zeros_like(y_ref)

    def mlp(rows):
        xa, xb = _load_packed_rows(x_ref.at[pl.ds(0, rows * PACK_TILE)], rows)
        half = D // 2
        gate = _dot(xa, wgb[0:half, :]) + _dot(xb, wgb[half:D, :])
        up = _dot(xa, wub[0:half, :]) + _dot(xb, wub[half:D, :])
        hid = (_silu(gate) * up).astype(bf16)
        _store_row_tiles(y_ref.at[pl.ds(0, rows * ROW_TILE)], _dot(hid, wdb[...]))

    sparse = nv_ref[i] <= blk // 2

    @pl.when(jnp.logical_and(used, jnp.logical_not(sparse)))
    def _full():
        mlp(blk)

    @pl.when(jnp.logical_and(used, sparse))
    def _half():
        mlp(blk // 2)
        y_ref[pl.ds(blk // 2 * ROW_TILE, blk // 2 * ROW_TILE), :] = jnp.zeros((blk // 2 * ROW_TILE, 128), f32)


def _moe(block_e, block_row, n_used, next_e, parity, n_valid, x_rows, w_gate, w_up, w_down, blk):
    nblk = block_e.shape[0]
    rspec = pl.BlockSpec((blk * ROW_TILE, 128), lambda i, be, br, *_: (br[i], 0))
    xspec = pl.BlockSpec((blk * PACK_TILE, 128), lambda i, be, br, nu, *_: (br[jnp.minimum(i, nu[0] - 1)], 0))
    anyspec = pl.BlockSpec(memory_space=pl.ANY)
    grid_spec = pltpu.PrefetchScalarGridSpec(
        num_scalar_prefetch=6,
        grid=(nblk,),
        in_specs=[xspec, anyspec, anyspec, anyspec],
        out_specs=rspec,
        scratch_shapes=[pltpu.VMEM((2, D, D_EXPERT), f32), pltpu.VMEM((2, D, D_EXPERT), f32),
                        pltpu.VMEM((2, D_EXPERT, D), f32),
                        pltpu.VMEM((D, D_EXPERT), bf16), pltpu.VMEM((D, D_EXPERT), bf16),
                        pltpu.VMEM((D_EXPERT, D), bf16),
                        pltpu.SemaphoreType.DMA((2,))],
    )
    return pl.pallas_call(
        functools.partial(_moe_kernel, blk=blk),
        grid_spec=grid_spec,
        out_shape=jax.ShapeDtypeStruct((nblk * blk * ROW_TILE, 128), f32),
        compiler_params=pltpu.CompilerParams(dimension_semantics=("arbitrary",), vmem_limit_bytes=VMEM_LIMIT),
        name="moe",
    )(block_e, block_row, n_used, next_e, parity, n_valid, x_rows, w_gate, w_up, w_down)


def _combine_kernel(d1_ref, d2_ref, y_hbm, x1_ref, rt_ref, gt_ref, lng_ref, lnb_ref, o_ref, buf, sem, *, tm):
    i = pl.program_id(0)
    last = pl.num_programs(0) - 1

    @pl.when(i == 0)
    def _prologue():
        for s in range(2):
            def issue(r, c, s=s):
                base = jnp.minimum(s, last) * tm
                _tile_copy(y_hbm, d1_ref[base + r], buf.at[s, 0], r, sem.at[s]).start()
                _tile_copy(y_hbm, d2_ref[base + r], buf.at[s, 1], r, sem.at[s]).start()
                return c
            lax.fori_loop(0, tm, issue, 0)

    def rows_ready(slot):
        return pltpu.make_async_copy(buf.at[slot], buf.at[slot], sem.at[slot])

    slot = i % 3
    nxt = (i + 2) % 3
    base = jnp.minimum(i + 2, last) * tm
    rows_ready(slot).wait()
    for r in range(tm):
        _tile_copy(y_hbm, d1_ref[base + r], buf.at[nxt, 0], r, sem.at[nxt]).start(priority=0)
        _tile_copy(y_hbm, d2_ref[base + r], buf.at[nxt, 1], r, sem.at[nxt]).start(priority=1)
    rt = rt_ref[...]
    moe = rt[:, 4:5] * _load_row_tiles(buf.at[slot, 0], tm) + rt[:, 5:6] * _load_row_tiles(buf.at[slot, 1], tm)
    y = ALPHA * x1_ref[...] + (1.0 + gt_ref[0, 0]) * moe
    o_ref[...] = _layer_norm(y, lng_ref[...], lnb_ref[...])

    @pl.when(i == last)
    def _drain():
        rows_ready((i + 1) % 3).wait()
        rows_ready((i + 2) % 3).wait()


def _combine(dest1, dest2, y_rows, x1, rt, mod4, ln_g, ln_b, t):
    n = x1.shape[0]
    tm = min(COMBINE_ROWS, t)
    nt = t // tm
    full = lambda shape: pl.BlockSpec(shape, lambda i, d1, d2: (0,) * len(shape))
    rows = lambda w: pl.BlockSpec((tm, w), lambda i, d1, d2: (i, 0))
    grid_spec = pltpu.PrefetchScalarGridSpec(
        num_scalar_prefetch=2,
        grid=(n // tm,),
        in_specs=[pl.BlockSpec(memory_space=pl.ANY), rows(D), rows(128),
                  pl.BlockSpec((1, 1, 1, D), lambda i, d1, d2: (5, i // nt, 0, 0)),
                  full((1, D)), full((1, D))],
        out_specs=rows(D),
        scratch_shapes=[pltpu.VMEM((3, 2, tm * ROW_TILE, 128), f32), pltpu.SemaphoreType.DMA((3,))],
    )
    return pl.pallas_call(
        functools.partial(_combine_kernel, tm=tm),
        grid_spec=grid_spec,
        out_shape=jax.ShapeDtypeStruct((n, D), f32),
        compiler_params=pltpu.CompilerParams(dimension_semantics=("arbitrary",), vmem_limit_bytes=VMEM_LIMIT),
        name="combine",
    )(dest1, dest2, y_rows, x1, rt, mod4, ln_g, ln_b)


def _hi_lo(w):
    hi = w.astype(bf16)
    return hi, (w - hi.astype(f32)).astype(bf16)


def _pad_lanes(v, offset, width):
    return jnp.zeros((width,), f32).at[offset:offset + v.shape[0]].set(v)


def _layer(x, c, ada_w, ada_b, w_in, conv_w, conv_b, dt_bias, a_log, d_skip, ssm_norm_g, fg_bias, att_norm_g,
           w_out, ln1_g, ln1_b, router_g_w, router_g_b, router_e_w, router_e_b, w_gate, w_up, w_down, ln2_g, ln2_b):
    bsz, t, _ = x.shape
    n = bsz * t

    mod4 = _ada(c, ada_w, ada_b).reshape(6, bsz, 1, D)

    w_main, w_small = _pack_w_in(w_in)
    zx, bc, qkv, dtf = _inproj(x, mod4, w_main, w_small, conv_w, conv_b.reshape(1, -1))

    pc = jnp.stack([_pad_lanes(dt_bias, 0, 128), _pad_lanes(a_log, 0, 128), _pad_lanes(fg_bias, 8, 128)]
                   + [jnp.zeros((128,), f32)] * 5)
    rep = lambda v: jnp.repeat(v, HEAD_DIM)
    pe = jnp.stack([rep(dt_bias), rep(a_log), rep(d_skip), ssm_norm_g] + [jnp.zeros((D_SSM,), f32)] * 4)
    y_ssm, cumc = _ssd(zx, bc, dtf, pc, pe)

    y_att = _attn(qkv, cumc, att_norm_g.reshape(1, -1))

    wr = jnp.concatenate([router_g_w, router_e_w, jnp.zeros((D, 128 - N_GROUPS_R - N_EXPERTS), f32)], axis=1)
    wr_hi, wr_lo = _hi_lo(wr)
    rb = jnp.concatenate([router_g_b, router_e_b, jnp.zeros((128 - N_GROUPS_R - N_EXPERTS,), f32)]).reshape(1, 128)
    blk = MOE_ROWS
    nblk = (2 * n) // blk + N_EXPERTS
    x1, rt, dest, cnt, tbl, x_rows = _outproj(y_ssm.reshape(n, D_SSM), y_att.reshape(n, D_ATT), x.reshape(n, D), mod4,
                                              w_out.astype(bf16), ln1_g.reshape(1, D), ln1_b.reshape(1, D),
                                              wr_hi, wr_lo, rb, t, blk, nblk)

    counts = cnt[0, N_GROUPS_R:N_GROUPS_R + N_EXPERTS].astype(i32)
    nb = (counts + blk - 1) // blk
    nb_end = jnp.cumsum(nb)
    n_used = nb_end[N_EXPERTS - 1:]
    every = jnp.arange(nblk, dtype=i32)
    step = jnp.minimum(every, n_used[0] - 1)
    owner = (nb_end[None, :] <= step[:, None]).astype(i32)
    block_e = jnp.sum(owner, axis=1)
    first = jnp.sum(owner * nb[None, :], axis=1)
    lanes = jnp.arange(128, dtype=i32)[None, :]
    nth_row = jnp.dot((lanes == (step - first)[:, None]).astype(f32), tbl, precision=lax.Precision.HIGHEST)
    block_id = jnp.sum(jnp.where(lanes == (block_e + N_GROUPS_R)[:, None], nth_row, 0.0), axis=1).astype(i32)
    block_row = jnp.where(every < n_used[0], block_id, every)
    run_end = first + jnp.sum((jnp.arange(N_EXPERTS)[None, :] == block_e[:, None]) * nb[None, :], axis=1)
    next_e = jnp.where(run_end < n_used[0], jnp.sum((nb_end[None, :] <= run_end[:, None]).astype(i32), axis=1), -1)
    parity = jnp.sum(((jnp.arange(N_EXPERTS)[None, :] < block_e[:, None]) & (nb[None, :] > 0)).astype(i32), axis=1) % 2

    own = (jnp.arange(N_EXPERTS)[None, :] == block_e[:, None]).astype(i32)
    n_valid = jnp.clip(jnp.sum(own * counts[None, :], axis=1) - (step - first) * blk, 0, blk)
    y_rows = _moe(block_e, block_row, n_used, next_e, parity, n_valid, x_rows,
                  w_gate, w_up, w_down, blk)
    out = _combine(dest[0], dest[1], y_rows.reshape(nblk * blk, ROW_TILE, 128), x1, rt, mod4,
                   ln2_g.reshape(1, D), ln2_b.reshape(1, D), t)
    return out.reshape(bsz, t, D)


def kernel(x, c, ada_w, ada_b, w_in, conv_w, conv_b, dt_bias, a_log, d_skip, ssm_norm_g, fg_bias, att_norm_g, w_out,
           ln1_g, ln1_b, router_g_w, router_g_b, router_e_w, router_e_b, w_gate, w_up, w_down, ln2_g, ln2_b):
    depth = ada_w.shape[0]
    for l in range(depth):
        x = _layer(x, c, ada_w[l], ada_b[l], w_in[l], conv_w[l], conv_b[l], dt_bias[l], a_log[l], d_skip[l],
                   ssm_norm_g[l], fg_bias[l], att_norm_g[l], w_out[l], ln1_g[l], ln1_b[l], router_g_w[l],
                   router_g_b[l], router_e_w[l], router_e_b[l], w_gate[l], w_up[l], w_down[l], ln2_g[l], ln2_b[l])
    return x
```

```python
import functools

import jax
import jax.numpy as jnp
import numpy as np
from jax import lax
from jax.experimental import pallas as pl
from jax.experimental.pallas import tpu as pltpu

f32 = jnp.float32
bf16 = jnp.bfloat16
i32 = jnp.int32

D = 1024
D_SSM = 512
D_ATT = 512
HEAD_DIM = 64
GROUP_W = 256
N_STATE = 128
CONV_K = 4
N_GROUPS_R = 4
EXPERTS_PER_GROUP = 8
N_EXPERTS = 32
D_EXPERT = 512
ALPHA = 2.0 ** 0.25
EPS = 1e-5
NEG = -1e30
LOG2E = 1.4426950408889634
QK_SCALE = HEAD_DIM ** -0.5 * LOG2E
V_ROWS = 80

SSD_CHUNK = 256
ATT_BLOCK = 256
INPROJ_ROWS = 512
OUTPROJ_ROWS = 512
OUTPROJ_PARTS = 2
MOE_ROWS = 512
COMBINE_ROWS = 128
VMEM_LIMIT = 48 * 1024 * 1024


def _dot(a, b):
    return jnp.dot(a, b, preferred_element_type=f32)


def _dot_nt(a, b):
    return lax.dot_general(a, b, (((1,), (1,)), ((), ())), preferred_element_type=f32)


def _dot_tn(a, b):
    return lax.dot_general(a, b, (((0,), (0,)), ((), ())), preferred_element_type=f32)


def _split3(v):
    hi = v.astype(bf16)
    r1 = v - hi.astype(f32)
    mid = r1.astype(bf16)
    lo = (r1 - mid.astype(f32)).astype(bf16)
    return hi, mid, lo


def _dot_exact_lhs(m, v):
    hi, mid, lo = _split3(v)
    return (_dot(m, hi) + _dot(m, mid)) + _dot(m, lo)


def _dot_exact_rhs(v, m):
    hi, mid, lo = _split3(v)
    return (_dot(hi, m) + _dot(mid, m)) + _dot(lo, m)


ROW_TILE = 8


def _store_row_tiles(ref, val):
    rows = val.shape[0]
    for c in range(ROW_TILE):
        ref[pl.ds(c, rows, stride=ROW_TILE), :] = val[:, c * 128:(c + 1) * 128]


def _load_row_tiles(ref, rows):
    return jnp.concatenate([ref[pl.ds(c, rows, stride=ROW_TILE), :] for c in range(ROW_TILE)], axis=1)


PACK_TILE = 4
u32 = jnp.uint32


def _pack_rows(vb):
    lo = pltpu.bitcast(vb[:, 0:D // 2].astype(f32), u32) >> 16
    hi = pltpu.bitcast(vb[:, D // 2:D].astype(f32), u32) & jnp.uint32(0xFFFF0000)
    return lo | hi


def _store_packed_rows(ref, packed):
    rows = packed.shape[0]
    for c in range(PACK_TILE):
        ref[pl.ds(c, rows, stride=PACK_TILE), :] = packed[:, c * 128:(c + 1) * 128]


def _load_packed_rows(ref, rows):
    words = [ref[pl.ds(c, rows, stride=PACK_TILE), :] for c in range(PACK_TILE)]
    lo = jnp.concatenate([pltpu.bitcast(w << 16, f32) for w in words], axis=1)
    hi = jnp.concatenate([pltpu.bitcast(w & jnp.uint32(0xFFFF0000), f32) for w in words], axis=1)
    return lo.astype(bf16), hi.astype(bf16)


def _softplus(x):
    return jnp.maximum(x, 0.0) + jnp.log1p(jnp.exp(-jnp.abs(x)))


def _silu(x):
    return x * jax.nn.sigmoid(x)


def _ada_kernel(c_ref, w_ref, b_ref, o_ref):
    s = _silu(c_ref[...]).astype(bf16)
    o_ref[0] = _dot(s, w_ref[...].astype(bf16)) + b_ref[0]


def _ada(c, w, b):
    bsz = c.shape[0]
    return pl.pallas_call(
        _ada_kernel,
        grid=(6,),
        in_specs=[pl.BlockSpec((bsz, D), lambda j: (0, 0)),
                  pl.BlockSpec((D, D), lambda j: (0, j)),
                  pl.BlockSpec((1, 1, D), lambda j: (j, 0, 0))],
        out_specs=pl.BlockSpec((1, bsz, D), lambda j: (j, 0, 0)),
        out_shape=jax.ShapeDtypeStruct((6, bsz, D), f32),
        compiler_params=pltpu.CompilerParams(dimension_semantics=("arbitrary",), vmem_limit_bytes=VMEM_LIMIT),
        name="ada",
    )(c, w, b.reshape(6, 1, D))


W_IN_COLS = 3088
PACK_ROWS = 256


def _pack_w_in_kernel(wt_ref, wm_ref, ws_ref):
    for j in range(3072 // PACK_ROWS):
        src = j * PACK_ROWS if j * PACK_ROWS < 1536 else j * PACK_ROWS + 8
        wm_ref[:, j * PACK_ROWS:(j + 1) * PACK_ROWS] = wt_ref[src:src + PACK_ROWS, :].T.astype(bf16)
    small = jnp.concatenate([wt_ref[1536:1544, :], wt_ref[W_IN_COLS - 8:W_IN_COLS, :], jnp.zeros((112, D), f32)],
                            axis=0).T
    hi = small.astype(bf16)
    ws_ref[:, 0:128] = hi
    ws_ref[:, 128:256] = (small - hi.astype(f32)).astype(bf16)


def _pack_w_in(w_in):
    full = lambda shape: pl.BlockSpec(shape, lambda i: (0,) * len(shape))
    return pl.pallas_call(
        _pack_w_in_kernel,
        grid=(1,),
        in_specs=[full((W_IN_COLS, D))],
        out_specs=[full((D, 3072)), full((D, 256))],
        out_shape=[jax.ShapeDtypeStruct((D, 3072), bf16), jax.ShapeDtypeStruct((D, 256), bf16)],
        compiler_params=pltpu.CompilerParams(dimension_semantics=("arbitrary",), vmem_limit_bytes=VMEM_LIMIT),
        name="pack_w_in",
    )(w_in.T)


def _inproj_kernel(x_ref, sc_ref, sh_ref, wm_ref, ws_ref, cw_ref, cb_ref, zx_ref, bc_ref, qkv_ref, dtf_ref, xcat, *, tm):
    i = pl.program_id(1)

    @pl.when(i == 0)
    def _init():
        xcat[0:8, :] = jnp.zeros((8, 2 * D_SSM), f32)

    u = x_ref[0] * (1.0 + sc_ref[0, 0]) + sh_ref[0, 0]
    ub = u.astype(bf16)
    xcat[8:8 + tm, 0:512] = _dot(ub, wm_ref[:, 512:1024])
    xcat[8:8 + tm, 512:1024] = _dot(ub, wm_ref[:, 1024:1536])
    zx_ref[0, :, 0:512] = _dot(ub, wm_ref[:, 0:512])
    acc = cw_ref[0:1, :] * xcat[5:5 + tm, :] + cb_ref[...]
    for k in range(1, CONV_K):
        acc = acc + cw_ref[k:k + 1, :] * xcat[5 + k:5 + k + tm, :]
    xcat[0:8, :] = xcat[tm:tm + 8, :]
    xbc = _silu(acc)
    zx_ref[0, :, 512:1024] = xbc[:, 0:512]
    bc_ref[0] = xbc[:, 512:1024].astype(bf16)
    for j, scale in enumerate((QK_SCALE, 1.0, 1.0)):
        qkv_ref[0, :, j * 512:(j + 1) * 512] = (
            _dot(ub, wm_ref[:, 1536 + j * 512:1536 + (j + 1) * 512]) * scale).astype(bf16)
    ul = (u - ub.astype(f32)).astype(bf16)
    d_hl = _dot(ub, ws_ref[...])
    dtf_ref[0] = (d_hl[:, 0:128] + _dot(ul, ws_ref[:, 0:128])) + d_hl[:, 128:256]


def _inproj(x, mod4, w_main, ws, conv_w, conv_b):
    bsz, t, _ = x.shape
    tm = min(INPROJ_ROWS, t)
    vec = lambda k: pl.BlockSpec((1, 1, 1, D), lambda b, i, k=k: (k, b, 0, 0))
    full = lambda shape: pl.BlockSpec(shape, lambda b, i: (0,) * len(shape))
    rows = lambda w: pl.BlockSpec((1, tm, w), lambda b, i: (b, i, 0))
    return pl.pallas_call(
        functools.partial(_inproj_kernel, tm=tm),
        grid=(bsz, t // tm),
        in_specs=[rows(D), vec(1), vec(0), full((D, 3072)), full((D, 256)),
                  full((CONV_K, 2 * D_SSM)), full((1, 2 * D_SSM))],
        out_specs=[rows(1024), rows(512), rows(1536), rows(128)],
        out_shape=[jax.ShapeDtypeStruct((bsz, t, 1024), f32),
                   jax.ShapeDtypeStruct((bsz, t, 512), bf16),
                   jax.ShapeDtypeStruct((bsz, t, 1536), bf16),
                   jax.ShapeDtypeStruct((bsz, t, 128), f32)],
        scratch_shapes=[pltpu.VMEM((tm + 8, 2 * D_SSM), f32)],
        compiler_params=pltpu.CompilerParams(dimension_semantics=("parallel", "arbitrary"),
                                             vmem_limit_bytes=VMEM_LIMIT),
        name="inproj",
    )(x, mod4, mod4, w_main, ws, conv_w, conv_b)


def _ssd_kernel(z_ref, xs_ref, bc_ref, dtf_ref, pc_ref, pe_ref, y_ref, cumc_ref, state, carry, *, lc):
    j = pl.program_id(1)

    @pl.when(j == 0)
    def _init():
        state[...] = jnp.zeros_like(state)
        carry[...] = jnp.zeros_like(carry)

    xs = xs_ref[0]
    bm = bc_ref[0, :, 0:256]
    cm = bc_ref[0, :, 256:512]

    dtf = dtf_ref[0]
    lane = lax.broadcasted_iota(i32, (lc, 128), 1)
    dt_c = _softplus(dtf + pc_ref[0:1, :])
    a_c = dt_c * (-jnp.exp(pc_ref[1:2, :]))
    logf = -_softplus(-(dtf + pc_ref[2:3, :]))
    v = jnp.where(lane < 8, a_c, logf)
    r_i = lax.broadcasted_iota(i32, (lc, lc), 0)
    c_i = lax.broadcasted_iota(i32, (lc, lc), 1)
    tri = r_i >= c_i
    tri_b = jnp.where(tri, 1.0, 0.0).astype(bf16)
    cum = _dot_exact_lhs(tri_b, v) + carry[...]
    carry[...] = jnp.where(lane[0:1, :] >= 8, cum[lc - 1:lc, :], 0.0)
    cumc_ref[0] = cum
    cs_t = cum.T[0:8, :]

    e_r = lax.broadcasted_iota(i32, (128, D_SSM), 0)
    e_c = lax.broadcasted_iota(i32, (128, D_SSM), 1)
    expand = jnp.where(jnp.right_shift(e_c, 6) == e_r, 1.0, 0.0).astype(bf16)
    dt_e = _dot_exact_rhs(dt_c, expand)
    cs_e = _dot_exact_rhs(cum, expand)

    xdt = xs * dt_e
    ecs = jnp.exp(cs_e)
    cs_last = cs_e[lc - 1:lc, :]
    dec_st = jnp.exp(cs_last - cs_e)
    lane_g = lax.broadcasted_iota(i32, (1, GROUP_W), 1)
    ys = []
    for g in range(2):
        gs = slice(g * GROUP_W, (g + 1) * GROUP_W)
        bg = bm[:, g * N_STATE:(g + 1) * N_STATE]
        cg = cm[:, g * N_STATE:(g + 1) * N_STATE]
        cb = _dot_nt(cg, bg)
        xdt_g = xdt[:, gs]
        xdt_gb = xdt_g.astype(bf16)
        ms, xb = [], []
        for hh in range(4):
            h = g * 4 + hh
            lm = jnp.exp(jnp.where(tri, cum[:, h:h + 1] - cs_t[h:h + 1, :], -jnp.inf))
            ms.append((cb * lm).astype(bf16))
            xb.append(jnp.where(jnp.right_shift(lane_g, 6) == hh, xdt_gb, jnp.zeros_like(xdt_gb)))
        y_diag = _dot(jnp.concatenate(ms, axis=1), jnp.concatenate(xb, axis=0))
        st = state[g]
        y_off = _dot(cg, st.astype(bf16)) * ecs[:, gs]
        upd = _dot_tn(bg, (xdt_g * dec_st[:, gs]).astype(bf16))
        state[g] = st * jnp.exp(cs_last[:, gs]) + upd
        ys.append(y_diag + y_off + xs[:, gs] * pe_ref[2:3, gs])

    outs = []
    for g in range(2):
        gs = slice(g * GROUP_W, (g + 1) * GROUP_W)
        yg = ys[g] * _silu(z_ref[0, :, gs])
        ms_ = jnp.mean(yg * yg, axis=-1, keepdims=True)
        outs.append(yg * lax.rsqrt(ms_ + EPS))
    y_ref[0] = (jnp.concatenate(outs, axis=1) * pe_ref[3:4, :]).astype(bf16)


def _ssd(zx, bc, dtf, pc, pe):
    bsz, t, _ = zx.shape
    lc = min(SSD_CHUNK, t)
    col = lambda k: pl.BlockSpec((1, lc, 512), lambda b, j, k=k: (b, j, k))
    full = lambda shape: pl.BlockSpec(shape, lambda b, j: (0,) * len(shape))
    return pl.pallas_call(
        functools.partial(_ssd_kernel, lc=lc),
        grid=(bsz, t // lc),
        in_specs=[col(0), col(1), col(0),
                  pl.BlockSpec((1, lc, 128), lambda b, j: (b, j, 0)),
                  full((8, 128)), full((8, D_SSM))],
        out_specs=[pl.BlockSpec((1, lc, D_SSM), lambda b, j: (b, j, 0)),
                   pl.BlockSpec((1, lc, 128), lambda b, j: (b, j, 0))],
        out_shape=[jax.ShapeDtypeStruct((bsz, t, D_SSM), bf16),
                   jax.ShapeDtypeStruct((bsz, t, 128), f32)],
        scratch_shapes=[pltpu.VMEM((2, N_STATE, GROUP_W), f32),
                        pltpu.VMEM((1, 128), f32)],
        compiler_params=pltpu.CompilerParams(dimension_semantics=("parallel", "arbitrary"),
                                             vmem_limit_bytes=VMEM_LIMIT),
        name="ssd",
    )(zx, zx, bc, dtf, pc, pe)


def _attn_kernel(q_ref, k_ref, v_ref, cc_ref, psel_ref, ng_ref, o_ref, kaug, vt, acc, sc0, sc1, *, tq, t):
    i = pl.program_id(1)
    nkb = t // tq
    n_heads = D_ATT // HEAD_DIM
    lane = lax.broadcasted_iota(i32, (1, 128), 1)
    lo_half = lane < HEAD_DIM

    @pl.when(i == 0)
    def _build():
        eye = jnp.where(lax.broadcasted_iota(i32, (D_ATT, D_ATT), 0) == lax.broadcasted_iota(i32, (D_ATT, D_ATT), 1),
                        1.0, 0.0).astype(bf16)
        ones_rows = jnp.where(lax.broadcasted_iota(i32, (V_ROWS - HEAD_DIM, tq), 0) == 0, 1.0, 0.0).astype(bf16)
        for jb in range(nkb):
            rows = slice(jb * tq, (jb + 1) * tq)
            v_t = _dot_nt(eye, v_ref[0, rows, :]).astype(bf16)
            for h in range(n_heads):
                vt[jb, h * V_ROWS:h * V_ROWS + HEAD_DIM, :] = v_t[h * HEAD_DIM:(h + 1) * HEAD_DIM, :]
                vt[jb, h * V_ROWS + HEAD_DIM:(h + 1) * V_ROWS, :] = ones_rows
            pieces = jnp.concatenate(_split3(cc_ref[0, rows, :] * (-LOG2E)), axis=1)
            for p in range(n_heads // 2):
                a = _dot(pieces, psel_ref[p]).astype(bf16)
                kp = k_ref[0, rows, p * 128:(p + 1) * 128]
                kaug[2 * p, rows, :] = jnp.where(lo_half, kp, a)
                kaug[2 * p + 1, rows, :] = jnp.where(lo_half, a, kp)

    ones_hi = jnp.where((lane >= HEAD_DIM) & (lane < HEAD_DIM + 3), 1.0, 0.0).astype(bf16)
    ones_lo = jnp.where(lane < 3, 1.0, 0.0).astype(bf16)
    qa = []
    for p in range(n_heads // 2):
        qp = q_ref[0, :, p * 128:(p + 1) * 128]
        qa.append(jnp.where(lo_half, qp, ones_hi))
        qa.append(jnp.where(lo_half, ones_lo, qp))
    keep = lax.broadcasted_iota(i32, (tq, tq), 0) <= lax.broadcasted_iota(i32, (tq, tq), 1)
    acc[...] = jnp.zeros_like(acc)

    def score(jb, buf):
        k0 = pl.multiple_of(jb * tq, tq)
        for h in range(n_heads):
            buf[h] = _dot_nt(kaug[h, pl.ds(k0, tq), :], qa[h])

    def absorb(jb, ms, masked, buf):
        new_ms = []
        for h in range(n_heads):
            s = buf[h]
            if masked:
                s = jnp.where(keep, s, NEG)
            m_new = jnp.maximum(ms[h], jnp.max(s, axis=0, keepdims=True))
            alpha = jnp.exp2(ms[h] - m_new)
            p = jnp.exp2(s - m_new).astype(bf16)
            new_ms.append(m_new)
            acc[h] = acc[h] * alpha + _dot(vt[jb, h * V_ROWS:(h + 1) * V_ROWS, :], p)
        return tuple(new_ms)

    def pair(pp, ms):
        j0 = 2 * pp
        score(j0 + 1, sc1)
        ms = absorb(j0, ms, False, sc0)
        score(j0 + 2, sc0)
        return absorb(j0 + 1, ms, False, sc1)

    score(0, sc0)
    ms = lax.fori_loop(0, i // 2, pair, tuple(jnp.full((1, tq), NEG, f32) for _ in range(n_heads)))

    @pl.when(i % 2 == 0)
    def _even():
        absorb(i, ms, True, sc0)

    @pl.when(i % 2 == 1)
    def _odd():
        score(i, sc1)
        absorb(i, absorb(i - 1, ms, False, sc0), True, sc1)

    out_t = jnp.concatenate([acc[h, 0:HEAD_DIM, :] * (1.0 / acc[h, HEAD_DIM:HEAD_DIM + 1, :]) for h in range(n_heads)],
                            axis=0)
    ms_ = jnp.mean(out_t * out_t, axis=0, keepdims=True)
    out_t = out_t * lax.rsqrt(ms_ + EPS)
    o_ref[0] = (out_t.T * ng_ref[...]).astype(bf16)


def _piece_select():
    sel = np.zeros((4, 384, 128), np.float32)
    for pair in range(4):
        for j in range(3):
            sel[pair, j * 128 + 8 + 2 * pair, HEAD_DIM + j] = 1.0
            sel[pair, j * 128 + 8 + 2 * pair + 1, j] = 1.0
    return jnp.asarray(sel, bf16)


def _attn(qkv, cumc, norm_g):
    bsz, t, _ = qkv.shape
    tq = min(ATT_BLOCK, t)
    return pl.pallas_call(
        functools.partial(_attn_kernel, tq=tq, t=t),
        grid=(bsz, t // tq),
        in_specs=[pl.BlockSpec((1, tq, D_ATT), lambda b, i: (b, i, 0)),
                  pl.BlockSpec((1, t, D_ATT), lambda b, i: (b, 0, 1)),
                  pl.BlockSpec((1, t, D_ATT), lambda b, i: (b, 0, 2)),
                  pl.BlockSpec((1, t, 128), lambda b, i: (b, 0, 0)),
                  pl.BlockSpec((4, 384, 128), lambda b, i: (0, 0, 0)),
                  pl.BlockSpec((1, D_ATT), lambda b, i: (0, 0))],
        out_specs=pl.BlockSpec((1, tq, D_ATT), lambda b, i: (b, i, 0)),
        out_shape=jax.ShapeDtypeStruct((bsz, t, D_ATT), bf16),
        scratch_shapes=[pltpu.VMEM((D_ATT // HEAD_DIM, t, 128), bf16),
                        pltpu.VMEM((t // tq, (D_ATT // HEAD_DIM) * V_ROWS, tq), bf16),
                        pltpu.VMEM((D_ATT // HEAD_DIM, V_ROWS, tq), f32),
                        pltpu.VMEM((D_ATT // HEAD_DIM, tq, tq), f32),
                        pltpu.VMEM((D_ATT // HEAD_DIM, tq, tq), f32)],
        compiler_params=pltpu.CompilerParams(dimension_semantics=("parallel", "arbitrary"),
                                             vmem_limit_bytes=VMEM_LIMIT),
        name="attn",
    )(qkv, qkv, qkv, cumc, _piece_select(), norm_g)


def _layer_norm(y, g, b):
    mu = jnp.mean(y, axis=-1, keepdims=True)
    yc = y - mu
    var = jnp.mean(yc * yc, axis=-1, keepdims=True)
    return yc * lax.rsqrt(var + EPS) * g + b


def _outproj_kernel(ys_ref, ya_ref, x_ref, gt_ref, sc_ref, sh_ref, wo_ref, lng_ref, lnb_ref, wrh_ref, wrl_ref, rb_ref,
                    x1_ref, rt_ref, dest_ref, cnt_ref, tbl_ref, xrows_hbm,
                    carry, galloc, tbl, u2t, didx_v, didx_s, cnt_v, cnt_s, zeros, idx_sem, sc_sem, z_sem,
                    *, tm, blk, n_blk):
    i = pl.program_id(0)
    last = pl.num_programs(0) - 1
    cur = i % 2
    prev = 1 - cur

    def idx_ready(slot):
        return pltpu.make_async_copy(didx_v.at[slot], didx_s.at[slot], idx_sem.at[slot])

    def dispatched(slot):
        return pltpu.make_async_copy(u2t.at[slot], u2t.at[slot], sc_sem.at[slot])

    def dispatch_copy(slot, r, k):
        return pltpu.make_async_copy(u2t.at[slot, pl.ds(r * PACK_TILE, PACK_TILE)],
                                     xrows_hbm.at[pl.ds(didx_s[slot, k, r] * PACK_TILE, PACK_TILE)], sc_sem.at[slot])

    @pl.when(i == 0)
    def _init():
        carry[...] = jnp.zeros_like(carry)
        galloc[...] = jnp.zeros_like(galloc)
        tbl[...] = jnp.zeros_like(tbl)

    @pl.when(i >= 2)
    def _reuse():
        dispatched(cur).wait()
        dispatched(cur).wait()

    def step(dispatch_prev):
        n_parts = OUTPROJ_PARTS
        n_slices = 3 * n_parts + 2

        def dispatch_slice(c):
            if dispatch_prev:
                for r_ in range(c * tm // n_slices, (c + 1) * tm // n_slices):
                    dispatch_copy(prev, r_, 0).start(priority=0)
                    dispatch_copy(prev, r_, 1).start(priority=1)

        if dispatch_prev:
            idx_ready(prev).wait()
        part = tm // n_parts
        hs = []
        for a in range(n_parts):
            rs = slice(a * part, (a + 1) * part)
            dispatch_slice(a)
            hs.append(_dot(ys_ref[rs, :], wo_ref[0:D_SSM, :]) + _dot(ya_ref[rs, :], wo_ref[D_SSM:D, :]))
        logit_parts = []
        for a in range(n_parts):
            rs = slice(a * part, (a + 1) * part)
            dispatch_slice(n_parts + 2 * a)
            x1 = _layer_norm(ALPHA * x_ref[rs, :] + (1.0 + gt_ref[0, 0]) * hs[a], lng_ref[...], lnb_ref[...])
            x1_ref[rs, :] = x1
            u2 = x1 * (1.0 + sc_ref[0, 0]) + sh_ref[0, 0]
            uh = u2.astype(bf16)
            _store_packed_rows(u2t.at[cur, pl.ds(a * part * PACK_TILE, part * PACK_TILE)], _pack_rows(uh))
            dispatch_slice(n_parts + 2 * a + 1)
            ul = (u2 - uh.astype(f32)).astype(bf16)
            logit_parts.append((_dot(uh, wrh_ref[...]) + _dot(ul, wrh_ref[...])) + _dot(uh, wrl_ref[...]))
        logits = jnp.concatenate(logit_parts, axis=0) + rb_ref[...]
        lane = lax.broadcasted_iota(i32, (tm, 128), 1).astype(f32)
        big = jnp.float32(1e9)

        def first_max(vals):
            m = jnp.max(vals, axis=-1, keepdims=True)
            return m, jnp.min(jnp.where(vals == m, lane, big), axis=-1, keepdims=True)

        gl = jnp.where(lane < N_GROUPS_R, logits, NEG)
        gmax, gidx = first_max(gl)
        g_p = 1.0 / jnp.sum(jnp.exp(gl - gmax), axis=-1, keepdims=True)
        lo = N_GROUPS_R + EXPERTS_PER_GROUP * gidx
        el = jnp.where((lane >= lo) & (lane < lo + EXPERTS_PER_GROUP), logits, NEG)
        m1, i1 = first_max(el)
        el2 = jnp.where(lane == i1, NEG, el)
        m2, i2 = first_max(el2)
        r = jnp.exp(m2 - m1)
        w1 = g_p / (1.0 + r)
        w2 = g_p * r / (1.0 + r)
        dispatch_slice(3 * n_parts)

        oh1 = lane == i1
        oh2 = lane == i2
        oh = jnp.where(oh1 | oh2, 1.0, 0.0)
        r_i = lax.broadcasted_iota(i32, (tm, tm), 0)
        c_i = lax.broadcasted_iota(i32, (tm, tm), 1)
        lower = jnp.where(r_i > c_i, 1.0, 0.0).astype(bf16)
        c_old = carry[...]
        prefix = _dot(lower, oh.astype(bf16)) + c_old
        rank1 = jnp.sum(jnp.where(oh1, prefix, 0.0), axis=-1, keepdims=True)
        rank2 = jnp.sum(jnp.where(oh2, prefix, 0.0), axis=-1, keepdims=True)
        c_new = c_old + jnp.sum(oh, axis=0, keepdims=True)
        carry[...] = c_new
        dispatch_slice(3 * n_parts + 1)

        nb_old = jnp.floor((c_old + (blk - 1)) * (1.0 / blk))
        nb_new = jnp.floor((c_new + (blk - 1)) * (1.0 / blk))
        fresh = nb_new - nb_old
        sq_r = lax.broadcasted_iota(i32, (128, 128), 0)
        sq_c = lax.broadcasted_iota(i32, (128, 128), 1)
        before = jnp.where(sq_r < sq_c, 1.0, 0.0).astype(bf16)
        base = galloc[...] + _dot(jnp.broadcast_to(fresh, (8, 128)).astype(bf16), before)[0:1, :]
        galloc[...] = galloc[...] + jnp.sum(fresh, axis=-1, keepdims=True)
        ordinal = sq_r.astype(f32)
        tbl[...] = jnp.where((ordinal >= nb_old) & (ordinal < nb_new), base + (ordinal - nb_old), tbl[...])
        tbl_ref[...] = tbl[...]
        cnt_ref[...] = jnp.concatenate([c_new, galloc[...], jnp.zeros((6, 128), f32)], axis=0)

        rt = jnp.where(lane == 4, w1, jnp.where(lane == 5, w2, 0.0))
        rt_ref[...] = rt

        ids_bf = tbl[...].astype(bf16)

        def row_of(onehot, rank):
            nth = jnp.floor(rank * (1.0 / blk))
            ids = _dot(jnp.where(lane == nth, 1.0, 0.0).astype(bf16), ids_bf)
            return jnp.sum(jnp.where(onehot, ids, 0.0), axis=-1, keepdims=True) * blk + (rank - nth * blk)

        d1 = row_of(oh1, rank1)
        d2 = row_of(oh2, rank2)
        dd = jnp.where(lane == 0, d1, jnp.where(lane == 1, d2, 0.0)).T[0:8, :].astype(i32)
        dest_ref[...] = dd
        didx_v[cur] = dd
        idx_ready(cur).start()

    pl.when(i == 0)(lambda: step(False))
    pl.when(i > 0)(lambda: step(True))

    @pl.when(i == last)
    def _epilogue():
        idx_ready(cur).wait()

        def issue(r_, c):
            dispatch_copy(cur, r_, 0).start()
            dispatch_copy(cur, r_, 1).start()
            return c
        lax.fori_loop(0, tm, issue, 0)

        cnt = carry[...]
        n_blocks = jnp.floor((cnt + (blk - 1)) * (1.0 / blk))
        ordinal = lax.broadcasted_iota(i32, (128, 128), 0).astype(f32)
        last_id = jnp.sum(jnp.where(ordinal == n_blocks - 1.0, tbl[...], 0.0), axis=0, keepdims=True)
        used = cnt - (n_blocks - 1.0) * blk
        cnt_v[...] = jnp.concatenate([last_id * blk + used, blk - used, galloc[...], jnp.zeros((5, 128), f32)],
                                     axis=0).astype(i32)
        to_smem = pltpu.make_async_copy(cnt_v, cnt_s, z_sem)
        to_smem.start()
        to_smem.wait()
        zeros[...] = jnp.zeros_like(zeros)
        sizes = [1 << b for b in reversed(range(blk.bit_length() - 1))]

        def for_each_fill(fn):
            def tail(e, c):
                pad = cnt_s[1, N_GROUPS_R + e]
                off = cnt_s[0, N_GROUPS_R + e]
                for sz in sizes:
                    @pl.when((pad & sz) != 0)
                    def _(off=off, sz=sz):
                        fn(pltpu.make_async_copy(zeros.at[pl.ds(0, sz * PACK_TILE)],
                                                 xrows_hbm.at[pl.ds(off * PACK_TILE, sz * PACK_TILE)], z_sem))
                    off = off + (pad & sz)
                return c
            lax.fori_loop(0, N_EXPERTS, tail, 0)

            def whole(b, c):
                fn(pltpu.make_async_copy(zeros, xrows_hbm.at[pl.ds(b * blk * PACK_TILE, blk * PACK_TILE)], z_sem))
                return c
            lax.fori_loop(cnt_s[2, 0], n_blk, whole, 0)

        for_each_fill(lambda copy: copy.start())
        for_each_fill(lambda copy: copy.wait())

        @pl.when(i > 0)
        def _prev_done():
            dispatched(prev).wait()
            dispatched(prev).wait()
        dispatched(cur).wait()
        dispatched(cur).wait()


def _outproj(y_ssm, y_att, x, mod4, w_out, ln_g, ln_b, wr_hi, wr_lo, rb, t, blk, n_blk):
    n = x.shape[0]
    tm = min(OUTPROJ_ROWS, t)
    nt = t // tm
    assert n // blk + 1 <= 128 and n_blk <= 256, "block-id table: 128 blocks per expert, ids exact in bf16"
    vec = lambda k: pl.BlockSpec((1, 1, 1, D), lambda i, k=k: (k, i // nt, 0, 0))
    full = lambda shape: pl.BlockSpec(shape, lambda i: (0,) * len(shape))
    rows = lambda w: pl.BlockSpec((tm, w), lambda i: (i, 0))
    return pl.pallas_call(
        functools.partial(_outproj_kernel, tm=tm, blk=blk, n_blk=n_blk),
        grid=(n // tm,),
        in_specs=[rows(D_SSM), rows(D_ATT), rows(D), vec(2), vec(4), vec(3),
                  full((D, D)), full((1, D)), full((1, D)), full((D, 128)), full((D, 128)), full((1, 128))],
        out_specs=[rows(D), rows(128), pl.BlockSpec((8, tm), lambda i: (0, i)), full((8, 128)), full((128, 128)),
                   pl.BlockSpec(memory_space=pl.ANY)],
        out_shape=[jax.ShapeDtypeStruct((n, D), f32), jax.ShapeDtypeStruct((n, 128), f32),
                   jax.ShapeDtypeStruct((8, n), i32), jax.ShapeDtypeStruct((8, 128), f32),
                   jax.ShapeDtypeStruct((128, 128), f32),
                   jax.ShapeDtypeStruct((n_blk * blk * PACK_TILE, 128), u32)],
        scratch_shapes=[pltpu.VMEM((1, 128), f32), pltpu.VMEM((1, 128), f32), pltpu.VMEM((128, 128), f32),
                        pltpu.VMEM((2, tm * PACK_TILE, 128), u32),
                        pltpu.VMEM((2, 8, tm), i32), pltpu.SMEM((2, 8, tm), i32),
                        pltpu.VMEM((8, 128), i32), pltpu.SMEM((8, 128), i32),
                        pltpu.VMEM((blk * PACK_TILE, 128), u32),
                        pltpu.SemaphoreType.DMA((2,)), pltpu.SemaphoreType.DMA((2,)), pltpu.SemaphoreType.DMA(())],
        compiler_params=pltpu.CompilerParams(dimension_semantics=("arbitrary",), vmem_limit_bytes=VMEM_LIMIT),
        name="outproj",
    )(y_ssm, y_att, x, mod4, mod4, mod4, w_out, ln_g, ln_b, wr_hi, wr_lo, rb)


def _tile_copy(src_hbm, row, buf, slot, sem):
    return pltpu.make_async_copy(src_hbm.at[row], buf.at[pl.ds(slot * ROW_TILE, ROW_TILE)], sem)


def _moe_kernel(be_ref, br_ref, nu_ref, nxt_ref, par_ref, nv_ref, x_ref, wg_hbm, wu_hbm, wd_hbm, y_ref,
                wfg, wfu, wfd, wgb, wub, wdb, wsem, *, blk):
    i = pl.program_id(0)
    used = i < nu_ref[0]

    def fetch(e, slot):
        return [pltpu.make_async_copy(src.at[e], dst.at[slot], wsem.at[slot])
                for src, dst in ((wg_hbm, wfg), (wu_hbm, wfu), (wd_hbm, wfd))]

    @pl.when(i == 0)
    def _first():
        for c in fetch(be_ref[0], par_ref[0]):
            c.start()

    @pl.when(jnp.logical_and(used, jnp.logical_or(i == 0, be_ref[i] != be_ref[jnp.maximum(i - 1, 0)])))
    def _switch():
        slot = par_ref[i]
        for c in fetch(be_ref[i], slot):
            c.wait()

        @pl.when(nxt_ref[i] >= 0)
        def _next():
            for c in fetch(nxt_ref[i], 1 - slot):
                c.start(priority=1)
        wgb[...] = wfg[slot].astype(bf16)
        wub[...] = wfu[slot].astype(bf16)
        wdb[...] = wfd[slot].astype(bf16)

    @pl.when(jnp.logical_not(used))
    def _spare():
        y_ref[...] = jnp.zeros_like(y_ref)

    def mlp(rows):
        xa, xb = _load_packed_rows(x_ref.at[pl.ds(0, rows * PACK_TILE)], rows)
        half = D // 2
        gate = _dot(xa, wgb[0:half, :]) + _dot(xb, wgb[half:D, :])
        up = _dot(xa, wub[0:half, :]) + _dot(xb, wub[half:D, :])
        hid = (_silu(gate) * up).astype(bf16)
        _store_row_tiles(y_ref.at[pl.ds(0, rows * ROW_TILE)], _dot(hid, wdb[...]))

    sparse = nv_ref[i] <= blk // 2

    @pl.when(jnp.logical_and(used, jnp.logical_not(sparse)))
    def _full():
        mlp(blk)

    @pl.when(jnp.logical_and(used, sparse))
    def _half():
        mlp(blk // 2)
        y_ref[pl.ds(blk // 2 * ROW_TILE, blk // 2 * ROW_TILE), :] = jnp.zeros((blk // 2 * ROW_TILE, 128), f32)


def _moe(block_e, block_row, n_used, next_e, parity, n_valid, x_rows, w_gate, w_up, w_down, blk):
    nblk = block_e.shape[0]
    rspec = pl.BlockSpec((blk * ROW_TILE, 128), lambda i, be, br, *_: (br[i], 0))
    xspec = pl.BlockSpec((blk * PACK_TILE, 128), lambda i, be, br, nu, *_: (br[jnp.minimum(i, nu[0] - 1)], 0))
    anyspec = pl.BlockSpec(memory_space=pl.ANY)
    grid_spec = pltpu.PrefetchScalarGridSpec(
        num_scalar_prefetch=6,
        grid=(nblk,),
        in_specs=[xspec, anyspec, anyspec, anyspec],
        out_specs=rspec,
        scratch_shapes=[pltpu.VMEM((2, D, D_EXPERT), f32), pltpu.VMEM((2, D, D_EXPERT), f32),
                        pltpu.VMEM((2, D_EXPERT, D), f32),
                        pltpu.VMEM((D, D_EXPERT), bf16), pltpu.VMEM((D, D_EXPERT), bf16),
                        pltpu.VMEM((D_EXPERT, D), bf16),
                        pltpu.SemaphoreType.DMA((2,))],
    )
    return pl.pallas_call(
        functools.partial(_moe_kernel, blk=blk),
        grid_spec=grid_spec,
        out_shape=jax.ShapeDtypeStruct((nblk * blk * ROW_TILE, 128), f32),
        compiler_params=pltpu.CompilerParams(dimension_semantics=("arbitrary",), vmem_limit_bytes=VMEM_LIMIT),
        name="moe",
    )(block_e, block_row, n_used, next_e, parity, n_valid, x_rows, w_gate, w_up, w_down)


def _combine_kernel(d1_ref, d2_ref, y_hbm, x1_ref, rt_ref, gt_ref, lng_ref, lnb_ref, o_ref, buf, sem, *, tm):
    i = pl.program_id(0)
    last = pl.num_programs(0) - 1

    @pl.when(i == 0)
    def _prologue():
        for s in range(2):
            def issue(r, c, s=s):
                base = jnp.minimum(s, last) * tm
                _tile_copy(y_hbm, d1_ref[base + r], buf.at[s, 0], r, sem.at[s]).start()
                _tile_copy(y_hbm, d2_ref[base + r], buf.at[s, 1], r, sem.at[s]).start()
                return c
            lax.fori_loop(0, tm, issue, 0)

    def rows_ready(slot):
        return pltpu.make_async_copy(buf.at[slot], buf.at[slot], sem.at[slot])

    slot = i % 3
    nxt = (i + 2) % 3
    base = jnp.minimum(i + 2, last) * tm
    rows_ready(slot).wait()
    for r in range(tm):
        _tile_copy(y_hbm, d1_ref[base + r], buf.at[nxt, 0], r, sem.at[nxt]).start(priority=0)
        _tile_copy(y_hbm, d2_ref[base + r], buf.at[nxt, 1], r, sem.at[nxt]).start(priority=1)
    rt = rt_ref[...]
    moe = rt[:, 4:5] * _load_row_tiles(buf.at[slot, 0], tm) + rt[:, 5:6] * _load_row_tiles(buf.at[slot, 1], tm)
    y = ALPHA * x1_ref[...] + (1.0 + gt_ref[0, 0]) * moe
    o_ref[...] = _layer_norm(y, lng_ref[...], lnb_ref[...])

    @pl.when(i == last)
    def _drain():
        rows_ready((i + 1) % 3).wait()
        rows_ready((i + 2) % 3).wait()


def _combine(dest1, dest2, y_rows, x1, rt, mod4, ln_g, ln_b, t):
    n = x1.shape[0]
    tm = min(COMBINE_ROWS, t)
    nt = t // tm
    full = lambda shape: pl.BlockSpec(shape, lambda i, d1, d2: (0,) * len(shape))
    rows = lambda w: pl.BlockSpec((tm, w), lambda i, d1, d2: (i, 0))
    grid_spec = pltpu.PrefetchScalarGridSpec(
        num_scalar_prefetch=2,
        grid=(n // tm,),
        in_specs=[pl.BlockSpec(memory_space=pl.ANY), rows(D), rows(128),
                  pl.BlockSpec((1, 1, 1, D), lambda i, d1, d2: (5, i // nt, 0, 0)),
                  full((1, D)), full((1, D))],
        out_specs=rows(D),
        scratch_shapes=[pltpu.VMEM((3, 2, tm * ROW_TILE, 128), f32), pltpu.SemaphoreType.DMA((3,))],
    )
    return pl.pallas_call(
        functools.partial(_combine_kernel, tm=tm),
        grid_spec=grid_spec,
        out_shape=jax.ShapeDtypeStruct((n, D), f32),
        compiler_params=pltpu.CompilerParams(dimension_semantics=("arbitrary",), vmem_limit_bytes=VMEM_LIMIT),
        name="combine",
    )(dest1, dest2, y_rows, x1, rt, mod4, ln_g, ln_b)


def _hi_lo(w):
    hi = w.astype(bf16)
    return hi, (w - hi.astype(f32)).astype(bf16)


def _pad_lanes(v, offset, width):
    return jnp.zeros((width,), f32).at[offset:offset + v.shape[0]].set(v)


def _layer(x, c, ada_w, ada_b, w_in, conv_w, conv_b, dt_bias, a_log, d_skip, ssm_norm_g, fg_bias, att_norm_g,
           w_out, ln1_g, ln1_b, router_g_w, router_g_b, router_e_w, router_e_b, w_gate, w_up, w_down, ln2_g, ln2_b):
    bsz, t, _ = x.shape
    n = bsz * t

    mod4 = _ada(c, ada_w, ada_b).reshape(6, bsz, 1, D)

    w_main, w_small = _pack_w_in(w_in)
    zx, bc, qkv, dtf = _inproj(x, mod4, w_main, w_small, conv_w, conv_b.reshape(1, -1))

    pc = jnp.stack([_pad_lanes(dt_bias, 0, 128), _pad_lanes(a_log, 0, 128), _pad_lanes(fg_bias, 8, 128)]
                   + [jnp.zeros((128,), f32)] * 5)
    rep = lambda v: jnp.repeat(v, HEAD_DIM)
    pe = jnp.stack([rep(dt_bias), rep(a_log), rep(d_skip), ssm_norm_g] + [jnp.zeros((D_SSM,), f32)] * 4)
    y_ssm, cumc = _ssd(zx, bc, dtf, pc, pe)

    y_att = _attn(qkv, cumc, att_norm_g.reshape(1, -1))

    wr = jnp.concatenate([router_g_w, router_e_w, jnp.zeros((D, 128 - N_GROUPS_R - N_EXPERTS), f32)], axis=1)
    wr_hi, wr_lo = _hi_lo(wr)
    rb = jnp.concatenate([router_g_b, router_e_b, jnp.zeros((128 - N_GROUPS_R - N_EXPERTS,), f32)]).reshape(1, 128)
    blk = MOE_ROWS
    nblk = (2 * n) // blk + N_EXPERTS
    x1, rt, dest, cnt, tbl, x_rows = _outproj(y_ssm.reshape(n, D_SSM), y_att.reshape(n, D_ATT), x.reshape(n, D), mod4,
                                              w_out.astype(bf16), ln1_g.reshape(1, D), ln1_b.reshape(1, D),
                                              wr_hi, wr_lo, rb, t, blk, nblk)

    counts = cnt[0, N_GROUPS_R:N_GROUPS_R + N_EXPERTS].astype(i32)
    nb = (counts + blk - 1) // blk
    nb_end = jnp.cumsum(nb)
    n_used = nb_end[N_EXPERTS - 1:]
    every = jnp.arange(nblk, dtype=i32)
    step = jnp.minimum(every, n_used[0] - 1)
    owner = (nb_end[None, :] <= step[:, None]).astype(i32)
    block_e = jnp.sum(owner, axis=1)
    first = jnp.sum(owner * nb[None, :], axis=1)
    lanes = jnp.arange(128, dtype=i32)[None, :]
    nth_row = jnp.dot((lanes == (step - first)[:, None]).astype(f32), tbl, precision=lax.Precision.HIGHEST)
    block_id = jnp.sum(jnp.where(lanes == (block_e + N_GROUPS_R)[:, None], nth_row, 0.0), axis=1).astype(i32)
    block_row = jnp.where(every < n_used[0], block_id, every)
    run_end = first + jnp.sum((jnp.arange(N_EXPERTS)[None, :] == block_e[:, None]) * nb[None, :], axis=1)
    next_e = jnp.where(run_end < n_used[0], jnp.sum((nb_end[None, :] <= run_end[:, None]).astype(i32), axis=1), -1)
    parity = jnp.sum(((jnp.arange(N_EXPERTS)[None, :] < block_e[:, None]) & (nb[None, :] > 0)).astype(i32), axis=1) % 2

    own = (jnp.arange(N_EXPERTS)[None, :] == block_e[:, None]).astype(i32)
    n_valid = jnp.clip(jnp.sum(own * counts[None, :], axis=1) - (step - first) * blk, 0, blk)
    y_rows = _moe(block_e, block_row, n_used, next_e, parity, n_valid, x_rows,
                  w_gate, w_up, w_down, blk)
    out = _combine(dest[0], dest[1], y_rows.reshape(nblk * blk, ROW_TILE, 128), x1, rt, mod4,
                   ln2_g.reshape(1, D), ln2_b.reshape(1, D), t)
    return out.reshape(bsz, t, D)


def kernel(x, c, ada_w, ada_b, w_in, conv_w, conv_b, dt_bias, a_log, d_skip, ssm_norm_g, fg_bias, att_norm_g, w_out,
           ln1_g, ln1_b, router_g_w, router_g_b, router_e_w, router_e_b, w_gate, w_up, w_down, ln2_g, ln2_b):
    depth = ada_w.shape[0]
    for l in range(depth):
        x = _layer(x, c, ada_w[l], ada_b[l], w_in[l], conv_w[l], conv_b[l], dt_bias[l], a_log[l], d_skip[l],
                   ssm_norm_g[l], fg_bias[l], att_norm_g[l], w_out[l], ln1_g[l], ln1_b[l], router_g_w[l],
                   router_g_b[l], router_e_w[l], router_e_b[l], w_gate[l], w_up[l], w_down[l], ln2_g[l], ln2_b[l])
    return x
```

```python
import functools

import jax
import jax.numpy as jnp
import numpy as np
from jax import lax
from jax.experimental import pallas as pl
from jax.experimental.pallas import tpu as pltpu

f32 = jnp.float32
bf16 = jnp.bfloat16
i32 = jnp.int32

D = 1024
D_SSM = 512
D_ATT = 512
HEAD_DIM = 64
GROUP_W = 256
N_STATE = 128
CONV_K = 4
N_GROUPS_R = 4
EXPERTS_PER_GROUP = 8
N_EXPERTS = 32
D_EXPERT = 512
ALPHA = 2.0 ** 0.25
EPS = 1e-5
NEG = -1e30
LOG2E = 1.4426950408889634
QK_SCALE = HEAD_DIM ** -0.5 * LOG2E
V_ROWS = 80

SSD_CHUNK = 256
ATT_BLOCK = 256
INPROJ_ROWS = 512
OUTPROJ_ROWS = 512
OUTPROJ_PARTS = 2
MOE_ROWS = 512
COMBINE_ROWS = 256
VMEM_LIMIT = 48 * 1024 * 1024


def _dot(a, b):
    return jnp.dot(a, b, preferred_element_type=f32)


def _dot_nt(a, b):
    return lax.dot_general(a, b, (((1,), (1,)), ((), ())), preferred_element_type=f32)


def _dot_tn(a, b):
    return lax.dot_general(a, b, (((0,), (0,)), ((), ())), preferred_element_type=f32)


def _split3(v):
    hi = v.astype(bf16)
    r1 = v - hi.astype(f32)
    mid = r1.astype(bf16)
    lo = (r1 - mid.astype(f32)).astype(bf16)
    return hi, mid, lo


def _dot_exact_lhs(m, v):
    hi, mid, lo = _split3(v)
    return (_dot(m, hi) + _dot(m, mid)) + _dot(m, lo)


def _dot_exact_rhs(v, m):
    hi, mid, lo = _split3(v)
    return (_dot(hi, m) + _dot(mid, m)) + _dot(lo, m)


ROW_TILE = 8


def _store_row_tiles(ref, val):
    rows = val.shape[0]
    for c in range(ROW_TILE):
        ref[pl.ds(c, rows, stride=ROW_TILE), :] = val[:, c * 128:(c + 1) * 128]


def _load_row_tiles(ref, rows):
    return jnp.concatenate([ref[pl.ds(c, rows, stride=ROW_TILE), :] for c in range(ROW_TILE)], axis=1)


PACK_TILE = 4
u32 = jnp.uint32


def _pack_rows(vb):
    lo = pltpu.bitcast(vb[:, 0:D // 2].astype(f32), u32) >> 16
    hi = pltpu.bitcast(vb[:, D // 2:D].astype(f32), u32) & jnp.uint32(0xFFFF0000)
    return lo | hi


def _store_packed_rows(ref, packed):
    rows = packed.shape[0]
    for c in range(PACK_TILE):
        ref[pl.ds(c, rows, stride=PACK_TILE), :] = packed[:, c * 128:(c + 1) * 128]


def _load_packed_rows(ref, rows):
    words = [ref[pl.ds(c, rows, stride=PACK_TILE), :] for c in range(PACK_TILE)]
    lo = jnp.concatenate([pltpu.bitcast(w << 16, f32) for w in words], axis=1)
    hi = jnp.concatenate([pltpu.bitcast(w & jnp.uint32(0xFFFF0000), f32) for w in words], axis=1)
    return lo.astype(bf16), hi.astype(bf16)


def _softplus(x):
    return jnp.maximum(x, 0.0) + jnp.log1p(jnp.exp(-jnp.abs(x)))


def _silu(x):
    return x * jax.nn.sigmoid(x)


def _ada_kernel(c_ref, w_ref, b_ref, o_ref):
    s = _silu(c_ref[...]).astype(bf16)
    o_ref[0] = _dot(s, w_ref[...].astype(bf16)) + b_ref[0]


def _ada(c, w, b):
    bsz = c.shape[0]
    return pl.pallas_call(
        _ada_kernel,
        grid=(6,),
        in_specs=[pl.BlockSpec((bsz, D), lambda j: (0, 0)),
                  pl.BlockSpec((D, D), lambda j: (0, j)),
                  pl.BlockSpec((1, 1, D), lambda j: (j, 0, 0))],
        out_specs=pl.BlockSpec((1, bsz, D), lambda j: (j, 0, 0)),
        out_shape=jax.ShapeDtypeStruct((6, bsz, D), f32),
        compiler_params=pltpu.CompilerParams(dimension_semantics=("arbitrary",), vmem_limit_bytes=VMEM_LIMIT),
        name="ada",
    )(c, w, b.reshape(6, 1, D))


W_IN_COLS = 3088
PACK_ROWS = 256


def _pack_w_in_kernel(wt_ref, wm_ref, ws_ref):
    for j in range(3072 // PACK_ROWS):
        src = j * PACK_ROWS if j * PACK_ROWS < 1536 else j * PACK_ROWS + 8
        wm_ref[:, j * PACK_ROWS:(j + 1) * PACK_ROWS] = wt_ref[src:src + PACK_ROWS, :].T.astype(bf16)
    small = jnp.concatenate([wt_ref[1536:1544, :], wt_ref[W_IN_COLS - 8:W_IN_COLS, :], jnp.zeros((112, D), f32)],
                            axis=0).T
    hi = small.astype(bf16)
    ws_ref[:, 0:128] = hi
    ws_ref[:, 128:256] = (small - hi.astype(f32)).astype(bf16)


def _pack_w_in(w_in):
    full = lambda shape: pl.BlockSpec(shape, lambda i: (0,) * len(shape))
    return pl.pallas_call(
        _pack_w_in_kernel,
        grid=(1,),
        in_specs=[full((W_IN_COLS, D))],
        out_specs=[full((D, 3072)), full((D, 256))],
        out_shape=[jax.ShapeDtypeStruct((D, 3072), bf16), jax.ShapeDtypeStruct((D, 256), bf16)],
        compiler_params=pltpu.CompilerParams(dimension_semantics=("arbitrary",), vmem_limit_bytes=VMEM_LIMIT),
        name="pack_w_in",
    )(w_in.T)


def _inproj_kernel(x_ref, sc_ref, sh_ref, wm_ref, ws_ref, cw_ref, cb_ref, zx_ref, bc_ref, qkv_ref, dtf_ref, xcat, *, tm):
    i = pl.program_id(1)

    @pl.when(i == 0)
    def _init():
        xcat[0:8, :] = jnp.zeros((8, 2 * D_SSM), f32)

    u = x_ref[0] * (1.0 + sc_ref[0, 0]) + sh_ref[0, 0]
    ub = u.astype(bf16)
    xcat[8:8 + tm, 0:512] = _dot(ub, wm_ref[:, 512:1024])
    xcat[8:8 + tm, 512:1024] = _dot(ub, wm_ref[:, 1024:1536])
    zx_ref[0, :, 0:512] = _dot(ub, wm_ref[:, 0:512])
    acc = cw_ref[0:1, :] * xcat[5:5 + tm, :] + cb_ref[...]
    for k in range(1, CONV_K):
        acc = acc + cw_ref[k:k + 1, :] * xcat[5 + k:5 + k + tm, :]
    xcat[0:8, :] = xcat[tm:tm + 8, :]
    xbc = _silu(acc)
    zx_ref[0, :, 512:1024] = xbc[:, 0:512]
    bc_ref[0] = xbc[:, 512:1024].astype(bf16)
    for j, scale in enumerate((QK_SCALE, 1.0, 1.0)):
        qkv_ref[0, :, j * 512:(j + 1) * 512] = (
            _dot(ub, wm_ref[:, 1536 + j * 512:1536 + (j + 1) * 512]) * scale).astype(bf16)
    ul = (u - ub.astype(f32)).astype(bf16)
    d_hl = _dot(ub, ws_ref[...])
    dtf_ref[0] = (d_hl[:, 0:128] + _dot(ul, ws_ref[:, 0:128])) + d_hl[:, 128:256]


def _inproj(x, mod4, w_main, ws, conv_w, conv_b):
    bsz, t, _ = x.shape
    tm = min(INPROJ_ROWS, t)
    vec = lambda k: pl.BlockSpec((1, 1, 1, D), lambda b, i, k=k: (k, b, 0, 0))
    full = lambda shape: pl.BlockSpec(shape, lambda b, i: (0,) * len(shape))
    rows = lambda w: pl.BlockSpec((1, tm, w), lambda b, i: (b, i, 0))
    return pl.pallas_call(
        functools.partial(_inproj_kernel, tm=tm),
        grid=(bsz, t // tm),
        in_specs=[rows(D), vec(1), vec(0), full((D, 3072)), full((D, 256)),
                  full((CONV_K, 2 * D_SSM)), full((1, 2 * D_SSM))],
        out_specs=[rows(1024), rows(512), rows(1536), rows(128)],
        out_shape=[jax.ShapeDtypeStruct((bsz, t, 1024), f32),
                   jax.ShapeDtypeStruct((bsz, t, 512), bf16),
                   jax.ShapeDtypeStruct((bsz, t, 1536), bf16),
                   jax.ShapeDtypeStruct((bsz, t, 128), f32)],
        scratch_shapes=[pltpu.VMEM((tm + 8, 2 * D_SSM), f32)],
        compiler_params=pltpu.CompilerParams(dimension_semantics=("parallel", "arbitrary"),
                                             vmem_limit_bytes=VMEM_LIMIT),
        name="inproj",
    )(x, mod4, mod4, w_main, ws, conv_w, conv_b)


def _ssd_kernel(z_ref, xs_ref, bc_ref, dtf_ref, pc_ref, pe_ref, y_ref, cumc_ref, state, carry, *, lc):
    j = pl.program_id(1)

    @pl.when(j == 0)
    def _init():
        state[...] = jnp.zeros_like(state)
        carry[...] = jnp.zeros_like(carry)

    xs = xs_ref[0]
    bm = bc_ref[0, :, 0:256]
    cm = bc_ref[0, :, 256:512]

    dtf = dtf_ref[0]
    lane = lax.broadcasted_iota(i32, (lc, 128), 1)
    dt_c = _softplus(dtf + pc_ref[0:1, :])
    a_c = dt_c * (-jnp.exp(pc_ref[1:2, :]))
    logf = -_softplus(-(dtf + pc_ref[2:3, :]))
    v = jnp.where(lane < 8, a_c, logf)
    r_i = lax.broadcasted_iota(i32, (lc, lc), 0)
    c_i = lax.broadcasted_iota(i32, (lc, lc), 1)
    tri = r_i >= c_i
    tri_b = jnp.where(tri, 1.0, 0.0).astype(bf16)
    cum = _dot_exact_lhs(tri_b, v) + carry[...]
    carry[...] = jnp.where(lane[0:1, :] >= 8, cum[lc - 1:lc, :], 0.0)
    cumc_ref[0] = cum
    cs_t = cum.T[0:8, :]

    e_r = lax.broadcasted_iota(i32, (128, D_SSM), 0)
    e_c = lax.broadcasted_iota(i32, (128, D_SSM), 1)
    expand = jnp.where(jnp.right_shift(e_c, 6) == e_r, 1.0, 0.0).astype(bf16)
    dt_e = _dot_exact_rhs(dt_c, expand)
    cs_e = _dot_exact_rhs(cum, expand)

    xdt = xs * dt_e
    ecs = jnp.exp(cs_e)
    cs_last = cs_e[lc - 1:lc, :]
    dec_st = jnp.exp(cs_last - cs_e)
    lane_g = lax.broadcasted_iota(i32, (1, GROUP_W), 1)
    ys = []
    for g in range(2):
        gs = slice(g * GROUP_W, (g + 1) * GROUP_W)
        bg = bm[:, g * N_STATE:(g + 1) * N_STATE]
        cg = cm[:, g * N_STATE:(g + 1) * N_STATE]
        cb = _dot_nt(cg, bg)
        xdt_g = xdt[:, gs]
        xdt_gb = xdt_g.astype(bf16)
        ms, xb = [], []
        for hh in range(4):
            h = g * 4 + hh
            lm = jnp.exp(jnp.where(tri, cum[:, h:h + 1] - cs_t[h:h + 1, :], -jnp.inf))
            ms.append((cb * lm).astype(bf16))
            xb.append(jnp.where(jnp.right_shift(lane_g, 6) == hh, xdt_gb, jnp.zeros_like(xdt_gb)))
        y_diag = _dot(jnp.concatenate(ms, axis=1), jnp.concatenate(xb, axis=0))
        st = state[g]
        y_off = _dot(cg, st.astype(bf16)) * ecs[:, gs]
        upd = _dot_tn(bg, (xdt_g * dec_st[:, gs]).astype(bf16))
        state[g] = st * jnp.exp(cs_last[:, gs]) + upd
        ys.append(y_diag + y_off + xs[:, gs] * pe_ref[2:3, gs])

    outs = []
    for g in range(2):
        gs = slice(g * GROUP_W, (g + 1) * GROUP_W)
        yg = ys[g] * _silu(z_ref[0, :, gs])
        ms_ = jnp.mean(yg * yg, axis=-1, keepdims=True)
        outs.append(yg * lax.rsqrt(ms_ + EPS))
    y_ref[0] = (jnp.concatenate(outs, axis=1) * pe_ref[3:4, :]).astype(bf16)


def _ssd(zx, bc, dtf, pc, pe):
    bsz, t, _ = zx.shape
    lc = min(SSD_CHUNK, t)
    col = lambda k: pl.BlockSpec((1, lc, 512), lambda b, j, k=k: (b, j, k))
    full = lambda shape: pl.BlockSpec(shape, lambda b, j: (0,) * len(shape))
    return pl.pallas_call(
        functools.partial(_ssd_kernel, lc=lc),
        grid=(bsz, t // lc),
        in_specs=[col(0), col(1), col(0),
                  pl.BlockSpec((1, lc, 128), lambda b, j: (b, j, 0)),
                  full((8, 128)), full((8, D_SSM))],
        out_specs=[pl.BlockSpec((1, lc, D_SSM), lambda b, j: (b, j, 0)),
                   pl.BlockSpec((1, lc, 128), lambda b, j: (b, j, 0))],
        out_shape=[jax.ShapeDtypeStruct((bsz, t, D_SSM), bf16),
                   jax.ShapeDtypeStruct((bsz, t, 128), f32)],
        scratch_shapes=[pltpu.VMEM((2, N_STATE, GROUP_W), f32),
                        pltpu.VMEM((1, 128), f32)],
        compiler_params=pltpu.CompilerParams(dimension_semantics=("parallel", "arbitrary"),
                                             vmem_limit_bytes=VMEM_LIMIT),
        name="ssd",
    )(zx, zx, bc, dtf, pc, pe)


def _attn_kernel(q_ref, k_ref, v_ref, cc_ref, psel_ref, ng_ref, o_ref, kaug, vt, acc, sc0, sc1, *, tq, t):
    i = pl.program_id(1)
    nkb = t // tq
    n_heads = D_ATT // HEAD_DIM
    lane = lax.broadcasted_iota(i32, (1, 128), 1)
    lo_half = lane < HEAD_DIM

    @pl.when(i == 0)
    def _build():
        ones_rows = jnp.where(lax.broadcasted_iota(i32, (V_ROWS - HEAD_DIM, tq), 0) == 0, 1.0, 0.0).astype(bf16)
        for jb in range(nkb):
            rows = slice(jb * tq, (jb + 1) * tq)
            v_t = v_ref[0, rows, :].astype(f32).T.astype(bf16)
            for h in range(n_heads):
                vt[jb, h * V_ROWS:h * V_ROWS + HEAD_DIM, :] = v_t[h * HEAD_DIM:(h + 1) * HEAD_DIM, :]
                vt[jb, h * V_ROWS + HEAD_DIM:(h + 1) * V_ROWS, :] = ones_rows
            pieces = jnp.concatenate(_split3(cc_ref[0, rows, :] * (-LOG2E)), axis=1)
            spare = _dot(pieces, psel_ref[...]).astype(bf16)
            for p in range(n_heads // 2):
                a = spare[:, p * 128:(p + 1) * 128]
                kp = k_ref[0, rows, p * 128:(p + 1) * 128]
                kaug[2 * p, rows, :] = jnp.where(lo_half, kp, a)
                kaug[2 * p + 1, rows, :] = jnp.where(lo_half, a, kp)

    ones_hi = jnp.where((lane >= HEAD_DIM) & (lane < HEAD_DIM + 3), 1.0, 0.0).astype(bf16)
    ones_lo = jnp.where(lane < 3, 1.0, 0.0).astype(bf16)
    qa = []
    for p in range(n_heads // 2):
        qp = q_ref[0, :, p * 128:(p + 1) * 128]
        qa.append(jnp.where(lo_half, qp, ones_hi))
        qa.append(jnp.where(lo_half, ones_lo, qp))
    keep = lax.broadcasted_iota(i32, (tq, tq), 0) <= lax.broadcasted_iota(i32, (tq, tq), 1)
    acc[...] = jnp.zeros_like(acc)

    def score(jb, buf):
        k0 = pl.multiple_of(jb * tq, tq)
        for h in range(n_heads):
            buf[h] = _dot_nt(kaug[h, pl.ds(k0, tq), :], qa[h])

    def absorb(jb, ms, masked, buf):
        new_ms = []
        for h in range(n_heads):
            s = buf[h]
            if masked:
                s = jnp.where(keep, s, NEG)
            m_new = jnp.maximum(ms[h], jnp.max(s, axis=0, keepdims=True))
            alpha = jnp.exp2(ms[h] - m_new)
            p = jnp.exp2(s - m_new).astype(bf16)
            new_ms.append(m_new)
            acc[h] = acc[h] * alpha + _dot(vt[jb, h * V_ROWS:(h + 1) * V_ROWS, :], p)
        return tuple(new_ms)

    def pair(pp, ms):
        j0 = 2 * pp
        score(j0 + 1, sc1)
        ms = absorb(j0, ms, False, sc0)
        score(j0 + 2, sc0)
        return absorb(j0 + 1, ms, False, sc1)

    score(0, sc0)
    ms = lax.fori_loop(0, i // 2, pair, tuple(jnp.full((1, tq), NEG, f32) for _ in range(n_heads)))

    @pl.when(i % 2 == 0)
    def _even():
        absorb(i, ms, True, sc0)

    @pl.when(i % 2 == 1)
    def _odd():
        score(i, sc1)
        absorb(i, absorb(i - 1, ms, False, sc0), True, sc1)

    out_t = jnp.concatenate([acc[h, 0:HEAD_DIM, :] * (1.0 / acc[h, HEAD_DIM:HEAD_DIM + 1, :]) for h in range(n_heads)],
                            axis=0)
    ms_ = jnp.mean(out_t * out_t, axis=0, keepdims=True)
    out_t = out_t * lax.rsqrt(ms_ + EPS)
    o_ref[0] = (out_t.T * ng_ref[...]).astype(bf16)


def _piece_select():
    sel = np.zeros((384, 4 * 128), np.float32)
    for pair in range(4):
        for j in range(3):
            sel[j * 128 + 8 + 2 * pair, pair * 128 + HEAD_DIM + j] = 1.0
            sel[j * 128 + 8 + 2 * pair + 1, pair * 128 + j] = 1.0
    return jnp.asarray(sel, bf16)


def _attn(qkv, cumc, norm_g):
    bsz, t, _ = qkv.shape
    tq = min(ATT_BLOCK, t)
    return pl.pallas_call(
        functools.partial(_attn_kernel, tq=tq, t=t),
        grid=(bsz, t // tq),
        in_specs=[pl.BlockSpec((1, tq, D_ATT), lambda b, i: (b, i, 0)),
                  pl.BlockSpec((1, t, D_ATT), lambda b, i: (b, 0, 1)),
                  pl.BlockSpec((1, t, D_ATT), lambda b, i: (b, 0, 2)),
                  pl.BlockSpec((1, t, 128), lambda b, i: (b, 0, 0)),
                  pl.BlockSpec((384, 512), lambda b, i: (0, 0)),
                  pl.BlockSpec((1, D_ATT), lambda b, i: (0, 0))],
        out_specs=pl.BlockSpec((1, tq, D_ATT), lambda b, i: (b, i, 0)),
        out_shape=jax.ShapeDtypeStruct((bsz, t, D_ATT), bf16),
        scratch_shapes=[pltpu.VMEM((D_ATT // HEAD_DIM, t, 128), bf16),
                        pltpu.VMEM((t // tq, (D_ATT // HEAD_DIM) * V_ROWS, tq), bf16),
                        pltpu.VMEM((D_ATT // HEAD_DIM, V_ROWS, tq), f32),
                        pltpu.VMEM((D_ATT // HEAD_DIM, tq, tq), f32),
                        pltpu.VMEM((D_ATT // HEAD_DIM, tq, tq), f32)],
        compiler_params=pltpu.CompilerParams(dimension_semantics=("parallel", "arbitrary"),
                                             vmem_limit_bytes=VMEM_LIMIT),
        name="attn",
    )(qkv, qkv, qkv, cumc, _piece_select(), norm_g)


def _layer_norm(y, g, b):
    mu = jnp.mean(y, axis=-1, keepdims=True)
    yc = y - mu
    var = jnp.mean(yc * yc, axis=-1, keepdims=True)
    return yc * lax.rsqrt(var + EPS) * g + b


def _outproj_kernel(ys_ref, ya_ref, x_ref, gt_ref, sc_ref, sh_ref, wo_ref, lng_ref, lnb_ref, wrh_ref, wrl_ref, rb_ref,
                    x1_ref, rt_ref, dest_ref, cnt_ref, tbl_ref, xrows_hbm,
                    carry, galloc, tbl, u2t, didx_v, didx_s, cnt_v, cnt_s, zeros, idx_sem, sc_sem, z_sem,
                    *, tm, blk, n_blk):
    i = pl.program_id(0)
    last = pl.num_programs(0) - 1
    cur = i % 2
    prev = 1 - cur

    def idx_ready(slot):
        return pltpu.make_async_copy(didx_v.at[slot], didx_s.at[slot], idx_sem.at[slot])

    def dispatched(slot):
        return pltpu.make_async_copy(u2t.at[slot], u2t.at[slot], sc_sem.at[slot])

    def dispatch_copy(slot, r, k):
        return pltpu.make_async_copy(u2t.at[slot, pl.ds(r * PACK_TILE, PACK_TILE)],
                                     xrows_hbm.at[pl.ds(didx_s[slot, k, r] * PACK_TILE, PACK_TILE)], sc_sem.at[slot])

    @pl.when(i == 0)
    def _init():
        carry[...] = jnp.zeros_like(carry)
        galloc[...] = jnp.zeros_like(galloc)
        tbl[...] = jnp.zeros_like(tbl)

    @pl.when(i >= 2)
    def _reuse():
        dispatched(cur).wait()
        dispatched(cur).wait()

    def step(dispatch_prev):
        n_parts = OUTPROJ_PARTS
        n_slices = 3 * n_parts + 2

        def dispatch_slice(c):
            if dispatch_prev:
                for r_ in range(c * tm // n_slices, (c + 1) * tm // n_slices):
                    dispatch_copy(prev, r_, 0).start(priority=0)
                    dispatch_copy(prev, r_, 1).start(priority=1)

        if dispatch_prev:
            idx_ready(prev).wait()
        part = tm // n_parts
        hs = []
        for a in range(n_parts):
            rs = slice(a * part, (a + 1) * part)
            dispatch_slice(a)
            hs.append(_dot(ys_ref[rs, :], wo_ref[0:D_SSM, :]) + _dot(ya_ref[rs, :], wo_ref[D_SSM:D, :]))
        logit_parts = []
        for a in range(n_parts):
            rs = slice(a * part, (a + 1) * part)
            dispatch_slice(n_parts + 2 * a)
            x1 = _layer_norm(ALPHA * x_ref[rs, :] + (1.0 + gt_ref[0, 0]) * hs[a], lng_ref[...], lnb_ref[...])
            x1_ref[rs, :] = x1
            u2 = x1 * (1.0 + sc_ref[0, 0]) + sh_ref[0, 0]
            uh = u2.astype(bf16)
            _store_packed_rows(u2t.at[cur, pl.ds(a * part * PACK_TILE, part * PACK_TILE)], _pack_rows(uh))
            dispatch_slice(n_parts + 2 * a + 1)
            ul = (u2 - uh.astype(f32)).astype(bf16)
            logit_parts.append((_dot(uh, wrh_ref[...]) + _dot(ul, wrh_ref[...])) + _dot(uh, wrl_ref[...]))
        logits = jnp.concatenate(logit_parts, axis=0) + rb_ref[...]
        lane = lax.broadcasted_iota(i32, (tm, 128), 1).astype(f32)
        big = jnp.float32(1e9)

        def first_max(vals):
            m = jnp.max(vals, axis=-1, keepdims=True)
            return m, jnp.min(jnp.where(vals == m, lane, big), axis=-1, keepdims=True)

        gl = jnp.where(lane < N_GROUPS_R, logits, NEG)
        gmax, gidx = first_max(gl)
        g_p = 1.0 / jnp.sum(jnp.exp(gl - gmax), axis=-1, keepdims=True)
        lo = N_GROUPS_R + EXPERTS_PER_GROUP * gidx
        el = jnp.where((lane >= lo) & (lane < lo + EXPERTS_PER_GROUP), logits, NEG)
        m1, i1 = first_max(el)
        el2 = jnp.where(lane == i1, NEG, el)
        m2, i2 = first_max(el2)
        r = jnp.exp(m2 - m1)
        w1 = g_p / (1.0 + r)
        w2 = g_p * r / (1.0 + r)
        dispatch_slice(3 * n_parts)

        oh1 = lane == i1
        oh2 = lane == i2
        oh = jnp.where(oh1 | oh2, 1.0, 0.0)
        r_i = lax.broadcasted_iota(i32, (tm, tm), 0)
        c_i = lax.broadcasted_iota(i32, (tm, tm), 1)
        lower = jnp.where(r_i > c_i, 1.0, 0.0).astype(bf16)
        c_old = carry[...]
        prefix = _dot(lower, oh.astype(bf16)) + c_old
        rank1 = jnp.sum(jnp.where(oh1, prefix, 0.0), axis=-1, keepdims=True)
        rank2 = jnp.sum(jnp.where(oh2, prefix, 0.0), axis=-1, keepdims=True)
        c_new = c_old + jnp.sum(oh, axis=0, keepdims=True)
        carry[...] = c_new
        dispatch_slice(3 * n_parts + 1)

        nb_old = jnp.floor((c_old + (blk - 1)) * (1.0 / blk))
        nb_new = jnp.floor((c_new + (blk - 1)) * (1.0 / blk))
        fresh = nb_new - nb_old
        sq_r = lax.broadcasted_iota(i32, (128, 128), 0)
        sq_c = lax.broadcasted_iota(i32, (128, 128), 1)
        before = jnp.where(sq_r < sq_c, 1.0, 0.0).astype(bf16)
        base = galloc[...] + _dot(jnp.broadcast_to(fresh, (8, 128)).astype(bf16), before)[0:1, :]
        galloc[...] = galloc[...] + jnp.sum(fresh, axis=-1, keepdims=True)
        ordinal = sq_r.astype(f32)
        tbl[...] = jnp.where((ordinal >= nb_old) & (ordinal < nb_new), base + (ordinal - nb_old), tbl[...])
        tbl_ref[...] = tbl[...]
        cnt_ref[...] = jnp.concatenate([c_new, galloc[...], jnp.zeros((6, 128), f32)], axis=0)

        rt = jnp.where(lane == 4, w1, jnp.where(lane == 5, w2, 0.0))
        rt_ref[...] = rt

        ids_bf = tbl[...].astype(bf16)

        def row_of(onehot, rank):
            nth = jnp.floor(rank * (1.0 / blk))
            ids = _dot(jnp.where(lane == nth, 1.0, 0.0).astype(bf16), ids_bf)
            return jnp.sum(jnp.where(onehot, ids, 0.0), axis=-1, keepdims=True) * blk + (rank - nth * blk)

        d1 = row_of(oh1, rank1)
        d2 = row_of(oh2, rank2)
        dd = jnp.where(lane == 0, d1, jnp.where(lane == 1, d2, 0.0)).T[0:8, :].astype(i32)
        dest_ref[...] = dd
        didx_v[cur] = dd
        idx_ready(cur).start()

    pl.when(i == 0)(lambda: step(False))
    pl.when(i > 0)(lambda: step(True))

    @pl.when(i == last)
    def _epilogue():
        idx_ready(cur).wait()

        def issue(r_, c):
            dispatch_copy(cur, r_, 0).start()
            dispatch_copy(cur, r_, 1).start()
            return c
        lax.fori_loop(0, tm, issue, 0)

        cnt = carry[...]
        n_blocks = jnp.floor((cnt + (blk - 1)) * (1.0 / blk))
        ordinal = lax.broadcasted_iota(i32, (128, 128), 0).astype(f32)
        last_id = jnp.sum(jnp.where(ordinal == n_blocks - 1.0, tbl[...], 0.0), axis=0, keepdims=True)
        used = cnt - (n_blocks - 1.0) * blk
        cnt_v[...] = jnp.concatenate([last_id * blk + used, blk - used, galloc[...], jnp.zeros((5, 128), f32)],
                                     axis=0).astype(i32)
        to_smem = pltpu.make_async_copy(cnt_v, cnt_s, z_sem)
        to_smem.start()
        to_smem.wait()
        zeros[...] = jnp.zeros_like(zeros)
        sizes = [1 << b for b in reversed(range(blk.bit_length() - 1))]

        def for_each_fill(fn):
            def tail(e, c):
                pad = cnt_s[1, N_GROUPS_R + e]
                off = cnt_s[0, N_GROUPS_R + e]
                for sz in sizes:
                    @pl.when((pad & sz) != 0)
                    def _(off=off, sz=sz):
                        fn(pltpu.make_async_copy(zeros.at[pl.ds(0, sz * PACK_TILE)],
                                                 xrows_hbm.at[pl.ds(off * PACK_TILE, sz * PACK_TILE)], z_sem))
                    off = off + (pad & sz)
                return c
            lax.fori_loop(0, N_EXPERTS, tail, 0)

            def whole(b, c):
                fn(pltpu.make_async_copy(zeros, xrows_hbm.at[pl.ds(b * blk * PACK_TILE, blk * PACK_TILE)], z_sem))
                return c
            lax.fori_loop(cnt_s[2, 0], n_blk, whole, 0)

        for_each_fill(lambda copy: copy.start())
        for_each_fill(lambda copy: copy.wait())

        @pl.when(i > 0)
        def _prev_done():
            dispatched(prev).wait()
            dispatched(prev).wait()
        dispatched(cur).wait()
        dispatched(cur).wait()


def _outproj(y_ssm, y_att, x, mod4, w_out, ln_g, ln_b, wr_hi, wr_lo, rb, t, blk, n_blk):
    n = x.shape[0]
    tm = min(OUTPROJ_ROWS, t)
    nt = t // tm
    assert n // blk + 1 <= 128 and n_blk <= 256, "block-id table: 128 blocks per expert, ids exact in bf16"
    vec = lambda k: pl.BlockSpec((1, 1, 1, D), lambda i, k=k: (k, i // nt, 0, 0))
    full = lambda shape: pl.BlockSpec(shape, lambda i: (0,) * len(shape))
    rows = lambda w: pl.BlockSpec((tm, w), lambda i: (i, 0))
    return pl.pallas_call(
        functools.partial(_outproj_kernel, tm=tm, blk=blk, n_blk=n_blk),
        grid=(n // tm,),
        in_specs=[rows(D_SSM), rows(D_ATT), rows(D), vec(2), vec(4), vec(3),
                  full((D, D)), full((1, D)), full((1, D)), full((D, 128)), full((D, 128)), full((1, 128))],
        out_specs=[rows(D), rows(128), pl.BlockSpec((8, tm), lambda i: (0, i)), full((8, 128)), full((128, 128)),
                   pl.BlockSpec(memory_space=pl.ANY)],
        out_shape=[jax.ShapeDtypeStruct((n, D), f32), jax.ShapeDtypeStruct((n, 128), f32),
                   jax.ShapeDtypeStruct((8, n), i32), jax.ShapeDtypeStruct((8, 128), f32),
                   jax.ShapeDtypeStruct((128, 128), f32),
                   jax.ShapeDtypeStruct((n_blk * blk * PACK_TILE, 128), u32)],
        scratch_shapes=[pltpu.VMEM((1, 128), f32), pltpu.VMEM((1, 128), f32), pltpu.VMEM((128, 128), f32),
                        pltpu.VMEM((2, tm * PACK_TILE, 128), u32),
                        pltpu.VMEM((2, 8, tm), i32), pltpu.SMEM((2, 8, tm), i32),
                        pltpu.VMEM((8, 128), i32), pltpu.SMEM((8, 128), i32),
                        pltpu.VMEM((blk * PACK_TILE, 128), u32),
                        pltpu.SemaphoreType.DMA((2,)), pltpu.SemaphoreType.DMA((2,)), pltpu.SemaphoreType.DMA(())],
        compiler_params=pltpu.CompilerParams(dimension_semantics=("arbitrary",), vmem_limit_bytes=VMEM_LIMIT),
        name="outproj",
    )(y_ssm, y_att, x, mod4, mod4, mod4, w_out, ln_g, ln_b, wr_hi, wr_lo, rb)


def _tile_copy(src_hbm, row, buf, slot, sem):
    return pltpu.make_async_copy(src_hbm.at[row], buf.at[pl.ds(slot * ROW_TILE, ROW_TILE)], sem)


def _moe_kernel(be_ref, br_ref, nu_ref, nxt_ref, par_ref, nv_ref, x_ref, wg_hbm, wu_hbm, wd_hbm, y_ref,
                wfg, wfu, wfd, wgb, wub, wdb, wsem, *, blk):
    i = pl.program_id(0)
    used = i < nu_ref[0]

    def fetch(e, slot):
        return [pltpu.make_async_copy(src.at[e], dst.at[slot], wsem.at[slot])
                for src, dst in ((wg_hbm, wfg), (wu_hbm, wfu), (wd_hbm, wfd))]

    @pl.when(i == 0)
    def _first():
        for c in fetch(be_ref[0], par_ref[0]):
            c.start()

    @pl.when(jnp.logical_and(used, jnp.logical_or(i == 0, be_ref[i] != be_ref[jnp.maximum(i - 1, 0)])))
    def _switch():
        slot = par_ref[i]
        for c in fetch(be_ref[i], slot):
            c.wait()

        @pl.when(nxt_ref[i] >= 0)
        def _next():
            for c in fetch(nxt_ref[i], 1 - slot):
                c.start(priority=1)
        wgb[...] = wfg[slot].astype(bf16)
        wub[...] = wfu[slot].astype(bf16)
        wdb[...] = wfd[slot].astype(bf16)

    @pl.when(jnp.logical_not(used))
    def _spare():
        y_ref[...] = jnp.zeros_like(y_ref)

    def mlp(rows):
        xa, xb = _load_packed_rows(x_ref.at[pl.ds(0, rows * PACK_TILE)], rows)
        half = D // 2
        gate = _dot(xa, wgb[0:half, :]) + _dot(xb, wgb[half:D, :])
        up = _dot(xa, wub[0:half, :]) + _dot(xb, wub[half:D, :])
        hid = (_silu(gate) * up).astype(bf16)
        _store_row_tiles(y_ref.at[pl.ds(0, rows * ROW_TILE)], _dot(hid, wdb[...]))

    sparse = nv_ref[i] <= blk // 2

    @pl.when(jnp.logical_and(used, jnp.logical_not(sparse)))
    def _full():
        mlp(blk)

    @pl.when(jnp.logical_and(used, sparse))
    def _half():
        mlp(blk // 2)
        y_ref[pl.ds(blk // 2 * ROW_TILE, blk // 2 * ROW_TILE), :] = jnp.zeros((blk // 2 * ROW_TILE, 128), f32)


def _moe(block_e, block_row, n_used, next_e, parity, n_valid, x_rows, w_gate, w_up, w_down, blk):
    nblk = block_e.shape[0]
    rspec = pl.BlockSpec((blk * ROW_TILE, 128), lambda i, be, br, *_: (br[i], 0))
    xspec = pl.BlockSpec((blk * PACK_TILE, 128), lambda i, be, br, nu, *_: (br[jnp.minimum(i, nu[0] - 1)], 0))
    anyspec = pl.BlockSpec(memory_space=pl.ANY)
    grid_spec = pltpu.PrefetchScalarGridSpec(
        num_scalar_prefetch=6,
        grid=(nblk,),
        in_specs=[xspec, anyspec, anyspec, anyspec],
        out_specs=rspec,
        scratch_shapes=[pltpu.VMEM((2, D, D_EXPERT), f32), pltpu.VMEM((2, D, D_EXPERT), f32),
                        pltpu.VMEM((2, D_EXPERT, D), f32),
                        pltpu.VMEM((D, D_EXPERT), bf16), pltpu.VMEM((D, D_EXPERT), bf16),
                        pltpu.VMEM((D_EXPERT, D), bf16),
                        pltpu.SemaphoreType.DMA((2,))],
    )
    return pl.pallas_call(
        functools.partial(_moe_kernel, blk=blk),
        grid_spec=grid_spec,
        out_shape=jax.ShapeDtypeStruct((nblk * blk * ROW_TILE, 128), f32),
        compiler_params=pltpu.CompilerParams(dimension_semantics=("arbitrary",), vmem_limit_bytes=VMEM_LIMIT),
        name="moe",
    )(block_e, block_row, n_used, next_e, parity, n_valid, x_rows, w_gate, w_up, w_down)


def _combine_kernel(d1_ref, d2_ref, y_hbm, x1_ref, rt_ref, gt_ref, lng_ref, lnb_ref, o_ref, buf, sem, *, tm):
    i = pl.program_id(0)
    last = pl.num_programs(0) - 1

    @pl.when(i == 0)
    def _prologue():
        for s in range(2):
            def issue(r, c, s=s):
                base = jnp.minimum(s, last) * tm
                _tile_copy(y_hbm, d1_ref[base + r], buf.at[s, 0], r, sem.at[s]).start()
                _tile_copy(y_hbm, d2_ref[base + r], buf.at[s, 1], r, sem.at[s]).start()
                return c
            lax.fori_loop(0, tm, issue, 0)

    def rows_ready(slot):
        return pltpu.make_async_copy(buf.at[slot], buf.at[slot], sem.at[slot])

    slot = i % 3
    nxt = (i + 2) % 3
    base = jnp.minimum(i + 2, last) * tm
    rows_ready(slot).wait()
    for r in range(tm):
        _tile_copy(y_hbm, d1_ref[base + r], buf.at[nxt, 0], r, sem.at[nxt]).start(priority=0)
        _tile_copy(y_hbm, d2_ref[base + r], buf.at[nxt, 1], r, sem.at[nxt]).start(priority=1)
    rt = rt_ref[...]
    moe = rt[:, 4:5] * _load_row_tiles(buf.at[slot, 0], tm) + rt[:, 5:6] * _load_row_tiles(buf.at[slot, 1], tm)
    y = ALPHA * x1_ref[...] + (1.0 + gt_ref[0, 0]) * moe
    o_ref[...] = _layer_norm(y, lng_ref[...], lnb_ref[...])

    @pl.when(i == last)
    def _drain():
        rows_ready((i + 1) % 3).wait()
        rows_ready((i + 2) % 3).wait()


def _combine(dest1, dest2, y_rows, x1, rt, mod4, ln_g, ln_b, t):
    n = x1.shape[0]
    tm = min(COMBINE_ROWS, t)
    nt = t // tm
    full = lambda shape: pl.BlockSpec(shape, lambda i, d1, d2: (0,) * len(shape))
    rows = lambda w: pl.BlockSpec((tm, w), lambda i, d1, d2: (i, 0))
    grid_spec = pltpu.PrefetchScalarGridSpec(
        num_scalar_prefetch=2,
        grid=(n // tm,),
        in_specs=[pl.BlockSpec(memory_space=pl.ANY), rows(D), rows(128),
                  pl.BlockSpec((1, 1, 1, D), lambda i, d1, d2: (5, i // nt, 0, 0)),
                  full((1, D)), full((1, D))],
        out_specs=rows(D),
        scratch_shapes=[pltpu.VMEM((3, 2, tm * ROW_TILE, 128), f32), pltpu.SemaphoreType.DMA((3,))],
    )
    return pl.pallas_call(
        functools.partial(_combine_kernel, tm=tm),
        grid_spec=grid_spec,
        out_shape=jax.ShapeDtypeStruct((n, D), f32),
        compiler_params=pltpu.CompilerParams(dimension_semantics=("arbitrary",), vmem_limit_bytes=VMEM_LIMIT),
        name="combine",
    )(dest1, dest2, y_rows, x1, rt, mod4, ln_g, ln_b)


def _hi_lo(w):
    hi = w.astype(bf16)
    return hi, (w - hi.astype(f32)).astype(bf16)


def _pad_lanes(v, offset, width):
    return jnp.zeros((width,), f32).at[offset:offset + v.shape[0]].set(v)


def _layer(x, c, ada_w, ada_b, w_in, conv_w, conv_b, dt_bias, a_log, d_skip, ssm_norm_g, fg_bias, att_norm_g,
           w_out, ln1_g, ln1_b, router_g_w, router_g_b, router_e_w, router_e_b, w_gate, w_up, w_down, ln2_g, ln2_b):
    bsz, t, _ = x.shape
    n = bsz * t

    mod4 = _ada(c, ada_w, ada_b).reshape(6, bsz, 1, D)

    w_main, w_small = _pack_w_in(w_in)
    zx, bc, qkv, dtf = _inproj(x, mod4, w_main, w_small, conv_w, conv_b.reshape(1, -1))

    pc = jnp.stack([_pad_lanes(dt_bias, 0, 128), _pad_lanes(a_log, 0, 128), _pad_lanes(fg_bias, 8, 128)]
                   + [jnp.zeros((128,), f32)] * 5)
    rep = lambda v: jnp.repeat(v, HEAD_DIM)
    pe = jnp.stack([rep(dt_bias), rep(a_log), rep(d_skip), ssm_norm_g] + [jnp.zeros((D_SSM,), f32)] * 4)
    y_ssm, cumc = _ssd(zx, bc, dtf, pc, pe)

    y_att = _attn(qkv, cumc, att_norm_g.reshape(1, -1))

    wr = jnp.concatenate([router_g_w, router_e_w, jnp.zeros((D, 128 - N_GROUPS_R - N_EXPERTS), f32)], axis=1)
    wr_hi, wr_lo = _hi_lo(wr)
    rb = jnp.concatenate([router_g_b, router_e_b, jnp.zeros((128 - N_GROUPS_R - N_EXPERTS,), f32)]).reshape(1, 128)
    blk = MOE_ROWS
    nblk = (2 * n) // blk + N_EXPERTS
    x1, rt, dest, cnt, tbl, x_rows = _outproj(y_ssm.reshape(n, D_SSM), y_att.reshape(n, D_ATT), x.reshape(n, D), mod4,
                                              w_out.astype(bf16), ln1_g.reshape(1, D), ln1_b.reshape(1, D),
                                              wr_hi, wr_lo, rb, t, blk, nblk)

    counts = cnt[0, N_GROUPS_R:N_GROUPS_R + N_EXPERTS].astype(i32)
    nb = (counts + blk - 1) // blk
    nb_end = jnp.cumsum(nb)
    n_used = nb_end[N_EXPERTS - 1:]
    every = jnp.arange(nblk, dtype=i32)
    step = jnp.minimum(every, n_used[0] - 1)
    owner = (nb_end[None, :] <= step[:, None]).astype(i32)
    block_e = jnp.sum(owner, axis=1)
    first = jnp.sum(owner * nb[None, :], axis=1)
    lanes = jnp.arange(128, dtype=i32)[None, :]
    nth_row = jnp.dot((lanes == (step - first)[:, None]).astype(f32), tbl, precision=lax.Precision.HIGHEST)
    block_id = jnp.sum(jnp.where(lanes == (block_e + N_GROUPS_R)[:, None], nth_row, 0.0), axis=1).astype(i32)
    block_row = jnp.where(every < n_used[0], block_id, every)
    run_end = first + jnp.sum((jnp.arange(N_EXPERTS)[None, :] == block_e[:, None]) * nb[None, :], axis=1)
    next_e = jnp.where(run_end < n_used[0], jnp.sum((nb_end[None, :] <= run_end[:, None]).astype(i32), axis=1), -1)
    parity = jnp.sum(((jnp.arange(N_EXPERTS)[None, :] < block_e[:, None]) & (nb[None, :] > 0)).astype(i32), axis=1) % 2

    own = (jnp.arange(N_EXPERTS)[None, :] == block_e[:, None]).astype(i32)
    n_valid = jnp.clip(jnp.sum(own * counts[None, :], axis=1) - (step - first) * blk, 0, blk)
    y_rows = _moe(block_e, block_row, n_used, next_e, parity, n_valid, x_rows,
                  w_gate, w_up, w_down, blk)
    out = _combine(dest[0], dest[1], y_rows.reshape(nblk * blk, ROW_TILE, 128), x1, rt, mod4,
                   ln2_g.reshape(1, D), ln2_b.reshape(1, D), t)
    return out.reshape(bsz, t, D)


def kernel(x, c, ada_w, ada_b, w_in, conv_w, conv_b, dt_bias, a_log, d_skip, ssm_norm_g, fg_bias, att_norm_g, w_out,
           ln1_g, ln1_b, router_g_w, router_g_b, router_e_w, router_e_b, w_gate, w_up, w_down, ln2_g, ln2_b):
    depth = ada_w.shape[0]
    for l in range(depth):
        x = _layer(x, c, ada_w[l], ada_b[l], w_in[l], conv_w[l], conv_b[l], dt_bias[l], a_log[l], d_skip[l],
                   ssm_norm_g[l], fg_bias[l], att_norm_g[l], w_out[l], ln1_g[l], ln1_b[l], router_g_w[l],
                   router_g_b[l], router_e_w[l], router_e_b[l], w_gate[l], w_up[l], w_down[l], ln2_g[l], ln2_b[l])
    return x
```

```python
import functools

import jax
import jax.numpy as jnp
import numpy as np
from jax import lax
from jax.experimental import pallas as pl
from jax.experimental.pallas import tpu as pltpu

f32 = jnp.float32
bf16 = jnp.bfloat16
i32 = jnp.int32

D = 1024
D_SSM = 512
D_ATT = 512
HEAD_DIM = 64
GROUP_W = 256
N_STATE = 128
CONV_K = 4
N_GROUPS_R = 4
EXPERTS_PER_GROUP = 8
N_EXPERTS = 32
D_EXPERT = 512
ALPHA = 2.0 ** 0.25
EPS = 1e-5
NEG = -1e30
LOG2E = 1.4426950408889634
QK_SCALE = HEAD_DIM ** -0.5 * LOG2E
V_ROWS = 80

SSD_CHUNK = 256
ATT_BLOCK = 256
INPROJ_ROWS = 512
OUTPROJ_ROWS = 512
OUTPROJ_PARTS = 2
MOE_ROWS = 512
COMBINE_ROWS = 256
VMEM_LIMIT = 48 * 1024 * 1024


def _dot(a, b):
    return jnp.dot(a, b, preferred_element_type=f32)


def _dot_nt(a, b):
    return lax.dot_general(a, b, (((1,), (1,)), ((), ())), preferred_element_type=f32)


def _dot_tn(a, b):
    return lax.dot_general(a, b, (((0,), (0,)), ((), ())), preferred_element_type=f32)


def _split3(v):
    hi = v.astype(bf16)
    r1 = v - hi.astype(f32)
    mid = r1.astype(bf16)
    lo = (r1 - mid.astype(f32)).astype(bf16)
    return hi, mid, lo


def _dot_exact_lhs(m, v):
    hi, mid, lo = _split3(v)
    return (_dot(m, hi) + _dot(m, mid)) + _dot(m, lo)


def _dot_exact_rhs(v, m):
    hi, mid, lo = _split3(v)
    return (_dot(hi, m) + _dot(mid, m)) + _dot(lo, m)


ROW_TILE = 8


def _store_row_tiles(ref, val):
    rows = val.shape[0]
    for c in range(ROW_TILE):
        ref[pl.ds(c, rows, stride=ROW_TILE), :] = val[:, c * 128:(c + 1) * 128]


def _load_row_tiles(ref, rows):
    return jnp.concatenate([ref[pl.ds(c, rows, stride=ROW_TILE), :] for c in range(ROW_TILE)], axis=1)


PACK_TILE = 4
u32 = jnp.uint32


def _pack_rows(vb):
    lo = pltpu.bitcast(vb[:, 0:D // 2].astype(f32), u32) >> 16
    hi = pltpu.bitcast(vb[:, D // 2:D].astype(f32), u32) & jnp.uint32(0xFFFF0000)
    return lo | hi


def _store_packed_rows(ref, packed):
    rows = packed.shape[0]
    for c in range(PACK_TILE):
        ref[pl.ds(c, rows, stride=PACK_TILE), :] = packed[:, c * 128:(c + 1) * 128]


def _load_packed_rows(ref, rows):
    words = [ref[pl.ds(c, rows, stride=PACK_TILE), :] for c in range(PACK_TILE)]
    lo = jnp.concatenate([pltpu.bitcast(w << 16, f32) for w in words], axis=1)
    hi = jnp.concatenate([pltpu.bitcast(w & jnp.uint32(0xFFFF0000), f32) for w in words], axis=1)
    return lo.astype(bf16), hi.astype(bf16)


def _softplus(x):
    return jnp.maximum(x, 0.0) + jnp.log1p(jnp.exp(-jnp.abs(x)))


def _silu(x):
    return x * jax.nn.sigmoid(x)


def _ada_kernel(c_ref, w_ref, b_ref, o_ref):
    s = _silu(c_ref[...]).astype(bf16)
    o_ref[0] = _dot(s, w_ref[...].astype(bf16)) + b_ref[0]


def _ada(c, w, b):
    bsz = c.shape[0]
    return pl.pallas_call(
        _ada_kernel,
        grid=(6,),
        in_specs=[pl.BlockSpec((bsz, D), lambda j: (0, 0)),
                  pl.BlockSpec((D, D), lambda j: (0, j)),
                  pl.BlockSpec((1, 1, D), lambda j: (j, 0, 0))],
        out_specs=pl.BlockSpec((1, bsz, D), lambda j: (j, 0, 0)),
        out_shape=jax.ShapeDtypeStruct((6, bsz, D), f32),
        compiler_params=pltpu.CompilerParams(dimension_semantics=("arbitrary",), vmem_limit_bytes=VMEM_LIMIT),
        name="ada",
    )(c, w, b.reshape(6, 1, D))


W_IN_COLS = 3088
PACK_ROWS = 256


def _pack_w_in_kernel(wt_ref, wm_ref, ws_ref):
    for j in range(3072 // PACK_ROWS):
        src = j * PACK_ROWS if j * PACK_ROWS < 1536 else j * PACK_ROWS + 8
        wm_ref[:, j * PACK_ROWS:(j + 1) * PACK_ROWS] = wt_ref[src:src + PACK_ROWS, :].T.astype(bf16)
    small = jnp.concatenate([wt_ref[1536:1544, :], wt_ref[W_IN_COLS - 8:W_IN_COLS, :], jnp.zeros((112, D), f32)],
                            axis=0).T
    hi = small.astype(bf16)
    ws_ref[:, 0:128] = hi
    ws_ref[:, 128:256] = (small - hi.astype(f32)).astype(bf16)


def _pack_w_in(w_in):
    full = lambda shape: pl.BlockSpec(shape, lambda i: (0,) * len(shape))
    return pl.pallas_call(
        _pack_w_in_kernel,
        grid=(1,),
        in_specs=[full((W_IN_COLS, D))],
        out_specs=[full((D, 3072)), full((D, 256))],
        out_shape=[jax.ShapeDtypeStruct((D, 3072), bf16), jax.ShapeDtypeStruct((D, 256), bf16)],
        compiler_params=pltpu.CompilerParams(dimension_semantics=("arbitrary",), vmem_limit_bytes=VMEM_LIMIT),
        name="pack_w_in",
    )(w_in.T)


def _inproj_kernel(x_ref, sc_ref, sh_ref, wm_ref, ws_ref, cw_ref, cb_ref, zx_ref, bc_ref, qkv_ref, dtf_ref, xcat, *, tm):
    i = pl.program_id(1)

    @pl.when(i == 0)
    def _init():
        xcat[0:8, :] = jnp.zeros((8, 2 * D_SSM), f32)

    u = x_ref[0] * (1.0 + sc_ref[0, 0]) + sh_ref[0, 0]
    ub = u.astype(bf16)
    xcat[8:8 + tm, 0:512] = _dot(ub, wm_ref[:, 512:1024])
    xcat[8:8 + tm, 512:1024] = _dot(ub, wm_ref[:, 1024:1536])
    zx_ref[0, :, 0:512] = _dot(ub, wm_ref[:, 0:512])
    acc = cw_ref[0:1, :] * xcat[5:5 + tm, :] + cb_ref[...]
    for k in range(1, CONV_K):
        acc = acc + cw_ref[k:k + 1, :] * xcat[5 + k:5 + k + tm, :]
    xcat[0:8, :] = xcat[tm:tm + 8, :]
    xbc = _silu(acc)
    zx_ref[0, :, 512:1024] = xbc[:, 0:512]
    bc_ref[0] = xbc[:, 512:1024].astype(bf16)
    for j, scale in enumerate((QK_SCALE, 1.0, 1.0)):
        qkv_ref[0, :, j * 512:(j + 1) * 512] = (
            _dot(ub, wm_ref[:, 1536 + j * 512:1536 + (j + 1) * 512]) * scale).astype(bf16)
    ul = (u - ub.astype(f32)).astype(bf16)
    d_hl = _dot(ub, ws_ref[...])
    dtf_ref[0] = (d_hl[:, 0:128] + _dot(ul, ws_ref[:, 0:128])) + d_hl[:, 128:256]


def _inproj(x, mod4, w_main, ws, conv_w, conv_b):
    bsz, t, _ = x.shape
    tm = min(INPROJ_ROWS, t)
    vec = lambda k: pl.BlockSpec((1, 1, 1, D), lambda b, i, k=k: (k, b, 0, 0))
    full = lambda shape: pl.BlockSpec(shape, lambda b, i: (0,) * len(shape))
    rows = lambda w: pl.BlockSpec((1, tm, w), lambda b, i: (b, i, 0))
    return pl.pallas_call(
        functools.partial(_inproj_kernel, tm=tm),
        grid=(bsz, t // tm),
        in_specs=[rows(D), vec(1), vec(0), full((D, 3072)), full((D, 256)),
                  full((CONV_K, 2 * D_SSM)), full((1, 2 * D_SSM))],
        out_specs=[rows(1024), rows(512), rows(1536), rows(128)],
        out_shape=[jax.ShapeDtypeStruct((bsz, t, 1024), f32),
                   jax.ShapeDtypeStruct((bsz, t, 512), bf16),
                   jax.ShapeDtypeStruct((bsz, t, 1536), bf16),
                   jax.ShapeDtypeStruct((bsz, t, 128), f32)],
        scratch_shapes=[pltpu.VMEM((tm + 8, 2 * D_SSM), f32)],
        compiler_params=pltpu.CompilerParams(dimension_semantics=("parallel", "arbitrary"),
                                             vmem_limit_bytes=VMEM_LIMIT),
        name="inproj",
    )(x, mod4, mod4, w_main, ws, conv_w, conv_b)


def _ssd_kernel(z_ref, xs_ref, bc_ref, dtf_ref, pc_ref, pe_ref, y_ref, cumc_ref, state, carry, *, lc):
    j = pl.program_id(1)

    @pl.when(j == 0)
    def _init():
        state[...] = jnp.zeros_like(state)
        carry[...] = jnp.zeros_like(carry)

    xs = xs_ref[0]
    bm = bc_ref[0, :, 0:256]
    cm = bc_ref[0, :, 256:512]

    dtf = dtf_ref[0]
    lane = lax.broadcasted_iota(i32, (lc, 128), 1)
    dt_c = _softplus(dtf + pc_ref[0:1, :])
    a_c = dt_c * (-jnp.exp(pc_ref[1:2, :]))
    logf = -_softplus(-(dtf + pc_ref[2:3, :]))
    v = jnp.where(lane < 8, a_c, logf) * LOG2E
    r_i = lax.broadcasted_iota(i32, (lc, lc), 0)
    c_i = lax.broadcasted_iota(i32, (lc, lc), 1)
    tri = r_i >= c_i
    tri_b = jnp.where(tri, 1.0, 0.0).astype(bf16)
    cum = _dot_exact_lhs(tri_b, v) + carry[...]
    carry[...] = jnp.where(lane[0:1, :] >= 8, cum[lc - 1:lc, :], 0.0)
    cumc_ref[0] = cum
    cs_t = cum.T[0:8, :]

    e_r = lax.broadcasted_iota(i32, (128, D_SSM), 0)
    e_c = lax.broadcasted_iota(i32, (128, D_SSM), 1)
    expand = jnp.where(jnp.right_shift(e_c, 6) == e_r, 1.0, 0.0).astype(bf16)
    dt_e = _dot_exact_rhs(dt_c, expand)
    cs_e = _dot_exact_rhs(cum, expand)

    xdt = xs * dt_e
    ecs = jnp.exp2(cs_e)
    cs_last = cs_e[lc - 1:lc, :]
    dec_st = jnp.exp2(cs_last - cs_e)
    lane_g = lax.broadcasted_iota(i32, (1, GROUP_W), 1)
    ys = []
    for g in range(2):
        gs = slice(g * GROUP_W, (g + 1) * GROUP_W)
        bg = bm[:, g * N_STATE:(g + 1) * N_STATE]
        cg = cm[:, g * N_STATE:(g + 1) * N_STATE]
        cb = _dot_nt(cg, bg)
        xdt_g = xdt[:, gs]
        xdt_gb = xdt_g.astype(bf16)
        ms, xb = [], []
        for hh in range(4):
            h = g * 4 + hh
            lm = jnp.exp2(jnp.where(tri, cum[:, h:h + 1] - cs_t[h:h + 1, :], -jnp.inf))
            ms.append((cb * lm).astype(bf16))
            xb.append(jnp.where(jnp.right_shift(lane_g, 6) == hh, xdt_gb, jnp.zeros_like(xdt_gb)))
        y_diag = _dot(jnp.concatenate(ms, axis=1), jnp.concatenate(xb, axis=0))
        st = state[g]
        y_off = _dot(cg, st.astype(bf16)) * ecs[:, gs]
        upd = _dot_tn(bg, (xdt_g * dec_st[:, gs]).astype(bf16))
        state[g] = st * jnp.exp2(cs_last[:, gs]) + upd
        ys.append(y_diag + y_off + xs[:, gs] * pe_ref[2:3, gs])

    outs = []
    for g in range(2):
        gs = slice(g * GROUP_W, (g + 1) * GROUP_W)
        yg = ys[g] * _silu(z_ref[0, :, gs])
        ms_ = jnp.mean(yg * yg, axis=-1, keepdims=True)
        outs.append(yg * lax.rsqrt(ms_ + EPS))
    y_ref[0] = (jnp.concatenate(outs, axis=1) * pe_ref[3:4, :]).astype(bf16)


def _ssd(zx, bc, dtf, pc, pe):
    bsz, t, _ = zx.shape
    lc = min(SSD_CHUNK, t)
    col = lambda k: pl.BlockSpec((1, lc, 512), lambda b, j, k=k: (b, j, k))
    full = lambda shape: pl.BlockSpec(shape, lambda b, j: (0,) * len(shape))
    return pl.pallas_call(
        functools.partial(_ssd_kernel, lc=lc),
        grid=(bsz, t // lc),
        in_specs=[col(0), col(1), col(0),
                  pl.BlockSpec((1, lc, 128), lambda b, j: (b, j, 0)),
                  full((8, 128)), full((8, D_SSM))],
        out_specs=[pl.BlockSpec((1, lc, D_SSM), lambda b, j: (b, j, 0)),
                   pl.BlockSpec((1, lc, 128), lambda b, j: (b, j, 0))],
        out_shape=[jax.ShapeDtypeStruct((bsz, t, D_SSM), bf16),
                   jax.ShapeDtypeStruct((bsz, t, 128), f32)],
        scratch_shapes=[pltpu.VMEM((2, N_STATE, GROUP_W), f32),
                        pltpu.VMEM((1, 128), f32)],
        compiler_params=pltpu.CompilerParams(dimension_semantics=("parallel", "arbitrary"),
                                             vmem_limit_bytes=VMEM_LIMIT),
        name="ssd",
    )(zx, zx, bc, dtf, pc, pe)


def _attn_kernel(q_ref, k_ref, v_ref, cc_ref, psel_ref, ng_ref, o_ref, kaug, vt, acc, sc0, sc1, *, tq, t):
    i = pl.program_id(1)
    nkb = t // tq
    n_heads = D_ATT // HEAD_DIM
    lane = lax.broadcasted_iota(i32, (1, 128), 1)
    lo_half = lane < HEAD_DIM

    @pl.when(i == 0)
    def _build():
        ones_rows = jnp.where(lax.broadcasted_iota(i32, (V_ROWS - HEAD_DIM, tq), 0) == 0, 1.0, 0.0).astype(bf16)
        for jb in range(nkb):
            rows = slice(jb * tq, (jb + 1) * tq)
            v_t = v_ref[0, rows, :].astype(f32).T.astype(bf16)
            for h in range(n_heads):
                vt[jb, h * V_ROWS:h * V_ROWS + HEAD_DIM, :] = v_t[h * HEAD_DIM:(h + 1) * HEAD_DIM, :]
                vt[jb, h * V_ROWS + HEAD_DIM:(h + 1) * V_ROWS, :] = ones_rows
            pieces = jnp.concatenate(_split3(-cc_ref[0, rows, :]), axis=1)
            spare = _dot(pieces, psel_ref[...]).astype(bf16)
            for p in range(n_heads // 2):
                a = spare[:, p * 128:(p + 1) * 128]
                kp = k_ref[0, rows, p * 128:(p + 1) * 128]
                kaug[2 * p, rows, :] = jnp.where(lo_half, kp, a)
                kaug[2 * p + 1, rows, :] = jnp.where(lo_half, a, kp)

    ones_hi = jnp.where((lane >= HEAD_DIM) & (lane < HEAD_DIM + 3), 1.0, 0.0).astype(bf16)
    ones_lo = jnp.where(lane < 3, 1.0, 0.0).astype(bf16)
    qa = []
    for p in range(n_heads // 2):
        qp = q_ref[0, :, p * 128:(p + 1) * 128]
        qa.append(jnp.where(lo_half, qp, ones_hi))
        qa.append(jnp.where(lo_half, ones_lo, qp))
    keep = lax.broadcasted_iota(i32, (tq, tq), 0) <= lax.broadcasted_iota(i32, (tq, tq), 1)
    acc[...] = jnp.zeros_like(acc)

    def score(jb, buf):
        k0 = pl.multiple_of(jb * tq, tq)
        for h in range(n_heads):
            buf[h] = _dot_nt(kaug[h, pl.ds(k0, tq), :], qa[h])

    def absorb(jb, ms, masked, buf):
        new_ms = []
        for h in range(n_heads):
            s = buf[h]
            if masked:
                s = jnp.where(keep, s, NEG)
            m_new = jnp.maximum(ms[h], jnp.max(s, axis=0, keepdims=True))
            alpha = jnp.exp2(ms[h] - m_new)
            p = jnp.exp2(s - m_new).astype(bf16)
            new_ms.append(m_new)
            acc[h] = acc[h] * alpha + _dot(vt[jb, h * V_ROWS:(h + 1) * V_ROWS, :], p)
        return tuple(new_ms)

    def pair(pp, ms):
        j0 = 2 * pp
        score(j0 + 1, sc1)
        ms = absorb(j0, ms, False, sc0)
        score(j0 + 2, sc0)
        return absorb(j0 + 1, ms, False, sc1)

    score(0, sc0)
    ms = lax.fori_loop(0, i // 2, pair, tuple(jnp.full((1, tq), NEG, f32) for _ in range(n_heads)))

    @pl.when(i % 2 == 0)
    def _even():
        absorb(i, ms, True, sc0)

    @pl.when(i % 2 == 1)
    def _odd():
        score(i, sc1)
        absorb(i, absorb(i - 1, ms, False, sc0), True, sc1)

    out_t = jnp.concatenate([acc[h, 0:HEAD_DIM, :] * (1.0 / acc[h, HEAD_DIM:HEAD_DIM + 1, :]) for h in range(n_heads)],
                            axis=0)
    ms_ = jnp.mean(out_t * out_t, axis=0, keepdims=True)
    out_t = out_t * lax.rsqrt(ms_ + EPS)
    o_ref[0] = (out_t.T * ng_ref[...]).astype(bf16)


def _piece_select():
    sel = np.zeros((384, 4 * 128), np.float32)
    for pair in range(4):
        for j in range(3):
            sel[j * 128 + 8 + 2 * pair, pair * 128 + HEAD_DIM + j] = 1.0
            sel[j * 128 + 8 + 2 * pair + 1, pair * 128 + j] = 1.0
    return jnp.asarray(sel, bf16)


def _attn(qkv, cumc, norm_g):
    bsz, t, _ = qkv.shape
    tq = min(ATT_BLOCK, t)
    return pl.pallas_call(
        functools.partial(_attn_kernel, tq=tq, t=t),
        grid=(bsz, t // tq),
        in_specs=[pl.BlockSpec((1, tq, D_ATT), lambda b, i: (b, i, 0)),
                  pl.BlockSpec((1, t, D_ATT), lambda b, i: (b, 0, 1)),
                  pl.BlockSpec((1, t, D_ATT), lambda b, i: (b, 0, 2)),
                  pl.BlockSpec((1, t, 128), lambda b, i: (b, 0, 0)),
                  pl.BlockSpec((384, 512), lambda b, i: (0, 0)),
                  pl.BlockSpec((1, D_ATT), lambda b, i: (0, 0))],
        out_specs=pl.BlockSpec((1, tq, D_ATT), lambda b, i: (b, i, 0)),
        out_shape=jax.ShapeDtypeStruct((bsz, t, D_ATT), bf16),
        scratch_shapes=[pltpu.VMEM((D_ATT // HEAD_DIM, t, 128), bf16),
                        pltpu.VMEM((t // tq, (D_ATT // HEAD_DIM) * V_ROWS, tq), bf16),
                        pltpu.VMEM((D_ATT // HEAD_DIM, V_ROWS, tq), f32),
                        pltpu.VMEM((D_ATT // HEAD_DIM, tq, tq), f32),
                        pltpu.VMEM((D_ATT // HEAD_DIM, tq, tq), f32)],
        compiler_params=pltpu.CompilerParams(dimension_semantics=("parallel", "arbitrary"),
                                             vmem_limit_bytes=VMEM_LIMIT),
        name="attn",
    )(qkv, qkv, qkv, cumc, _piece_select(), norm_g)


def _layer_norm(y, g, b):
    mu = jnp.mean(y, axis=-1, keepdims=True)
    yc = y - mu
    var = jnp.mean(yc * yc, axis=-1, keepdims=True)
    return yc * lax.rsqrt(var + EPS) * g + b


def _outproj_kernel(ys_ref, ya_ref, x_ref, gt_ref, sc_ref, sh_ref, wo_ref, lng_ref, lnb_ref, wrh_ref, wrl_ref, rb_ref,
                    x1_ref, rt_ref, dest_ref, cnt_ref, tbl_ref, xrows_hbm,
                    carry, galloc, tbl, u2t, didx_v, didx_s, cnt_v, cnt_s, zeros, idx_sem, sc_sem, z_sem,
                    *, tm, blk, n_blk):
    i = pl.program_id(0)
    last = pl.num_programs(0) - 1
    cur = i % 2
    prev = 1 - cur

    def idx_ready(slot):
        return pltpu.make_async_copy(didx_v.at[slot], didx_s.at[slot], idx_sem.at[slot])

    def dispatched(slot):
        return pltpu.make_async_copy(u2t.at[slot], u2t.at[slot], sc_sem.at[slot])

    def dispatch_copy(slot, r, k):
        return pltpu.make_async_copy(u2t.at[slot, pl.ds(r * PACK_TILE, PACK_TILE)],
                                     xrows_hbm.at[pl.ds(didx_s[slot, k, r] * PACK_TILE, PACK_TILE)], sc_sem.at[slot])

    @pl.when(i == 0)
    def _init():
        carry[...] = jnp.zeros_like(carry)
        galloc[...] = jnp.zeros_like(galloc)
        tbl[...] = jnp.zeros_like(tbl)

    @pl.when(i >= 2)
    def _reuse():
        dispatched(cur).wait()
        dispatched(cur).wait()

    def step(dispatch_prev):
        n_parts = OUTPROJ_PARTS
        n_slices = 3 * n_parts + 2

        def dispatch_slice(c):
            if dispatch_prev:
                for r_ in range(c * tm // n_slices, (c + 1) * tm // n_slices):
                    dispatch_copy(prev, r_, 0).start(priority=0)
                    dispatch_copy(prev, r_, 1).start(priority=1)

        if dispatch_prev:
            idx_ready(prev).wait()
        part = tm // n_parts
        hs = []
        for a in range(n_parts):
            rs = slice(a * part, (a + 1) * part)
            dispatch_slice(a)
            hs.append(_dot(ys_ref[rs, :], wo_ref[0:D_SSM, :]) + _dot(ya_ref[rs, :], wo_ref[D_SSM:D, :]))
        logit_parts = []
        for a in range(n_parts):
            rs = slice(a * part, (a + 1) * part)
            dispatch_slice(n_parts + 2 * a)
            x1 = _layer_norm(ALPHA * x_ref[rs, :] + (1.0 + gt_ref[0, 0]) * hs[a], lng_ref[...], lnb_ref[...])
            x1_ref[rs, :] = x1
            u2 = x1 * (1.0 + sc_ref[0, 0]) + sh_ref[0, 0]
            uh = u2.astype(bf16)
            _store_packed_rows(u2t.at[cur, pl.ds(a * part * PACK_TILE, part * PACK_TILE)], _pack_rows(uh))
            dispatch_slice(n_parts + 2 * a + 1)
            ul = (u2 - uh.astype(f32)).astype(bf16)
            logit_parts.append((_dot(uh, wrh_ref[...]) + _dot(ul, wrh_ref[...])) + _dot(uh, wrl_ref[...]))
        logits = jnp.concatenate(logit_parts, axis=0) + rb_ref[...]
        lane = lax.broadcasted_iota(i32, (tm, 128), 1).astype(f32)
        big = jnp.float32(1e9)

        def first_max(vals):
            m = jnp.max(vals, axis=-1, keepdims=True)
            return m, jnp.min(jnp.where(vals == m, lane, big), axis=-1, keepdims=True)

        gl = jnp.where(lane < N_GROUPS_R, logits, NEG)
        gmax, gidx = first_max(gl)
        g_p = 1.0 / jnp.sum(jnp.exp(gl - gmax), axis=-1, keepdims=True)
        lo = N_GROUPS_R + EXPERTS_PER_GROUP * gidx
        el = jnp.where((lane >= lo) & (lane < lo + EXPERTS_PER_GROUP), logits, NEG)
        m1, i1 = first_max(el)
        el2 = jnp.where(lane == i1, NEG, el)
        m2, i2 = first_max(el2)
        r = jnp.exp(m2 - m1)
        w1 = g_p / (1.0 + r)
        w2 = g_p * r / (1.0 + r)
        dispatch_slice(3 * n_parts)

        oh1 = lane == i1
        oh2 = lane == i2
        oh = jnp.where(oh1 | oh2, 1.0, 0.0)
        r_i = lax.broadcasted_iota(i32, (tm, tm), 0)
        c_i = lax.broadcasted_iota(i32, (tm, tm), 1)
        lower = jnp.where(r_i > c_i, 1.0, 0.0).astype(bf16)
        c_old = carry[...]
        prefix = _dot(lower, oh.astype(bf16)) + c_old
        rank1 = jnp.sum(jnp.where(oh1, prefix, 0.0), axis=-1, keepdims=True)
        rank2 = jnp.sum(jnp.where(oh2, prefix, 0.0), axis=-1, keepdims=True)
        c_new = c_old + jnp.sum(oh, axis=0, keepdims=True)
        carry[...] = c_new
        dispatch_slice(3 * n_parts + 1)

        nb_old = jnp.floor((c_old + (blk - 1)) * (1.0 / blk))
        nb_new = jnp.floor((c_new + (blk - 1)) * (1.0 / blk))
        fresh = nb_new - nb_old
        sq_r = lax.broadcasted_iota(i32, (128, 128), 0)
        sq_c = lax.broadcasted_iota(i32, (128, 128), 1)
        before = jnp.where(sq_r < sq_c, 1.0, 0.0).astype(bf16)
        base = galloc[...] + _dot(jnp.broadcast_to(fresh, (8, 128)).astype(bf16), before)[0:1, :]
        galloc[...] = galloc[...] + jnp.sum(fresh, axis=-1, keepdims=True)
        ordinal = sq_r.astype(f32)
        tbl[...] = jnp.where((ordinal >= nb_old) & (ordinal < nb_new), base + (ordinal - nb_old), tbl[...])
        tbl_ref[...] = tbl[...]
        cnt_ref[...] = jnp.concatenate([c_new, galloc[...], jnp.zeros((6, 128), f32)], axis=0)

        rt = jnp.where(lane == 4, w1, jnp.where(lane == 5, w2, 0.0))
        rt_ref[...] = rt

        ids_bf = tbl[...].astype(bf16)

        def row_of(onehot, rank):
            nth = jnp.floor(rank * (1.0 / blk))
            ids = _dot(jnp.where(lane == nth, 1.0, 0.0).astype(bf16), ids_bf)
            return jnp.sum(jnp.where(onehot, ids, 0.0), axis=-1, keepdims=True) * blk + (rank - nth * blk)

        d1 = row_of(oh1, rank1)
        d2 = row_of(oh2, rank2)
        dd = jnp.where(lane == 0, d1, jnp.where(lane == 1, d2, 0.0)).T[0:8, :].astype(i32)
        dest_ref[...] = dd
        didx_v[cur] = dd
        idx_ready(cur).start()

    pl.when(i == 0)(lambda: step(False))
    pl.when(i > 0)(lambda: step(True))

    @pl.when(i == last)
    def _epilogue():
        idx_ready(cur).wait()

        def issue(r_, c):
            dispatch_copy(cur, r_, 0).start()
            dispatch_copy(cur, r_, 1).start()
            return c
        lax.fori_loop(0, tm, issue, 0)

        cnt = carry[...]
        n_blocks = jnp.floor((cnt + (blk - 1)) * (1.0 / blk))
        ordinal = lax.broadcasted_iota(i32, (128, 128), 0).astype(f32)
        last_id = jnp.sum(jnp.where(ordinal == n_blocks - 1.0, tbl[...], 0.0), axis=0, keepdims=True)
        used = cnt - (n_blocks - 1.0) * blk
        cnt_v[...] = jnp.concatenate([last_id * blk + used, blk - used, galloc[...], jnp.zeros((5, 128), f32)],
                                     axis=0).astype(i32)
        to_smem = pltpu.make_async_copy(cnt_v, cnt_s, z_sem)
        to_smem.start()
        to_smem.wait()
        zeros[...] = jnp.zeros_like(zeros)
        sizes = [1 << b for b in reversed(range(blk.bit_length() - 1))]

        def for_each_fill(fn):
            def tail(e, c):
                pad = cnt_s[1, N_GROUPS_R + e]
                off = cnt_s[0, N_GROUPS_R + e]
                for sz in sizes:
                    @pl.when((pad & sz) != 0)
                    def _(off=off, sz=sz):
                        fn(pltpu.make_async_copy(zeros.at[pl.ds(0, sz * PACK_TILE)],
                                                 xrows_hbm.at[pl.ds(off * PACK_TILE, sz * PACK_TILE)], z_sem))
                    off = off + (pad & sz)
                return c
            lax.fori_loop(0, N_EXPERTS, tail, 0)

            def whole(b, c):
                fn(pltpu.make_async_copy(zeros, xrows_hbm.at[pl.ds(b * blk * PACK_TILE, blk * PACK_TILE)], z_sem))
                return c
            lax.fori_loop(cnt_s[2, 0], n_blk, whole, 0)

        for_each_fill(lambda copy: copy.start())
        for_each_fill(lambda copy: copy.wait())

        @pl.when(i > 0)
        def _prev_done():
            dispatched(prev).wait()
            dispatched(prev).wait()
        dispatched(cur).wait()
        dispatched(cur).wait()


def _outproj(y_ssm, y_att, x, mod4, w_out, ln_g, ln_b, wr_hi, wr_lo, rb, t, blk, n_blk):
    n = x.shape[0]
    tm = min(OUTPROJ_ROWS, t)
    nt = t // tm
    assert n // blk + 1 <= 128 and n_blk <= 256, "block-id table: 128 blocks per expert, ids exact in bf16"
    vec = lambda k: pl.BlockSpec((1, 1, 1, D), lambda i, k=k: (k, i // nt, 0, 0))
    full = lambda shape: pl.BlockSpec(shape, lambda i: (0,) * len(shape))
    rows = lambda w: pl.BlockSpec((tm, w), lambda i: (i, 0))
    return pl.pallas_call(
        functools.partial(_outproj_kernel, tm=tm, blk=blk, n_blk=n_blk),
        grid=(n // tm,),
        in_specs=[rows(D_SSM), rows(D_ATT), rows(D), vec(2), vec(4), vec(3),
                  full((D, D)), full((1, D)), full((1, D)), full((D, 128)), full((D, 128)), full((1, 128))],
        out_specs=[rows(D), rows(128), pl.BlockSpec((8, tm), lambda i: (0, i)), full((8, 128)), full((128, 128)),
                   pl.BlockSpec(memory_space=pl.ANY)],
        out_shape=[jax.ShapeDtypeStruct((n, D), f32), jax.ShapeDtypeStruct((n, 128), f32),
                   jax.ShapeDtypeStruct((8, n), i32), jax.ShapeDtypeStruct((8, 128), f32),
                   jax.ShapeDtypeStruct((128, 128), f32),
                   jax.ShapeDtypeStruct((n_blk * blk * PACK_TILE, 128), u32)],
        scratch_shapes=[pltpu.VMEM((1, 128), f32), pltpu.VMEM((1, 128), f32), pltpu.VMEM((128, 128), f32),
                        pltpu.VMEM((2, tm * PACK_TILE, 128), u32),
                        pltpu.VMEM((2, 8, tm), i32), pltpu.SMEM((2, 8, tm), i32),
                        pltpu.VMEM((8, 128), i32), pltpu.SMEM((8, 128), i32),
                        pltpu.VMEM((blk * PACK_TILE, 128), u32),
                        pltpu.SemaphoreType.DMA((2,)), pltpu.SemaphoreType.DMA((2,)), pltpu.SemaphoreType.DMA(())],
        compiler_params=pltpu.CompilerParams(dimension_semantics=("arbitrary",), vmem_limit_bytes=VMEM_LIMIT),
        name="outproj",
    )(y_ssm, y_att, x, mod4, mod4, mod4, w_out, ln_g, ln_b, wr_hi, wr_lo, rb)


def _tile_copy(src_hbm, row, buf, slot, sem):
    return pltpu.make_async_copy(src_hbm.at[row], buf.at[pl.ds(slot * ROW_TILE, ROW_TILE)], sem)


def _moe_kernel(be_ref, br_ref, nu_ref, nxt_ref, par_ref, nv_ref, x_ref, wg_hbm, wu_hbm, wd_hbm, y_ref,
                wfg, wfu, wfd, wgb, wub, wdb, wsem, *, blk):
    i = pl.program_id(0)
    used = i < nu_ref[0]

    def fetch(e, slot):
        return [pltpu.make_async_copy(src.at[e], dst.at[slot], wsem.at[slot])
                for src, dst in ((wg_hbm, wfg), (wu_hbm, wfu), (wd_hbm, wfd))]

    @pl.when(i == 0)
    def _first():
        for c in fetch(be_ref[0], par_ref[0]):
            c.start()

    @pl.when(jnp.logical_and(used, jnp.logical_or(i == 0, be_ref[i] != be_ref[jnp.maximum(i - 1, 0)])))
    def _switch():
        slot = par_ref[i]
        for c in fetch(be_ref[i], slot):
            c.wait()

        @pl.when(nxt_ref[i] >= 0)
        def _next():
            for c in fetch(nxt_ref[i], 1 - slot):
                c.start(priority=1)
        wgb[...] = wfg[slot].astype(bf16)
        wub[...] = wfu[slot].astype(bf16)
        wdb[...] = wfd[slot].astype(bf16)

    @pl.when(jnp.logical_not(used))
    def _spare():
        y_ref[...] = jnp.zeros_like(y_ref)

    def mlp(rows):
        xa, xb = _load_packed_rows(x_ref.at[pl.ds(0, rows * PACK_TILE)], rows)
        half = D // 2
        gate = _dot(xa, wgb[0:half, :]) + _dot(xb, wgb[half:D, :])
        up = _dot(xa, wub[0:half, :]) + _dot(xb, wub[half:D, :])
        hid = (_silu(gate) * up).astype(bf16)
        _store_row_tiles(y_ref.at[pl.ds(0, rows * ROW_TILE)], _dot(hid, wdb[...]))

    sparse = nv_ref[i] <= blk // 2

    @pl.when(jnp.logical_and(used, jnp.logical_not(sparse)))
    def _full():
        mlp(blk)

    @pl.when(jnp.logical_and(used, sparse))
    def _half():
        mlp(blk // 2)
        y_ref[pl.ds(blk // 2 * ROW_TILE, blk // 2 * ROW_TILE), :] = jnp.zeros((blk // 2 * ROW_TILE, 128), f32)


def _moe(block_e, block_row, n_used, next_e, parity, n_valid, x_rows, w_gate, w_up, w_down, blk):
    nblk = block_e.shape[0]
    rspec = pl.BlockSpec((blk * ROW_TILE, 128), lambda i, be, br, *_: (br[i], 0))
    xspec = pl.BlockSpec((blk * PACK_TILE, 128), lambda i, be, br, nu, *_: (br[jnp.minimum(i, nu[0] - 1)], 0))
    anyspec = pl.BlockSpec(memory_space=pl.ANY)
    grid_spec = pltpu.PrefetchScalarGridSpec(
        num_scalar_prefetch=6,
        grid=(nblk,),
        in_specs=[xspec, anyspec, anyspec, anyspec],
        out_specs=rspec,
        scratch_shapes=[pltpu.VMEM((2, D, D_EXPERT), f32), pltpu.VMEM((2, D, D_EXPERT), f32),
                        pltpu.VMEM((2, D_EXPERT, D), f32),
                        pltpu.VMEM((D, D_EXPERT), bf16), pltpu.VMEM((D, D_EXPERT), bf16),
                        pltpu.VMEM((D_EXPERT, D), bf16),
                        pltpu.SemaphoreType.DMA((2,))],
    )
    return pl.pallas_call(
        functools.partial(_moe_kernel, blk=blk),
        grid_spec=grid_spec,
        out_shape=jax.ShapeDtypeStruct((nblk * blk * ROW_TILE, 128), f32),
        compiler_params=pltpu.CompilerParams(dimension_semantics=("arbitrary",), vmem_limit_bytes=VMEM_LIMIT),
        name="moe",
    )(block_e, block_row, n_used, next_e, parity, n_valid, x_rows, w_gate, w_up, w_down)


def _combine_kernel(d1_ref, d2_ref, y_hbm, x1_ref, rt_ref, gt_ref, lng_ref, lnb_ref, o_ref, buf, sem, *, tm):
    i = pl.program_id(0)
    last = pl.num_programs(0) - 1

    @pl.when(i == 0)
    def _prologue():
        for s in range(2):
            def issue(r, c, s=s):
                base = jnp.minimum(s, last) * tm
                _tile_copy(y_hbm, d1_ref[base + r], buf.at[s, 0], r, sem.at[s]).start()
                _tile_copy(y_hbm, d2_ref[base + r], buf.at[s, 1], r, sem.at[s]).start()
                return c
            lax.fori_loop(0, tm, issue, 0)

    def rows_ready(slot):
        return pltpu.make_async_copy(buf.at[slot], buf.at[slot], sem.at[slot])

    slot = i % 3
    nxt = (i + 2) % 3
    base = jnp.minimum(i + 2, last) * tm
    rows_ready(slot).wait()
    for r in range(tm):
        _tile_copy(y_hbm, d1_ref[base + r], buf.at[nxt, 0], r, sem.at[nxt]).start(priority=0)
        _tile_copy(y_hbm, d2_ref[base + r], buf.at[nxt, 1], r, sem.at[nxt]).start(priority=1)
    rt = rt_ref[...]
    moe = rt[:, 4:5] * _load_row_tiles(buf.at[slot, 0], tm) + rt[:, 5:6] * _load_row_tiles(buf.at[slot, 1], tm)
    y = ALPHA * x1_ref[...] + (1.0 + gt_ref[0, 0]) * moe
    o_ref[...] = _layer_norm(y, lng_ref[...], lnb_ref[...])

    @pl.when(i == last)
    def _drain():
        rows_ready((i + 1) % 3).wait()
        rows_ready((i + 2) % 3).wait()


def _combine(dest1, dest2, y_rows, x1, rt, mod4, ln_g, ln_b, t):
    n = x1.shape[0]
    tm = min(COMBINE_ROWS, t)
    nt = t // tm
    full = lambda shape: pl.BlockSpec(shape, lambda i, d1, d2: (0,) * len(shape))
    rows = lambda w: pl.BlockSpec((tm, w), lambda i, d1, d2: (i, 0))
    grid_spec = pltpu.PrefetchScalarGridSpec(
        num_scalar_prefetch=2,
        grid=(n // tm,),
        in_specs=[pl.BlockSpec(memory_space=pl.ANY), rows(D), rows(128),
                  pl.BlockSpec((1, 1, 1, D), lambda i, d1, d2: (5, i // nt, 0, 0)),
                  full((1, D)), full((1, D))],
        out_specs=rows(D),
        scratch_shapes=[pltpu.VMEM((3, 2, tm * ROW_TILE, 128), f32), pltpu.SemaphoreType.DMA((3,))],
    )
    return pl.pallas_call(
        functools.partial(_combine_kernel, tm=tm),
        grid_spec=grid_spec,
        out_shape=jax.ShapeDtypeStruct((n, D), f32),
        compiler_params=pltpu.CompilerParams(dimension_semantics=("arbitrary",), vmem_limit_bytes=VMEM_LIMIT),
        name="combine",
    )(dest1, dest2, y_rows, x1, rt, mod4, ln_g, ln_b)


def _hi_lo(w):
    hi = w.astype(bf16)
    return hi, (w - hi.astype(f32)).astype(bf16)


def _pad_lanes(v, offset, width):
    return jnp.zeros((width,), f32).at[offset:offset + v.shape[0]].set(v)


def _layer(x, c, ada_w, ada_b, w_in, conv_w, conv_b, dt_bias, a_log, d_skip, ssm_norm_g, fg_bias, att_norm_g,
           w_out, ln1_g, ln1_b, router_g_w, router_g_b, router_e_w, router_e_b, w_gate, w_up, w_down, ln2_g, ln2_b):
    bsz, t, _ = x.shape
    n = bsz * t

    mod4 = _ada(c, ada_w, ada_b).reshape(6, bsz, 1, D)

    w_main, w_small = _pack_w_in(w_in)
    zx, bc, qkv, dtf = _inproj(x, mod4, w_main, w_small, conv_w, conv_b.reshape(1, -1))

    pc = jnp.stack([_pad_lanes(dt_bias, 0, 128), _pad_lanes(a_log, 0, 128), _pad_lanes(fg_bias, 8, 128)]
                   + [jnp.zeros((128,), f32)] * 5)
    rep = lambda v: jnp.repeat(v, HEAD_DIM)
    pe = jnp.stack([rep(dt_bias), rep(a_log), rep(d_skip), ssm_norm_g] + [jnp.zeros((D_SSM,), f32)] * 4)
    y_ssm, cumc = _ssd(zx, bc, dtf, pc, pe)

    y_att = _attn(qkv, cumc, att_norm_g.reshape(1, -1))

    wr = jnp.concatenate([router_g_w, router_e_w, jnp.zeros((D, 128 - N_GROUPS_R - N_EXPERTS), f32)], axis=1)
    wr_hi, wr_lo = _hi_lo(wr)
    rb = jnp.concatenate([router_g_b, router_e_b, jnp.zeros((128 - N_GROUPS_R - N_EXPERTS,), f32)]).reshape(1, 128)
    blk = MOE_ROWS
    nblk = (2 * n) // blk + N_EXPERTS
    x1, rt, dest, cnt, tbl, x_rows = _outproj(y_ssm.reshape(n, D_SSM), y_att.reshape(n, D_ATT), x.reshape(n, D), mod4,
                                              w_out.astype(bf16), ln1_g.reshape(1, D), ln1_b.reshape(1, D),
                                              wr_hi, wr_lo, rb, t, blk, nblk)

    counts = cnt[0, N_GROUPS_R:N_GROUPS_R + N_EXPERTS].astype(i32)
    nb = (counts + blk - 1) // blk
    nb_end = jnp.cumsum(nb)
    n_used = nb_end[N_EXPERTS - 1:]
    every = jnp.arange(nblk, dtype=i32)
    step = jnp.minimum(every, n_used[0] - 1)
    owner = (nb_end[None, :] <= step[:, None]).astype(i32)
    block_e = jnp.sum(owner, axis=1)
    first = jnp.sum(owner * nb[None, :], axis=1)
    lanes = jnp.arange(128, dtype=i32)[None, :]
    nth_row = jnp.dot((lanes == (step - first)[:, None]).astype(f32), tbl, precision=lax.Precision.HIGHEST)
    block_id = jnp.sum(jnp.where(lanes == (block_e + N_GROUPS_R)[:, None], nth_row, 0.0), axis=1).astype(i32)
    block_row = jnp.where(every < n_used[0], block_id, every)
    run_end = first + jnp.sum((jnp.arange(N_EXPERTS)[None, :] == block_e[:, None]) * nb[None, :], axis=1)
    next_e = jnp.where(run_end < n_used[0], jnp.sum((nb_end[None, :] <= run_end[:, None]).astype(i32), axis=1), -1)
    parity = jnp.sum(((jnp.arange(N_EXPERTS)[None, :] < block_e[:, None]) & (nb[None, :] > 0)).astype(i32), axis=1) % 2

    own = (jnp.arange(N_EXPERTS)[None, :] == block_e[:, None]).astype(i32)
    n_valid = jnp.clip(jnp.sum(own * counts[None, :], axis=1) - (step - first) * blk, 0, blk)
    y_rows = _moe(block_e, block_row, n_used, next_e, parity, n_valid, x_rows,
                  w_gate, w_up, w_down, blk)
    out = _combine(dest[0], dest[1], y_rows.reshape(nblk * blk, ROW_TILE, 128), x1, rt, mod4,
                   ln2_g.reshape(1, D), ln2_b.reshape(1, D), t)
    return out.reshape(bsz, t, D)


def kernel(x, c, ada_w, ada_b, w_in, conv_w, conv_b, dt_bias, a_log, d_skip, ssm_norm_g, fg_bias, att_norm_g, w_out,
           ln1_g, ln1_b, router_g_w, router_g_b, router_e_w, router_e_b, w_gate, w_up, w_down, ln2_g, ln2_b):
    depth = ada_w.shape[0]
    for l in range(depth):
        x = _layer(x, c, ada_w[l], ada_b[l], w_in[l], conv_w[l], conv_b[l], dt_bias[l], a_log[l], d_skip[l],
                   ssm_norm_g[l], fg_bias[l], att_norm_g[l], w_out[l], ln1_g[l], ln1_b[l], router_g_w[l],
                   router_g_b[l], router_e_w[l], router_e_b[l], w_gate[l], w_up[l], w_down[l], ln2_g[l], ln2_b[l])
    return x
```

```python
import functools

import jax
import jax.numpy as jnp
import numpy as np
from jax import lax
from jax.experimental import pallas as pl
from jax.experimental.pallas import tpu as pltpu

f32 = jnp.float32
bf16 = jnp.bfloat16
i32 = jnp.int32

D = 1024
D_SSM = 512
D_ATT = 512
HEAD_DIM = 64
GROUP_W = 256
N_STATE = 128
CONV_K = 4
N_GROUPS_R = 4
EXPERTS_PER_GROUP = 8
N_EXPERTS = 32
D_EXPERT = 512
ALPHA = 2.0 ** 0.25
EPS = 1e-5
NEG = -1e30
LOG2E = 1.4426950408889634
QK_SCALE = HEAD_DIM ** -0.5 * LOG2E
V_ROWS = 80

SSD_CHUNK = 256
ATT_BLOCK = 256
INPROJ_ROWS = 512
OUTPROJ_ROWS = 256
OUTPROJ_PARTS = 2
MOE_ROWS = 512
COMBINE_ROWS = 256
VMEM_LIMIT = 48 * 1024 * 1024


def _dot(a, b):
    return jnp.dot(a, b, preferred_element_type=f32)


def _dot_nt(a, b):
    return lax.dot_general(a, b, (((1,), (1,)), ((), ())), preferred_element_type=f32)


def _dot_tn(a, b):
    return lax.dot_general(a, b, (((0,), (0,)), ((), ())), preferred_element_type=f32)


def _split3(v):
    hi = v.astype(bf16)
    r1 = v - hi.astype(f32)
    mid = r1.astype(bf16)
    lo = (r1 - mid.astype(f32)).astype(bf16)
    return hi, mid, lo


def _dot_exact_lhs(m, v):
    hi, mid, lo = _split3(v)
    return (_dot(m, hi) + _dot(m, mid)) + _dot(m, lo)


def _dot_exact_rhs(v, m):
    hi, mid, lo = _split3(v)
    return (_dot(hi, m) + _dot(mid, m)) + _dot(lo, m)


ROW_TILE = 8


def _store_row_tiles(ref, val):
    rows = val.shape[0]
    for c in range(ROW_TILE):
        ref[pl.ds(c, rows, stride=ROW_TILE), :] = val[:, c * 128:(c + 1) * 128]


def _load_row_tiles(ref, rows):
    return jnp.concatenate([ref[pl.ds(c, rows, stride=ROW_TILE), :] for c in range(ROW_TILE)], axis=1)


PACK_TILE = 4
u32 = jnp.uint32


def _pack_rows(vb):
    lo = pltpu.bitcast(vb[:, 0:D // 2].astype(f32), u32) >> 16
    hi = pltpu.bitcast(vb[:, D // 2:D].astype(f32), u32) & jnp.uint32(0xFFFF0000)
    return lo | hi


def _store_packed_rows(ref, packed):
    rows = packed.shape[0]
    for c in range(PACK_TILE):
        ref[pl.ds(c, rows, stride=PACK_TILE), :] = packed[:, c * 128:(c + 1) * 128]


def _load_packed_rows(ref, rows):
    words = [ref[pl.ds(c, rows, stride=PACK_TILE), :] for c in range(PACK_TILE)]
    lo = jnp.concatenate([pltpu.bitcast(w << 16, f32) for w in words], axis=1)
    hi = jnp.concatenate([pltpu.bitcast(w & jnp.uint32(0xFFFF0000), f32) for w in words], axis=1)
    return lo.astype(bf16), hi.astype(bf16)


def _softplus(x):
    return jnp.maximum(x, 0.0) + jnp.log1p(jnp.exp(-jnp.abs(x)))


def _silu(x):
    return x * jax.nn.sigmoid(x)


def _ada_kernel(c_ref, w_ref, b_ref, o_ref):
    s = _silu(c_ref[...]).astype(bf16)
    o_ref[0] = _dot(s, w_ref[...].astype(bf16)) + b_ref[0]


def _ada(c, w, b):
    bsz = c.shape[0]
    return pl.pallas_call(
        _ada_kernel,
        grid=(6,),
        in_specs=[pl.BlockSpec((bsz, D), lambda j: (0, 0)),
                  pl.BlockSpec((D, D), lambda j: (0, j)),
                  pl.BlockSpec((1, 1, D), lambda j: (j, 0, 0))],
        out_specs=pl.BlockSpec((1, bsz, D), lambda j: (j, 0, 0)),
        out_shape=jax.ShapeDtypeStruct((6, bsz, D), f32),
        compiler_params=pltpu.CompilerParams(dimension_semantics=("arbitrary",), vmem_limit_bytes=VMEM_LIMIT),
        name="ada",
    )(c, w, b.reshape(6, 1, D))


W_IN_COLS = 3088
PACK_ROWS = 256


def _pack_w_in_kernel(wt_ref, wm_ref, ws_ref):
    for j in range(3072 // PACK_ROWS):
        src = j * PACK_ROWS if j * PACK_ROWS < 1536 else j * PACK_ROWS + 8
        wm_ref[:, j * PACK_ROWS:(j + 1) * PACK_ROWS] = wt_ref[src:src + PACK_ROWS, :].T.astype(bf16)
    small = jnp.concatenate([wt_ref[1536:1544, :], wt_ref[W_IN_COLS - 8:W_IN_COLS, :], jnp.zeros((112, D), f32)],
                            axis=0).T
    hi = small.astype(bf16)
    ws_ref[:, 0:128] = hi
    ws_ref[:, 128:256] = (small - hi.astype(f32)).astype(bf16)


def _pack_w_in(w_in):
    full = lambda shape: pl.BlockSpec(shape, lambda i: (0,) * len(shape))
    return pl.pallas_call(
        _pack_w_in_kernel,
        grid=(1,),
        in_specs=[full((W_IN_COLS, D))],
        out_specs=[full((D, 3072)), full((D, 256))],
        out_shape=[jax.ShapeDtypeStruct((D, 3072), bf16), jax.ShapeDtypeStruct((D, 256), bf16)],
        compiler_params=pltpu.CompilerParams(dimension_semantics=("arbitrary",), vmem_limit_bytes=VMEM_LIMIT),
        name="pack_w_in",
    )(w_in.T)


def _inproj_kernel(x_ref, sc_ref, sh_ref, wm_ref, ws_ref, cw_ref, cb_ref, zx_ref, bc_ref, qkv_ref, dtf_ref, xcat, *, tm):
    i = pl.program_id(1)

    @pl.when(i == 0)
    def _init():
        xcat[0:8, :] = jnp.zeros((8, 2 * D_SSM), f32)

    u = x_ref[0] * (1.0 + sc_ref[0, 0]) + sh_ref[0, 0]
    ub = u.astype(bf16)
    xcat[8:8 + tm, 0:512] = _dot(ub, wm_ref[:, 512:1024])
    xcat[8:8 + tm, 512:1024] = _dot(ub, wm_ref[:, 1024:1536])
    zx_ref[0, :, 0:512] = _dot(ub, wm_ref[:, 0:512])
    acc = cw_ref[0:1, :] * xcat[5:5 + tm, :] + cb_ref[...]
    for k in range(1, CONV_K):
        acc = acc + cw_ref[k:k + 1, :] * xcat[5 + k:5 + k + tm, :]
    xcat[0:8, :] = xcat[tm:tm + 8, :]
    xbc = _silu(acc)
    zx_ref[0, :, 512:1024] = xbc[:, 0:512]
    bc_ref[0] = xbc[:, 512:1024].astype(bf16)
    for j, scale in enumerate((QK_SCALE, 1.0, 1.0)):
        qkv_ref[0, :, j * 512:(j + 1) * 512] = (
            _dot(ub, wm_ref[:, 1536 + j * 512:1536 + (j + 1) * 512]) * scale).astype(bf16)
    ul = (u - ub.astype(f32)).astype(bf16)
    d_hl = _dot(ub, ws_ref[...])
    dtf_ref[0] = (d_hl[:, 0:128] + _dot(ul, ws_ref[:, 0:128])) + d_hl[:, 128:256]


def _inproj(x, mod4, w_main, ws, conv_w, conv_b):
    bsz, t, _ = x.shape
    tm = min(INPROJ_ROWS, t)
    vec = lambda k: pl.BlockSpec((1, 1, 1, D), lambda b, i, k=k: (k, b, 0, 0))
    full = lambda shape: pl.BlockSpec(shape, lambda b, i: (0,) * len(shape))
    rows = lambda w: pl.BlockSpec((1, tm, w), lambda b, i: (b, i, 0))
    return pl.pallas_call(
        functools.partial(_inproj_kernel, tm=tm),
        grid=(bsz, t // tm),
        in_specs=[rows(D), vec(1), vec(0), full((D, 3072)), full((D, 256)),
                  full((CONV_K, 2 * D_SSM)), full((1, 2 * D_SSM))],
        out_specs=[rows(1024), rows(512), rows(1536), rows(128)],
        out_shape=[jax.ShapeDtypeStruct((bsz, t, 1024), f32),
                   jax.ShapeDtypeStruct((bsz, t, 512), bf16),
                   jax.ShapeDtypeStruct((bsz, t, 1536), bf16),
                   jax.ShapeDtypeStruct((bsz, t, 128), f32)],
        scratch_shapes=[pltpu.VMEM((tm + 8, 2 * D_SSM), f32)],
        compiler_params=pltpu.CompilerParams(dimension_semantics=("parallel", "arbitrary"),
                                             vmem_limit_bytes=VMEM_LIMIT),
        name="inproj",
    )(x, mod4, mod4, w_main, ws, conv_w, conv_b)


def _ssd_kernel(z_ref, xs_ref, bc_ref, dtf_ref, pc_ref, pe_ref, y_ref, cumc_ref, state, carry, *, lc):
    j = pl.program_id(1)

    @pl.when(j == 0)
    def _init():
        state[...] = jnp.zeros_like(state)
        carry[...] = jnp.zeros_like(carry)

    xs = xs_ref[0]
    bm = bc_ref[0, :, 0:256]
    cm = bc_ref[0, :, 256:512]

    dtf = dtf_ref[0]
    lane = lax.broadcasted_iota(i32, (lc, 128), 1)
    dt_c = _softplus(dtf + pc_ref[0:1, :])
    a_c = dt_c * (-jnp.exp(pc_ref[1:2, :]))
    logf = -_softplus(-(dtf + pc_ref[2:3, :]))
    v = jnp.where(lane < 8, a_c, logf) * LOG2E
    r_i = lax.broadcasted_iota(i32, (lc, lc), 0)
    c_i = lax.broadcasted_iota(i32, (lc, lc), 1)
    tri = r_i >= c_i
    tri_b = jnp.where(tri, 1.0, 0.0).astype(bf16)
    cum = _dot_exact_lhs(tri_b, v) + carry[...]
    carry[...] = jnp.where(lane[0:1, :] >= 8, cum[lc - 1:lc, :], 0.0)
    cumc_ref[0] = cum
    cs_t = cum.T[0:8, :]

    e_r = lax.broadcasted_iota(i32, (128, D_SSM), 0)
    e_c = lax.broadcasted_iota(i32, (128, D_SSM), 1)
    expand = jnp.where(jnp.right_shift(e_c, 6) == e_r, 1.0, 0.0).astype(bf16)
    dt_e = _dot_exact_rhs(dt_c, expand)
    cs_e = _dot_exact_rhs(cum, expand)

    xdt = xs * dt_e
    ecs = jnp.exp2(cs_e)
    cs_last = cs_e[lc - 1:lc, :]
    dec_st = jnp.exp2(cs_last - cs_e)
    lane_g = lax.broadcasted_iota(i32, (1, GROUP_W), 1)
    ys = []
    for g in range(2):
        gs = slice(g * GROUP_W, (g + 1) * GROUP_W)
        bg = bm[:, g * N_STATE:(g + 1) * N_STATE]
        cg = cm[:, g * N_STATE:(g + 1) * N_STATE]
        cb = _dot_nt(cg, bg)
        xdt_g = xdt[:, gs]
        xdt_gb = xdt_g.astype(bf16)
        ms, xb = [], []
        for hh in range(4):
            h = g * 4 + hh
            lm = jnp.exp2(jnp.where(tri, cum[:, h:h + 1] - cs_t[h:h + 1, :], -jnp.inf))
            ms.append((cb * lm).astype(bf16))
            xb.append(jnp.where(jnp.right_shift(lane_g, 6) == hh, xdt_gb, jnp.zeros_like(xdt_gb)))
        y_diag = _dot(jnp.concatenate(ms, axis=1), jnp.concatenate(xb, axis=0))
        st = state[g]
        y_off = _dot(cg, st.astype(bf16)) * ecs[:, gs]
        upd = _dot_tn(bg, (xdt_g * dec_st[:, gs]).astype(bf16))
        state[g] = st * jnp.exp2(cs_last[:, gs]) + upd
        ys.append(y_diag + y_off + xs[:, gs] * pe_ref[2:3, gs])

    outs = []
    for g in range(2):
        gs = slice(g * GROUP_W, (g + 1) * GROUP_W)
        yg = ys[g] * _silu(z_ref[0, :, gs])
        ms_ = jnp.mean(yg * yg, axis=-1, keepdims=True)
        outs.append(yg * lax.rsqrt(ms_ + EPS))
    y_ref[0] = (jnp.concatenate(outs, axis=1) * pe_ref[3:4, :]).astype(bf16)


def _ssd(zx, bc, dtf, pc, pe):
    bsz, t, _ = zx.shape
    lc = min(SSD_CHUNK, t)
    col = lambda k: pl.BlockSpec((1, lc, 512), lambda b, j, k=k: (b, j, k))
    full = lambda shape: pl.BlockSpec(shape, lambda b, j: (0,) * len(shape))
    return pl.pallas_call(
        functools.partial(_ssd_kernel, lc=lc),
        grid=(bsz, t // lc),
        in_specs=[col(0), col(1), col(0),
                  pl.BlockSpec((1, lc, 128), lambda b, j: (b, j, 0)),
                  full((8, 128)), full((8, D_SSM))],
        out_specs=[pl.BlockSpec((1, lc, D_SSM), lambda b, j: (b, j, 0)),
                   pl.BlockSpec((1, lc, 128), lambda b, j: (b, j, 0))],
        out_shape=[jax.ShapeDtypeStruct((bsz, t, D_SSM), bf16),
                   jax.ShapeDtypeStruct((bsz, t, 128), f32)],
        scratch_shapes=[pltpu.VMEM((2, N_STATE, GROUP_W), f32),
                        pltpu.VMEM((1, 128), f32)],
        compiler_params=pltpu.CompilerParams(dimension_semantics=("parallel", "arbitrary"),
                                             vmem_limit_bytes=VMEM_LIMIT),
        name="ssd",
    )(zx, zx, bc, dtf, pc, pe)


def _attn_kernel(q_ref, k_ref, v_ref, cc_ref, psel_ref, ng_ref, o_ref, kaug, vt, acc, sc0, sc1, *, tq, t):
    i = pl.program_id(1)
    nkb = t // tq
    n_heads = D_ATT // HEAD_DIM
    lane = lax.broadcasted_iota(i32, (1, 128), 1)
    lo_half = lane < HEAD_DIM

    @pl.when(i == 0)
    def _build():
        ones_rows = jnp.where(lax.broadcasted_iota(i32, (V_ROWS - HEAD_DIM, tq), 0) == 0, 1.0, 0.0).astype(bf16)
        for jb in range(nkb):
            rows = slice(jb * tq, (jb + 1) * tq)
            v_t = v_ref[0, rows, :].astype(f32).T.astype(bf16)
            for h in range(n_heads):
                vt[jb, h * V_ROWS:h * V_ROWS + HEAD_DIM, :] = v_t[h * HEAD_DIM:(h + 1) * HEAD_DIM, :]
                vt[jb, h * V_ROWS + HEAD_DIM:(h + 1) * V_ROWS, :] = ones_rows
            pieces = jnp.concatenate(_split3(-cc_ref[0, rows, :]), axis=1)
            spare = _dot(pieces, psel_ref[...]).astype(bf16)
            for p in range(n_heads // 2):
                a = spare[:, p * 128:(p + 1) * 128]
                kp = k_ref[0, rows, p * 128:(p + 1) * 128]
                kaug[2 * p, rows, :] = jnp.where(lo_half, kp, a)
                kaug[2 * p + 1, rows, :] = jnp.where(lo_half, a, kp)

    ones_hi = jnp.where((lane >= HEAD_DIM) & (lane < HEAD_DIM + 3), 1.0, 0.0).astype(bf16)
    ones_lo = jnp.where(lane < 3, 1.0, 0.0).astype(bf16)
    qa = []
    for p in range(n_heads // 2):
        qp = q_ref[0, :, p * 128:(p + 1) * 128]
        qa.append(jnp.where(lo_half, qp, ones_hi))
        qa.append(jnp.where(lo_half, ones_lo, qp))
    keep = lax.broadcasted_iota(i32, (tq, tq), 0) <= lax.broadcasted_iota(i32, (tq, tq), 1)
    acc[...] = jnp.zeros_like(acc)

    def score(jb, buf):
        k0 = pl.multiple_of(jb * tq, tq)
        for h in range(n_heads):
            buf[h] = _dot_nt(kaug[h, pl.ds(k0, tq), :], qa[h])

    def absorb(jb, ms, masked, buf):
        new_ms = []
        for h in range(n_heads):
            s = buf[h]
            if masked:
                s = jnp.where(keep, s, NEG)
            m_new = jnp.maximum(ms[h], jnp.max(s, axis=0, keepdims=True))
            alpha = jnp.exp2(ms[h] - m_new)
            p = jnp.exp2(s - m_new).astype(bf16)
            new_ms.append(m_new)
            acc[h] = acc[h] * alpha + _dot(vt[jb, h * V_ROWS:(h + 1) * V_ROWS, :], p)
        return tuple(new_ms)

    def pair(pp, ms):
        j0 = 2 * pp
        score(j0 + 1, sc1)
        ms = absorb(j0, ms, False, sc0)
        score(j0 + 2, sc0)
        return absorb(j0 + 1, ms, False, sc1)

    score(0, sc0)
    ms = lax.fori_loop(0, i // 2, pair, tuple(jnp.full((1, tq), NEG, f32) for _ in range(n_heads)))

    @pl.when(i % 2 == 0)
    def _even():
        absorb(i, ms, True, sc0)

    @pl.when(i % 2 == 1)
    def _odd():
        score(i, sc1)
        absorb(i, absorb(i - 1, ms, False, sc0), True, sc1)

    out_t = jnp.concatenate([acc[h, 0:HEAD_DIM, :] * (1.0 / acc[h, HEAD_DIM:HEAD_DIM + 1, :]) for h in range(n_heads)],
                            axis=0)
    ms_ = jnp.mean(out_t * out_t, axis=0, keepdims=True)
    out_t = out_t * lax.rsqrt(ms_ + EPS)
    o_ref[0] = (out_t.T * ng_ref[...]).astype(bf16)


def _piece_select():
    sel = np.zeros((384, 4 * 128), np.float32)
    for pair in range(4):
        for j in range(3):
            sel[j * 128 + 8 + 2 * pair, pair * 128 + HEAD_DIM + j] = 1.0
            sel[j * 128 + 8 + 2 * pair + 1, pair * 128 + j] = 1.0
    return jnp.asarray(sel, bf16)


def _attn(qkv, cumc, norm_g):
    bsz, t, _ = qkv.shape
    tq = min(ATT_BLOCK, t)
    return pl.pallas_call(
        functools.partial(_attn_kernel, tq=tq, t=t),
        grid=(bsz, t // tq),
        in_specs=[pl.BlockSpec((1, tq, D_ATT), lambda b, i: (b, i, 0)),
                  pl.BlockSpec((1, t, D_ATT), lambda b, i: (b, 0, 1)),
                  pl.BlockSpec((1, t, D_ATT), lambda b, i: (b, 0, 2)),
                  pl.BlockSpec((1, t, 128), lambda b, i: (b, 0, 0)),
                  pl.BlockSpec((384, 512), lambda b, i: (0, 0)),
                  pl.BlockSpec((1, D_ATT), lambda b, i: (0, 0))],
        out_specs=pl.BlockSpec((1, tq, D_ATT), lambda b, i: (b, i, 0)),
        out_shape=jax.ShapeDtypeStruct((bsz, t, D_ATT), bf16),
        scratch_shapes=[pltpu.VMEM((D_ATT // HEAD_DIM, t, 128), bf16),
                        pltpu.VMEM((t // tq, (D_ATT // HEAD_DIM) * V_ROWS, tq), bf16),
                        pltpu.VMEM((D_ATT // HEAD_DIM, V_ROWS, tq), f32),
                        pltpu.VMEM((D_ATT // HEAD_DIM, tq, tq), f32),
                        pltpu.VMEM((D_ATT // HEAD_DIM, tq, tq), f32)],
        compiler_params=pltpu.CompilerParams(dimension_semantics=("parallel", "arbitrary"),
                                             vmem_limit_bytes=VMEM_LIMIT),
        name="attn",
    )(qkv, qkv, qkv, cumc, _piece_select(), norm_g)


def _layer_norm(y, g, b):
    mu = jnp.mean(y, axis=-1, keepdims=True)
    yc = y - mu
    var = jnp.mean(yc * yc, axis=-1, keepdims=True)
    return yc * lax.rsqrt(var + EPS) * g + b


def _outproj_kernel(ys_ref, ya_ref, x_ref, gt_ref, sc_ref, sh_ref, wo_ref, lng_ref, lnb_ref, wrh_ref, wrl_ref, rb_ref,
                    x1_ref, rt_ref, dest_ref, cnt_ref, tbl_ref, xrows_hbm,
                    carry, galloc, tbl, u2t, didx_v, didx_s, cnt_v, cnt_s, zeros, idx_sem, sc_sem, z_sem,
                    *, tm, blk, n_blk):
    i = pl.program_id(0)
    last = pl.num_programs(0) - 1
    cur = i % 2
    prev = 1 - cur

    def idx_ready(slot):
        return pltpu.make_async_copy(didx_v.at[slot], didx_s.at[slot], idx_sem.at[slot])

    def dispatched(slot):
        return pltpu.make_async_copy(u2t.at[slot], u2t.at[slot], sc_sem.at[slot])

    def dispatch_copy(slot, r, k):
        return pltpu.make_async_copy(u2t.at[slot, pl.ds(r * PACK_TILE, PACK_TILE)],
                                     xrows_hbm.at[pl.ds(didx_s[slot, k, r] * PACK_TILE, PACK_TILE)], sc_sem.at[slot])

    @pl.when(i == 0)
    def _init():
        carry[...] = jnp.zeros_like(carry)
        galloc[...] = jnp.zeros_like(galloc)
        tbl[...] = jnp.zeros_like(tbl)

    @pl.when(i >= 2)
    def _reuse():
        dispatched(cur).wait()
        dispatched(cur).wait()

    def step(dispatch_prev):
        n_parts = OUTPROJ_PARTS
        n_slices = 3 * n_parts + 2

        def dispatch_slice(c):
            if dispatch_prev:
                for r_ in range(c * tm // n_slices, (c + 1) * tm // n_slices):
                    dispatch_copy(prev, r_, 0).start(priority=0)
                    dispatch_copy(prev, r_, 1).start(priority=1)

        if dispatch_prev:
            idx_ready(prev).wait()
        part = tm // n_parts
        hs = []
        for a in range(n_parts):
            rs = slice(a * part, (a + 1) * part)
            dispatch_slice(a)
            hs.append(_dot(ys_ref[rs, :], wo_ref[0:D_SSM, :]) + _dot(ya_ref[rs, :], wo_ref[D_SSM:D, :]))
        logit_parts = []
        for a in range(n_parts):
            rs = slice(a * part, (a + 1) * part)
            dispatch_slice(n_parts + 2 * a)
            x1 = _layer_norm(ALPHA * x_ref[rs, :] + (1.0 + gt_ref[0, 0]) * hs[a], lng_ref[...], lnb_ref[...])
            x1_ref[rs, :] = x1
            u2 = x1 * (1.0 + sc_ref[0, 0]) + sh_ref[0, 0]
            uh = u2.astype(bf16)
            _store_packed_rows(u2t.at[cur, pl.ds(a * part * PACK_TILE, part * PACK_TILE)], _pack_rows(uh))
            dispatch_slice(n_parts + 2 * a + 1)
            ul = (u2 - uh.astype(f32)).astype(bf16)
            logit_parts.append((_dot(uh, wrh_ref[...]) + _dot(ul, wrh_ref[...])) + _dot(uh, wrl_ref[...]))
        logits = jnp.concatenate(logit_parts, axis=0) + rb_ref[...]
        lane = lax.broadcasted_iota(i32, (tm, 128), 1).astype(f32)
        big = jnp.float32(1e9)

        def first_max(vals):
            m = jnp.max(vals, axis=-1, keepdims=True)
            return m, jnp.min(jnp.where(vals == m, lane, big), axis=-1, keepdims=True)

        gl = jnp.where(lane < N_GROUPS_R, logits, NEG)
        gmax, gidx = first_max(gl)
        g_p = 1.0 / jnp.sum(jnp.exp(gl - gmax), axis=-1, keepdims=True)
        lo = N_GROUPS_R + EXPERTS_PER_GROUP * gidx
        el = jnp.where((lane >= lo) & (lane < lo + EXPERTS_PER_GROUP), logits, NEG)
        m1, i1 = first_max(el)
        el2 = jnp.where(lane == i1, NEG, el)
        m2, i2 = first_max(el2)
        r = jnp.exp(m2 - m1)
        w1 = g_p / (1.0 + r)
        w2 = g_p * r / (1.0 + r)
        dispatch_slice(3 * n_parts)

        oh1 = lane == i1
        oh2 = lane == i2
        oh = jnp.where(oh1 | oh2, 1.0, 0.0)
        r_i = lax.broadcasted_iota(i32, (tm, tm), 0)
        c_i = lax.broadcasted_iota(i32, (tm, tm), 1)
        lower = jnp.where(r_i > c_i, 1.0, 0.0).astype(bf16)
        c_old = carry[...]
        prefix = _dot(lower, oh.astype(bf16)) + c_old
        rank1 = jnp.sum(jnp.where(oh1, prefix, 0.0), axis=-1, keepdims=True)
        rank2 = jnp.sum(jnp.where(oh2, prefix, 0.0), axis=-1, keepdims=True)
        c_new = c_old + jnp.sum(oh, axis=0, keepdims=True)
        carry[...] = c_new
        dispatch_slice(3 * n_parts + 1)

        nb_old = jnp.floor((c_old + (blk - 1)) * (1.0 / blk))
        nb_new = jnp.floor((c_new + (blk - 1)) * (1.0 / blk))
        fresh = nb_new - nb_old
        sq_r = lax.broadcasted_iota(i32, (128, 128), 0)
        sq_c = lax.broadcasted_iota(i32, (128, 128), 1)
        before = jnp.where(sq_r < sq_c, 1.0, 0.0).astype(bf16)
        base = galloc[...] + _dot(jnp.broadcast_to(fresh, (8, 128)).astype(bf16), before)[0:1, :]
        galloc[...] = galloc[...] + jnp.sum(fresh, axis=-1, keepdims=True)
        ordinal = sq_r.astype(f32)
        tbl[...] = jnp.where((ordinal >= nb_old) & (ordinal < nb_new), base + (ordinal - nb_old), tbl[...])
        tbl_ref[...] = tbl[...]
        cnt_ref[...] = jnp.concatenate([c_new, galloc[...], jnp.zeros((6, 128), f32)], axis=0)

        rt = jnp.where(lane == 4, w1, jnp.where(lane == 5, w2, 0.0))
        rt_ref[...] = rt

        ids_bf = tbl[...].astype(bf16)

        def row_of(onehot, rank):
            nth = jnp.floor(rank * (1.0 / blk))
            ids = _dot(jnp.where(lane == nth, 1.0, 0.0).astype(bf16), ids_bf)
            return jnp.sum(jnp.where(onehot, ids, 0.0), axis=-1, keepdims=True) * blk + (rank - nth * blk)

        d1 = row_of(oh1, rank1)
        d2 = row_of(oh2, rank2)
        dd = jnp.where(lane == 0, d1, jnp.where(lane == 1, d2, 0.0)).T[0:8, :].astype(i32)
        dest_ref[...] = dd
        didx_v[cur] = dd
        idx_ready(cur).start()

    pl.when(i == 0)(lambda: step(False))
    pl.when(i > 0)(lambda: step(True))

    @pl.when(i == last)
    def _epilogue():
        idx_ready(cur).wait()

        def issue(r_, c):
            dispatch_copy(cur, r_, 0).start()
            dispatch_copy(cur, r_, 1).start()
            return c
        lax.fori_loop(0, tm, issue, 0)

        cnt = carry[...]
        n_blocks = jnp.floor((cnt + (blk - 1)) * (1.0 / blk))
        ordinal = lax.broadcasted_iota(i32, (128, 128), 0).astype(f32)
        last_id = jnp.sum(jnp.where(ordinal == n_blocks - 1.0, tbl[...], 0.0), axis=0, keepdims=True)
        used = cnt - (n_blocks - 1.0) * blk
        cnt_v[...] = jnp.concatenate([last_id * blk + used, blk - used, galloc[...], jnp.zeros((5, 128), f32)],
                                     axis=0).astype(i32)
        to_smem = pltpu.make_async_copy(cnt_v, cnt_s, z_sem)
        to_smem.start()
        to_smem.wait()
        zeros[...] = jnp.zeros_like(zeros)
        sizes = [1 << b for b in reversed(range(blk.bit_length() - 1))]

        def for_each_fill(fn):
            def tail(e, c):
                pad = cnt_s[1, N_GROUPS_R + e]
                off = cnt_s[0, N_GROUPS_R + e]
                for sz in sizes:
                    @pl.when((pad & sz) != 0)
                    def _(off=off, sz=sz):
                        fn(pltpu.make_async_copy(zeros.at[pl.ds(0, sz * PACK_TILE)],
                                                 xrows_hbm.at[pl.ds(off * PACK_TILE, sz * PACK_TILE)], z_sem))
                    off = off + (pad & sz)
                return c
            lax.fori_loop(0, N_EXPERTS, tail, 0)

            def whole(b, c):
                fn(pltpu.make_async_copy(zeros, xrows_hbm.at[pl.ds(b * blk * PACK_TILE, blk * PACK_TILE)], z_sem))
                return c
            lax.fori_loop(cnt_s[2, 0], n_blk, whole, 0)

        for_each_fill(lambda copy: copy.start())
        for_each_fill(lambda copy: copy.wait())

        @pl.when(i > 0)
        def _prev_done():
            dispatched(prev).wait()
            dispatched(prev).wait()
        dispatched(cur).wait()
        dispatched(cur).wait()


def _outproj(y_ssm, y_att, x, mod4, w_out, ln_g, ln_b, wr_hi, wr_lo, rb, t, blk, n_blk):
    n = x.shape[0]
    tm = min(OUTPROJ_ROWS, t)
    nt = t // tm
    assert n // blk + 1 <= 128 and n_blk <= 256, "block-id table: 128 blocks per expert, ids exact in bf16"
    vec = lambda k: pl.BlockSpec((1, 1, 1, D), lambda i, k=k: (k, i // nt, 0, 0))
    full = lambda shape: pl.BlockSpec(shape, lambda i: (0,) * len(shape))
    rows = lambda w: pl.BlockSpec((tm, w), lambda i: (i, 0))
    return pl.pallas_call(
        functools.partial(_outproj_kernel, tm=tm, blk=blk, n_blk=n_blk),
        grid=(n // tm,),
        in_specs=[rows(D_SSM), rows(D_ATT), rows(D), vec(2), vec(4), vec(3),
                  full((D, D)), full((1, D)), full((1, D)), full((D, 128)), full((D, 128)), full((1, 128))],
        out_specs=[rows(D), rows(128), pl.BlockSpec((8, tm), lambda i: (0, i)), full((8, 128)), full((128, 128)),
                   pl.BlockSpec(memory_space=pl.ANY)],
        out_shape=[jax.ShapeDtypeStruct((n, D), f32), jax.ShapeDtypeStruct((n, 128), f32),
                   jax.ShapeDtypeStruct((8, n), i32), jax.ShapeDtypeStruct((8, 128), f32),
                   jax.ShapeDtypeStruct((128, 128), f32),
                   jax.ShapeDtypeStruct((n_blk * blk * PACK_TILE, 128), u32)],
        scratch_shapes=[pltpu.VMEM((1, 128), f32), pltpu.VMEM((1, 128), f32), pltpu.VMEM((128, 128), f32),
                        pltpu.VMEM((2, tm * PACK_TILE, 128), u32),
                        pltpu.VMEM((2, 8, tm), i32), pltpu.SMEM((2, 8, tm), i32),
                        pltpu.VMEM((8, 128), i32), pltpu.SMEM((8, 128), i32),
                        pltpu.VMEM((blk * PACK_TILE, 128), u32),
                        pltpu.SemaphoreType.DMA((2,)), pltpu.SemaphoreType.DMA((2,)), pltpu.SemaphoreType.DMA(())],
        compiler_params=pltpu.CompilerParams(dimension_semantics=("arbitrary",), vmem_limit_bytes=VMEM_LIMIT),
        name="outproj",
    )(y_ssm, y_att, x, mod4, mod4, mod4, w_out, ln_g, ln_b, wr_hi, wr_lo, rb)


def _tile_copy(src_hbm, row, buf, slot, sem):
    return pltpu.make_async_copy(src_hbm.at[row], buf.at[pl.ds(slot * ROW_TILE, ROW_TILE)], sem)


def _moe_kernel(be_ref, br_ref, nu_ref, nxt_ref, par_ref, nv_ref, x_ref, wg_hbm, wu_hbm, wd_hbm, y_ref,
                wfg, wfu, wfd, wgb, wub, wdb, wsem, *, blk):
    i = pl.program_id(0)
    used = i < nu_ref[0]

    def fetch(e, slot):
        return [pltpu.make_async_copy(src.at[e], dst.at[slot], wsem.at[slot])
                for src, dst in ((wg_hbm, wfg), (wu_hbm, wfu), (wd_hbm, wfd))]

    @pl.when(i == 0)
    def _first():
        for c in fetch(be_ref[0], par_ref[0]):
            c.start()

    @pl.when(jnp.logical_and(used, jnp.logical_or(i == 0, be_ref[i] != be_ref[jnp.maximum(i - 1, 0)])))
    def _switch():
        slot = par_ref[i]
        for c in fetch(be_ref[i], slot):
            c.wait()

        @pl.when(nxt_ref[i] >= 0)
        def _next():
            for c in fetch(nxt_ref[i], 1 - slot):
                c.start(priority=1)
        wgb[...] = wfg[slot].astype(bf16)
        wub[...] = wfu[slot].astype(bf16)
        wdb[...] = wfd[slot].astype(bf16)

    @pl.when(jnp.logical_not(used))
    def _spare():
        y_ref[...] = jnp.zeros_like(y_ref)

    def mlp(rows):
        xa, xb = _load_packed_rows(x_ref.at[pl.ds(0, rows * PACK_TILE)], rows)
        half = D // 2
        gate = _dot(xa, wgb[0:half, :]) + _dot(xb, wgb[half:D, :])
        up = _dot(xa, wub[0:half, :]) + _dot(xb, wub[half:D, :])
        hid = (_silu(gate) * up).astype(bf16)
        _store_row_tiles(y_ref.at[pl.ds(0, rows * ROW_TILE)], _dot(hid, wdb[...]))

    sparse = nv_ref[i] <= blk // 2

    @pl.when(jnp.logical_and(used, jnp.logical_not(sparse)))
    def _full():
        mlp(blk)

    @pl.when(jnp.logical_and(used, sparse))
    def _half():
        mlp(blk // 2)
        y_ref[pl.ds(blk // 2 * ROW_TILE, blk // 2 * ROW_TILE), :] = jnp.zeros((blk // 2 * ROW_TILE, 128), f32)


def _moe(block_e, block_row, n_used, next_e, parity, n_valid, x_rows, w_gate, w_up, w_down, blk):
    nblk = block_e.shape[0]
    rspec = pl.BlockSpec((blk * ROW_TILE, 128), lambda i, be, br, *_: (br[i], 0))
    xspec = pl.BlockSpec((blk * PACK_TILE, 128), lambda i, be, br, nu, *_: (br[jnp.minimum(i, nu[0] - 1)], 0))
    anyspec = pl.BlockSpec(memory_space=pl.ANY)
    grid_spec = pltpu.PrefetchScalarGridSpec(
        num_scalar_prefetch=6,
        grid=(nblk,),
        in_specs=[xspec, anyspec, anyspec, anyspec],
        out_specs=rspec,
        scratch_shapes=[pltpu.VMEM((2, D, D_EXPERT), f32), pltpu.VMEM((2, D, D_EXPERT), f32),
                        pltpu.VMEM((2, D_EXPERT, D), f32),
                        pltpu.VMEM((D, D_EXPERT), bf16), pltpu.VMEM((D, D_EXPERT), bf16),
                        pltpu.VMEM((D_EXPERT, D), bf16),
                        pltpu.SemaphoreType.DMA((2,))],
    )
    return pl.pallas_call(
        functools.partial(_moe_kernel, blk=blk),
        grid_spec=grid_spec,
        out_shape=jax.ShapeDtypeStruct((nblk * blk * ROW_TILE, 128), f32),
        compiler_params=pltpu.CompilerParams(dimension_semantics=("arbitrary",), vmem_limit_bytes=VMEM_LIMIT),
        name="moe",
    )(block_e, block_row, n_used, next_e, parity, n_valid, x_rows, w_gate, w_up, w_down)


def _combine_kernel(d1_ref, d2_ref, y_hbm, x1_ref, rt_ref, gt_ref, lng_ref, lnb_ref, o_ref, buf, sem, *, tm):
    i = pl.program_id(0)
    last = pl.num_programs(0) - 1

    @pl.when(i == 0)
    def _prologue():
        for s in range(2):
            def issue(r, c, s=s):
                base = jnp.minimum(s, last) * tm
                _tile_copy(y_hbm, d1_ref[base + r], buf.at[s, 0], r, sem.at[s]).start()
                _tile_copy(y_hbm, d2_ref[base + r], buf.at[s, 1], r, sem.at[s]).start()
                return c
            lax.fori_loop(0, tm, issue, 0)

    def rows_ready(slot):
        return pltpu.make_async_copy(buf.at[slot], buf.at[slot], sem.at[slot])

    slot = i % 3
    nxt = (i + 2) % 3
    base = jnp.minimum(i + 2, last) * tm
    rows_ready(slot).wait()
    for r in range(tm):
        _tile_copy(y_hbm, d1_ref[base + r], buf.at[nxt, 0], r, sem.at[nxt]).start(priority=0)
        _tile_copy(y_hbm, d2_ref[base + r], buf.at[nxt, 1], r, sem.at[nxt]).start(priority=1)
    rt = rt_ref[...]
    moe = rt[:, 4:5] * _load_row_tiles(buf.at[slot, 0], tm) + rt[:, 5:6] * _load_row_tiles(buf.at[slot, 1], tm)
    y = ALPHA * x1_ref[...] + (1.0 + gt_ref[0, 0]) * moe
    o_ref[...] = _layer_norm(y, lng_ref[...], lnb_ref[...])

    @pl.when(i == last)
    def _drain():
        rows_ready((i + 1) % 3).wait()
        rows_ready((i + 2) % 3).wait()


def _combine(dest1, dest2, y_rows, x1, rt, mod4, ln_g, ln_b, t):
    n = x1.shape[0]
    tm = min(COMBINE_ROWS, t)
    nt = t // tm
    full = lambda shape: pl.BlockSpec(shape, lambda i, d1, d2: (0,) * len(shape))
    rows = lambda w: pl.BlockSpec((tm, w), lambda i, d1, d2: (i, 0))
    grid_spec = pltpu.PrefetchScalarGridSpec(
        num_scalar_prefetch=2,
        grid=(n // tm,),
        in_specs=[pl.BlockSpec(memory_space=pl.ANY), rows(D), rows(128),
                  pl.BlockSpec((1, 1, 1, D), lambda i, d1, d2: (5, i // nt, 0, 0)),
                  full((1, D)), full((1, D))],
        out_specs=rows(D),
        scratch_shapes=[pltpu.VMEM((3, 2, tm * ROW_TILE, 128), f32), pltpu.SemaphoreType.DMA((3,))],
    )
    return pl.pallas_call(
        functools.partial(_combine_kernel, tm=tm),
        grid_spec=grid_spec,
        out_shape=jax.ShapeDtypeStruct((n, D), f32),
        compiler_params=pltpu.CompilerParams(dimension_semantics=("arbitrary",), vmem_limit_bytes=VMEM_LIMIT),
        name="combine",
    )(dest1, dest2, y_rows, x1, rt, mod4, ln_g, ln_b)


def _hi_lo(w):
    hi = w.astype(bf16)
    return hi, (w - hi.astype(f32)).astype(bf16)


def _pad_lanes(v, offset, width):
    return jnp.zeros((width,), f32).at[offset:offset + v.shape[0]].set(v)


def _layer(x, c, ada_w, ada_b, w_in, conv_w, conv_b, dt_bias, a_log, d_skip, ssm_norm_g, fg_bias, att_norm_g,
           w_out, ln1_g, ln1_b, router_g_w, router_g_b, router_e_w, router_e_b, w_gate, w_up, w_down, ln2_g, ln2_b):
    bsz, t, _ = x.shape
    n = bsz * t

    mod4 = _ada(c, ada_w, ada_b).reshape(6, bsz, 1, D)

    w_main, w_small = _pack_w_in(w_in)
    zx, bc, qkv, dtf = _inproj(x, mod4, w_main, w_small, conv_w, conv_b.reshape(1, -1))

    pc = jnp.stack([_pad_lanes(dt_bias, 0, 128), _pad_lanes(a_log, 0, 128), _pad_lanes(fg_bias, 8, 128)]
                   + [jnp.zeros((128,), f32)] * 5)
    rep = lambda v: jnp.repeat(v, HEAD_DIM)
    pe = jnp.stack([rep(dt_bias), rep(a_log), rep(d_skip), ssm_norm_g] + [jnp.zeros((D_SSM,), f32)] * 4)
    y_ssm, cumc = _ssd(zx, bc, dtf, pc, pe)

    y_att = _attn(qkv, cumc, att_norm_g.reshape(1, -1))

    wr = jnp.concatenate([router_g_w, router_e_w, jnp.zeros((D, 128 - N_GROUPS_R - N_EXPERTS), f32)], axis=1)
    wr_hi, wr_lo = _hi_lo(wr)
    rb = jnp.concatenate([router_g_b, router_e_b, jnp.zeros((128 - N_GROUPS_R - N_EXPERTS,), f32)]).reshape(1, 128)
    blk = MOE_ROWS
    nblk = (2 * n) // blk + N_EXPERTS
    x1, rt, dest, cnt, tbl, x_rows = _outproj(y_ssm.reshape(n, D_SSM), y_att.reshape(n, D_ATT), x.reshape(n, D), mod4,
                                              w_out.astype(bf16), ln1_g.reshape(1, D), ln1_b.reshape(1, D),
                                              wr_hi, wr_lo, rb, t, blk, nblk)

    counts = cnt[0, N_GROUPS_R:N_GROUPS_R + N_EXPERTS].astype(i32)
    nb = (counts + blk - 1) // blk
    nb_end = jnp.cumsum(nb)
    n_used = nb_end[N_EXPERTS - 1:]
    every = jnp.arange(nblk, dtype=i32)
    step = jnp.minimum(every, n_used[0] - 1)
    owner = (nb_end[None, :] <= step[:, None]).astype(i32)
    block_e = jnp.sum(owner, axis=1)
    first = jnp.sum(owner * nb[None, :], axis=1)
    lanes = jnp.arange(128, dtype=i32)[None, :]
    nth_row = jnp.dot((lanes == (step - first)[:, None]).astype(f32), tbl, precision=lax.Precision.HIGHEST)
    block_id = jnp.sum(jnp.where(lanes == (block_e + N_GROUPS_R)[:, None], nth_row, 0.0), axis=1).astype(i32)
    block_row = jnp.where(every < n_used[0], block_id, every)
    run_end = first + jnp.sum((jnp.arange(N_EXPERTS)[None, :] == block_e[:, None]) * nb[None, :], axis=1)
    next_e = jnp.where(run_end < n_used[0], jnp.sum((nb_end[None, :] <= run_end[:, None]).astype(i32), axis=1), -1)
    parity = jnp.sum(((jnp.arange(N_EXPERTS)[None, :] < block_e[:, None]) & (nb[None, :] > 0)).astype(i32), axis=1) % 2

    own = (jnp.arange(N_EXPERTS)[None, :] == block_e[:, None]).astype(i32)
    n_valid = jnp.clip(jnp.sum(own * counts[None, :], axis=1) - (step - first) * blk, 0, blk)
    y_rows = _moe(block_e, block_row, n_used, next_e, parity, n_valid, x_rows,
                  w_gate, w_up, w_down, blk)
    out = _combine(dest[0], dest[1], y_rows.reshape(nblk * blk, ROW_TILE, 128), x1, rt, mod4,
                   ln2_g.reshape(1, D), ln2_b.reshape(1, D), t)
    return out.reshape(bsz, t, D)


def kernel(x, c, ada_w, ada_b, w_in, conv_w, conv_b, dt_bias, a_log, d_skip, ssm_norm_g, fg_bias, att_norm_g, w_out,
           ln1_g, ln1_b, router_g_w, router_g_b, router_e_w, router_e_b, w_gate, w_up, w_down, ln2_g, ln2_b):
    depth = ada_w.shape[0]
    for l in range(depth):
        x = _layer(x, c, ada_w[l], ada_b[l], w_in[l], conv_w[l], conv_b[l], dt_bias[l], a_log[l], d_skip[l],
                   ssm_norm_g[l], fg_bias[l], att_norm_g[l], w_out[l], ln1_g[l], ln1_b[l], router_g_w[l],
                   router_g_b[l], router_e_w[l], router_e_b[l], w_gate[l], w_up[l], w_down[l], ln2_g[l], ln2_b[l])
    return x
```

```python
import functools

import jax
import jax.numpy as jnp
import numpy as np
from jax import lax
from jax.experimental import pallas as pl
from jax.experimental.pallas import tpu as pltpu

f32 = jnp.float32
bf16 = jnp.bfloat16
i32 = jnp.int32

D = 1024
D_SSM = 512
D_ATT = 512
HEAD_DIM = 64
GROUP_W = 256
N_STATE = 128
CONV_K = 4
N_GROUPS_R = 4
EXPERTS_PER_GROUP = 8
N_EXPERTS = 32
D_EXPERT = 512
ALPHA = 2.0 ** 0.25
EPS = 1e-5
NEG = -1e30
LOG2E = 1.4426950408889634
QK_SCALE = HEAD_DIM ** -0.5 * LOG2E
V_ROWS = 80

SSD_CHUNK = 256
ATT_BLOCK = 256
INPROJ_ROWS = 512
OUTPROJ_ROWS = 1024
OUTPROJ_PARTS = 2
MOE_ROWS = 512
COMBINE_ROWS = 256
VMEM_LIMIT = 48 * 1024 * 1024


def _dot(a, b):
    return jnp.dot(a, b, preferred_element_type=f32)


def _dot_nt(a, b):
    return lax.dot_general(a, b, (((1,), (1,)), ((), ())), preferred_element_type=f32)


def _dot_tn(a, b):
    return lax.dot_general(a, b, (((0,), (0,)), ((), ())), preferred_element_type=f32)


def _split3(v):
    hi = v.astype(bf16)
    r1 = v - hi.astype(f32)
    mid = r1.astype(bf16)
    lo = (r1 - mid.astype(f32)).astype(bf16)
    return hi, mid, lo


def _dot_exact_lhs(m, v):
    hi, mid, lo = _split3(v)
    return (_dot(m, hi) + _dot(m, mid)) + _dot(m, lo)


def _dot_exact_rhs(v, m):
    hi, mid, lo = _split3(v)
    return (_dot(hi, m) + _dot(mid, m)) + _dot(lo, m)


ROW_TILE = 8


def _store_row_tiles(ref, val):
    rows = val.shape[0]
    for c in range(ROW_TILE):
        ref[pl.ds(c, rows, stride=ROW_TILE), :] = val[:, c * 128:(c + 1) * 128]


def _load_row_tiles(ref, rows):
    return jnp.concatenate([ref[pl.ds(c, rows, stride=ROW_TILE), :] for c in range(ROW_TILE)], axis=1)


PACK_TILE = 4
u32 = jnp.uint32


def _pack_rows(vb):
    lo = pltpu.bitcast(vb[:, 0:D // 2].astype(f32), u32) >> 16
    hi = pltpu.bitcast(vb[:, D // 2:D].astype(f32), u32) & jnp.uint32(0xFFFF0000)
    return lo | hi


def _store_packed_rows(ref, packed):
    rows = packed.shape[0]
    for c in range(PACK_TILE):
        ref[pl.ds(c, rows, stride=PACK_TILE), :] = packed[:, c * 128:(c + 1) * 128]


def _load_packed_rows(ref, rows):
    words = [ref[pl.ds(c, rows, stride=PACK_TILE), :] for c in range(PACK_TILE)]
    lo = jnp.concatenate([pltpu.bitcast(w << 16, f32) for w in words], axis=1)
    hi = jnp.concatenate([pltpu.bitcast(w & jnp.uint32(0xFFFF0000), f32) for w in words], axis=1)
    return lo.astype(bf16), hi.astype(bf16)


def _softplus(x):
    return jnp.maximum(x, 0.0) + jnp.log1p(jnp.exp(-jnp.abs(x)))


def _silu(x):
    return x * jax.nn.sigmoid(x)


def _ada_kernel(c_ref, w_ref, b_ref, o_ref):
    s = _silu(c_ref[...]).astype(bf16)
    o_ref[0] = _dot(s, w_ref[...].astype(bf16)) + b_ref[0]


def _ada(c, w, b):
    bsz = c.shape[0]
    return pl.pallas_call(
        _ada_kernel,
        grid=(6,),
        in_specs=[pl.BlockSpec((bsz, D), lambda j: (0, 0)),
                  pl.BlockSpec((D, D), lambda j: (0, j)),
                  pl.BlockSpec((1, 1, D), lambda j: (j, 0, 0))],
        out_specs=pl.BlockSpec((1, bsz, D), lambda j: (j, 0, 0)),
        out_shape=jax.ShapeDtypeStruct((6, bsz, D), f32),
        compiler_params=pltpu.CompilerParams(dimension_semantics=("arbitrary",), vmem_limit_bytes=VMEM_LIMIT),
        name="ada",
    )(c, w, b.reshape(6, 1, D))


W_IN_COLS = 3088
PACK_ROWS = 256


def _pack_w_in_kernel(wt_ref, wm_ref, ws_ref):
    for j in range(3072 // PACK_ROWS):
        src = j * PACK_ROWS if j * PACK_ROWS < 1536 else j * PACK_ROWS + 8
        wm_ref[:, j * PACK_ROWS:(j + 1) * PACK_ROWS] = wt_ref[src:src + PACK_ROWS, :].T.astype(bf16)
    small = jnp.concatenate([wt_ref[1536:1544, :], wt_ref[W_IN_COLS - 8:W_IN_COLS, :], jnp.zeros((112, D), f32)],
                            axis=0).T
    hi = small.astype(bf16)
    ws_ref[:, 0:128] = hi
    ws_ref[:, 128:256] = (small - hi.astype(f32)).astype(bf16)


def _pack_w_in(w_in):
    full = lambda shape: pl.BlockSpec(shape, lambda i: (0,) * len(shape))
    return pl.pallas_call(
        _pack_w_in_kernel,
        grid=(1,),
        in_specs=[full((W_IN_COLS, D))],
        out_specs=[full((D, 3072)), full((D, 256))],
        out_shape=[jax.ShapeDtypeStruct((D, 3072), bf16), jax.ShapeDtypeStruct((D, 256), bf16)],
        compiler_params=pltpu.CompilerParams(dimension_semantics=("arbitrary",), vmem_limit_bytes=VMEM_LIMIT),
        name="pack_w_in",
    )(w_in.T)


def _inproj_kernel(x_ref, sc_ref, sh_ref, wm_ref, ws_ref, cw_ref, cb_ref, zx_ref, bc_ref, qkv_ref, dtf_ref, xcat, *, tm):
    i = pl.program_id(1)

    @pl.when(i == 0)
    def _init():
        xcat[0:8, :] = jnp.zeros((8, 2 * D_SSM), f32)

    u = x_ref[0] * (1.0 + sc_ref[0, 0]) + sh_ref[0, 0]
    ub = u.astype(bf16)
    xcat[8:8 + tm, 0:512] = _dot(ub, wm_ref[:, 512:1024])
    xcat[8:8 + tm, 512:1024] = _dot(ub, wm_ref[:, 1024:1536])
    zx_ref[0, :, 0:512] = _dot(ub, wm_ref[:, 0:512])
    acc = cw_ref[0:1, :] * xcat[5:5 + tm, :] + cb_ref[...]
    for k in range(1, CONV_K):
        acc = acc + cw_ref[k:k + 1, :] * xcat[5 + k:5 + k + tm, :]
    xcat[0:8, :] = xcat[tm:tm + 8, :]
    xbc = _silu(acc)
    zx_ref[0, :, 512:1024] = xbc[:, 0:512]
    bc_ref[0] = xbc[:, 512:1024].astype(bf16)
    for j, scale in enumerate((QK_SCALE, 1.0, 1.0)):
        qkv_ref[0, :, j * 512:(j + 1) * 512] = (
            _dot(ub, wm_ref[:, 1536 + j * 512:1536 + (j + 1) * 512]) * scale).astype(bf16)
    ul = (u - ub.astype(f32)).astype(bf16)
    d_hl = _dot(ub, ws_ref[...])
    dtf_ref[0] = (d_hl[:, 0:128] + _dot(ul, ws_ref[:, 0:128])) + d_hl[:, 128:256]


def _inproj(x, mod4, w_main, ws, conv_w, conv_b):
    bsz, t, _ = x.shape
    tm = min(INPROJ_ROWS, t)
    vec = lambda k: pl.BlockSpec((1, 1, 1, D), lambda b, i, k=k: (k, b, 0, 0))
    full = lambda shape: pl.BlockSpec(shape, lambda b, i: (0,) * len(shape))
    rows = lambda w: pl.BlockSpec((1, tm, w), lambda b, i: (b, i, 0))
    return pl.pallas_call(
        functools.partial(_inproj_kernel, tm=tm),
        grid=(bsz, t // tm),
        in_specs=[rows(D), vec(1), vec(0), full((D, 3072)), full((D, 256)),
                  full((CONV_K, 2 * D_SSM)), full((1, 2 * D_SSM))],
        out_specs=[rows(1024), rows(512), rows(1536), rows(128)],
        out_shape=[jax.ShapeDtypeStruct((bsz, t, 1024), f32),
                   jax.ShapeDtypeStruct((bsz, t, 512), bf16),
                   jax.ShapeDtypeStruct((bsz, t, 1536), bf16),
                   jax.ShapeDtypeStruct((bsz, t, 128), f32)],
        scratch_shapes=[pltpu.VMEM((tm + 8, 2 * D_SSM), f32)],
        compiler_params=pltpu.CompilerParams(dimension_semantics=("parallel", "arbitrary"),
                                             vmem_limit_bytes=VMEM_LIMIT),
        name="inproj",
    )(x, mod4, mod4, w_main, ws, conv_w, conv_b)


def _ssd_kernel(z_ref, xs_ref, bc_ref, dtf_ref, pc_ref, pe_ref, y_ref, cumc_ref, state, carry, *, lc):
    j = pl.program_id(1)

    @pl.when(j == 0)
    def _init():
        state[...] = jnp.zeros_like(state)
        carry[...] = jnp.zeros_like(carry)

    xs = xs_ref[0]
    bm = bc_ref[0, :, 0:256]
    cm = bc_ref[0, :, 256:512]

    dtf = dtf_ref[0]
    lane = lax.broadcasted_iota(i32, (lc, 128), 1)
    dt_c = _softplus(dtf + pc_ref[0:1, :])
    a_c = dt_c * (-jnp.exp(pc_ref[1:2, :]))
    logf = -_softplus(-(dtf + pc_ref[2:3, :]))
    v = jnp.where(lane < 8, a_c, logf) * LOG2E
    r_i = lax.broadcasted_iota(i32, (lc, lc), 0)
    c_i = lax.broadcasted_iota(i32, (lc, lc), 1)
    tri = r_i >= c_i
    tri_b = jnp.where(tri, 1.0, 0.0).astype(bf16)
    cum = _dot_exact_lhs(tri_b, v) + carry[...]
    carry[...] = jnp.where(lane[0:1, :] >= 8, cum[lc - 1:lc, :], 0.0)
    cumc_ref[0] = cum
    cs_t = cum.T[0:8, :]

    e_r = lax.broadcasted_iota(i32, (128, D_SSM), 0)
    e_c = lax.broadcasted_iota(i32, (128, D_SSM), 1)
    expand = jnp.where(jnp.right_shift(e_c, 6) == e_r, 1.0, 0.0).astype(bf16)
    dt_e = _dot_exact_rhs(dt_c, expand)
    cs_e = _dot_exact_rhs(cum, expand)

    xdt = xs * dt_e
    ecs = jnp.exp2(cs_e)
    cs_last = cs_e[lc - 1:lc, :]
    dec_st = jnp.exp2(cs_last - cs_e)
    lane_g = lax.broadcasted_iota(i32, (1, GROUP_W), 1)
    ys = []
    for g in range(2):
        gs = slice(g * GROUP_W, (g + 1) * GROUP_W)
        bg = bm[:, g * N_STATE:(g + 1) * N_STATE]
        cg = cm[:, g * N_STATE:(g + 1) * N_STATE]
        cb = _dot_nt(cg, bg)
        xdt_g = xdt[:, gs]
        xdt_gb = xdt_g.astype(bf16)
        ms, xb = [], []
        for hh in range(4):
            h = g * 4 + hh
            lm = jnp.exp2(jnp.where(tri, cum[:, h:h + 1] - cs_t[h:h + 1, :], -jnp.inf))
            ms.append((cb * lm).astype(bf16))
            xb.append(jnp.where(jnp.right_shift(lane_g, 6) == hh, xdt_gb, jnp.zeros_like(xdt_gb)))
        y_diag = _dot(jnp.concatenate(ms, axis=1), jnp.concatenate(xb, axis=0))
        st = state[g]
        y_off = _dot(cg, st.astype(bf16)) * ecs[:, gs]
        upd = _dot_tn(bg, (xdt_g * dec_st[:, gs]).astype(bf16))
        state[g] = st * jnp.exp2(cs_last[:, gs]) + upd
        ys.append(y_diag + y_off + xs[:, gs] * pe_ref[2:3, gs])

    outs = []
    for g in range(2):
        gs = slice(g * GROUP_W, (g + 1) * GROUP_W)
        yg = ys[g] * _silu(z_ref[0, :, gs])
        ms_ = jnp.mean(yg * yg, axis=-1, keepdims=True)
        outs.append(yg * lax.rsqrt(ms_ + EPS))
    y_ref[0] = (jnp.concatenate(outs, axis=1) * pe_ref[3:4, :]).astype(bf16)


def _ssd(zx, bc, dtf, pc, pe):
    bsz, t, _ = zx.shape
    lc = min(SSD_CHUNK, t)
    col = lambda k: pl.BlockSpec((1, lc, 512), lambda b, j, k=k: (b, j, k))
    full = lambda shape: pl.BlockSpec(shape, lambda b, j: (0,) * len(shape))
    return pl.pallas_call(
        functools.partial(_ssd_kernel, lc=lc),
        grid=(bsz, t // lc),
        in_specs=[col(0), col(1), col(0),
                  pl.BlockSpec((1, lc, 128), lambda b, j: (b, j, 0)),
                  full((8, 128)), full((8, D_SSM))],
        out_specs=[pl.BlockSpec((1, lc, D_SSM), lambda b, j: (b, j, 0)),
                   pl.BlockSpec((1, lc, 128), lambda b, j: (b, j, 0))],
        out_shape=[jax.ShapeDtypeStruct((bsz, t, D_SSM), bf16),
                   jax.ShapeDtypeStruct((bsz, t, 128), f32)],
        scratch_shapes=[pltpu.VMEM((2, N_STATE, GROUP_W), f32),
                        pltpu.VMEM((1, 128), f32)],
        compiler_params=pltpu.CompilerParams(dimension_semantics=("parallel", "arbitrary"),
                                             vmem_limit_bytes=VMEM_LIMIT),
        name="ssd",
    )(zx, zx, bc, dtf, pc, pe)


def _attn_kernel(q_ref, k_ref, v_ref, cc_ref, psel_ref, ng_ref, o_ref, kaug, vt, acc, sc0, sc1, *, tq, t):
    i = pl.program_id(1)
    nkb = t // tq
    n_heads = D_ATT // HEAD_DIM
    lane = lax.broadcasted_iota(i32, (1, 128), 1)
    lo_half = lane < HEAD_DIM

    @pl.when(i == 0)
    def _build():
        ones_rows = jnp.where(lax.broadcasted_iota(i32, (V_ROWS - HEAD_DIM, tq), 0) == 0, 1.0, 0.0).astype(bf16)
        for jb in range(nkb):
            rows = slice(jb * tq, (jb + 1) * tq)
            v_t = v_ref[0, rows, :].astype(f32).T.astype(bf16)
            for h in range(n_heads):
                vt[jb, h * V_ROWS:h * V_ROWS + HEAD_DIM, :] = v_t[h * HEAD_DIM:(h + 1) * HEAD_DIM, :]
                vt[jb, h * V_ROWS + HEAD_DIM:(h + 1) * V_ROWS, :] = ones_rows
            pieces = jnp.concatenate(_split3(-cc_ref[0, rows, :]), axis=1)
            spare = _dot(pieces, psel_ref[...]).astype(bf16)
            for p in range(n_heads // 2):
                a = spare[:, p * 128:(p + 1) * 128]
                kp = k_ref[0, rows, p * 128:(p + 1) * 128]
                kaug[2 * p, rows, :] = jnp.where(lo_half, kp, a)
                kaug[2 * p + 1, rows, :] = jnp.where(lo_half, a, kp)

    ones_hi = jnp.where((lane >= HEAD_DIM) & (lane < HEAD_DIM + 3), 1.0, 0.0).astype(bf16)
    ones_lo = jnp.where(lane < 3, 1.0, 0.0).astype(bf16)
    qa = []
    for p in range(n_heads // 2):
        qp = q_ref[0, :, p * 128:(p + 1) * 128]
        qa.append(jnp.where(lo_half, qp, ones_hi))
        qa.append(jnp.where(lo_half, ones_lo, qp))
    keep = lax.broadcasted_iota(i32, (tq, tq), 0) <= lax.broadcasted_iota(i32, (tq, tq), 1)
    acc[...] = jnp.zeros_like(acc)

    def score(jb, buf):
        k0 = pl.multiple_of(jb * tq, tq)
        for h in range(n_heads):
            buf[h] = _dot_nt(kaug[h, pl.ds(k0, tq), :], qa[h])

    def absorb(jb, ms, masked, buf):
        new_ms = []
        for h in range(n_heads):
            s = buf[h]
            if masked:
                s = jnp.where(keep, s, NEG)
            m_new = jnp.maximum(ms[h], jnp.max(s, axis=0, keepdims=True))
            alpha = jnp.exp2(ms[h] - m_new)
            p = jnp.exp2(s - m_new).astype(bf16)
            new_ms.append(m_new)
            acc[h] = acc[h] * alpha + _dot(vt[jb, h * V_ROWS:(h + 1) * V_ROWS, :], p)
        return tuple(new_ms)

    def pair(pp, ms):
        j0 = 2 * pp
        score(j0 + 1, sc1)
        ms = absorb(j0, ms, False, sc0)
        score(j0 + 2, sc0)
        return absorb(j0 + 1, ms, False, sc1)

    score(0, sc0)
    ms = lax.fori_loop(0, i // 2, pair, tuple(jnp.full((1, tq), NEG, f32) for _ in range(n_heads)))

    @pl.when(i % 2 == 0)
    def _even():
        absorb(i, ms, True, sc0)

    @pl.when(i % 2 == 1)
    def _odd():
        score(i, sc1)
        absorb(i, absorb(i - 1, ms, False, sc0), True, sc1)

    out_t = jnp.concatenate([acc[h, 0:HEAD_DIM, :] * (1.0 / acc[h, HEAD_DIM:HEAD_DIM + 1, :]) for h in range(n_heads)],
                            axis=0)
    ms_ = jnp.mean(out_t * out_t, axis=0, keepdims=True)
    out_t = out_t * lax.rsqrt(ms_ + EPS)
    o_ref[0] = (out_t.T * ng_ref[...]).astype(bf16)


def _piece_select():
    sel = np.zeros((384, 4 * 128), np.float32)
    for pair in range(4):
        for j in range(3):
            sel[j * 128 + 8 + 2 * pair, pair * 128 + HEAD_DIM + j] = 1.0
            sel[j * 128 + 8 + 2 * pair + 1, pair * 128 + j] = 1.0
    return jnp.asarray(sel, bf16)


def _attn(qkv, cumc, norm_g):
    bsz, t, _ = qkv.shape
    tq = min(ATT_BLOCK, t)
    return pl.pallas_call(
        functools.partial(_attn_kernel, tq=tq, t=t),
        grid=(bsz, t // tq),
        in_specs=[pl.BlockSpec((1, tq, D_ATT), lambda b, i: (b, i, 0)),
                  pl.BlockSpec((1, t, D_ATT), lambda b, i: (b, 0, 1)),
                  pl.BlockSpec((1, t, D_ATT), lambda b, i: (b, 0, 2)),
                  pl.BlockSpec((1, t, 128), lambda b, i: (b, 0, 0)),
                  pl.BlockSpec((384, 512), lambda b, i: (0, 0)),
                  pl.BlockSpec((1, D_ATT), lambda b, i: (0, 0))],
        out_specs=pl.BlockSpec((1, tq, D_ATT), lambda b, i: (b, i, 0)),
        out_shape=jax.ShapeDtypeStruct((bsz, t, D_ATT), bf16),
        scratch_shapes=[pltpu.VMEM((D_ATT // HEAD_DIM, t, 128), bf16),
                        pltpu.VMEM((t // tq, (D_ATT // HEAD_DIM) * V_ROWS, tq), bf16),
                        pltpu.VMEM((D_ATT // HEAD_DIM, V_ROWS, tq), f32),
                        pltpu.VMEM((D_ATT // HEAD_DIM, tq, tq), f32),
                        pltpu.VMEM((D_ATT // HEAD_DIM, tq, tq), f32)],
        compiler_params=pltpu.CompilerParams(dimension_semantics=("parallel", "arbitrary"),
                                             vmem_limit_bytes=VMEM_LIMIT),
        name="attn",
    )(qkv, qkv, qkv, cumc, _piece_select(), norm_g)


def _layer_norm(y, g, b):
    mu = jnp.mean(y, axis=-1, keepdims=True)
    yc = y - mu
    var = jnp.mean(yc * yc, axis=-1, keepdims=True)
    return yc * lax.rsqrt(var + EPS) * g + b


def _outproj_kernel(ys_ref, ya_ref, x_ref, gt_ref, sc_ref, sh_ref, wo_ref, lng_ref, lnb_ref, wrh_ref, wrl_ref, rb_ref,
                    x1_ref, rt_ref, dest_ref, cnt_ref, tbl_ref, xrows_hbm,
                    carry, galloc, tbl, u2t, didx_v, didx_s, cnt_v, cnt_s, zeros, idx_sem, sc_sem, z_sem,
                    *, tm, blk, n_blk):
    i = pl.program_id(0)
    last = pl.num_programs(0) - 1
    cur = i % 2
    prev = 1 - cur

    def idx_ready(slot):
        return pltpu.make_async_copy(didx_v.at[slot], didx_s.at[slot], idx_sem.at[slot])

    def dispatched(slot):
        return pltpu.make_async_copy(u2t.at[slot], u2t.at[slot], sc_sem.at[slot])

    def dispatch_copy(slot, r, k):
        return pltpu.make_async_copy(u2t.at[slot, pl.ds(r * PACK_TILE, PACK_TILE)],
                                     xrows_hbm.at[pl.ds(didx_s[slot, k, r] * PACK_TILE, PACK_TILE)], sc_sem.at[slot])

    @pl.when(i == 0)
    def _init():
        carry[...] = jnp.zeros_like(carry)
        galloc[...] = jnp.zeros_like(galloc)
        tbl[...] = jnp.zeros_like(tbl)

    @pl.when(i >= 2)
    def _reuse():
        dispatched(cur).wait()
        dispatched(cur).wait()

    def step(dispatch_prev):
        n_parts = OUTPROJ_PARTS
        n_slices = 3 * n_parts + 2

        def dispatch_slice(c):
            if dispatch_prev:
                for r_ in range(c * tm // n_slices, (c + 1) * tm // n_slices):
                    dispatch_copy(prev, r_, 0).start(priority=0)
                    dispatch_copy(prev, r_, 1).start(priority=1)

        if dispatch_prev:
            idx_ready(prev).wait()
        part = tm // n_parts
        hs = []
        for a in range(n_parts):
            rs = slice(a * part, (a + 1) * part)
            dispatch_slice(a)
            hs.append(_dot(ys_ref[rs, :], wo_ref[0:D_SSM, :]) + _dot(ya_ref[rs, :], wo_ref[D_SSM:D, :]))
        logit_parts = []
        for a in range(n_parts):
            rs = slice(a * part, (a + 1) * part)
            dispatch_slice(n_parts + 2 * a)
            x1 = _layer_norm(ALPHA * x_ref[rs, :] + (1.0 + gt_ref[0, 0]) * hs[a], lng_ref[...], lnb_ref[...])
            x1_ref[rs, :] = x1
            u2 = x1 * (1.0 + sc_ref[0, 0]) + sh_ref[0, 0]
            uh = u2.astype(bf16)
            _store_packed_rows(u2t.at[cur, pl.ds(a * part * PACK_TILE, part * PACK_TILE)], _pack_rows(uh))
            dispatch_slice(n_parts + 2 * a + 1)
            ul = (u2 - uh.astype(f32)).astype(bf16)
            logit_parts.append((_dot(uh, wrh_ref[...]) + _dot(ul, wrh_ref[...])) + _dot(uh, wrl_ref[...]))
        logits = jnp.concatenate(logit_parts, axis=0) + rb_ref[...]
        lane = lax.broadcasted_iota(i32, (tm, 128), 1).astype(f32)
        big = jnp.float32(1e9)

        def first_max(vals):
            m = jnp.max(vals, axis=-1, keepdims=True)
            return m, jnp.min(jnp.where(vals == m, lane, big), axis=-1, keepdims=True)

        gl = jnp.where(lane < N_GROUPS_R, logits, NEG)
        gmax, gidx = first_max(gl)
        g_p = 1.0 / jnp.sum(jnp.exp(gl - gmax), axis=-1, keepdims=True)
        lo = N_GROUPS_R + EXPERTS_PER_GROUP * gidx
        el = jnp.where((lane >= lo) & (lane < lo + EXPERTS_PER_GROUP), logits, NEG)
        m1, i1 = first_max(el)
        el2 = jnp.where(lane == i1, NEG, el)
        m2, i2 = first_max(el2)
        r = jnp.exp(m2 - m1)
        w1 = g_p / (1.0 + r)
        w2 = g_p * r / (1.0 + r)
        dispatch_slice(3 * n_parts)

        oh1 = lane == i1
        oh2 = lane == i2
        oh = jnp.where(oh1 | oh2, 1.0, 0.0)
        r_i = lax.broadcasted_iota(i32, (tm, tm), 0)
        c_i = lax.broadcasted_iota(i32, (tm, tm), 1)
        lower = jnp.where(r_i > c_i, 1.0, 0.0).astype(bf16)
        c_old = carry[...]
        prefix = _dot(lower, oh.astype(bf16)) + c_old
        rank1 = jnp.sum(jnp.where(oh1, prefix, 0.0), axis=-1, keepdims=True)
        rank2 = jnp.sum(jnp.where(oh2, prefix, 0.0), axis=-1, keepdims=True)
        c_new = c_old + jnp.sum(oh, axis=0, keepdims=True)
        carry[...] = c_new
        dispatch_slice(3 * n_parts + 1)

        nb_old = jnp.floor((c_old + (blk - 1)) * (1.0 / blk))
        nb_new = jnp.floor((c_new + (blk - 1)) * (1.0 / blk))
        fresh = nb_new - nb_old
        sq_r = lax.broadcasted_iota(i32, (128, 128), 0)
        sq_c = lax.broadcasted_iota(i32, (128, 128), 1)
        before = jnp.where(sq_r < sq_c, 1.0, 0.0).astype(bf16)
        base = galloc[...] + _dot(jnp.broadcast_to(fresh, (8, 128)).astype(bf16), before)[0:1, :]
        galloc[...] = galloc[...] + jnp.sum(fresh, axis=-1, keepdims=True)
        ordinal = sq_r.astype(f32)
        tbl[...] = jnp.where((ordinal >= nb_old) & (ordinal < nb_new), base + (ordinal - nb_old), tbl[...])
        tbl_ref[...] = tbl[...]
        cnt_ref[...] = jnp.concatenate([c_new, galloc[...], jnp.zeros((6, 128), f32)], axis=0)

        rt = jnp.where(lane == 4, w1, jnp.where(lane == 5, w2, 0.0))
        rt_ref[...] = rt

        ids_bf = tbl[...].astype(bf16)

        def row_of(onehot, rank):
            nth = jnp.floor(rank * (1.0 / blk))
            ids = _dot(jnp.where(lane == nth, 1.0, 0.0).astype(bf16), ids_bf)
            return jnp.sum(jnp.where(onehot, ids, 0.0), axis=-1, keepdims=True) * blk + (rank - nth * blk)

        d1 = row_of(oh1, rank1)
        d2 = row_of(oh2, rank2)
        dd = jnp.where(lane == 0, d1, jnp.where(lane == 1, d2, 0.0)).T[0:8, :].astype(i32)
        dest_ref[...] = dd
        didx_v[cur] = dd
        idx_ready(cur).start()

    pl.when(i == 0)(lambda: step(False))
    pl.when(i > 0)(lambda: step(True))

    @pl.when(i == last)
    def _epilogue():
        idx_ready(cur).wait()

        def issue(r_, c):
            dispatch_copy(cur, r_, 0).start()
            dispatch_copy(cur, r_, 1).start()
            return c
        lax.fori_loop(0, tm, issue, 0)

        cnt = carry[...]
        n_blocks = jnp.floor((cnt + (blk - 1)) * (1.0 / blk))
        ordinal = lax.broadcasted_iota(i32, (128, 128), 0).astype(f32)
        last_id = jnp.sum(jnp.where(ordinal == n_blocks - 1.0, tbl[...], 0.0), axis=0, keepdims=True)
        used = cnt - (n_blocks - 1.0) * blk
        cnt_v[...] = jnp.concatenate([last_id * blk + used, blk - used, galloc[...], jnp.zeros((5, 128), f32)],
                                     axis=0).astype(i32)
        to_smem = pltpu.make_async_copy(cnt_v, cnt_s, z_sem)
        to_smem.start()
        to_smem.wait()
        zeros[...] = jnp.zeros_like(zeros)
        sizes = [1 << b for b in reversed(range(blk.bit_length() - 1))]

        def for_each_fill(fn):
            def tail(e, c):
                pad = cnt_s[1, N_GROUPS_R + e]
                off = cnt_s[0, N_GROUPS_R + e]
                for sz in sizes:
                    @pl.when((pad & sz) != 0)
                    def _(off=off, sz=sz):
                        fn(pltpu.make_async_copy(zeros.at[pl.ds(0, sz * PACK_TILE)],
                                                 xrows_hbm.at[pl.ds(off * PACK_TILE, sz * PACK_TILE)], z_sem))
                    off = off + (pad & sz)
                return c
            lax.fori_loop(0, N_EXPERTS, tail, 0)

            def whole(b, c):
                fn(pltpu.make_async_copy(zeros, xrows_hbm.at[pl.ds(b * blk * PACK_TILE, blk * PACK_TILE)], z_sem))
                return c
            lax.fori_loop(cnt_s[2, 0], n_blk, whole, 0)

        for_each_fill(lambda copy: copy.start())
        for_each_fill(lambda copy: copy.wait())

        @pl.when(i > 0)
        def _prev_done():
            dispatched(prev).wait()
            dispatched(prev).wait()
        dispatched(cur).wait()
        dispatched(cur).wait()


def _outproj(y_ssm, y_att, x, mod4, w_out, ln_g, ln_b, wr_hi, wr_lo, rb, t, blk, n_blk):
    n = x.shape[0]
    tm = min(OUTPROJ_ROWS, t)
    nt = t // tm
    assert n // blk + 1 <= 128 and n_blk <= 256, "block-id table: 128 blocks per expert, ids exact in bf16"
    vec = lambda k: pl.BlockSpec((1, 1, 1, D), lambda i, k=k: (k, i // nt, 0, 0))
    full = lambda shape: pl.BlockSpec(shape, lambda i: (0,) * len(shape))
    rows = lambda w: pl.BlockSpec((tm, w), lambda i: (i, 0))
    return pl.pallas_call(
        functools.partial(_outproj_kernel, tm=tm, blk=blk, n_blk=n_blk),
        grid=(n // tm,),
        in_specs=[rows(D_SSM), rows(D_ATT), rows(D), vec(2), vec(4), vec(3),
                  full((D, D)), full((1, D)), full((1, D)), full((D, 128)), full((D, 128)), full((1, 128))],
        out_specs=[rows(D), rows(128), pl.BlockSpec((8, tm), lambda i: (0, i)), full((8, 128)), full((128, 128)),
                   pl.BlockSpec(memory_space=pl.ANY)],
        out_shape=[jax.ShapeDtypeStruct((n, D), f32), jax.ShapeDtypeStruct((n, 128), f32),
                   jax.ShapeDtypeStruct((8, n), i32), jax.ShapeDtypeStruct((8, 128), f32),
                   jax.ShapeDtypeStruct((128, 128), f32),
                   jax.ShapeDtypeStruct((n_blk * blk * PACK_TILE, 128), u32)],
        scratch_shapes=[pltpu.VMEM((1, 128), f32), pltpu.VMEM((1, 128), f32), pltpu.VMEM((128, 128), f32),
                        pltpu.VMEM((2, tm * PACK_TILE, 128), u32),
                        pltpu.VMEM((2, 8, tm), i32), pltpu.SMEM((2, 8, tm), i32),
                        pltpu.VMEM((8, 128), i32), pltpu.SMEM((8, 128), i32),
                        pltpu.VMEM((blk * PACK_TILE, 128), u32),
                        pltpu.SemaphoreType.DMA((2,)), pltpu.SemaphoreType.DMA((2,)), pltpu.SemaphoreType.DMA(())],
        compiler_params=pltpu.CompilerParams(dimension_semantics=("arbitrary",), vmem_limit_bytes=VMEM_LIMIT),
        name="outproj",
    )(y_ssm, y_att, x, mod4, mod4, mod4, w_out, ln_g, ln_b, wr_hi, wr_lo, rb)


def _tile_copy(src_hbm, row, buf, slot, sem):
    return pltpu.make_async_copy(src_hbm.at[row], buf.at[pl.ds(slot * ROW_TILE, ROW_TILE)], sem)


def _moe_kernel(be_ref, br_ref, nu_ref, nxt_ref, par_ref, nv_ref, x_ref, wg_hbm, wu_hbm, wd_hbm, y_ref,
                wfg, wfu, wfd, wgb, wub, wdb, wsem, *, blk):
    i = pl.program_id(0)
    used = i < nu_ref[0]

    def fetch(e, slot):
        return [pltpu.make_async_copy(src.at[e], dst.at[slot], wsem.at[slot])
                for src, dst in ((wg_hbm, wfg), (wu_hbm, wfu), (wd_hbm, wfd))]

    @pl.when(i == 0)
    def _first():
        for c in fetch(be_ref[0], par_ref[0]):
            c.start()

    @pl.when(jnp.logical_and(used, jnp.logical_or(i == 0, be_ref[i] != be_ref[jnp.maximum(i - 1, 0)])))
    def _switch():
        slot = par_ref[i]
        for c in fetch(be_ref[i], slot):
            c.wait()

        @pl.when(nxt_ref[i] >= 0)
        def _next():
            for c in fetch(nxt_ref[i], 1 - slot):
                c.start(priority=1)
        wgb[...] = wfg[slot].astype(bf16)
        wub[...] = wfu[slot].astype(bf16)
        wdb[...] = wfd[slot].astype(bf16)

    @pl.when(jnp.logical_not(used))
    def _spare():
        y_ref[...] = jnp.zeros_like(y_ref)

    def mlp(rows):
        xa, xb = _load_packed_rows(x_ref.at[pl.ds(0, rows * PACK_TILE)], rows)
        half = D // 2
        gate = _dot(xa, wgb[0:half, :]) + _dot(xb, wgb[half:D, :])
        up = _dot(xa, wub[0:half, :]) + _dot(xb, wub[half:D, :])
        hid = (_silu(gate) * up).astype(bf16)
        _store_row_tiles(y_ref.at[pl.ds(0, rows * ROW_TILE)], _dot(hid, wdb[...]))

    sparse = nv_ref[i] <= blk // 2

    @pl.when(jnp.logical_and(used, jnp.logical_not(sparse)))
    def _full():
        mlp(blk)

    @pl.when(jnp.logical_and(used, sparse))
    def _half():
        mlp(blk // 2)
        y_ref[pl.ds(blk // 2 * ROW_TILE, blk // 2 * ROW_TILE), :] = jnp.zeros((blk // 2 * ROW_TILE, 128), f32)


def _moe(block_e, block_row, n_used, next_e, parity, n_valid, x_rows, w_gate, w_up, w_down, blk):
    nblk = block_e.shape[0]
    rspec = pl.BlockSpec((blk * ROW_TILE, 128), lambda i, be, br, *_: (br[i], 0))
    xspec = pl.BlockSpec((blk * PACK_TILE, 128), lambda i, be, br, nu, *_: (br[jnp.minimum(i, nu[0] - 1)], 0))
    anyspec = pl.BlockSpec(memory_space=pl.ANY)
    grid_spec = pltpu.PrefetchScalarGridSpec(
        num_scalar_prefetch=6,
        grid=(nblk,),
        in_specs=[xspec, anyspec, anyspec, anyspec],
        out_specs=rspec,
        scratch_shapes=[pltpu.VMEM((2, D, D_EXPERT), f32), pltpu.VMEM((2, D, D_EXPERT), f32),
                        pltpu.VMEM((2, D_EXPERT, D), f32),
                        pltpu.VMEM((D, D_EXPERT), bf16), pltpu.VMEM((D, D_EXPERT), bf16),
                        pltpu.VMEM((D_EXPERT, D), bf16),
                        pltpu.SemaphoreType.DMA((2,))],
    )
    return pl.pallas_call(
        functools.partial(_moe_kernel, blk=blk),
        grid_spec=grid_spec,
        out_shape=jax.ShapeDtypeStruct((nblk * blk * ROW_TILE, 128), f32),
        compiler_params=pltpu.CompilerParams(dimension_semantics=("arbitrary",), vmem_limit_bytes=VMEM_LIMIT),
        name="moe",
    )(block_e, block_row, n_used, next_e, parity, n_valid, x_rows, w_gate, w_up, w_down)


def _combine_kernel(d1_ref, d2_ref, y_hbm, x1_ref, rt_ref, gt_ref, lng_ref, lnb_ref, o_ref, buf, sem, *, tm):
    i = pl.program_id(0)
    last = pl.num_programs(0) - 1

    @pl.when(i == 0)
    def _prologue():
        for s in range(2):
            def issue(r, c, s=s):
                base = jnp.minimum(s, last) * tm
                _tile_copy(y_hbm, d1_ref[base + r], buf.at[s, 0], r, sem.at[s]).start()
                _tile_copy(y_hbm, d2_ref[base + r], buf.at[s, 1], r, sem.at[s]).start()
                return c
            lax.fori_loop(0, tm, issue, 0)

    def rows_ready(slot):
        return pltpu.make_async_copy(buf.at[slot], buf.at[slot], sem.at[slot])

    slot = i % 3
    nxt = (i + 2) % 3
    base = jnp.minimum(i + 2, last) * tm
    rows_ready(slot).wait()
    for r in range(tm):
        _tile_copy(y_hbm, d1_ref[base + r], buf.at[nxt, 0], r, sem.at[nxt]).start(priority=0)
        _tile_copy(y_hbm, d2_ref[base + r], buf.at[nxt, 1], r, sem.at[nxt]).start(priority=1)
    rt = rt_ref[...]
    moe = rt[:, 4:5] * _load_row_tiles(buf.at[slot, 0], tm) + rt[:, 5:6] * _load_row_tiles(buf.at[slot, 1], tm)
    y = ALPHA * x1_ref[...] + (1.0 + gt_ref[0, 0]) * moe
    o_ref[...] = _layer_norm(y, lng_ref[...], lnb_ref[...])

    @pl.when(i == last)
    def _drain():
        rows_ready((i + 1) % 3).wait()
        rows_ready((i + 2) % 3).wait()


def _combine(dest1, dest2, y_rows, x1, rt, mod4, ln_g, ln_b, t):
    n = x1.shape[0]
    tm = min(COMBINE_ROWS, t)
    nt = t // tm
    full = lambda shape: pl.BlockSpec(shape, lambda i, d1, d2: (0,) * len(shape))
    rows = lambda w: pl.BlockSpec((tm, w), lambda i, d1, d2: (i, 0))
    grid_spec = pltpu.PrefetchScalarGridSpec(
        num_scalar_prefetch=2,
        grid=(n // tm,),
        in_specs=[pl.BlockSpec(memory_space=pl.ANY), rows(D), rows(128),
                  pl.BlockSpec((1, 1, 1, D), lambda i, d1, d2: (5, i // nt, 0, 0)),
                  full((1, D)), full((1, D))],
        out_specs=rows(D),
        scratch_shapes=[pltpu.VMEM((3, 2, tm * ROW_TILE, 128), f32), pltpu.SemaphoreType.DMA((3,))],
    )
    return pl.pallas_call(
        functools.partial(_combine_kernel, tm=tm),
        grid_spec=grid_spec,
        out_shape=jax.ShapeDtypeStruct((n, D), f32),
        compiler_params=pltpu.CompilerParams(dimension_semantics=("arbitrary",), vmem_limit_bytes=VMEM_LIMIT),
        name="combine",
    )(dest1, dest2, y_rows, x1, rt, mod4, ln_g, ln_b)


def _hi_lo(w):
    hi = w.astype(bf16)
    return hi, (w - hi.astype(f32)).astype(bf16)


def _pad_lanes(v, offset, width):
    return jnp.zeros((width,), f32).at[offset:offset + v.shape[0]].set(v)


def _layer(x, c, ada_w, ada_b, w_in, conv_w, conv_b, dt_bias, a_log, d_skip, ssm_norm_g, fg_bias, att_norm_g,
           w_out, ln1_g, ln1_b, router_g_w, router_g_b, router_e_w, router_e_b, w_gate, w_up, w_down, ln2_g, ln2_b):
    bsz, t, _ = x.shape
    n = bsz * t

    mod4 = _ada(c, ada_w, ada_b).reshape(6, bsz, 1, D)

    w_main, w_small = _pack_w_in(w_in)
    zx, bc, qkv, dtf = _inproj(x, mod4, w_main, w_small, conv_w, conv_b.reshape(1, -1))

    pc = jnp.stack([_pad_lanes(dt_bias, 0, 128), _pad_lanes(a_log, 0, 128), _pad_lanes(fg_bias, 8, 128)]
                   + [jnp.zeros((128,), f32)] * 5)
    rep = lambda v: jnp.repeat(v, HEAD_DIM)
    pe = jnp.stack([rep(dt_bias), rep(a_log), rep(d_skip), ssm_norm_g] + [jnp.zeros((D_SSM,), f32)] * 4)
    y_ssm, cumc = _ssd(zx, bc, dtf, pc, pe)

    y_att = _attn(qkv, cumc, att_norm_g.reshape(1, -1))

    wr = jnp.concatenate([router_g_w, router_e_w, jnp.zeros((D, 128 - N_GROUPS_R - N_EXPERTS), f32)], axis=1)
    wr_hi, wr_lo = _hi_lo(wr)
    rb = jnp.concatenate([router_g_b, router_e_b, jnp.zeros((128 - N_GROUPS_R - N_EXPERTS,), f32)]).reshape(1, 128)
    blk = MOE_ROWS
    nblk = (2 * n) // blk + N_EXPERTS
    x1, rt, dest, cnt, tbl, x_rows = _outproj(y_ssm.reshape(n, D_SSM), y_att.reshape(n, D_ATT), x.reshape(n, D), mod4,
                                              w_out.astype(bf16), ln1_g.reshape(1, D), ln1_b.reshape(1, D),
                                              wr_hi, wr_lo, rb, t, blk, nblk)

    counts = cnt[0, N_GROUPS_R:N_GROUPS_R + N_EXPERTS].astype(i32)
    nb = (counts + blk - 1) // blk
    nb_end = jnp.cumsum(nb)
    n_used = nb_end[N_EXPERTS - 1:]
    every = jnp.arange(nblk, dtype=i32)
    step = jnp.minimum(every, n_used[0] - 1)
    owner = (nb_end[None, :] <= step[:, None]).astype(i32)
    block_e = jnp.sum(owner, axis=1)
    first = jnp.sum(owner * nb[None, :], axis=1)
    lanes = jnp.arange(128, dtype=i32)[None, :]
    nth_row = jnp.dot((lanes == (step - first)[:, None]).astype(f32), tbl, precision=lax.Precision.HIGHEST)
    block_id = jnp.sum(jnp.where(lanes == (block_e + N_GROUPS_R)[:, None], nth_row, 0.0), axis=1).astype(i32)
    block_row = jnp.where(every < n_used[0], block_id, every)
    run_end = first + jnp.sum((jnp.arange(N_EXPERTS)[None, :] == block_e[:, None]) * nb[None, :], axis=1)
    next_e = jnp.where(run_end < n_used[0], jnp.sum((nb_end[None, :] <= run_end[:, None]).astype(i32), axis=1), -1)
    parity = jnp.sum(((jnp.arange(N_EXPERTS)[None, :] < block_e[:, None]) & (nb[None, :] > 0)).astype(i32), axis=1) % 2

    own = (jnp.arange(N_EXPERTS)[None, :] == block_e[:, None]).astype(i32)
    n_valid = jnp.clip(jnp.sum(own * counts[None, :], axis=1) - (step - first) * blk, 0, blk)
    y_rows = _moe(block_e, block_row, n_used, next_e, parity, n_valid, x_rows,
                  w_gate, w_up, w_down, blk)
    out = _combine(dest[0], dest[1], y_rows.reshape(nblk * blk, ROW_TILE, 128), x1, rt, mod4,
                   ln2_g.reshape(1, D), ln2_b.reshape(1, D), t)
    return out.reshape(bsz, t, D)


def kernel(x, c, ada_w, ada_b, w_in, conv_w, conv_b, dt_bias, a_log, d_skip, ssm_norm_g, fg_bias, att_norm_g, w_out,
           ln1_g, ln1_b, router_g_w, router_g_b, router_e_w, router_e_b, w_gate, w_up, w_down, ln2_g, ln2_b):
    depth = ada_w.shape[0]
    for l in range(depth):
        x = _layer(x, c, ada_w[l], ada_b[l], w_in[l], conv_w[l], conv_b[l], dt_bias[l], a_log[l], d_skip[l],
                   ssm_norm_g[l], fg_bias[l], att_norm_g[l], w_out[l], ln1_g[l], ln1_b[l], router_g_w[l],
                   router_g_b[l], router_e_w[l], router_e_b[l], w_gate[l], w_up[l], w_down[l], ln2_g[l], ln2_b[l])
    return x
```

```python
import functools

import jax
import jax.numpy as jnp
import numpy as np
from jax import lax
from jax.experimental import pallas as pl
from jax.experimental.pallas import tpu as pltpu

f32 = jnp.float32
bf16 = jnp.bfloat16
i32 = jnp.int32

D = 1024
D_SSM = 512
D_ATT = 512
HEAD_DIM = 64
GROUP_W = 256
N_STATE = 128
CONV_K = 4
N_GROUPS_R = 4
EXPERTS_PER_GROUP = 8
N_EXPERTS = 32
D_EXPERT = 512
ALPHA = 2.0 ** 0.25
EPS = 1e-5
NEG = -1e30
LOG2E = 1.4426950408889634
QK_SCALE = HEAD_DIM ** -0.5 * LOG2E
V_ROWS = 80

SSD_CHUNK = 256
ATT_BLOCK = 256
INPROJ_ROWS = 512
OUTPROJ_ROWS = 1024
OUTPROJ_PARTS = 4
MOE_ROWS = 512
COMBINE_ROWS = 256
VMEM_LIMIT = 48 * 1024 * 1024


def _dot(a, b):
    return jnp.dot(a, b, preferred_element_type=f32)


def _dot_nt(a, b):
    return lax.dot_general(a, b, (((1,), (1,)), ((), ())), preferred_element_type=f32)


def _dot_tn(a, b):
    return lax.dot_general(a, b, (((0,), (0,)), ((), ())), preferred_element_type=f32)


def _split3(v):
    hi = v.astype(bf16)
    r1 = v - hi.astype(f32)
    mid = r1.astype(bf16)
    lo = (r1 - mid.astype(f32)).astype(bf16)
    return hi, mid, lo


def _dot_exact_lhs(m, v):
    hi, mid, lo = _split3(v)
    return (_dot(m, hi) + _dot(m, mid)) + _dot(m, lo)


def _dot_exact_rhs(v, m):
    hi, mid, lo = _split3(v)
    return (_dot(hi, m) + _dot(mid, m)) + _dot(lo, m)


ROW_TILE = 8


def _store_row_tiles(ref, val):
    rows = val.shape[0]
    for c in range(ROW_TILE):
        ref[pl.ds(c, rows, stride=ROW_TILE), :] = val[:, c * 128:(c + 1) * 128]


def _load_row_tiles(ref, rows):
    return jnp.concatenate([ref[pl.ds(c, rows, stride=ROW_TILE), :] for c in range(ROW_TILE)], axis=1)


PACK_TILE = 4
u32 = jnp.uint32


def _pack_rows(vb):
    lo = pltpu.bitcast(vb[:, 0:D // 2].astype(f32), u32) >> 16
    hi = pltpu.bitcast(vb[:, D // 2:D].astype(f32), u32) & jnp.uint32(0xFFFF0000)
    return lo | hi


def _store_packed_rows(ref, packed):
    rows = packed.shape[0]
    for c in range(PACK_TILE):
        ref[pl.ds(c, rows, stride=PACK_TILE), :] = packed[:, c * 128:(c + 1) * 128]


def _load_packed_rows(ref, rows):
    words = [ref[pl.ds(c, rows, stride=PACK_TILE), :] for c in range(PACK_TILE)]
    lo = jnp.concatenate([pltpu.bitcast(w << 16, f32) for w in words], axis=1)
    hi = jnp.concatenate([pltpu.bitcast(w & jnp.uint32(0xFFFF0000), f32) for w in words], axis=1)
    return lo.astype(bf16), hi.astype(bf16)


def _softplus(x):
    return jnp.maximum(x, 0.0) + jnp.log1p(jnp.exp(-jnp.abs(x)))


def _silu(x):
    return x * jax.nn.sigmoid(x)


def _ada_kernel(c_ref, w_ref, b_ref, o_ref):
    s = _silu(c_ref[...]).astype(bf16)
    o_ref[0] = _dot(s, w_ref[...].astype(bf16)) + b_ref[0]


def _ada(c, w, b):
    bsz = c.shape[0]
    return pl.pallas_call(
        _ada_kernel,
        grid=(6,),
        in_specs=[pl.BlockSpec((bsz, D), lambda j: (0, 0)),
                  pl.BlockSpec((D, D), lambda j: (0, j)),
                  pl.BlockSpec((1, 1, D), lambda j: (j, 0, 0))],
        out_specs=pl.BlockSpec((1, bsz, D), lambda j: (j, 0, 0)),
        out_shape=jax.ShapeDtypeStruct((6, bsz, D), f32),
        compiler_params=pltpu.CompilerParams(dimension_semantics=("arbitrary",), vmem_limit_bytes=VMEM_LIMIT),
        name="ada",
    )(c, w, b.reshape(6, 1, D))


W_IN_COLS = 3088
PACK_ROWS = 256


def _pack_w_in_kernel(wt_ref, wm_ref, ws_ref):
    for j in range(3072 // PACK_ROWS):
        src = j * PACK_ROWS if j * PACK_ROWS < 1536 else j * PACK_ROWS + 8
        wm_ref[:, j * PACK_ROWS:(j + 1) * PACK_ROWS] = wt_ref[src:src + PACK_ROWS, :].T.astype(bf16)
    small = jnp.concatenate([wt_ref[1536:1544, :], wt_ref[W_IN_COLS - 8:W_IN_COLS, :], jnp.zeros((112, D), f32)],
                            axis=0).T
    hi = small.astype(bf16)
    ws_ref[:, 0:128] = hi
    ws_ref[:, 128:256] = (small - hi.astype(f32)).astype(bf16)


def _pack_w_in(w_in):
    full = lambda shape: pl.BlockSpec(shape, lambda i: (0,) * len(shape))
    return pl.pallas_call(
        _pack_w_in_kernel,
        grid=(1,),
        in_specs=[full((W_IN_COLS, D))],
        out_specs=[full((D, 3072)), full((D, 256))],
        out_shape=[jax.ShapeDtypeStruct((D, 3072), bf16), jax.ShapeDtypeStruct((D, 256), bf16)],
        compiler_params=pltpu.CompilerParams(dimension_semantics=("arbitrary",), vmem_limit_bytes=VMEM_LIMIT),
        name="pack_w_in",
    )(w_in.T)


def _inproj_kernel(x_ref, sc_ref, sh_ref, wm_ref, ws_ref, cw_ref, cb_ref, zx_ref, bc_ref, qkv_ref, dtf_ref, xcat, *, tm):
    i = pl.program_id(1)

    @pl.when(i == 0)
    def _init():
        xcat[0:8, :] = jnp.zeros((8, 2 * D_SSM), f32)

    u = x_ref[0] * (1.0 + sc_ref[0, 0]) + sh_ref[0, 0]
    ub = u.astype(bf16)
    xcat[8:8 + tm, 0:512] = _dot(ub, wm_ref[:, 512:1024])
    xcat[8:8 + tm, 512:1024] = _dot(ub, wm_ref[:, 1024:1536])
    zx_ref[0, :, 0:512] = _dot(ub, wm_ref[:, 0:512])
    acc = cw_ref[0:1, :] * xcat[5:5 + tm, :] + cb_ref[...]
    for k in range(1, CONV_K):
        acc = acc + cw_ref[k:k + 1, :] * xcat[5 + k:5 + k + tm, :]
    xcat[0:8, :] = xcat[tm:tm + 8, :]
    xbc = _silu(acc)
    zx_ref[0, :, 512:1024] = xbc[:, 0:512]
    bc_ref[0] = xbc[:, 512:1024].astype(bf16)
    for j, scale in enumerate((QK_SCALE, 1.0, 1.0)):
        qkv_ref[0, :, j * 512:(j + 1) * 512] = (
            _dot(ub, wm_ref[:, 1536 + j * 512:1536 + (j + 1) * 512]) * scale).astype(bf16)
    ul = (u - ub.astype(f32)).astype(bf16)
    d_hl = _dot(ub, ws_ref[...])
    dtf_ref[0] = (d_hl[:, 0:128] + _dot(ul, ws_ref[:, 0:128])) + d_hl[:, 128:256]


def _inproj(x, mod4, w_main, ws, conv_w, conv_b):
    bsz, t, _ = x.shape
    tm = min(INPROJ_ROWS, t)
    vec = lambda k: pl.BlockSpec((1, 1, 1, D), lambda b, i, k=k: (k, b, 0, 0))
    full = lambda shape: pl.BlockSpec(shape, lambda b, i: (0,) * len(shape))
    rows = lambda w: pl.BlockSpec((1, tm, w), lambda b, i: (b, i, 0))
    return pl.pallas_call(
        functools.partial(_inproj_kernel, tm=tm),
        grid=(bsz, t // tm),
        in_specs=[rows(D), vec(1), vec(0), full((D, 3072)), full((D, 256)),
                  full((CONV_K, 2 * D_SSM)), full((1, 2 * D_SSM))],
        out_specs=[rows(1024), rows(512), rows(1536), rows(128)],
        out_shape=[jax.ShapeDtypeStruct((bsz, t, 1024), f32),
                   jax.ShapeDtypeStruct((bsz, t, 512), bf16),
                   jax.ShapeDtypeStruct((bsz, t, 1536), bf16),
                   jax.ShapeDtypeStruct((bsz, t, 128), f32)],
        scratch_shapes=[pltpu.VMEM((tm + 8, 2 * D_SSM), f32)],
        compiler_params=pltpu.CompilerParams(dimension_semantics=("parallel", "arbitrary"),
                                             vmem_limit_bytes=VMEM_LIMIT),
        name="inproj",
    )(x, mod4, mod4, w_main, ws, conv_w, conv_b)


def _ssd_kernel(z_ref, xs_ref, bc_ref, dtf_ref, pc_ref, pe_ref, y_ref, cumc_ref, state, carry, *, lc):
    j = pl.program_id(1)

    @pl.when(j == 0)
    def _init():
        state[...] = jnp.zeros_like(state)
        carry[...] = jnp.zeros_like(carry)

    xs = xs_ref[0]
    bm = bc_ref[0, :, 0:256]
    cm = bc_ref[0, :, 256:512]

    dtf = dtf_ref[0]
    lane = lax.broadcasted_iota(i32, (lc, 128), 1)
    dt_c = _softplus(dtf + pc_ref[0:1, :])
    a_c = dt_c * (-jnp.exp(pc_ref[1:2, :]))
    logf = -_softplus(-(dtf + pc_ref[2:3, :]))
    v = jnp.where(lane < 8, a_c, logf) * LOG2E
    r_i = lax.broadcasted_iota(i32, (lc, lc), 0)
    c_i = lax.broadcasted_iota(i32, (lc, lc), 1)
    tri = r_i >= c_i
    tri_b = jnp.where(tri, 1.0, 0.0).astype(bf16)
    cum = _dot_exact_lhs(tri_b, v) + carry[...]
    carry[...] = jnp.where(lane[0:1, :] >= 8, cum[lc - 1:lc, :], 0.0)
    cumc_ref[0] = cum
    cs_t = cum.T[0:8, :]

    e_r = lax.broadcasted_iota(i32, (128, D_SSM), 0)
    e_c = lax.broadcasted_iota(i32, (128, D_SSM), 1)
    expand = jnp.where(jnp.right_shift(e_c, 6) == e_r, 1.0, 0.0).astype(bf16)
    dt_e = _dot_exact_rhs(dt_c, expand)
    cs_e = _dot_exact_rhs(cum, expand)

    xdt = xs * dt_e
    ecs = jnp.exp2(cs_e)
    cs_last = cs_e[lc - 1:lc, :]
    dec_st = jnp.exp2(cs_last - cs_e)
    lane_g = lax.broadcasted_iota(i32, (1, GROUP_W), 1)
    ys = []
    for g in range(2):
        gs = slice(g * GROUP_W, (g + 1) * GROUP_W)
        bg = bm[:, g * N_STATE:(g + 1) * N_STATE]
        cg = cm[:, g * N_STATE:(g + 1) * N_STATE]
        cb = _dot_nt(cg, bg)
        xdt_g = xdt[:, gs]
        xdt_gb = xdt_g.astype(bf16)
        ms, xb = [], []
        for hh in range(4):
            h = g * 4 + hh
            lm = jnp.exp2(jnp.where(tri, cum[:, h:h + 1] - cs_t[h:h + 1, :], -jnp.inf))
            ms.append((cb * lm).astype(bf16))
            xb.append(jnp.where(jnp.right_shift(lane_g, 6) == hh, xdt_gb, jnp.zeros_like(xdt_gb)))
        y_diag = _dot(jnp.concatenate(ms, axis=1), jnp.concatenate(xb, axis=0))
        st = state[g]
        y_off = _dot(cg, st.astype(bf16)) * ecs[:, gs]
        upd = _dot_tn(bg, (xdt_g * dec_st[:, gs]).astype(bf16))
        state[g] = st * jnp.exp2(cs_last[:, gs]) + upd
        ys.append(y_diag + y_off + xs[:, gs] * pe_ref[2:3, gs])

    outs = []
    for g in range(2):
        gs = slice(g * GROUP_W, (g + 1) * GROUP_W)
        yg = ys[g] * _silu(z_ref[0, :, gs])
        ms_ = jnp.mean(yg * yg, axis=-1, keepdims=True)
        outs.append(yg * lax.rsqrt(ms_ + EPS))
    y_ref[0] = (jnp.concatenate(outs, axis=1) * pe_ref[3:4, :]).astype(bf16)


def _ssd(zx, bc, dtf, pc, pe):
    bsz, t, _ = zx.shape
    lc = min(SSD_CHUNK, t)
    col = lambda k: pl.BlockSpec((1, lc, 512), lambda b, j, k=k: (b, j, k))
    full = lambda shape: pl.BlockSpec(shape, lambda b, j: (0,) * len(shape))
    return pl.pallas_call(
        functools.partial(_ssd_kernel, lc=lc),
        grid=(bsz, t // lc),
        in_specs=[col(0), col(1), col(0),
                  pl.BlockSpec((1, lc, 128), lambda b, j: (b, j, 0)),
                  full((8, 128)), full((8, D_SSM))],
        out_specs=[pl.BlockSpec((1, lc, D_SSM), lambda b, j: (b, j, 0)),
                   pl.BlockSpec((1, lc, 128), lambda b, j: (b, j, 0))],
        out_shape=[jax.ShapeDtypeStruct((bsz, t, D_SSM), bf16),
                   jax.ShapeDtypeStruct((bsz, t, 128), f32)],
        scratch_shapes=[pltpu.VMEM((2, N_STATE, GROUP_W), f32),
                        pltpu.VMEM((1, 128), f32)],
        compiler_params=pltpu.CompilerParams(dimension_semantics=("parallel", "arbitrary"),
                                             vmem_limit_bytes=VMEM_LIMIT),
        name="ssd",
    )(zx, zx, bc, dtf, pc, pe)


def _attn_kernel(q_ref, k_ref, v_ref, cc_ref, psel_ref, ng_ref, o_ref, kaug, vt, acc, sc0, sc1, *, tq, t):
    i = pl.program_id(1)
    nkb = t // tq
    n_heads = D_ATT // HEAD_DIM
    lane = lax.broadcasted_iota(i32, (1, 128), 1)
    lo_half = lane < HEAD_DIM

    @pl.when(i == 0)
    def _build():
        ones_rows = jnp.where(lax.broadcasted_iota(i32, (V_ROWS - HEAD_DIM, tq), 0) == 0, 1.0, 0.0).astype(bf16)
        for jb in range(nkb):
            rows = slice(jb * tq, (jb + 1) * tq)
            v_t = v_ref[0, rows, :].astype(f32).T.astype(bf16)
            for h in range(n_heads):
                vt[jb, h * V_ROWS:h * V_ROWS + HEAD_DIM, :] = v_t[h * HEAD_DIM:(h + 1) * HEAD_DIM, :]
                vt[jb, h * V_ROWS + HEAD_DIM:(h + 1) * V_ROWS, :] = ones_rows
            pieces = jnp.concatenate(_split3(-cc_ref[0, rows, :]), axis=1)
            spare = _dot(pieces, psel_ref[...]).astype(bf16)
            for p in range(n_heads // 2):
                a = spare[:, p * 128:(p + 1) * 128]
                kp = k_ref[0, rows, p * 128:(p + 1) * 128]
                kaug[2 * p, rows, :] = jnp.where(lo_half, kp, a)
                kaug[2 * p + 1, rows, :] = jnp.where(lo_half, a, kp)

    ones_hi = jnp.where((lane >= HEAD_DIM) & (lane < HEAD_DIM + 3), 1.0, 0.0).astype(bf16)
    ones_lo = jnp.where(lane < 3, 1.0, 0.0).astype(bf16)
    qa = []
    for p in range(n_heads // 2):
        qp = q_ref[0, :, p * 128:(p + 1) * 128]
        qa.append(jnp.where(lo_half, qp, ones_hi))
        qa.append(jnp.where(lo_half, ones_lo, qp))
    keep = lax.broadcasted_iota(i32, (tq, tq), 0) <= lax.broadcasted_iota(i32, (tq, tq), 1)
    acc[...] = jnp.zeros_like(acc)

    def score(jb, buf):
        k0 = pl.multiple_of(jb * tq, tq)
        for h in range(n_heads):
            buf[h] = _dot_nt(kaug[h, pl.ds(k0, tq), :], qa[h])

    def absorb(jb, ms, masked, buf):
        new_ms = []
        for h in range(n_heads):
            s = buf[h]
            if masked:
                s = jnp.where(keep, s, NEG)
            m_new = jnp.maximum(ms[h], jnp.max(s, axis=0, keepdims=True))
            alpha = jnp.exp2(ms[h] - m_new)
            p = jnp.exp2(s - m_new).astype(bf16)
            new_ms.append(m_new)
            acc[h] = acc[h] * alpha + _dot(vt[jb, h * V_ROWS:(h + 1) * V_ROWS, :], p)
        return tuple(new_ms)

    def pair(pp, ms):
        j0 = 2 * pp
        score(j0 + 1, sc1)
        ms = absorb(j0, ms, False, sc0)
        score(j0 + 2, sc0)
        return absorb(j0 + 1, ms, False, sc1)

    score(0, sc0)
    ms = lax.fori_loop(0, i // 2, pair, tuple(jnp.full((1, tq), NEG, f32) for _ in range(n_heads)))

    @pl.when(i % 2 == 0)
    def _even():
        absorb(i, ms, True, sc0)

    @pl.when(i % 2 == 1)
    def _odd():
        score(i, sc1)
        absorb(i, absorb(i - 1, ms, False, sc0), True, sc1)

    out_t = jnp.concatenate([acc[h, 0:HEAD_DIM, :] * (1.0 / acc[h, HEAD_DIM:HEAD_DIM + 1, :]) for h in range(n_heads)],
                            axis=0)
    ms_ = jnp.mean(out_t * out_t, axis=0, keepdims=True)
    out_t = out_t * lax.rsqrt(ms_ + EPS)
    o_ref[0] = (out_t.T * ng_ref[...]).astype(bf16)


def _piece_select():
    sel = np.zeros((384, 4 * 128), np.float32)
    for pair in range(4):
        for j in range(3):
            sel[j * 128 + 8 + 2 * pair, pair * 128 + HEAD_DIM + j] = 1.0
            sel[j * 128 + 8 + 2 * pair + 1, pair * 128 + j] = 1.0
    return jnp.asarray(sel, bf16)


def _attn(qkv, cumc, norm_g):
    bsz, t, _ = qkv.shape
    tq = min(ATT_BLOCK, t)
    return pl.pallas_call(
        functools.partial(_attn_kernel, tq=tq, t=t),
        grid=(bsz, t // tq),
        in_specs=[pl.BlockSpec((1, tq, D_ATT), lambda b, i: (b, i, 0)),
                  pl.BlockSpec((1, t, D_ATT), lambda b, i: (b, 0, 1)),
                  pl.BlockSpec((1, t, D_ATT), lambda b, i: (b, 0, 2)),
                  pl.BlockSpec((1, t, 128), lambda b, i: (b, 0, 0)),
                  pl.BlockSpec((384, 512), lambda b, i: (0, 0)),
                  pl.BlockSpec((1, D_ATT), lambda b, i: (0, 0))],
        out_specs=pl.BlockSpec((1, tq, D_ATT), lambda b, i: (b, i, 0)),
        out_shape=jax.ShapeDtypeStruct((bsz, t, D_ATT), bf16),
        scratch_shapes=[pltpu.VMEM((D_ATT // HEAD_DIM, t, 128), bf16),
                        pltpu.VMEM((t // tq, (D_ATT // HEAD_DIM) * V_ROWS, tq), bf16),
                        pltpu.VMEM((D_ATT // HEAD_DIM, V_ROWS, tq), f32),
                        pltpu.VMEM((D_ATT // HEAD_DIM, tq, tq), f32),
                        pltpu.VMEM((D_ATT // HEAD_DIM, tq, tq), f32)],
        compiler_params=pltpu.CompilerParams(dimension_semantics=("parallel", "arbitrary"),
                                             vmem_limit_bytes=VMEM_LIMIT),
        name="attn",
    )(qkv, qkv, qkv, cumc, _piece_select(), norm_g)


def _layer_norm(y, g, b):
    mu = jnp.mean(y, axis=-1, keepdims=True)
    yc = y - mu
    var = jnp.mean(yc * yc, axis=-1, keepdims=True)
    return yc * lax.rsqrt(var + EPS) * g + b


def _outproj_kernel(ys_ref, ya_ref, x_ref, gt_ref, sc_ref, sh_ref, wo_ref, lng_ref, lnb_ref, wrh_ref, wrl_ref, rb_ref,
                    x1_ref, rt_ref, dest_ref, cnt_ref, tbl_ref, xrows_hbm,
                    carry, galloc, tbl, u2t, didx_v, didx_s, cnt_v, cnt_s, zeros, idx_sem, sc_sem, z_sem,
                    *, tm, blk, n_blk):
    i = pl.program_id(0)
    last = pl.num_programs(0) - 1
    cur = i % 2
    prev = 1 - cur

    def idx_ready(slot):
        return pltpu.make_async_copy(didx_v.at[slot], didx_s.at[slot], idx_sem.at[slot])

    def dispatched(slot):
        return pltpu.make_async_copy(u2t.at[slot], u2t.at[slot], sc_sem.at[slot])

    def dispatch_copy(slot, r, k):
        return pltpu.make_async_copy(u2t.at[slot, pl.ds(r * PACK_TILE, PACK_TILE)],
                                     xrows_hbm.at[pl.ds(didx_s[slot, k, r] * PACK_TILE, PACK_TILE)], sc_sem.at[slot])

    @pl.when(i == 0)
    def _init():
        carry[...] = jnp.zeros_like(carry)
        galloc[...] = jnp.zeros_like(galloc)
        tbl[...] = jnp.zeros_like(tbl)

    @pl.when(i >= 2)
    def _reuse():
        dispatched(cur).wait()
        dispatched(cur).wait()

    def step(dispatch_prev):
        n_parts = OUTPROJ_PARTS
        n_slices = 3 * n_parts + 2

        def dispatch_slice(c):
            if dispatch_prev:
                for r_ in range(c * tm // n_slices, (c + 1) * tm // n_slices):
                    dispatch_copy(prev, r_, 0).start(priority=0)
                    dispatch_copy(prev, r_, 1).start(priority=1)

        if dispatch_prev:
            idx_ready(prev).wait()
        part = tm // n_parts
        hs = []
        for a in range(n_parts):
            rs = slice(a * part, (a + 1) * part)
            dispatch_slice(a)
            hs.append(_dot(ys_ref[rs, :], wo_ref[0:D_SSM, :]) + _dot(ya_ref[rs, :], wo_ref[D_SSM:D, :]))
        logit_parts = []
        for a in range(n_parts):
            rs = slice(a * part, (a + 1) * part)
            dispatch_slice(n_parts + 2 * a)
            x1 = _layer_norm(ALPHA * x_ref[rs, :] + (1.0 + gt_ref[0, 0]) * hs[a], lng_ref[...], lnb_ref[...])
            x1_ref[rs, :] = x1
            u2 = x1 * (1.0 + sc_ref[0, 0]) + sh_ref[0, 0]
            uh = u2.astype(bf16)
            _store_packed_rows(u2t.at[cur, pl.ds(a * part * PACK_TILE, part * PACK_TILE)], _pack_rows(uh))
            dispatch_slice(n_parts + 2 * a + 1)
            ul = (u2 - uh.astype(f32)).astype(bf16)
            logit_parts.append((_dot(uh, wrh_ref[...]) + _dot(ul, wrh_ref[...])) + _dot(uh, wrl_ref[...]))
        logits = jnp.concatenate(logit_parts, axis=0) + rb_ref[...]
        lane = lax.broadcasted_iota(i32, (tm, 128), 1).astype(f32)
        big = jnp.float32(1e9)

        def first_max(vals):
            m = jnp.max(vals, axis=-1, keepdims=True)
            return m, jnp.min(jnp.where(vals == m, lane, big), axis=-1, keepdims=True)

        gl = jnp.where(lane < N_GROUPS_R, logits, NEG)
        gmax, gidx = first_max(gl)
        g_p = 1.0 / jnp.sum(jnp.exp(gl - gmax), axis=-1, keepdims=True)
        lo = N_GROUPS_R + EXPERTS_PER_GROUP * gidx
        el = jnp.where((lane >= lo) & (lane < lo + EXPERTS_PER_GROUP), logits, NEG)
        m1, i1 = first_max(el)
        el2 = jnp.where(lane == i1, NEG, el)
        m2, i2 = first_max(el2)
        r = jnp.exp(m2 - m1)
        w1 = g_p / (1.0 + r)
        w2 = g_p * r / (1.0 + r)
        dispatch_slice(3 * n_parts)

        oh1 = lane == i1
        oh2 = lane == i2
        oh = jnp.where(oh1 | oh2, 1.0, 0.0)
        r_i = lax.broadcasted_iota(i32, (tm, tm), 0)
        c_i = lax.broadcasted_iota(i32, (tm, tm), 1)
        lower = jnp.where(r_i > c_i, 1.0, 0.0).astype(bf16)
        c_old = carry[...]
        prefix = _dot(lower, oh.astype(bf16)) + c_old
        rank1 = jnp.sum(jnp.where(oh1, prefix, 0.0), axis=-1, keepdims=True)
        rank2 = jnp.sum(jnp.where(oh2, prefix, 0.0), axis=-1, keepdims=True)
        c_new = c_old + jnp.sum(oh, axis=0, keepdims=True)
        carry[...] = c_new
        dispatch_slice(3 * n_parts + 1)

        nb_old = jnp.floor((c_old + (blk - 1)) * (1.0 / blk))
        nb_new = jnp.floor((c_new + (blk - 1)) * (1.0 / blk))
        fresh = nb_new - nb_old
        sq_r = lax.broadcasted_iota(i32, (128, 128), 0)
        sq_c = lax.broadcasted_iota(i32, (128, 128), 1)
        before = jnp.where(sq_r < sq_c, 1.0, 0.0).astype(bf16)
        base = galloc[...] + _dot(jnp.broadcast_to(fresh, (8, 128)).astype(bf16), before)[0:1, :]
        galloc[...] = galloc[...] + jnp.sum(fresh, axis=-1, keepdims=True)
        ordinal = sq_r.astype(f32)
        tbl[...] = jnp.where((ordinal >= nb_old) & (ordinal < nb_new), base + (ordinal - nb_old), tbl[...])
        tbl_ref[...] = tbl[...]
        cnt_ref[...] = jnp.concatenate([c_new, galloc[...], jnp.zeros((6, 128), f32)], axis=0)

        rt = jnp.where(lane == 4, w1, jnp.where(lane == 5, w2, 0.0))
        rt_ref[...] = rt

        ids_bf = tbl[...].astype(bf16)

        def row_of(onehot, rank):
            nth = jnp.floor(rank * (1.0 / blk))
            ids = _dot(jnp.where(lane == nth, 1.0, 0.0).astype(bf16), ids_bf)
            return jnp.sum(jnp.where(onehot, ids, 0.0), axis=-1, keepdims=True) * blk + (rank - nth * blk)

        d1 = row_of(oh1, rank1)
        d2 = row_of(oh2, rank2)
        dd = jnp.where(lane == 0, d1, jnp.where(lane == 1, d2, 0.0)).T[0:8, :].astype(i32)
        dest_ref[...] = dd
        didx_v[cur] = dd
        idx_ready(cur).start()

    pl.when(i == 0)(lambda: step(False))
    pl.when(i > 0)(lambda: step(True))

    @pl.when(i == last)
    def _epilogue():
        idx_ready(cur).wait()

        def issue(r_, c):
            dispatch_copy(cur, r_, 0).start()
            dispatch_copy(cur, r_, 1).start()
            return c
        lax.fori_loop(0, tm, issue, 0, unroll=8)

        cnt = carry[...]
        n_blocks = jnp.floor((cnt + (blk - 1)) * (1.0 / blk))
        ordinal = lax.broadcasted_iota(i32, (128, 128), 0).astype(f32)
        last_id = jnp.sum(jnp.where(ordinal == n_blocks - 1.0, tbl[...], 0.0), axis=0, keepdims=True)
        used = cnt - (n_blocks - 1.0) * blk
        cnt_v[...] = jnp.concatenate([last_id * blk + used, blk - used, galloc[...], jnp.zeros((5, 128), f32)],
                                     axis=0).astype(i32)
        to_smem = pltpu.make_async_copy(cnt_v, cnt_s, z_sem)
        to_smem.start()
        to_smem.wait()
        zeros[...] = jnp.zeros_like(zeros)
        sizes = [1 << b for b in reversed(range(blk.bit_length() - 1))]

        def for_each_fill(fn):
            def tail(e, c):
                pad = cnt_s[1, N_GROUPS_R + e]
                off = cnt_s[0, N_GROUPS_R + e]
                for sz in sizes:
                    @pl.when((pad & sz) != 0)
                    def _(off=off, sz=sz):
                        fn(pltpu.make_async_copy(zeros.at[pl.ds(0, sz * PACK_TILE)],
                                                 xrows_hbm.at[pl.ds(off * PACK_TILE, sz * PACK_TILE)], z_sem))
                    off = off + (pad & sz)
                return c
            lax.fori_loop(0, N_EXPERTS, tail, 0)

            def whole(b, c):
                fn(pltpu.make_async_copy(zeros, xrows_hbm.at[pl.ds(b * blk * PACK_TILE, blk * PACK_TILE)], z_sem))
                return c
            lax.fori_loop(cnt_s[2, 0], n_blk, whole, 0)

        for_each_fill(lambda copy: copy.start())
        for_each_fill(lambda copy: copy.wait())

        @pl.when(i > 0)
        def _prev_done():
            dispatched(prev).wait()
            dispatched(prev).wait()
        dispatched(cur).wait()
        dispatched(cur).wait()


def _outproj(y_ssm, y_att, x, mod4, w_out, ln_g, ln_b, wr_hi, wr_lo, rb, t, blk, n_blk):
    n = x.shape[0]
    tm = min(OUTPROJ_ROWS, t)
    nt = t // tm
    assert n // blk + 1 <= 128 and n_blk <= 256, "block-id table: 128 blocks per expert, ids exact in bf16"
    vec = lambda k: pl.BlockSpec((1, 1, 1, D), lambda i, k=k: (k, i // nt, 0, 0))
    full = lambda shape: pl.BlockSpec(shape, lambda i: (0,) * len(shape))
    rows = lambda w: pl.BlockSpec((tm, w), lambda i: (i, 0))
    return pl.pallas_call(
        functools.partial(_outproj_kernel, tm=tm, blk=blk, n_blk=n_blk),
        grid=(n // tm,),
        in_specs=[rows(D_SSM), rows(D_ATT), rows(D), vec(2), vec(4), vec(3),
                  full((D, D)), full((1, D)), full((1, D)), full((D, 128)), full((D, 128)), full((1, 128))],
        out_specs=[rows(D), rows(128), pl.BlockSpec((8, tm), lambda i: (0, i)), full((8, 128)), full((128, 128)),
                   pl.BlockSpec(memory_space=pl.ANY)],
        out_shape=[jax.ShapeDtypeStruct((n, D), f32), jax.ShapeDtypeStruct((n, 128), f32),
                   jax.ShapeDtypeStruct((8, n), i32), jax.ShapeDtypeStruct((8, 128), f32),
                   jax.ShapeDtypeStruct((128, 128), f32),
                   jax.ShapeDtypeStruct((n_blk * blk * PACK_TILE, 128), u32)],
        scratch_shapes=[pltpu.VMEM((1, 128), f32), pltpu.VMEM((1, 128), f32), pltpu.VMEM((128, 128), f32),
                        pltpu.VMEM((2, tm * PACK_TILE, 128), u32),
                        pltpu.VMEM((2, 8, tm), i32), pltpu.SMEM((2, 8, tm), i32),
                        pltpu.VMEM((8, 128), i32), pltpu.SMEM((8, 128), i32),
                        pltpu.VMEM((blk * PACK_TILE, 128), u32),
                        pltpu.SemaphoreType.DMA((2,)), pltpu.SemaphoreType.DMA((2,)), pltpu.SemaphoreType.DMA(())],
        compiler_params=pltpu.CompilerParams(dimension_semantics=("arbitrary",), vmem_limit_bytes=VMEM_LIMIT),
        name="outproj",
    )(y_ssm, y_att, x, mod4, mod4, mod4, w_out, ln_g, ln_b, wr_hi, wr_lo, rb)


def _tile_copy(src_hbm, row, buf, slot, sem):
    return pltpu.make_async_copy(src_hbm.at[row], buf.at[pl.ds(slot * ROW_TILE, ROW_TILE)], sem)


def _moe_kernel(be_ref, br_ref, nu_ref, nxt_ref, par_ref, nv_ref, x_ref, wg_hbm, wu_hbm, wd_hbm, y_ref,
                wfg, wfu, wfd, wgb, wub, wdb, wsem, *, blk):
    i = pl.program_id(0)
    used = i < nu_ref[0]

    def fetch(e, slot):
        return [pltpu.make_async_copy(src.at[e], dst.at[slot], wsem.at[slot])
                for src, dst in ((wg_hbm, wfg), (wu_hbm, wfu), (wd_hbm, wfd))]

    @pl.when(i == 0)
    def _first():
        for c in fetch(be_ref[0], par_ref[0]):
            c.start()

    @pl.when(jnp.logical_and(used, jnp.logical_or(i == 0, be_ref[i] != be_ref[jnp.maximum(i - 1, 0)])))
    def _switch():
        slot = par_ref[i]
        for c in fetch(be_ref[i], slot):
            c.wait()

        @pl.when(nxt_ref[i] >= 0)
        def _next():
            for c in fetch(nxt_ref[i], 1 - slot):
                c.start(priority=1)
        wgb[...] = wfg[slot].astype(bf16)
        wub[...] = wfu[slot].astype(bf16)
        wdb[...] = wfd[slot].astype(bf16)

    @pl.when(jnp.logical_not(used))
    def _spare():
        y_ref[...] = jnp.zeros_like(y_ref)

    def mlp(rows):
        xa, xb = _load_packed_rows(x_ref.at[pl.ds(0, rows * PACK_TILE)], rows)
        half = D // 2
        gate = _dot(xa, wgb[0:half, :]) + _dot(xb, wgb[half:D, :])
        up = _dot(xa, wub[0:half, :]) + _dot(xb, wub[half:D, :])
        hid = (_silu(gate) * up).astype(bf16)
        _store_row_tiles(y_ref.at[pl.ds(0, rows * ROW_TILE)], _dot(hid, wdb[...]))

    sparse = nv_ref[i] <= blk // 2

    @pl.when(jnp.logical_and(used, jnp.logical_not(sparse)))
    def _full():
        mlp(blk)

    @pl.when(jnp.logical_and(used, sparse))
    def _half():
        mlp(blk // 2)
        y_ref[pl.ds(blk // 2 * ROW_TILE, blk // 2 * ROW_TILE), :] = jnp.zeros((blk // 2 * ROW_TILE, 128), f32)


def _moe(block_e, block_row, n_used, next_e, parity, n_valid, x_rows, w_gate, w_up, w_down, blk):
    nblk = block_e.shape[0]
    rspec = pl.BlockSpec((blk * ROW_TILE, 128), lambda i, be, br, *_: (br[i], 0))
    xspec = pl.BlockSpec((blk * PACK_TILE, 128), lambda i, be, br, nu, *_: (br[jnp.minimum(i, nu[0] - 1)], 0))
    anyspec = pl.BlockSpec(memory_space=pl.ANY)
    grid_spec = pltpu.PrefetchScalarGridSpec(
        num_scalar_prefetch=6,
        grid=(nblk,),
        in_specs=[xspec, anyspec, anyspec, anyspec],
        out_specs=rspec,
        scratch_shapes=[pltpu.VMEM((2, D, D_EXPERT), f32), pltpu.VMEM((2, D, D_EXPERT), f32),
                        pltpu.VMEM((2, D_EXPERT, D), f32),
                        pltpu.VMEM((D, D_EXPERT), bf16), pltpu.VMEM((D, D_EXPERT), bf16),
                        pltpu.VMEM((D_EXPERT, D), bf16),
                        pltpu.SemaphoreType.DMA((2,))],
    )
    return pl.pallas_call(
        functools.partial(_moe_kernel, blk=blk),
        grid_spec=grid_spec,
        out_shape=jax.ShapeDtypeStruct((nblk * blk * ROW_TILE, 128), f32),
        compiler_params=pltpu.CompilerParams(dimension_semantics=("arbitrary",), vmem_limit_bytes=VMEM_LIMIT),
        name="moe",
    )(block_e, block_row, n_used, next_e, parity, n_valid, x_rows, w_gate, w_up, w_down)


def _combine_kernel(d1_ref, d2_ref, y_hbm, x1_ref, rt_ref, gt_ref, lng_ref, lnb_ref, o_ref, buf, sem, *, tm):
    i = pl.program_id(0)
    last = pl.num_programs(0) - 1

    @pl.when(i == 0)
    def _prologue():
        for s in range(2):
            def issue(r, c, s=s):
                base = jnp.minimum(s, last) * tm
                _tile_copy(y_hbm, d1_ref[base + r], buf.at[s, 0], r, sem.at[s]).start()
                _tile_copy(y_hbm, d2_ref[base + r], buf.at[s, 1], r, sem.at[s]).start()
                return c
            lax.fori_loop(0, tm, issue, 0)

    def rows_ready(slot):
        return pltpu.make_async_copy(buf.at[slot], buf.at[slot], sem.at[slot])

    slot = i % 3
    nxt = (i + 2) % 3
    base = jnp.minimum(i + 2, last) * tm
    rows_ready(slot).wait()
    for r in range(tm):
        _tile_copy(y_hbm, d1_ref[base + r], buf.at[nxt, 0], r, sem.at[nxt]).start(priority=0)
        _tile_copy(y_hbm, d2_ref[base + r], buf.at[nxt, 1], r, sem.at[nxt]).start(priority=1)
    rt = rt_ref[...]
    moe = rt[:, 4:5] * _load_row_tiles(buf.at[slot, 0], tm) + rt[:, 5:6] * _load_row_tiles(buf.at[slot, 1], tm)
    y = ALPHA * x1_ref[...] + (1.0 + gt_ref[0, 0]) * moe
    o_ref[...] = _layer_norm(y, lng_ref[...], lnb_ref[...])

    @pl.when(i == last)
    def _drain():
        rows_ready((i + 1) % 3).wait()
        rows_ready((i + 2) % 3).wait()


def _combine(dest1, dest2, y_rows, x1, rt, mod4, ln_g, ln_b, t):
    n = x1.shape[0]
    tm = min(COMBINE_ROWS, t)
    nt = t // tm
    full = lambda shape: pl.BlockSpec(shape, lambda i, d1, d2: (0,) * len(shape))
    rows = lambda w: pl.BlockSpec((tm, w), lambda i, d1, d2: (i, 0))
    grid_spec = pltpu.PrefetchScalarGridSpec(
        num_scalar_prefetch=2,
        grid=(n // tm,),
        in_specs=[pl.BlockSpec(memory_space=pl.ANY), rows(D), rows(128),
                  pl.BlockSpec((1, 1, 1, D), lambda i, d1, d2: (5, i // nt, 0, 0)),
                  full((1, D)), full((1, D))],
        out_specs=rows(D),
        scratch_shapes=[pltpu.VMEM((3, 2, tm * ROW_TILE, 128), f32), pltpu.SemaphoreType.DMA((3,))],
    )
    return pl.pallas_call(
        functools.partial(_combine_kernel, tm=tm),
        grid_spec=grid_spec,
        out_shape=jax.ShapeDtypeStruct((n, D), f32),
        compiler_params=pltpu.CompilerParams(dimension_semantics=("arbitrary",), vmem_limit_bytes=VMEM_LIMIT),
        name="combine",
    )(dest1, dest2, y_rows, x1, rt, mod4, ln_g, ln_b)


def _hi_lo(w):
    hi = w.astype(bf16)
    return hi, (w - hi.astype(f32)).astype(bf16)


def _pad_lanes(v, offset, width):
    return jnp.zeros((width,), f32).at[offset:offset + v.shape[0]].set(v)


def _layer(x, c, ada_w, ada_b, w_in, conv_w, conv_b, dt_bias, a_log, d_skip, ssm_norm_g, fg_bias, att_norm_g,
           w_out, ln1_g, ln1_b, router_g_w, router_g_b, router_e_w, router_e_b, w_gate, w_up, w_down, ln2_g, ln2_b):
    bsz, t, _ = x.shape
    n = bsz * t

    mod4 = _ada(c, ada_w, ada_b).reshape(6, bsz, 1, D)

    w_main, w_small = _pack_w_in(w_in)
    zx, bc, qkv, dtf = _inproj(x, mod4, w_main, w_small, conv_w, conv_b.reshape(1, -1))

    pc = jnp.stack([_pad_lanes(dt_bias, 0, 128), _pad_lanes(a_log, 0, 128), _pad_lanes(fg_bias, 8, 128)]
                   + [jnp.zeros((128,), f32)] * 5)
    rep = lambda v: jnp.repeat(v, HEAD_DIM)
    pe = jnp.stack([rep(dt_bias), rep(a_log), rep(d_skip), ssm_norm_g] + [jnp.zeros((D_SSM,), f32)] * 4)
    y_ssm, cumc = _ssd(zx, bc, dtf, pc, pe)

    y_att = _attn(qkv, cumc, att_norm_g.reshape(1, -1))

    wr = jnp.concatenate([router_g_w, router_e_w, jnp.zeros((D, 128 - N_GROUPS_R - N_EXPERTS), f32)], axis=1)
    wr_hi, wr_lo = _hi_lo(wr)
    rb = jnp.concatenate([router_g_b, router_e_b, jnp.zeros((128 - N_GROUPS_R - N_EXPERTS,), f32)]).reshape(1, 128)
    blk = MOE_ROWS
    nblk = (2 * n) // blk + N_EXPERTS
    x1, rt, dest, cnt, tbl, x_rows = _outproj(y_ssm.reshape(n, D_SSM), y_att.reshape(n, D_ATT), x.reshape(n, D), mod4,
                                              w_out.astype(bf16), ln1_g.reshape(1, D), ln1_b.reshape(1, D),
                                              wr_hi, wr_lo, rb, t, blk, nblk)

    counts = cnt[0, N_GROUPS_R:N_GROUPS_R + N_EXPERTS].astype(i32)
    nb = (counts + blk - 1) // blk
    nb_end = jnp.cumsum(nb)
    n_used = nb_end[N_EXPERTS - 1:]
    every = jnp.arange(nblk, dtype=i32)
    step = jnp.minimum(every, n_used[0] - 1)
    owner = (nb_end[None, :] <= step[:, None]).astype(i32)
    block_e = jnp.sum(owner, axis=1)
    first = jnp.sum(owner * nb[None, :], axis=1)
    lanes = jnp.arange(128, dtype=i32)[None, :]
    nth_row = jnp.dot((lanes == (step - first)[:, None]).astype(f32), tbl, precision=lax.Precision.HIGHEST)
    block_id = jnp.sum(jnp.where(lanes == (block_e + N_GROUPS_R)[:, None], nth_row, 0.0), axis=1).astype(i32)
    block_row = jnp.where(every < n_used[0], block_id, every)
    run_end = first + jnp.sum((jnp.arange(N_EXPERTS)[None, :] == block_e[:, None]) * nb[None, :], axis=1)
    next_e = jnp.where(run_end < n_used[0], jnp.sum((nb_end[None, :] <= run_end[:, None]).astype(i32), axis=1), -1)
    parity = jnp.sum(((jnp.arange(N_EXPERTS)[None, :] < block_e[:, None]) & (nb[None, :] > 0)).astype(i32), axis=1) % 2

    own = (jnp.arange(N_EXPERTS)[None, :] == block_e[:, None]).astype(i32)
    n_valid = jnp.clip(jnp.sum(own * counts[None, :], axis=1) - (step - first) * blk, 0, blk)
    y_rows = _moe(block_e, block_row, n_used, next_e, parity, n_valid, x_rows,
                  w_gate, w_up, w_down, blk)
    out = _combine(dest[0], dest[1], y_rows.reshape(nblk * blk, ROW_TILE, 128), x1, rt, mod4,
                   ln2_g.reshape(1, D), ln2_b.reshape(1, D), t)
    return out.reshape(bsz, t, D)


def kernel(x, c, ada_w, ada_b, w_in, conv_w, conv_b, dt_bias, a_log, d_skip, ssm_norm_g, fg_bias, att_norm_g, w_out,
           ln1_g, ln1_b, router_g_w, router_g_b, router_e_w, router_e_b, w_gate, w_up, w_down, ln2_g, ln2_b):
    depth = ada_w.shape[0]
    for l in range(depth):
        x = _layer(x, c, ada_w[l], ada_b[l], w_in[l], conv_w[l], conv_b[l], dt_bias[l], a_log[l], d_skip[l],
                   ssm_norm_g[l], fg_bias[l], att_norm_g[l], w_out[l], ln1_g[l], ln1_b[l], router_g_w[l],
                   router_g_b[l], router_e_w[l], router_e_b[l], w_gate[l], w_up[l], w_down[l], ln2_g[l], ln2_b[l])
    return x
```

```python
import functools

import jax
import jax.numpy as jnp
import numpy as np
from jax import lax
from jax.experimental import pallas as pl
from jax.experimental.pallas import tpu as pltpu

f32 = jnp.float32
bf16 = jnp.bfloat16
i32 = jnp.int32

D = 1024
D_SSM = 512
D_ATT = 512
HEAD_DIM = 64
GROUP_W = 256
N_STATE = 128
CONV_K = 4
N_GROUPS_R = 4
EXPERTS_PER_GROUP = 8
N_EXPERTS = 32
D_EXPERT = 512
ALPHA = 2.0 ** 0.25
EPS = 1e-5
NEG = -1e30
LOG2E = 1.4426950408889634
QK_SCALE = HEAD_DIM ** -0.5 * LOG2E
V_ROWS = 80

SSD_CHUNK = 256
ATT_BLOCK = 256
INPROJ_ROWS = 512
OUTPROJ_ROWS = 1024
OUTPROJ_PARTS = 8
MOE_ROWS = 512
COMBINE_ROWS = 256
VMEM_LIMIT = 48 * 1024 * 1024


def _dot(a, b):
    return jnp.dot(a, b, preferred_element_type=f32)


def _dot_nt(a, b):
    return lax.dot_general(a, b, (((1,), (1,)), ((), ())), preferred_element_type=f32)


def _dot_tn(a, b):
    return lax.dot_general(a, b, (((0,), (0,)), ((), ())), preferred_element_type=f32)


def _split3(v):
    hi = v.astype(bf16)
    r1 = v - hi.astype(f32)
    mid = r1.astype(bf16)
    lo = (r1 - mid.astype(f32)).astype(bf16)
    return hi, mid, lo


def _dot_exact_lhs(m, v):
    hi, mid, lo = _split3(v)
    return (_dot(m, hi) + _dot(m, mid)) + _dot(m, lo)


def _dot_exact_rhs(v, m):
    hi, mid, lo = _split3(v)
    return (_dot(hi, m) + _dot(mid, m)) + _dot(lo, m)


ROW_TILE = 8


def _store_row_tiles(ref, val):
    rows = val.shape[0]
    for c in range(ROW_TILE):
        ref[pl.ds(c, rows, stride=ROW_TILE), :] = val[:, c * 128:(c + 1) * 128]


def _load_row_tiles(ref, rows):
    return jnp.concatenate([ref[pl.ds(c, rows, stride=ROW_TILE), :] for c in range(ROW_TILE)], axis=1)


PACK_TILE = 4
u32 = jnp.uint32


def _pack_rows(vb):
    lo = pltpu.bitcast(vb[:, 0:D // 2].astype(f32), u32) >> 16
    hi = pltpu.bitcast(vb[:, D // 2:D].astype(f32), u32) & jnp.uint32(0xFFFF0000)
    return lo | hi


def _store_packed_rows(ref, packed):
    rows = packed.shape[0]
    for c in range(PACK_TILE):
        ref[pl.ds(c, rows, stride=PACK_TILE), :] = packed[:, c * 128:(c + 1) * 128]


def _load_packed_rows(ref, rows):
    words = [ref[pl.ds(c, rows, stride=PACK_TILE), :] for c in range(PACK_TILE)]
    lo = jnp.concatenate([pltpu.bitcast(w << 16, f32) for w in words], axis=1)
    hi = jnp.concatenate([pltpu.bitcast(w & jnp.uint32(0xFFFF0000), f32) for w in words], axis=1)
    return lo.astype(bf16), hi.astype(bf16)


def _softplus(x):
    return jnp.maximum(x, 0.0) + jnp.log1p(jnp.exp(-jnp.abs(x)))


def _silu(x):
    return x * jax.nn.sigmoid(x)


def _ada_kernel(c_ref, w_ref, b_ref, o_ref):
    s = _silu(c_ref[...]).astype(bf16)
    o_ref[0] = _dot(s, w_ref[...].astype(bf16)) + b_ref[0]


def _ada(c, w, b):
    bsz = c.shape[0]
    return pl.pallas_call(
        _ada_kernel,
        grid=(6,),
        in_specs=[pl.BlockSpec((bsz, D), lambda j: (0, 0)),
                  pl.BlockSpec((D, D), lambda j: (0, j)),
                  pl.BlockSpec((1, 1, D), lambda j: (j, 0, 0))],
        out_specs=pl.BlockSpec((1, bsz, D), lambda j: (j, 0, 0)),
        out_shape=jax.ShapeDtypeStruct((6, bsz, D), f32),
        compiler_params=pltpu.CompilerParams(dimension_semantics=("arbitrary",), vmem_limit_bytes=VMEM_LIMIT),
        name="ada",
    )(c, w, b.reshape(6, 1, D))


W_IN_COLS = 3088
PACK_ROWS = 256


def _pack_w_in_kernel(wt_ref, wm_ref, ws_ref):
    for j in range(3072 // PACK_ROWS):
        src = j * PACK_ROWS if j * PACK_ROWS < 1536 else j * PACK_ROWS + 8
        wm_ref[:, j * PACK_ROWS:(j + 1) * PACK_ROWS] = wt_ref[src:src + PACK_ROWS, :].T.astype(bf16)
    small = jnp.concatenate([wt_ref[1536:1544, :], wt_ref[W_IN_COLS - 8:W_IN_COLS, :], jnp.zeros((112, D), f32)],
                            axis=0).T
    hi = small.astype(bf16)
    ws_ref[:, 0:128] = hi
    ws_ref[:, 128:256] = (small - hi.astype(f32)).astype(bf16)


def _pack_w_in(w_in):
    full = lambda shape: pl.BlockSpec(shape, lambda i: (0,) * len(shape))
    return pl.pallas_call(
        _pack_w_in_kernel,
        grid=(1,),
        in_specs=[full((W_IN_COLS, D))],
        out_specs=[full((D, 3072)), full((D, 256))],
        out_shape=[jax.ShapeDtypeStruct((D, 3072), bf16), jax.ShapeDtypeStruct((D, 256), bf16)],
        compiler_params=pltpu.CompilerParams(dimension_semantics=("arbitrary",), vmem_limit_bytes=VMEM_LIMIT),
        name="pack_w_in",
    )(w_in.T)


def _inproj_kernel(x_ref, sc_ref, sh_ref, wm_ref, ws_ref, cw_ref, cb_ref, zx_ref, bc_ref, qkv_ref, dtf_ref, xcat, *, tm):
    i = pl.program_id(1)

    @pl.when(i == 0)
    def _init():
        xcat[0:8, :] = jnp.zeros((8, 2 * D_SSM), f32)

    u = x_ref[0] * (1.0 + sc_ref[0, 0]) + sh_ref[0, 0]
    ub = u.astype(bf16)
    xcat[8:8 + tm, 0:512] = _dot(ub, wm_ref[:, 512:1024])
    xcat[8:8 + tm, 512:1024] = _dot(ub, wm_ref[:, 1024:1536])
    zx_ref[0, :, 0:512] = _dot(ub, wm_ref[:, 0:512])
    acc = cw_ref[0:1, :] * xcat[5:5 + tm, :] + cb_ref[...]
    for k in range(1, CONV_K):
        acc = acc + cw_ref[k:k + 1, :] * xcat[5 + k:5 + k + tm, :]
    xcat[0:8, :] = xcat[tm:tm + 8, :]
    xbc = _silu(acc)
    zx_ref[0, :, 512:1024] = xbc[:, 0:512]
    bc_ref[0] = xbc[:, 512:1024].astype(bf16)
    for j, scale in enumerate((QK_SCALE, 1.0, 1.0)):
        qkv_ref[0, :, j * 512:(j + 1) * 512] = (
            _dot(ub, wm_ref[:, 1536 + j * 512:1536 + (j + 1) * 512]) * scale).astype(bf16)
    ul = (u - ub.astype(f32)).astype(bf16)
    d_hl = _dot(ub, ws_ref[...])
    dtf_ref[0] = (d_hl[:, 0:128] + _dot(ul, ws_ref[:, 0:128])) + d_hl[:, 128:256]


def _inproj(x, mod4, w_main, ws, conv_w, conv_b):
    bsz, t, _ = x.shape
    tm = min(INPROJ_ROWS, t)
    vec = lambda k: pl.BlockSpec((1, 1, 1, D), lambda b, i, k=k: (k, b, 0, 0))
    full = lambda shape: pl.BlockSpec(shape, lambda b, i: (0,) * len(shape))
    rows = lambda w: pl.BlockSpec((1, tm, w), lambda b, i: (b, i, 0))
    return pl.pallas_call(
        functools.partial(_inproj_kernel, tm=tm),
        grid=(bsz, t // tm),
        in_specs=[rows(D), vec(1), vec(0), full((D, 3072)), full((D, 256)),
                  full((CONV_K, 2 * D_SSM)), full((1, 2 * D_SSM))],
        out_specs=[rows(1024), rows(512), rows(1536), rows(128)],
        out_shape=[jax.ShapeDtypeStruct((bsz, t, 1024), f32),
                   jax.ShapeDtypeStruct((bsz, t, 512), bf16),
                   jax.ShapeDtypeStruct((bsz, t, 1536), bf16),
                   jax.ShapeDtypeStruct((bsz, t, 128), f32)],
        scratch_shapes=[pltpu.VMEM((tm + 8, 2 * D_SSM), f32)],
        compiler_params=pltpu.CompilerParams(dimension_semantics=("parallel", "arbitrary"),
                                             vmem_limit_bytes=VMEM_LIMIT),
        name="inproj",
    )(x, mod4, mod4, w_main, ws, conv_w, conv_b)


def _ssd_kernel(z_ref, xs_ref, bc_ref, dtf_ref, pc_ref, pe_ref, y_ref, cumc_ref, state, carry, *, lc):
    j = pl.program_id(1)

    @pl.when(j == 0)
    def _init():
        state[...] = jnp.zeros_like(state)
        carry[...] = jnp.zeros_like(carry)

    xs = xs_ref[0]
    bm = bc_ref[0, :, 0:256]
    cm = bc_ref[0, :, 256:512]

    dtf = dtf_ref[0]
    lane = lax.broadcasted_iota(i32, (lc, 128), 1)
    dt_c = _softplus(dtf + pc_ref[0:1, :])
    a_c = dt_c * (-jnp.exp(pc_ref[1:2, :]))
    logf = -_softplus(-(dtf + pc_ref[2:3, :]))
    v = jnp.where(lane < 8, a_c, logf) * LOG2E
    r_i = lax.broadcasted_iota(i32, (lc, lc), 0)
    c_i = lax.broadcasted_iota(i32, (lc, lc), 1)
    tri = r_i >= c_i
    tri_b = jnp.where(tri, 1.0, 0.0).astype(bf16)
    cum = _dot_exact_lhs(tri_b, v) + carry[...]
    carry[...] = jnp.where(lane[0:1, :] >= 8, cum[lc - 1:lc, :], 0.0)
    cumc_ref[0] = cum
    cs_t = cum.T[0:8, :]

    e_r = lax.broadcasted_iota(i32, (128, D_SSM), 0)
    e_c = lax.broadcasted_iota(i32, (128, D_SSM), 1)
    expand = jnp.where(jnp.right_shift(e_c, 6) == e_r, 1.0, 0.0).astype(bf16)
    dt_e = _dot_exact_rhs(dt_c, expand)
    cs_e = _dot_exact_rhs(cum, expand)

    xdt = xs * dt_e
    ecs = jnp.exp2(cs_e)
    cs_last = cs_e[lc - 1:lc, :]
    dec_st = jnp.exp2(cs_last - cs_e)
    lane_g = lax.broadcasted_iota(i32, (1, GROUP_W), 1)
    ys = []
    for g in range(2):
        gs = slice(g * GROUP_W, (g + 1) * GROUP_W)
        bg = bm[:, g * N_STATE:(g + 1) * N_STATE]
        cg = cm[:, g * N_STATE:(g + 1) * N_STATE]
        cb = _dot_nt(cg, bg)
        xdt_g = xdt[:, gs]
        xdt_gb = xdt_g.astype(bf16)
        ms, xb = [], []
        for hh in range(4):
            h = g * 4 + hh
            lm = jnp.exp2(jnp.where(tri, cum[:, h:h + 1] - cs_t[h:h + 1, :], -jnp.inf))
            ms.append((cb * lm).astype(bf16))
            xb.append(jnp.where(jnp.right_shift(lane_g, 6) == hh, xdt_gb, jnp.zeros_like(xdt_gb)))
        y_diag = _dot(jnp.concatenate(ms, axis=1), jnp.concatenate(xb, axis=0))
        st = state[g]
        y_off = _dot(cg, st.astype(bf16)) * ecs[:, gs]
        upd = _dot_tn(bg, (xdt_g * dec_st[:, gs]).astype(bf16))
        state[g] = st * jnp.exp2(cs_last[:, gs]) + upd
        ys.append(y_diag + y_off + xs[:, gs] * pe_ref[2:3, gs])

    outs = []
    for g in range(2):
        gs = slice(g * GROUP_W, (g + 1) * GROUP_W)
        yg = ys[g] * _silu(z_ref[0, :, gs])
        ms_ = jnp.mean(yg * yg, axis=-1, keepdims=True)
        outs.append(yg * lax.rsqrt(ms_ + EPS))
    y_ref[0] = (jnp.concatenate(outs, axis=1) * pe_ref[3:4, :]).astype(bf16)


def _ssd(zx, bc, dtf, pc, pe):
    bsz, t, _ = zx.shape
    lc = min(SSD_CHUNK, t)
    col = lambda k: pl.BlockSpec((1, lc, 512), lambda b, j, k=k: (b, j, k))
    full = lambda shape: pl.BlockSpec(shape, lambda b, j: (0,) * len(shape))
    return pl.pallas_call(
        functools.partial(_ssd_kernel, lc=lc),
        grid=(bsz, t // lc),
        in_specs=[col(0), col(1), col(0),
                  pl.BlockSpec((1, lc, 128), lambda b, j: (b, j, 0)),
                  full((8, 128)), full((8, D_SSM))],
        out_specs=[pl.BlockSpec((1, lc, D_SSM), lambda b, j: (b, j, 0)),
                   pl.BlockSpec((1, lc, 128), lambda b, j: (b, j, 0))],
        out_shape=[jax.ShapeDtypeStruct((bsz, t, D_SSM), bf16),
                   jax.ShapeDtypeStruct((bsz, t, 128), f32)],
        scratch_shapes=[pltpu.VMEM((2, N_STATE, GROUP_W), f32),
                        pltpu.VMEM((1, 128), f32)],
        compiler_params=pltpu.CompilerParams(dimension_semantics=("parallel", "arbitrary"),
                                             vmem_limit_bytes=VMEM_LIMIT),
        name="ssd",
    )(zx, zx, bc, dtf, pc, pe)


def _attn_kernel(q_ref, k_ref, v_ref, cc_ref, psel_ref, ng_ref, o_ref, kaug, vt, acc, sc0, sc1, *, tq, t):
    i = pl.program_id(1)
    nkb = t // tq
    n_heads = D_ATT // HEAD_DIM
    lane = lax.broadcasted_iota(i32, (1, 128), 1)
    lo_half = lane < HEAD_DIM

    @pl.when(i == 0)
    def _build():
        ones_rows = jnp.where(lax.broadcasted_iota(i32, (V_ROWS - HEAD_DIM, tq), 0) == 0, 1.0, 0.0).astype(bf16)
        for jb in range(nkb):
            rows = slice(jb * tq, (jb + 1) * tq)
            v_t = v_ref[0, rows, :].astype(f32).T.astype(bf16)
            for h in range(n_heads):
                vt[jb, h * V_ROWS:h * V_ROWS + HEAD_DIM, :] = v_t[h * HEAD_DIM:(h + 1) * HEAD_DIM, :]
                vt[jb, h * V_ROWS + HEAD_DIM:(h + 1) * V_ROWS, :] = ones_rows
            pieces = jnp.concatenate(_split3(-cc_ref[0, rows, :]), axis=1)
            spare = _dot(pieces, psel_ref[...]).astype(bf16)
            for p in range(n_heads // 2):
                a = spare[:, p * 128:(p + 1) * 128]
                kp = k_ref[0, rows, p * 128:(p + 1) * 128]
                kaug[2 * p, rows, :] = jnp.where(lo_half, kp, a)
                kaug[2 * p + 1, rows, :] = jnp.where(lo_half, a, kp)

    ones_hi = jnp.where((lane >= HEAD_DIM) & (lane < HEAD_DIM + 3), 1.0, 0.0).astype(bf16)
    ones_lo = jnp.where(lane < 3, 1.0, 0.0).astype(bf16)
    qa = []
    for p in range(n_heads // 2):
        qp = q_ref[0, :, p * 128:(p + 1) * 128]
        qa.append(jnp.where(lo_half, qp, ones_hi))
        qa.append(jnp.where(lo_half, ones_lo, qp))
    keep = lax.broadcasted_iota(i32, (tq, tq), 0) <= lax.broadcasted_iota(i32, (tq, tq), 1)
    acc[...] = jnp.zeros_like(acc)

    def score(jb, buf):
        k0 = pl.multiple_of(jb * tq, tq)
        for h in range(n_heads):
            buf[h] = _dot_nt(kaug[h, pl.ds(k0, tq), :], qa[h])

    def absorb(jb, ms, masked, buf):
        new_ms = []
        for h in range(n_heads):
            s = buf[h]
            if masked:
                s = jnp.where(keep, s, NEG)
            m_new = jnp.maximum(ms[h], jnp.max(s, axis=0, keepdims=True))
            alpha = jnp.exp2(ms[h] - m_new)
            p = jnp.exp2(s - m_new).astype(bf16)
            new_ms.append(m_new)
            acc[h] = acc[h] * alpha + _dot(vt[jb, h * V_ROWS:(h + 1) * V_ROWS, :], p)
        return tuple(new_ms)

    def pair(pp, ms):
        j0 = 2 * pp
        score(j0 + 1, sc1)
        ms = absorb(j0, ms, False, sc0)
        score(j0 + 2, sc0)
        return absorb(j0 + 1, ms, False, sc1)

    score(0, sc0)
    ms = lax.fori_loop(0, i // 2, pair, tuple(jnp.full((1, tq), NEG, f32) for _ in range(n_heads)))

    @pl.when(i % 2 == 0)
    def _even():
        absorb(i, ms, True, sc0)

    @pl.when(i % 2 == 1)
    def _odd():
        score(i, sc1)
        absorb(i, absorb(i - 1, ms, False, sc0), True, sc1)

    out_t = jnp.concatenate([acc[h, 0:HEAD_DIM, :] * (1.0 / acc[h, HEAD_DIM:HEAD_DIM + 1, :]) for h in range(n_heads)],
                            axis=0)
    ms_ = jnp.mean(out_t * out_t, axis=0, keepdims=True)
    out_t = out_t * lax.rsqrt(ms_ + EPS)
    o_ref[0] = (out_t.T * ng_ref[...]).astype(bf16)


def _piece_select():
    sel = np.zeros((384, 4 * 128), np.float32)
    for pair in range(4):
        for j in range(3):
            sel[j * 128 + 8 + 2 * pair, pair * 128 + HEAD_DIM + j] = 1.0
            sel[j * 128 + 8 + 2 * pair + 1, pair * 128 + j] = 1.0
    return jnp.asarray(sel, bf16)


def _attn(qkv, cumc, norm_g):
    bsz, t, _ = qkv.shape
    tq = min(ATT_BLOCK, t)
    return pl.pallas_call(
        functools.partial(_attn_kernel, tq=tq, t=t),
        grid=(bsz, t // tq),
        in_specs=[pl.BlockSpec((1, tq, D_ATT), lambda b, i: (b, i, 0)),
                  pl.BlockSpec((1, t, D_ATT), lambda b, i: (b, 0, 1)),
                  pl.BlockSpec((1, t, D_ATT), lambda b, i: (b, 0, 2)),
                  pl.BlockSpec((1, t, 128), lambda b, i: (b, 0, 0)),
                  pl.BlockSpec((384, 512), lambda b, i: (0, 0)),
                  pl.BlockSpec((1, D_ATT), lambda b, i: (0, 0))],
        out_specs=pl.BlockSpec((1, tq, D_ATT), lambda b, i: (b, i, 0)),
        out_shape=jax.ShapeDtypeStruct((bsz, t, D_ATT), bf16),
        scratch_shapes=[pltpu.VMEM((D_ATT // HEAD_DIM, t, 128), bf16),
                        pltpu.VMEM((t // tq, (D_ATT // HEAD_DIM) * V_ROWS, tq), bf16),
                        pltpu.VMEM((D_ATT // HEAD_DIM, V_ROWS, tq), f32),
                        pltpu.VMEM((D_ATT // HEAD_DIM, tq, tq), f32),
                        pltpu.VMEM((D_ATT // HEAD_DIM, tq, tq), f32)],
        compiler_params=pltpu.CompilerParams(dimension_semantics=("parallel", "arbitrary"),
                                             vmem_limit_bytes=VMEM_LIMIT),
        name="attn",
    )(qkv, qkv, qkv, cumc, _piece_select(), norm_g)


def _layer_norm(y, g, b):
    mu = jnp.mean(y, axis=-1, keepdims=True)
    yc = y - mu
    var = jnp.mean(yc * yc, axis=-1, keepdims=True)
    return yc * lax.rsqrt(var + EPS) * g + b


def _outproj_kernel(ys_ref, ya_ref, x_ref, gt_ref, sc_ref, sh_ref, wo_ref, lng_ref, lnb_ref, wrh_ref, wrl_ref, rb_ref,
                    x1_ref, rt_ref, dest_ref, cnt_ref, tbl_ref, xrows_hbm,
                    carry, galloc, tbl, u2t, didx_v, didx_s, cnt_v, cnt_s, zeros, idx_sem, sc_sem, z_sem,
                    *, tm, blk, n_blk):
    i = pl.program_id(0)
    last = pl.num_programs(0) - 1
    cur = i % 2
    prev = 1 - cur

    def idx_ready(slot):
        return pltpu.make_async_copy(didx_v.at[slot], didx_s.at[slot], idx_sem.at[slot])

    def dispatched(slot):
        return pltpu.make_async_copy(u2t.at[slot], u2t.at[slot], sc_sem.at[slot])

    def dispatch_copy(slot, r, k):
        return pltpu.make_async_copy(u2t.at[slot, pl.ds(r * PACK_TILE, PACK_TILE)],
                                     xrows_hbm.at[pl.ds(didx_s[slot, k, r] * PACK_TILE, PACK_TILE)], sc_sem.at[slot])

    @pl.when(i == 0)
    def _init():
        carry[...] = jnp.zeros_like(carry)
        galloc[...] = jnp.zeros_like(galloc)
        tbl[...] = jnp.zeros_like(tbl)

    @pl.when(i >= 2)
    def _reuse():
        dispatched(cur).wait()
        dispatched(cur).wait()

    def step(dispatch_prev):
        n_parts = OUTPROJ_PARTS
        n_slices = 3 * n_parts + 2

        def dispatch_slice(c):
            if dispatch_prev:
                for r_ in range(c * tm // n_slices, (c + 1) * tm // n_slices):
                    dispatch_copy(prev, r_, 0).start(priority=0)
                    dispatch_copy(prev, r_, 1).start(priority=1)

        if dispatch_prev:
            idx_ready(prev).wait()
        part = tm // n_parts
        hs = []
        for a in range(n_parts):
            rs = slice(a * part, (a + 1) * part)
            dispatch_slice(a)
            hs.append(_dot(ys_ref[rs, :], wo_ref[0:D_SSM, :]) + _dot(ya_ref[rs, :], wo_ref[D_SSM:D, :]))
        logit_parts = []
        for a in range(n_parts):
            rs = slice(a * part, (a + 1) * part)
            dispatch_slice(n_parts + 2 * a)
            x1 = _layer_norm(ALPHA * x_ref[rs, :] + (1.0 + gt_ref[0, 0]) * hs[a], lng_ref[...], lnb_ref[...])
            x1_ref[rs, :] = x1
            u2 = x1 * (1.0 + sc_ref[0, 0]) + sh_ref[0, 0]
            uh = u2.astype(bf16)
            _store_packed_rows(u2t.at[cur, pl.ds(a * part * PACK_TILE, part * PACK_TILE)], _pack_rows(uh))
            dispatch_slice(n_parts + 2 * a + 1)
            ul = (u2 - uh.astype(f32)).astype(bf16)
            logit_parts.append((_dot(uh, wrh_ref[...]) + _dot(ul, wrh_ref[...])) + _dot(uh, wrl_ref[...]))
        logits = jnp.concatenate(logit_parts, axis=0) + rb_ref[...]
        lane = lax.broadcasted_iota(i32, (tm, 128), 1).astype(f32)
        big = jnp.float32(1e9)

        def first_max(vals):
            m = jnp.max(vals, axis=-1, keepdims=True)
            return m, jnp.min(jnp.where(vals == m, lane, big), axis=-1, keepdims=True)

        gl = jnp.where(lane < N_GROUPS_R, logits, NEG)
        gmax, gidx = first_max(gl)
        g_p = 1.0 / jnp.sum(jnp.exp(gl - gmax), axis=-1, keepdims=True)
        lo = N_GROUPS_R + EXPERTS_PER_GROUP * gidx
        el = jnp.where((lane >= lo) & (lane < lo + EXPERTS_PER_GROUP), logits, NEG)
        m1, i1 = first_max(el)
        el2 = jnp.where(lane == i1, NEG, el)
        m2, i2 = first_max(el2)
        r = jnp.exp(m2 - m1)
        w1 = g_p / (1.0 + r)
        w2 = g_p * r / (1.0 + r)
        dispatch_slice(3 * n_parts)

        oh1 = lane == i1
        oh2 = lane == i2
        oh = jnp.where(oh1 | oh2, 1.0, 0.0)
        r_i = lax.broadcasted_iota(i32, (tm, tm), 0)
        c_i = lax.broadcasted_iota(i32, (tm, tm), 1)
        lower = jnp.where(r_i > c_i, 1.0, 0.0).astype(bf16)
        c_old = carry[...]
        prefix = _dot(lower, oh.astype(bf16)) + c_old
        rank1 = jnp.sum(jnp.where(oh1, prefix, 0.0), axis=-1, keepdims=True)
        rank2 = jnp.sum(jnp.where(oh2, prefix, 0.0), axis=-1, keepdims=True)
        c_new = c_old + jnp.sum(oh, axis=0, keepdims=True)
        carry[...] = c_new
        dispatch_slice(3 * n_parts + 1)

        nb_old = jnp.floor((c_old + (blk - 1)) * (1.0 / blk))
        nb_new = jnp.floor((c_new + (blk - 1)) * (1.0 / blk))
        fresh = nb_new - nb_old
        sq_r = lax.broadcasted_iota(i32, (128, 128), 0)
        sq_c = lax.broadcasted_iota(i32, (128, 128), 1)
        before = jnp.where(sq_r < sq_c, 1.0, 0.0).astype(bf16)
        base = galloc[...] + _dot(jnp.broadcast_to(fresh, (8, 128)).astype(bf16), before)[0:1, :]
        galloc[...] = galloc[...] + jnp.sum(fresh, axis=-1, keepdims=True)
        ordinal = sq_r.astype(f32)
        tbl[...] = jnp.where((ordinal >= nb_old) & (ordinal < nb_new), base + (ordinal - nb_old), tbl[...])
        tbl_ref[...] = tbl[...]
        cnt_ref[...] = jnp.concatenate([c_new, galloc[...], jnp.zeros((6, 128), f32)], axis=0)

        rt = jnp.where(lane == 4, w1, jnp.where(lane == 5, w2, 0.0))
        rt_ref[...] = rt

        ids_bf = tbl[...].astype(bf16)

        def row_of(onehot, rank):
            nth = jnp.floor(rank * (1.0 / blk))
            ids = _dot(jnp.where(lane == nth, 1.0, 0.0).astype(bf16), ids_bf)
            return jnp.sum(jnp.where(onehot, ids, 0.0), axis=-1, keepdims=True) * blk + (rank - nth * blk)

        d1 = row_of(oh1, rank1)
        d2 = row_of(oh2, rank2)
        dd = jnp.where(lane == 0, d1, jnp.where(lane == 1, d2, 0.0)).T[0:8, :].astype(i32)
        dest_ref[...] = dd
        didx_v[cur] = dd
        idx_ready(cur).start()

    pl.when(i == 0)(lambda: step(False))
    pl.when(i > 0)(lambda: step(True))

    @pl.when(i == last)
    def _epilogue():
        idx_ready(cur).wait()

        def issue(r_, c):
            dispatch_copy(cur, r_, 0).start()
            dispatch_copy(cur, r_, 1).start()
            return c
        lax.fori_loop(0, tm, issue, 0, unroll=8)

        cnt = carry[...]
        n_blocks = jnp.floor((cnt + (blk - 1)) * (1.0 / blk))
        ordinal = lax.broadcasted_iota(i32, (128, 128), 0).astype(f32)
        last_id = jnp.sum(jnp.where(ordinal == n_blocks - 1.0, tbl[...], 0.0), axis=0, keepdims=True)
        used = cnt - (n_blocks - 1.0) * blk
        cnt_v[...] = jnp.concatenate([last_id * blk + used, blk - used, galloc[...], jnp.zeros((5, 128), f32)],
                                     axis=0).astype(i32)
        to_smem = pltpu.make_async_copy(cnt_v, cnt_s, z_sem)
        to_smem.start()
        to_smem.wait()
        zeros[...] = jnp.zeros_like(zeros)
        sizes = [1 << b for b in reversed(range(blk.bit_length() - 1))]

        def for_each_fill(fn):
            def tail(e, c):
                pad = cnt_s[1, N_GROUPS_R + e]
                off = cnt_s[0, N_GROUPS_R + e]
                for sz in sizes:
                    @pl.when((pad & sz) != 0)
                    def _(off=off, sz=sz):
                        fn(pltpu.make_async_copy(zeros.at[pl.ds(0, sz * PACK_TILE)],
                                                 xrows_hbm.at[pl.ds(off * PACK_TILE, sz * PACK_TILE)], z_sem))
                    off = off + (pad & sz)
                return c
            lax.fori_loop(0, N_EXPERTS, tail, 0)

            def whole(b, c):
                fn(pltpu.make_async_copy(zeros, xrows_hbm.at[pl.ds(b * blk * PACK_TILE, blk * PACK_TILE)], z_sem))
                return c
            lax.fori_loop(cnt_s[2, 0], n_blk, whole, 0)

        for_each_fill(lambda copy: copy.start())
        for_each_fill(lambda copy: copy.wait())

        @pl.when(i > 0)
        def _prev_done():
            dispatched(prev).wait()
            dispatched(prev).wait()
        dispatched(cur).wait()
        dispatched(cur).wait()


def _outproj(y_ssm, y_att, x, mod4, w_out, ln_g, ln_b, wr_hi, wr_lo, rb, t, blk, n_blk):
    n = x.shape[0]
    tm = min(OUTPROJ_ROWS, t)
    nt = t // tm
    assert n // blk + 1 <= 128 and n_blk <= 256, "block-id table: 128 blocks per expert, ids exact in bf16"
    vec = lambda k: pl.BlockSpec((1, 1, 1, D), lambda i, k=k: (k, i // nt, 0, 0))
    full = lambda shape: pl.BlockSpec(shape, lambda i: (0,) * len(shape))
    rows = lambda w: pl.BlockSpec((tm, w), lambda i: (i, 0))
    return pl.pallas_call(
        functools.partial(_outproj_kernel, tm=tm, blk=blk, n_blk=n_blk),
        grid=(n // tm,),
        in_specs=[rows(D_SSM), rows(D_ATT), rows(D), vec(2), vec(4), vec(3),
                  full((D, D)), full((1, D)), full((1, D)), full((D, 128)), full((D, 128)), full((1, 128))],
        out_specs=[rows(D), rows(128), pl.BlockSpec((8, tm), lambda i: (0, i)), full((8, 128)), full((128, 128)),
                   pl.BlockSpec(memory_space=pl.ANY)],
        out_shape=[jax.ShapeDtypeStruct((n, D), f32), jax.ShapeDtypeStruct((n, 128), f32),
                   jax.ShapeDtypeStruct((8, n), i32), jax.ShapeDtypeStruct((8, 128), f32),
                   jax.ShapeDtypeStruct((128, 128), f32),
                   jax.ShapeDtypeStruct((n_blk * blk * PACK_TILE, 128), u32)],
        scratch_shapes=[pltpu.VMEM((1, 128), f32), pltpu.VMEM((1, 128), f32), pltpu.VMEM((128, 128), f32),
                        pltpu.VMEM((2, tm * PACK_TILE, 128), u32),
                        pltpu.VMEM((2, 8, tm), i32), pltpu.SMEM((2, 8, tm), i32),
                        pltpu.VMEM((8, 128), i32), pltpu.SMEM((8, 128), i32),
                        pltpu.VMEM((blk * PACK_TILE, 128), u32),
                        pltpu.SemaphoreType.DMA((2,)), pltpu.SemaphoreType.DMA((2,)), pltpu.SemaphoreType.DMA(())],
        compiler_params=pltpu.CompilerParams(dimension_semantics=("arbitrary",), vmem_limit_bytes=VMEM_LIMIT),
        name="outproj",
    )(y_ssm, y_att, x, mod4, mod4, mod4, w_out, ln_g, ln_b, wr_hi, wr_lo, rb)


def _tile_copy(src_hbm, row, buf, slot, sem):
    return pltpu.make_async_copy(src_hbm.at[row], buf.at[pl.ds(slot * ROW_TILE, ROW_TILE)], sem)


def _moe_kernel(be_ref, br_ref, nu_ref, nxt_ref, par_ref, nv_ref, x_ref, wg_hbm, wu_hbm, wd_hbm, y_ref,
                wfg, wfu, wfd, wgb, wub, wdb, wsem, *, blk):
    i = pl.program_id(0)
    used = i < nu_ref[0]

    def fetch(e, slot):
        return [pltpu.make_async_copy(src.at[e], dst.at[slot], wsem.at[slot])
                for src, dst in ((wg_hbm, wfg), (wu_hbm, wfu), (wd_hbm, wfd))]

    @pl.when(i == 0)
    def _first():
        for c in fetch(be_ref[0], par_ref[0]):
            c.start()

    @pl.when(jnp.logical_and(used, jnp.logical_or(i == 0, be_ref[i] != be_ref[jnp.maximum(i - 1, 0)])))
    def _switch():
        slot = par_ref[i]
        for c in fetch(be_ref[i], slot):
            c.wait()

        @pl.when(nxt_ref[i] >= 0)
        def _next():
            for c in fetch(nxt_ref[i], 1 - slot):
                c.start(priority=1)
        wgb[...] = wfg[slot].astype(bf16)
        wub[...] = wfu[slot].astype(bf16)
        wdb[...] = wfd[slot].astype(bf16)

    @pl.when(jnp.logical_not(used))
    def _spare():
        y_ref[...] = jnp.zeros_like(y_ref)

    def mlp(rows):
        xa, xb = _load_packed_rows(x_ref.at[pl.ds(0, rows * PACK_TILE)], rows)
        half = D // 2
        gate = _dot(xa, wgb[0:half, :]) + _dot(xb, wgb[half:D, :])
        up = _dot(xa, wub[0:half, :]) + _dot(xb, wub[half:D, :])
        hid = (_silu(gate) * up).astype(bf16)
        _store_row_tiles(y_ref.at[pl.ds(0, rows * ROW_TILE)], _dot(hid, wdb[...]))

    sparse = nv_ref[i] <= blk // 2

    @pl.when(jnp.logical_and(used, jnp.logical_not(sparse)))
    def _full():
        mlp(blk)

    @pl.when(jnp.logical_and(used, sparse))
    def _half():
        mlp(blk // 2)
        y_ref[pl.ds(blk // 2 * ROW_TILE, blk // 2 * ROW_TILE), :] = jnp.zeros((blk // 2 * ROW_TILE, 128), f32)


def _moe(block_e, block_row, n_used, next_e, parity, n_valid, x_rows, w_gate, w_up, w_down, blk):
    nblk = block_e.shape[0]
    rspec = pl.BlockSpec((blk * ROW_TILE, 128), lambda i, be, br, *_: (br[i], 0))
    xspec = pl.BlockSpec((blk * PACK_TILE, 128), lambda i, be, br, nu, *_: (br[jnp.minimum(i, nu[0] - 1)], 0))
    anyspec = pl.BlockSpec(memory_space=pl.ANY)
    grid_spec = pltpu.PrefetchScalarGridSpec(
        num_scalar_prefetch=6,
        grid=(nblk,),
        in_specs=[xspec, anyspec, anyspec, anyspec],
        out_specs=rspec,
        scratch_shapes=[pltpu.VMEM((2, D, D_EXPERT), f32), pltpu.VMEM((2, D, D_EXPERT), f32),
                        pltpu.VMEM((2, D_EXPERT, D), f32),
                        pltpu.VMEM((D, D_EXPERT), bf16), pltpu.VMEM((D, D_EXPERT), bf16),
                        pltpu.VMEM((D_EXPERT, D), bf16),
                        pltpu.SemaphoreType.DMA((2,))],
    )
    return pl.pallas_call(
        functools.partial(_moe_kernel, blk=blk),
        grid_spec=grid_spec,
        out_shape=jax.ShapeDtypeStruct((nblk * blk * ROW_TILE, 128), f32),
        compiler_params=pltpu.CompilerParams(dimension_semantics=("arbitrary",), vmem_limit_bytes=VMEM_LIMIT),
        name="moe",
    )(block_e, block_row, n_used, next_e, parity, n_valid, x_rows, w_gate, w_up, w_down)


def _combine_kernel(d1_ref, d2_ref, y_hbm, x1_ref, rt_ref, gt_ref, lng_ref, lnb_ref, o_ref, buf, sem, *, tm):
    i = pl.program_id(0)
    last = pl.num_programs(0) - 1

    @pl.when(i == 0)
    def _prologue():
        for s in range(2):
            def issue(r, c, s=s):
                base = jnp.minimum(s, last) * tm
                _tile_copy(y_hbm, d1_ref[base + r], buf.at[s, 0], r, sem.at[s]).start()
                _tile_copy(y_hbm, d2_ref[base + r], buf.at[s, 1], r, sem.at[s]).start()
                return c
            lax.fori_loop(0, tm, issue, 0)

    def rows_ready(slot):
        return pltpu.make_async_copy(buf.at[slot], buf.at[slot], sem.at[slot])

    slot = i % 3
    nxt = (i + 2) % 3
    base = jnp.minimum(i + 2, last) * tm
    rows_ready(slot).wait()
    for r in range(tm):
        _tile_copy(y_hbm, d1_ref[base + r], buf.at[nxt, 0], r, sem.at[nxt]).start(priority=0)
        _tile_copy(y_hbm, d2_ref[base + r], buf.at[nxt, 1], r, sem.at[nxt]).start(priority=1)
    rt = rt_ref[...]
    moe = rt[:, 4:5] * _load_row_tiles(buf.at[slot, 0], tm) + rt[:, 5:6] * _load_row_tiles(buf.at[slot, 1], tm)
    y = ALPHA * x1_ref[...] + (1.0 + gt_ref[0, 0]) * moe
    o_ref[...] = _layer_norm(y, lng_ref[...], lnb_ref[...])

    @pl.when(i == last)
    def _drain():
        rows_ready((i + 1) % 3).wait()
        rows_ready((i + 2) % 3).wait()


def _combine(dest1, dest2, y_rows, x1, rt, mod4, ln_g, ln_b, t):
    n = x1.shape[0]
    tm = min(COMBINE_ROWS, t)
    nt = t // tm
    full = lambda shape: pl.BlockSpec(shape, lambda i, d1, d2: (0,) * len(shape))
    rows = lambda w: pl.BlockSpec((tm, w), lambda i, d1, d2: (i, 0))
    grid_spec = pltpu.PrefetchScalarGridSpec(
        num_scalar_prefetch=2,
        grid=(n // tm,),
        in_specs=[pl.BlockSpec(memory_space=pl.ANY), rows(D), rows(128),
                  pl.BlockSpec((1, 1, 1, D), lambda i, d1, d2: (5, i // nt, 0, 0)),
                  full((1, D)), full((1, D))],
        out_specs=rows(D),
        scratch_shapes=[pltpu.VMEM((3, 2, tm * ROW_TILE, 128), f32), pltpu.SemaphoreType.DMA((3,))],
    )
    return pl.pallas_call(
        functools.partial(_combine_kernel, tm=tm),
        grid_spec=grid_spec,
        out_shape=jax.ShapeDtypeStruct((n, D), f32),
        compiler_params=pltpu.CompilerParams(dimension_semantics=("arbitrary",), vmem_limit_bytes=VMEM_LIMIT),
        name="combine",
    )(dest1, dest2, y_rows, x1, rt, mod4, ln_g, ln_b)


def _hi_lo(w):
    hi = w.astype(bf16)
    return hi, (w - hi.astype(f32)).astype(bf16)


def _pad_lanes(v, offset, width):
    return jnp.zeros((width,), f32).at[offset:offset + v.shape[0]].set(v)


def _layer(x, c, ada_w, ada_b, w_in, conv_w, conv_b, dt_bias, a_log, d_skip, ssm_norm_g, fg_bias, att_norm_g,
           w_out, ln1_g, ln1_b, router_g_w, router_g_b, router_e_w, router_e_b, w_gate, w_up, w_down, ln2_g, ln2_b):
    bsz, t, _ = x.shape
    n = bsz * t

    mod4 = _ada(c, ada_w, ada_b).reshape(6, bsz, 1, D)

    w_main, w_small = _pack_w_in(w_in)
    zx, bc, qkv, dtf = _inproj(x, mod4, w_main, w_small, conv_w, conv_b.reshape(1, -1))

    pc = jnp.stack([_pad_lanes(dt_bias, 0, 128), _pad_lanes(a_log, 0, 128), _pad_lanes(fg_bias, 8, 128)]
                   + [jnp.zeros((128,), f32)] * 5)
    rep = lambda v: jnp.repeat(v, HEAD_DIM)
    pe = jnp.stack([rep(dt_bias), rep(a_log), rep(d_skip), ssm_norm_g] + [jnp.zeros((D_SSM,), f32)] * 4)
    y_ssm, cumc = _ssd(zx, bc, dtf, pc, pe)

    y_att = _attn(qkv, cumc, att_norm_g.reshape(1, -1))

    wr = jnp.concatenate([router_g_w, router_e_w, jnp.zeros((D, 128 - N_GROUPS_R - N_EXPERTS), f32)], axis=1)
    wr_hi, wr_lo = _hi_lo(wr)
    rb = jnp.concatenate([router_g_b, router_e_b, jnp.zeros((128 - N_GROUPS_R - N_EXPERTS,), f32)]).reshape(1, 128)
    blk = MOE_ROWS
    nblk = (2 * n) // blk + N_EXPERTS
    x1, rt, dest, cnt, tbl, x_rows = _outproj(y_ssm.reshape(n, D_SSM), y_att.reshape(n, D_ATT), x.reshape(n, D), mod4,
                                              w_out.astype(bf16), ln1_g.reshape(1, D), ln1_b.reshape(1, D),
                                              wr_hi, wr_lo, rb, t, blk, nblk)

    counts = cnt[0, N_GROUPS_R:N_GROUPS_R + N_EXPERTS].astype(i32)
    nb = (counts + blk - 1) // blk
    nb_end = jnp.cumsum(nb)
    n_used = nb_end[N_EXPERTS - 1:]
    every = jnp.arange(nblk, dtype=i32)
    step = jnp.minimum(every, n_used[0] - 1)
    owner = (nb_end[None, :] <= step[:, None]).astype(i32)
    block_e = jnp.sum(owner, axis=1)
    first = jnp.sum(owner * nb[None, :], axis=1)
    lanes = jnp.arange(128, dtype=i32)[None, :]
    nth_row = jnp.dot((lanes == (step - first)[:, None]).astype(f32), tbl, precision=lax.Precision.HIGHEST)
    block_id = jnp.sum(jnp.where(lanes == (block_e + N_GROUPS_R)[:, None], nth_row, 0.0), axis=1).astype(i32)
    block_row = jnp.where(every < n_used[0], block_id, every)
    run_end = first + jnp.sum((jnp.arange(N_EXPERTS)[None, :] == block_e[:, None]) * nb[None, :], axis=1)
    next_e = jnp.where(run_end < n_used[0], jnp.sum((nb_end[None, :] <= run_end[:, None]).astype(i32), axis=1), -1)
    parity = jnp.sum(((jnp.arange(N_EXPERTS)[None, :] < block_e[:, None]) & (nb[None, :] > 0)).astype(i32), axis=1) % 2

    own = (jnp.arange(N_EXPERTS)[None, :] == block_e[:, None]).astype(i32)
    n_valid = jnp.clip(jnp.sum(own * counts[None, :], axis=1) - (step - first) * blk, 0, blk)
    y_rows = _moe(block_e, block_row, n_used, next_e, parity, n_valid, x_rows,
                  w_gate, w_up, w_down, blk)
    out = _combine(dest[0], dest[1], y_rows.reshape(nblk * blk, ROW_TILE, 128), x1, rt, mod4,
                   ln2_g.reshape(1, D), ln2_b.reshape(1, D), t)
    return out.reshape(bsz, t, D)


def kernel(x, c, ada_w, ada_b, w_in, conv_w, conv_b, dt_bias, a_log, d_skip, ssm_norm_g, fg_bias, att_norm_g, w_out,
           ln1_g, ln1_b, router_g_w, router_g_b, router_e_w, router_e_b, w_gate, w_up, w_down, ln2_g, ln2_b):
    depth = ada_w.shape[0]
    for l in range(depth):
        x = _layer(x, c, ada_w[l], ada_b[l], w_in[l], conv_w[l], conv_b[l], dt_bias[l], a_log[l], d_skip[l],
                   ssm_norm_g[l], fg_bias[l], att_norm_g[l], w_out[l], ln1_g[l], ln1_b[l], router_g_w[l],
                   router_g_b[l], router_e_w[l], router_e_b[l], w_gate[l], w_up[l], w_down[l], ln2_g[l], ln2_b[l])
    return x
```

```python
import functools

import jax
import jax.numpy as jnp
import numpy as np
from jax import lax
from jax.experimental import pallas as pl
from jax.experimental.pallas import tpu as pltpu

f32 = jnp.float32
bf16 = jnp.bfloat16
i32 = jnp.int32

D = 1024
D_SSM = 512
D_ATT = 512
HEAD_DIM = 64
GROUP_W = 256
N_STATE = 128
CONV_K = 4
N_GROUPS_R = 4
EXPERTS_PER_GROUP = 8
N_EXPERTS = 32
D_EXPERT = 512
ALPHA = 2.0 ** 0.25
EPS = 1e-5
NEG = -1e30
LOG2E = 1.4426950408889634
QK_SCALE = HEAD_DIM ** -0.5 * LOG2E
V_ROWS = 80

SSD_CHUNK = 256
ATT_BLOCK = 256
INPROJ_ROWS = 512
OUTPROJ_ROWS = 1024
OUTPROJ_PARTS = 4
MOE_ROWS = 512
COMBINE_ROWS = 256
VMEM_LIMIT = 48 * 1024 * 1024


def _dot(a, b):
    return jnp.dot(a, b, preferred_element_type=f32)


def _dot_nt(a, b):
    return lax.dot_general(a, b, (((1,), (1,)), ((), ())), preferred_element_type=f32)


def _dot_tn(a, b):
    return lax.dot_general(a, b, (((0,), (0,)), ((), ())), preferred_element_type=f32)


def _split3(v):
    hi = v.astype(bf16)
    r1 = v - hi.astype(f32)
    mid = r1.astype(bf16)
    lo = (r1 - mid.astype(f32)).astype(bf16)
    return hi, mid, lo


def _dot_exact_lhs(m, v):
    hi, mid, lo = _split3(v)
    return (_dot(m, hi) + _dot(m, mid)) + _dot(m, lo)


def _dot_exact_rhs(v, m):
    hi, mid, lo = _split3(v)
    return (_dot(hi, m) + _dot(mid, m)) + _dot(lo, m)


ROW_TILE = 8


def _store_row_tiles(ref, val):
    rows = val.shape[0]
    for c in range(ROW_TILE):
        ref[pl.ds(c, rows, stride=ROW_TILE), :] = val[:, c * 128:(c + 1) * 128]


def _load_row_tiles(ref, rows):
    return jnp.concatenate([ref[pl.ds(c, rows, stride=ROW_TILE), :] for c in range(ROW_TILE)], axis=1)


PACK_TILE = 4
u32 = jnp.uint32


def _pack_rows(vb):
    lo = pltpu.bitcast(vb[:, 0:D // 2].astype(f32), u32) >> 16
    hi = pltpu.bitcast(vb[:, D // 2:D].astype(f32), u32) & jnp.uint32(0xFFFF0000)
    return lo | hi


def _store_packed_rows(ref, packed):
    rows = packed.shape[0]
    for c in range(PACK_TILE):
        ref[pl.ds(c, rows, stride=PACK_TILE), :] = packed[:, c * 128:(c + 1) * 128]


def _load_packed_rows(ref, rows):
    words = [ref[pl.ds(c, rows, stride=PACK_TILE), :] for c in range(PACK_TILE)]
    lo = jnp.concatenate([pltpu.bitcast(w << 16, f32) for w in words], axis=1)
    hi = jnp.concatenate([pltpu.bitcast(w & jnp.uint32(0xFFFF0000), f32) for w in words], axis=1)
    return lo.astype(bf16), hi.astype(bf16)


def _softplus(x):
    return jnp.maximum(x, 0.0) + jnp.log1p(jnp.exp(-jnp.abs(x)))


def _silu(x):
    return x * jax.nn.sigmoid(x)


def _ada_kernel(c_ref, w_ref, b_ref, o_ref):
    s = _silu(c_ref[...]).astype(bf16)
    o_ref[0] = _dot(s, w_ref[...].astype(bf16)) + b_ref[0]


def _ada(c, w, b):
    bsz = c.shape[0]
    return pl.pallas_call(
        _ada_kernel,
        grid=(6,),
        in_specs=[pl.BlockSpec((bsz, D), lambda j: (0, 0)),
                  pl.BlockSpec((D, D), lambda j: (0, j)),
                  pl.BlockSpec((1, 1, D), lambda j: (j, 0, 0))],
        out_specs=pl.BlockSpec((1, bsz, D), lambda j: (j, 0, 0)),
        out_shape=jax.ShapeDtypeStruct((6, bsz, D), f32),
        compiler_params=pltpu.CompilerParams(dimension_semantics=("arbitrary",), vmem_limit_bytes=VMEM_LIMIT),
        name="ada",
    )(c, w, b.reshape(6, 1, D))


W_IN_COLS = 3088
PACK_ROWS = 256


def _pack_w_in_kernel(wt_ref, wm_ref, ws_ref):
    for j in range(3072 // PACK_ROWS):
        src = j * PACK_ROWS if j * PACK_ROWS < 1536 else j * PACK_ROWS + 8
        wm_ref[:, j * PACK_ROWS:(j + 1) * PACK_ROWS] = wt_ref[src:src + PACK_ROWS, :].T.astype(bf16)
    small = jnp.concatenate([wt_ref[1536:1544, :], wt_ref[W_IN_COLS - 8:W_IN_COLS, :], jnp.zeros((112, D), f32)],
                            axis=0).T
    hi = small.astype(bf16)
    ws_ref[:, 0:128] = hi
    ws_ref[:, 128:256] = (small - hi.astype(f32)).astype(bf16)


def _pack_w_in(w_in):
    full = lambda shape: pl.BlockSpec(shape, lambda i: (0,) * len(shape))
    return pl.pallas_call(
        _pack_w_in_kernel,
        grid=(1,),
        in_specs=[full((W_IN_COLS, D))],
        out_specs=[full((D, 3072)), full((D, 256))],
        out_shape=[jax.ShapeDtypeStruct((D, 3072), bf16), jax.ShapeDtypeStruct((D, 256), bf16)],
        compiler_params=pltpu.CompilerParams(dimension_semantics=("arbitrary",), vmem_limit_bytes=VMEM_LIMIT),
        name="pack_w_in",
    )(w_in.T)


def _inproj_kernel(x_ref, sc_ref, sh_ref, wm_ref, ws_ref, cw_ref, cb_ref, zx_ref, bc_ref, qkv_ref, dtf_ref, xcat, *, tm):
    i = pl.program_id(1)

    @pl.when(i == 0)
    def _init():
        xcat[0:8, :] = jnp.zeros((8, 2 * D_SSM), f32)

    u = x_ref[0] * (1.0 + sc_ref[0, 0]) + sh_ref[0, 0]
    ub = u.astype(bf16)
    xcat[8:8 + tm, 0:512] = _dot(ub, wm_ref[:, 512:1024])
    xcat[8:8 + tm, 512:1024] = _dot(ub, wm_ref[:, 1024:1536])
    zx_ref[0, :, 0:512] = _dot(ub, wm_ref[:, 0:512])
    acc = cw_ref[0:1, :] * xcat[5:5 + tm, :] + cb_ref[...]
    for k in range(1, CONV_K):
        acc = acc + cw_ref[k:k + 1, :] * xcat[5 + k:5 + k + tm, :]
    xcat[0:8, :] = xcat[tm:tm + 8, :]
    xbc = _silu(acc)
    zx_ref[0, :, 512:1024] = xbc[:, 0:512]
    bc_ref[0] = xbc[:, 512:1024].astype(bf16)
    for j, scale in enumerate((QK_SCALE, 1.0, 1.0)):
        qkv_ref[0, :, j * 512:(j + 1) * 512] = (
            _dot(ub, wm_ref[:, 1536 + j * 512:1536 + (j + 1) * 512]) * scale).astype(bf16)
    ul = (u - ub.astype(f32)).astype(bf16)
    d_hl = _dot(ub, ws_ref[...])
    dtf_ref[0] = (d_hl[:, 0:128] + _dot(ul, ws_ref[:, 0:128])) + d_hl[:, 128:256]


def _inproj(x, mod4, w_main, ws, conv_w, conv_b):
    bsz, t, _ = x.shape
    tm = min(INPROJ_ROWS, t)
    vec = lambda k: pl.BlockSpec((1, 1, 1, D), lambda b, i, k=k: (k, b, 0, 0))
    full = lambda shape: pl.BlockSpec(shape, lambda b, i: (0,) * len(shape))
    rows = lambda w: pl.BlockSpec((1, tm, w), lambda b, i: (b, i, 0))
    return pl.pallas_call(
        functools.partial(_inproj_kernel, tm=tm),
        grid=(bsz, t // tm),
        in_specs=[rows(D), vec(1), vec(0), full((D, 3072)), full((D, 256)),
                  full((CONV_K, 2 * D_SSM)), full((1, 2 * D_SSM))],
        out_specs=[rows(1024), rows(512), rows(1536), rows(128)],
        out_shape=[jax.ShapeDtypeStruct((bsz, t, 1024), f32),
                   jax.ShapeDtypeStruct((bsz, t, 512), bf16),
                   jax.ShapeDtypeStruct((bsz, t, 1536), bf16),
                   jax.ShapeDtypeStruct((bsz, t, 128), f32)],
        scratch_shapes=[pltpu.VMEM((tm + 8, 2 * D_SSM), f32)],
        compiler_params=pltpu.CompilerParams(dimension_semantics=("parallel", "arbitrary"),
                                             vmem_limit_bytes=VMEM_LIMIT),
        name="inproj",
    )(x, mod4, mod4, w_main, ws, conv_w, conv_b)


def _ssd_kernel(z_ref, xs_ref, bc_ref, dtf_ref, pc_ref, pe_ref, y_ref, cumc_ref, state, carry, *, lc):
    j = pl.program_id(1)

    @pl.when(j == 0)
    def _init():
        state[...] = jnp.zeros_like(state)
        carry[...] = jnp.zeros_like(carry)

    xs = xs_ref[0]
    bm = bc_ref[0, :, 0:256]
    cm = bc_ref[0, :, 256:512]

    dtf = dtf_ref[0]
    lane = lax.broadcasted_iota(i32, (lc, 128), 1)
    dt_c = _softplus(dtf + pc_ref[0:1, :])
    a_c = dt_c * (-jnp.exp(pc_ref[1:2, :]))
    logf = -_softplus(-(dtf + pc_ref[2:3, :]))
    v = jnp.where(lane < 8, a_c, logf) * LOG2E
    r_i = lax.broadcasted_iota(i32, (lc, lc), 0)
    c_i = lax.broadcasted_iota(i32, (lc, lc), 1)
    tri = r_i >= c_i
    tri_b = jnp.where(tri, 1.0, 0.0).astype(bf16)
    cum = _dot_exact_lhs(tri_b, v) + carry[...]
    carry[...] = jnp.where(lane[0:1, :] >= 8, cum[lc - 1:lc, :], 0.0)
    cumc_ref[0] = cum
    cs_t = cum.T[0:8, :]

    e_r = lax.broadcasted_iota(i32, (128, D_SSM), 0)
    e_c = lax.broadcasted_iota(i32, (128, D_SSM), 1)
    expand = jnp.where(jnp.right_shift(e_c, 6) == e_r, 1.0, 0.0).astype(bf16)
    dt_e = _dot_exact_rhs(dt_c, expand)
    cs_e = _dot_exact_rhs(cum, expand)

    xdt = xs * dt_e
    ecs = jnp.exp2(cs_e)
    cs_last = cs_e[lc - 1:lc, :]
    dec_st = jnp.exp2(cs_last - cs_e)
    lane_g = lax.broadcasted_iota(i32, (1, GROUP_W), 1)
    ys = []
    for g in range(2):
        gs = slice(g * GROUP_W, (g + 1) * GROUP_W)
        bg = bm[:, g * N_STATE:(g + 1) * N_STATE]
        cg = cm[:, g * N_STATE:(g + 1) * N_STATE]
        cb = _dot_nt(cg, bg)
        xdt_g = xdt[:, gs]
        xdt_gb = xdt_g.astype(bf16)
        ms, xb = [], []
        for hh in range(4):
            h = g * 4 + hh
            lm = jnp.exp2(jnp.where(tri, cum[:, h:h + 1] - cs_t[h:h + 1, :], -jnp.inf))
            ms.append((cb * lm).astype(bf16))
            xb.append(jnp.where(jnp.right_shift(lane_g, 6) == hh, xdt_gb, jnp.zeros_like(xdt_gb)))
        y_diag = _dot(jnp.concatenate(ms, axis=1), jnp.concatenate(xb, axis=0))
        st = state[g]
        y_off = _dot(cg, st.astype(bf16)) * ecs[:, gs]
        upd = _dot_tn(bg, (xdt_g * dec_st[:, gs]).astype(bf16))
        state[g] = st * jnp.exp2(cs_last[:, gs]) + upd
        ys.append(y_diag + y_off + xs[:, gs] * pe_ref[2:3, gs])

    outs = []
    for g in range(2):
        gs = slice(g * GROUP_W, (g + 1) * GROUP_W)
        yg = ys[g] * _silu(z_ref[0, :, gs])
        ms_ = jnp.mean(yg * yg, axis=-1, keepdims=True)
        outs.append(yg * lax.rsqrt(ms_ + EPS))
    y_ref[0] = (jnp.concatenate(outs, axis=1) * pe_ref[3:4, :]).astype(bf16)


def _ssd(zx, bc, dtf, pc, pe):
    bsz, t, _ = zx.shape
    lc = min(SSD_CHUNK, t)
    col = lambda k: pl.BlockSpec((1, lc, 512), lambda b, j, k=k: (b, j, k))
    full = lambda shape: pl.BlockSpec(shape, lambda b, j: (0,) * len(shape))
    return pl.pallas_call(
        functools.partial(_ssd_kernel, lc=lc),
        grid=(bsz, t // lc),
        in_specs=[col(0), col(1), col(0),
                  pl.BlockSpec((1, lc, 128), lambda b, j: (b, j, 0)),
                  full((8, 128)), full((8, D_SSM))],
        out_specs=[pl.BlockSpec((1, lc, D_SSM), lambda b, j: (b, j, 0)),
                   pl.BlockSpec((1, lc, 128), lambda b, j: (b, j, 0))],
        out_shape=[jax.ShapeDtypeStruct((bsz, t, D_SSM), bf16),
                   jax.ShapeDtypeStruct((bsz, t, 128), f32)],
        scratch_shapes=[pltpu.VMEM((2, N_STATE, GROUP_W), f32),
                        pltpu.VMEM((1, 128), f32)],
        compiler_params=pltpu.CompilerParams(dimension_semantics=("parallel", "arbitrary"),
                                             vmem_limit_bytes=VMEM_LIMIT),
        name="ssd",
    )(zx, zx, bc, dtf, pc, pe)


def _attn_kernel(q_ref, k_ref, v_ref, cc_ref, psel_ref, ng_ref, o_ref, kaug, vt, acc, sc0, sc1, *, tq, t):
    i = pl.program_id(1)
    nkb = t // tq
    n_heads = D_ATT // HEAD_DIM
    lane = lax.broadcasted_iota(i32, (1, 128), 1)
    lo_half = lane < HEAD_DIM

    @pl.when(i == 0)
    def _build():
        ones_rows = jnp.where(lax.broadcasted_iota(i32, (V_ROWS - HEAD_DIM, tq), 0) == 0, 1.0, 0.0).astype(bf16)
        for jb in range(nkb):
            rows = slice(jb * tq, (jb + 1) * tq)
            v_t = v_ref[0, rows, :].astype(f32).T.astype(bf16)
            for h in range(n_heads):
                vt[jb, h * V_ROWS:h * V_ROWS + HEAD_DIM, :] = v_t[h * HEAD_DIM:(h + 1) * HEAD_DIM, :]
                vt[jb, h * V_ROWS + HEAD_DIM:(h + 1) * V_ROWS, :] = ones_rows
            pieces = jnp.concatenate(_split3(-cc_ref[0, rows, :]), axis=1)
            spare = _dot(pieces, psel_ref[...]).astype(bf16)
            for p in range(n_heads // 2):
                a = spare[:, p * 128:(p + 1) * 128]
                kp = k_ref[0, rows, p * 128:(p + 1) * 128]
                kaug[2 * p, rows, :] = jnp.where(lo_half, kp, a)
                kaug[2 * p + 1, rows, :] = jnp.where(lo_half, a, kp)

    ones_hi = jnp.where((lane >= HEAD_DIM) & (lane < HEAD_DIM + 3), 1.0, 0.0).astype(bf16)
    ones_lo = jnp.where(lane < 3, 1.0, 0.0).astype(bf16)
    qa = []
    for p in range(n_heads // 2):
        qp = q_ref[0, :, p * 128:(p + 1) * 128]
        qa.append(jnp.where(lo_half, qp, ones_hi))
        qa.append(jnp.where(lo_half, ones_lo, qp))
    keep = lax.broadcasted_iota(i32, (tq, tq), 0) <= lax.broadcasted_iota(i32, (tq, tq), 1)
    acc[...] = jnp.zeros_like(acc)

    def score(jb, buf):
        k0 = pl.multiple_of(jb * tq, tq)
        for h in range(n_heads):
            buf[h] = _dot_nt(kaug[h, pl.ds(k0, tq), :], qa[h])

    def absorb(jb, ms, masked, buf):
        new_ms = []
        for h in range(n_heads):
            s = buf[h]
            if masked:
                s = jnp.where(keep, s, NEG)
            m_new = jnp.maximum(ms[h], jnp.max(s, axis=0, keepdims=True))
            alpha = jnp.exp2(ms[h] - m_new)
            p = jnp.exp2(s - m_new).astype(bf16)
            new_ms.append(m_new)
            acc[h] = acc[h] * alpha + _dot(vt[jb, h * V_ROWS:(h + 1) * V_ROWS, :], p)
        return tuple(new_ms)

    def pair(pp, ms):
        j0 = 2 * pp
        score(j0 + 1, sc1)
        ms = absorb(j0, ms, False, sc0)
        score(j0 + 2, sc0)
        return absorb(j0 + 1, ms, False, sc1)

    score(0, sc0)
    ms = lax.fori_loop(0, i // 2, pair, tuple(jnp.full((1, tq), NEG, f32) for _ in range(n_heads)))

    @pl.when(i % 2 == 0)
    def _even():
        absorb(i, ms, True, sc0)

    @pl.when(i % 2 == 1)
    def _odd():
        score(i, sc1)
        absorb(i, absorb(i - 1, ms, False, sc0), True, sc1)

    out_t = jnp.concatenate([acc[h, 0:HEAD_DIM, :] * (1.0 / acc[h, HEAD_DIM:HEAD_DIM + 1, :]) for h in range(n_heads)],
                            axis=0)
    ms_ = jnp.mean(out_t * out_t, axis=0, keepdims=True)
    out_t = out_t * lax.rsqrt(ms_ + EPS)
    o_ref[0] = (out_t.T * ng_ref[...]).astype(bf16)


def _piece_select():
    sel = np.zeros((384, 4 * 128), np.float32)
    for pair in range(4):
        for j in range(3):
            sel[j * 128 + 8 + 2 * pair, pair * 128 + HEAD_DIM + j] = 1.0
            sel[j * 128 + 8 + 2 * pair + 1, pair * 128 + j] = 1.0
    return jnp.asarray(sel, bf16)


def _attn(qkv, cumc, norm_g):
    bsz, t, _ = qkv.shape
    tq = min(ATT_BLOCK, t)
    return pl.pallas_call(
        functools.partial(_attn_kernel, tq=tq, t=t),
        grid=(bsz, t // tq),
        in_specs=[pl.BlockSpec((1, tq, D_ATT), lambda b, i: (b, i, 0)),
                  pl.BlockSpec((1, t, D_ATT), lambda b, i: (b, 0, 1)),
                  pl.BlockSpec((1, t, D_ATT), lambda b, i: (b, 0, 2)),
                  pl.BlockSpec((1, t, 128), lambda b, i: (b, 0, 0)),
                  pl.BlockSpec((384, 512), lambda b, i: (0, 0)),
                  pl.BlockSpec((1, D_ATT), lambda b, i: (0, 0))],
        out_specs=pl.BlockSpec((1, tq, D_ATT), lambda b, i: (b, i, 0)),
        out_shape=jax.ShapeDtypeStruct((bsz, t, D_ATT), bf16),
        scratch_shapes=[pltpu.VMEM((D_ATT // HEAD_DIM, t, 128), bf16),
                        pltpu.VMEM((t // tq, (D_ATT // HEAD_DIM) * V_ROWS, tq), bf16),
                        pltpu.VMEM((D_ATT // HEAD_DIM, V_ROWS, tq), f32),
                        pltpu.VMEM((D_ATT // HEAD_DIM, tq, tq), f32),
                        pltpu.VMEM((D_ATT // HEAD_DIM, tq, tq), f32)],
        compiler_params=pltpu.CompilerParams(dimension_semantics=("parallel", "arbitrary"),
                                             vmem_limit_bytes=VMEM_LIMIT),
        name="attn",
    )(qkv, qkv, qkv, cumc, _piece_select(), norm_g)


def _layer_norm(y, g, b):
    mu = jnp.mean(y, axis=-1, keepdims=True)
    yc = y - mu
    var = jnp.mean(yc * yc, axis=-1, keepdims=True)
    return yc * lax.rsqrt(var + EPS) * g + b


def _outproj_kernel(ys_ref, ya_ref, x_ref, gt_ref, sc_ref, sh_ref, wo_ref, lng_ref, lnb_ref, wrh_ref, wrl_ref, rb_ref,
                    x1_ref, rt_ref, dest_ref, cnt_ref, tbl_ref, xrows_hbm,
                    carry, galloc, tbl, u2t, didx_v, didx_s, cnt_v, cnt_s, zeros, idx_sem, sc_sem, z_sem,
                    *, tm, blk, n_blk):
    i = pl.program_id(0)
    last = pl.num_programs(0) - 1
    cur = i % 2
    prev = 1 - cur

    def idx_ready(slot):
        return pltpu.make_async_copy(didx_v.at[slot], didx_s.at[slot], idx_sem.at[slot])

    def dispatched(slot):
        return pltpu.make_async_copy(u2t.at[slot], u2t.at[slot], sc_sem.at[slot])

    def dispatch_copy(slot, r, k):
        return pltpu.make_async_copy(u2t.at[slot, pl.ds(r * PACK_TILE, PACK_TILE)],
                                     xrows_hbm.at[pl.ds(didx_s[slot, k, r] * PACK_TILE, PACK_TILE)], sc_sem.at[slot])

    @pl.when(i == 0)
    def _init():
        carry[...] = jnp.zeros_like(carry)
        galloc[...] = jnp.zeros_like(galloc)
        tbl[...] = jnp.zeros_like(tbl)

    @pl.when(i >= 2)
    def _reuse():
        dispatched(cur).wait()
        dispatched(cur).wait()

    def step(dispatch_prev):
        n_parts = OUTPROJ_PARTS
        n_slices = 3 * n_parts + 2

        def dispatch_slice(c):
            if dispatch_prev:
                for r_ in range(c * tm // n_slices, (c + 1) * tm // n_slices):
                    dispatch_copy(prev, r_, 0).start(priority=0)
                    dispatch_copy(prev, r_, 1).start(priority=1)

        if dispatch_prev:
            idx_ready(prev).wait()
        part = tm // n_parts
        hs = []
        for a in range(n_parts):
            rs = slice(a * part, (a + 1) * part)
            dispatch_slice(a)
            hs.append(_dot(ys_ref[rs, :], wo_ref[0:D_SSM, :]) + _dot(ya_ref[rs, :], wo_ref[D_SSM:D, :]))
        logit_parts = []
        for a in range(n_parts):
            rs = slice(a * part, (a + 1) * part)
            dispatch_slice(n_parts + 2 * a)
            x1 = _layer_norm(ALPHA * x_ref[rs, :] + (1.0 + gt_ref[0, 0]) * hs[a], lng_ref[...], lnb_ref[...])
            x1_ref[rs, :] = x1
            u2 = x1 * (1.0 + sc_ref[0, 0]) + sh_ref[0, 0]
            uh = u2.astype(bf16)
            _store_packed_rows(u2t.at[cur, pl.ds(a * part * PACK_TILE, part * PACK_TILE)], _pack_rows(uh))
            dispatch_slice(n_parts + 2 * a + 1)
            ul = (u2 - uh.astype(f32)).astype(bf16)
            logit_parts.append((_dot(uh, wrh_ref[...]) + _dot(ul, wrh_ref[...])) + _dot(uh, wrl_ref[...]))
        logits = jnp.concatenate(logit_parts, axis=0) + rb_ref[...]
        lane = lax.broadcasted_iota(i32, (tm, 128), 1).astype(f32)
        big = jnp.float32(1e9)

        def first_max(vals):
            m = jnp.max(vals, axis=-1, keepdims=True)
            return m, jnp.min(jnp.where(vals == m, lane, big), axis=-1, keepdims=True)

        gl = jnp.where(lane < N_GROUPS_R, logits, NEG)
        gmax, gidx = first_max(gl)
        g_p = 1.0 / jnp.sum(jnp.exp(gl - gmax), axis=-1, keepdims=True)
        lo = N_GROUPS_R + EXPERTS_PER_GROUP * gidx
        el = jnp.where((lane >= lo) & (lane < lo + EXPERTS_PER_GROUP), logits, NEG)
        m1, i1 = first_max(el)
        el2 = jnp.where(lane == i1, NEG, el)
        m2, i2 = first_max(el2)
        r = jnp.exp(m2 - m1)
        w1 = g_p / (1.0 + r)
        w2 = g_p * r / (1.0 + r)
        dispatch_slice(3 * n_parts)

        oh1 = lane == i1
        oh2 = lane == i2
        oh = jnp.where(oh1 | oh2, 1.0, 0.0)
        r_i = lax.broadcasted_iota(i32, (tm, tm), 0)
        c_i = lax.broadcasted_iota(i32, (tm, tm), 1)
        lower = jnp.where(r_i > c_i, 1.0, 0.0).astype(bf16)
        c_old = carry[...]
        prefix = _dot(lower, oh.astype(bf16)) + c_old
        rank1 = jnp.sum(jnp.where(oh1, prefix, 0.0), axis=-1, keepdims=True)
        rank2 = jnp.sum(jnp.where(oh2, prefix, 0.0), axis=-1, keepdims=True)
        c_new = c_old + jnp.sum(oh, axis=0, keepdims=True)
        carry[...] = c_new
        dispatch_slice(3 * n_parts + 1)

        nb_old = jnp.floor((c_old + (blk - 1)) * (1.0 / blk))
        nb_new = jnp.floor((c_new + (blk - 1)) * (1.0 / blk))
        fresh = nb_new - nb_old
        sq_r = lax.broadcasted_iota(i32, (128, 128), 0)
        sq_c = lax.broadcasted_iota(i32, (128, 128), 1)
        before = jnp.where(sq_r < sq_c, 1.0, 0.0).astype(bf16)
        base = galloc[...] + _dot(jnp.broadcast_to(fresh, (8, 128)).astype(bf16), before)[0:1, :]
        galloc[...] = galloc[...] + jnp.sum(fresh, axis=-1, keepdims=True)
        ordinal = sq_r.astype(f32)
        tbl[...] = jnp.where((ordinal >= nb_old) & (ordinal < nb_new), base + (ordinal - nb_old), tbl[...])
        tbl_ref[...] = tbl[...]
        cnt_ref[...] = jnp.concatenate([c_new, galloc[...], jnp.zeros((6, 128), f32)], axis=0)

        rt = jnp.where(lane == 4, w1, jnp.where(lane == 5, w2, 0.0))
        rt_ref[...] = rt

        ids_bf = tbl[...].astype(bf16)

        def row_of(onehot, rank):
            nth = jnp.floor(rank * (1.0 / blk))
            ids = _dot(jnp.where(lane == nth, 1.0, 0.0).astype(bf16), ids_bf)
            return jnp.sum(jnp.where(onehot, ids, 0.0), axis=-1, keepdims=True) * blk + (rank - nth * blk)

        d1 = row_of(oh1, rank1)
        d2 = row_of(oh2, rank2)
        dd = jnp.where(lane == 0, d1, jnp.where(lane == 1, d2, 0.0)).T[0:8, :].astype(i32)
        dest_ref[...] = dd
        didx_v[cur] = dd
        idx_ready(cur).start()

    pl.when(i == 0)(lambda: step(False))
    pl.when(i > 0)(lambda: step(True))

    @pl.when(i == last)
    def _epilogue():
        idx_ready(cur).wait()

        def issue(r_, c):
            dispatch_copy(cur, r_, 0).start()
            dispatch_copy(cur, r_, 1).start()
            return c
        lax.fori_loop(0, tm, issue, 0, unroll=8)

        cnt = carry[...]
        n_blocks = jnp.floor((cnt + (blk - 1)) * (1.0 / blk))
        ordinal = lax.broadcasted_iota(i32, (128, 128), 0).astype(f32)
        last_id = jnp.sum(jnp.where(ordinal == n_blocks - 1.0, tbl[...], 0.0), axis=0, keepdims=True)
        used = cnt - (n_blocks - 1.0) * blk
        cnt_v[...] = jnp.concatenate([last_id * blk + used, blk - used, galloc[...], jnp.zeros((5, 128), f32)],
                                     axis=0).astype(i32)
        to_smem = pltpu.make_async_copy(cnt_v, cnt_s, z_sem)
        to_smem.start()
        to_smem.wait()
        zeros[...] = jnp.zeros_like(zeros)
        sizes = [1 << b for b in reversed(range(blk.bit_length() - 1))]

        def for_each_fill(fn):
            def tail(e, c):
                pad = cnt_s[1, N_GROUPS_R + e]
                off = cnt_s[0, N_GROUPS_R + e]
                for sz in sizes:
                    @pl.when((pad & sz) != 0)
                    def _(off=off, sz=sz):
                        fn(pltpu.make_async_copy(zeros.at[pl.ds(0, sz * PACK_TILE)],
                                                 xrows_hbm.at[pl.ds(off * PACK_TILE, sz * PACK_TILE)], z_sem))
                    off = off + (pad & sz)
                return c
            lax.fori_loop(0, N_EXPERTS, tail, 0)

            def whole(b, c):
                fn(pltpu.make_async_copy(zeros, xrows_hbm.at[pl.ds(b * blk * PACK_TILE, blk * PACK_TILE)], z_sem))
                return c
            lax.fori_loop(cnt_s[2, 0], n_blk, whole, 0)

        for_each_fill(lambda copy: copy.start())
        for_each_fill(lambda copy: copy.wait())

        @pl.when(i > 0)
        def _prev_done():
            dispatched(prev).wait()
            dispatched(prev).wait()
        dispatched(cur).wait()
        dispatched(cur).wait()


def _outproj(y_ssm, y_att, x, mod4, w_out, ln_g, ln_b, wr_hi, wr_lo, rb, t, blk, n_blk):
    n = x.shape[0]
    tm = min(OUTPROJ_ROWS, t)
    nt = t // tm
    assert n // blk + 1 <= 128 and n_blk <= 256, "block-id table: 128 blocks per expert, ids exact in bf16"
    vec = lambda k: pl.BlockSpec((1, 1, 1, D), lambda i, k=k: (k, i // nt, 0, 0))
    full = lambda shape: pl.BlockSpec(shape, lambda i: (0,) * len(shape))
    rows = lambda w: pl.BlockSpec((tm, w), lambda i: (i, 0))
    return pl.pallas_call(
        functools.partial(_outproj_kernel, tm=tm, blk=blk, n_blk=n_blk),
        grid=(n // tm,),
        in_specs=[rows(D_SSM), rows(D_ATT), rows(D), vec(2), vec(4), vec(3),
                  full((D, D)), full((1, D)), full((1, D)), full((D, 128)), full((D, 128)), full((1, 128))],
        out_specs=[rows(D), rows(128), pl.BlockSpec((8, tm), lambda i: (0, i)), full((8, 128)), full((128, 128)),
                   pl.BlockSpec(memory_space=pl.ANY)],
        out_shape=[jax.ShapeDtypeStruct((n, D), f32), jax.ShapeDtypeStruct((n, 128), f32),
                   jax.ShapeDtypeStruct((8, n), i32), jax.ShapeDtypeStruct((8, 128), f32),
                   jax.ShapeDtypeStruct((128, 128), f32),
                   jax.ShapeDtypeStruct((n_blk * blk * PACK_TILE, 128), u32)],
        scratch_shapes=[pltpu.VMEM((1, 128), f32), pltpu.VMEM((1, 128), f32), pltpu.VMEM((128, 128), f32),
                        pltpu.VMEM((2, tm * PACK_TILE, 128), u32),
                        pltpu.VMEM((2, 8, tm), i32), pltpu.SMEM((2, 8, tm), i32),
                        pltpu.VMEM((8, 128), i32), pltpu.SMEM((8, 128), i32),
                        pltpu.VMEM((blk * PACK_TILE, 128), u32),
                        pltpu.SemaphoreType.DMA((2,)), pltpu.SemaphoreType.DMA((2,)), pltpu.SemaphoreType.DMA(())],
        compiler_params=pltpu.CompilerParams(dimension_semantics=("arbitrary",), vmem_limit_bytes=VMEM_LIMIT),
        name="outproj",
    )(y_ssm, y_att, x, mod4, mod4, mod4, w_out, ln_g, ln_b, wr_hi, wr_lo, rb)


def _tile_copy(src_hbm, row, buf, slot, sem):
    return pltpu.make_async_copy(src_hbm.at[row], buf.at[pl.ds(slot * ROW_TILE, ROW_TILE)], sem)


def _moe_kernel(be_ref, br_ref, nu_ref, nxt_ref, par_ref, nv_ref, x_hbm, wg_hbm, wu_hbm, wd_hbm, y_ref,
                wfg, wfu, wfd, wgb, wub, wdb, wsem, xbuf, xsem, *, blk):
    i = pl.program_id(0)
    used = i < nu_ref[0]

    def fetch(e, slot):
        return [pltpu.make_async_copy(src.at[e], dst.at[slot], wsem.at[slot])
                for src, dst in ((wg_hbm, wfg), (wu_hbm, wfu), (wd_hbm, wfd))]

    last = nu_ref[0] - 1

    def rows_copy(b, slot):
        start = pl.multiple_of(br_ref[b] * (blk * PACK_TILE), blk * PACK_TILE)
        return pltpu.make_async_copy(x_hbm.at[pl.ds(start, blk * PACK_TILE)], xbuf.at[slot], xsem.at[slot])

    @pl.when(i == 0)
    def _first():
        for c in fetch(be_ref[0], par_ref[0]):
            c.start()
        rows_copy(0, 0).start()
        rows_copy(jnp.minimum(1, last), 1).start()

    @pl.when(used)
    def _ring():
        rows_copy(i, i % 3).wait()
        rows_copy(jnp.minimum(i + 2, last), (i + 2) % 3).start()

    @pl.when(jnp.logical_and(used, jnp.logical_or(i == 0, be_ref[i] != be_ref[jnp.maximum(i - 1, 0)])))
    def _switch():
        slot = par_ref[i]
        for c in fetch(be_ref[i], slot):
            c.wait()

        @pl.when(nxt_ref[i] >= 0)
        def _next():
            for c in fetch(nxt_ref[i], 1 - slot):
                c.start(priority=1)
        wgb[...] = wfg[slot].astype(bf16)
        wub[...] = wfu[slot].astype(bf16)
        wdb[...] = wfd[slot].astype(bf16)

    @pl.when(jnp.logical_not(used))
    def _spare():
        y_ref[...] = jnp.zeros_like(y_ref)

    def mlp(rows):
        xa, xb = _load_packed_rows(xbuf.at[i % 3, pl.ds(0, rows * PACK_TILE)], rows)
        half = D // 2
        gate = _dot(xa, wgb[0:half, :]) + _dot(xb, wgb[half:D, :])
        up = _dot(xa, wub[0:half, :]) + _dot(xb, wub[half:D, :])
        hid = (_silu(gate) * up).astype(bf16)
        _store_row_tiles(y_ref.at[pl.ds(0, rows * ROW_TILE)], _dot(hid, wdb[...]))

    sparse = nv_ref[i] <= blk // 2

    @pl.when(jnp.logical_and(used, jnp.logical_not(sparse)))
    def _full():
        mlp(blk)

    @pl.when(jnp.logical_and(used, sparse))
    def _half():
        mlp(blk // 2)
        y_ref[pl.ds(blk // 2 * ROW_TILE, blk // 2 * ROW_TILE), :] = jnp.zeros((blk // 2 * ROW_TILE, 128), f32)

    @pl.when(i == last)
    def _drain():
        rows_copy(last, (i + 1) % 3).wait()
        rows_copy(last, (i + 2) % 3).wait()


def _moe(block_e, block_row, n_used, next_e, parity, n_valid, x_rows, w_gate, w_up, w_down, blk):
    nblk = block_e.shape[0]
    rspec = pl.BlockSpec((blk * ROW_TILE, 128), lambda i, be, br, *_: (br[i], 0))
    anyspec = pl.BlockSpec(memory_space=pl.ANY)
    grid_spec = pltpu.PrefetchScalarGridSpec(
        num_scalar_prefetch=6,
        grid=(nblk,),
        in_specs=[anyspec, anyspec, anyspec, anyspec],
        out_specs=rspec,
        scratch_shapes=[pltpu.VMEM((2, D, D_EXPERT), f32), pltpu.VMEM((2, D, D_EXPERT), f32),
                        pltpu.VMEM((2, D_EXPERT, D), f32),
                        pltpu.VMEM((D, D_EXPERT), bf16), pltpu.VMEM((D, D_EXPERT), bf16),
                        pltpu.VMEM((D_EXPERT, D), bf16),
                        pltpu.SemaphoreType.DMA((2,)),
                        pltpu.VMEM((3, blk * PACK_TILE, 128), u32), pltpu.SemaphoreType.DMA((3,))],
    )
    return pl.pallas_call(
        functools.partial(_moe_kernel, blk=blk),
        grid_spec=grid_spec,
        out_shape=jax.ShapeDtypeStruct((nblk * blk * ROW_TILE, 128), f32),
        compiler_params=pltpu.CompilerParams(dimension_semantics=("arbitrary",), vmem_limit_bytes=VMEM_LIMIT),
        name="moe",
    )(block_e, block_row, n_used, next_e, parity, n_valid, x_rows, w_gate, w_up, w_down)


def _combine_kernel(d1_ref, d2_ref, y_hbm, x1_ref, rt_ref, gt_ref, lng_ref, lnb_ref, o_ref, buf, sem, *, tm):
    i = pl.program_id(0)
    last = pl.num_programs(0) - 1

    @pl.when(i == 0)
    def _prologue():
        for s in range(2):
            def issue(r, c, s=s):
                base = jnp.minimum(s, last) * tm
                _tile_copy(y_hbm, d1_ref[base + r], buf.at[s, 0], r, sem.at[s]).start()
                _tile_copy(y_hbm, d2_ref[base + r], buf.at[s, 1], r, sem.at[s]).start()
                return c
            lax.fori_loop(0, tm, issue, 0)

    def rows_ready(slot):
        return pltpu.make_async_copy(buf.at[slot], buf.at[slot], sem.at[slot])

    slot = i % 3
    nxt = (i + 2) % 3
    base = jnp.minimum(i + 2, last) * tm
    rows_ready(slot).wait()
    for r in range(tm):
        _tile_copy(y_hbm, d1_ref[base + r], buf.at[nxt, 0], r, sem.at[nxt]).start(priority=0)
        _tile_copy(y_hbm, d2_ref[base + r], buf.at[nxt, 1], r, sem.at[nxt]).start(priority=1)
    rt = rt_ref[...]
    moe = rt[:, 4:5] * _load_row_tiles(buf.at[slot, 0], tm) + rt[:, 5:6] * _load_row_tiles(buf.at[slot, 1], tm)
    y = ALPHA * x1_ref[...] + (1.0 + gt_ref[0, 0]) * moe
    o_ref[...] = _layer_norm(y, lng_ref[...], lnb_ref[...])

    @pl.when(i == last)
    def _drain():
        rows_ready((i + 1) % 3).wait()
        rows_ready((i + 2) % 3).wait()


def _combine(dest1, dest2, y_rows, x1, rt, mod4, ln_g, ln_b, t):
    n = x1.shape[0]
    tm = min(COMBINE_ROWS, t)
    nt = t // tm
    full = lambda shape: pl.BlockSpec(shape, lambda i, d1, d2: (0,) * len(shape))
    rows = lambda w: pl.BlockSpec((tm, w), lambda i, d1, d2: (i, 0))
    grid_spec = pltpu.PrefetchScalarGridSpec(
        num_scalar_prefetch=2,
        grid=(n // tm,),
        in_specs=[pl.BlockSpec(memory_space=pl.ANY), rows(D), rows(128),
                  pl.BlockSpec((1, 1, 1, D), lambda i, d1, d2: (5, i // nt, 0, 0)),
                  full((1, D)), full((1, D))],
        out_specs=rows(D),
        scratch_shapes=[pltpu.VMEM((3, 2, tm * ROW_TILE, 128), f32), pltpu.SemaphoreType.DMA((3,))],
    )
    return pl.pallas_call(
        functools.partial(_combine_kernel, tm=tm),
        grid_spec=grid_spec,
        out_shape=jax.ShapeDtypeStruct((n, D), f32),
        compiler_params=pltpu.CompilerParams(dimension_semantics=("arbitrary",), vmem_limit_bytes=VMEM_LIMIT),
        name="combine",
    )(dest1, dest2, y_rows, x1, rt, mod4, ln_g, ln_b)


def _hi_lo(w):
    hi = w.astype(bf16)
    return hi, (w - hi.astype(f32)).astype(bf16)


def _pad_lanes(v, offset, width):
    return jnp.zeros((width,), f32).at[offset:offset + v.shape[0]].set(v)


def _layer(x, c, ada_w, ada_b, w_in, conv_w, conv_b, dt_bias, a_log, d_skip, ssm_norm_g, fg_bias, att_norm_g,
           w_out, ln1_g, ln1_b, router_g_w, router_g_b, router_e_w, router_e_b, w_gate, w_up, w_down, ln2_g, ln2_b):
    bsz, t, _ = x.shape
    n = bsz * t

    mod4 = _ada(c, ada_w, ada_b).reshape(6, bsz, 1, D)

    w_main, w_small = _pack_w_in(w_in)
    zx, bc, qkv, dtf = _inproj(x, mod4, w_main, w_small, conv_w, conv_b.reshape(1, -1))

    pc = jnp.stack([_pad_lanes(dt_bias, 0, 128), _pad_lanes(a_log, 0, 128), _pad_lanes(fg_bias, 8, 128)]
                   + [jnp.zeros((128,), f32)] * 5)
    rep = lambda v: jnp.repeat(v, HEAD_DIM)
    pe = jnp.stack([rep(dt_bias), rep(a_log), rep(d_skip), ssm_norm_g] + [jnp.zeros((D_SSM,), f32)] * 4)
    y_ssm, cumc = _ssd(zx, bc, dtf, pc, pe)

    y_att = _attn(qkv, cumc, att_norm_g.reshape(1, -1))

    wr = jnp.concatenate([router_g_w, router_e_w, jnp.zeros((D, 128 - N_GROUPS_R - N_EXPERTS), f32)], axis=1)
    wr_hi, wr_lo = _hi_lo(wr)
    rb = jnp.concatenate([router_g_b, router_e_b, jnp.zeros((128 - N_GROUPS_R - N_EXPERTS,), f32)]).reshape(1, 128)
    blk = MOE_ROWS
    nblk = (2 * n) // blk + N_EXPERTS
    x1, rt, dest, cnt, tbl, x_rows = _outproj(y_ssm.reshape(n, D_SSM), y_att.reshape(n, D_ATT), x.reshape(n, D), mod4,
                                              w_out.astype(bf16), ln1_g.reshape(1, D), ln1_b.reshape(1, D),
                                              wr_hi, wr_lo, rb, t, blk, nblk)

    counts = cnt[0, N_GROUPS_R:N_GROUPS_R + N_EXPERTS].astype(i32)
    nb = (counts + blk - 1) // blk
    nb_end = jnp.cumsum(nb)
    n_used = nb_end[N_EXPERTS - 1:]
    every = jnp.arange(nblk, dtype=i32)
    step = jnp.minimum(every, n_used[0] - 1)
    owner = (nb_end[None, :] <= step[:, None]).astype(i32)
    block_e = jnp.sum(owner, axis=1)
    first = jnp.sum(owner * nb[None, :], axis=1)
    lanes = jnp.arange(128, dtype=i32)[None, :]
    nth_row = jnp.dot((lanes == (step - first)[:, None]).astype(f32), tbl, precision=lax.Precision.HIGHEST)
    block_id = jnp.sum(jnp.where(lanes == (block_e + N_GROUPS_R)[:, None], nth_row, 0.0), axis=1).astype(i32)
    block_row = jnp.where(every < n_used[0], block_id, every)
    run_end = first + jnp.sum((jnp.arange(N_EXPERTS)[None, :] == block_e[:, None]) * nb[None, :], axis=1)
    next_e = jnp.where(run_end < n_used[0], jnp.sum((nb_end[None, :] <= run_end[:, None]).astype(i32), axis=1), -1)
    parity = jnp.sum(((jnp.arange(N_EXPERTS)[None, :] < block_e[:, None]) & (nb[None, :] > 0)).astype(i32), axis=1) % 2

    own = (jnp.arange(N_EXPERTS)[None, :] == block_e[:, None]).astype(i32)
    n_valid = jnp.clip(jnp.sum(own * counts[None, :], axis=1) - (step - first) * blk, 0, blk)
    y_rows = _moe(block_e, block_row, n_used, next_e, parity, n_valid, x_rows,
                  w_gate, w_up, w_down, blk)
    out = _combine(dest[0], dest[1], y_rows.reshape(nblk * blk, ROW_TILE, 128), x1, rt, mod4,
                   ln2_g.reshape(1, D), ln2_b.reshape(1, D), t)
    return out.reshape(bsz, t, D)


def kernel(x, c, ada_w, ada_b, w_in, conv_w, conv_b, dt_bias, a_log, d_skip, ssm_norm_g, fg_bias, att_norm_g, w_out,
           ln1_g, ln1_b, router_g_w, router_g_b, router_e_w, router_e_b, w_gate, w_up, w_down, ln2_g, ln2_b):
    depth = ada_w.shape[0]
    for l in range(depth):
        x = _layer(x, c, ada_w[l], ada_b[l], w_in[l], conv_w[l], conv_b[l], dt_bias[l], a_log[l], d_skip[l],
                   ssm_norm_g[l], fg_bias[l], att_norm_g[l], w_out[l], ln1_g[l], ln1_b[l], router_g_w[l],
                   router_g_b[l], router_e_w[l], router_e_b[l], w_gate[l], w_up[l], w_down[l], ln2_g[l], ln2_b[l])
    return x
```
